```python
import jax, jax.numpy as jnp
from jax import lax
import numpy as np

D_MODEL = 1024
BATCH = 8
SEQ = 2048
DEPTH = 2

N_A = DEPTH // 2
N_B = DEPTH - N_A
CHUNK = 128
A_GROUPS = 8
A_GROUP_DIM = D_MODEL // A_GROUPS
N_HEADS = 16
HEAD_DIM = D_MODEL // N_HEADS
Q_BLOCK = 128
D_FF = 4 * D_MODEL
PLE_DIM = 256
EPS = 1e-6

kernel_name = "yoco_gmlp_stickbreaking_hybrid"


def rms_norm(x, g):
    xf = x.astype(jnp.float32)
    y = xf * lax.rsqrt(jnp.mean(xf * xf, axis=-1, keepdims=True) + EPS)
    return (y * g.astype(jnp.float32)).astype(x.dtype)


def sgu_mixer(h, w_in, g_v, w_s, b_s, w_out):
    bsz, seq, _ = h.shape
    z = jax.nn.gelu(h @ w_in)
    u, v = jnp.split(z, 2, axis=-1)
    v = rms_norm(v, g_v)
    v = v.reshape(bsz, seq // CHUNK, CHUNK, A_GROUPS, A_GROUP_DIM)
    causal = jnp.tril(jnp.ones((CHUNK, CHUNK), dtype=w_s.dtype))
    w = w_s * causal[None]
    mix = jnp.einsum('gts,bcsgd->bctgd', w, v) + jnp.transpose(b_s)[None, None, :, :, None]
    y = u * mix.reshape(bsz, seq, D_MODEL)
    return y @ w_out


def sqrelu_mlp(h, w_up, w_down):
    a = jax.nn.relu(h @ w_up)
    return (a * a) @ w_down


def shared_kv(x, ln_kv, w_kv, g_k):
    bsz, seq, _ = x.shape
    h = rms_norm(x, ln_kv)
    k, v = jnp.split(h @ w_kv, 2, axis=-1)
    k = rms_norm(k.reshape(bsz, seq, N_HEADS, HEAD_DIM), g_k)
    v = v.reshape(bsz, seq, N_HEADS, HEAD_DIM)
    return jnp.transpose(k, (0, 2, 1, 3)), jnp.transpose(v, (0, 2, 1, 3))


def stick_breaking(q, k, v):
    seq = q.shape[2]
    scale = HEAD_DIM ** -0.5
    outs = []
    for blk in range(seq // Q_BLOCK):
        t0 = blk * Q_BLOCK
        t1 = t0 + Q_BLOCK
        qb = q[:, :, t0:t1].astype(jnp.float32)
        kb = k[:, :, :t1].astype(jnp.float32)
        vb = v[:, :, :t1].astype(jnp.float32)
        z = jnp.einsum('bhqd,bhkd->bhqk', qb, kb) * scale
        q_idx = t0 + jnp.arange(Q_BLOCK)[:, None]
        k_idx = jnp.arange(t1)[None, :]
        causal = k_idx < q_idx
        log_1m_beta = jnp.where(causal, jax.nn.log_sigmoid(-z), 0.0)
        between = lax.cumsum(log_1m_beta, axis=3, reverse=True) - log_1m_beta
        a = jnp.where(causal, jnp.exp(jax.nn.log_sigmoid(z) + between), 0.0)
        o = jnp.einsum('bhqk,bhkd->bhqd', a, vb)
        outs.append(o.astype(v.dtype))
    return jnp.concatenate(outs, axis=2)


def stick_breaking_mixer(h, w_q, g_q, k, v, w_out):
    bsz, seq, _ = h.shape
    q = rms_norm((h @ w_q).reshape(bsz, seq, N_HEADS, HEAD_DIM), g_q)
    q = jnp.transpose(q, (0, 2, 1, 3))
    o = stick_breaking(q, k, v)
    o = jnp.transpose(o, (0, 2, 1, 3)).reshape(bsz, seq, D_MODEL)
    return o @ w_out


def _fwd_setup_inputs(seed: int = 0) -> dict:
    key = jax.random.key(seed)
    ks = jax.random.split(key, 32)

    def nrm(k, shape, scale):
        return jax.random.normal(k, shape, dtype=jnp.float32) * scale

    def gain(k, shape):
        return 1.0 + nrm(k, shape, 0.02)

    return {
        "x": nrm(ks[0], (BATCH, SEQ, D_MODEL), 1.0),
        "p": nrm(ks[1], (DEPTH, BATCH, SEQ, PLE_DIM), 1.0),
        "ln_mix_a": gain(ks[2], (N_A, D_MODEL)),
        "w_in_a": nrm(ks[3], (N_A, D_MODEL, 2 * D_MODEL), D_MODEL ** -0.5),
        "g_v_a": gain(ks[4], (N_A, D_MODEL)),
        "w_spatial": nrm(ks[5], (N_A, A_GROUPS, CHUNK, CHUNK), CHUNK ** -0.5),
        "b_spatial": 1.0 + nrm(ks[6], (N_A, A_GROUPS, CHUNK), 0.02),
        "w_out_a": nrm(ks[7], (N_A, D_MODEL, D_MODEL), D_MODEL ** -0.5),
        "ln_kv": gain(ks[8], (D_MODEL,)),
        "w_kv": nrm(ks[9], (D_MODEL, 2 * D_MODEL), D_MODEL ** -0.5),
        "g_k": gain(ks[10], (HEAD_DIM,)),
        "ln_mix_b": gain(ks[11], (N_B, D_MODEL)),
        "w_q": nrm(ks[12], (N_B, D_MODEL, D_MODEL), D_MODEL ** -0.5),
        "g_q": gain(ks[13], (N_B, HEAD_DIM)),
        "w_out_b": nrm(ks[14], (N_B, D_MODEL, D_MODEL), D_MODEL ** -0.5),
        "ln_mlp": gain(ks[15], (DEPTH, D_MODEL)),
        "w_up": nrm(ks[16], (DEPTH, D_MODEL, D_FF), D_MODEL ** -0.5),
        "w_down": nrm(ks[17], (DEPTH, D_FF, D_MODEL), D_FF ** -0.5),
        "ln_ple": gain(ks[18], (DEPTH, D_MODEL)),
        "w_ple_gate": nrm(ks[19], (DEPTH, D_MODEL, D_MODEL), D_MODEL ** -0.5),
        "w_ple_proj": nrm(ks[20], (DEPTH, PLE_DIM, D_MODEL), PLE_DIM ** -0.5),
    }


def _fwd_reference(x, p, ln_mix_a, w_in_a, g_v_a, w_spatial, b_spatial, w_out_a,
              ln_kv, w_kv, g_k, ln_mix_b, w_q, g_q, w_out_b,
              ln_mlp, w_up, w_down, ln_ple, w_ple_gate, w_ple_proj):
    k_shared = None
    v_shared = None
    for i in range(DEPTH):
        if i < N_A:
            h = rms_norm(x, ln_mix_a[i])
            x = x + sgu_mixer(h, w_in_a[i], g_v_a[i], w_spatial[i], b_spatial[i], w_out_a[i])
        else:
            j = i - N_A
            h = rms_norm(x, ln_mix_b[j])
            x = x + stick_breaking_mixer(h, w_q[j], g_q[j], k_shared, v_shared, w_out_b[j])
        x = x + sqrelu_mlp(rms_norm(x, ln_mlp[i]), w_up[i], w_down[i])
        gate = jax.nn.sigmoid(rms_norm(x, ln_ple[i]) @ w_ple_gate[i])
        x = x + (p[i] @ w_ple_proj[i]) * gate
        if i == N_A - 1:
            k_shared, v_shared = shared_kv(x, ln_kv, w_kv, g_k)
    return x


import jax as _jax
import jax.numpy as _jnp

TWIN_FORMAT = 'train_step'
FWD_PARAMS = ['x', 'p', 'ln_mix_a', 'w_in_a', 'g_v_a', 'w_spatial', 'b_spatial', 'w_out_a', 'ln_kv', 'w_kv', 'g_k', 'ln_mix_b', 'w_q', 'g_q', 'w_out_b', 'ln_mlp', 'w_up', 'w_down', 'ln_ple', 'w_ple_gate', 'w_ple_proj']
TWIN_WEIGHTS = ['ln_mix_a', 'w_in_a', 'g_v_a', 'w_spatial', 'b_spatial', 'w_out_a', 'ln_kv', 'w_kv', 'g_k', 'ln_mix_b', 'w_q', 'g_q', 'w_out_b', 'ln_mlp', 'w_up', 'w_down', 'ln_ple', 'w_ple_gate', 'w_ple_proj']
TWIN_DIFF_INPUT = 'x'
TWIN_INPUTS = ['x', 'p', 'ln_mix_a', 'w_in_a', 'g_v_a', 'w_spatial', 'b_spatial', 'w_out_a', 'ln_kv', 'w_kv', 'g_k', 'ln_mix_b', 'w_q', 'g_q', 'w_out_b', 'ln_mlp', 'w_up', 'w_down', 'ln_ple', 'w_ple_gate', 'w_ple_proj', 'loss_target', 'm_ln_mix_a', 'm_w_in_a', 'm_g_v_a', 'm_w_spatial', 'm_b_spatial', 'm_w_out_a', 'm_ln_kv', 'm_w_kv', 'm_g_k', 'm_ln_mix_b', 'm_w_q', 'm_g_q', 'm_w_out_b', 'm_ln_mlp', 'm_w_up', 'm_w_down', 'm_ln_ple', 'm_w_ple_gate', 'm_w_ple_proj', 'v_ln_mix_a', 'v_w_in_a', 'v_g_v_a', 'v_w_spatial', 'v_b_spatial', 'v_w_out_a', 'v_ln_kv', 'v_w_kv', 'v_g_k', 'v_ln_mix_b', 'v_w_q', 'v_g_q', 'v_w_out_b', 'v_ln_mlp', 'v_w_up', 'v_w_down', 'v_ln_ple', 'v_w_ple_gate', 'v_w_ple_proj']
TWIN_OUTPUTS = ['loss', 'grad_x', 'grad_ln_mix_a', 'grad_w_in_a', 'grad_g_v_a', 'grad_w_spatial', 'grad_b_spatial', 'grad_w_out_a', 'grad_ln_kv', 'grad_w_kv', 'grad_g_k', 'grad_ln_mix_b', 'grad_w_q', 'grad_g_q', 'grad_w_out_b', 'grad_ln_mlp', 'grad_w_up', 'grad_w_down', 'grad_ln_ple', 'grad_w_ple_gate', 'grad_w_ple_proj', 'delta_ln_mix_a', 'delta_w_in_a', 'delta_g_v_a', 'delta_w_spatial', 'delta_b_spatial', 'delta_w_out_a', 'delta_ln_kv', 'delta_w_kv', 'delta_g_k', 'delta_ln_mix_b', 'delta_w_q', 'delta_g_q', 'delta_w_out_b', 'delta_ln_mlp', 'delta_w_up', 'delta_w_down', 'delta_ln_ple', 'delta_w_ple_gate', 'delta_w_ple_proj', 'new_m_ln_mix_a', 'new_m_w_in_a', 'new_m_g_v_a', 'new_m_w_spatial', 'new_m_b_spatial', 'new_m_w_out_a', 'new_m_ln_kv', 'new_m_w_kv', 'new_m_g_k', 'new_m_ln_mix_b', 'new_m_w_q', 'new_m_g_q', 'new_m_w_out_b', 'new_m_ln_mlp', 'new_m_w_up', 'new_m_w_down', 'new_m_ln_ple', 'new_m_w_ple_gate', 'new_m_w_ple_proj', 'new_v_ln_mix_a', 'new_v_w_in_a', 'new_v_g_v_a', 'new_v_w_spatial', 'new_v_b_spatial', 'new_v_w_out_a', 'new_v_ln_kv', 'new_v_w_kv', 'new_v_g_k', 'new_v_ln_mix_b', 'new_v_w_q', 'new_v_g_q', 'new_v_w_out_b', 'new_v_ln_mlp', 'new_v_w_up', 'new_v_w_down', 'new_v_ln_ple', 'new_v_w_ple_gate', 'new_v_w_ple_proj']
TWIN_LEAF_KINDS = {'loss': 'loss', 'grad_x': 'grad_x', 'grad_ln_mix_a': 'grad_w', 'grad_w_in_a': 'grad_w', 'grad_g_v_a': 'grad_w', 'grad_w_spatial': 'grad_w', 'grad_b_spatial': 'grad_w', 'grad_w_out_a': 'grad_w', 'grad_ln_kv': 'grad_w', 'grad_w_kv': 'grad_w', 'grad_g_k': 'grad_w', 'grad_ln_mix_b': 'grad_w', 'grad_w_q': 'grad_w', 'grad_g_q': 'grad_w', 'grad_w_out_b': 'grad_w', 'grad_ln_mlp': 'grad_w', 'grad_w_up': 'grad_w', 'grad_w_down': 'grad_w', 'grad_ln_ple': 'grad_w', 'grad_w_ple_gate': 'grad_w', 'grad_w_ple_proj': 'grad_w', 'delta_ln_mix_a': 'delta_w', 'delta_w_in_a': 'delta_w', 'delta_g_v_a': 'delta_w', 'delta_w_spatial': 'delta_w', 'delta_b_spatial': 'delta_w', 'delta_w_out_a': 'delta_w', 'delta_ln_kv': 'delta_w', 'delta_w_kv': 'delta_w', 'delta_g_k': 'delta_w', 'delta_ln_mix_b': 'delta_w', 'delta_w_q': 'delta_w', 'delta_g_q': 'delta_w', 'delta_w_out_b': 'delta_w', 'delta_ln_mlp': 'delta_w', 'delta_w_up': 'delta_w', 'delta_w_down': 'delta_w', 'delta_ln_ple': 'delta_w', 'delta_w_ple_gate': 'delta_w', 'delta_w_ple_proj': 'delta_w', 'new_m_ln_mix_a': 'new_m', 'new_m_w_in_a': 'new_m', 'new_m_g_v_a': 'new_m', 'new_m_w_spatial': 'new_m', 'new_m_b_spatial': 'new_m', 'new_m_w_out_a': 'new_m', 'new_m_ln_kv': 'new_m', 'new_m_w_kv': 'new_m', 'new_m_g_k': 'new_m', 'new_m_ln_mix_b': 'new_m', 'new_m_w_q': 'new_m', 'new_m_g_q': 'new_m', 'new_m_w_out_b': 'new_m', 'new_m_ln_mlp': 'new_m', 'new_m_w_up': 'new_m', 'new_m_w_down': 'new_m', 'new_m_ln_ple': 'new_m', 'new_m_w_ple_gate': 'new_m', 'new_m_w_ple_proj': 'new_m', 'new_v_ln_mix_a': 'new_v', 'new_v_w_in_a': 'new_v', 'new_v_g_v_a': 'new_v', 'new_v_w_spatial': 'new_v', 'new_v_b_spatial': 'new_v', 'new_v_w_out_a': 'new_v', 'new_v_ln_kv': 'new_v', 'new_v_w_kv': 'new_v', 'new_v_g_k': 'new_v', 'new_v_ln_mix_b': 'new_v', 'new_v_w_q': 'new_v', 'new_v_g_q': 'new_v', 'new_v_w_out_b': 'new_v', 'new_v_ln_mlp': 'new_v', 'new_v_w_up': 'new_v', 'new_v_w_down': 'new_v', 'new_v_ln_ple': 'new_v', 'new_v_w_ple_gate': 'new_v', 'new_v_w_ple_proj': 'new_v'}


def _forward(args):
    return _fwd_reference(*[args[k] for k in FWD_PARAMS])


def _output_shape():
    out = _jax.eval_shape(lambda: _forward(_fwd_setup_inputs(0)))
    return out.shape, out.dtype

N_MICROBATCH = 1
ADAM_LR = 0.001
ADAM_B1 = 0.9
ADAM_B2 = 0.999
ADAM_EPS = 1e-08
ADAM_WD = 0.01
ADAM_STEP = 10
PER_EXAMPLE_BATCH_AXIS = {'x': 0, 'p': 1, 'loss_target': 0}
SHARED_INPUTS = []
_WEIGHT_DTYPES = {'ln_mix_a': _jnp.float32, 'w_in_a': _jnp.float32, 'g_v_a': _jnp.float32, 'w_spatial': _jnp.float32, 'b_spatial': _jnp.float32, 'w_out_a': _jnp.float32, 'ln_kv': _jnp.float32, 'w_kv': _jnp.float32, 'g_k': _jnp.float32, 'ln_mix_b': _jnp.float32, 'w_q': _jnp.float32, 'g_q': _jnp.float32, 'w_out_b': _jnp.float32, 'ln_mlp': _jnp.float32, 'w_up': _jnp.float32, 'w_down': _jnp.float32, 'ln_ple': _jnp.float32, 'w_ple_gate': _jnp.float32, 'w_ple_proj': _jnp.float32}
MOMENT_SCALE = {'ln_mix_a': 1.137105e+01, 'w_in_a': 6.683692e-01, 'g_v_a': 3.203462e+00, 'w_spatial': 2.093139e+00, 'b_spatial': 6.969716e+00, 'w_out_a': 7.450499e+00, 'ln_kv': 1.069122e+01, 'w_kv': 4.513353e+00, 'g_k': 1.404090e+01, 'ln_mix_b': 1.027475e+00, 'w_q': 1.040095e+00, 'g_q': 1.394375e+01, 'w_out_b': 5.903830e+00, 'ln_mlp': 4.954853e+01, 'w_up': 3.160836e+00, 'w_down': 1.004555e+01, 'ln_ple': 7.643637e-01, 'w_ple_gate': 5.839442e-01, 'w_ple_proj': 3.295109e-01}


def _to_microbatches(a, axis):
    t = _jnp.moveaxis(a, axis, 0)
    t = t.reshape((N_MICROBATCH, t.shape[0] // N_MICROBATCH) + t.shape[1:])
    return _jnp.moveaxis(t, 1, axis + 1)


def setup_inputs(seed: int = 0) -> dict:
    inp = _fwd_setup_inputs(seed)
    key = _jax.random.fold_in(_jax.random.key(seed), 7919)
    shape, _ = _output_shape()
    out = dict(inp)
    out["loss_target"] = _jax.random.normal(_jax.random.fold_in(key, 0), shape, _jnp.float32)
    for i, name in enumerate(TWIN_WEIGHTS):
        w = inp[name].astype(_jnp.float32)
        if MOMENT_SCALE is None:
            s = _jnp.sqrt(_jnp.mean(_jnp.square(w)) + 1e-30)
        else:
            s = MOMENT_SCALE[name]
        km, kv = _jax.random.split(_jax.random.fold_in(key, i + 1))
        out[name] = w
        out["m_" + name] = s * _jax.random.normal(km, w.shape, _jnp.float32)
        out["v_" + name] = (s * s) * _jax.random.uniform(kv, w.shape, _jnp.float32, 0.5, 1.5)
    if N_MICROBATCH > 1:
        for name, axis in PER_EXAMPLE_BATCH_AXIS.items():
            out[name] = _to_microbatches(out[name], axis)
    return {'x': out['x'], 'p': out['p'], 'ln_mix_a': out['ln_mix_a'], 'w_in_a': out['w_in_a'], 'g_v_a': out['g_v_a'], 'w_spatial': out['w_spatial'], 'b_spatial': out['b_spatial'], 'w_out_a': out['w_out_a'], 'ln_kv': out['ln_kv'], 'w_kv': out['w_kv'], 'g_k': out['g_k'], 'ln_mix_b': out['ln_mix_b'], 'w_q': out['w_q'], 'g_q': out['g_q'], 'w_out_b': out['w_out_b'], 'ln_mlp': out['ln_mlp'], 'w_up': out['w_up'], 'w_down': out['w_down'], 'ln_ple': out['ln_ple'], 'w_ple_gate': out['w_ple_gate'], 'w_ple_proj': out['w_ple_proj'], 'loss_target': out['loss_target'], 'm_ln_mix_a': out['m_ln_mix_a'], 'm_w_in_a': out['m_w_in_a'], 'm_g_v_a': out['m_g_v_a'], 'm_w_spatial': out['m_w_spatial'], 'm_b_spatial': out['m_b_spatial'], 'm_w_out_a': out['m_w_out_a'], 'm_ln_kv': out['m_ln_kv'], 'm_w_kv': out['m_w_kv'], 'm_g_k': out['m_g_k'], 'm_ln_mix_b': out['m_ln_mix_b'], 'm_w_q': out['m_w_q'], 'm_g_q': out['m_g_q'], 'm_w_out_b': out['m_w_out_b'], 'm_ln_mlp': out['m_ln_mlp'], 'm_w_up': out['m_w_up'], 'm_w_down': out['m_w_down'], 'm_ln_ple': out['m_ln_ple'], 'm_w_ple_gate': out['m_w_ple_gate'], 'm_w_ple_proj': out['m_w_ple_proj'], 'v_ln_mix_a': out['v_ln_mix_a'], 'v_w_in_a': out['v_w_in_a'], 'v_g_v_a': out['v_g_v_a'], 'v_w_spatial': out['v_w_spatial'], 'v_b_spatial': out['v_b_spatial'], 'v_w_out_a': out['v_w_out_a'], 'v_ln_kv': out['v_ln_kv'], 'v_w_kv': out['v_w_kv'], 'v_g_k': out['v_g_k'], 'v_ln_mix_b': out['v_ln_mix_b'], 'v_w_q': out['v_w_q'], 'v_g_q': out['v_g_q'], 'v_w_out_b': out['v_w_out_b'], 'v_ln_mlp': out['v_ln_mlp'], 'v_w_up': out['v_w_up'], 'v_w_down': out['v_w_down'], 'v_ln_ple': out['v_ln_ple'], 'v_w_ple_gate': out['v_w_ple_gate'], 'v_w_ple_proj': out['v_w_ple_proj']}


def _loss(weights, diff, rest, loss_target):
    with _jax.named_scope("forward"):
        args = {**rest, TWIN_DIFF_INPUT: diff, **{k: w.astype(_WEIGHT_DTYPES[k]) for k, w in weights.items()}}
        y = _forward(args)
    with _jax.named_scope("loss_head"):
        err = _jnp.square(y.astype(_jnp.float32) - loss_target)
        return 0.5 * _jnp.sum(_jnp.mean(err, axis=-1)) if err.ndim else 0.5 * err


def _adamw(w, g, m, v):
    m = ADAM_B1 * m + (1.0 - ADAM_B1) * g
    v = ADAM_B2 * v + (1.0 - ADAM_B2) * _jnp.square(g)
    m_hat = m / (1.0 - ADAM_B1 ** ADAM_STEP)
    v_hat = v / (1.0 - ADAM_B2 ** ADAM_STEP)
    delta = -ADAM_LR * (m_hat / (_jnp.sqrt(v_hat) + ADAM_EPS) + ADAM_WD * w)
    return delta, m, v


def reference(x, p, ln_mix_a, w_in_a, g_v_a, w_spatial, b_spatial, w_out_a, ln_kv, w_kv, g_k, ln_mix_b, w_q, g_q, w_out_b, ln_mlp, w_up, w_down, ln_ple, w_ple_gate, w_ple_proj, loss_target, m_ln_mix_a, m_w_in_a, m_g_v_a, m_w_spatial, m_b_spatial, m_w_out_a, m_ln_kv, m_w_kv, m_g_k, m_ln_mix_b, m_w_q, m_g_q, m_w_out_b, m_ln_mlp, m_w_up, m_w_down, m_ln_ple, m_w_ple_gate, m_w_ple_proj, v_ln_mix_a, v_w_in_a, v_g_v_a, v_w_spatial, v_b_spatial, v_w_out_a, v_ln_kv, v_w_kv, v_g_k, v_ln_mix_b, v_w_q, v_g_q, v_w_out_b, v_ln_mlp, v_w_up, v_w_down, v_ln_ple, v_w_ple_gate, v_w_ple_proj):
    given = dict(x=x, p=p, ln_mix_a=ln_mix_a, w_in_a=w_in_a, g_v_a=g_v_a, w_spatial=w_spatial, b_spatial=b_spatial, w_out_a=w_out_a, ln_kv=ln_kv, w_kv=w_kv, g_k=g_k, ln_mix_b=ln_mix_b, w_q=w_q, g_q=g_q, w_out_b=w_out_b, ln_mlp=ln_mlp, w_up=w_up, w_down=w_down, ln_ple=ln_ple, w_ple_gate=w_ple_gate, w_ple_proj=w_ple_proj, loss_target=loss_target, m_ln_mix_a=m_ln_mix_a, m_w_in_a=m_w_in_a, m_g_v_a=m_g_v_a, m_w_spatial=m_w_spatial, m_b_spatial=m_b_spatial, m_w_out_a=m_w_out_a, m_ln_kv=m_ln_kv, m_w_kv=m_w_kv, m_g_k=m_g_k, m_ln_mix_b=m_ln_mix_b, m_w_q=m_w_q, m_g_q=m_g_q, m_w_out_b=m_w_out_b, m_ln_mlp=m_ln_mlp, m_w_up=m_w_up, m_w_down=m_w_down, m_ln_ple=m_ln_ple, m_w_ple_gate=m_w_ple_gate, m_w_ple_proj=m_w_ple_proj, v_ln_mix_a=v_ln_mix_a, v_w_in_a=v_w_in_a, v_g_v_a=v_g_v_a, v_w_spatial=v_w_spatial, v_b_spatial=v_b_spatial, v_w_out_a=v_w_out_a, v_ln_kv=v_ln_kv, v_w_kv=v_w_kv, v_g_k=v_g_k, v_ln_mix_b=v_ln_mix_b, v_w_q=v_w_q, v_g_q=v_g_q, v_w_out_b=v_w_out_b, v_ln_mlp=v_ln_mlp, v_w_up=v_w_up, v_w_down=v_w_down, v_ln_ple=v_ln_ple, v_w_ple_gate=v_w_ple_gate, v_w_ple_proj=v_w_ple_proj)
    weights = {n: given[n] for n in TWIN_WEIGHTS}
    shared = {n: given[n] for n in SHARED_INPUTS}
    per_example = {n: given[n] for n in ['x', 'p']}
    grad_fn = _jax.value_and_grad(_loss, argnums=(0, 1))

    def one_microbatch(ex, loss_target):
        ex = dict(ex)
        diff = ex.pop(TWIN_DIFF_INPUT)
        return grad_fn(weights, diff, {**shared, **ex}, loss_target)

    if N_MICROBATCH == 1:
        loss, (grad_w, grad_x) = one_microbatch(per_example, given["loss_target"])
    else:
        def body(carry, xs):
            loss_sum, grad_sum = carry
            l_k, (gw_k, gx_k) = one_microbatch(xs[0], xs[1])
            with _jax.named_scope("update"):
                return (loss_sum + l_k, _jax.tree.map(_jnp.add, grad_sum, gw_k)), gx_k

        init = (_jnp.zeros((), _jnp.float32), _jax.tree.map(_jnp.zeros_like, weights))
        (loss, grad_w), grad_x = _jax.lax.scan(body, init, (per_example, given["loss_target"]))
    with _jax.named_scope("update"):
        delta_w, new_m, new_v = {}, {}, {}
        for n in TWIN_WEIGHTS:
            delta_w[n], new_m[n], new_v[n] = _adamw(weights[n], grad_w[n], given["m_" + n], given["v_" + n])
    return (loss, grad_x, *[grad_w[n] for n in TWIN_WEIGHTS], *[delta_w[n] for n in TWIN_WEIGHTS],
            *[new_m[n] for n in TWIN_WEIGHTS], *[new_v[n] for n in TWIN_WEIGHTS])
```

```python
import functools

import jax
import jax.numpy as jnp
from jax import lax
from jax.experimental import pallas as pl
from jax.experimental.pallas import tpu as pltpu

F32 = jnp.float32
BF16 = jnp.bfloat16

D_MODEL = 1024
D_FF = 4096
PLE_DIM = 256
N_GROUPS = 8
CHUNK = 128
HEAD_DIM = 64
LANES = 128
ATT_BLOCK = 256
EPS = 1e-6
N_SHARDS = 4
VMEM_LIMIT = 56 * 1024 * 1024

ADAM_LR = 0.001
ADAM_B1 = 0.9
ADAM_B2 = 0.999
ADAM_EPS = 1e-08
ADAM_WD = 0.01
ADAM_STEP = 10

MESH = pl.DeviceIdType.MESH


def _pcall(body, *, name, out_shape, grid=None, in_specs=None, out_specs=None, scratch_shapes=(),
           semantics=None, aliases=None, collective_id=None, side_effects=False):
    params = dict(vmem_limit_bytes=VMEM_LIMIT)
    if semantics is not None:
        params["dimension_semantics"] = semantics
    if collective_id is not None:
        params["collective_id"] = collective_id
    if side_effects:
        params["has_side_effects"] = True
    kwargs = {}
    if grid is not None:
        kwargs["grid"] = grid
    if in_specs is not None:
        kwargs["in_specs"] = in_specs
    if out_specs is not None:
        kwargs["out_specs"] = out_specs
    if aliases:
        kwargs["input_output_aliases"] = aliases
    return pl.pallas_call(body, name=name, out_shape=out_shape, scratch_shapes=list(scratch_shapes),
                          compiler_params=pltpu.CompilerParams(**params), **kwargs)


def _sds(shape, dtype):
    return jax.ShapeDtypeStruct(shape, dtype)


_GELU_C = 0.7978845608028654
_GELU_A = 0.044715


def _gelu(x):
    inner = _GELU_C * (x + _GELU_A * (x * x * x))
    return 0.5 * x * (1.0 + jnp.tanh(inner))


def _gelu_grad(x):
    x2 = x * x
    t = jnp.tanh(_GELU_C * (x + _GELU_A * (x2 * x)))
    return 0.5 * (1.0 + t) + 0.5 * x * (1.0 - t * t) * (_GELU_C * (1.0 + 3.0 * _GELU_A * x2))


def _sigmoid(x):
    return 1.0 / (1.0 + jnp.exp(-x))


def _log_sigmoid(z):
    return jnp.minimum(z, 0.0) - jnp.log(1.0 + jnp.exp(-jnp.abs(z)))


def _split_bf16(a):
    hi = a.astype(BF16)
    lo = (a - hi.astype(F32)).astype(BF16)
    return hi, lo


def _dot(a, b):
    return jnp.dot(a, b, preferred_element_type=F32)


def _dot_nt(a, b):
    return lax.dot_general(a, b, (((1,), (1,)), ((), ())), preferred_element_type=F32)


def _dot_tn(a, b):
    return lax.dot_general(a, b, (((0,), (0,)), ((), ())), preferred_element_type=F32)


def _head_rstd(x):
    lane = lax.broadcasted_iota(jnp.int32, x.shape, 1)
    low = lane < HEAD_DIM
    sq = x * x
    s_lo = jnp.sum(jnp.where(low, sq, 0.0), axis=-1, keepdims=True)
    s_hi = jnp.sum(jnp.where(low, 0.0, sq), axis=-1, keepdims=True)
    ms = jnp.where(low, s_lo, s_hi) * (1.0 / HEAD_DIM)
    return lax.rsqrt(ms + EPS)


def _head_mean(x):
    lane = lax.broadcasted_iota(jnp.int32, x.shape, 1)
    low = lane < HEAD_DIM
    s_lo = jnp.sum(jnp.where(low, x, 0.0), axis=-1, keepdims=True)
    s_hi = jnp.sum(jnp.where(low, 0.0, x), axis=-1, keepdims=True)
    return jnp.where(low, s_lo, s_hi) * (1.0 / HEAD_DIM)


def norm_matmul(x, g, w, *, name, epilogue="none", tm=1024, tn=512):
    t, d = x.shape
    if w.ndim == 3:
        per = w.shape[2]
        n = N_SHARDS * per
        tn = min(tn, per)
        w_spec = pl.BlockSpec((None, d, tn), lambda i, j: (j // (per // tn), 0, j % (per // tn)))
    else:
        n = w.shape[1]
        tn = min(tn, n)
        w_spec = pl.BlockSpec((d, tn), lambda i, j: (0, j))
    tm = min(tm, t)

    def body(x_ref, g_ref, w_ref, h_ref, r_ref, *rest):
        outs, hs = rest[:-1], rest[-1]

        @pl.when(pl.program_id(1) == 0)
        def _():
            xv = x_ref[...]
            r = lax.rsqrt(jnp.mean(xv * xv, axis=-1, keepdims=True) + EPS)
            h = ((xv * r) * g_ref[...]).astype(BF16)
            hs[...] = h
            h_ref[...] = h
            r_ref[...] = r

        y = _dot(hs[...], w_ref[...])
        if epilogue == "none":
            outs[0][...] = y
        else:
            a = jnp.maximum(y, 0.0)
            outs[0][...] = a.astype(BF16)
            outs[1][...] = (a * a).astype(BF16)

    out_shape = [_sds((t, d), BF16), _sds((t, 1), F32)]
    out_specs = [pl.BlockSpec((tm, d), lambda i, j: (i, 0)), pl.BlockSpec((tm, 1), lambda i, j: (i, 0))]
    if epilogue == "none":
        out_shape.append(_sds((t, n), F32))
        out_specs.append(pl.BlockSpec((tm, tn), lambda i, j: (i, j)))
    else:
        out_shape += [_sds((t, n), BF16), _sds((t, n), BF16)]
        out_specs += [pl.BlockSpec((tm, tn), lambda i, j: (i, j))] * 2
    return _pcall(
        body, name=name, out_shape=out_shape, grid=(t // tm, n // tn),
        in_specs=[pl.BlockSpec((tm, d), lambda i, j: (i, 0)), pl.BlockSpec((1, d), lambda i, j: (0, 0)), w_spec],
        out_specs=out_specs, scratch_shapes=[pltpu.VMEM((tm, d), BF16)],
        semantics=("parallel", "arbitrary"))(x, g, w)


def matmul_residual(a, w, res, *, name, tm=512, tn=512):
    t, k = a.shape
    n = w.shape[1]
    tm, tn = min(tm, t), min(tn, n)

    def body(a_ref, w_ref, res_ref, o_ref):
        o_ref[...] = res_ref[...] + _dot(a_ref[...], w_ref[...])

    return _pcall(
        body, name=name, out_shape=_sds((t, n), F32), grid=(t // tm, n // tn),
        in_specs=[pl.BlockSpec((tm, k), lambda i, j: (i, 0)), pl.BlockSpec((k, tn), lambda i, j: (0, j)),
                  pl.BlockSpec((tm, tn), lambda i, j: (i, j))],
        out_specs=pl.BlockSpec((tm, tn), lambda i, j: (i, j)),
        semantics=("parallel", "parallel"))(a, w, res)


def ple_forward(x, g, w_gate, p, w_proj, *, name, tm=256):
    t, d = x.shape
    tm = min(tm, t)

    def body(x_ref, g_ref, wg_ref, p_ref, wp_ref, h_ref, r_ref, gate_ref, pp_ref, o_ref):
        xv = x_ref[...]
        r = lax.rsqrt(jnp.mean(xv * xv, axis=-1, keepdims=True) + EPS)
        h = ((xv * r) * g_ref[...]).astype(BF16)
        h_ref[...] = h
        r_ref[...] = r
        gate = _sigmoid(_dot(h, wg_ref[...]))
        gate_ref[...] = gate
        pb = p_ref[...].astype(BF16)
        per = d // N_SHARDS
        for s in range(N_SHARDS):
            cols = slice(s * per, (s + 1) * per)
            pp = _dot(pb, wp_ref[s])
            pp_ref[:, cols] = pp.astype(BF16)
            o_ref[:, cols] = xv[:, cols] + pp * gate[:, cols]

    row = lambda i: (i, 0)
    fixed = lambda i: (0, 0)
    return _pcall(
        body, name=name,
        out_shape=[_sds((t, d), BF16), _sds((t, 1), F32), _sds((t, d), F32), _sds((t, d), BF16), _sds((t, d), F32)],
        grid=(t // tm,),
        in_specs=[pl.BlockSpec((tm, d), row), pl.BlockSpec((1, d), fixed), pl.BlockSpec((d, d), fixed),
                  pl.BlockSpec((tm, PLE_DIM), row),
                  pl.BlockSpec((N_SHARDS, PLE_DIM, d // N_SHARDS), lambda i: (0, 0, 0))],
        out_specs=[pl.BlockSpec((tm, d), row), pl.BlockSpec((tm, 1), row), pl.BlockSpec((tm, d), row),
                   pl.BlockSpec((tm, d), row), pl.BlockSpec((tm, d), row)],
        semantics=("parallel",))(x, g, w_gate, p, w_proj)


def _tril_mask():
    r = lax.broadcasted_iota(jnp.int32, (CHUNK, CHUNK), 0)
    c = lax.broadcasted_iota(jnp.int32, (CHUNK, CHUNK), 1)
    return c <= r


def _sgu_common(pre_ref, gv_ref, ws_ref):
    pre = pre_ref[...]
    pre_u, pre_v = pre[:, :D_MODEL], pre[:, D_MODEL:]
    u = _gelu(pre_u)
    v = _gelu(pre_v)
    r = lax.rsqrt(jnp.mean(v * v, axis=-1, keepdims=True) + EPS)
    vhat = v * r
    vn = (vhat * gv_ref[...]).astype(BF16)
    tril = _tril_mask()
    wm = [jnp.where(tril, ws_ref[g], 0.0).astype(BF16) for g in range(N_GROUPS)]
    return pre_u, pre_v, u, r, vhat, vn, wm, tril


def sgu_forward(pre, g_v, w_s, b_full, *, name):
    t = pre.shape[0]

    def body(pre_ref, gv_ref, ws_ref, b_ref, y_ref):
        _, _, u, _, _, vn, wm, _ = _sgu_common(pre_ref, gv_ref, ws_ref)
        for g in range(N_GROUPS):
            cols = slice(g * LANES, (g + 1) * LANES)
            mix = _dot(wm[g], vn[:, cols]) + b_ref[:, cols]
            y_ref[:, cols] = (u[:, cols] * mix).astype(BF16)

    return _pcall(
        body, name=name, out_shape=_sds((t, D_MODEL), BF16), grid=(t // CHUNK,),
        in_specs=[pl.BlockSpec((CHUNK, 2 * D_MODEL), lambda i: (i, 0)), pl.BlockSpec((1, D_MODEL), lambda i: (0, 0)),
                  pl.BlockSpec((N_GROUPS, CHUNK, CHUNK), lambda i: (0, 0, 0)),
                  pl.BlockSpec((CHUNK, D_MODEL), lambda i: (0, 0))],
        out_specs=pl.BlockSpec((CHUNK, D_MODEL), lambda i: (i, 0)),
        semantics=("parallel",))(pre, g_v, w_s, b_full)


def head_norm(pre, g128, *, name, col_block=0, scale=1.0, passthrough=False, tm=512):
    t = pre.shape[0]
    tm = min(tm, t)

    def body(*refs):
        if passthrough:
            x_ref, v_ref, g_ref, o_ref, vo_ref = refs
            vo_ref[...] = v_ref[...].astype(BF16)
        else:
            x_ref, g_ref, o_ref = refs
        g = g_ref[...] * scale
        for b in range(D_MODEL // LANES):
            cols = slice(b * LANES, (b + 1) * LANES)
            xv = x_ref[:, cols]
            o_ref[:, cols] = ((xv * _head_rstd(xv)) * g).astype(BF16)

    x_spec = pl.BlockSpec((tm, D_MODEL), lambda i: (i, col_block))
    g_spec = pl.BlockSpec((1, LANES), lambda i: (0, 0))
    o_spec = pl.BlockSpec((tm, D_MODEL), lambda i: (i, 0))
    if passthrough:
        return _pcall(body, name=name, out_shape=[_sds((t, D_MODEL), BF16)] * 2, grid=(t // tm,),
                      in_specs=[x_spec, pl.BlockSpec((tm, D_MODEL), lambda i: (i, 1)), g_spec],
                      out_specs=[o_spec, o_spec], semantics=("parallel",))(pre, pre, g128)
    return _pcall(body, name=name, out_shape=_sds((t, D_MODEL), BF16), grid=(t // tm,),
                  in_specs=[x_spec, g_spec], out_specs=o_spec, semantics=("parallel",))(pre, g128)


def _suffix_matrix(n):
    r = lax.broadcasted_iota(jnp.int32, (n, n), 0)
    c = lax.broadcasted_iota(jnp.int32, (n, n), 1)
    return jnp.where(r > c, 1.0, 0.0).astype(BF16)


def _prefix_matrix(n):
    r = lax.broadcasted_iota(jnp.int32, (n, n), 0)
    c = lax.broadcasted_iota(jnp.int32, (n, n), 1)
    return jnp.where(r < c, 1.0, 0.0).astype(BF16)


def _exact_cumsum(a, tri):
    hi, lo = _split_bf16(a)
    return _dot(hi, tri) + _dot(lo, tri)


def _strict_causal(n):
    r = lax.broadcasted_iota(jnp.int32, (n, n), 0)
    c = lax.broadcasted_iota(jnp.int32, (n, n), 1)
    return c < r


def stick_breaking_forward(q, k, v, *, name):
    t = q.shape[0]
    blk = min(ATT_BLOCK, t)
    nq = t // blk

    def body(q_ref, k_ref, v_ref, o_ref):
        i = pl.program_id(1)
        lane = lax.broadcasted_iota(jnp.int32, (blk, LANES), 1)
        tri = _suffix_matrix(blk)
        causal = _strict_causal(blk)
        qv = q_ref[...]
        out = jnp.zeros((blk, LANES), F32)
        for hh in range(2):
            mine = (lane < HEAD_DIM) if hh == 0 else (lane >= HEAD_DIM)
            qm = jnp.where(mine, qv, jnp.zeros_like(qv))

            def block(j, carry, acc, masked):
                rows = pl.ds(pl.multiple_of(j * blk, blk), blk)
                z = _dot_nt(qm, k_ref[rows, :])
                ls = _log_sigmoid(z)
                lg = ls - z
                if masked:
                    lg = jnp.where(causal, lg, 0.0)
                s = ls + _exact_cumsum(lg, tri) + carry
                a = jnp.exp(s)
                if masked:
                    a = jnp.where(causal, a, 0.0)
                acc = acc + _dot(a.astype(BF16), v_ref[rows, :])
                return carry + jnp.sum(lg, axis=-1, keepdims=True), acc

            carry, acc = block(i, jnp.zeros((blk, 1), F32), jnp.zeros((blk, LANES), F32), True)

            def step(n, state):
                return block(i - 1 - n, state[0], state[1], False)

            _, acc = lax.fori_loop(0, i, step, (carry, acc))
            out = jnp.where(mine, acc, out)
        o_ref[...] = out.astype(BF16)

    return _pcall(
        body, name=name, out_shape=_sds((t, D_MODEL), BF16), grid=(D_MODEL // LANES, nq),
        in_specs=[pl.BlockSpec((blk, LANES), lambda p, i: (i, p)), pl.BlockSpec((t, LANES), lambda p, i: (0, p)),
                  pl.BlockSpec((t, LANES), lambda p, i: (0, p))],
        out_specs=pl.BlockSpec((blk, LANES), lambda p, i: (i, p)),
        semantics=("parallel", "arbitrary"))(q, k, v)


def loss_forward(x, target, *, name, tm=512):
    t, d = x.shape
    tm = min(tm, t)

    def body(x_ref, t_ref, l_ref, dx_ref):
        @pl.when(pl.program_id(0) == 0)
        def _():
            l_ref[...] = jnp.zeros_like(l_ref)

        diff = x_ref[...] - t_ref[...]
        dx_ref[...] = diff * (1.0 / d)
        l_ref[...] += 0.5 * jnp.sum(jnp.mean(diff * diff, axis=-1, keepdims=True))

    return _pcall(
        body, name=name, out_shape=[_sds((8, LANES), F32), _sds((t, d), F32)], grid=(t // tm,),
        in_specs=[pl.BlockSpec((tm, d), lambda i: (i, 0))] * 2,
        out_specs=[pl.BlockSpec((8, LANES), lambda i: (0, 0)), pl.BlockSpec((tm, d), lambda i: (i, 0))],
        semantics=("arbitrary",))(x, target)


def matmul_nt(dy, w, *, name, mul=None, out_dtype=F32, tm=512, tk=512):
    t, n = dy.shape
    k = w.shape[0]
    tm, tk = min(tm, t), min(tk, k)

    def body(*refs):
        if mul is None:
            dy_ref, w_ref, o_ref = refs
        else:
            dy_ref, w_ref, m_ref, o_ref = refs
        y = _dot_nt(dy_ref[...].astype(BF16), w_ref[...])
        if mul is not None:
            y = y * (2.0 * m_ref[...].astype(F32))
        o_ref[...] = y.astype(out_dtype)

    in_specs = [pl.BlockSpec((tm, n), lambda i, j: (i, 0)), pl.BlockSpec((tk, n), lambda i, j: (j, 0))]
    args = [dy, w]
    if mul is not None:
        in_specs.append(pl.BlockSpec((tm, tk), lambda i, j: (i, j)))
        args.append(mul)
    return _pcall(body, name=name, out_shape=_sds((t, k), out_dtype), grid=(t // tm, k // tk), in_specs=in_specs,
                  out_specs=pl.BlockSpec((tm, tk), lambda i, j: (i, j)), semantics=("parallel", "parallel"))(*args)


def matmul_tn(a, dy, *, name, col_shards, tk=512):
    t, k = a.shape
    n = dy.shape[1]
    tk = min(tk, k)
    if col_shards:
        tn = n // N_SHARDS
        out_shape = _sds((N_SHARDS, k, tn), F32)
        out_spec = pl.BlockSpec((None, tk, tn), lambda i, j: (j, i, 0))
    else:
        tn = min(512, n)
        out_shape = _sds((k, n), F32)
        out_spec = pl.BlockSpec((tk, tn), lambda i, j: (i, j))

    def body(a_ref, dy_ref, o_ref):
        o_ref[...] = _dot_tn(a_ref[...].astype(BF16), dy_ref[...].astype(BF16))

    return _pcall(body, name=name, out_shape=out_shape, grid=(k // tk, n // tn),
                  in_specs=[pl.BlockSpec((t, tk), lambda i, j: (0, i)), pl.BlockSpec((t, tn), lambda i, j: (0, j))],
                  out_specs=out_spec, semantics=("parallel", "parallel"))(a, dy)


def norm_backward(dpre, w, x, g, rstd, dx_out, *, name, tm=256):
    t, d = x.shape
    n = dpre.shape[1]
    tm = min(tm, t)
    if w.ndim == 3:
        w_spec = pl.BlockSpec(w.shape, lambda i: (0, 0, 0))
    else:
        w_spec = pl.BlockSpec(w.shape, lambda i: (0, 0))

    def body(dp_ref, w_ref, x_ref, g_ref, r_ref, dxo_ref, dx_ref, dg_ref):
        @pl.when(pl.program_id(0) == 0)
        def _():
            dg_ref[...] = jnp.zeros_like(dg_ref)

        if w.ndim == 3:
            per = n // N_SHARDS
            dh = _dot_nt(dp_ref[:, 0:per], w_ref[0])
            for s in range(1, N_SHARDS):
                dh = dh + _dot_nt(dp_ref[:, s * per:(s + 1) * per], w_ref[s])
        else:
            dh = _dot_nt(dp_ref[...], w_ref[...])
        r = r_ref[...]
        xn = x_ref[...] * r
        dg_ref[...] += jnp.sum(dh * xn, axis=0, keepdims=True)
        dxn = dh * g_ref[...]
        dx = r * (dxn - xn * jnp.mean(dxn * xn, axis=-1, keepdims=True))
        dx_ref[...] = dxo_ref[...] + dx

    row = lambda i: (i, 0)
    fixed = lambda i: (0, 0)
    return _pcall(
        body, name=name, out_shape=[_sds((t, d), F32), _sds((1, d), F32)], grid=(t // tm,),
        in_specs=[pl.BlockSpec((tm, n), row), w_spec, pl.BlockSpec((tm, d), row),
                  pl.BlockSpec((1, d), fixed), pl.BlockSpec((tm, 1), row), pl.BlockSpec((tm, d), row)],
        out_specs=[pl.BlockSpec((tm, d), row), pl.BlockSpec((1, d), fixed)],
        semantics=("arbitrary",))(dpre, w, x, g, rstd, dx_out)


def ple_backward(dx, gate, pp, *, name, tm=512):
    t, d = dx.shape
    tm = min(tm, t)

    def body(dx_ref, gate_ref, pp_ref, dg_ref, dp_ref):
        dxv = dx_ref[...]
        gate = gate_ref[...]
        dg_ref[...] = (dxv * pp_ref[...].astype(F32) * (gate * (1.0 - gate))).astype(BF16)
        dp_ref[...] = (dxv * gate).astype(BF16)

    spec = pl.BlockSpec((tm, d), lambda i: (i, 0))
    return _pcall(body, name=name, out_shape=[_sds((t, d), BF16)] * 2, grid=(t // tm,), in_specs=[spec] * 3,
                  out_specs=[spec] * 2, semantics=("parallel",))(dx, gate, pp)


def sgu_backward(dy, pre, g_v, w_s, b_full, *, name):
    t = pre.shape[0]
    n_chunks = t // CHUNK

    def body(dy_ref, pre_ref, gv_ref, ws_ref, b_ref, dpre_ref, dws_ref, db_ref, dgv_ref, dvn_s, dbf_s):
        step = pl.program_id(0)

        @pl.when(step == 0)
        def _():
            dws_ref[...] = jnp.zeros_like(dws_ref)
            dgv_ref[...] = jnp.zeros_like(dgv_ref)
            dbf_s[...] = jnp.zeros_like(dbf_s)

        pre_u, pre_v, u, r, vhat, vn, wm, tril = _sgu_common(pre_ref, gv_ref, ws_ref)
        dyv = dy_ref[...]
        for g in range(N_GROUPS):
            cols = slice(g * LANES, (g + 1) * LANES)
            mix = _dot(wm[g], vn[:, cols]) + b_ref[:, cols]
            dmix = dyv[:, cols] * u[:, cols]
            dmix_b = dmix.astype(BF16)
            du = dyv[:, cols] * mix
            dpre_ref[:, cols] = (du * _gelu_grad(pre_u[:, cols])).astype(BF16)
            dws_ref[g] += jnp.where(tril, _dot_nt(dmix_b, vn[:, cols]), 0.0)
            dbf_s[:, cols] += dmix
            dvn_s[:, cols] = _dot_tn(wm[g], dmix_b)
        dvn = dvn_s[...]
        dgv_ref[...] += jnp.sum(dvn * vhat, axis=0, keepdims=True)
        dxn = dvn * gv_ref[...]
        dv = r * (dxn - vhat * jnp.mean(dxn * vhat, axis=-1, keepdims=True))
        dpre_ref[:, D_MODEL:] = (dv * _gelu_grad(pre_v)).astype(BF16)

        @pl.when(step == n_chunks - 1)
        def _():
            lane = lax.broadcasted_iota(jnp.int32, (CHUNK, LANES), 1)
            acc = jnp.zeros((CHUNK, LANES), F32)
            for g in range(N_GROUPS):
                s = jnp.sum(dbf_s[:, g * LANES:(g + 1) * LANES], axis=-1, keepdims=True)
                acc = jnp.where(lane == g, s, acc)
            db_ref[...] = acc

    fixed2 = lambda i: (0, 0)
    return _pcall(
        body, name=name,
        out_shape=[_sds((t, 2 * D_MODEL), BF16), _sds((N_GROUPS, CHUNK, CHUNK), F32), _sds((CHUNK, LANES), F32),
                   _sds((1, D_MODEL), F32)],
        grid=(n_chunks,),
        in_specs=[pl.BlockSpec((CHUNK, D_MODEL), lambda i: (i, 0)), pl.BlockSpec((CHUNK, 2 * D_MODEL), lambda i: (i, 0)),
                  pl.BlockSpec((1, D_MODEL), fixed2), pl.BlockSpec((N_GROUPS, CHUNK, CHUNK), lambda i: (0, 0, 0)),
                  pl.BlockSpec((CHUNK, D_MODEL), fixed2)],
        out_specs=[pl.BlockSpec((CHUNK, 2 * D_MODEL), lambda i: (i, 0)),
                   pl.BlockSpec((N_GROUPS, CHUNK, CHUNK), lambda i: (0, 0, 0)), pl.BlockSpec((CHUNK, LANES), fixed2),
                   pl.BlockSpec((1, D_MODEL), fixed2)],
        scratch_shapes=[pltpu.VMEM((CHUNK, D_MODEL), F32), pltpu.VMEM((CHUNK, D_MODEL), F32)],
        semantics=("arbitrary",))(dy, pre, g_v, w_s, b_full)


def head_norm_backward(dy, pre, g128, *, name, col_block=0, scale=1.0, passthrough=None, tm=512):
    t = dy.shape[0]
    tm = min(tm, t)
    width = 2 * D_MODEL if passthrough is not None else D_MODEL

    def body(*refs):
        if passthrough is not None:
            dy_ref, x_ref, g_ref, dv_ref, o_ref, dg_ref = refs
            o_ref[:, D_MODEL:] = dv_ref[...].astype(BF16)
        else:
            dy_ref, x_ref, g_ref, o_ref, dg_ref = refs

        @pl.when(pl.program_id(0) == 0)
        def _():
            dg_ref[...] = jnp.zeros_like(dg_ref)

        g = g_ref[...]
        dg = jnp.zeros((1, LANES), F32)
        for b in range(D_MODEL // LANES):
            cols = slice(b * LANES, (b + 1) * LANES)
            xv = x_ref[:, cols]
            r = _head_rstd(xv)
            xn = xv * r
            dyv = dy_ref[:, cols] * scale
            dg = dg + jnp.sum(dyv * xn, axis=0, keepdims=True)
            dxn = dyv * g
            o_ref[:, cols] = (r * (dxn - xn * _head_mean(dxn * xn))).astype(BF16)
        dg_ref[...] += dg

    row = lambda i: (i, 0)
    in_specs = [pl.BlockSpec((tm, D_MODEL), row), pl.BlockSpec((tm, D_MODEL), lambda i: (i, col_block)),
                pl.BlockSpec((1, LANES), lambda i: (0, 0))]
    args = [dy, pre, g128]
    if passthrough is not None:
        in_specs.append(pl.BlockSpec((tm, D_MODEL), row))
        args.append(passthrough)
    return _pcall(body, name=name, out_shape=[_sds((t, width), BF16), _sds((1, LANES), F32)], grid=(t // tm,),
                  in_specs=in_specs,
                  out_specs=[pl.BlockSpec((tm, width), row), pl.BlockSpec((1, LANES), lambda i: (0, 0))],
                  semantics=("arbitrary",))(*args)


def stick_breaking_backward(q, k, v, do, *, name):
    t = q.shape[0]
    blk = min(ATT_BLOCK, t)
    nq = t // blk

    def body(q_ref, k_ref, v_ref, do_ref, dq_ref, dk_ref, dv_ref, s_buf, sg_buf):
        i = pl.program_id(1)

        @pl.when(i == 0)
        def _():
            dk_ref[...] = jnp.zeros_like(dk_ref)
            dv_ref[...] = jnp.zeros_like(dv_ref)

        lane = lax.broadcasted_iota(jnp.int32, (blk, LANES), 1)
        suffix = _suffix_matrix(blk)
        prefix = _prefix_matrix(blk)
        causal = _strict_causal(blk)
        qv = q_ref[...]
        dov = do_ref[...]
        dq_out = jnp.zeros((blk, LANES), F32)
        for hh in range(2):
            mine = (lane < HEAD_DIM) if hh == 0 else (lane >= HEAD_DIM)
            qm = jnp.where(mine, qv, jnp.zeros_like(qv))
            dom = jnp.where(mine, dov, jnp.zeros_like(dov))

            def log_weights(j, carry, masked):
                rows = pl.ds(pl.multiple_of(j * blk, blk), blk)
                z = _dot_nt(qm, k_ref[rows, :])
                ls = _log_sigmoid(z)
                lg = ls - z
                if masked:
                    lg = jnp.where(causal, lg, 0.0)
                s_buf[j] = ls + _exact_cumsum(lg, suffix) + carry
                sg_buf[j] = jnp.exp(ls)
                return carry + jnp.sum(lg, axis=-1, keepdims=True)

            carry = log_weights(i, jnp.zeros((blk, 1), F32), True)
            lax.fori_loop(0, i, lambda n, c: log_weights(i - 1 - n, c, False), carry)

            def grads(j, pcarry, dq_acc, masked):
                rows = pl.ds(pl.multiple_of(j * blk, blk), blk)
                a = jnp.exp(s_buf[j])
                if masked:
                    a = jnp.where(causal, a, 0.0)
                sg = sg_buf[j]
                ds = _dot_nt(dom, v_ref[rows, :]) * a
                before = _exact_cumsum(ds, prefix) + pcarry
                if masked:
                    before = jnp.where(causal, before, 0.0)
                dz = (ds * (1.0 - sg) - sg * before).astype(BF16)
                dq_acc = dq_acc + _dot(dz, k_ref[rows, :])
                dk_ref[rows, :] += _dot_tn(dz, qm)
                dv_ref[rows, :] += _dot_tn(a.astype(BF16), dom)
                return pcarry + jnp.sum(ds, axis=-1, keepdims=True), dq_acc

            state = lax.fori_loop(0, i, lambda j, st: grads(j, st[0], st[1], False),
                                  (jnp.zeros((blk, 1), F32), jnp.zeros((blk, LANES), F32)))
            _, dq_acc = grads(i, state[0], state[1], True)
            dq_out = jnp.where(mine, dq_acc, dq_out)
        dq_ref[...] = dq_out

    full = pl.BlockSpec((t, LANES), lambda p, i: (0, p))
    qblk = pl.BlockSpec((blk, LANES), lambda p, i: (i, p))
    return _pcall(
        body, name=name, out_shape=[_sds((t, D_MODEL), F32)] * 3, grid=(D_MODEL // LANES, nq),
        in_specs=[qblk, full, full, qblk], out_specs=[qblk, full, full],
        scratch_shapes=[pltpu.VMEM((nq, blk, blk), F32), pltpu.VMEM((nq, blk, blk), F32)],
        semantics=("parallel", "arbitrary"))(q, k, v, do)


def _mlp_forward(x, g, w_up, w_down, tag):
    h, r, a, a2 = norm_matmul(x, g, w_up, name=f"mlp_up_{tag}", epilogue="relu2")
    return matmul_residual(a2, w_down, x, name=f"mlp_down_{tag}"), (x, h, r, a, a2)


def _mlp_backward(dx, saved, g, w_up, w_down, tag):
    x, h, r, a, a2 = saved
    d_w_down = matmul_tn(a2, dx, name=f"d_w_down_{tag}", col_shards=False)
    dpre = matmul_nt(dx, w_down, name=f"d_mlp_act_{tag}", mul=a, out_dtype=BF16)
    d_w_up = matmul_tn(h, dpre, name=f"d_w_up_{tag}", col_shards=True)
    dx, d_g = norm_backward(dpre, w_up, x, g, r, dx, name=f"d_mlp_norm_{tag}")
    return dx, d_w_up, d_w_down, d_g


def _ple_backward(dx, saved, p, g, w_gate, tag):
    x, h, r, gate, pp = saved
    dgate, dproj = ple_backward(dx, gate, pp, name=f"d_ple_{tag}")
    d_w_proj = matmul_tn(p, dproj, name=f"d_w_ple_proj_{tag}", col_shards=True)
    d_w_gate = matmul_tn(h, dgate, name=f"d_w_ple_gate_{tag}", col_shards=False)
    dx, d_g = norm_backward(dgate, w_gate, x, g, r, dx, name=f"d_ple_norm_{tag}")
    return dx, d_w_gate, d_w_proj, d_g


def local_step(x, p, target, w):
    row = lambda v: v.reshape(1, -1)
    g128 = lambda v: jnp.tile(v.reshape(1, HEAD_DIM), (1, 2))
    scale = HEAD_DIM ** -0.5
    b_full = jnp.repeat(jnp.transpose(w["b_spatial"][0]), LANES, axis=1)
    w_s = w["w_spatial"][0]

    x0 = x
    h_a, r_a, pre_a = norm_matmul(x0, row(w["ln_mix_a"][0]), w["w_in_a"][0], name="sgu_in")
    y_a = sgu_forward(pre_a, row(w["g_v_a"][0]), w_s, b_full, name="sgu_mix")
    x1 = matmul_residual(y_a, w["w_out_a"][0], x0, name="sgu_out")
    x2, mlp0 = _mlp_forward(x1, row(w["ln_mlp"][0]), w["w_up"][0], w["w_down"][0], 0)
    ple0 = ple_forward(x2, row(w["ln_ple"][0]), w["w_ple_gate"][0], p[0], w["w_ple_proj"][0], name="ple_0")
    x3 = ple0[4]
    h_kv, r_kv, kv_pre = norm_matmul(x3, row(w["ln_kv"]), w["w_kv"], name="kv_proj")
    k_n, v_b = head_norm(kv_pre, g128(w["g_k"]), name="k_norm", passthrough=True)
    h_q, r_q, q_pre = norm_matmul(x3, row(w["ln_mix_b"][0]), w["w_q"][0], name="q_proj")
    q_n = head_norm(q_pre, g128(w["g_q"][0]), name="q_norm", scale=scale)
    o = stick_breaking_forward(q_n, k_n, v_b, name="sb_fwd")
    x4 = matmul_residual(o, w["w_out_b"][0], x3, name="sb_out")
    x5, mlp1 = _mlp_forward(x4, row(w["ln_mlp"][1]), w["w_up"][1], w["w_down"][1], 1)
    ple1 = ple_forward(x5, row(w["ln_ple"][1]), w["w_ple_gate"][1], p[1], w["w_ple_proj"][1], name="ple_1")
    x6 = ple1[4]
    loss_blk, dx = loss_forward(x6, target, name="loss")

    g = {}
    dx, dwg1, dwp1, dlnp1 = _ple_backward(dx, (x5,) + tuple(ple1[:4]), p[1], row(w["ln_ple"][1]), w["w_ple_gate"][1], 1)
    dx, dwu1, dwd1, dlnm1 = _mlp_backward(dx, mlp1, row(w["ln_mlp"][1]), w["w_up"][1], w["w_down"][1], 1)
    g["w_out_b"] = matmul_tn(o, dx, name="d_w_out_b", col_shards=False)
    do = matmul_nt(dx, w["w_out_b"][0], name="d_sb_out", out_dtype=BF16)
    dq_n, dk_n, dv = stick_breaking_backward(q_n, k_n, v_b, do, name="sb_bwd")
    dq_pre, dgq = head_norm_backward(dq_n, q_pre, g128(w["g_q"][0]), name="d_q_norm", scale=scale)
    dkv_pre, dgk = head_norm_backward(dk_n, kv_pre, g128(w["g_k"]), name="d_k_norm", passthrough=dv)
    g["w_q"] = matmul_tn(h_q, dq_pre, name="d_w_q", col_shards=False)
    g["w_kv"] = matmul_tn(h_kv, dkv_pre, name="d_w_kv", col_shards=True)
    dx, g["ln_mix_b"] = norm_backward(dq_pre, w["w_q"][0], x3, row(w["ln_mix_b"][0]), r_q, dx, name="d_q_in")
    dx, g["ln_kv"] = norm_backward(dkv_pre, w["w_kv"], x3, row(w["ln_kv"]), r_kv, dx, name="d_kv_in")
    g["g_q"] = dgq[:, :HEAD_DIM] + dgq[:, HEAD_DIM:]
    g["g_k"] = (dgk[:, :HEAD_DIM] + dgk[:, HEAD_DIM:]).reshape(HEAD_DIM)
    g["ln_kv"] = g["ln_kv"].reshape(D_MODEL)
    dx, dwg0, dwp0, dlnp0 = _ple_backward(dx, (x2,) + tuple(ple0[:4]), p[0], row(w["ln_ple"][0]), w["w_ple_gate"][0], 0)
    dx, dwu0, dwd0, dlnm0 = _mlp_backward(dx, mlp0, row(w["ln_mlp"][0]), w["w_up"][0], w["w_down"][0], 0)
    g["w_out_a"] = matmul_tn(y_a, dx, name="d_w_out_a", col_shards=False)
    dy_a = matmul_nt(dx, w["w_out_a"][0], name="d_sgu_out")
    dpre_a, dws, db, g["g_v_a"] = sgu_backward(dy_a, pre_a, row(w["g_v_a"][0]), w_s, b_full, name="d_sgu_mix")
    g["w_in_a"] = matmul_tn(h_a, dpre_a, name="d_w_in_a", col_shards=True)
    dx, g["ln_mix_a"] = norm_backward(dpre_a, w["w_in_a"][0], x0, row(w["ln_mix_a"][0]), r_a, dx, name="d_sgu_in")
    g["w_spatial"] = dws[None]
    g["b_spatial"] = jnp.transpose(db[:, :N_GROUPS])[None]
    g["w_up"] = (dwu0, dwu1)
    g["w_down"] = (dwd0, dwd1)
    g["w_ple_gate"] = (dwg0, dwg1)
    g["w_ple_proj"] = (dwp0, dwp1)
    g["ln_mlp"] = jnp.concatenate([dlnm0, dlnm1], axis=0)
    g["ln_ple"] = jnp.concatenate([dlnp0, dlnp1], axis=0)
    return loss_blk, dx, g


ANY = pl.BlockSpec(memory_space=pl.ANY)


def _place():
    x, y, c = lax.axis_index("x"), lax.axis_index("y"), lax.axis_index("c")
    others = [(1 - x, y), (x, 1 - y), (1 - x, 1 - y)]
    return x, y, c, 2 * x + y, others


def cast_bf16(w3, layer, *, name, tm=256):
    _, r, c = w3.shape
    tm = min(tm, r)

    def body(w_ref, o_ref):
        o_ref[...] = w_ref[...].astype(BF16)

    return _pcall(body, name=name, out_shape=_sds((r, c), BF16), grid=(r // tm,),
                  in_specs=[pl.BlockSpec((None, tm, c), lambda i: (layer, i, 0))],
                  out_specs=pl.BlockSpec((tm, c), lambda i: (i, 0)), semantics=("parallel",))(w3)


def gather_shards(mats, vecs, *, name):
    nm, nv = len(mats), len(vecs)
    halves = [m.reshape(2, m.shape[0] // 2, m.shape[1]) for m in mats]

    def body(*refs):
        src, vsrc = refs[:nm], refs[nm:nm + nv]
        out, vout = refs[nm + nv:2 * nm + nv], refs[2 * nm + nv:2 * (nm + nv)]
        send, recv, vsend, vrecv, loc = refs[2 * (nm + nv):]
        x, y, c, s_me, others = _place()
        sib = (x, y, 1 - c)

        def ici(l, k):
            ox, oy = others[k]
            return pltpu.make_async_remote_copy(src[l].at[c], out[l].at[s_me, c], send.at[l, k], recv.at[l, k],
                                                device_id=(ox, oy, c), device_id_type=MESH)

        def landed(l, k, half):
            ox, oy = others[k]
            return out[l].at[2 * ox + oy, half]

        def passed_on(l, k):
            return pltpu.make_async_remote_copy(landed(l, k, c), landed(l, k, c), send.at[l, 3 + k], recv.at[l, 3 + k],
                                                device_id=sib, device_id_type=MESH)

        def vec(l, k):
            ox, oy = others[k]
            return pltpu.make_async_remote_copy(vsrc[l], vout[l].at[s_me], vsend.at[l, k], vrecv.at[l, k],
                                                device_id=(ox, oy, c), device_id_type=MESH)

        for l in range(nm):
            for k in range(3):
                ici(l, k).start()
        for l in range(nv):
            for k in range(3):
                vec(l, k).start()
        for l in range(nm):
            for h in range(2):
                own = pltpu.make_async_copy(src[l].at[h], out[l].at[s_me, h], loc)
                own.start()
                own.wait()
        for l in range(nv):
            own = pltpu.make_async_copy(vsrc[l], vout[l].at[s_me], loc)
            own.start()
            own.wait()
        for l in range(nm):
            for k in range(3):
                pltpu.make_async_remote_copy(landed(l, k, c), landed(l, k, c), send.at[l, k], recv.at[l, k],
                                             device_id=sib, device_id_type=MESH).wait_recv()
                passed_on(l, k).start()
        for l in range(nm):
            for k in range(3):
                pltpu.make_async_remote_copy(landed(l, k, 1 - c), landed(l, k, 1 - c), send.at[l, 3 + k],
                                             recv.at[l, 3 + k], device_id=sib, device_id_type=MESH).wait_recv()
        for l in range(nv):
            for k in range(3):
                ox, oy = others[k]
                pltpu.make_async_remote_copy(vsrc[l], vout[l].at[2 * ox + oy], vsend.at[l, k], vrecv.at[l, k],
                                             device_id=sib, device_id_type=MESH).wait_recv()
        for l in range(nm):
            for k in range(3):
                ici(l, k).wait_send()
                passed_on(l, k).wait_send()
        for l in range(nv):
            for k in range(3):
                vec(l, k).wait_send()

    out_shape = ([_sds((N_SHARDS,) + h.shape, BF16) for h in halves]
                 + [_sds((N_SHARDS,) + v.shape, F32) for v in vecs])
    res = _pcall(body, name=name, out_shape=out_shape, in_specs=[ANY] * (nm + nv), out_specs=[ANY] * (nm + nv),
                 scratch_shapes=[pltpu.SemaphoreType.DMA((nm, 6)), pltpu.SemaphoreType.DMA((nm, 6)),
                                 pltpu.SemaphoreType.DMA((max(nv, 1), 3)), pltpu.SemaphoreType.DMA((max(nv, 1), 3)),
                                 pltpu.SemaphoreType.DMA(())],
                 side_effects=True)(*halves, *vecs)
    gm = [r.reshape(N_SHARDS, m.shape[0], m.shape[1]) for r, m in zip(res[:nm], mats)]
    return gm, list(res[nm:])


def pair_exchange(grads, *, name):
    n = len(grads)
    views = [g.reshape(N_SHARDS, 2, g.shape[1] // 2, g.shape[2]) for g in grads]

    def body(*refs):
        src, own, got = refs[:n], refs[n:2 * n], refs[2 * n:3 * n]
        send, recv, loc = refs[3 * n:]
        x, y, c, _, _ = _place()
        sib = (x, y, 1 - c)

        def swap(l):
            return pltpu.make_async_remote_copy(src[l].at[:, 1 - c], got[l], send.at[l], recv.at[l],
                                                device_id=sib, device_id_type=MESH)

        for l in range(n):
            swap(l).start()
        for l in range(n):
            keep = pltpu.make_async_copy(src[l].at[:, c], own[l], loc)
            keep.start()
            keep.wait()
        for l in range(n):
            swap(l).wait()

    half = [_sds((N_SHARDS, v.shape[2], v.shape[3]), F32) for v in views]
    res = _pcall(body, name=name, out_shape=half + half, in_specs=[ANY] * n, out_specs=[ANY] * (2 * n),
                 scratch_shapes=[pltpu.SemaphoreType.DMA((n,)), pltpu.SemaphoreType.DMA((n,)),
                                 pltpu.SemaphoreType.DMA(())],
                 side_effects=True)(*views)
    return list(zip(res[:n], res[n:]))


def add_to_wire(a, b, *, name, tm=256):
    s, r, c = a.shape
    tm = min(tm, r)

    def body(a_ref, b_ref, o_ref):
        o_ref[...] = (a_ref[...] + b_ref[...]).astype(BF16)

    spec = pl.BlockSpec((None, tm, c), lambda i, j: (i, j, 0))
    return _pcall(body, name=name, out_shape=_sds((s, r, c), BF16), grid=(s, r // tm), in_specs=[spec, spec],
                  out_specs=spec, semantics=("parallel", "parallel"))(a, b)


def chip_exchange(parts, *, name):
    n = len(parts)

    def body(*refs):
        src, out = refs[:n], refs[n:2 * n]
        send, recv, loc = refs[2 * n:]
        x, y, c, s_me, others = _place()

        def ici(l, k):
            ox, oy = others[k]
            return pltpu.make_async_remote_copy(src[l].at[2 * ox + oy], out[l].at[s_me], send.at[l, k], recv.at[l, k],
                                                device_id=(ox, oy, c), device_id_type=MESH)

        for l in range(n):
            for k in range(3):
                ici(l, k).start()
        for l in range(n):
            own = pltpu.make_async_copy(src[l].at[s_me], out[l].at[s_me], loc)
            own.start()
            own.wait()
        for l in range(n):
            for k in range(3):
                ox, oy = others[k]
                pltpu.make_async_remote_copy(src[l].at[s_me], out[l].at[2 * ox + oy], send.at[l, k], recv.at[l, k],
                                             device_id=(ox, oy, c), device_id_type=MESH).wait_recv()
        for l in range(n):
            for k in range(3):
                ici(l, k).wait_send()

    res = _pcall(body, name=name, out_shape=[_sds(p.shape, p.dtype) for p in parts], in_specs=[ANY] * n,
                 out_specs=[ANY] * n,
                 scratch_shapes=[pltpu.SemaphoreType.DMA((n, 3)), pltpu.SemaphoreType.DMA((n, 3)),
                                 pltpu.SemaphoreType.DMA(())],
                 side_effects=True)(*parts)
    return list(res)


def sum_chips(parts, *, name, tm=256):
    _, r, c = parts.shape
    tm = min(tm, r)

    def body(p_ref, o_ref):
        o_ref[...] = ((p_ref[0].astype(F32) + p_ref[1].astype(F32)) + p_ref[2].astype(F32)) + p_ref[3].astype(F32)

    return _pcall(body, name=name, out_shape=_sds((r, c), F32), grid=(r // tm,),
                  in_specs=[pl.BlockSpec((N_SHARDS, tm, c), lambda i: (0, i, 0))],
                  out_specs=pl.BlockSpec((tm, c), lambda i: (i, 0)), semantics=("parallel",))(parts)


def pair_share(totals, layout, out_shapes, *, name):
    n = len(totals)

    def body(*refs):
        src, out = refs[:n], refs[n:n + len(out_shapes)]
        send, recv, loc = refs[n + len(out_shapes):]
        x, y, c, _, _ = _place()
        sib = (x, y, 1 - c)

        def rows(l, half):
            o, lead = layout[l]
            rh = totals[l].shape[0]
            return out[o].at[lead + (pl.ds(pl.multiple_of(half * rh, 8), rh), slice(None))]

        def share(l):
            return pltpu.make_async_remote_copy(src[l], rows(l, c), send.at[l], recv.at[l],
                                                device_id=sib, device_id_type=MESH)

        for l in range(n):
            share(l).start()
        for l in range(n):
            keep = pltpu.make_async_copy(src[l], rows(l, c), loc)
            keep.start()
            keep.wait()
        for l in range(n):
            pltpu.make_async_remote_copy(src[l], rows(l, 1 - c), send.at[l], recv.at[l],
                                         device_id=sib, device_id_type=MESH).wait_recv()
            share(l).wait_send()

    res = _pcall(body, name=name, out_shape=[_sds(s, F32) for s in out_shapes], in_specs=[ANY] * n,
                 out_specs=[ANY] * len(out_shapes),
                 scratch_shapes=[pltpu.SemaphoreType.DMA((n,)), pltpu.SemaphoreType.DMA((n,)),
                                 pltpu.SemaphoreType.DMA(())],
                 side_effects=True)(*totals)
    return list(res)


def all_reduce_small(packed, *, name):
    n_dev, r, c = packed.shape

    def body(in_ref, out_ref, land, send, recv):
        x, y, cc, _, _ = _place()
        me = 4 * x + 2 * y + cc
        peers = [(px, py, pc) for px in range(2) for py in range(2) for pc in range(2)]

        def scatter(d):
            return pltpu.make_async_remote_copy(in_ref.at[d], land.at[me], send.at[0, d], recv.at[0, me],
                                                device_id=peers[d], device_id_type=MESH)

        def gather(d):
            return pltpu.make_async_remote_copy(out_ref.at[me], out_ref.at[me], send.at[1, d], recv.at[1, me],
                                                device_id=peers[d], device_id_type=MESH)

        for d in range(n_dev):
            @pl.when(d != me)
            def _():
                scatter(d).start()
        land[me] = in_ref[me]
        for d in range(n_dev):
            @pl.when(d != me)
            def _():
                pltpu.make_async_remote_copy(in_ref.at[d], land.at[d], send.at[0, d], recv.at[0, d],
                                             device_id=peers[d], device_id_type=MESH).wait_recv()
        total = land[0]
        for d in range(1, n_dev):
            total = total + land[d]
        out_ref[me] = total
        for d in range(n_dev):
            @pl.when(d != me)
            def _():
                gather(d).start()
        for d in range(n_dev):
            @pl.when(d != me)
            def _():
                pltpu.make_async_remote_copy(out_ref.at[d], out_ref.at[d], send.at[1, d], recv.at[1, d],
                                             device_id=peers[d], device_id_type=MESH).wait_recv()
        for d in range(n_dev):
            @pl.when(d != me)
            def _():
                scatter(d).wait_send()
                gather(d).wait_send()

    vm = pl.BlockSpec(memory_space=pltpu.VMEM)
    return _pcall(body, name=name, out_shape=_sds(packed.shape, F32), in_specs=[vm], out_specs=vm,
                  scratch_shapes=[pltpu.VMEM(packed.shape, F32), pltpu.SemaphoreType.DMA((2, n_dev)),
                                  pltpu.SemaphoreType.DMA((2, n_dev))],
                  side_effects=True)(packed)


def adamw(w, g, m, v, *, name, tm=256):
    shape = w.shape
    cols = shape[-1]
    rows = 1
    for s in shape[:-1]:
        rows *= s
    tm = min(tm, rows)
    assert rows % tm == 0
    two_d = lambda a: a.reshape(rows, cols)

    def body(w_ref, g_ref, m_ref, v_ref, d_ref, mo_ref, vo_ref):
        gv = g_ref[...]
        m_new = ADAM_B1 * m_ref[...] + (1.0 - ADAM_B1) * gv
        v_new = ADAM_B2 * v_ref[...] + (1.0 - ADAM_B2) * (gv * gv)
        m_hat = m_new / (1.0 - ADAM_B1 ** ADAM_STEP)
        v_hat = v_new / (1.0 - ADAM_B2 ** ADAM_STEP)
        d_ref[...] = -ADAM_LR * (m_hat / (jnp.sqrt(v_hat) + ADAM_EPS) + ADAM_WD * w_ref[...])
        mo_ref[...] = m_new
        vo_ref[...] = v_new

    spec = pl.BlockSpec((tm, cols), lambda i: (i, 0))
    outs = _pcall(body, name=name, out_shape=[_sds((rows, cols), F32)] * 3, grid=(rows // tm,), in_specs=[spec] * 4,
                  out_specs=[spec] * 3, semantics=("parallel",))(two_d(w), two_d(g), two_d(m), two_d(v))
    return [o.reshape(shape) for o in outs]


WEIGHTS = ("ln_mix_a", "w_in_a", "g_v_a", "w_spatial", "b_spatial", "w_out_a", "ln_kv", "w_kv", "g_k", "ln_mix_b",
           "w_q", "g_q", "w_out_b", "ln_mlp", "w_up", "w_down", "ln_ple", "w_ple_gate", "w_ple_proj")
MATRICES = (("w_in_a", 1, True), ("w_out_a", 1, False), ("w_kv", 0, True), ("w_q", 1, False), ("w_out_b", 1, False),
            ("w_up", 2, True), ("w_down", 2, False), ("w_ple_gate", 2, False), ("w_ple_proj", 2, True))
FIRST_LAYER = ("w_in_a", "w_out_a", "w_kv", "w_up", "w_down", "w_ple_gate", "w_ple_proj")
REPLICATED = ("w_spatial", "b_spatial", "ln_kv", "g_k", "ln_mix_b", "g_q", "ln_mlp", "ln_ple")
SHARDED_VECTORS = ("ln_mix_a", "g_v_a")
SMALL_ROWS = 18


def kernel(x, p, ln_mix_a, w_in_a, g_v_a, w_spatial, b_spatial, w_out_a, ln_kv, w_kv, g_k, ln_mix_b, w_q, g_q, w_out_b, ln_mlp, w_up, w_down, ln_ple, w_ple_gate, w_ple_proj, loss_target, m_ln_mix_a, m_w_in_a, m_g_v_a, m_w_spatial, m_b_spatial, m_w_out_a, m_ln_kv, m_w_kv, m_g_k, m_ln_mix_b, m_w_q, m_g_q, m_w_out_b, m_ln_mlp, m_w_up, m_w_down, m_ln_ple, m_w_ple_gate, m_w_ple_proj, v_ln_mix_a, v_w_in_a, v_g_v_a, v_w_spatial, v_b_spatial, v_w_out_a, v_ln_kv, v_w_kv, v_g_k, v_ln_mix_b, v_w_q, v_g_q, v_w_out_b, v_ln_mlp, v_w_up, v_w_down, v_ln_ple, v_w_ple_gate, v_w_ple_proj):
    given = dict(locals())
    weights = {n: given[n] for n in WEIGHTS}
    shard = 2 * lax.axis_index("x") + lax.axis_index("y")

    leaves = []
    for name, layers, cols in MATRICES:
        w3 = weights[name] if layers else weights[name][None]
        for layer in range(max(layers, 1)):
            leaves.append((name, layer, cols, cast_bf16(w3, layer, name=f"cast_{name}_{layer}")))
    first = [lf for lf in leaves if lf[0] in FIRST_LAYER and lf[1] == 0]
    second = [lf for lf in leaves if not (lf[0] in FIRST_LAYER and lf[1] == 0)]
    got_a, vec_a = gather_shards([lf[3] for lf in first], [ln_mix_a, g_v_a], name="gather_layer0")
    got_b, _ = gather_shards([lf[3] for lf in second], [], name="gather_layer1")
    full = {}
    for (name, layer, cols, _), arr in zip(first + second, got_a + got_b):
        if not cols:
            arr = arr.reshape(N_SHARDS * arr.shape[1], arr.shape[2])
        full.setdefault(name, {})[layer] = arr
    w = {name: (tuple(full[name][l] for l in sorted(full[name])) if layers else full[name][0])
         for name, layers, _ in MATRICES}
    w["ln_mix_a"] = vec_a[0].reshape(1, D_MODEL)
    w["g_v_a"] = vec_a[1].reshape(1, D_MODEL)
    for name in REPLICATED:
        w[name] = weights[name]

    t = x.shape[1]
    loss_blk, dx, g = local_step(x[0], p.reshape(2, t, PLE_DIM), loss_target[0], w)
    loss = lax.psum(loss_blk[0, 0], ("x", "y", "c"))

    big = []
    for name, layers, cols in MATRICES:
        for layer in range(max(layers, 1)):
            arr = g[name][layer] if layers == 2 else g[name]
            if not cols:
                arr = arr.reshape(N_SHARDS, arr.shape[0] // N_SHARDS, arr.shape[1])
            big.append(arr)
    pairs = pair_exchange(big, name="grad_pair_exchange")
    wire = [add_to_wire(a, b, name=f"grad_pair_sum_{i}") for i, (a, b) in enumerate(pairs)]
    landed = chip_exchange(wire, name="grad_chip_exchange")
    totals = [sum_chips(parts, name=f"grad_chip_sum_{i}") for i, parts in enumerate(landed)]
    layout, out_shapes = [], []
    for o, (name, layers, _) in enumerate(MATRICES):
        out_shapes.append(weights[name].shape)
        for layer in range(max(layers, 1)):
            layout.append((o, (layer,) if layers else ()))
    shared = pair_share(totals, layout, out_shapes, name="grad_pair_share")
    grads = {name: shared[o] for o, (name, _, _) in enumerate(MATRICES)}

    small = REPLICATED + SHARDED_VECTORS
    flat = jnp.concatenate([g[n].reshape(-1) for n in small])
    room = 8 * SMALL_ROWS * D_MODEL
    flat = jnp.concatenate([flat, jnp.zeros((room - flat.shape[0],), F32)])
    reduced = all_reduce_small(flat.reshape(8, SMALL_ROWS, D_MODEL), name="grad_small_all_reduce").reshape(-1)
    at = 0
    for n in small:
        size = g[n].size
        piece = reduced[at:at + size]
        at += size
        if n in SHARDED_VECTORS:
            per = D_MODEL // N_SHARDS
            grads[n] = lax.dynamic_slice(piece, (shard * per,), (per,)).reshape(weights[n].shape)
        else:
            grads[n] = piece.reshape(weights[n].shape)

    delta, new_m, new_v = {}, {}, {}
    for n in WEIGHTS:
        wn, gn, mn, vn = weights[n], grads[n], given["m_" + n], given["v_" + n]
        if wn.ndim == 1:
            wn, gn, mn, vn = (a.reshape(1, -1) for a in (wn, gn, mn, vn))
        outs = adamw(wn, gn, mn, vn, name=f"adamw_{n}")
        delta[n], new_m[n], new_v[n] = (o.reshape(weights[n].shape) for o in outs)
    return (loss, dx.reshape(x.shape), *[grads[n] for n in WEIGHTS], *[delta[n] for n in WEIGHTS],
            *[new_m[n] for n in WEIGHTS], *[new_v[n] for n in WEIGHTS])
```

```python
import jax
import jax.numpy as jnp
from jax import lax
from jax.experimental import pallas as pl
from jax.experimental.pallas import tpu as pltpu

F32 = jnp.float32
BF16 = jnp.bfloat16

D_MODEL = 1024
D_FF = 4096
PLE_DIM = 256
N_GROUPS = 8
CHUNK = 128
HEAD_DIM = 64
LANES = 128
ATT_BLOCK = 256
EPS = 1e-6
N_SHARDS = 4
VMEM_LIMIT = 56 * 1024 * 1024

ADAM_LR = 0.001
ADAM_B1 = 0.9
ADAM_B2 = 0.999
ADAM_EPS = 1e-08
ADAM_WD = 0.01
ADAM_STEP = 10

MESH = pl.DeviceIdType.MESH


def _pcall(body, *, name, out_shape, grid=None, in_specs=None, out_specs=None, scratch_shapes=(),
           semantics=None, aliases=None, side_effects=False, num_prefetch=0):
    params = dict(vmem_limit_bytes=VMEM_LIMIT)
    if semantics is not None:
        params["dimension_semantics"] = semantics
    if side_effects:
        params["has_side_effects"] = True
    kwargs = {}
    if aliases:
        kwargs["input_output_aliases"] = aliases
    if num_prefetch:
        spec = pltpu.PrefetchScalarGridSpec(num_scalar_prefetch=num_prefetch, grid=grid, in_specs=in_specs,
                                            out_specs=out_specs, scratch_shapes=list(scratch_shapes))
        return pl.pallas_call(body, name=name, out_shape=out_shape, grid_spec=spec,
                              compiler_params=pltpu.CompilerParams(**params), **kwargs)
    if grid is not None:
        kwargs["grid"] = grid
    if in_specs is not None:
        kwargs["in_specs"] = in_specs
    if out_specs is not None:
        kwargs["out_specs"] = out_specs
    if aliases:
        kwargs["input_output_aliases"] = aliases
    return pl.pallas_call(body, name=name, out_shape=out_shape, scratch_shapes=list(scratch_shapes),
                          compiler_params=pltpu.CompilerParams(**params), **kwargs)


def _sds(shape, dtype):
    return jax.ShapeDtypeStruct(shape, dtype)


_GELU_C = 0.7978845608028654
_GELU_A = 0.044715


def _gelu(x):
    inner = _GELU_C * (x + _GELU_A * (x * x * x))
    return 0.5 * x * (1.0 + jnp.tanh(inner))


def _gelu_grad(x):
    x2 = x * x
    t = jnp.tanh(_GELU_C * (x + _GELU_A * (x2 * x)))
    return 0.5 * (1.0 + t) + 0.5 * x * (1.0 - t * t) * (_GELU_C * (1.0 + 3.0 * _GELU_A * x2))


def _sigmoid(x):
    return 1.0 / (1.0 + jnp.exp(-x))


def _log_sigmoid(z):
    return jnp.minimum(z, 0.0) - jnp.log(1.0 + jnp.exp(-jnp.abs(z)))


def _split_bf16(a):
    hi = a.astype(BF16)
    lo = (a - hi.astype(F32)).astype(BF16)
    return hi, lo


def _dot(a, b):
    return jnp.dot(a, b, preferred_element_type=F32)


def _dot_nt(a, b):
    return lax.dot_general(a, b, (((1,), (1,)), ((), ())), preferred_element_type=F32)


def _dot_tn(a, b):
    return lax.dot_general(a, b, (((0,), (0,)), ((), ())), preferred_element_type=F32)


def _head_rstd(x):
    lane = lax.broadcasted_iota(jnp.int32, x.shape, 1)
    low = lane < HEAD_DIM
    sq = x * x
    s_lo = jnp.sum(jnp.where(low, sq, 0.0), axis=-1, keepdims=True)
    s_hi = jnp.sum(jnp.where(low, 0.0, sq), axis=-1, keepdims=True)
    ms = jnp.where(low, s_lo, s_hi) * (1.0 / HEAD_DIM)
    return lax.rsqrt(ms + EPS)


def _head_mean(x):
    lane = lax.broadcasted_iota(jnp.int32, x.shape, 1)
    low = lane < HEAD_DIM
    s_lo = jnp.sum(jnp.where(low, x, 0.0), axis=-1, keepdims=True)
    s_hi = jnp.sum(jnp.where(low, 0.0, x), axis=-1, keepdims=True)
    return jnp.where(low, s_lo, s_hi) * (1.0 / HEAD_DIM)


def norm_matmul(x, g, w, *, name, epilogue="none", tm=1024, tn=512):
    t, d = x.shape
    if w.ndim == 3:
        per = w.shape[2]
        n = N_SHARDS * per
        tn = min(tn, per)
        w_spec = pl.BlockSpec((None, d, tn), lambda i, j: (j // (per // tn), 0, j % (per // tn)))
    else:
        n = w.shape[1]
        tn = min(tn, n)
        w_spec = pl.BlockSpec((d, tn), lambda i, j: (0, j))
    tm = min(tm, t)

    def body(x_ref, g_ref, w_ref, h_ref, r_ref, *rest):
        outs, hs = rest[:-1], rest[-1]

        @pl.when(pl.program_id(1) == 0)
        def _():
            xv = x_ref[...]
            r = lax.rsqrt(jnp.mean(xv * xv, axis=-1, keepdims=True) + EPS)
            h = ((xv * r) * g_ref[...]).astype(BF16)
            hs[...] = h
            h_ref[...] = h
            r_ref[...] = r

        y = _dot(hs[...], w_ref[...])
        if epilogue == "none":
            outs[0][...] = y
        else:
            a = jnp.maximum(y, 0.0)
            outs[0][...] = a.astype(BF16)
            outs[1][...] = (a * a).astype(BF16)

    out_shape = [_sds((t, d), BF16), _sds((t, 1), F32)]
    out_specs = [pl.BlockSpec((tm, d), lambda i, j: (i, 0)), pl.BlockSpec((tm, 1), lambda i, j: (i, 0))]
    if epilogue == "none":
        out_shape.append(_sds((t, n), F32))
        out_specs.append(pl.BlockSpec((tm, tn), lambda i, j: (i, j)))
    else:
        out_shape += [_sds((t, n), BF16), _sds((t, n), BF16)]
        out_specs += [pl.BlockSpec((tm, tn), lambda i, j: (i, j))] * 2
    return _pcall(
        body, name=name, out_shape=out_shape, grid=(t // tm, n // tn),
        in_specs=[pl.BlockSpec((tm, d), lambda i, j: (i, 0)), pl.BlockSpec((1, d), lambda i, j: (0, 0)), w_spec],
        out_specs=out_specs, scratch_shapes=[pltpu.VMEM((tm, d), BF16)],
        semantics=("parallel", "arbitrary"))(x, g, w)


def matmul_residual(a, w, res, *, name, tm=512, tn=512):
    t, k = a.shape
    n = w.shape[1]
    tm, tn = min(tm, t), min(tn, n)

    def body(a_ref, w_ref, res_ref, o_ref):
        o_ref[...] = res_ref[...] + _dot(a_ref[...], w_ref[...])

    return _pcall(
        body, name=name, out_shape=_sds((t, n), F32), grid=(t // tm, n // tn),
        in_specs=[pl.BlockSpec((tm, k), lambda i, j: (i, 0)), pl.BlockSpec((k, tn), lambda i, j: (0, j)),
                  pl.BlockSpec((tm, tn), lambda i, j: (i, j))],
        out_specs=pl.BlockSpec((tm, tn), lambda i, j: (i, j)),
        semantics=("parallel", "parallel"))(a, w, res)


def ple_forward(x, g, w_gate, p, w_proj, *, name, tm=256):
    t, d = x.shape
    tm = min(tm, t)

    def body(x_ref, g_ref, wg_ref, p_ref, wp_ref, h_ref, r_ref, gate_ref, pp_ref, o_ref):
        xv = x_ref[...]
        r = lax.rsqrt(jnp.mean(xv * xv, axis=-1, keepdims=True) + EPS)
        h = ((xv * r) * g_ref[...]).astype(BF16)
        h_ref[...] = h
        r_ref[...] = r
        gate = _sigmoid(_dot(h, wg_ref[...]))
        gate_ref[...] = gate
        pb = p_ref[...].astype(BF16)
        per = d // N_SHARDS
        for s in range(N_SHARDS):
            cols = slice(s * per, (s + 1) * per)
            pp = _dot(pb, wp_ref[s])
            pp_ref[:, cols] = pp.astype(BF16)
            o_ref[:, cols] = xv[:, cols] + pp * gate[:, cols]

    row = lambda i: (i, 0)
    fixed = lambda i: (0, 0)
    return _pcall(
        body, name=name,
        out_shape=[_sds((t, d), BF16), _sds((t, 1), F32), _sds((t, d), F32), _sds((t, d), BF16), _sds((t, d), F32)],
        grid=(t // tm,),
        in_specs=[pl.BlockSpec((tm, d), row), pl.BlockSpec((1, d), fixed), pl.BlockSpec((d, d), fixed),
                  pl.BlockSpec((tm, PLE_DIM), row),
                  pl.BlockSpec((N_SHARDS, PLE_DIM, d // N_SHARDS), lambda i: (0, 0, 0))],
        out_specs=[pl.BlockSpec((tm, d), row), pl.BlockSpec((tm, 1), row), pl.BlockSpec((tm, d), row),
                   pl.BlockSpec((tm, d), row), pl.BlockSpec((tm, d), row)],
        semantics=("parallel",))(x, g, w_gate, p, w_proj)


def _tril_mask():
    r = lax.broadcasted_iota(jnp.int32, (CHUNK, CHUNK), 0)
    c = lax.broadcasted_iota(jnp.int32, (CHUNK, CHUNK), 1)
    return c <= r


def _sgu_common(pre_ref, gv_ref, ws_ref):
    pre = pre_ref[...]
    pre_u, pre_v = pre[:, :D_MODEL], pre[:, D_MODEL:]
    u = _gelu(pre_u)
    v = _gelu(pre_v)
    r = lax.rsqrt(jnp.mean(v * v, axis=-1, keepdims=True) + EPS)
    vhat = v * r
    vn = (vhat * gv_ref[...]).astype(BF16)
    tril = _tril_mask()
    wm = [jnp.where(tril, ws_ref[g], 0.0).astype(BF16) for g in range(N_GROUPS)]
    return pre_u, pre_v, u, r, vhat, vn, wm, tril


def sgu_forward(pre, g_v, w_s, b_full, *, name):
    t = pre.shape[0]

    def body(pre_ref, gv_ref, ws_ref, b_ref, y_ref):
        _, _, u, _, _, vn, wm, _ = _sgu_common(pre_ref, gv_ref, ws_ref)
        for g in range(N_GROUPS):
            cols = slice(g * LANES, (g + 1) * LANES)
            mix = _dot(wm[g], vn[:, cols]) + b_ref[:, cols]
            y_ref[:, cols] = (u[:, cols] * mix).astype(BF16)

    return _pcall(
        body, name=name, out_shape=_sds((t, D_MODEL), BF16), grid=(t // CHUNK,),
        in_specs=[pl.BlockSpec((CHUNK, 2 * D_MODEL), lambda i: (i, 0)), pl.BlockSpec((1, D_MODEL), lambda i: (0, 0)),
                  pl.BlockSpec((N_GROUPS, CHUNK, CHUNK), lambda i: (0, 0, 0)),
                  pl.BlockSpec((CHUNK, D_MODEL), lambda i: (0, 0))],
        out_specs=pl.BlockSpec((CHUNK, D_MODEL), lambda i: (i, 0)),
        semantics=("parallel",))(pre, g_v, w_s, b_full)


def head_norm(pre, g128, *, name, col_block=0, scale=1.0, passthrough=False, tm=512):
    t = pre.shape[0]
    tm = min(tm, t)

    def body(*refs):
        if passthrough:
            x_ref, v_ref, g_ref, o_ref, vo_ref = refs
            vo_ref[...] = v_ref[...].astype(BF16)
        else:
            x_ref, g_ref, o_ref = refs
        g = g_ref[...] * scale
        for b in range(D_MODEL // LANES):
            cols = slice(b * LANES, (b + 1) * LANES)
            xv = x_ref[:, cols]
            o_ref[:, cols] = ((xv * _head_rstd(xv)) * g).astype(BF16)

    x_spec = pl.BlockSpec((tm, D_MODEL), lambda i: (i, col_block))
    g_spec = pl.BlockSpec((1, LANES), lambda i: (0, 0))
    o_spec = pl.BlockSpec((tm, D_MODEL), lambda i: (i, 0))
    if passthrough:
        return _pcall(body, name=name, out_shape=[_sds((t, D_MODEL), BF16)] * 2, grid=(t // tm,),
                      in_specs=[x_spec, pl.BlockSpec((tm, D_MODEL), lambda i: (i, 1)), g_spec],
                      out_specs=[o_spec, o_spec], semantics=("parallel",))(pre, pre, g128)
    return _pcall(body, name=name, out_shape=_sds((t, D_MODEL), BF16), grid=(t // tm,),
                  in_specs=[x_spec, g_spec], out_specs=o_spec, semantics=("parallel",))(pre, g128)


def _suffix_matrix(n):
    r = lax.broadcasted_iota(jnp.int32, (n, n), 0)
    c = lax.broadcasted_iota(jnp.int32, (n, n), 1)
    return jnp.where(r > c, 1.0, 0.0).astype(BF16)


def _prefix_matrix(n):
    r = lax.broadcasted_iota(jnp.int32, (n, n), 0)
    c = lax.broadcasted_iota(jnp.int32, (n, n), 1)
    return jnp.where(r < c, 1.0, 0.0).astype(BF16)


def _exact_cumsum(a, tri):
    rows = a.shape[0]
    hi, lo = _split_bf16(a)
    both = _dot(jnp.concatenate([hi, lo], axis=0), tri)
    return both[:rows] + both[rows:]


def _stacked_causal(n):
    r = lax.broadcasted_iota(jnp.int32, (2 * n, n), 0)
    c = lax.broadcasted_iota(jnp.int32, (2 * n, n), 1)
    return c < jnp.where(r >= n, r - n, r)


def _stack_heads(a, low):
    zero = jnp.zeros_like(a)
    return jnp.concatenate([jnp.where(low, a, zero), jnp.where(low, zero, a)], axis=0)


def stick_breaking_forward(q, k, v, *, name):
    t = q.shape[0]
    blk = min(ATT_BLOCK, t)
    nq = t // blk

    def body(q_ref, k_ref, v_ref, o_ref):
        i = pl.program_id(1)
        low = lax.broadcasted_iota(jnp.int32, (blk, LANES), 1) < HEAD_DIM
        tri = _suffix_matrix(blk)
        causal = _stacked_causal(blk)
        qs = _stack_heads(q_ref[...], low)

        def block(j, carry, acc, masked):
            rows = pl.ds(pl.multiple_of(j * blk, blk), blk)
            z = _dot_nt(qs, k_ref[rows, :])
            ls = _log_sigmoid(z)
            lg = ls - z
            if masked:
                lg = jnp.where(causal, lg, 0.0)
            s = ls + _exact_cumsum(lg, tri) + carry
            a = jnp.exp(s)
            if masked:
                a = jnp.where(causal, a, 0.0)
            acc = acc + _dot(a.astype(BF16), v_ref[rows, :])
            return carry + jnp.sum(lg, axis=-1, keepdims=True), acc

        carry, acc = block(i, jnp.zeros((2 * blk, 1), F32), jnp.zeros((2 * blk, LANES), F32), True)
        _, acc = lax.fori_loop(0, i, lambda n, st: block(i - 1 - n, st[0], st[1], False), (carry, acc))
        o_ref[...] = jnp.where(low, acc[:blk], acc[blk:]).astype(BF16)

    return _pcall(
        body, name=name, out_shape=_sds((t, D_MODEL), BF16), grid=(D_MODEL // LANES, nq),
        in_specs=[pl.BlockSpec((blk, LANES), lambda p, i: (i, p)), pl.BlockSpec((t, LANES), lambda p, i: (0, p)),
                  pl.BlockSpec((t, LANES), lambda p, i: (0, p))],
        out_specs=pl.BlockSpec((blk, LANES), lambda p, i: (i, p)),
        semantics=("parallel", "arbitrary"))(q, k, v)


def loss_forward(x, target, *, name, tm=512):
    t, d = x.shape
    tm = min(tm, t)

    def body(x_ref, t_ref, l_ref, dx_ref):
        @pl.when(pl.program_id(0) == 0)
        def _():
            l_ref[...] = jnp.zeros_like(l_ref)

        diff = x_ref[...] - t_ref[...]
        dx_ref[...] = diff * (1.0 / d)
        l_ref[...] += 0.5 * jnp.sum(jnp.mean(diff * diff, axis=-1, keepdims=True))

    return _pcall(
        body, name=name, out_shape=[_sds((8, LANES), F32), _sds((t, d), F32)], grid=(t // tm,),
        in_specs=[pl.BlockSpec((tm, d), lambda i: (i, 0))] * 2,
        out_specs=[pl.BlockSpec((8, LANES), lambda i: (0, 0)), pl.BlockSpec((tm, d), lambda i: (i, 0))],
        semantics=("arbitrary",))(x, target)


def matmul_nt(dy, w, *, name, mul=None, out_dtype=F32, tm=512, tk=512):
    t, n = dy.shape
    k = w.shape[0]
    tm, tk = min(tm, t), min(tk, k)

    def body(*refs):
        if mul is None:
            dy_ref, w_ref, o_ref = refs
        else:
            dy_ref, w_ref, m_ref, o_ref = refs
        y = _dot_nt(dy_ref[...].astype(BF16), w_ref[...])
        if mul is not None:
            y = y * (2.0 * m_ref[...].astype(F32))
        o_ref[...] = y.astype(out_dtype)

    in_specs = [pl.BlockSpec((tm, n), lambda i, j: (i, 0)), pl.BlockSpec((tk, n), lambda i, j: (j, 0))]
    args = [dy, w]
    if mul is not None:
        in_specs.append(pl.BlockSpec((tm, tk), lambda i, j: (i, j)))
        args.append(mul)
    return _pcall(body, name=name, out_shape=_sds((t, k), out_dtype), grid=(t // tm, k // tk), in_specs=in_specs,
                  out_specs=pl.BlockSpec((tm, tk), lambda i, j: (i, j)), semantics=("parallel", "parallel"))(*args)


def matmul_tn(a, dy, *, name, col_shards, tk=512):
    t, k = a.shape
    n = dy.shape[1]
    tk = min(tk, k)
    if col_shards:
        tn = n // N_SHARDS
        out_shape = _sds((N_SHARDS, k, tn), F32)
        out_spec = pl.BlockSpec((None, tk, tn), lambda i, j: (j, i, 0))
    else:
        tn = min(512, n)
        out_shape = _sds((k, n), F32)
        out_spec = pl.BlockSpec((tk, tn), lambda i, j: (i, j))

    def body(a_ref, dy_ref, o_ref):
        o_ref[...] = _dot_tn(a_ref[...].astype(BF16), dy_ref[...].astype(BF16))

    return _pcall(body, name=name, out_shape=out_shape, grid=(k // tk, n // tn),
                  in_specs=[pl.BlockSpec((t, tk), lambda i, j: (0, i)), pl.BlockSpec((t, tn), lambda i, j: (0, j))],
                  out_specs=out_spec, semantics=("parallel", "parallel"))(a, dy)


def norm_backward(dpre, w, x, g, rstd, dx_out, *, name, tm=256):
    t, d = x.shape
    n = dpre.shape[1]
    tm = min(tm, t)
    if w.ndim == 3:
        w_spec = pl.BlockSpec(w.shape, lambda i: (0, 0, 0))
    else:
        w_spec = pl.BlockSpec(w.shape, lambda i: (0, 0))

    def body(dp_ref, w_ref, x_ref, g_ref, r_ref, dxo_ref, dx_ref, dg_ref):
        @pl.when(pl.program_id(0) == 0)
        def _():
            dg_ref[...] = jnp.zeros_like(dg_ref)

        if w.ndim == 3:
            per = n // N_SHARDS
            dh = _dot_nt(dp_ref[:, 0:per], w_ref[0])
            for s in range(1, N_SHARDS):
                dh = dh + _dot_nt(dp_ref[:, s * per:(s + 1) * per], w_ref[s])
        else:
            dh = _dot_nt(dp_ref[...], w_ref[...])
        r = r_ref[...]
        xn = x_ref[...] * r
        dg_ref[...] += jnp.sum(dh * xn, axis=0, keepdims=True)
        dxn = dh * g_ref[...]
        dx = r * (dxn - xn * jnp.mean(dxn * xn, axis=-1, keepdims=True))
        dx_ref[...] = dxo_ref[...] + dx

    row = lambda i: (i, 0)
    fixed = lambda i: (0, 0)
    return _pcall(
        body, name=name, out_shape=[_sds((t, d), F32), _sds((1, d), F32)], grid=(t // tm,),
        in_specs=[pl.BlockSpec((tm, n), row), w_spec, pl.BlockSpec((tm, d), row),
                  pl.BlockSpec((1, d), fixed), pl.BlockSpec((tm, 1), row), pl.BlockSpec((tm, d), row)],
        out_specs=[pl.BlockSpec((tm, d), row), pl.BlockSpec((1, d), fixed)],
        semantics=("arbitrary",))(dpre, w, x, g, rstd, dx_out)


def ple_backward(dx, gate, pp, *, name, tm=512):
    t, d = dx.shape
    tm = min(tm, t)

    def body(dx_ref, gate_ref, pp_ref, dg_ref, dp_ref):
        dxv = dx_ref[...]
        gate = gate_ref[...]
        dg_ref[...] = (dxv * pp_ref[...].astype(F32) * (gate * (1.0 - gate))).astype(BF16)
        dp_ref[...] = (dxv * gate).astype(BF16)

    spec = pl.BlockSpec((tm, d), lambda i: (i, 0))
    return _pcall(body, name=name, out_shape=[_sds((t, d), BF16)] * 2, grid=(t // tm,), in_specs=[spec] * 3,
                  out_specs=[spec] * 2, semantics=("parallel",))(dx, gate, pp)


def sgu_backward(dy, pre, g_v, w_s, b_full, *, name):
    t = pre.shape[0]
    n_chunks = t // CHUNK

    def body(dy_ref, pre_ref, gv_ref, ws_ref, b_ref, dpre_ref, dws_ref, db_ref, dgv_ref, dvn_s, dbf_s):
        step = pl.program_id(0)

        @pl.when(step == 0)
        def _():
            dws_ref[...] = jnp.zeros_like(dws_ref)
            dgv_ref[...] = jnp.zeros_like(dgv_ref)
            dbf_s[...] = jnp.zeros_like(dbf_s)

        pre_u, pre_v, u, r, vhat, vn, wm, tril = _sgu_common(pre_ref, gv_ref, ws_ref)
        dyv = dy_ref[...]
        for g in range(N_GROUPS):
            cols = slice(g * LANES, (g + 1) * LANES)
            mix = _dot(wm[g], vn[:, cols]) + b_ref[:, cols]
            dmix = dyv[:, cols] * u[:, cols]
            dmix_b = dmix.astype(BF16)
            du = dyv[:, cols] * mix
            dpre_ref[:, cols] = (du * _gelu_grad(pre_u[:, cols])).astype(BF16)
            dws_ref[g] += jnp.where(tril, _dot_nt(dmix_b, vn[:, cols]), 0.0)
            dbf_s[:, cols] += dmix
            dvn_s[:, cols] = _dot_tn(wm[g], dmix_b)
        dvn = dvn_s[...]
        dgv_ref[...] += jnp.sum(dvn * vhat, axis=0, keepdims=True)
        dxn = dvn * gv_ref[...]
        dv = r * (dxn - vhat * jnp.mean(dxn * vhat, axis=-1, keepdims=True))
        dpre_ref[:, D_MODEL:] = (dv * _gelu_grad(pre_v)).astype(BF16)

        @pl.when(step == n_chunks - 1)
        def _():
            lane = lax.broadcasted_iota(jnp.int32, (CHUNK, LANES), 1)
            acc = jnp.zeros((CHUNK, LANES), F32)
            for g in range(N_GROUPS):
                s = jnp.sum(dbf_s[:, g * LANES:(g + 1) * LANES], axis=-1, keepdims=True)
                acc = jnp.where(lane == g, s, acc)
            db_ref[...] = acc

    fixed2 = lambda i: (0, 0)
    return _pcall(
        body, name=name,
        out_shape=[_sds((t, 2 * D_MODEL), BF16), _sds((N_GROUPS, CHUNK, CHUNK), F32), _sds((CHUNK, LANES), F32),
                   _sds((1, D_MODEL), F32)],
        grid=(n_chunks,),
        in_specs=[pl.BlockSpec((CHUNK, D_MODEL), lambda i: (i, 0)), pl.BlockSpec((CHUNK, 2 * D_MODEL), lambda i: (i, 0)),
                  pl.BlockSpec((1, D_MODEL), fixed2), pl.BlockSpec((N_GROUPS, CHUNK, CHUNK), lambda i: (0, 0, 0)),
                  pl.BlockSpec((CHUNK, D_MODEL), fixed2)],
        out_specs=[pl.BlockSpec((CHUNK, 2 * D_MODEL), lambda i: (i, 0)),
                   pl.BlockSpec((N_GROUPS, CHUNK, CHUNK), lambda i: (0, 0, 0)), pl.BlockSpec((CHUNK, LANES), fixed2),
                   pl.BlockSpec((1, D_MODEL), fixed2)],
        scratch_shapes=[pltpu.VMEM((CHUNK, D_MODEL), F32), pltpu.VMEM((CHUNK, D_MODEL), F32)],
        semantics=("arbitrary",))(dy, pre, g_v, w_s, b_full)


def head_norm_backward(dy, pre, g128, *, name, col_block=0, scale=1.0, passthrough=None, tm=512):
    t = dy.shape[0]
    tm = min(tm, t)
    width = 2 * D_MODEL if passthrough is not None else D_MODEL

    def body(*refs):
        if passthrough is not None:
            dy_ref, x_ref, g_ref, dv_ref, o_ref, dg_ref = refs
            o_ref[:, D_MODEL:] = dv_ref[...].astype(BF16)
        else:
            dy_ref, x_ref, g_ref, o_ref, dg_ref = refs

        @pl.when(pl.program_id(0) == 0)
        def _():
            dg_ref[...] = jnp.zeros_like(dg_ref)

        g = g_ref[...]
        dg = jnp.zeros((1, LANES), F32)
        for b in range(D_MODEL // LANES):
            cols = slice(b * LANES, (b + 1) * LANES)
            xv = x_ref[:, cols]
            r = _head_rstd(xv)
            xn = xv * r
            dyv = dy_ref[:, cols] * scale
            dg = dg + jnp.sum(dyv * xn, axis=0, keepdims=True)
            dxn = dyv * g
            o_ref[:, cols] = (r * (dxn - xn * _head_mean(dxn * xn))).astype(BF16)
        dg_ref[...] += dg

    row = lambda i: (i, 0)
    in_specs = [pl.BlockSpec((tm, D_MODEL), row), pl.BlockSpec((tm, D_MODEL), lambda i: (i, col_block)),
                pl.BlockSpec((1, LANES), lambda i: (0, 0))]
    args = [dy, pre, g128]
    if passthrough is not None:
        in_specs.append(pl.BlockSpec((tm, D_MODEL), row))
        args.append(passthrough)
    return _pcall(body, name=name, out_shape=[_sds((t, width), BF16), _sds((1, LANES), F32)], grid=(t // tm,),
                  in_specs=in_specs,
                  out_specs=[pl.BlockSpec((tm, width), row), pl.BlockSpec((1, LANES), lambda i: (0, 0))],
                  semantics=("arbitrary",))(*args)


def stick_breaking_backward(q, k, v, do, *, name):
    t = q.shape[0]
    blk = min(ATT_BLOCK, t)
    nq = t // blk

    def body(q_ref, k_ref, v_ref, do_ref, dq_ref, dk_ref, dv_ref, s_buf, sg_buf):
        i = pl.program_id(1)

        @pl.when(i == 0)
        def _():
            dk_ref[...] = jnp.zeros_like(dk_ref)
            dv_ref[...] = jnp.zeros_like(dv_ref)

        low = lax.broadcasted_iota(jnp.int32, (blk, LANES), 1) < HEAD_DIM
        suffix = _suffix_matrix(blk)
        prefix = _prefix_matrix(blk)
        causal = _stacked_causal(blk)
        qs = _stack_heads(q_ref[...], low)
        dos = _stack_heads(do_ref[...], low)

        def log_weights(j, carry, masked):
            rows = pl.ds(pl.multiple_of(j * blk, blk), blk)
            z = _dot_nt(qs, k_ref[rows, :])
            ls = _log_sigmoid(z)
            lg = ls - z
            if masked:
                lg = jnp.where(causal, lg, 0.0)
            s_buf[j] = ls + _exact_cumsum(lg, suffix) + carry
            sg_buf[j] = jnp.exp(ls)
            return carry + jnp.sum(lg, axis=-1, keepdims=True)

        carry = log_weights(i, jnp.zeros((2 * blk, 1), F32), True)
        lax.fori_loop(0, i, lambda n, c: log_weights(i - 1 - n, c, False), carry)

        def grads(j, pcarry, dq_acc, masked):
            rows = pl.ds(pl.multiple_of(j * blk, blk), blk)
            a = jnp.exp(s_buf[j])
            if masked:
                a = jnp.where(causal, a, 0.0)
            sg = sg_buf[j]
            ds = _dot_nt(dos, v_ref[rows, :]) * a
            before = _exact_cumsum(ds, prefix) + pcarry
            if masked:
                before = jnp.where(causal, before, 0.0)
            dz = (ds - sg * (ds + before)).astype(BF16)
            dq_acc = dq_acc + _dot(dz, k_ref[rows, :])
            dk_ref[rows, :] += _dot_tn(dz, qs)
            dv_ref[rows, :] += _dot_tn(a.astype(BF16), dos)
            return pcarry + jnp.sum(ds, axis=-1, keepdims=True), dq_acc

        state = lax.fori_loop(0, i, lambda j, st: grads(j, st[0], st[1], False),
                              (jnp.zeros((2 * blk, 1), F32), jnp.zeros((2 * blk, LANES), F32)))
        _, dq_acc = grads(i, state[0], state[1], True)
        dq_ref[...] = jnp.where(low, dq_acc[:blk], dq_acc[blk:])

    full = pl.BlockSpec((t, LANES), lambda p, i: (0, p))
    qblk = pl.BlockSpec((blk, LANES), lambda p, i: (i, p))
    return _pcall(
        body, name=name, out_shape=[_sds((t, D_MODEL), F32)] * 3, grid=(D_MODEL // LANES, nq),
        in_specs=[qblk, full, full, qblk], out_specs=[qblk, full, full],
        scratch_shapes=[pltpu.VMEM((nq, 2 * blk, blk), F32), pltpu.VMEM((nq, 2 * blk, blk), F32)],
        semantics=("parallel", "arbitrary"))(q, k, v, do)


def _mlp_forward(x, g, w_up, w_down, tag):
    h, r, a, a2 = norm_matmul(x, g, w_up, name=f"mlp_up_{tag}", epilogue="relu2")
    return matmul_residual(a2, w_down, x, name=f"mlp_down_{tag}"), (x, h, r, a, a2)


def _mlp_backward(dx, saved, g, w_up, w_down, tag):
    x, h, r, a, a2 = saved
    d_w_down = matmul_tn(a2, dx, name=f"d_w_down_{tag}", col_shards=False)
    dpre = matmul_nt(dx, w_down, name=f"d_mlp_act_{tag}", mul=a, out_dtype=BF16)
    d_w_up = matmul_tn(h, dpre, name=f"d_w_up_{tag}", col_shards=True)
    dx, d_g = norm_backward(dpre, w_up, x, g, r, dx, name=f"d_mlp_norm_{tag}")
    return dx, d_w_up, d_w_down, d_g


def _ple_backward(dx, saved, p, g, w_gate, tag):
    x, h, r, gate, pp = saved
    dgate, dproj = ple_backward(dx, gate, pp, name=f"d_ple_{tag}")
    d_w_proj = matmul_tn(p, dproj, name=f"d_w_ple_proj_{tag}", col_shards=True)
    d_w_gate = matmul_tn(h, dgate, name=f"d_w_ple_gate_{tag}", col_shards=False)
    dx, d_g = norm_backward(dgate, w_gate, x, g, r, dx, name=f"d_ple_norm_{tag}")
    return dx, d_w_gate, d_w_proj, d_g


def local_step(x, p, target, w):
    row = lambda v: v.reshape(1, -1)
    g128 = lambda v: jnp.tile(v.reshape(1, HEAD_DIM), (1, 2))
    scale = HEAD_DIM ** -0.5
    b_full = jnp.repeat(jnp.transpose(w["b_spatial"][0]), LANES, axis=1)
    w_s = w["w_spatial"][0]

    x0 = x
    h_a, r_a, pre_a = norm_matmul(x0, row(w["ln_mix_a"][0]), w["w_in_a"][0], name="sgu_in")
    y_a = sgu_forward(pre_a, row(w["g_v_a"][0]), w_s, b_full, name="sgu_mix")
    x1 = matmul_residual(y_a, w["w_out_a"][0], x0, name="sgu_out")
    x2, mlp0 = _mlp_forward(x1, row(w["ln_mlp"][0]), w["w_up"][0], w["w_down"][0], 0)
    ple0 = ple_forward(x2, row(w["ln_ple"][0]), w["w_ple_gate"][0], p[0], w["w_ple_proj"][0], name="ple_0")
    x3 = ple0[4]
    h_kv, r_kv, kv_pre = norm_matmul(x3, row(w["ln_kv"]), w["w_kv"], name="kv_proj")
    k_n, v_b = head_norm(kv_pre, g128(w["g_k"]), name="k_norm", passthrough=True)
    h_q, r_q, q_pre = norm_matmul(x3, row(w["ln_mix_b"][0]), w["w_q"][0], name="q_proj")
    q_n = head_norm(q_pre, g128(w["g_q"][0]), name="q_norm", scale=scale)
    o = stick_breaking_forward(q_n, k_n, v_b, name="sb_fwd")
    x4 = matmul_residual(o, w["w_out_b"][0], x3, name="sb_out")
    x5, mlp1 = _mlp_forward(x4, row(w["ln_mlp"][1]), w["w_up"][1], w["w_down"][1], 1)
    ple1 = ple_forward(x5, row(w["ln_ple"][1]), w["w_ple_gate"][1], p[1], w["w_ple_proj"][1], name="ple_1")
    x6 = ple1[4]
    loss_blk, dx = loss_forward(x6, target, name="loss")

    g = {}
    dx, dwg1, dwp1, dlnp1 = _ple_backward(dx, (x5,) + tuple(ple1[:4]), p[1], row(w["ln_ple"][1]), w["w_ple_gate"][1], 1)
    dx, dwu1, dwd1, dlnm1 = _mlp_backward(dx, mlp1, row(w["ln_mlp"][1]), w["w_up"][1], w["w_down"][1], 1)
    g["w_out_b"] = matmul_tn(o, dx, name="d_w_out_b", col_shards=False)
    do = matmul_nt(dx, w["w_out_b"][0], name="d_sb_out", out_dtype=BF16)
    dq_n, dk_n, dv = stick_breaking_backward(q_n, k_n, v_b, do, name="sb_bwd")
    dq_pre, dgq = head_norm_backward(dq_n, q_pre, g128(w["g_q"][0]), name="d_q_norm", scale=scale)
    dkv_pre, dgk = head_norm_backward(dk_n, kv_pre, g128(w["g_k"]), name="d_k_norm", passthrough=dv)
    g["w_q"] = matmul_tn(h_q, dq_pre, name="d_w_q", col_shards=False)
    g["w_kv"] = matmul_tn(h_kv, dkv_pre, name="d_w_kv", col_shards=True)
    dx, g["ln_mix_b"] = norm_backward(dq_pre, w["w_q"][0], x3, row(w["ln_mix_b"][0]), r_q, dx, name="d_q_in")
    dx, g["ln_kv"] = norm_backward(dkv_pre, w["w_kv"], x3, row(w["ln_kv"]), r_kv, dx, name="d_kv_in")
    g["g_q"] = dgq[:, :HEAD_DIM] + dgq[:, HEAD_DIM:]
    g["g_k"] = (dgk[:, :HEAD_DIM] + dgk[:, HEAD_DIM:]).reshape(HEAD_DIM)
    g["ln_kv"] = g["ln_kv"].reshape(D_MODEL)
    dx, dwg0, dwp0, dlnp0 = _ple_backward(dx, (x2,) + tuple(ple0[:4]), p[0], row(w["ln_ple"][0]), w["w_ple_gate"][0], 0)
    dx, dwu0, dwd0, dlnm0 = _mlp_backward(dx, mlp0, row(w["ln_mlp"][0]), w["w_up"][0], w["w_down"][0], 0)
    g["w_out_a"] = matmul_tn(y_a, dx, name="d_w_out_a", col_shards=False)
    dy_a = matmul_nt(dx, w["w_out_a"][0], name="d_sgu_out")
    dpre_a, dws, db, g["g_v_a"] = sgu_backward(dy_a, pre_a, row(w["g_v_a"][0]), w_s, b_full, name="d_sgu_mix")
    g["w_in_a"] = matmul_tn(h_a, dpre_a, name="d_w_in_a", col_shards=True)
    dx, g["ln_mix_a"] = norm_backward(dpre_a, w["w_in_a"][0], x0, row(w["ln_mix_a"][0]), r_a, dx, name="d_sgu_in")
    g["w_spatial"] = dws[None]
    g["b_spatial"] = jnp.transpose(db[:, :N_GROUPS])[None]
    g["w_up"] = (dwu0, dwu1)
    g["w_down"] = (dwd0, dwd1)
    g["w_ple_gate"] = (dwg0, dwg1)
    g["w_ple_proj"] = (dwp0, dwp1)
    g["ln_mlp"] = jnp.concatenate([dlnm0, dlnm1], axis=0)
    g["ln_ple"] = jnp.concatenate([dlnp0, dlnp1], axis=0)
    return loss_blk, dx, g


ANY = pl.BlockSpec(memory_space=pl.ANY)


def _place():
    x, y, c = lax.axis_index("x"), lax.axis_index("y"), lax.axis_index("c")
    others = [(1 - x, y), (x, 1 - y), (1 - x, 1 - y)]
    return x, y, c, 2 * x + y, others


def cast_into_slot(w3, layer, slot, *, name, tm=256):
    _, r, c = w3.shape
    tm = min(tm, r)

    def body(slot_ref, w_ref, o_ref):
        o_ref[...] = w_ref[...].astype(BF16)

    return _pcall(body, name=name, out_shape=_sds((N_SHARDS, r, c), BF16), grid=(r // tm,), num_prefetch=1,
                  in_specs=[pl.BlockSpec((None, tm, c), lambda i, s: (layer, i, 0))],
                  out_specs=pl.BlockSpec((None, tm, c), lambda i, s: (s[0], i, 0)),
                  semantics=("parallel",))(slot, w3)


def gather_shards(mats, vecs, *, name):
    nm, nv = len(mats), len(vecs)
    halves = [m.reshape(N_SHARDS, 2, m.shape[1] // 2, m.shape[2]) for m in mats]

    def body(*refs):
        vsrc = refs[nm:nm + nv]
        out, vout = refs[nm + nv:2 * nm + nv], refs[2 * nm + nv:2 * (nm + nv)]
        send, recv, vsend, vrecv, loc = refs[2 * (nm + nv):]
        x, y, c, s_me, others = _place()
        sib = (x, y, 1 - c)

        def ici(l, k):
            ox, oy = others[k]
            return pltpu.make_async_remote_copy(out[l].at[s_me, c], out[l].at[s_me, c], send.at[l, k], recv.at[l, k],
                                                device_id=(ox, oy, c), device_id_type=MESH)

        def landed(l, k, half):
            ox, oy = others[k]
            return out[l].at[2 * ox + oy, half]

        def passed_on(l, k):
            return pltpu.make_async_remote_copy(landed(l, k, c), landed(l, k, c), send.at[l, 3 + k], recv.at[l, 3 + k],
                                                device_id=sib, device_id_type=MESH)

        def vec(l, k):
            ox, oy = others[k]
            return pltpu.make_async_remote_copy(vsrc[l], vout[l].at[s_me], vsend.at[l, k], vrecv.at[l, k],
                                                device_id=(ox, oy, c), device_id_type=MESH)

        for l in range(nm):
            for k in range(3):
                ici(l, k).start()
        for l in range(nv):
            for k in range(3):
                vec(l, k).start()
        for l in range(nv):
            own = pltpu.make_async_copy(vsrc[l], vout[l].at[s_me], loc)
            own.start()
            own.wait()
        for l in range(nm):
            for k in range(3):
                pltpu.make_async_remote_copy(landed(l, k, c), landed(l, k, c), send.at[l, k], recv.at[l, k],
                                             device_id=sib, device_id_type=MESH).wait_recv()
                passed_on(l, k).start()
        for l in range(nm):
            for k in range(3):
                pltpu.make_async_remote_copy(landed(l, k, 1 - c), landed(l, k, 1 - c), send.at[l, 3 + k],
                                             recv.at[l, 3 + k], device_id=sib, device_id_type=MESH).wait_recv()
        for l in range(nv):
            for k in range(3):
                ox, oy = others[k]
                pltpu.make_async_remote_copy(vsrc[l], vout[l].at[2 * ox + oy], vsend.at[l, k], vrecv.at[l, k],
                                             device_id=sib, device_id_type=MESH).wait_recv()
        for l in range(nm):
            for k in range(3):
                ici(l, k).wait_send()
                passed_on(l, k).wait_send()
        for l in range(nv):
            for k in range(3):
                vec(l, k).wait_send()

    out_shape = [_sds(h.shape, BF16) for h in halves] + [_sds((N_SHARDS,) + v.shape, F32) for v in vecs]
    res = _pcall(body, name=name, out_shape=out_shape, in_specs=[ANY] * (nm + nv), out_specs=[ANY] * (nm + nv),
                 scratch_shapes=[pltpu.SemaphoreType.DMA((nm, 6)), pltpu.SemaphoreType.DMA((nm, 6)),
                                 pltpu.SemaphoreType.DMA((max(nv, 1), 3)), pltpu.SemaphoreType.DMA((max(nv, 1), 3)),
                                 pltpu.SemaphoreType.DMA(())],
                 aliases={l: l for l in range(nm)}, side_effects=True)(*halves, *vecs)
    return [r.reshape(m.shape) for r, m in zip(res[:nm], mats)], list(res[nm:])


def pair_exchange(grads, *, name):
    n = len(grads)

    def body(*refs):
        src, got = refs[:n], refs[n:2 * n]
        send, recv = refs[2 * n:]
        x, y, c, _, _ = _place()

        def swap(l):
            return pltpu.make_async_remote_copy(src[l].at[:, 1 - c], got[l], send.at[l], recv.at[l],
                                                device_id=(x, y, 1 - c), device_id_type=MESH)

        for l in range(n):
            swap(l).start()
        for l in range(n):
            swap(l).wait()

    res = _pcall(body, name=name, out_shape=[_sds((N_SHARDS,) + g.shape[2:], F32) for g in grads],
                 in_specs=[ANY] * n, out_specs=[ANY] * n,
                 scratch_shapes=[pltpu.SemaphoreType.DMA((n,)), pltpu.SemaphoreType.DMA((n,))],
                 side_effects=True)(*grads)
    return list(res)


def add_to_wire(mine, theirs, core, *, name, tm=256):
    s, _, r, c = mine.shape
    tm = min(tm, r)

    def body(core_ref, a_ref, b_ref, o_ref):
        o_ref[...] = (a_ref[...] + b_ref[...]).astype(BF16)

    spec = pl.BlockSpec((None, tm, c), lambda i, j, cr: (i, j, 0))
    return _pcall(body, name=name, out_shape=_sds((s, r, c), BF16), grid=(s, r // tm), num_prefetch=1,
                  in_specs=[pl.BlockSpec((None, None, tm, c), lambda i, j, cr: (i, cr[0], j, 0)), spec],
                  out_specs=spec, semantics=("parallel", "parallel"))(core, mine, theirs)


def chip_exchange(parts, *, name):
    n = len(parts)

    def body(*refs):
        src, out = refs[:n], refs[n:2 * n]
        send, recv = refs[2 * n:]
        x, y, c, s_me, others = _place()

        def ici(l, k):
            ox, oy = others[k]
            return pltpu.make_async_remote_copy(src[l].at[2 * ox + oy], out[l].at[k], send.at[l, k], recv.at[l, k],
                                                device_id=(ox, oy, c), device_id_type=MESH)

        for l in range(n):
            for k in range(3):
                ici(l, k).start()
        for l in range(n):
            for k in range(3):
                ici(l, k).wait()

    res = _pcall(body, name=name, out_shape=[_sds((3,) + p.shape[1:], p.dtype) for p in parts], in_specs=[ANY] * n,
                 out_specs=[ANY] * n,
                 scratch_shapes=[pltpu.SemaphoreType.DMA((n, 3)), pltpu.SemaphoreType.DMA((n, 3))],
                 side_effects=True)(*parts)
    return list(res)


def sum_chips(wire, landed, place, dest, layer, n_layers, *, name, tm=256):
    _, r, c = wire.shape
    tm = min(tm, r)

    def body(place_ref, w_ref, l_ref, *rest):
        o_ref = rest[-1]
        o_ref[...] = ((w_ref[...].astype(F32) + l_ref[0].astype(F32)) + l_ref[1].astype(F32)) + l_ref[2].astype(F32)

    in_specs = [pl.BlockSpec((None, tm, c), lambda i, pr: (pr[0], i, 0)),
                pl.BlockSpec((3, tm, c), lambda i, pr: (0, i, 0))]
    args = [place, wire, landed]
    aliases = None
    if dest is not None:
        in_specs.append(ANY)
        args.append(dest)
        aliases = {3: 0}
    return _pcall(body, name=name, out_shape=_sds((n_layers, 2, r, c), F32), grid=(r // tm,), num_prefetch=1,
                  in_specs=in_specs,
                  out_specs=pl.BlockSpec((None, None, tm, c), lambda i, pr: (layer, pr[1], i, 0)),
                  aliases=aliases, semantics=("parallel",))(*args)


def pair_share(bufs, *, name):
    n = len(bufs)
    slots = [(o, l) for o, b in enumerate(bufs) for l in range(b.shape[0])]

    def body(*refs):
        out = refs[n:2 * n]
        send, recv = refs[2 * n:]
        x, y, c, _, _ = _place()

        def share(i, half):
            o, l = slots[i]
            return pltpu.make_async_remote_copy(out[o].at[l, half], out[o].at[l, half], send.at[i], recv.at[i],
                                                device_id=(x, y, 1 - c), device_id_type=MESH)

        for i in range(len(slots)):
            share(i, c).start()
        for i in range(len(slots)):
            share(i, 1 - c).wait_recv()
            share(i, c).wait_send()

    res = _pcall(body, name=name, out_shape=[_sds(b.shape, F32) for b in bufs], in_specs=[ANY] * n,
                 out_specs=[ANY] * n,
                 scratch_shapes=[pltpu.SemaphoreType.DMA((len(slots),)), pltpu.SemaphoreType.DMA((len(slots),))],
                 aliases={o: o for o in range(n)}, side_effects=True)(*bufs)
    return list(res)


def all_reduce_small(packed, *, name):
    n_dev, r, c = packed.shape

    def body(in_ref, out_ref, land, send, recv):
        x, y, cc, _, _ = _place()
        me = 4 * x + 2 * y + cc
        peers = [(px, py, pc) for px in range(2) for py in range(2) for pc in range(2)]

        def scatter(d):
            return pltpu.make_async_remote_copy(in_ref.at[d], land.at[me], send.at[0, d], recv.at[0, me],
                                                device_id=peers[d], device_id_type=MESH)

        def gather(d):
            return pltpu.make_async_remote_copy(out_ref.at[me], out_ref.at[me], send.at[1, d], recv.at[1, me],
                                                device_id=peers[d], device_id_type=MESH)

        for d in range(n_dev):
            @pl.when(d != me)
            def _():
                scatter(d).start()
        land[me] = in_ref[me]
        for d in range(n_dev):
            @pl.when(d != me)
            def _():
                pltpu.make_async_remote_copy(in_ref.at[d], land.at[d], send.at[0, d], recv.at[0, d],
                                             device_id=peers[d], device_id_type=MESH).wait_recv()
        total = land[0]
        for d in range(1, n_dev):
            total = total + land[d]
        out_ref[me] = total
        for d in range(n_dev):
            @pl.when(d != me)
            def _():
                gather(d).start()
        for d in range(n_dev):
            @pl.when(d != me)
            def _():
                pltpu.make_async_remote_copy(out_ref.at[d], out_ref.at[d], send.at[1, d], recv.at[1, d],
                                             device_id=peers[d], device_id_type=MESH).wait_recv()
        for d in range(n_dev):
            @pl.when(d != me)
            def _():
                scatter(d).wait_send()
                gather(d).wait_send()

    vm = pl.BlockSpec(memory_space=pltpu.VMEM)
    return _pcall(body, name=name, out_shape=_sds(packed.shape, F32), in_specs=[vm], out_specs=vm,
                  scratch_shapes=[pltpu.VMEM(packed.shape, F32), pltpu.SemaphoreType.DMA((2, n_dev)),
                                  pltpu.SemaphoreType.DMA((2, n_dev))],
                  side_effects=True)(packed)


def adamw(w, g, m, v, *, name, tm=256):
    shape = w.shape
    cols = shape[-1]
    rows = 1
    for s in shape[:-1]:
        rows *= s
    tm = min(tm, rows)
    assert rows % tm == 0
    two_d = lambda a: a.reshape(rows, cols)

    def body(w_ref, g_ref, m_ref, v_ref, d_ref, mo_ref, vo_ref):
        gv = g_ref[...]
        m_new = ADAM_B1 * m_ref[...] + (1.0 - ADAM_B1) * gv
        v_new = ADAM_B2 * v_ref[...] + (1.0 - ADAM_B2) * (gv * gv)
        m_hat = m_new / (1.0 - ADAM_B1 ** ADAM_STEP)
        v_hat = v_new / (1.0 - ADAM_B2 ** ADAM_STEP)
        d_ref[...] = -ADAM_LR * (m_hat / (jnp.sqrt(v_hat) + ADAM_EPS) + ADAM_WD * w_ref[...])
        mo_ref[...] = m_new
        vo_ref[...] = v_new

    spec = pl.BlockSpec((tm, cols), lambda i: (i, 0))
    outs = _pcall(body, name=name, out_shape=[_sds((rows, cols), F32)] * 3, grid=(rows // tm,), in_specs=[spec] * 4,
                  out_specs=[spec] * 3, semantics=("parallel",))(two_d(w), two_d(g), two_d(m), two_d(v))
    return [o.reshape(shape) for o in outs]


WEIGHTS = ("ln_mix_a", "w_in_a", "g_v_a", "w_spatial", "b_spatial", "w_out_a", "ln_kv", "w_kv", "g_k", "ln_mix_b",
           "w_q", "g_q", "w_out_b", "ln_mlp", "w_up", "w_down", "ln_ple", "w_ple_gate", "w_ple_proj")
MATRICES = (("w_in_a", 1, True), ("w_out_a", 1, False), ("w_kv", 0, True), ("w_q", 1, False), ("w_out_b", 1, False),
            ("w_up", 2, True), ("w_down", 2, False), ("w_ple_gate", 2, False), ("w_ple_proj", 2, True))
FIRST_LAYER = ("w_in_a", "w_out_a", "w_kv", "w_up", "w_down", "w_ple_gate", "w_ple_proj")
REPLICATED = ("w_spatial", "b_spatial", "ln_kv", "g_k", "ln_mix_b", "g_q", "ln_mlp", "ln_ple")
SHARDED_VECTORS = ("ln_mix_a", "g_v_a")
SMALL_ROWS = 18


def kernel(x, p, ln_mix_a, w_in_a, g_v_a, w_spatial, b_spatial, w_out_a, ln_kv, w_kv, g_k, ln_mix_b, w_q, g_q, w_out_b, ln_mlp, w_up, w_down, ln_ple, w_ple_gate, w_ple_proj, loss_target, m_ln_mix_a, m_w_in_a, m_g_v_a, m_w_spatial, m_b_spatial, m_w_out_a, m_ln_kv, m_w_kv, m_g_k, m_ln_mix_b, m_w_q, m_g_q, m_w_out_b, m_ln_mlp, m_w_up, m_w_down, m_ln_ple, m_w_ple_gate, m_w_ple_proj, v_ln_mix_a, v_w_in_a, v_g_v_a, v_w_spatial, v_b_spatial, v_w_out_a, v_ln_kv, v_w_kv, v_g_k, v_ln_mix_b, v_w_q, v_g_q, v_w_out_b, v_ln_mlp, v_w_up, v_w_down, v_ln_ple, v_w_ple_gate, v_w_ple_proj):
    given = dict(locals())
    weights = {n: given[n] for n in WEIGHTS}
    shard = 2 * lax.axis_index("x") + lax.axis_index("y")
    core = lax.axis_index("c")
    shard_1 = shard.astype(jnp.int32).reshape(1)
    core_1 = core.astype(jnp.int32).reshape(1)
    place = jnp.stack([shard, core]).astype(jnp.int32)

    leaves = []
    for name, layers, cols in MATRICES:
        w3 = weights[name] if layers else weights[name][None]
        for layer in range(max(layers, 1)):
            leaves.append((name, layer, cols, cast_into_slot(w3, layer, shard_1, name=f"cast_{name}_{layer}")))
    first = [lf for lf in leaves if lf[0] in FIRST_LAYER and lf[1] == 0]
    second = [lf for lf in leaves if not (lf[0] in FIRST_LAYER and lf[1] == 0)]
    got_a, vec_a = gather_shards([lf[3] for lf in first], [ln_mix_a, g_v_a], name="gather_layer0")
    got_b, _ = gather_shards([lf[3] for lf in second], [], name="gather_layer1")
    full = {}
    for (name, layer, cols, _), arr in zip(first + second, got_a + got_b):
        if not cols:
            arr = arr.reshape(N_SHARDS * arr.shape[1], arr.shape[2])
        full.setdefault(name, {})[layer] = arr
    w = {name: (tuple(full[name][l] for l in sorted(full[name])) if layers else full[name][0])
         for name, layers, _ in MATRICES}
    w["ln_mix_a"] = vec_a[0].reshape(1, D_MODEL)
    w["g_v_a"] = vec_a[1].reshape(1, D_MODEL)
    for name in REPLICATED:
        w[name] = weights[name]

    t = x.shape[1]
    loss_blk, dx, g = local_step(x[0], p.reshape(2, t, PLE_DIM), loss_target[0], w)
    loss = lax.psum(loss_blk[0, 0], ("x", "y", "c"))

    big = []
    for name, layers, cols in MATRICES:
        for layer in range(max(layers, 1)):
            arr = g[name][layer] if layers == 2 else g[name]
            rows = arr.shape[-2] if cols else arr.shape[0] // N_SHARDS
            big.append(arr.reshape(N_SHARDS, 2, rows // 2, arr.shape[-1]))
    theirs = pair_exchange(big, name="grad_pair_exchange")
    wire = [add_to_wire(a, b, core_1, name=f"grad_pair_sum_{i}") for i, (a, b) in enumerate(zip(big, theirs))]
    landed = chip_exchange(wire, name="grad_chip_exchange")
    bufs, i = [], 0
    for name, layers, _ in MATRICES:
        buf = None
        for layer in range(max(layers, 1)):
            buf = sum_chips(wire[i], landed[i], place, buf, layer, max(layers, 1), name=f"grad_chip_sum_{i}")
            i += 1
        bufs.append(buf)
    shared = pair_share(bufs, name="grad_pair_share")
    grads = {name: shared[o].reshape(weights[name].shape) for o, (name, _, _) in enumerate(MATRICES)}

    small = REPLICATED + SHARDED_VECTORS
    flat = jnp.concatenate([g[n].reshape(-1) for n in small])
    room = 8 * SMALL_ROWS * D_MODEL
    flat = jnp.concatenate([flat, jnp.zeros((room - flat.shape[0],), F32)])
    reduced = all_reduce_small(flat.reshape(8, SMALL_ROWS, D_MODEL), name="grad_small_all_reduce").reshape(-1)
    at = 0
    for n in small:
        size = g[n].size
        piece = reduced[at:at + size]
        at += size
        if n in SHARDED_VECTORS:
            per = D_MODEL // N_SHARDS
            grads[n] = lax.dynamic_slice(piece, (shard * per,), (per,)).reshape(weights[n].shape)
        else:
            grads[n] = piece.reshape(weights[n].shape)

    delta, new_m, new_v = {}, {}, {}
    for n in WEIGHTS:
        wn, gn, mn, vn = weights[n], grads[n], given["m_" + n], given["v_" + n]
        if wn.ndim == 1:
            wn, gn, mn, vn = (a.reshape(1, -1) for a in (wn, gn, mn, vn))
        outs = adamw(wn, gn, mn, vn, name=f"adamw_{n}")
        delta[n], new_m[n], new_v[n] = (o.reshape(weights[n].shape) for o in outs)
    return (loss, dx.reshape(x.shape), *[grads[n] for n in WEIGHTS], *[delta[n] for n in WEIGHTS],
            *[new_m[n] for n in WEIGHTS], *[new_v[n] for n in WEIGHTS])
```

```python
import jax
import jax.numpy as jnp
from jax import lax
from jax.experimental import pallas as pl
from jax.experimental.pallas import tpu as pltpu

F32 = jnp.float32
BF16 = jnp.bfloat16

D_MODEL = 1024
D_FF = 4096
PLE_DIM = 256
N_GROUPS = 8
CHUNK = 128
HEAD_DIM = 64
LANES = 128
ATT_BLOCK = 256
EPS = 1e-6
N_SHARDS = 4
VMEM_LIMIT = 56 * 1024 * 1024

ADAM_LR = 0.001
ADAM_B1 = 0.9
ADAM_B2 = 0.999
ADAM_EPS = 1e-08
ADAM_WD = 0.01
ADAM_STEP = 10

MESH = pl.DeviceIdType.MESH


def _pcall(body, *, name, out_shape, grid=None, in_specs=None, out_specs=None, scratch_shapes=(),
           semantics=None, aliases=None, side_effects=False, num_prefetch=0):
    params = dict(vmem_limit_bytes=VMEM_LIMIT)
    if semantics is not None:
        params["dimension_semantics"] = semantics
    if side_effects:
        params["has_side_effects"] = True
    kwargs = {}
    if aliases:
        kwargs["input_output_aliases"] = aliases
    if num_prefetch:
        spec = pltpu.PrefetchScalarGridSpec(num_scalar_prefetch=num_prefetch, grid=grid, in_specs=in_specs,
                                            out_specs=out_specs, scratch_shapes=list(scratch_shapes))
        return pl.pallas_call(body, name=name, out_shape=out_shape, grid_spec=spec,
                              compiler_params=pltpu.CompilerParams(**params), **kwargs)
    if grid is not None:
        kwargs["grid"] = grid
    if in_specs is not None:
        kwargs["in_specs"] = in_specs
    if out_specs is not None:
        kwargs["out_specs"] = out_specs
    if aliases:
        kwargs["input_output_aliases"] = aliases
    return pl.pallas_call(body, name=name, out_shape=out_shape, scratch_shapes=list(scratch_shapes),
                          compiler_params=pltpu.CompilerParams(**params), **kwargs)


def _sds(shape, dtype):
    return jax.ShapeDtypeStruct(shape, dtype)


_GELU_C = 0.7978845608028654
_GELU_A = 0.044715


def _gelu(x):
    inner = _GELU_C * (x + _GELU_A * (x * x * x))
    return 0.5 * x * (1.0 + jnp.tanh(inner))


def _gelu_grad(x):
    x2 = x * x
    t = jnp.tanh(_GELU_C * (x + _GELU_A * (x2 * x)))
    return 0.5 * (1.0 + t) + 0.5 * x * (1.0 - t * t) * (_GELU_C * (1.0 + 3.0 * _GELU_A * x2))


def _sigmoid(x):
    return 1.0 / (1.0 + jnp.exp(-x))


def _log_sigmoid(z):
    return jnp.minimum(z, 0.0) - jnp.log(1.0 + jnp.exp(-jnp.abs(z)))


def _split_bf16(a):
    hi = a.astype(BF16)
    lo = (a - hi.astype(F32)).astype(BF16)
    return hi, lo


def _dot(a, b):
    return jnp.dot(a, b, preferred_element_type=F32)


def _dot_nt(a, b):
    return lax.dot_general(a, b, (((1,), (1,)), ((), ())), preferred_element_type=F32)


def _dot_tn(a, b):
    return lax.dot_general(a, b, (((0,), (0,)), ((), ())), preferred_element_type=F32)


def _head_rstd(x):
    lane = lax.broadcasted_iota(jnp.int32, x.shape, 1)
    low = lane < HEAD_DIM
    sq = x * x
    s_lo = jnp.sum(jnp.where(low, sq, 0.0), axis=-1, keepdims=True)
    s_hi = jnp.sum(jnp.where(low, 0.0, sq), axis=-1, keepdims=True)
    ms = jnp.where(low, s_lo, s_hi) * (1.0 / HEAD_DIM)
    return lax.rsqrt(ms + EPS)


def _head_mean(x):
    lane = lax.broadcasted_iota(jnp.int32, x.shape, 1)
    low = lane < HEAD_DIM
    s_lo = jnp.sum(jnp.where(low, x, 0.0), axis=-1, keepdims=True)
    s_hi = jnp.sum(jnp.where(low, 0.0, x), axis=-1, keepdims=True)
    return jnp.where(low, s_lo, s_hi) * (1.0 / HEAD_DIM)


def _full(shape):
    zeros = (0,) * len(shape)
    return pl.BlockSpec(shape, lambda i: zeros)


def norm_matmul(x, g, w, *, name, epilogue="none", tm=512):
    t, d = x.shape
    sharded = w.ndim == 3
    per = w.shape[2] if sharded else w.shape[1]
    n = N_SHARDS * per if sharded else per
    tm = min(tm, t)

    def body(x_ref, g_ref, w_ref, h_ref, r_ref, *outs):
        xv = x_ref[...]
        r = lax.rsqrt(jnp.mean(xv * xv, axis=-1, keepdims=True) + EPS)
        h = ((xv * r) * g_ref[...]).astype(BF16)
        h_ref[...] = h
        r_ref[...] = r
        for s in range(N_SHARDS if sharded else 1):
            cols = slice(s * per, (s + 1) * per)
            y = _dot(h, w_ref[s] if sharded else w_ref[...])
            if epilogue == "none":
                outs[0][:, cols] = y
            else:
                a = jnp.maximum(y, 0.0)
                outs[0][:, cols] = a.astype(BF16)
                outs[1][:, cols] = (a * a).astype(BF16)

    row = lambda i: (i, 0)
    out_shape = [_sds((t, d), BF16), _sds((t, 1), F32)]
    out_specs = [pl.BlockSpec((tm, d), row), pl.BlockSpec((tm, 1), row)]
    if epilogue == "none":
        out_shape.append(_sds((t, n), F32))
        out_specs.append(pl.BlockSpec((tm, n), row))
    else:
        out_shape += [_sds((t, n), BF16), _sds((t, n), BF16)]
        out_specs += [pl.BlockSpec((tm, n), row)] * 2
    return _pcall(
        body, name=name, out_shape=out_shape, grid=(t // tm,),
        in_specs=[pl.BlockSpec((tm, d), row), _full((1, d)), _full(w.shape)],
        out_specs=out_specs, semantics=("parallel",))(x, g, w)


def matmul_residual(a, w, res, *, name, tm=512):
    t, k = a.shape
    n = w.shape[1]
    tm = min(tm, t)

    def body(a_ref, w_ref, res_ref, o_ref):
        o_ref[...] = res_ref[...] + _dot(a_ref[...], w_ref[...])

    row = lambda i: (i, 0)
    return _pcall(
        body, name=name, out_shape=_sds((t, n), F32), grid=(t // tm,),
        in_specs=[pl.BlockSpec((tm, k), row), _full(w.shape), pl.BlockSpec((tm, n), row)],
        out_specs=pl.BlockSpec((tm, n), row), semantics=("parallel",))(a, w, res)


def ple_forward(x, g, w_gate, p, w_proj, *, name, tm=256):
    t, d = x.shape
    tm = min(tm, t)

    def body(x_ref, g_ref, wg_ref, p_ref, wp_ref, h_ref, r_ref, gate_ref, pp_ref, o_ref):
        xv = x_ref[...]
        r = lax.rsqrt(jnp.mean(xv * xv, axis=-1, keepdims=True) + EPS)
        h = ((xv * r) * g_ref[...]).astype(BF16)
        h_ref[...] = h
        r_ref[...] = r
        gate = _sigmoid(_dot(h, wg_ref[...]))
        gate_ref[...] = gate
        pb = p_ref[...].astype(BF16)
        per = d // N_SHARDS
        for s in range(N_SHARDS):
            cols = slice(s * per, (s + 1) * per)
            pp = _dot(pb, wp_ref[s])
            pp_ref[:, cols] = pp.astype(BF16)
            o_ref[:, cols] = xv[:, cols] + pp * gate[:, cols]

    row = lambda i: (i, 0)
    fixed = lambda i: (0, 0)
    return _pcall(
        body, name=name,
        out_shape=[_sds((t, d), BF16), _sds((t, 1), F32), _sds((t, d), F32), _sds((t, d), BF16), _sds((t, d), F32)],
        grid=(t // tm,),
        in_specs=[pl.BlockSpec((tm, d), row), pl.BlockSpec((1, d), fixed), pl.BlockSpec((d, d), fixed),
                  pl.BlockSpec((tm, PLE_DIM), row),
                  pl.BlockSpec((N_SHARDS, PLE_DIM, d // N_SHARDS), lambda i: (0, 0, 0))],
        out_specs=[pl.BlockSpec((tm, d), row), pl.BlockSpec((tm, 1), row), pl.BlockSpec((tm, d), row),
                   pl.BlockSpec((tm, d), row), pl.BlockSpec((tm, d), row)],
        semantics=("parallel",))(x, g, w_gate, p, w_proj)


def _tril_mask():
    r = lax.broadcasted_iota(jnp.int32, (CHUNK, CHUNK), 0)
    c = lax.broadcasted_iota(jnp.int32, (CHUNK, CHUNK), 1)
    return c <= r


def _sgu_common(pre_ref, gv_ref, ws_ref):
    pre = pre_ref[...]
    pre_u, pre_v = pre[:, :D_MODEL], pre[:, D_MODEL:]
    u = _gelu(pre_u)
    v = _gelu(pre_v)
    r = lax.rsqrt(jnp.mean(v * v, axis=-1, keepdims=True) + EPS)
    vhat = v * r
    vn = (vhat * gv_ref[...]).astype(BF16)
    tril = _tril_mask()
    wm = [jnp.where(tril, ws_ref[g], 0.0).astype(BF16) for g in range(N_GROUPS)]
    return pre_u, pre_v, u, r, vhat, vn, wm, tril


def sgu_forward(pre, g_v, w_s, b_full, *, name):
    t = pre.shape[0]

    def body(pre_ref, gv_ref, ws_ref, b_ref, y_ref):
        _, _, u, _, _, vn, wm, _ = _sgu_common(pre_ref, gv_ref, ws_ref)
        for g in range(N_GROUPS):
            cols = slice(g * LANES, (g + 1) * LANES)
            mix = _dot(wm[g], vn[:, cols]) + b_ref[:, cols]
            y_ref[:, cols] = (u[:, cols] * mix).astype(BF16)

    return _pcall(
        body, name=name, out_shape=_sds((t, D_MODEL), BF16), grid=(t // CHUNK,),
        in_specs=[pl.BlockSpec((CHUNK, 2 * D_MODEL), lambda i: (i, 0)), pl.BlockSpec((1, D_MODEL), lambda i: (0, 0)),
                  pl.BlockSpec((N_GROUPS, CHUNK, CHUNK), lambda i: (0, 0, 0)),
                  pl.BlockSpec((CHUNK, D_MODEL), lambda i: (0, 0))],
        out_specs=pl.BlockSpec((CHUNK, D_MODEL), lambda i: (i, 0)),
        semantics=("parallel",))(pre, g_v, w_s, b_full)


def head_norm(pre, g128, *, name, col_block=0, scale=1.0, passthrough=False, tm=512):
    t = pre.shape[0]
    tm = min(tm, t)

    def body(*refs):
        if passthrough:
            x_ref, v_ref, g_ref, o_ref, vo_ref = refs
            vo_ref[...] = v_ref[...].astype(BF16)
        else:
            x_ref, g_ref, o_ref = refs
        g = g_ref[...] * scale
        for b in range(D_MODEL // LANES):
            cols = slice(b * LANES, (b + 1) * LANES)
            xv = x_ref[:, cols]
            o_ref[:, cols] = ((xv * _head_rstd(xv)) * g).astype(BF16)

    x_spec = pl.BlockSpec((tm, D_MODEL), lambda i: (i, col_block))
    g_spec = pl.BlockSpec((1, LANES), lambda i: (0, 0))
    o_spec = pl.BlockSpec((tm, D_MODEL), lambda i: (i, 0))
    if passthrough:
        return _pcall(body, name=name, out_shape=[_sds((t, D_MODEL), BF16)] * 2, grid=(t // tm,),
                      in_specs=[x_spec, pl.BlockSpec((tm, D_MODEL), lambda i: (i, 1)), g_spec],
                      out_specs=[o_spec, o_spec], semantics=("parallel",))(pre, pre, g128)
    return _pcall(body, name=name, out_shape=_sds((t, D_MODEL), BF16), grid=(t // tm,),
                  in_specs=[x_spec, g_spec], out_specs=o_spec, semantics=("parallel",))(pre, g128)


def _suffix_matrix(n):
    r = lax.broadcasted_iota(jnp.int32, (n, n), 0)
    c = lax.broadcasted_iota(jnp.int32, (n, n), 1)
    return jnp.where(r > c, 1.0, 0.0).astype(BF16)


def _prefix_matrix(n):
    r = lax.broadcasted_iota(jnp.int32, (n, n), 0)
    c = lax.broadcasted_iota(jnp.int32, (n, n), 1)
    return jnp.where(r < c, 1.0, 0.0).astype(BF16)


def _exact_cumsum(a, tri):
    rows = a.shape[0]
    hi, lo = _split_bf16(a)
    both = _dot(jnp.concatenate([hi, lo], axis=0), tri)
    return both[:rows] + both[rows:]


def _stacked_causal(n):
    r = lax.broadcasted_iota(jnp.int32, (2 * n, n), 0)
    c = lax.broadcasted_iota(jnp.int32, (2 * n, n), 1)
    return c < jnp.where(r >= n, r - n, r)


def _stack_heads(a, low):
    zero = jnp.zeros_like(a)
    return jnp.concatenate([jnp.where(low, a, zero), jnp.where(low, zero, a)], axis=0)


def stick_breaking_forward(q, k, v, *, name):
    t = q.shape[0]
    blk = min(ATT_BLOCK, t)
    nq = t // blk

    def body(q_ref, k_ref, v_ref, o_ref):
        i = pl.program_id(1)
        low = lax.broadcasted_iota(jnp.int32, (blk, LANES), 1) < HEAD_DIM
        tri = _suffix_matrix(blk)
        causal = _stacked_causal(blk)
        qs = _stack_heads(q_ref[...], low)

        def block(j, carry, acc, masked):
            rows = pl.ds(pl.multiple_of(j * blk, blk), blk)
            z = _dot_nt(qs, k_ref[rows, :])
            ls = _log_sigmoid(z)
            lg = ls - z
            if masked:
                lg = jnp.where(causal, lg, 0.0)
            s = ls + _exact_cumsum(lg, tri) + carry
            a = jnp.exp(s)
            if masked:
                a = jnp.where(causal, a, 0.0)
            acc = acc + _dot(a.astype(BF16), v_ref[rows, :])
            return carry + jnp.sum(lg, axis=-1, keepdims=True), acc

        carry, acc = block(i, jnp.zeros((2 * blk, 1), F32), jnp.zeros((2 * blk, LANES), F32), True)
        _, acc = lax.fori_loop(0, i, lambda n, st: block(i - 1 - n, st[0], st[1], False), (carry, acc))
        o_ref[...] = jnp.where(low, acc[:blk], acc[blk:]).astype(BF16)

    return _pcall(
        body, name=name, out_shape=_sds((t, D_MODEL), BF16), grid=(D_MODEL // LANES, nq),
        in_specs=[pl.BlockSpec((blk, LANES), lambda p, i: (i, p)), pl.BlockSpec((t, LANES), lambda p, i: (0, p)),
                  pl.BlockSpec((t, LANES), lambda p, i: (0, p))],
        out_specs=pl.BlockSpec((blk, LANES), lambda p, i: (i, p)),
        semantics=("parallel", "arbitrary"))(q, k, v)


def loss_forward(x, target, *, name, tm=512):
    t, d = x.shape
    tm = min(tm, t)

    def body(x_ref, t_ref, l_ref, dx_ref):
        @pl.when(pl.program_id(0) == 0)
        def _():
            l_ref[...] = jnp.zeros_like(l_ref)

        diff = x_ref[...] - t_ref[...]
        dx_ref[...] = diff * (1.0 / d)
        l_ref[...] += 0.5 * jnp.sum(jnp.mean(diff * diff, axis=-1, keepdims=True))

    return _pcall(
        body, name=name, out_shape=[_sds((8, LANES), F32), _sds((t, d), F32)], grid=(t // tm,),
        in_specs=[pl.BlockSpec((tm, d), lambda i: (i, 0))] * 2,
        out_specs=[pl.BlockSpec((8, LANES), lambda i: (0, 0)), pl.BlockSpec((tm, d), lambda i: (i, 0))],
        semantics=("arbitrary",))(x, target)


def matmul_nt(dy, w, *, name, mul=None, out_dtype=F32, tm=512):
    t, n = dy.shape
    k = w.shape[0]
    tm = min(tm, t)

    def body(*refs):
        if mul is None:
            dy_ref, w_ref, o_ref = refs
        else:
            dy_ref, w_ref, m_ref, o_ref = refs
        y = _dot_nt(dy_ref[...].astype(BF16), w_ref[...])
        if mul is not None:
            y = y * (2.0 * m_ref[...].astype(F32))
        o_ref[...] = y.astype(out_dtype)

    row = lambda i: (i, 0)
    in_specs = [pl.BlockSpec((tm, n), row), _full(w.shape)]
    args = [dy, w]
    if mul is not None:
        in_specs.append(pl.BlockSpec((tm, k), row))
        args.append(mul)
    return _pcall(body, name=name, out_shape=_sds((t, k), out_dtype), grid=(t // tm,), in_specs=in_specs,
                  out_specs=pl.BlockSpec((tm, k), row), semantics=("parallel",))(*args)


def matmul_tn(a, dy, *, name, col_shards, tk=512):
    t, k = a.shape
    n = dy.shape[1]
    if col_shards:
        tn = n // N_SHARDS

        def body(a_ref, dy_ref, o_ref):
            o_ref[...] = _dot_tn(a_ref[...].astype(BF16), dy_ref[...].astype(BF16))

        return _pcall(body, name=name, out_shape=_sds((N_SHARDS, k, tn), F32), grid=(N_SHARDS,),
                      in_specs=[_full((t, k)), pl.BlockSpec((t, tn), lambda j: (0, j))],
                      out_specs=pl.BlockSpec((None, k, tn), lambda j: (j, 0, 0)), semantics=("parallel",))(a, dy)

    tk = min(tk, k)

    def body(a_ref, dy_ref, o_ref, dy_bf):
        @pl.when(pl.program_id(0) == 0)
        def _():
            dy_bf[...] = dy_ref[...].astype(BF16)

        o_ref[...] = _dot_tn(a_ref[...].astype(BF16), dy_bf[...])

    return _pcall(body, name=name, out_shape=_sds((k, n), F32), grid=(k // tk,),
                  in_specs=[pl.BlockSpec((t, tk), lambda i: (0, i)), _full((t, n))],
                  out_specs=pl.BlockSpec((tk, n), lambda i: (i, 0)),
                  scratch_shapes=[pltpu.VMEM((t, n), BF16)], semantics=("arbitrary",))(a, dy)


def norm_backward(dpre, w, x, g, rstd, dx_out, *, name, tm=512):
    t, d = x.shape
    n = dpre.shape[1]
    tm = min(tm, t)
    if w.ndim == 3:
        w_spec = pl.BlockSpec(w.shape, lambda i: (0, 0, 0))
    else:
        w_spec = pl.BlockSpec(w.shape, lambda i: (0, 0))

    def body(dp_ref, w_ref, x_ref, g_ref, r_ref, dxo_ref, dx_ref, dg_ref):
        @pl.when(pl.program_id(0) == 0)
        def _():
            dg_ref[...] = jnp.zeros_like(dg_ref)

        if w.ndim == 3:
            per = n // N_SHARDS
            dh = _dot_nt(dp_ref[:, 0:per], w_ref[0])
            for s in range(1, N_SHARDS):
                dh = dh + _dot_nt(dp_ref[:, s * per:(s + 1) * per], w_ref[s])
        else:
            dh = _dot_nt(dp_ref[...], w_ref[...])
        r = r_ref[...]
        xn = x_ref[...] * r
        dg_ref[...] += jnp.sum(dh * xn, axis=0, keepdims=True)
        dxn = dh * g_ref[...]
        dx = r * (dxn - xn * jnp.mean(dxn * xn, axis=-1, keepdims=True))
        dx_ref[...] = dxo_ref[...] + dx

    row = lambda i: (i, 0)
    fixed = lambda i: (0, 0)
    return _pcall(
        body, name=name, out_shape=[_sds((t, d), F32), _sds((1, d), F32)], grid=(t // tm,),
        in_specs=[pl.BlockSpec((tm, n), row), w_spec, pl.BlockSpec((tm, d), row),
                  pl.BlockSpec((1, d), fixed), pl.BlockSpec((tm, 1), row), pl.BlockSpec((tm, d), row)],
        out_specs=[pl.BlockSpec((tm, d), row), pl.BlockSpec((1, d), fixed)],
        semantics=("arbitrary",))(dpre, w, x, g, rstd, dx_out)


def ple_backward(dx, gate, pp, *, name, tm=512):
    t, d = dx.shape
    tm = min(tm, t)

    def body(dx_ref, gate_ref, pp_ref, dg_ref, dp_ref):
        dxv = dx_ref[...]
        gate = gate_ref[...]
        dg_ref[...] = (dxv * pp_ref[...].astype(F32) * (gate * (1.0 - gate))).astype(BF16)
        dp_ref[...] = (dxv * gate).astype(BF16)

    spec = pl.BlockSpec((tm, d), lambda i: (i, 0))
    return _pcall(body, name=name, out_shape=[_sds((t, d), BF16)] * 2, grid=(t // tm,), in_specs=[spec] * 3,
                  out_specs=[spec] * 2, semantics=("parallel",))(dx, gate, pp)


def sgu_backward(dy, pre, g_v, w_s, b_full, *, name):
    t = pre.shape[0]
    n_chunks = t // CHUNK

    def body(dy_ref, pre_ref, gv_ref, ws_ref, b_ref, dpre_ref, dws_ref, db_ref, dgv_ref, dvn_s, dbf_s):
        step = pl.program_id(0)

        @pl.when(step == 0)
        def _():
            dws_ref[...] = jnp.zeros_like(dws_ref)
            dgv_ref[...] = jnp.zeros_like(dgv_ref)
            dbf_s[...] = jnp.zeros_like(dbf_s)

        pre_u, pre_v, u, r, vhat, vn, wm, tril = _sgu_common(pre_ref, gv_ref, ws_ref)
        dyv = dy_ref[...]
        for g in range(N_GROUPS):
            cols = slice(g * LANES, (g + 1) * LANES)
            mix = _dot(wm[g], vn[:, cols]) + b_ref[:, cols]
            dmix = dyv[:, cols] * u[:, cols]
            dmix_b = dmix.astype(BF16)
            du = dyv[:, cols] * mix
            dpre_ref[:, cols] = (du * _gelu_grad(pre_u[:, cols])).astype(BF16)
            dws_ref[g] += jnp.where(tril, _dot_nt(dmix_b, vn[:, cols]), 0.0)
            dbf_s[:, cols] += dmix
            dvn_s[:, cols] = _dot_tn(wm[g], dmix_b)
        dvn = dvn_s[...]
        dgv_ref[...] += jnp.sum(dvn * vhat, axis=0, keepdims=True)
        dxn = dvn * gv_ref[...]
        dv = r * (dxn - vhat * jnp.mean(dxn * vhat, axis=-1, keepdims=True))
        dpre_ref[:, D_MODEL:] = (dv * _gelu_grad(pre_v)).astype(BF16)

        @pl.when(step == n_chunks - 1)
        def _():
            lane = lax.broadcasted_iota(jnp.int32, (CHUNK, LANES), 1)
            acc = jnp.zeros((CHUNK, LANES), F32)
            for g in range(N_GROUPS):
                s = jnp.sum(dbf_s[:, g * LANES:(g + 1) * LANES], axis=-1, keepdims=True)
                acc = jnp.where(lane == g, s, acc)
            db_ref[...] = acc

    fixed2 = lambda i: (0, 0)
    return _pcall(
        body, name=name,
        out_shape=[_sds((t, 2 * D_MODEL), BF16), _sds((N_GROUPS, CHUNK, CHUNK), F32), _sds((CHUNK, LANES), F32),
                   _sds((1, D_MODEL), F32)],
        grid=(n_chunks,),
        in_specs=[pl.BlockSpec((CHUNK, D_MODEL), lambda i: (i, 0)), pl.BlockSpec((CHUNK, 2 * D_MODEL), lambda i: (i, 0)),
                  pl.BlockSpec((1, D_MODEL), fixed2), pl.BlockSpec((N_GROUPS, CHUNK, CHUNK), lambda i: (0, 0, 0)),
                  pl.BlockSpec((CHUNK, D_MODEL), fixed2)],
        out_specs=[pl.BlockSpec((CHUNK, 2 * D_MODEL), lambda i: (i, 0)),
                   pl.BlockSpec((N_GROUPS, CHUNK, CHUNK), lambda i: (0, 0, 0)), pl.BlockSpec((CHUNK, LANES), fixed2),
                   pl.BlockSpec((1, D_MODEL), fixed2)],
        scratch_shapes=[pltpu.VMEM((CHUNK, D_MODEL), F32), pltpu.VMEM((CHUNK, D_MODEL), F32)],
        semantics=("arbitrary",))(dy, pre, g_v, w_s, b_full)


def head_norm_backward(dy, pre, g128, *, name, col_block=0, scale=1.0, passthrough=None, tm=512):
    t = dy.shape[0]
    tm = min(tm, t)
    width = 2 * D_MODEL if passthrough is not None else D_MODEL

    def body(*refs):
        if passthrough is not None:
            dy_ref, x_ref, g_ref, dv_ref, o_ref, dg_ref = refs
            o_ref[:, D_MODEL:] = dv_ref[...].astype(BF16)
        else:
            dy_ref, x_ref, g_ref, o_ref, dg_ref = refs

        @pl.when(pl.program_id(0) == 0)
        def _():
            dg_ref[...] = jnp.zeros_like(dg_ref)

        g = g_ref[...]
        dg = jnp.zeros((1, LANES), F32)
        for b in range(D_MODEL // LANES):
            cols = slice(b * LANES, (b + 1) * LANES)
            xv = x_ref[:, cols]
            r = _head_rstd(xv)
            xn = xv * r
            dyv = dy_ref[:, cols] * scale
            dg = dg + jnp.sum(dyv * xn, axis=0, keepdims=True)
            dxn = dyv * g
            o_ref[:, cols] = (r * (dxn - xn * _head_mean(dxn * xn))).astype(BF16)
        dg_ref[...] += dg

    row = lambda i: (i, 0)
    in_specs = [pl.BlockSpec((tm, D_MODEL), row), pl.BlockSpec((tm, D_MODEL), lambda i: (i, col_block)),
                pl.BlockSpec((1, LANES), lambda i: (0, 0))]
    args = [dy, pre, g128]
    if passthrough is not None:
        in_specs.append(pl.BlockSpec((tm, D_MODEL), row))
        args.append(passthrough)
    return _pcall(body, name=name, out_shape=[_sds((t, width), BF16), _sds((1, LANES), F32)], grid=(t // tm,),
                  in_specs=in_specs,
                  out_specs=[pl.BlockSpec((tm, width), row), pl.BlockSpec((1, LANES), lambda i: (0, 0))],
                  semantics=("arbitrary",))(*args)


def stick_breaking_backward(q, k, v, do, *, name):
    t = q.shape[0]
    blk = min(ATT_BLOCK, t)
    nq = t // blk

    def body(q_ref, k_ref, v_ref, do_ref, dq_ref, dk_ref, dv_ref, s_buf, sg_buf):
        i = pl.program_id(1)

        @pl.when(i == 0)
        def _():
            dk_ref[...] = jnp.zeros_like(dk_ref)
            dv_ref[...] = jnp.zeros_like(dv_ref)

        low = lax.broadcasted_iota(jnp.int32, (blk, LANES), 1) < HEAD_DIM
        suffix = _suffix_matrix(blk)
        prefix = _prefix_matrix(blk)
        causal = _stacked_causal(blk)
        qs = _stack_heads(q_ref[...], low)
        dos = _stack_heads(do_ref[...], low)

        def log_weights(j, carry, masked):
            rows = pl.ds(pl.multiple_of(j * blk, blk), blk)
            z = _dot_nt(qs, k_ref[rows, :])
            ls = _log_sigmoid(z)
            lg = ls - z
            if masked:
                lg = jnp.where(causal, lg, 0.0)
            s_buf[j] = ls + _exact_cumsum(lg, suffix) + carry
            sg_buf[j] = jnp.exp(ls)
            return carry + jnp.sum(lg, axis=-1, keepdims=True)

        carry = log_weights(i, jnp.zeros((2 * blk, 1), F32), True)
        lax.fori_loop(0, i, lambda n, c: log_weights(i - 1 - n, c, False), carry)

        def grads(j, pcarry, dq_acc, masked):
            rows = pl.ds(pl.multiple_of(j * blk, blk), blk)
            a = jnp.exp(s_buf[j])
            if masked:
                a = jnp.where(causal, a, 0.0)
            sg = sg_buf[j]
            ds = _dot_nt(dos, v_ref[rows, :]) * a
            before = _exact_cumsum(ds, prefix) + pcarry
            if masked:
                before = jnp.where(causal, before, 0.0)
            dz = (ds - sg * (ds + before)).astype(BF16)
            dq_acc = dq_acc + _dot(dz, k_ref[rows, :])
            dk_ref[rows, :] += _dot_tn(dz, qs)
            dv_ref[rows, :] += _dot_tn(a.astype(BF16), dos)
            return pcarry + jnp.sum(ds, axis=-1, keepdims=True), dq_acc

        state = lax.fori_loop(0, i, lambda j, st: grads(j, st[0], st[1], False),
                              (jnp.zeros((2 * blk, 1), F32), jnp.zeros((2 * blk, LANES), F32)))
        _, dq_acc = grads(i, state[0], state[1], True)
        dq_ref[...] = jnp.where(low, dq_acc[:blk], dq_acc[blk:])

    full = pl.BlockSpec((t, LANES), lambda p, i: (0, p))
    qblk = pl.BlockSpec((blk, LANES), lambda p, i: (i, p))
    return _pcall(
        body, name=name, out_shape=[_sds((t, D_MODEL), F32)] * 3, grid=(D_MODEL // LANES, nq),
        in_specs=[qblk, full, full, qblk], out_specs=[qblk, full, full],
        scratch_shapes=[pltpu.VMEM((nq, 2 * blk, blk), F32), pltpu.VMEM((nq, 2 * blk, blk), F32)],
        semantics=("parallel", "arbitrary"))(q, k, v, do)


def _mlp_forward(x, g, w_up, w_down, tag):
    h, r, a, a2 = norm_matmul(x, g, w_up, name=f"mlp_up_{tag}", epilogue="relu2")
    return matmul_residual(a2, w_down, x, name=f"mlp_down_{tag}"), (x, h, r, a, a2)


def _mlp_backward(dx, saved, g, w_up, w_down, tag):
    x, h, r, a, a2 = saved
    d_w_down = matmul_tn(a2, dx, name=f"d_w_down_{tag}", col_shards=False)
    dpre = matmul_nt(dx, w_down, name=f"d_mlp_act_{tag}", mul=a, out_dtype=BF16)
    d_w_up = matmul_tn(h, dpre, name=f"d_w_up_{tag}", col_shards=True)
    dx, d_g = norm_backward(dpre, w_up, x, g, r, dx, name=f"d_mlp_norm_{tag}")
    return dx, d_w_up, d_w_down, d_g


def _ple_backward(dx, saved, p, g, w_gate, tag):
    x, h, r, gate, pp = saved
    dgate, dproj = ple_backward(dx, gate, pp, name=f"d_ple_{tag}")
    d_w_proj = matmul_tn(p, dproj, name=f"d_w_ple_proj_{tag}", col_shards=True)
    d_w_gate = matmul_tn(h, dgate, name=f"d_w_ple_gate_{tag}", col_shards=False)
    dx, d_g = norm_backward(dgate, w_gate, x, g, r, dx, name=f"d_ple_norm_{tag}")
    return dx, d_w_gate, d_w_proj, d_g


def local_step(x, p, target, w):
    row = lambda v: v.reshape(1, -1)
    g128 = lambda v: jnp.tile(v.reshape(1, HEAD_DIM), (1, 2))
    scale = HEAD_DIM ** -0.5
    b_full = jnp.repeat(jnp.transpose(w["b_spatial"][0]), LANES, axis=1)
    w_s = w["w_spatial"][0]

    x0 = x
    h_a, r_a, pre_a = norm_matmul(x0, row(w["ln_mix_a"][0]), w["w_in_a"][0], name="sgu_in")
    y_a = sgu_forward(pre_a, row(w["g_v_a"][0]), w_s, b_full, name="sgu_mix")
    x1 = matmul_residual(y_a, w["w_out_a"][0], x0, name="sgu_out")
    x2, mlp0 = _mlp_forward(x1, row(w["ln_mlp"][0]), w["w_up"][0], w["w_down"][0], 0)
    ple0 = ple_forward(x2, row(w["ln_ple"][0]), w["w_ple_gate"][0], p[0], w["w_ple_proj"][0], name="ple_0")
    x3 = ple0[4]
    h_kv, r_kv, kv_pre = norm_matmul(x3, row(w["ln_kv"]), w["w_kv"], name="kv_proj")
    k_n, v_b = head_norm(kv_pre, g128(w["g_k"]), name="k_norm", passthrough=True)
    h_q, r_q, q_pre = norm_matmul(x3, row(w["ln_mix_b"][0]), w["w_q"][0], name="q_proj")
    q_n = head_norm(q_pre, g128(w["g_q"][0]), name="q_norm", scale=scale)
    o = stick_breaking_forward(q_n, k_n, v_b, name="sb_fwd")
    x4 = matmul_residual(o, w["w_out_b"][0], x3, name="sb_out")
    x5, mlp1 = _mlp_forward(x4, row(w["ln_mlp"][1]), w["w_up"][1], w["w_down"][1], 1)
    ple1 = ple_forward(x5, row(w["ln_ple"][1]), w["w_ple_gate"][1], p[1], w["w_ple_proj"][1], name="ple_1")
    x6 = ple1[4]
    loss_blk, dx = loss_forward(x6, target, name="loss")

    g = {}
    dx, dwg1, dwp1, dlnp1 = _ple_backward(dx, (x5,) + tuple(ple1[:4]), p[1], row(w["ln_ple"][1]), w["w_ple_gate"][1], 1)
    dx, dwu1, dwd1, dlnm1 = _mlp_backward(dx, mlp1, row(w["ln_mlp"][1]), w["w_up"][1], w["w_down"][1], 1)
    g["w_out_b"] = matmul_tn(o, dx, name="d_w_out_b", col_shards=False)
    do = matmul_nt(dx, w["w_out_b"][0], name="d_sb_out", out_dtype=BF16)
    dq_n, dk_n, dv = stick_breaking_backward(q_n, k_n, v_b, do, name="sb_bwd")
    dq_pre, dgq = head_norm_backward(dq_n, q_pre, g128(w["g_q"][0]), name="d_q_norm", scale=scale)
    dkv_pre, dgk = head_norm_backward(dk_n, kv_pre, g128(w["g_k"]), name="d_k_norm", passthrough=dv)
    g["w_q"] = matmul_tn(h_q, dq_pre, name="d_w_q", col_shards=False)
    g["w_kv"] = matmul_tn(h_kv, dkv_pre, name="d_w_kv", col_shards=True)
    dx, g["ln_mix_b"] = norm_backward(dq_pre, w["w_q"][0], x3, row(w["ln_mix_b"][0]), r_q, dx, name="d_q_in")
    dx, g["ln_kv"] = norm_backward(dkv_pre, w["w_kv"], x3, row(w["ln_kv"]), r_kv, dx, name="d_kv_in")
    g["g_q"] = dgq[:, :HEAD_DIM] + dgq[:, HEAD_DIM:]
    g["g_k"] = (dgk[:, :HEAD_DIM] + dgk[:, HEAD_DIM:]).reshape(HEAD_DIM)
    g["ln_kv"] = g["ln_kv"].reshape(D_MODEL)
    dx, dwg0, dwp0, dlnp0 = _ple_backward(dx, (x2,) + tuple(ple0[:4]), p[0], row(w["ln_ple"][0]), w["w_ple_gate"][0], 0)
    dx, dwu0, dwd0, dlnm0 = _mlp_backward(dx, mlp0, row(w["ln_mlp"][0]), w["w_up"][0], w["w_down"][0], 0)
    g["w_out_a"] = matmul_tn(y_a, dx, name="d_w_out_a", col_shards=False)
    dy_a = matmul_nt(dx, w["w_out_a"][0], name="d_sgu_out")
    dpre_a, dws, db, g["g_v_a"] = sgu_backward(dy_a, pre_a, row(w["g_v_a"][0]), w_s, b_full, name="d_sgu_mix")
    g["w_in_a"] = matmul_tn(h_a, dpre_a, name="d_w_in_a", col_shards=True)
    dx, g["ln_mix_a"] = norm_backward(dpre_a, w["w_in_a"][0], x0, row(w["ln_mix_a"][0]), r_a, dx, name="d_sgu_in")
    g["w_spatial"] = dws[None]
    g["b_spatial"] = jnp.transpose(db[:, :N_GROUPS])[None]
    g["w_up"] = (dwu0, dwu1)
    g["w_down"] = (dwd0, dwd1)
    g["w_ple_gate"] = (dwg0, dwg1)
    g["w_ple_proj"] = (dwp0, dwp1)
    g["ln_mlp"] = jnp.concatenate([dlnm0, dlnm1], axis=0)
    g["ln_ple"] = jnp.concatenate([dlnp0, dlnp1], axis=0)
    return loss_blk, dx, g


ANY = pl.BlockSpec(memory_space=pl.ANY)


def _place():
    x, y, c = lax.axis_index("x"), lax.axis_index("y"), lax.axis_index("c")
    others = [(1 - x, y), (x, 1 - y), (1 - x, 1 - y)]
    return x, y, c, 2 * x + y, others


def cast_into_slot(w3, layer, slot, *, name, tm=256):
    _, r, c = w3.shape
    tm = min(tm, r)

    def body(slot_ref, w_ref, o_ref):
        o_ref[...] = w_ref[...].astype(BF16)

    return _pcall(body, name=name, out_shape=_sds((N_SHARDS, r, c), BF16), grid=(r // tm,), num_prefetch=1,
                  in_specs=[pl.BlockSpec((None, tm, c), lambda i, s: (layer, i, 0))],
                  out_specs=pl.BlockSpec((None, tm, c), lambda i, s: (s[0], i, 0)),
                  semantics=("parallel",))(slot, w3)


def gather_shards(mats, vecs, *, name):
    nm, nv = len(mats), len(vecs)
    halves = [m.reshape(N_SHARDS, 2, m.shape[1] // 2, m.shape[2]) for m in mats]

    def body(*refs):
        vsrc = refs[nm:nm + nv]
        out, vout = refs[nm + nv:2 * nm + nv], refs[2 * nm + nv:2 * (nm + nv)]
        send, recv, vsend, vrecv, loc = refs[2 * (nm + nv):]
        x, y, c, s_me, others = _place()
        sib = (x, y, 1 - c)

        def ici(l, k):
            ox, oy = others[k]
            return pltpu.make_async_remote_copy(out[l].at[s_me, c], out[l].at[s_me, c], send.at[l, k], recv.at[l, k],
                                                device_id=(ox, oy, c), device_id_type=MESH)

        def landed(l, k, half):
            ox, oy = others[k]
            return out[l].at[2 * ox + oy, half]

        def passed_on(l, k):
            return pltpu.make_async_remote_copy(landed(l, k, c), landed(l, k, c), send.at[l, 3 + k], recv.at[l, 3 + k],
                                                device_id=sib, device_id_type=MESH)

        def vec(l, k):
            ox, oy = others[k]
            return pltpu.make_async_remote_copy(vsrc[l], vout[l].at[s_me], vsend.at[l, k], vrecv.at[l, k],
                                                device_id=(ox, oy, c), device_id_type=MESH)

        for l in range(nm):
            for k in range(3):
                ici(l, k).start()
        for l in range(nv):
            for k in range(3):
                vec(l, k).start()
        for l in range(nv):
            own = pltpu.make_async_copy(vsrc[l], vout[l].at[s_me], loc)
            own.start()
            own.wait()
        for l in range(nm):
            for k in range(3):
                pltpu.make_async_remote_copy(landed(l, k, c), landed(l, k, c), send.at[l, k], recv.at[l, k],
                                             device_id=sib, device_id_type=MESH).wait_recv()
                passed_on(l, k).start()
        for l in range(nm):
            for k in range(3):
                pltpu.make_async_remote_copy(landed(l, k, 1 - c), landed(l, k, 1 - c), send.at[l, 3 + k],
                                             recv.at[l, 3 + k], device_id=sib, device_id_type=MESH).wait_recv()
        for l in range(nv):
            for k in range(3):
                ox, oy = others[k]
                pltpu.make_async_remote_copy(vsrc[l], vout[l].at[2 * ox + oy], vsend.at[l, k], vrecv.at[l, k],
                                             device_id=sib, device_id_type=MESH).wait_recv()
        for l in range(nm):
            for k in range(3):
                ici(l, k).wait_send()
                passed_on(l, k).wait_send()
        for l in range(nv):
            for k in range(3):
                vec(l, k).wait_send()

    out_shape = [_sds(h.shape, BF16) for h in halves] + [_sds((N_SHARDS,) + v.shape, F32) for v in vecs]
    res = _pcall(body, name=name, out_shape=out_shape, in_specs=[ANY] * (nm + nv), out_specs=[ANY] * (nm + nv),
                 scratch_shapes=[pltpu.SemaphoreType.DMA((nm, 6)), pltpu.SemaphoreType.DMA((nm, 6)),
                                 pltpu.SemaphoreType.DMA((max(nv, 1), 3)), pltpu.SemaphoreType.DMA((max(nv, 1), 3)),
                                 pltpu.SemaphoreType.DMA(())],
                 aliases={l: l for l in range(nm)}, side_effects=True)(*halves, *vecs)
    return [r.reshape(m.shape) for r, m in zip(res[:nm], mats)], list(res[nm:])


def pair_exchange(grads, *, name):
    n = len(grads)

    def body(*refs):
        src, got = refs[:n], refs[n:2 * n]
        send, recv = refs[2 * n:]
        x, y, c, _, _ = _place()

        def swap(l):
            return pltpu.make_async_remote_copy(src[l].at[:, 1 - c], got[l], send.at[l], recv.at[l],
                                                device_id=(x, y, 1 - c), device_id_type=MESH)

        for l in range(n):
            swap(l).start()
        for l in range(n):
            swap(l).wait()

    res = _pcall(body, name=name, out_shape=[_sds((N_SHARDS,) + g.shape[2:], F32) for g in grads],
                 in_specs=[ANY] * n, out_specs=[ANY] * n,
                 scratch_shapes=[pltpu.SemaphoreType.DMA((n,)), pltpu.SemaphoreType.DMA((n,))],
                 side_effects=True)(*grads)
    return list(res)


def add_to_wire(mine, theirs, core, *, name, tm=256):
    s, _, r, c = mine.shape
    tm = min(tm, r)

    def body(core_ref, a_ref, b_ref, o_ref):
        o_ref[...] = (a_ref[...] + b_ref[...]).astype(BF16)

    spec = pl.BlockSpec((None, tm, c), lambda i, j, cr: (i, j, 0))
    return _pcall(body, name=name, out_shape=_sds((s, r, c), BF16), grid=(s, r // tm), num_prefetch=1,
                  in_specs=[pl.BlockSpec((None, None, tm, c), lambda i, j, cr: (i, cr[0], j, 0)), spec],
                  out_specs=spec, semantics=("parallel", "parallel"))(core, mine, theirs)


def chip_exchange(parts, *, name):
    n = len(parts)

    def body(*refs):
        src, out = refs[:n], refs[n:2 * n]
        send, recv = refs[2 * n:]
        x, y, c, s_me, others = _place()

        def ici(l, k):
            ox, oy = others[k]
            return pltpu.make_async_remote_copy(src[l].at[2 * ox + oy], out[l].at[k], send.at[l, k], recv.at[l, k],
                                                device_id=(ox, oy, c), device_id_type=MESH)

        for l in range(n):
            for k in range(3):
                ici(l, k).start()
        for l in range(n):
            for k in range(3):
                ici(l, k).wait()

    res = _pcall(body, name=name, out_shape=[_sds((3,) + p.shape[1:], p.dtype) for p in parts], in_specs=[ANY] * n,
                 out_specs=[ANY] * n,
                 scratch_shapes=[pltpu.SemaphoreType.DMA((n, 3)), pltpu.SemaphoreType.DMA((n, 3))],
                 side_effects=True)(*parts)
    return list(res)


def sum_chips(wire, landed, place, dest, layer, n_layers, *, name, tm=256):
    _, r, c = wire.shape
    tm = min(tm, r)

    def body(place_ref, w_ref, l_ref, *rest):
        o_ref = rest[-1]
        o_ref[...] = ((w_ref[...].astype(F32) + l_ref[0].astype(F32)) + l_ref[1].astype(F32)) + l_ref[2].astype(F32)

    in_specs = [pl.BlockSpec((None, tm, c), lambda i, pr: (pr[0], i, 0)),
                pl.BlockSpec((3, tm, c), lambda i, pr: (0, i, 0))]
    args = [place, wire, landed]
    aliases = None
    if dest is not None:
        in_specs.append(ANY)
        args.append(dest)
        aliases = {3: 0}
    return _pcall(body, name=name, out_shape=_sds((n_layers, 2, r, c), F32), grid=(r // tm,), num_prefetch=1,
                  in_specs=in_specs,
                  out_specs=pl.BlockSpec((None, None, tm, c), lambda i, pr: (layer, pr[1], i, 0)),
                  aliases=aliases, semantics=("parallel",))(*args)


def pair_share(bufs, *, name):
    n = len(bufs)
    slots = [(o, l) for o, b in enumerate(bufs) for l in range(b.shape[0])]

    def body(*refs):
        out = refs[n:2 * n]
        send, recv = refs[2 * n:]
        x, y, c, _, _ = _place()

        def share(i, half):
            o, l = slots[i]
            return pltpu.make_async_remote_copy(out[o].at[l, half], out[o].at[l, half], send.at[i], recv.at[i],
                                                device_id=(x, y, 1 - c), device_id_type=MESH)

        for i in range(len(slots)):
            share(i, c).start()
        for i in range(len(slots)):
            share(i, 1 - c).wait_recv()
            share(i, c).wait_send()

    res = _pcall(body, name=name, out_shape=[_sds(b.shape, F32) for b in bufs], in_specs=[ANY] * n,
                 out_specs=[ANY] * n,
                 scratch_shapes=[pltpu.SemaphoreType.DMA((len(slots),)), pltpu.SemaphoreType.DMA((len(slots),))],
                 aliases={o: o for o in range(n)}, side_effects=True)(*bufs)
    return list(res)


def all_reduce_small(packed, *, name):
    n_dev, r, c = packed.shape

    def body(in_ref, out_ref, land, send, recv):
        x, y, cc, _, _ = _place()
        me = 4 * x + 2 * y + cc
        peers = [(px, py, pc) for px in range(2) for py in range(2) for pc in range(2)]

        def scatter(d):
            return pltpu.make_async_remote_copy(in_ref.at[d], land.at[me], send.at[0, d], recv.at[0, me],
                                                device_id=peers[d], device_id_type=MESH)

        def gather(d):
            return pltpu.make_async_remote_copy(out_ref.at[me], out_ref.at[me], send.at[1, d], recv.at[1, me],
                                                device_id=peers[d], device_id_type=MESH)

        for d in range(n_dev):
            @pl.when(d != me)
            def _():
                scatter(d).start()
        land[me] = in_ref[me]
        for d in range(n_dev):
            @pl.when(d != me)
            def _():
                pltpu.make_async_remote_copy(in_ref.at[d], land.at[d], send.at[0, d], recv.at[0, d],
                                             device_id=peers[d], device_id_type=MESH).wait_recv()
        total = land[0]
        for d in range(1, n_dev):
            total = total + land[d]
        out_ref[me] = total
        for d in range(n_dev):
            @pl.when(d != me)
            def _():
                gather(d).start()
        for d in range(n_dev):
            @pl.when(d != me)
            def _():
                pltpu.make_async_remote_copy(out_ref.at[d], out_ref.at[d], send.at[1, d], recv.at[1, d],
                                             device_id=peers[d], device_id_type=MESH).wait_recv()
        for d in range(n_dev):
            @pl.when(d != me)
            def _():
                scatter(d).wait_send()
                gather(d).wait_send()

    vm = pl.BlockSpec(memory_space=pltpu.VMEM)
    return _pcall(body, name=name, out_shape=_sds(packed.shape, F32), in_specs=[vm], out_specs=vm,
                  scratch_shapes=[pltpu.VMEM(packed.shape, F32), pltpu.SemaphoreType.DMA((2, n_dev)),
                                  pltpu.SemaphoreType.DMA((2, n_dev))],
                  side_effects=True)(packed)


def adamw(w, g, m, v, *, name, tm=256):
    shape = w.shape
    cols = shape[-1]
    rows = 1
    for s in shape[:-1]:
        rows *= s
    tm = min(tm, rows)
    assert rows % tm == 0
    two_d = lambda a: a.reshape(rows, cols)

    def body(w_ref, g_ref, m_ref, v_ref, d_ref, mo_ref, vo_ref):
        gv = g_ref[...]
        m_new = ADAM_B1 * m_ref[...] + (1.0 - ADAM_B1) * gv
        v_new = ADAM_B2 * v_ref[...] + (1.0 - ADAM_B2) * (gv * gv)
        m_hat = m_new / (1.0 - ADAM_B1 ** ADAM_STEP)
        v_hat = v_new / (1.0 - ADAM_B2 ** ADAM_STEP)
        d_ref[...] = -ADAM_LR * (m_hat / (jnp.sqrt(v_hat) + ADAM_EPS) + ADAM_WD * w_ref[...])
        mo_ref[...] = m_new
        vo_ref[...] = v_new

    spec = pl.BlockSpec((tm, cols), lambda i: (i, 0))
    outs = _pcall(body, name=name, out_shape=[_sds((rows, cols), F32)] * 3, grid=(rows // tm,), in_specs=[spec] * 4,
                  out_specs=[spec] * 3, semantics=("parallel",))(two_d(w), two_d(g), two_d(m), two_d(v))
    return [o.reshape(shape) for o in outs]


WEIGHTS = ("ln_mix_a", "w_in_a", "g_v_a", "w_spatial", "b_spatial", "w_out_a", "ln_kv", "w_kv", "g_k", "ln_mix_b",
           "w_q", "g_q", "w_out_b", "ln_mlp", "w_up", "w_down", "ln_ple", "w_ple_gate", "w_ple_proj")
MATRICES = (("w_in_a", 1, True), ("w_out_a", 1, False), ("w_kv", 0, True), ("w_q", 1, False), ("w_out_b", 1, False),
            ("w_up", 2, True), ("w_down", 2, False), ("w_ple_gate", 2, False), ("w_ple_proj", 2, True))
FIRST_LAYER = ("w_in_a", "w_out_a", "w_kv", "w_up", "w_down", "w_ple_gate", "w_ple_proj")
REPLICATED = ("w_spatial", "b_spatial", "ln_kv", "g_k", "ln_mix_b", "g_q", "ln_mlp", "ln_ple")
SHARDED_VECTORS = ("ln_mix_a", "g_v_a")
SMALL_ROWS = 18


def kernel(x, p, ln_mix_a, w_in_a, g_v_a, w_spatial, b_spatial, w_out_a, ln_kv, w_kv, g_k, ln_mix_b, w_q, g_q, w_out_b, ln_mlp, w_up, w_down, ln_ple, w_ple_gate, w_ple_proj, loss_target, m_ln_mix_a, m_w_in_a, m_g_v_a, m_w_spatial, m_b_spatial, m_w_out_a, m_ln_kv, m_w_kv, m_g_k, m_ln_mix_b, m_w_q, m_g_q, m_w_out_b, m_ln_mlp, m_w_up, m_w_down, m_ln_ple, m_w_ple_gate, m_w_ple_proj, v_ln_mix_a, v_w_in_a, v_g_v_a, v_w_spatial, v_b_spatial, v_w_out_a, v_ln_kv, v_w_kv, v_g_k, v_ln_mix_b, v_w_q, v_g_q, v_w_out_b, v_ln_mlp, v_w_up, v_w_down, v_ln_ple, v_w_ple_gate, v_w_ple_proj):
    given = dict(locals())
    weights = {n: given[n] for n in WEIGHTS}
    shard = 2 * lax.axis_index("x") + lax.axis_index("y")
    core = lax.axis_index("c")
    shard_1 = shard.astype(jnp.int32).reshape(1)
    core_1 = core.astype(jnp.int32).reshape(1)
    place = jnp.stack([shard, core]).astype(jnp.int32)

    leaves = []
    for name, layers, cols in MATRICES:
        w3 = weights[name] if layers else weights[name][None]
        for layer in range(max(layers, 1)):
            leaves.append((name, layer, cols, cast_into_slot(w3, layer, shard_1, name=f"cast_{name}_{layer}")))
    first = [lf for lf in leaves if lf[0] in FIRST_LAYER and lf[1] == 0]
    second = [lf for lf in leaves if not (lf[0] in FIRST_LAYER and lf[1] == 0)]
    got_a, vec_a = gather_shards([lf[3] for lf in first], [ln_mix_a, g_v_a], name="gather_layer0")
    got_b, _ = gather_shards([lf[3] for lf in second], [], name="gather_layer1")
    full = {}
    for (name, layer, cols, _), arr in zip(first + second, got_a + got_b):
        if not cols:
            arr = arr.reshape(N_SHARDS * arr.shape[1], arr.shape[2])
        full.setdefault(name, {})[layer] = arr
    w = {name: (tuple(full[name][l] for l in sorted(full[name])) if layers else full[name][0])
         for name, layers, _ in MATRICES}
    w["ln_mix_a"] = vec_a[0].reshape(1, D_MODEL)
    w["g_v_a"] = vec_a[1].reshape(1, D_MODEL)
    for name in REPLICATED:
        w[name] = weights[name]

    t = x.shape[1]
    loss_blk, dx, g = local_step(x[0], p.reshape(2, t, PLE_DIM), loss_target[0], w)
    loss = lax.psum(loss_blk[0, 0], ("x", "y", "c"))

    big = []
    for name, layers, cols in MATRICES:
        for layer in range(max(layers, 1)):
            arr = g[name][layer] if layers == 2 else g[name]
            rows = arr.shape[-2] if cols else arr.shape[0] // N_SHARDS
            big.append(arr.reshape(N_SHARDS, 2, rows // 2, arr.shape[-1]))
    theirs = pair_exchange(big, name="grad_pair_exchange")
    wire = [add_to_wire(a, b, core_1, name=f"grad_pair_sum_{i}") for i, (a, b) in enumerate(zip(big, theirs))]
    landed = chip_exchange(wire, name="grad_chip_exchange")
    bufs, i = [], 0
    for name, layers, _ in MATRICES:
        buf = None
        for layer in range(max(layers, 1)):
            buf = sum_chips(wire[i], landed[i], place, buf, layer, max(layers, 1), name=f"grad_chip_sum_{i}")
            i += 1
        bufs.append(buf)
    shared = pair_share(bufs, name="grad_pair_share")
    grads = {name: shared[o].reshape(weights[name].shape) for o, (name, _, _) in enumerate(MATRICES)}

    small = REPLICATED + SHARDED_VECTORS
    flat = jnp.concatenate([g[n].reshape(-1) for n in small])
    room = 8 * SMALL_ROWS * D_MODEL
    flat = jnp.concatenate([flat, jnp.zeros((room - flat.shape[0],), F32)])
    reduced = all_reduce_small(flat.reshape(8, SMALL_ROWS, D_MODEL), name="grad_small_all_reduce").reshape(-1)
    at = 0
    for n in small:
        size = g[n].size
        piece = reduced[at:at + size]
        at += size
        if n in SHARDED_VECTORS:
            per = D_MODEL // N_SHARDS
            grads[n] = lax.dynamic_slice(piece, (shard * per,), (per,)).reshape(weights[n].shape)
        else:
            grads[n] = piece.reshape(weights[n].shape)

    delta, new_m, new_v = {}, {}, {}
    for n in WEIGHTS:
        wn, gn, mn, vn = weights[n], grads[n], given["m_" + n], given["v_" + n]
        if wn.ndim == 1:
            wn, gn, mn, vn = (a.reshape(1, -1) for a in (wn, gn, mn, vn))
        outs = adamw(wn, gn, mn, vn, name=f"adamw_{n}")
        delta[n], new_m[n], new_v[n] = (o.reshape(weights[n].shape) for o in outs)
    return (loss, dx.reshape(x.shape), *[grads[n] for n in WEIGHTS], *[delta[n] for n in WEIGHTS],
            *[new_m[n] for n in WEIGHTS], *[new_v[n] for n in WEIGHTS])
```

```python
import jax
import jax.numpy as jnp
from jax import lax
from jax.experimental import pallas as pl
from jax.experimental.pallas import tpu as pltpu

F32 = jnp.float32
BF16 = jnp.bfloat16

D_MODEL = 1024
D_FF = 4096
PLE_DIM = 256
N_GROUPS = 8
CHUNK = 128
HEAD_DIM = 64
LANES = 128
ATT_BLOCK = 256
EPS = 1e-6
N_SHARDS = 4
VMEM_LIMIT = 56 * 1024 * 1024

ADAM_LR = 0.001
ADAM_B1 = 0.9
ADAM_B2 = 0.999
ADAM_EPS = 1e-08
ADAM_WD = 0.01
ADAM_STEP = 10

MESH = pl.DeviceIdType.MESH


def _pcall(body, *, name, out_shape, grid=None, in_specs=None, out_specs=None, scratch_shapes=(),
           semantics=None, aliases=None, side_effects=False, num_prefetch=0):
    params = dict(vmem_limit_bytes=VMEM_LIMIT)
    if semantics is not None:
        params["dimension_semantics"] = semantics
    if side_effects:
        params["has_side_effects"] = True
    kwargs = {}
    if aliases:
        kwargs["input_output_aliases"] = aliases
    if num_prefetch:
        spec = pltpu.PrefetchScalarGridSpec(num_scalar_prefetch=num_prefetch, grid=grid, in_specs=in_specs,
                                            out_specs=out_specs, scratch_shapes=list(scratch_shapes))
        return pl.pallas_call(body, name=name, out_shape=out_shape, grid_spec=spec,
                              compiler_params=pltpu.CompilerParams(**params), **kwargs)
    if grid is not None:
        kwargs["grid"] = grid
    if in_specs is not None:
        kwargs["in_specs"] = in_specs
    if out_specs is not None:
        kwargs["out_specs"] = out_specs
    if aliases:
        kwargs["input_output_aliases"] = aliases
    return pl.pallas_call(body, name=name, out_shape=out_shape, scratch_shapes=list(scratch_shapes),
                          compiler_params=pltpu.CompilerParams(**params), **kwargs)


def _sds(shape, dtype):
    return jax.ShapeDtypeStruct(shape, dtype)


_GELU_C = 0.7978845608028654
_GELU_A = 0.044715


def _gelu(x):
    inner = _GELU_C * (x + _GELU_A * (x * x * x))
    return 0.5 * x * (1.0 + jnp.tanh(inner))


def _gelu_grad(x):
    x2 = x * x
    t = jnp.tanh(_GELU_C * (x + _GELU_A * (x2 * x)))
    return 0.5 * (1.0 + t) + 0.5 * x * (1.0 - t * t) * (_GELU_C * (1.0 + 3.0 * _GELU_A * x2))


def _sigmoid(x):
    return 1.0 / (1.0 + jnp.exp(-x))


def _log_sigmoid(z):
    return jnp.minimum(z, 0.0) - jnp.log(1.0 + jnp.exp(-jnp.abs(z)))


def _split_bf16(a):
    hi = a.astype(BF16)
    lo = (a - hi.astype(F32)).astype(BF16)
    return hi, lo


def _dot(a, b):
    return jnp.dot(a, b, preferred_element_type=F32)


def _dot_nt(a, b):
    return lax.dot_general(a, b, (((1,), (1,)), ((), ())), preferred_element_type=F32)


def _dot_tn(a, b):
    return lax.dot_general(a, b, (((0,), (0,)), ((), ())), preferred_element_type=F32)


def _head_rstd(x):
    lane = lax.broadcasted_iota(jnp.int32, x.shape, 1)
    low = lane < HEAD_DIM
    sq = x * x
    s_lo = jnp.sum(jnp.where(low, sq, 0.0), axis=-1, keepdims=True)
    s_hi = jnp.sum(jnp.where(low, 0.0, sq), axis=-1, keepdims=True)
    ms = jnp.where(low, s_lo, s_hi) * (1.0 / HEAD_DIM)
    return lax.rsqrt(ms + EPS)


def _head_mean(x):
    lane = lax.broadcasted_iota(jnp.int32, x.shape, 1)
    low = lane < HEAD_DIM
    s_lo = jnp.sum(jnp.where(low, x, 0.0), axis=-1, keepdims=True)
    s_hi = jnp.sum(jnp.where(low, 0.0, x), axis=-1, keepdims=True)
    return jnp.where(low, s_lo, s_hi) * (1.0 / HEAD_DIM)


def _full(shape):
    zeros = (0,) * len(shape)
    return pl.BlockSpec(shape, lambda i: zeros)


def norm_matmul(x, g, w, *, name, epilogue="none", tm=512):
    t, d = x.shape
    sharded = w.ndim == 3
    per = w.shape[2] if sharded else w.shape[1]
    n = N_SHARDS * per if sharded else per
    tm = min(tm, t)

    def body(x_ref, g_ref, w_ref, h_ref, r_ref, *outs):
        xv = x_ref[...]
        r = lax.rsqrt(jnp.mean(xv * xv, axis=-1, keepdims=True) + EPS)
        h = ((xv * r) * g_ref[...]).astype(BF16)
        h_ref[...] = h
        r_ref[...] = r
        for s in range(N_SHARDS if sharded else 1):
            cols = slice(s * per, (s + 1) * per)
            y = _dot(h, w_ref[s] if sharded else w_ref[...])
            if epilogue == "none":
                outs[0][:, cols] = y
            else:
                a = jnp.maximum(y, 0.0)
                outs[0][:, cols] = a.astype(BF16)
                outs[1][:, cols] = (a * a).astype(BF16)

    row = lambda i: (i, 0)
    out_shape = [_sds((t, d), BF16), _sds((t, 1), F32)]
    out_specs = [pl.BlockSpec((tm, d), row), pl.BlockSpec((tm, 1), row)]
    if epilogue == "none":
        out_shape.append(_sds((t, n), F32))
        out_specs.append(pl.BlockSpec((tm, n), row))
    else:
        out_shape += [_sds((t, n), BF16), _sds((t, n), BF16)]
        out_specs += [pl.BlockSpec((tm, n), row)] * 2
    return _pcall(
        body, name=name, out_shape=out_shape, grid=(t // tm,),
        in_specs=[pl.BlockSpec((tm, d), row), _full((1, d)), _full(w.shape)],
        out_specs=out_specs, semantics=("parallel",))(x, g, w)


def matmul_residual(a, w, res, *, name, tm=512):
    t, k = a.shape
    n = w.shape[1]
    tm = min(tm, t)

    def body(a_ref, w_ref, res_ref, o_ref):
        o_ref[...] = res_ref[...] + _dot(a_ref[...], w_ref[...])

    row = lambda i: (i, 0)
    return _pcall(
        body, name=name, out_shape=_sds((t, n), F32), grid=(t // tm,),
        in_specs=[pl.BlockSpec((tm, k), row), _full(w.shape), pl.BlockSpec((tm, n), row)],
        out_specs=pl.BlockSpec((tm, n), row), semantics=("parallel",))(a, w, res)


def ple_forward(x, g, w_gate, p, w_proj, *, name, tm=256):
    t, d = x.shape
    tm = min(tm, t)

    def body(x_ref, g_ref, wg_ref, p_ref, wp_ref, h_ref, r_ref, gate_ref, pp_ref, o_ref):
        xv = x_ref[...]
        r = lax.rsqrt(jnp.mean(xv * xv, axis=-1, keepdims=True) + EPS)
        h = ((xv * r) * g_ref[...]).astype(BF16)
        h_ref[...] = h
        r_ref[...] = r
        gate = _sigmoid(_dot(h, wg_ref[...]))
        gate_ref[...] = gate
        pb = p_ref[...].astype(BF16)
        per = d // N_SHARDS
        for s in range(N_SHARDS):
            cols = slice(s * per, (s + 1) * per)
            pp = _dot(pb, wp_ref[s])
            pp_ref[:, cols] = pp.astype(BF16)
            o_ref[:, cols] = xv[:, cols] + pp * gate[:, cols]

    row = lambda i: (i, 0)
    fixed = lambda i: (0, 0)
    return _pcall(
        body, name=name,
        out_shape=[_sds((t, d), BF16), _sds((t, 1), F32), _sds((t, d), F32), _sds((t, d), BF16), _sds((t, d), F32)],
        grid=(t // tm,),
        in_specs=[pl.BlockSpec((tm, d), row), pl.BlockSpec((1, d), fixed), pl.BlockSpec((d, d), fixed),
                  pl.BlockSpec((tm, PLE_DIM), row),
                  pl.BlockSpec((N_SHARDS, PLE_DIM, d // N_SHARDS), lambda i: (0, 0, 0))],
        out_specs=[pl.BlockSpec((tm, d), row), pl.BlockSpec((tm, 1), row), pl.BlockSpec((tm, d), row),
                   pl.BlockSpec((tm, d), row), pl.BlockSpec((tm, d), row)],
        semantics=("parallel",))(x, g, w_gate, p, w_proj)


def _tril_mask():
    r = lax.broadcasted_iota(jnp.int32, (CHUNK, CHUNK), 0)
    c = lax.broadcasted_iota(jnp.int32, (CHUNK, CHUNK), 1)
    return c <= r


def _sgu_common(pre_ref, gv_ref, ws_ref):
    pre = pre_ref[...]
    pre_u, pre_v = pre[:, :D_MODEL], pre[:, D_MODEL:]
    u = _gelu(pre_u)
    v = _gelu(pre_v)
    r = lax.rsqrt(jnp.mean(v * v, axis=-1, keepdims=True) + EPS)
    vhat = v * r
    vn = (vhat * gv_ref[...]).astype(BF16)
    tril = _tril_mask()
    wm = [jnp.where(tril, ws_ref[g], 0.0).astype(BF16) for g in range(N_GROUPS)]
    return pre_u, pre_v, u, r, vhat, vn, wm, tril


def sgu_forward(pre, g_v, w_s, b_full, *, name):
    t = pre.shape[0]

    def body(pre_ref, gv_ref, ws_ref, b_ref, y_ref):
        _, _, u, _, _, vn, wm, _ = _sgu_common(pre_ref, gv_ref, ws_ref)
        for g in range(N_GROUPS):
            cols = slice(g * LANES, (g + 1) * LANES)
            mix = _dot(wm[g], vn[:, cols]) + b_ref[:, cols]
            y_ref[:, cols] = (u[:, cols] * mix).astype(BF16)

    return _pcall(
        body, name=name, out_shape=_sds((t, D_MODEL), BF16), grid=(t // CHUNK,),
        in_specs=[pl.BlockSpec((CHUNK, 2 * D_MODEL), lambda i: (i, 0)), pl.BlockSpec((1, D_MODEL), lambda i: (0, 0)),
                  pl.BlockSpec((N_GROUPS, CHUNK, CHUNK), lambda i: (0, 0, 0)),
                  pl.BlockSpec((CHUNK, D_MODEL), lambda i: (0, 0))],
        out_specs=pl.BlockSpec((CHUNK, D_MODEL), lambda i: (i, 0)),
        semantics=("parallel",))(pre, g_v, w_s, b_full)


def head_norm(pre, g128, *, name, col_block=0, scale=1.0, passthrough=False, tm=512):
    t = pre.shape[0]
    tm = min(tm, t)

    def body(*refs):
        if passthrough:
            x_ref, v_ref, g_ref, o_ref, vo_ref = refs
            vo_ref[...] = v_ref[...].astype(BF16)
        else:
            x_ref, g_ref, o_ref = refs
        g = g_ref[...] * scale
        for b in range(D_MODEL // LANES):
            cols = slice(b * LANES, (b + 1) * LANES)
            xv = x_ref[:, cols]
            o_ref[:, cols] = ((xv * _head_rstd(xv)) * g).astype(BF16)

    x_spec = pl.BlockSpec((tm, D_MODEL), lambda i: (i, col_block))
    g_spec = pl.BlockSpec((1, LANES), lambda i: (0, 0))
    o_spec = pl.BlockSpec((tm, D_MODEL), lambda i: (i, 0))
    if passthrough:
        return _pcall(body, name=name, out_shape=[_sds((t, D_MODEL), BF16)] * 2, grid=(t // tm,),
                      in_specs=[x_spec, pl.BlockSpec((tm, D_MODEL), lambda i: (i, 1)), g_spec],
                      out_specs=[o_spec, o_spec], semantics=("parallel",))(pre, pre, g128)
    return _pcall(body, name=name, out_shape=_sds((t, D_MODEL), BF16), grid=(t // tm,),
                  in_specs=[x_spec, g_spec], out_specs=o_spec, semantics=("parallel",))(pre, g128)


def _suffix_matrix(n):
    r = lax.broadcasted_iota(jnp.int32, (n, n), 0)
    c = lax.broadcasted_iota(jnp.int32, (n, n), 1)
    return jnp.where(r > c, 1.0, 0.0).astype(BF16)


def _prefix_matrix(n):
    r = lax.broadcasted_iota(jnp.int32, (n, n), 0)
    c = lax.broadcasted_iota(jnp.int32, (n, n), 1)
    return jnp.where(r < c, 1.0, 0.0).astype(BF16)


def _exact_cumsum(a, tri):
    rows = a.shape[0]
    hi, lo = _split_bf16(a)
    both = _dot(jnp.concatenate([hi, lo], axis=0), tri)
    return both[:rows] + both[rows:]


def _stacked_causal(n):
    r = lax.broadcasted_iota(jnp.int32, (2 * n, n), 0)
    c = lax.broadcasted_iota(jnp.int32, (2 * n, n), 1)
    return c < jnp.where(r >= n, r - n, r)


def _stack_heads(a, low):
    zero = jnp.zeros_like(a)
    return jnp.concatenate([jnp.where(low, a, zero), jnp.where(low, zero, a)], axis=0)


def stick_breaking_forward(q, k, v, *, name):
    t = q.shape[0]
    blk = min(ATT_BLOCK, t)
    nq = t // blk

    def body(q_ref, k_ref, v_ref, o_ref):
        i = pl.program_id(1)
        low = lax.broadcasted_iota(jnp.int32, (blk, LANES), 1) < HEAD_DIM
        tri = _suffix_matrix(blk)
        causal = _stacked_causal(blk)
        qs = _stack_heads(q_ref[...], low)

        def block(j, carry, acc, masked):
            rows = pl.ds(pl.multiple_of(j * blk, blk), blk)
            z = _dot_nt(qs, k_ref[rows, :])
            ls = _log_sigmoid(z)
            lg = ls - z
            if masked:
                lg = jnp.where(causal, lg, 0.0)
            s = ls + _exact_cumsum(lg, tri) + carry
            a = jnp.exp(s)
            if masked:
                a = jnp.where(causal, a, 0.0)
            acc = acc + _dot(a.astype(BF16), v_ref[rows, :])
            return carry + jnp.sum(lg, axis=-1, keepdims=True), acc

        carry, acc = block(i, jnp.zeros((2 * blk, 1), F32), jnp.zeros((2 * blk, LANES), F32), True)
        _, acc = lax.fori_loop(0, i, lambda n, st: block(i - 1 - n, st[0], st[1], False), (carry, acc))
        o_ref[...] = jnp.where(low, acc[:blk], acc[blk:]).astype(BF16)

    return _pcall(
        body, name=name, out_shape=_sds((t, D_MODEL), BF16), grid=(D_MODEL // LANES, nq),
        in_specs=[pl.BlockSpec((blk, LANES), lambda p, i: (i, p)), pl.BlockSpec((t, LANES), lambda p, i: (0, p)),
                  pl.BlockSpec((t, LANES), lambda p, i: (0, p))],
        out_specs=pl.BlockSpec((blk, LANES), lambda p, i: (i, p)),
        semantics=("parallel", "arbitrary"))(q, k, v)


def loss_forward(x, target, *, name, tm=512):
    t, d = x.shape
    tm = min(tm, t)

    def body(x_ref, t_ref, l_ref, dx_ref):
        @pl.when(pl.program_id(0) == 0)
        def _():
            l_ref[...] = jnp.zeros_like(l_ref)

        diff = x_ref[...] - t_ref[...]
        dx_ref[...] = diff * (1.0 / d)
        l_ref[...] += 0.5 * jnp.sum(jnp.mean(diff * diff, axis=-1, keepdims=True))

    return _pcall(
        body, name=name, out_shape=[_sds((8, LANES), F32), _sds((t, d), F32)], grid=(t // tm,),
        in_specs=[pl.BlockSpec((tm, d), lambda i: (i, 0))] * 2,
        out_specs=[pl.BlockSpec((8, LANES), lambda i: (0, 0)), pl.BlockSpec((tm, d), lambda i: (i, 0))],
        semantics=("arbitrary",))(x, target)


def matmul_nt(dy, w, *, name, mul=None, out_dtype=F32, tm=512):
    t, n = dy.shape
    k = w.shape[0]
    tm = min(tm, t)

    def body(*refs):
        if mul is None:
            dy_ref, w_ref, o_ref = refs
        else:
            dy_ref, w_ref, m_ref, o_ref = refs
        y = _dot_nt(dy_ref[...].astype(BF16), w_ref[...])
        if mul is not None:
            y = y * (2.0 * m_ref[...].astype(F32))
        o_ref[...] = y.astype(out_dtype)

    row = lambda i: (i, 0)
    in_specs = [pl.BlockSpec((tm, n), row), _full(w.shape)]
    args = [dy, w]
    if mul is not None:
        in_specs.append(pl.BlockSpec((tm, k), row))
        args.append(mul)
    return _pcall(body, name=name, out_shape=_sds((t, k), out_dtype), grid=(t // tm,), in_specs=in_specs,
                  out_specs=pl.BlockSpec((tm, k), row), semantics=("parallel",))(*args)


def matmul_tn(a, dy, *, name, col_shards, tk=512):
    t, k = a.shape
    n = dy.shape[1]
    if col_shards:
        tn = n // N_SHARDS

        def body(a_ref, dy_ref, o_ref):
            o_ref[...] = _dot_tn(a_ref[...].astype(BF16), dy_ref[...].astype(BF16))

        return _pcall(body, name=name, out_shape=_sds((N_SHARDS, k, tn), F32), grid=(N_SHARDS,),
                      in_specs=[_full((t, k)), pl.BlockSpec((t, tn), lambda j: (0, j))],
                      out_specs=pl.BlockSpec((None, k, tn), lambda j: (j, 0, 0)), semantics=("parallel",))(a, dy)

    tk = min(tk, k)

    def body(a_ref, dy_ref, o_ref, dy_bf):
        @pl.when(pl.program_id(0) == 0)
        def _():
            dy_bf[...] = dy_ref[...].astype(BF16)

        o_ref[...] = _dot_tn(a_ref[...].astype(BF16), dy_bf[...])

    return _pcall(body, name=name, out_shape=_sds((k, n), F32), grid=(k // tk,),
                  in_specs=[pl.BlockSpec((t, tk), lambda i: (0, i)), _full((t, n))],
                  out_specs=pl.BlockSpec((tk, n), lambda i: (i, 0)),
                  scratch_shapes=[pltpu.VMEM((t, n), BF16)], semantics=("arbitrary",))(a, dy)


def norm_backward(dpre, w, x, g, rstd, dx_out, *, name, tm=512):
    t, d = x.shape
    n = dpre.shape[1]
    tm = min(tm, t)
    if w.ndim == 3:
        w_spec = pl.BlockSpec(w.shape, lambda i: (0, 0, 0))
    else:
        w_spec = pl.BlockSpec(w.shape, lambda i: (0, 0))

    def body(dp_ref, w_ref, x_ref, g_ref, r_ref, dxo_ref, dx_ref, dg_ref):
        @pl.when(pl.program_id(0) == 0)
        def _():
            dg_ref[...] = jnp.zeros_like(dg_ref)

        if w.ndim == 3:
            per = n // N_SHARDS
            dh = _dot_nt(dp_ref[:, 0:per], w_ref[0])
            for s in range(1, N_SHARDS):
                dh = dh + _dot_nt(dp_ref[:, s * per:(s + 1) * per], w_ref[s])
        else:
            dh = _dot_nt(dp_ref[...], w_ref[...])
        r = r_ref[...]
        xn = x_ref[...] * r
        dg_ref[...] += jnp.sum(dh * xn, axis=0, keepdims=True)
        dxn = dh * g_ref[...]
        dx = r * (dxn - xn * jnp.mean(dxn * xn, axis=-1, keepdims=True))
        dx_ref[...] = dxo_ref[...] + dx

    row = lambda i: (i, 0)
    fixed = lambda i: (0, 0)
    return _pcall(
        body, name=name, out_shape=[_sds((t, d), F32), _sds((1, d), F32)], grid=(t // tm,),
        in_specs=[pl.BlockSpec((tm, n), row), w_spec, pl.BlockSpec((tm, d), row),
                  pl.BlockSpec((1, d), fixed), pl.BlockSpec((tm, 1), row), pl.BlockSpec((tm, d), row)],
        out_specs=[pl.BlockSpec((tm, d), row), pl.BlockSpec((1, d), fixed)],
        semantics=("arbitrary",))(dpre, w, x, g, rstd, dx_out)


def ple_backward(dx, gate, pp, *, name, tm=512):
    t, d = dx.shape
    tm = min(tm, t)

    def body(dx_ref, gate_ref, pp_ref, dg_ref, dp_ref):
        dxv = dx_ref[...]
        gate = gate_ref[...]
        dg_ref[...] = (dxv * pp_ref[...].astype(F32) * (gate * (1.0 - gate))).astype(BF16)
        dp_ref[...] = (dxv * gate).astype(BF16)

    spec = pl.BlockSpec((tm, d), lambda i: (i, 0))
    return _pcall(body, name=name, out_shape=[_sds((t, d), BF16)] * 2, grid=(t // tm,), in_specs=[spec] * 3,
                  out_specs=[spec] * 2, semantics=("parallel",))(dx, gate, pp)


def sgu_backward(dy, pre, g_v, w_s, b_full, *, name):
    t = pre.shape[0]
    n_chunks = t // CHUNK

    def body(dy_ref, pre_ref, gv_ref, ws_ref, b_ref, dpre_ref, dws_ref, db_ref, dgv_ref, dvn_s, dbf_s):
        step = pl.program_id(0)

        @pl.when(step == 0)
        def _():
            dws_ref[...] = jnp.zeros_like(dws_ref)
            dgv_ref[...] = jnp.zeros_like(dgv_ref)
            dbf_s[...] = jnp.zeros_like(dbf_s)

        pre_u, pre_v, u, r, vhat, vn, wm, tril = _sgu_common(pre_ref, gv_ref, ws_ref)
        dyv = dy_ref[...]
        for g in range(N_GROUPS):
            cols = slice(g * LANES, (g + 1) * LANES)
            mix = _dot(wm[g], vn[:, cols]) + b_ref[:, cols]
            dmix = dyv[:, cols] * u[:, cols]
            dmix_b = dmix.astype(BF16)
            du = dyv[:, cols] * mix
            dpre_ref[:, cols] = (du * _gelu_grad(pre_u[:, cols])).astype(BF16)
            dws_ref[g] += jnp.where(tril, _dot_nt(dmix_b, vn[:, cols]), 0.0)
            dbf_s[:, cols] += dmix
            dvn_s[:, cols] = _dot_tn(wm[g], dmix_b)
        dvn = dvn_s[...]
        dgv_ref[...] += jnp.sum(dvn * vhat, axis=0, keepdims=True)
        dxn = dvn * gv_ref[...]
        dv = r * (dxn - vhat * jnp.mean(dxn * vhat, axis=-1, keepdims=True))
        dpre_ref[:, D_MODEL:] = (dv * _gelu_grad(pre_v)).astype(BF16)

        @pl.when(step == n_chunks - 1)
        def _():
            lane = lax.broadcasted_iota(jnp.int32, (CHUNK, LANES), 1)
            acc = jnp.zeros((CHUNK, LANES), F32)
            for g in range(N_GROUPS):
                s = jnp.sum(dbf_s[:, g * LANES:(g + 1) * LANES], axis=-1, keepdims=True)
                acc = jnp.where(lane == g, s, acc)
            db_ref[...] = acc

    fixed2 = lambda i: (0, 0)
    return _pcall(
        body, name=name,
        out_shape=[_sds((t, 2 * D_MODEL), BF16), _sds((N_GROUPS, CHUNK, CHUNK), F32), _sds((CHUNK, LANES), F32),
                   _sds((1, D_MODEL), F32)],
        grid=(n_chunks,),
        in_specs=[pl.BlockSpec((CHUNK, D_MODEL), lambda i: (i, 0)), pl.BlockSpec((CHUNK, 2 * D_MODEL), lambda i: (i, 0)),
                  pl.BlockSpec((1, D_MODEL), fixed2), pl.BlockSpec((N_GROUPS, CHUNK, CHUNK), lambda i: (0, 0, 0)),
                  pl.BlockSpec((CHUNK, D_MODEL), fixed2)],
        out_specs=[pl.BlockSpec((CHUNK, 2 * D_MODEL), lambda i: (i, 0)),
                   pl.BlockSpec((N_GROUPS, CHUNK, CHUNK), lambda i: (0, 0, 0)), pl.BlockSpec((CHUNK, LANES), fixed2),
                   pl.BlockSpec((1, D_MODEL), fixed2)],
        scratch_shapes=[pltpu.VMEM((CHUNK, D_MODEL), F32), pltpu.VMEM((CHUNK, D_MODEL), F32)],
        semantics=("arbitrary",))(dy, pre, g_v, w_s, b_full)


def head_norm_backward(dy, pre, g128, *, name, col_block=0, scale=1.0, passthrough=None, tm=512):
    t = dy.shape[0]
    tm = min(tm, t)
    width = 2 * D_MODEL if passthrough is not None else D_MODEL

    def body(*refs):
        if passthrough is not None:
            dy_ref, x_ref, g_ref, dv_ref, o_ref, dg_ref = refs
            o_ref[:, D_MODEL:] = dv_ref[...].astype(BF16)
        else:
            dy_ref, x_ref, g_ref, o_ref, dg_ref = refs

        @pl.when(pl.program_id(0) == 0)
        def _():
            dg_ref[...] = jnp.zeros_like(dg_ref)

        g = g_ref[...]
        dg = jnp.zeros((1, LANES), F32)
        for b in range(D_MODEL // LANES):
            cols = slice(b * LANES, (b + 1) * LANES)
            xv = x_ref[:, cols]
            r = _head_rstd(xv)
            xn = xv * r
            dyv = dy_ref[:, cols] * scale
            dg = dg + jnp.sum(dyv * xn, axis=0, keepdims=True)
            dxn = dyv * g
            o_ref[:, cols] = (r * (dxn - xn * _head_mean(dxn * xn))).astype(BF16)
        dg_ref[...] += dg

    row = lambda i: (i, 0)
    in_specs = [pl.BlockSpec((tm, D_MODEL), row), pl.BlockSpec((tm, D_MODEL), lambda i: (i, col_block)),
                pl.BlockSpec((1, LANES), lambda i: (0, 0))]
    args = [dy, pre, g128]
    if passthrough is not None:
        in_specs.append(pl.BlockSpec((tm, D_MODEL), row))
        args.append(passthrough)
    return _pcall(body, name=name, out_shape=[_sds((t, width), BF16), _sds((1, LANES), F32)], grid=(t // tm,),
                  in_specs=in_specs,
                  out_specs=[pl.BlockSpec((tm, width), row), pl.BlockSpec((1, LANES), lambda i: (0, 0))],
                  semantics=("arbitrary",))(*args)


def stick_breaking_backward(q, k, v, do, *, name):
    t = q.shape[0]
    blk = min(ATT_BLOCK, t)
    nq = t // blk

    def body(q_ref, k_ref, v_ref, do_ref, dq_ref, dk_ref, dv_ref, s_buf, sg_buf):
        i = pl.program_id(1)

        @pl.when(i == 0)
        def _():
            dk_ref[...] = jnp.zeros_like(dk_ref)
            dv_ref[...] = jnp.zeros_like(dv_ref)

        low = lax.broadcasted_iota(jnp.int32, (blk, LANES), 1) < HEAD_DIM
        suffix = _suffix_matrix(blk)
        prefix = _prefix_matrix(blk)
        causal = _stacked_causal(blk)
        qs = _stack_heads(q_ref[...], low)
        dos = _stack_heads(do_ref[...], low)

        def log_weights(j, carry, masked):
            rows = pl.ds(pl.multiple_of(j * blk, blk), blk)
            z = _dot_nt(qs, k_ref[rows, :])
            ls = _log_sigmoid(z)
            lg = ls - z
            if masked:
                lg = jnp.where(causal, lg, 0.0)
            s_buf[j] = ls + _exact_cumsum(lg, suffix) + carry
            sg_buf[j] = jnp.exp(ls)
            return carry + jnp.sum(lg, axis=-1, keepdims=True)

        carry = log_weights(i, jnp.zeros((2 * blk, 1), F32), True)
        lax.fori_loop(0, i, lambda n, c: log_weights(i - 1 - n, c, False), carry)

        def grads(j, pcarry, dq_acc, masked):
            rows = pl.ds(pl.multiple_of(j * blk, blk), blk)
            a = jnp.exp(s_buf[j])
            if masked:
                a = jnp.where(causal, a, 0.0)
            sg = sg_buf[j]
            ds = _dot_nt(dos, v_ref[rows, :]) * a
            before = _exact_cumsum(ds, prefix) + pcarry
            if masked:
                before = jnp.where(causal, before, 0.0)
            dz = (ds - sg * (ds + before)).astype(BF16)
            dq_acc = dq_acc + _dot(dz, k_ref[rows, :])
            dk_ref[rows, :] += _dot_tn(dz, qs)
            dv_ref[rows, :] += _dot_tn(a.astype(BF16), dos)
            return pcarry + jnp.sum(ds, axis=-1, keepdims=True), dq_acc

        state = lax.fori_loop(0, i, lambda j, st: grads(j, st[0], st[1], False),
                              (jnp.zeros((2 * blk, 1), F32), jnp.zeros((2 * blk, LANES), F32)))
        _, dq_acc = grads(i, state[0], state[1], True)
        dq_ref[...] = jnp.where(low, dq_acc[:blk], dq_acc[blk:])

    full = pl.BlockSpec((t, LANES), lambda p, i: (0, p))
    qblk = pl.BlockSpec((blk, LANES), lambda p, i: (i, p))
    return _pcall(
        body, name=name, out_shape=[_sds((t, D_MODEL), F32)] * 3, grid=(D_MODEL // LANES, nq),
        in_specs=[qblk, full, full, qblk], out_specs=[qblk, full, full],
        scratch_shapes=[pltpu.VMEM((nq, 2 * blk, blk), F32), pltpu.VMEM((nq, 2 * blk, blk), F32)],
        semantics=("parallel", "arbitrary"))(q, k, v, do)


def _mlp_forward(x, g, w_up, w_down, tag):
    h, r, a, a2 = norm_matmul(x, g, w_up, name=f"mlp_up_{tag}", epilogue="relu2")
    return matmul_residual(a2, w_down, x, name=f"mlp_down_{tag}"), (x, h, r, a, a2)


def _mlp_backward(dx, saved, g, w_up, w_down, tag):
    x, h, r, a, a2 = saved
    d_w_down = matmul_tn(a2, dx, name=f"d_w_down_{tag}", col_shards=False)
    dpre = matmul_nt(dx, w_down, name=f"d_mlp_act_{tag}", mul=a, out_dtype=BF16)
    d_w_up = matmul_tn(h, dpre, name=f"d_w_up_{tag}", col_shards=True)
    dx, d_g = norm_backward(dpre, w_up, x, g, r, dx, name=f"d_mlp_norm_{tag}")
    return dx, d_w_up, d_w_down, d_g


def _ple_backward(dx, saved, p, g, w_gate, tag):
    x, h, r, gate, pp = saved
    dgate, dproj = ple_backward(dx, gate, pp, name=f"d_ple_{tag}")
    d_w_proj = matmul_tn(p, dproj, name=f"d_w_ple_proj_{tag}", col_shards=True)
    d_w_gate = matmul_tn(h, dgate, name=f"d_w_ple_gate_{tag}", col_shards=False)
    dx, d_g = norm_backward(dgate, w_gate, x, g, r, dx, name=f"d_ple_norm_{tag}")
    return dx, d_w_gate, d_w_proj, d_g


def local_step(x, p, target, w, late=None):
    row = lambda v: v.reshape(1, -1)
    g128 = lambda v: jnp.tile(v.reshape(1, HEAD_DIM), (1, 2))
    scale = HEAD_DIM ** -0.5
    b_full = jnp.repeat(jnp.transpose(w["b_spatial"][0]), LANES, axis=1)
    w_s = w["w_spatial"][0]

    x0 = x
    h_a, r_a, pre_a = norm_matmul(x0, row(w["ln_mix_a"][0]), w["w_in_a"][0], name="sgu_in")
    y_a = sgu_forward(pre_a, row(w["g_v_a"][0]), w_s, b_full, name="sgu_mix")
    x1 = matmul_residual(y_a, w["w_out_a"][0], x0, name="sgu_out")
    x2, mlp0 = _mlp_forward(x1, row(w["ln_mlp"][0]), w["w_up"][0], w["w_down"][0], 0)
    ple0 = ple_forward(x2, row(w["ln_ple"][0]), w["w_ple_gate"][0], p[0], w["w_ple_proj"][0], name="ple_0")
    x3 = ple0[4]
    if late is not None:
        late.after_first_layer(x3)
    h_kv, r_kv, kv_pre = norm_matmul(x3, row(w["ln_kv"]), w["w_kv"], name="kv_proj")
    k_n, v_b = head_norm(kv_pre, g128(w["g_k"]), name="k_norm", passthrough=True)
    if late is not None:
        w = {**w, **late.second_layer_weights(k_n)}
    h_q, r_q, q_pre = norm_matmul(x3, row(w["ln_mix_b"][0]), w["w_q"][0], name="q_proj")
    q_n = head_norm(q_pre, g128(w["g_q"][0]), name="q_norm", scale=scale)
    o = stick_breaking_forward(q_n, k_n, v_b, name="sb_fwd")
    x4 = matmul_residual(o, w["w_out_b"][0], x3, name="sb_out")
    x5, mlp1 = _mlp_forward(x4, row(w["ln_mlp"][1]), w["w_up"][1], w["w_down"][1], 1)
    ple1 = ple_forward(x5, row(w["ln_ple"][1]), w["w_ple_gate"][1], p[1], w["w_ple_proj"][1], name="ple_1")
    x6 = ple1[4]
    loss_blk, dx = loss_forward(x6, target, name="loss")

    g = {}
    dx, dwg1, dwp1, dlnp1 = _ple_backward(dx, (x5,) + tuple(ple1[:4]), p[1], row(w["ln_ple"][1]), w["w_ple_gate"][1], 1)
    dx, dwu1, dwd1, dlnm1 = _mlp_backward(dx, mlp1, row(w["ln_mlp"][1]), w["w_up"][1], w["w_down"][1], 1)
    g["w_out_b"] = matmul_tn(o, dx, name="d_w_out_b", col_shards=False)
    do = matmul_nt(dx, w["w_out_b"][0], name="d_sb_out", out_dtype=BF16)
    dq_n, dk_n, dv = stick_breaking_backward(q_n, k_n, v_b, do, name="sb_bwd")
    dq_pre, dgq = head_norm_backward(dq_n, q_pre, g128(w["g_q"][0]), name="d_q_norm", scale=scale)
    dkv_pre, dgk = head_norm_backward(dk_n, kv_pre, g128(w["g_k"]), name="d_k_norm", passthrough=dv)
    g["w_q"] = matmul_tn(h_q, dq_pre, name="d_w_q", col_shards=False)
    g["w_kv"] = matmul_tn(h_kv, dkv_pre, name="d_w_kv", col_shards=True)
    dx, g["ln_mix_b"] = norm_backward(dq_pre, w["w_q"][0], x3, row(w["ln_mix_b"][0]), r_q, dx, name="d_q_in")
    dx, g["ln_kv"] = norm_backward(dkv_pre, w["w_kv"], x3, row(w["ln_kv"]), r_kv, dx, name="d_kv_in")
    g["g_q"] = dgq[:, :HEAD_DIM] + dgq[:, HEAD_DIM:]
    g["g_k"] = (dgk[:, :HEAD_DIM] + dgk[:, HEAD_DIM:]).reshape(HEAD_DIM)
    g["ln_kv"] = g["ln_kv"].reshape(D_MODEL)
    dx, dwg0, dwp0, dlnp0 = _ple_backward(dx, (x2,) + tuple(ple0[:4]), p[0], row(w["ln_ple"][0]), w["w_ple_gate"][0], 0)
    dx, dwu0, dwd0, dlnm0 = _mlp_backward(dx, mlp0, row(w["ln_mlp"][0]), w["w_up"][0], w["w_down"][0], 0)
    g["w_out_a"] = matmul_tn(y_a, dx, name="d_w_out_a", col_shards=False)
    dy_a = matmul_nt(dx, w["w_out_a"][0], name="d_sgu_out")
    dpre_a, dws, db, g["g_v_a"] = sgu_backward(dy_a, pre_a, row(w["g_v_a"][0]), w_s, b_full, name="d_sgu_mix")
    g["w_in_a"] = matmul_tn(h_a, dpre_a, name="d_w_in_a", col_shards=True)
    dx, g["ln_mix_a"] = norm_backward(dpre_a, w["w_in_a"][0], x0, row(w["ln_mix_a"][0]), r_a, dx, name="d_sgu_in")
    g["w_spatial"] = dws[None]
    g["b_spatial"] = jnp.transpose(db[:, :N_GROUPS])[None]
    g["w_up"] = (dwu0, dwu1)
    g["w_down"] = (dwd0, dwd1)
    g["w_ple_gate"] = (dwg0, dwg1)
    g["w_ple_proj"] = (dwp0, dwp1)
    g["ln_mlp"] = jnp.concatenate([dlnm0, dlnm1], axis=0)
    g["ln_ple"] = jnp.concatenate([dlnp0, dlnp1], axis=0)
    return loss_blk, dx, g


ANY = pl.BlockSpec(memory_space=pl.ANY)


def _place():
    x, y, c = lax.axis_index("x"), lax.axis_index("y"), lax.axis_index("c")
    others = [(1 - x, y), (x, 1 - y), (1 - x, 1 - y)]
    return x, y, c, 2 * x + y, others


def cast_into_slot(w3, layer, slot, *, name, tm=256):
    _, r, c = w3.shape
    tm = min(tm, r)

    def body(slot_ref, w_ref, o_ref):
        o_ref[...] = w_ref[...].astype(BF16)

    return _pcall(body, name=name, out_shape=_sds((N_SHARDS, r, c), BF16), grid=(r // tm,), num_prefetch=1,
                  in_specs=[pl.BlockSpec((None, tm, c), lambda i, s: (layer, i, 0))],
                  out_specs=pl.BlockSpec((None, tm, c), lambda i, s: (s[0], i, 0)),
                  semantics=("parallel",))(slot, w3)


def gather_shards(mats, vecs, *, name):
    nm, nv = len(mats), len(vecs)
    halves = [m.reshape(N_SHARDS, 2, m.shape[1] // 2, m.shape[2]) for m in mats]

    def body(*refs):
        vsrc = refs[nm:nm + nv]
        out, vout = refs[nm + nv:2 * nm + nv], refs[2 * nm + nv:2 * (nm + nv)]
        send, recv, vsend, vrecv, loc = refs[2 * (nm + nv):]
        x, y, c, s_me, others = _place()
        sib = (x, y, 1 - c)

        def ici(l, k):
            ox, oy = others[k]
            return pltpu.make_async_remote_copy(out[l].at[s_me, c], out[l].at[s_me, c], send.at[l, k], recv.at[l, k],
                                                device_id=(ox, oy, c), device_id_type=MESH)

        def landed(l, k, half):
            ox, oy = others[k]
            return out[l].at[2 * ox + oy, half]

        def passed_on(l, k):
            return pltpu.make_async_remote_copy(landed(l, k, c), landed(l, k, c), send.at[l, 3 + k], recv.at[l, 3 + k],
                                                device_id=sib, device_id_type=MESH)

        def vec(l, k):
            ox, oy = others[k]
            return pltpu.make_async_remote_copy(vsrc[l], vout[l].at[s_me], vsend.at[l, k], vrecv.at[l, k],
                                                device_id=(ox, oy, c), device_id_type=MESH)

        for l in range(nm):
            for k in range(3):
                ici(l, k).start()
        for l in range(nv):
            for k in range(3):
                vec(l, k).start()
        for l in range(nv):
            own = pltpu.make_async_copy(vsrc[l], vout[l].at[s_me], loc)
            own.start()
            own.wait()
        for l in range(nm):
            for k in range(3):
                pltpu.make_async_remote_copy(landed(l, k, c), landed(l, k, c), send.at[l, k], recv.at[l, k],
                                             device_id=sib, device_id_type=MESH).wait_recv()
                passed_on(l, k).start()
        for l in range(nm):
            for k in range(3):
                pltpu.make_async_remote_copy(landed(l, k, 1 - c), landed(l, k, 1 - c), send.at[l, 3 + k],
                                             recv.at[l, 3 + k], device_id=sib, device_id_type=MESH).wait_recv()
        for l in range(nv):
            for k in range(3):
                ox, oy = others[k]
                pltpu.make_async_remote_copy(vsrc[l], vout[l].at[2 * ox + oy], vsend.at[l, k], vrecv.at[l, k],
                                             device_id=sib, device_id_type=MESH).wait_recv()
        for l in range(nm):
            for k in range(3):
                ici(l, k).wait_send()
                passed_on(l, k).wait_send()
        for l in range(nv):
            for k in range(3):
                vec(l, k).wait_send()

    out_shape = [_sds(h.shape, BF16) for h in halves] + [_sds((N_SHARDS,) + v.shape, F32) for v in vecs]
    res = _pcall(body, name=name, out_shape=out_shape, in_specs=[ANY] * (nm + nv), out_specs=[ANY] * (nm + nv),
                 scratch_shapes=[pltpu.SemaphoreType.DMA((nm, 6)), pltpu.SemaphoreType.DMA((nm, 6)),
                                 pltpu.SemaphoreType.DMA((max(nv, 1), 3)), pltpu.SemaphoreType.DMA((max(nv, 1), 3)),
                                 pltpu.SemaphoreType.DMA(())],
                 aliases={l: l for l in range(nm)}, side_effects=True)(*halves, *vecs)
    return [r.reshape(m.shape) for r, m in zip(res[:nm], mats)], list(res[nm:])


HBM = pl.BlockSpec(memory_space=pltpu.HBM)
SEM = pl.BlockSpec(memory_space=pltpu.SEMAPHORE)
DATAFLOW = pltpu.SideEffectType.DATAFLOW_SIDE_EFFECTING


def _split_call(body, *, name, out_shape, in_specs, out_specs, aliases):
    return pl.pallas_call(body, name=name, out_shape=out_shape, in_specs=in_specs, out_specs=out_specs,
                          input_output_aliases=aliases,
                          compiler_params=pltpu.CompilerParams(has_side_effects=DATAFLOW))


def _token_shape():
    return jax.ShapeDtypeStruct((8, LANES), F32)


def gather_start(mats, after, *, name):
    n = len(mats)
    halves = [pltpu.with_memory_space_constraint(m.reshape(N_SHARDS, 2, m.shape[1] // 2, m.shape[2]), pltpu.HBM)
              for m in mats]

    def body(*refs):
        send, recv = refs[n + 1], refs[n + 2]
        out, token = refs[n + 3:2 * n + 3], refs[2 * n + 3]
        x, y, c, s_me, others = _place()
        for l in range(n):
            for k in range(3):
                ox, oy = others[k]
                pltpu.make_async_remote_copy(out[l].at[s_me, c], out[l].at[s_me, c], send.at[3 * l + k],
                                             recv.at[3 * l + k], device_id=(ox, oy, c), device_id_type=MESH).start()
        token[...] = jnp.zeros_like(token)

    res = _split_call(
        body, name=name,
        out_shape=(pltpu.SemaphoreType.DMA((3 * n,)), pltpu.SemaphoreType.DMA((3 * n,)),
                   *[pltpu.HBM(h.shape, BF16) for h in halves], _token_shape()),
        in_specs=[HBM] * n + [ANY], out_specs=(SEM, SEM, *[HBM] * n, pl.BlockSpec(memory_space=pltpu.VMEM)),
        aliases={l: 2 + l for l in range(n)})(*halves, after)
    return res[0], res[1], list(res[2:2 + n]), res[2 + n]


def gather_pass_on(bufs, send_a, recv_a, after, *, name):
    n = len(bufs)

    def body(*refs):
        send_a, recv_a = refs[n], refs[n + 1]
        out = refs[n + 3:2 * n + 3]
        send_b, recv_b, token = refs[2 * n + 3:]
        x, y, c, s_me, others = _place()
        for l in range(n):
            for k in range(3):
                ox, oy = others[k]
                landed, i = out[l].at[2 * ox + oy, c], 3 * l + k
                pltpu.make_async_remote_copy(landed, landed, send_a.at[i], recv_a.at[i],
                                             device_id=(x, y, 1 - c), device_id_type=MESH).wait_recv()
                pltpu.make_async_remote_copy(landed, landed, send_b.at[i], recv_b.at[i],
                                             device_id=(x, y, 1 - c), device_id_type=MESH).start()
        for l in range(n):
            for k in range(3):
                mine, i = out[l].at[s_me, c], 3 * l + k
                pltpu.make_async_remote_copy(mine, mine, send_a.at[i], recv_a.at[i],
                                             device_id=(x, y, 1 - c), device_id_type=MESH).wait_send()
        token[...] = jnp.zeros_like(token)

    res = _split_call(
        body, name=name,
        out_shape=(*[pltpu.HBM(b.shape, BF16) for b in bufs], pltpu.SemaphoreType.DMA((3 * n,)),
                   pltpu.SemaphoreType.DMA((3 * n,)), _token_shape()),
        in_specs=[HBM] * n + [SEM, SEM, ANY],
        out_specs=(*[HBM] * n, SEM, SEM, pl.BlockSpec(memory_space=pltpu.VMEM)),
        aliases={l: l for l in range(n)})(*bufs, send_a, recv_a, after)
    return list(res[:n]), res[n], res[n + 1], res[n + 2]


def gather_finish(bufs, send_b, recv_b, after, shapes, *, name):
    n = len(bufs)

    def body(*refs):
        send_b, recv_b = refs[n], refs[n + 1]
        out = refs[n + 3:]
        x, y, c, _, others = _place()
        for l in range(n):
            for k in range(3):
                ox, oy = others[k]
                theirs, mine, i = out[l].at[2 * ox + oy, 1 - c], out[l].at[2 * ox + oy, c], 3 * l + k
                pltpu.make_async_remote_copy(theirs, theirs, send_b.at[i], recv_b.at[i],
                                             device_id=(x, y, 1 - c), device_id_type=MESH).wait_recv()
                pltpu.make_async_remote_copy(mine, mine, send_b.at[i], recv_b.at[i],
                                             device_id=(x, y, 1 - c), device_id_type=MESH).wait_send()

    res = _split_call(
        body, name=name, out_shape=tuple(pltpu.HBM(b.shape, BF16) for b in bufs),
        in_specs=[HBM] * n + [SEM, SEM, ANY], out_specs=tuple([HBM] * n),
        aliases={l: l for l in range(n)})(*bufs, send_b, recv_b, after)
    return [r.reshape(s) for r, s in zip(res, shapes)]


def pair_exchange(grads, *, name):
    n = len(grads)

    def body(*refs):
        src, got = refs[:n], refs[n:2 * n]
        send, recv = refs[2 * n:]
        x, y, c, _, _ = _place()

        def swap(l):
            return pltpu.make_async_remote_copy(src[l].at[:, 1 - c], got[l], send.at[l], recv.at[l],
                                                device_id=(x, y, 1 - c), device_id_type=MESH)

        for l in range(n):
            swap(l).start()
        for l in range(n):
            swap(l).wait()

    res = _pcall(body, name=name, out_shape=[_sds((N_SHARDS,) + g.shape[2:], F32) for g in grads],
                 in_specs=[ANY] * n, out_specs=[ANY] * n,
                 scratch_shapes=[pltpu.SemaphoreType.DMA((n,)), pltpu.SemaphoreType.DMA((n,))],
                 side_effects=True)(*grads)
    return list(res)


def add_to_wire(mine, theirs, core, *, name, tm=256):
    s, _, r, c = mine.shape
    tm = min(tm, r)

    def body(core_ref, a_ref, b_ref, o_ref):
        o_ref[...] = (a_ref[...] + b_ref[...]).astype(BF16)

    spec = pl.BlockSpec((None, tm, c), lambda i, j, cr: (i, j, 0))
    return _pcall(body, name=name, out_shape=_sds((s, r, c), BF16), grid=(s, r // tm), num_prefetch=1,
                  in_specs=[pl.BlockSpec((None, None, tm, c), lambda i, j, cr: (i, cr[0], j, 0)), spec],
                  out_specs=spec, semantics=("parallel", "parallel"))(core, mine, theirs)


def chip_exchange(parts, *, name):
    n = len(parts)

    def body(*refs):
        src, out = refs[:n], refs[n:2 * n]
        send, recv = refs[2 * n:]
        x, y, c, s_me, others = _place()

        def ici(l, k):
            ox, oy = others[k]
            return pltpu.make_async_remote_copy(src[l].at[2 * ox + oy], out[l].at[k], send.at[l, k], recv.at[l, k],
                                                device_id=(ox, oy, c), device_id_type=MESH)

        for l in range(n):
            for k in range(3):
                ici(l, k).start()
        for l in range(n):
            for k in range(3):
                ici(l, k).wait()

    res = _pcall(body, name=name, out_shape=[_sds((3,) + p.shape[1:], p.dtype) for p in parts], in_specs=[ANY] * n,
                 out_specs=[ANY] * n,
                 scratch_shapes=[pltpu.SemaphoreType.DMA((n, 3)), pltpu.SemaphoreType.DMA((n, 3))],
                 side_effects=True)(*parts)
    return list(res)


def sum_chips(wire, landed, place, dest, layer, n_layers, *, name, tm=256):
    _, r, c = wire.shape
    tm = min(tm, r)

    def body(place_ref, w_ref, l_ref, *rest):
        o_ref = rest[-1]
        o_ref[...] = ((w_ref[...].astype(F32) + l_ref[0].astype(F32)) + l_ref[1].astype(F32)) + l_ref[2].astype(F32)

    in_specs = [pl.BlockSpec((None, tm, c), lambda i, pr: (pr[0], i, 0)),
                pl.BlockSpec((3, tm, c), lambda i, pr: (0, i, 0))]
    args = [place, wire, landed]
    aliases = None
    if dest is not None:
        in_specs.append(ANY)
        args.append(dest)
        aliases = {3: 0}
    return _pcall(body, name=name, out_shape=_sds((n_layers, 2, r, c), F32), grid=(r // tm,), num_prefetch=1,
                  in_specs=in_specs,
                  out_specs=pl.BlockSpec((None, None, tm, c), lambda i, pr: (layer, pr[1], i, 0)),
                  aliases=aliases, semantics=("parallel",))(*args)


def pair_share(bufs, *, name):
    n = len(bufs)
    slots = [(o, l) for o, b in enumerate(bufs) for l in range(b.shape[0])]

    def body(*refs):
        out = refs[n:2 * n]
        send, recv = refs[2 * n:]
        x, y, c, _, _ = _place()

        def share(i, half):
            o, l = slots[i]
            return pltpu.make_async_remote_copy(out[o].at[l, half], out[o].at[l, half], send.at[i], recv.at[i],
                                                device_id=(x, y, 1 - c), device_id_type=MESH)

        for i in range(len(slots)):
            share(i, c).start()
        for i in range(len(slots)):
            share(i, 1 - c).wait_recv()
            share(i, c).wait_send()

    res = _pcall(body, name=name, out_shape=[_sds(b.shape, F32) for b in bufs], in_specs=[ANY] * n,
                 out_specs=[ANY] * n,
                 scratch_shapes=[pltpu.SemaphoreType.DMA((len(slots),)), pltpu.SemaphoreType.DMA((len(slots),))],
                 aliases={o: o for o in range(n)}, side_effects=True)(*bufs)
    return list(res)


def all_reduce_small(packed, *, name):
    n_dev, r, c = packed.shape

    def body(in_ref, out_ref, land, send, recv):
        x, y, cc, _, _ = _place()
        me = 4 * x + 2 * y + cc
        peers = [(px, py, pc) for px in range(2) for py in range(2) for pc in range(2)]

        def scatter(d):
            return pltpu.make_async_remote_copy(in_ref.at[d], land.at[me], send.at[0, d], recv.at[0, me],
                                                device_id=peers[d], device_id_type=MESH)

        def gather(d):
            return pltpu.make_async_remote_copy(out_ref.at[me], out_ref.at[me], send.at[1, d], recv.at[1, me],
                                                device_id=peers[d], device_id_type=MESH)

        for d in range(n_dev):
            @pl.when(d != me)
            def _():
                scatter(d).start()
        land[me] = in_ref[me]
        for d in range(n_dev):
            @pl.when(d != me)
            def _():
                pltpu.make_async_remote_copy(in_ref.at[d], land.at[d], send.at[0, d], recv.at[0, d],
                                             device_id=peers[d], device_id_type=MESH).wait_recv()
        total = land[0]
        for d in range(1, n_dev):
            total = total + land[d]
        out_ref[me] = total
        for d in range(n_dev):
            @pl.when(d != me)
            def _():
                gather(d).start()
        for d in range(n_dev):
            @pl.when(d != me)
            def _():
                pltpu.make_async_remote_copy(out_ref.at[d], out_ref.at[d], send.at[1, d], recv.at[1, d],
                                             device_id=peers[d], device_id_type=MESH).wait_recv()
        for d in range(n_dev):
            @pl.when(d != me)
            def _():
                scatter(d).wait_send()
                gather(d).wait_send()

    vm = pl.BlockSpec(memory_space=pltpu.VMEM)
    return _pcall(body, name=name, out_shape=_sds(packed.shape, F32), in_specs=[vm], out_specs=vm,
                  scratch_shapes=[pltpu.VMEM(packed.shape, F32), pltpu.SemaphoreType.DMA((2, n_dev)),
                                  pltpu.SemaphoreType.DMA((2, n_dev))],
                  side_effects=True)(packed)


def adamw(w, g, m, v, *, name, tm=256):
    shape = w.shape
    cols = shape[-1]
    rows = 1
    for s in shape[:-1]:
        rows *= s
    tm = min(tm, rows)
    assert rows % tm == 0
    two_d = lambda a: a.reshape(rows, cols)

    def body(w_ref, g_ref, m_ref, v_ref, d_ref, mo_ref, vo_ref):
        gv = g_ref[...]
        m_new = ADAM_B1 * m_ref[...] + (1.0 - ADAM_B1) * gv
        v_new = ADAM_B2 * v_ref[...] + (1.0 - ADAM_B2) * (gv * gv)
        m_hat = m_new / (1.0 - ADAM_B1 ** ADAM_STEP)
        v_hat = v_new / (1.0 - ADAM_B2 ** ADAM_STEP)
        d_ref[...] = -ADAM_LR * (m_hat / (jnp.sqrt(v_hat) + ADAM_EPS) + ADAM_WD * w_ref[...])
        mo_ref[...] = m_new
        vo_ref[...] = v_new

    spec = pl.BlockSpec((tm, cols), lambda i: (i, 0))
    outs = _pcall(body, name=name, out_shape=[_sds((rows, cols), F32)] * 3, grid=(rows // tm,), in_specs=[spec] * 4,
                  out_specs=[spec] * 3, semantics=("parallel",))(two_d(w), two_d(g), two_d(m), two_d(v))
    return [o.reshape(shape) for o in outs]


WEIGHTS = ("ln_mix_a", "w_in_a", "g_v_a", "w_spatial", "b_spatial", "w_out_a", "ln_kv", "w_kv", "g_k", "ln_mix_b",
           "w_q", "g_q", "w_out_b", "ln_mlp", "w_up", "w_down", "ln_ple", "w_ple_gate", "w_ple_proj")
MATRICES = (("w_in_a", 1, True), ("w_out_a", 1, False), ("w_kv", 0, True), ("w_q", 1, False), ("w_out_b", 1, False),
            ("w_up", 2, True), ("w_down", 2, False), ("w_ple_gate", 2, False), ("w_ple_proj", 2, True))
FIRST_LAYER = ("w_in_a", "w_out_a", "w_kv", "w_up", "w_down", "w_ple_gate", "w_ple_proj")
REPLICATED = ("w_spatial", "b_spatial", "ln_kv", "g_k", "ln_mix_b", "g_q", "ln_mlp", "ln_ple")
SHARDED_VECTORS = ("ln_mix_a", "g_v_a")
SMALL_ROWS = 18


def kernel(x, p, ln_mix_a, w_in_a, g_v_a, w_spatial, b_spatial, w_out_a, ln_kv, w_kv, g_k, ln_mix_b, w_q, g_q, w_out_b, ln_mlp, w_up, w_down, ln_ple, w_ple_gate, w_ple_proj, loss_target, m_ln_mix_a, m_w_in_a, m_g_v_a, m_w_spatial, m_b_spatial, m_w_out_a, m_ln_kv, m_w_kv, m_g_k, m_ln_mix_b, m_w_q, m_g_q, m_w_out_b, m_ln_mlp, m_w_up, m_w_down, m_ln_ple, m_w_ple_gate, m_w_ple_proj, v_ln_mix_a, v_w_in_a, v_g_v_a, v_w_spatial, v_b_spatial, v_w_out_a, v_ln_kv, v_w_kv, v_g_k, v_ln_mix_b, v_w_q, v_g_q, v_w_out_b, v_ln_mlp, v_w_up, v_w_down, v_ln_ple, v_w_ple_gate, v_w_ple_proj):
    given = dict(locals())
    weights = {n: given[n] for n in WEIGHTS}
    shard = 2 * lax.axis_index("x") + lax.axis_index("y")
    core = lax.axis_index("c")
    shard_1 = shard.astype(jnp.int32).reshape(1)
    core_1 = core.astype(jnp.int32).reshape(1)
    place = jnp.stack([shard, core]).astype(jnp.int32)

    leaves = []
    for name, layers, cols in MATRICES:
        w3 = weights[name] if layers else weights[name][None]
        for layer in range(max(layers, 1)):
            leaves.append((name, layer, cols, cast_into_slot(w3, layer, shard_1, name=f"cast_{name}_{layer}")))
    first = [lf for lf in leaves if lf[0] in FIRST_LAYER and lf[1] == 0]
    second = [lf for lf in leaves if not (lf[0] in FIRST_LAYER and lf[1] == 0)]
    got_a, vec_a = gather_shards([lf[3] for lf in first], [ln_mix_a, g_v_a], name="gather_layer0")
    send_a, recv_a, flying, token = gather_start([lf[3] for lf in second], got_a[0], name="gather_layer1_start")

    def assemble(leaf_list, arrays):
        full = {}
        for (name, layer, cols, _), arr in zip(leaf_list, arrays):
            if not cols:
                arr = arr.reshape(N_SHARDS * arr.shape[1], arr.shape[2])
            full.setdefault(name, {})[layer] = arr
        return full

    full_a = assemble(first, got_a)
    w = {name: ((full_a[name][0],) if layers else full_a[name][0]) for name, layers, _ in MATRICES if name in full_a}
    w["ln_mix_a"] = vec_a[0].reshape(1, D_MODEL) + token[0, 0]
    w["g_v_a"] = vec_a[1].reshape(1, D_MODEL)
    for name in REPLICATED:
        w[name] = weights[name]

    class Late:
        def after_first_layer(self, x_done):
            self.passed = gather_pass_on(flying, send_a, recv_a, x_done, name="gather_layer1_pass_on")

        def second_layer_weights(self, k_done):
            bufs, send_b, recv_b, _ = self.passed
            got_b = gather_finish(bufs, send_b, recv_b, k_done, [lf[3].shape for lf in second],
                                  name="gather_layer1_finish")
            full_b = assemble(second, got_b)
            out = {}
            for name, layers, _ in MATRICES:
                if name in full_b:
                    both = {**full_a.get(name, {}), **full_b[name]}
                    out[name] = tuple(both[l] for l in sorted(both))
            return out

    t = x.shape[1]
    loss_blk, dx, g = local_step(x[0], p.reshape(2, t, PLE_DIM), loss_target[0], w, Late())
    loss = lax.psum(loss_blk[0, 0], ("x", "y", "c"))

    big = []
    for name, layers, cols in MATRICES:
        for layer in range(max(layers, 1)):
            arr = g[name][layer] if layers == 2 else g[name]
            rows = arr.shape[-2] if cols else arr.shape[0] // N_SHARDS
            big.append(arr.reshape(N_SHARDS, 2, rows // 2, arr.shape[-1]))
    theirs = pair_exchange(big, name="grad_pair_exchange")
    wire = [add_to_wire(a, b, core_1, name=f"grad_pair_sum_{i}") for i, (a, b) in enumerate(zip(big, theirs))]
    landed = chip_exchange(wire, name="grad_chip_exchange")
    bufs, i = [], 0
    for name, layers, _ in MATRICES:
        buf = None
        for layer in range(max(layers, 1)):
            buf = sum_chips(wire[i], landed[i], place, buf, layer, max(layers, 1), name=f"grad_chip_sum_{i}")
            i += 1
        bufs.append(buf)
    shared = pair_share(bufs, name="grad_pair_share")
    grads = {name: shared[o].reshape(weights[name].shape) for o, (name, _, _) in enumerate(MATRICES)}

    small = REPLICATED + SHARDED_VECTORS
    flat = jnp.concatenate([g[n].reshape(-1) for n in small])
    room = 8 * SMALL_ROWS * D_MODEL
    flat = jnp.concatenate([flat, jnp.zeros((room - flat.shape[0],), F32)])
    reduced = all_reduce_small(flat.reshape(8, SMALL_ROWS, D_MODEL), name="grad_small_all_reduce").reshape(-1)
    at = 0
    for n in small:
        size = g[n].size
        piece = reduced[at:at + size]
        at += size
        if n in SHARDED_VECTORS:
            per = D_MODEL // N_SHARDS
            grads[n] = lax.dynamic_slice(piece, (shard * per,), (per,)).reshape(weights[n].shape)
        else:
            grads[n] = piece.reshape(weights[n].shape)

    delta, new_m, new_v = {}, {}, {}
    for n in WEIGHTS:
        wn, gn, mn, vn = weights[n], grads[n], given["m_" + n], given["v_" + n]
        if wn.ndim == 1:
            wn, gn, mn, vn = (a.reshape(1, -1) for a in (wn, gn, mn, vn))
        outs = adamw(wn, gn, mn, vn, name=f"adamw_{n}")
        delta[n], new_m[n], new_v[n] = (o.reshape(weights[n].shape) for o in outs)
    return (loss, dx.reshape(x.shape), *[grads[n] for n in WEIGHTS], *[delta[n] for n in WEIGHTS],
            *[new_m[n] for n in WEIGHTS], *[new_v[n] for n in WEIGHTS])
```

```python
import jax
import jax.numpy as jnp
from jax import lax
from jax.experimental import pallas as pl
from jax.experimental.pallas import tpu as pltpu

F32 = jnp.float32
BF16 = jnp.bfloat16

D_MODEL = 1024
D_FF = 4096
PLE_DIM = 256
N_GROUPS = 8
CHUNK = 128
HEAD_DIM = 64
LANES = 128
ATT_BLOCK = 256
EPS = 1e-6
N_SHARDS = 4
VMEM_LIMIT = 56 * 1024 * 1024

ADAM_LR = 0.001
ADAM_B1 = 0.9
ADAM_B2 = 0.999
ADAM_EPS = 1e-08
ADAM_WD = 0.01
ADAM_STEP = 10

MESH = pl.DeviceIdType.MESH


def _pcall(body, *, name, out_shape, grid=None, in_specs=None, out_specs=None, scratch_shapes=(),
           semantics=None, aliases=None, side_effects=False, num_prefetch=0):
    params = dict(vmem_limit_bytes=VMEM_LIMIT)
    if semantics is not None:
        params["dimension_semantics"] = semantics
    if side_effects:
        params["has_side_effects"] = True
    kwargs = {}
    if aliases:
        kwargs["input_output_aliases"] = aliases
    if num_prefetch:
        spec = pltpu.PrefetchScalarGridSpec(num_scalar_prefetch=num_prefetch, grid=grid, in_specs=in_specs,
                                            out_specs=out_specs, scratch_shapes=list(scratch_shapes))
        return pl.pallas_call(body, name=name, out_shape=out_shape, grid_spec=spec,
                              compiler_params=pltpu.CompilerParams(**params), **kwargs)
    if grid is not None:
        kwargs["grid"] = grid
    if in_specs is not None:
        kwargs["in_specs"] = in_specs
    if out_specs is not None:
        kwargs["out_specs"] = out_specs
    if aliases:
        kwargs["input_output_aliases"] = aliases
    return pl.pallas_call(body, name=name, out_shape=out_shape, scratch_shapes=list(scratch_shapes),
                          compiler_params=pltpu.CompilerParams(**params), **kwargs)


def _sds(shape, dtype):
    return jax.ShapeDtypeStruct(shape, dtype)


_GELU_C = 0.7978845608028654
_GELU_A = 0.044715


def _gelu(x):
    inner = _GELU_C * (x + _GELU_A * (x * x * x))
    return 0.5 * x * (1.0 + jnp.tanh(inner))


def _gelu_grad(x):
    x2 = x * x
    t = jnp.tanh(_GELU_C * (x + _GELU_A * (x2 * x)))
    return 0.5 * (1.0 + t) + 0.5 * x * (1.0 - t * t) * (_GELU_C * (1.0 + 3.0 * _GELU_A * x2))


def _sigmoid(x):
    return 1.0 / (1.0 + jnp.exp(-x))


def _log_sigmoid(z):
    return jnp.minimum(z, 0.0) - jnp.log(1.0 + jnp.exp(-jnp.abs(z)))


def _split_bf16(a):
    hi = a.astype(BF16)
    lo = (a - hi.astype(F32)).astype(BF16)
    return hi, lo


def _dot(a, b):
    return jnp.dot(a, b, preferred_element_type=F32)


def _dot_nt(a, b):
    return lax.dot_general(a, b, (((1,), (1,)), ((), ())), preferred_element_type=F32)


def _dot_tn(a, b):
    return lax.dot_general(a, b, (((0,), (0,)), ((), ())), preferred_element_type=F32)


def _head_rstd(x):
    lane = lax.broadcasted_iota(jnp.int32, x.shape, 1)
    low = lane < HEAD_DIM
    sq = x * x
    s_lo = jnp.sum(jnp.where(low, sq, 0.0), axis=-1, keepdims=True)
    s_hi = jnp.sum(jnp.where(low, 0.0, sq), axis=-1, keepdims=True)
    ms = jnp.where(low, s_lo, s_hi) * (1.0 / HEAD_DIM)
    return lax.rsqrt(ms + EPS)


def _head_mean(x):
    lane = lax.broadcasted_iota(jnp.int32, x.shape, 1)
    low = lane < HEAD_DIM
    s_lo = jnp.sum(jnp.where(low, x, 0.0), axis=-1, keepdims=True)
    s_hi = jnp.sum(jnp.where(low, 0.0, x), axis=-1, keepdims=True)
    return jnp.where(low, s_lo, s_hi) * (1.0 / HEAD_DIM)


def _full(shape):
    zeros = (0,) * len(shape)
    return pl.BlockSpec(shape, lambda i: zeros)


def norm_matmul(x, g, w, *, name, epilogue="none", tm=512):
    t, d = x.shape
    sharded = w.ndim == 3
    per = w.shape[2] if sharded else w.shape[1]
    n = N_SHARDS * per if sharded else per
    tm = min(tm, t)

    def body(x_ref, g_ref, w_ref, h_ref, r_ref, *outs):
        xv = x_ref[...]
        r = lax.rsqrt(jnp.mean(xv * xv, axis=-1, keepdims=True) + EPS)
        h = ((xv * r) * g_ref[...]).astype(BF16)
        h_ref[...] = h
        r_ref[...] = r
        for s in range(N_SHARDS if sharded else 1):
            cols = slice(s * per, (s + 1) * per)
            y = _dot(h, w_ref[s] if sharded else w_ref[...])
            if epilogue == "none":
                outs[0][:, cols] = y
            else:
                a = jnp.maximum(y, 0.0)
                outs[0][:, cols] = a.astype(BF16)
                outs[1][:, cols] = (a * a).astype(BF16)

    row = lambda i: (i, 0)
    out_shape = [_sds((t, d), BF16), _sds((t, 1), F32)]
    out_specs = [pl.BlockSpec((tm, d), row), pl.BlockSpec((tm, 1), row)]
    if epilogue == "none":
        out_shape.append(_sds((t, n), F32))
        out_specs.append(pl.BlockSpec((tm, n), row))
    else:
        out_shape += [_sds((t, n), BF16), _sds((t, n), BF16)]
        out_specs += [pl.BlockSpec((tm, n), row)] * 2
    return _pcall(
        body, name=name, out_shape=out_shape, grid=(t // tm,),
        in_specs=[pl.BlockSpec((tm, d), row), _full((1, d)), _full(w.shape)],
        out_specs=out_specs, semantics=("parallel",))(x, g, w)


def matmul_residual(a, w, res, *, name, tm=512):
    t, k = a.shape
    n = w.shape[1]
    tm = min(tm, t)

    def body(a_ref, w_ref, res_ref, o_ref):
        o_ref[...] = res_ref[...] + _dot(a_ref[...], w_ref[...])

    row = lambda i: (i, 0)
    return _pcall(
        body, name=name, out_shape=_sds((t, n), F32), grid=(t // tm,),
        in_specs=[pl.BlockSpec((tm, k), row), _full(w.shape), pl.BlockSpec((tm, n), row)],
        out_specs=pl.BlockSpec((tm, n), row), semantics=("parallel",))(a, w, res)


def ple_forward(x, g, w_gate, p, w_proj, *, name, tm=256):
    t, d = x.shape
    tm = min(tm, t)

    def body(x_ref, g_ref, wg_ref, p_ref, wp_ref, h_ref, r_ref, gate_ref, pp_ref, o_ref):
        xv = x_ref[...]
        r = lax.rsqrt(jnp.mean(xv * xv, axis=-1, keepdims=True) + EPS)
        h = ((xv * r) * g_ref[...]).astype(BF16)
        h_ref[...] = h
        r_ref[...] = r
        gate = _sigmoid(_dot(h, wg_ref[...]))
        gate_ref[...] = gate
        pb = p_ref[...].astype(BF16)
        per = d // N_SHARDS
        for s in range(N_SHARDS):
            cols = slice(s * per, (s + 1) * per)
            pp = _dot(pb, wp_ref[s])
            pp_ref[:, cols] = pp.astype(BF16)
            o_ref[:, cols] = xv[:, cols] + pp * gate[:, cols]

    row = lambda i: (i, 0)
    fixed = lambda i: (0, 0)
    return _pcall(
        body, name=name,
        out_shape=[_sds((t, d), BF16), _sds((t, 1), F32), _sds((t, d), F32), _sds((t, d), BF16), _sds((t, d), F32)],
        grid=(t // tm,),
        in_specs=[pl.BlockSpec((tm, d), row), pl.BlockSpec((1, d), fixed), pl.BlockSpec((d, d), fixed),
                  pl.BlockSpec((tm, PLE_DIM), row),
                  pl.BlockSpec((N_SHARDS, PLE_DIM, d // N_SHARDS), lambda i: (0, 0, 0))],
        out_specs=[pl.BlockSpec((tm, d), row), pl.BlockSpec((tm, 1), row), pl.BlockSpec((tm, d), row),
                   pl.BlockSpec((tm, d), row), pl.BlockSpec((tm, d), row)],
        semantics=("parallel",))(x, g, w_gate, p, w_proj)


def _tril_mask():
    r = lax.broadcasted_iota(jnp.int32, (CHUNK, CHUNK), 0)
    c = lax.broadcasted_iota(jnp.int32, (CHUNK, CHUNK), 1)
    return c <= r


def _sgu_common(pre_ref, gv_ref, ws_ref):
    pre = pre_ref[...]
    pre_u, pre_v = pre[:, :D_MODEL], pre[:, D_MODEL:]
    u = _gelu(pre_u)
    v = _gelu(pre_v)
    r = lax.rsqrt(jnp.mean(v * v, axis=-1, keepdims=True) + EPS)
    vhat = v * r
    vn = (vhat * gv_ref[...]).astype(BF16)
    tril = _tril_mask()
    wm = [jnp.where(tril, ws_ref[g], 0.0).astype(BF16) for g in range(N_GROUPS)]
    return pre_u, pre_v, u, r, vhat, vn, wm, tril


def sgu_forward(pre, g_v, w_s, b_full, *, name):
    t = pre.shape[0]

    def body(pre_ref, gv_ref, ws_ref, b_ref, y_ref):
        _, _, u, _, _, vn, wm, _ = _sgu_common(pre_ref, gv_ref, ws_ref)
        for g in range(N_GROUPS):
            cols = slice(g * LANES, (g + 1) * LANES)
            mix = _dot(wm[g], vn[:, cols]) + b_ref[:, cols]
            y_ref[:, cols] = (u[:, cols] * mix).astype(BF16)

    return _pcall(
        body, name=name, out_shape=_sds((t, D_MODEL), BF16), grid=(t // CHUNK,),
        in_specs=[pl.BlockSpec((CHUNK, 2 * D_MODEL), lambda i: (i, 0)), pl.BlockSpec((1, D_MODEL), lambda i: (0, 0)),
                  pl.BlockSpec((N_GROUPS, CHUNK, CHUNK), lambda i: (0, 0, 0)),
                  pl.BlockSpec((CHUNK, D_MODEL), lambda i: (0, 0))],
        out_specs=pl.BlockSpec((CHUNK, D_MODEL), lambda i: (i, 0)),
        semantics=("parallel",))(pre, g_v, w_s, b_full)


def head_norm(pre, g128, *, name, col_block=0, scale=1.0, passthrough=False, tm=512):
    t = pre.shape[0]
    tm = min(tm, t)

    def body(*refs):
        if passthrough:
            x_ref, v_ref, g_ref, o_ref, vo_ref = refs
            vo_ref[...] = v_ref[...].astype(BF16)
        else:
            x_ref, g_ref, o_ref = refs
        g = g_ref[...] * scale
        for b in range(D_MODEL // LANES):
            cols = slice(b * LANES, (b + 1) * LANES)
            xv = x_ref[:, cols]
            o_ref[:, cols] = ((xv * _head_rstd(xv)) * g).astype(BF16)

    x_spec = pl.BlockSpec((tm, D_MODEL), lambda i: (i, col_block))
    g_spec = pl.BlockSpec((1, LANES), lambda i: (0, 0))
    o_spec = pl.BlockSpec((tm, D_MODEL), lambda i: (i, 0))
    if passthrough:
        return _pcall(body, name=name, out_shape=[_sds((t, D_MODEL), BF16)] * 2, grid=(t // tm,),
                      in_specs=[x_spec, pl.BlockSpec((tm, D_MODEL), lambda i: (i, 1)), g_spec],
                      out_specs=[o_spec, o_spec], semantics=("parallel",))(pre, pre, g128)
    return _pcall(body, name=name, out_shape=_sds((t, D_MODEL), BF16), grid=(t // tm,),
                  in_specs=[x_spec, g_spec], out_specs=o_spec, semantics=("parallel",))(pre, g128)


def _suffix_matrix(n):
    r = lax.broadcasted_iota(jnp.int32, (n, n), 0)
    c = lax.broadcasted_iota(jnp.int32, (n, n), 1)
    return jnp.where(r > c, 1.0, 0.0).astype(BF16)


def _prefix_matrix(n):
    r = lax.broadcasted_iota(jnp.int32, (n, n), 0)
    c = lax.broadcasted_iota(jnp.int32, (n, n), 1)
    return jnp.where(r < c, 1.0, 0.0).astype(BF16)


def _exact_cumsum(a, tri):
    rows = a.shape[0]
    hi, lo = _split_bf16(a)
    both = _dot(jnp.concatenate([hi, lo], axis=0), tri)
    return both[:rows] + both[rows:]


def _stacked_causal(n):
    r = lax.broadcasted_iota(jnp.int32, (2 * n, n), 0)
    c = lax.broadcasted_iota(jnp.int32, (2 * n, n), 1)
    return c < jnp.where(r >= n, r - n, r)


def _stack_heads(a, low):
    zero = jnp.zeros_like(a)
    return jnp.concatenate([jnp.where(low, a, zero), jnp.where(low, zero, a)], axis=0)


def stick_breaking_forward(q, k, v, *, name):
    t = q.shape[0]
    blk = min(ATT_BLOCK, t)
    nq = t // blk

    def body(q_ref, k_ref, v_ref, o_ref):
        i = pl.program_id(1)
        low = lax.broadcasted_iota(jnp.int32, (blk, LANES), 1) < HEAD_DIM
        tri = _suffix_matrix(blk)
        causal = _stacked_causal(blk)
        qs = _stack_heads(q_ref[...], low)

        def block(j, carry, acc, masked):
            rows = pl.ds(pl.multiple_of(j * blk, blk), blk)
            z = _dot_nt(qs, k_ref[rows, :])
            ls = _log_sigmoid(z)
            lg = ls - z
            if masked:
                lg = jnp.where(causal, lg, 0.0)
            s = ls + _exact_cumsum(lg, tri) + carry
            a = jnp.exp(s)
            if masked:
                a = jnp.where(causal, a, 0.0)
            acc = acc + _dot(a.astype(BF16), v_ref[rows, :])
            return carry + jnp.sum(lg, axis=-1, keepdims=True), acc

        carry, acc = block(i, jnp.zeros((2 * blk, 1), F32), jnp.zeros((2 * blk, LANES), F32), True)
        _, acc = lax.fori_loop(0, i, lambda n, st: block(i - 1 - n, st[0], st[1], False), (carry, acc))
        o_ref[...] = jnp.where(low, acc[:blk], acc[blk:]).astype(BF16)

    return _pcall(
        body, name=name, out_shape=_sds((t, D_MODEL), BF16), grid=(D_MODEL // LANES, nq),
        in_specs=[pl.BlockSpec((blk, LANES), lambda p, i: (i, p)), pl.BlockSpec((t, LANES), lambda p, i: (0, p)),
                  pl.BlockSpec((t, LANES), lambda p, i: (0, p))],
        out_specs=pl.BlockSpec((blk, LANES), lambda p, i: (i, p)),
        semantics=("parallel", "arbitrary"))(q, k, v)


def loss_forward(x, target, *, name, tm=512):
    t, d = x.shape
    tm = min(tm, t)

    def body(x_ref, t_ref, l_ref, dx_ref):
        @pl.when(pl.program_id(0) == 0)
        def _():
            l_ref[...] = jnp.zeros_like(l_ref)

        diff = x_ref[...] - t_ref[...]
        dx_ref[...] = diff * (1.0 / d)
        l_ref[...] += 0.5 * jnp.sum(jnp.mean(diff * diff, axis=-1, keepdims=True))

    return _pcall(
        body, name=name, out_shape=[_sds((8, LANES), F32), _sds((t, d), F32)], grid=(t // tm,),
        in_specs=[pl.BlockSpec((tm, d), lambda i: (i, 0))] * 2,
        out_specs=[pl.BlockSpec((8, LANES), lambda i: (0, 0)), pl.BlockSpec((tm, d), lambda i: (i, 0))],
        semantics=("arbitrary",))(x, target)


def matmul_nt(dy, w, *, name, mul=None, out_dtype=F32, tm=512):
    t, n = dy.shape
    k = w.shape[0]
    tm = min(tm, t)

    def body(*refs):
        if mul is None:
            dy_ref, w_ref, o_ref = refs
        else:
            dy_ref, w_ref, m_ref, o_ref = refs
        y = _dot_nt(dy_ref[...].astype(BF16), w_ref[...])
        if mul is not None:
            y = y * (2.0 * m_ref[...].astype(F32))
        o_ref[...] = y.astype(out_dtype)

    row = lambda i: (i, 0)
    in_specs = [pl.BlockSpec((tm, n), row), _full(w.shape)]
    args = [dy, w]
    if mul is not None:
        in_specs.append(pl.BlockSpec((tm, k), row))
        args.append(mul)
    return _pcall(body, name=name, out_shape=_sds((t, k), out_dtype), grid=(t // tm,), in_specs=in_specs,
                  out_specs=pl.BlockSpec((tm, k), row), semantics=("parallel",))(*args)


def matmul_tn(a, dy, *, name, col_shards, tk=512):
    t, k = a.shape
    n = dy.shape[1]
    if col_shards:
        tn = n // N_SHARDS

        def body(a_ref, dy_ref, o_ref):
            o_ref[...] = _dot_tn(a_ref[...].astype(BF16), dy_ref[...].astype(BF16))

        return _pcall(body, name=name, out_shape=_sds((N_SHARDS, k, tn), F32), grid=(N_SHARDS,),
                      in_specs=[_full((t, k)), pl.BlockSpec((t, tn), lambda j: (0, j))],
                      out_specs=pl.BlockSpec((None, k, tn), lambda j: (j, 0, 0)), semantics=("parallel",))(a, dy)

    tk = min(tk, k)

    def body(a_ref, dy_ref, o_ref, dy_bf):
        @pl.when(pl.program_id(0) == 0)
        def _():
            dy_bf[...] = dy_ref[...].astype(BF16)

        o_ref[...] = _dot_tn(a_ref[...].astype(BF16), dy_bf[...])

    return _pcall(body, name=name, out_shape=_sds((k, n), F32), grid=(k // tk,),
                  in_specs=[pl.BlockSpec((t, tk), lambda i: (0, i)), _full((t, n))],
                  out_specs=pl.BlockSpec((tk, n), lambda i: (i, 0)),
                  scratch_shapes=[pltpu.VMEM((t, n), BF16)], semantics=("arbitrary",))(a, dy)


def norm_backward(dpre, w, x, g, rstd, dx_out, *, name, tm=512):
    t, d = x.shape
    n = dpre.shape[1]
    tm = min(tm, t)
    if w.ndim == 3:
        w_spec = pl.BlockSpec(w.shape, lambda i: (0, 0, 0))
    else:
        w_spec = pl.BlockSpec(w.shape, lambda i: (0, 0))

    def body(dp_ref, w_ref, x_ref, g_ref, r_ref, dxo_ref, dx_ref, dg_ref):
        @pl.when(pl.program_id(0) == 0)
        def _():
            dg_ref[...] = jnp.zeros_like(dg_ref)

        if w.ndim == 3:
            per = n // N_SHARDS
            dh = _dot_nt(dp_ref[:, 0:per], w_ref[0])
            for s in range(1, N_SHARDS):
                dh = dh + _dot_nt(dp_ref[:, s * per:(s + 1) * per], w_ref[s])
        else:
            dh = _dot_nt(dp_ref[...], w_ref[...])
        r = r_ref[...]
        xn = x_ref[...] * r
        dg_ref[...] += jnp.sum(dh * xn, axis=0, keepdims=True)
        dxn = dh * g_ref[...]
        dx = r * (dxn - xn * jnp.mean(dxn * xn, axis=-1, keepdims=True))
        dx_ref[...] = dxo_ref[...] + dx

    row = lambda i: (i, 0)
    fixed = lambda i: (0, 0)
    return _pcall(
        body, name=name, out_shape=[_sds((t, d), F32), _sds((1, d), F32)], grid=(t // tm,),
        in_specs=[pl.BlockSpec((tm, n), row), w_spec, pl.BlockSpec((tm, d), row),
                  pl.BlockSpec((1, d), fixed), pl.BlockSpec((tm, 1), row), pl.BlockSpec((tm, d), row)],
        out_specs=[pl.BlockSpec((tm, d), row), pl.BlockSpec((1, d), fixed)],
        semantics=("arbitrary",))(dpre, w, x, g, rstd, dx_out)


def ple_backward(dx, gate, pp, *, name, tm=512):
    t, d = dx.shape
    tm = min(tm, t)

    def body(dx_ref, gate_ref, pp_ref, dg_ref, dp_ref):
        dxv = dx_ref[...]
        gate = gate_ref[...]
        dg_ref[...] = (dxv * pp_ref[...].astype(F32) * (gate * (1.0 - gate))).astype(BF16)
        dp_ref[...] = (dxv * gate).astype(BF16)

    spec = pl.BlockSpec((tm, d), lambda i: (i, 0))
    return _pcall(body, name=name, out_shape=[_sds((t, d), BF16)] * 2, grid=(t // tm,), in_specs=[spec] * 3,
                  out_specs=[spec] * 2, semantics=("parallel",))(dx, gate, pp)


def sgu_backward(dy, pre, g_v, w_s, b_full, *, name):
    t = pre.shape[0]
    n_chunks = t // CHUNK

    def body(dy_ref, pre_ref, gv_ref, ws_ref, b_ref, dpre_ref, dws_ref, db_ref, dgv_ref, dvn_s, dbf_s):
        step = pl.program_id(0)

        @pl.when(step == 0)
        def _():
            dws_ref[...] = jnp.zeros_like(dws_ref)
            dgv_ref[...] = jnp.zeros_like(dgv_ref)
            dbf_s[...] = jnp.zeros_like(dbf_s)

        pre_u, pre_v, u, r, vhat, vn, wm, tril = _sgu_common(pre_ref, gv_ref, ws_ref)
        dyv = dy_ref[...]
        for g in range(N_GROUPS):
            cols = slice(g * LANES, (g + 1) * LANES)
            mix = _dot(wm[g], vn[:, cols]) + b_ref[:, cols]
            dmix = dyv[:, cols] * u[:, cols]
            dmix_b = dmix.astype(BF16)
            du = dyv[:, cols] * mix
            dpre_ref[:, cols] = (du * _gelu_grad(pre_u[:, cols])).astype(BF16)
            dws_ref[g] += jnp.where(tril, _dot_nt(dmix_b, vn[:, cols]), 0.0)
            dbf_s[:, cols] += dmix
            dvn_s[:, cols] = _dot_tn(wm[g], dmix_b)
        dvn = dvn_s[...]
        dgv_ref[...] += jnp.sum(dvn * vhat, axis=0, keepdims=True)
        dxn = dvn * gv_ref[...]
        dv = r * (dxn - vhat * jnp.mean(dxn * vhat, axis=-1, keepdims=True))
        dpre_ref[:, D_MODEL:] = (dv * _gelu_grad(pre_v)).astype(BF16)

        @pl.when(step == n_chunks - 1)
        def _():
            lane = lax.broadcasted_iota(jnp.int32, (CHUNK, LANES), 1)
            acc = jnp.zeros((CHUNK, LANES), F32)
            for g in range(N_GROUPS):
                s = jnp.sum(dbf_s[:, g * LANES:(g + 1) * LANES], axis=-1, keepdims=True)
                acc = jnp.where(lane == g, s, acc)
            db_ref[...] = acc

    fixed2 = lambda i: (0, 0)
    return _pcall(
        body, name=name,
        out_shape=[_sds((t, 2 * D_MODEL), BF16), _sds((N_GROUPS, CHUNK, CHUNK), F32), _sds((CHUNK, LANES), F32),
                   _sds((1, D_MODEL), F32)],
        grid=(n_chunks,),
        in_specs=[pl.BlockSpec((CHUNK, D_MODEL), lambda i: (i, 0)), pl.BlockSpec((CHUNK, 2 * D_MODEL), lambda i: (i, 0)),
                  pl.BlockSpec((1, D_MODEL), fixed2), pl.BlockSpec((N_GROUPS, CHUNK, CHUNK), lambda i: (0, 0, 0)),
                  pl.BlockSpec((CHUNK, D_MODEL), fixed2)],
        out_specs=[pl.BlockSpec((CHUNK, 2 * D_MODEL), lambda i: (i, 0)),
                   pl.BlockSpec((N_GROUPS, CHUNK, CHUNK), lambda i: (0, 0, 0)), pl.BlockSpec((CHUNK, LANES), fixed2),
                   pl.BlockSpec((1, D_MODEL), fixed2)],
        scratch_shapes=[pltpu.VMEM((CHUNK, D_MODEL), F32), pltpu.VMEM((CHUNK, D_MODEL), F32)],
        semantics=("arbitrary",))(dy, pre, g_v, w_s, b_full)


def head_norm_backward(dy, pre, g128, *, name, col_block=0, scale=1.0, passthrough=None, tm=512):
    t = dy.shape[0]
    tm = min(tm, t)
    width = 2 * D_MODEL if passthrough is not None else D_MODEL

    def body(*refs):
        if passthrough is not None:
            dy_ref, x_ref, g_ref, dv_ref, o_ref, dg_ref = refs
            o_ref[:, D_MODEL:] = dv_ref[...].astype(BF16)
        else:
            dy_ref, x_ref, g_ref, o_ref, dg_ref = refs

        @pl.when(pl.program_id(0) == 0)
        def _():
            dg_ref[...] = jnp.zeros_like(dg_ref)

        g = g_ref[...]
        dg = jnp.zeros((1, LANES), F32)
        for b in range(D_MODEL // LANES):
            cols = slice(b * LANES, (b + 1) * LANES)
            xv = x_ref[:, cols]
            r = _head_rstd(xv)
            xn = xv * r
            dyv = dy_ref[:, cols] * scale
            dg = dg + jnp.sum(dyv * xn, axis=0, keepdims=True)
            dxn = dyv * g
            o_ref[:, cols] = (r * (dxn - xn * _head_mean(dxn * xn))).astype(BF16)
        dg_ref[...] += dg

    row = lambda i: (i, 0)
    in_specs = [pl.BlockSpec((tm, D_MODEL), row), pl.BlockSpec((tm, D_MODEL), lambda i: (i, col_block)),
                pl.BlockSpec((1, LANES), lambda i: (0, 0))]
    args = [dy, pre, g128]
    if passthrough is not None:
        in_specs.append(pl.BlockSpec((tm, D_MODEL), row))
        args.append(passthrough)
    return _pcall(body, name=name, out_shape=[_sds((t, width), BF16), _sds((1, LANES), F32)], grid=(t // tm,),
                  in_specs=in_specs,
                  out_specs=[pl.BlockSpec((tm, width), row), pl.BlockSpec((1, LANES), lambda i: (0, 0))],
                  semantics=("arbitrary",))(*args)


def stick_breaking_backward(q, k, v, do, *, name):
    t = q.shape[0]
    blk = min(ATT_BLOCK, t)
    nq = t // blk

    def body(q_ref, k_ref, v_ref, do_ref, dq_ref, dk_ref, dv_ref, s_buf, sg_buf):
        i = pl.program_id(1)

        @pl.when(i == 0)
        def _():
            dk_ref[...] = jnp.zeros_like(dk_ref)
            dv_ref[...] = jnp.zeros_like(dv_ref)

        low = lax.broadcasted_iota(jnp.int32, (blk, LANES), 1) < HEAD_DIM
        suffix = _suffix_matrix(blk)
        prefix = _prefix_matrix(blk)
        causal = _stacked_causal(blk)
        qs = _stack_heads(q_ref[...], low)
        dos = _stack_heads(do_ref[...], low)

        def log_weights(j, carry, masked):
            rows = pl.ds(pl.multiple_of(j * blk, blk), blk)
            z = _dot_nt(qs, k_ref[rows, :])
            ls = _log_sigmoid(z)
            lg = ls - z
            if masked:
                lg = jnp.where(causal, lg, 0.0)
            s_buf[j] = ls + _exact_cumsum(lg, suffix) + carry
            sg_buf[j] = jnp.exp(ls)
            return carry + jnp.sum(lg, axis=-1, keepdims=True)

        carry = log_weights(i, jnp.zeros((2 * blk, 1), F32), True)
        lax.fori_loop(0, i, lambda n, c: log_weights(i - 1 - n, c, False), carry)

        def grads(j, pcarry, dq_acc, masked):
            rows = pl.ds(pl.multiple_of(j * blk, blk), blk)
            a = jnp.exp(s_buf[j])
            if masked:
                a = jnp.where(causal, a, 0.0)
            sg = sg_buf[j]
            ds = _dot_nt(dos, v_ref[rows, :]) * a
            before = _exact_cumsum(ds, prefix) + pcarry
            if masked:
                before = jnp.where(causal, before, 0.0)
            dz = (ds - sg * (ds + before)).astype(BF16)
            dq_acc = dq_acc + _dot(dz, k_ref[rows, :])
            dk_ref[rows, :] += _dot_tn(dz, qs)
            dv_ref[rows, :] += _dot_tn(a.astype(BF16), dos)
            return pcarry + jnp.sum(ds, axis=-1, keepdims=True), dq_acc

        state = lax.fori_loop(0, i, lambda j, st: grads(j, st[0], st[1], False),
                              (jnp.zeros((2 * blk, 1), F32), jnp.zeros((2 * blk, LANES), F32)))
        _, dq_acc = grads(i, state[0], state[1], True)
        dq_ref[...] = jnp.where(low, dq_acc[:blk], dq_acc[blk:])

    full = pl.BlockSpec((t, LANES), lambda p, i: (0, p))
    qblk = pl.BlockSpec((blk, LANES), lambda p, i: (i, p))
    return _pcall(
        body, name=name, out_shape=[_sds((t, D_MODEL), F32)] * 3, grid=(D_MODEL // LANES, nq),
        in_specs=[qblk, full, full, qblk], out_specs=[qblk, full, full],
        scratch_shapes=[pltpu.VMEM((nq, 2 * blk, blk), F32), pltpu.VMEM((nq, 2 * blk, blk), F32)],
        semantics=("parallel", "arbitrary"))(q, k, v, do)


def _mlp_forward(x, g, w_up, w_down, tag):
    h, r, a, a2 = norm_matmul(x, g, w_up, name=f"mlp_up_{tag}", epilogue="relu2")
    return matmul_residual(a2, w_down, x, name=f"mlp_down_{tag}"), (x, h, r, a, a2)


def _mlp_backward(dx, saved, g, w_up, w_down, tag):
    x, h, r, a, a2 = saved
    d_w_down = matmul_tn(a2, dx, name=f"d_w_down_{tag}", col_shards=False)
    dpre = matmul_nt(dx, w_down, name=f"d_mlp_act_{tag}", mul=a, out_dtype=BF16)
    d_w_up = matmul_tn(h, dpre, name=f"d_w_up_{tag}", col_shards=True)
    dx, d_g = norm_backward(dpre, w_up, x, g, r, dx, name=f"d_mlp_norm_{tag}")
    return dx, d_w_up, d_w_down, d_g


def _ple_backward(dx, saved, p, g, w_gate, tag):
    x, h, r, gate, pp = saved
    dgate, dproj = ple_backward(dx, gate, pp, name=f"d_ple_{tag}")
    d_w_proj = matmul_tn(p, dproj, name=f"d_w_ple_proj_{tag}", col_shards=True)
    d_w_gate = matmul_tn(h, dgate, name=f"d_w_ple_gate_{tag}", col_shards=False)
    dx, d_g = norm_backward(dgate, w_gate, x, g, r, dx, name=f"d_ple_norm_{tag}")
    return dx, d_w_gate, d_w_proj, d_g


def local_step(x, p, target, w, late=None):
    row = lambda v: v.reshape(1, -1)
    g128 = lambda v: jnp.tile(v.reshape(1, HEAD_DIM), (1, 2))
    scale = HEAD_DIM ** -0.5
    b_full = jnp.repeat(jnp.transpose(w["b_spatial"][0]), LANES, axis=1)
    w_s = w["w_spatial"][0]

    x0 = x
    h_a, r_a, pre_a = norm_matmul(x0, row(w["ln_mix_a"][0]), w["w_in_a"][0], name="sgu_in")
    y_a = sgu_forward(pre_a, row(w["g_v_a"][0]), w_s, b_full, name="sgu_mix")
    x1 = matmul_residual(y_a, w["w_out_a"][0], x0, name="sgu_out")
    x2, mlp0 = _mlp_forward(x1, row(w["ln_mlp"][0]), w["w_up"][0], w["w_down"][0], 0)
    ple0 = ple_forward(x2, row(w["ln_ple"][0]), w["w_ple_gate"][0], p[0], w["w_ple_proj"][0], name="ple_0")
    x3 = ple0[4]
    if late is not None:
        late.after_first_layer(x3)
    h_kv, r_kv, kv_pre = norm_matmul(x3, row(w["ln_kv"]), w["w_kv"], name="kv_proj")
    k_n, v_b = head_norm(kv_pre, g128(w["g_k"]), name="k_norm", passthrough=True)
    if late is not None:
        w = {**w, **late.second_layer_weights(k_n)}
    h_q, r_q, q_pre = norm_matmul(x3, row(w["ln_mix_b"][0]), w["w_q"][0], name="q_proj")
    q_n = head_norm(q_pre, g128(w["g_q"][0]), name="q_norm", scale=scale)
    o = stick_breaking_forward(q_n, k_n, v_b, name="sb_fwd")
    x4 = matmul_residual(o, w["w_out_b"][0], x3, name="sb_out")
    x5, mlp1 = _mlp_forward(x4, row(w["ln_mlp"][1]), w["w_up"][1], w["w_down"][1], 1)
    ple1 = ple_forward(x5, row(w["ln_ple"][1]), w["w_ple_gate"][1], p[1], w["w_ple_proj"][1], name="ple_1")
    x6 = ple1[4]
    loss_blk, dx = loss_forward(x6, target, name="loss")

    g = {}
    dx, dwg1, dwp1, dlnp1 = _ple_backward(dx, (x5,) + tuple(ple1[:4]), p[1], row(w["ln_ple"][1]), w["w_ple_gate"][1], 1)
    dx, dwu1, dwd1, dlnm1 = _mlp_backward(dx, mlp1, row(w["ln_mlp"][1]), w["w_up"][1], w["w_down"][1], 1)
    g["w_out_b"] = matmul_tn(o, dx, name="d_w_out_b", col_shards=False)
    do = matmul_nt(dx, w["w_out_b"][0], name="d_sb_out", out_dtype=BF16)
    dq_n, dk_n, dv = stick_breaking_backward(q_n, k_n, v_b, do, name="sb_bwd")
    dq_pre, dgq = head_norm_backward(dq_n, q_pre, g128(w["g_q"][0]), name="d_q_norm", scale=scale)
    dkv_pre, dgk = head_norm_backward(dk_n, kv_pre, g128(w["g_k"]), name="d_k_norm", passthrough=dv)
    g["w_q"] = matmul_tn(h_q, dq_pre, name="d_w_q", col_shards=False)
    g["w_kv"] = matmul_tn(h_kv, dkv_pre, name="d_w_kv", col_shards=True)
    dx, g["ln_mix_b"] = norm_backward(dq_pre, w["w_q"][0], x3, row(w["ln_mix_b"][0]), r_q, dx, name="d_q_in")
    dx, g["ln_kv"] = norm_backward(dkv_pre, w["w_kv"], x3, row(w["ln_kv"]), r_kv, dx, name="d_kv_in")
    g["g_q"] = dgq[:, :HEAD_DIM] + dgq[:, HEAD_DIM:]
    g["g_k"] = (dgk[:, :HEAD_DIM] + dgk[:, HEAD_DIM:]).reshape(HEAD_DIM)
    g["ln_kv"] = g["ln_kv"].reshape(D_MODEL)
    ln_ple0, ln_mlp0 = row(w["ln_ple"][0]), row(w["ln_mlp"][0])
    if late is not None:
        ln_ple0 = ln_ple0 + late.second_layer_grads(
            {("w_kv", 0): g["w_kv"], ("w_q", 0): g["w_q"], ("w_out_b", 0): g["w_out_b"], ("w_up", 1): dwu1,
             ("w_down", 1): dwd1, ("w_ple_gate", 1): dwg1, ("w_ple_proj", 1): dwp1}, dx)[0, 0]
    dx, dwg0, dwp0, dlnp0 = _ple_backward(dx, (x2,) + tuple(ple0[:4]), p[0], ln_ple0, w["w_ple_gate"][0], 0)
    if late is not None:
        ln_mlp0 = ln_mlp0 + late.first_ple_backward_done(dx)[0, 0]
    dx, dwu0, dwd0, dlnm0 = _mlp_backward(dx, mlp0, ln_mlp0, w["w_up"][0], w["w_down"][0], 0)
    g["w_out_a"] = matmul_tn(y_a, dx, name="d_w_out_a", col_shards=False)
    dy_a = matmul_nt(dx, w["w_out_a"][0], name="d_sgu_out")
    dpre_a, dws, db, g["g_v_a"] = sgu_backward(dy_a, pre_a, row(w["g_v_a"][0]), w_s, b_full, name="d_sgu_mix")
    g["w_in_a"] = matmul_tn(h_a, dpre_a, name="d_w_in_a", col_shards=True)
    dx, g["ln_mix_a"] = norm_backward(dpre_a, w["w_in_a"][0], x0, row(w["ln_mix_a"][0]), r_a, dx, name="d_sgu_in")
    g["w_spatial"] = dws[None]
    g["b_spatial"] = jnp.transpose(db[:, :N_GROUPS])[None]
    g["w_up"] = (dwu0, dwu1)
    g["w_down"] = (dwd0, dwd1)
    g["w_ple_gate"] = (dwg0, dwg1)
    g["w_ple_proj"] = (dwp0, dwp1)
    g["ln_mlp"] = jnp.concatenate([dlnm0, dlnm1], axis=0)
    g["ln_ple"] = jnp.concatenate([dlnp0, dlnp1], axis=0)
    return loss_blk, dx, g


ANY = pl.BlockSpec(memory_space=pl.ANY)


def _place():
    x, y, c = lax.axis_index("x"), lax.axis_index("y"), lax.axis_index("c")
    others = [(1 - x, y), (x, 1 - y), (1 - x, 1 - y)]
    return x, y, c, 2 * x + y, others


def cast_into_slot(w3, layer, slot, *, name, tm=256):
    _, r, c = w3.shape
    tm = min(tm, r)

    def body(slot_ref, w_ref, o_ref):
        o_ref[...] = w_ref[...].astype(BF16)

    return _pcall(body, name=name, out_shape=_sds((N_SHARDS, r, c), BF16), grid=(r // tm,), num_prefetch=1,
                  in_specs=[pl.BlockSpec((None, tm, c), lambda i, s: (layer, i, 0))],
                  out_specs=pl.BlockSpec((None, tm, c), lambda i, s: (s[0], i, 0)),
                  semantics=("parallel",))(slot, w3)


def gather_shards(mats, vecs, *, name):
    nm, nv = len(mats), len(vecs)
    halves = [m.reshape(N_SHARDS, 2, m.shape[1] // 2, m.shape[2]) for m in mats]

    def body(*refs):
        vsrc = refs[nm:nm + nv]
        out, vout = refs[nm + nv:2 * nm + nv], refs[2 * nm + nv:2 * (nm + nv)]
        send, recv, vsend, vrecv, loc = refs[2 * (nm + nv):]
        x, y, c, s_me, others = _place()
        sib = (x, y, 1 - c)

        def ici(l, k):
            ox, oy = others[k]
            return pltpu.make_async_remote_copy(out[l].at[s_me, c], out[l].at[s_me, c], send.at[l, k], recv.at[l, k],
                                                device_id=(ox, oy, c), device_id_type=MESH)

        def landed(l, k, half):
            ox, oy = others[k]
            return out[l].at[2 * ox + oy, half]

        def passed_on(l, k):
            return pltpu.make_async_remote_copy(landed(l, k, c), landed(l, k, c), send.at[l, 3 + k], recv.at[l, 3 + k],
                                                device_id=sib, device_id_type=MESH)

        def vec(l, k):
            ox, oy = others[k]
            return pltpu.make_async_remote_copy(vsrc[l], vout[l].at[s_me], vsend.at[l, k], vrecv.at[l, k],
                                                device_id=(ox, oy, c), device_id_type=MESH)

        for l in range(nm):
            for k in range(3):
                ici(l, k).start()
        for l in range(nv):
            for k in range(3):
                vec(l, k).start()
        for l in range(nv):
            own = pltpu.make_async_copy(vsrc[l], vout[l].at[s_me], loc)
            own.start()
            own.wait()
        for l in range(nm):
            for k in range(3):
                pltpu.make_async_remote_copy(landed(l, k, c), landed(l, k, c), send.at[l, k], recv.at[l, k],
                                             device_id=sib, device_id_type=MESH).wait_recv()
                passed_on(l, k).start()
        for l in range(nm):
            for k in range(3):
                pltpu.make_async_remote_copy(landed(l, k, 1 - c), landed(l, k, 1 - c), send.at[l, 3 + k],
                                             recv.at[l, 3 + k], device_id=sib, device_id_type=MESH).wait_recv()
        for l in range(nv):
            for k in range(3):
                ox, oy = others[k]
                pltpu.make_async_remote_copy(vsrc[l], vout[l].at[2 * ox + oy], vsend.at[l, k], vrecv.at[l, k],
                                             device_id=sib, device_id_type=MESH).wait_recv()
        for l in range(nm):
            for k in range(3):
                ici(l, k).wait_send()
                passed_on(l, k).wait_send()
        for l in range(nv):
            for k in range(3):
                vec(l, k).wait_send()

    out_shape = [_sds(h.shape, BF16) for h in halves] + [_sds((N_SHARDS,) + v.shape, F32) for v in vecs]
    res = _pcall(body, name=name, out_shape=out_shape, in_specs=[ANY] * (nm + nv), out_specs=[ANY] * (nm + nv),
                 scratch_shapes=[pltpu.SemaphoreType.DMA((nm, 6)), pltpu.SemaphoreType.DMA((nm, 6)),
                                 pltpu.SemaphoreType.DMA((max(nv, 1), 3)), pltpu.SemaphoreType.DMA((max(nv, 1), 3)),
                                 pltpu.SemaphoreType.DMA(())],
                 aliases={l: l for l in range(nm)}, side_effects=True)(*halves, *vecs)
    return [r.reshape(m.shape) for r, m in zip(res[:nm], mats)], list(res[nm:])


HBM = pl.BlockSpec(memory_space=pltpu.HBM)
SEM = pl.BlockSpec(memory_space=pltpu.SEMAPHORE)
DATAFLOW = pltpu.SideEffectType.DATAFLOW_SIDE_EFFECTING


def _split_call(body, *, name, out_shape, in_specs, out_specs, aliases):
    return pl.pallas_call(body, name=name, out_shape=out_shape, in_specs=in_specs, out_specs=out_specs,
                          input_output_aliases=aliases,
                          compiler_params=pltpu.CompilerParams(has_side_effects=DATAFLOW))


def _token_shape():
    return jax.ShapeDtypeStruct((8, LANES), F32)


def gather_start(mats, after, *, name):
    n = len(mats)
    halves = [pltpu.with_memory_space_constraint(m.reshape(N_SHARDS, 2, m.shape[1] // 2, m.shape[2]), pltpu.HBM)
              for m in mats]

    def body(*refs):
        send, recv = refs[n + 1], refs[n + 2]
        out, token = refs[n + 3:2 * n + 3], refs[2 * n + 3]
        x, y, c, s_me, others = _place()
        for l in range(n):
            for k in range(3):
                ox, oy = others[k]
                pltpu.make_async_remote_copy(out[l].at[s_me, c], out[l].at[s_me, c], send.at[3 * l + k],
                                             recv.at[3 * l + k], device_id=(ox, oy, c), device_id_type=MESH).start()
        token[...] = jnp.zeros_like(token)

    res = _split_call(
        body, name=name,
        out_shape=(pltpu.SemaphoreType.DMA((3 * n,)), pltpu.SemaphoreType.DMA((3 * n,)),
                   *[pltpu.HBM(h.shape, BF16) for h in halves], _token_shape()),
        in_specs=[HBM] * n + [ANY], out_specs=(SEM, SEM, *[HBM] * n, pl.BlockSpec(memory_space=pltpu.VMEM)),
        aliases={l: 2 + l for l in range(n)})(*halves, after)
    return res[0], res[1], list(res[2:2 + n]), res[2 + n]


def gather_pass_on(bufs, send_a, recv_a, after, *, name):
    n = len(bufs)

    def body(*refs):
        send_a, recv_a = refs[n], refs[n + 1]
        out = refs[n + 3:2 * n + 3]
        send_b, recv_b, token = refs[2 * n + 3:]
        x, y, c, s_me, others = _place()
        for l in range(n):
            for k in range(3):
                ox, oy = others[k]
                landed, i = out[l].at[2 * ox + oy, c], 3 * l + k
                pltpu.make_async_remote_copy(landed, landed, send_a.at[i], recv_a.at[i],
                                             device_id=(x, y, 1 - c), device_id_type=MESH).wait_recv()
                pltpu.make_async_remote_copy(landed, landed, send_b.at[i], recv_b.at[i],
                                             device_id=(x, y, 1 - c), device_id_type=MESH).start()
        for l in range(n):
            for k in range(3):
                mine, i = out[l].at[s_me, c], 3 * l + k
                pltpu.make_async_remote_copy(mine, mine, send_a.at[i], recv_a.at[i],
                                             device_id=(x, y, 1 - c), device_id_type=MESH).wait_send()
        token[...] = jnp.zeros_like(token)

    res = _split_call(
        body, name=name,
        out_shape=(*[pltpu.HBM(b.shape, BF16) for b in bufs], pltpu.SemaphoreType.DMA((3 * n,)),
                   pltpu.SemaphoreType.DMA((3 * n,)), _token_shape()),
        in_specs=[HBM] * n + [SEM, SEM, ANY],
        out_specs=(*[HBM] * n, SEM, SEM, pl.BlockSpec(memory_space=pltpu.VMEM)),
        aliases={l: l for l in range(n)})(*bufs, send_a, recv_a, after)
    return list(res[:n]), res[n], res[n + 1], res[n + 2]


def gather_finish(bufs, send_b, recv_b, after, shapes, *, name):
    n = len(bufs)

    def body(*refs):
        send_b, recv_b = refs[n], refs[n + 1]
        out = refs[n + 3:]
        x, y, c, _, others = _place()
        for l in range(n):
            for k in range(3):
                ox, oy = others[k]
                theirs, mine, i = out[l].at[2 * ox + oy, 1 - c], out[l].at[2 * ox + oy, c], 3 * l + k
                pltpu.make_async_remote_copy(theirs, theirs, send_b.at[i], recv_b.at[i],
                                             device_id=(x, y, 1 - c), device_id_type=MESH).wait_recv()
                pltpu.make_async_remote_copy(mine, mine, send_b.at[i], recv_b.at[i],
                                             device_id=(x, y, 1 - c), device_id_type=MESH).wait_send()

    res = _split_call(
        body, name=name, out_shape=tuple(pltpu.HBM(b.shape, BF16) for b in bufs),
        in_specs=[HBM] * n + [SEM, SEM, ANY], out_specs=tuple([HBM] * n),
        aliases={l: l for l in range(n)})(*bufs, send_b, recv_b, after)
    return [r.reshape(s) for r, s in zip(res, shapes)]


def exchange_start(srcs, dst_shapes, dst_dtype, plan, count, after, *, name):
    n, m = len(srcs), len(dst_shapes)
    srcs = [pltpu.with_memory_space_constraint(s, pltpu.HBM) for s in srcs]
    lands = [pltpu.with_memory_space_constraint(lax.empty(s, dst_dtype), pltpu.HBM) for s in dst_shapes]

    def body(*refs):
        send, recv = refs[n + m + 1], refs[n + m + 2]
        src, dst, token = refs[n + m + 3:2 * n + m + 3], refs[2 * n + m + 3:2 * (n + m) + 3], refs[2 * (n + m) + 3]
        for i, (s, d, dev) in enumerate(plan(_place(), src, dst)):
            pltpu.make_async_remote_copy(s, d, send.at[i], recv.at[i], device_id=dev, device_id_type=MESH).start()
        token[...] = jnp.zeros_like(token)

    res = _split_call(
        body, name=name,
        out_shape=(pltpu.SemaphoreType.DMA((count,)), pltpu.SemaphoreType.DMA((count,)),
                   *[pltpu.HBM(s.shape, s.dtype) for s in srcs], *[pltpu.HBM(s, dst_dtype) for s in dst_shapes],
                   _token_shape()),
        in_specs=[HBM] * (n + m) + [ANY],
        out_specs=(SEM, SEM, *[HBM] * (n + m), pl.BlockSpec(memory_space=pltpu.VMEM)),
        aliases={i: 2 + i for i in range(n + m)})(*srcs, *lands, after)
    return (list(res[2:2 + n]), list(res[2 + n:2 + n + m]), res[0], res[1], plan), res[2 + n + m]


def exchange_finish(state, after, *, name):
    srcs, lands, send, recv, plan = state
    n, m = len(srcs), len(lands)

    def body(*refs):
        send, recv = refs[n + m], refs[n + m + 1]
        src, dst = refs[n + m + 3:2 * n + m + 3], refs[2 * n + m + 3:]
        for i, (s, d, dev) in enumerate(plan(_place(), src, dst)):
            pltpu.make_async_remote_copy(s, d, send.at[i], recv.at[i], device_id=dev, device_id_type=MESH).wait()

    res = _split_call(
        body, name=name,
        out_shape=tuple(pltpu.HBM(a.shape, a.dtype) for a in srcs + lands),
        in_specs=[HBM] * (n + m) + [SEM, SEM, ANY], out_specs=tuple([HBM] * (n + m)),
        aliases={i: i for i in range(n + m)})(*srcs, *lands, send, recv, after)
    return list(res[:n]), list(res[n:])


def pair_plan(place, src, dst):
    x, y, c, _, _ = place
    return [(s.at[:, 1 - c], d, (x, y, 1 - c)) for s, d in zip(src, dst)]


def chip_plan(place, src, dst):
    x, y, c, _, others = place
    return [(s.at[2 * ox + oy], d.at[k], (ox, oy, c)) for s, d in zip(src, dst) for k, (ox, oy) in enumerate(others)]


def pair_exchange(grads, *, name):
    n = len(grads)

    def body(*refs):
        src, got = refs[:n], refs[n:2 * n]
        send, recv = refs[2 * n:]
        x, y, c, _, _ = _place()

        def swap(l):
            return pltpu.make_async_remote_copy(src[l].at[:, 1 - c], got[l], send.at[l], recv.at[l],
                                                device_id=(x, y, 1 - c), device_id_type=MESH)

        for l in range(n):
            swap(l).start()
        for l in range(n):
            swap(l).wait()

    res = _pcall(body, name=name, out_shape=[_sds((N_SHARDS,) + g.shape[2:], F32) for g in grads],
                 in_specs=[ANY] * n, out_specs=[ANY] * n,
                 scratch_shapes=[pltpu.SemaphoreType.DMA((n,)), pltpu.SemaphoreType.DMA((n,))],
                 side_effects=True)(*grads)
    return list(res)


def add_to_wire(mine, theirs, core, *, name, tm=256):
    s, _, r, c = mine.shape
    tm = min(tm, r)

    def body(core_ref, a_ref, b_ref, o_ref):
        o_ref[...] = (a_ref[...] + b_ref[...]).astype(BF16)

    spec = pl.BlockSpec((None, tm, c), lambda i, j, cr: (i, j, 0))
    return _pcall(body, name=name, out_shape=_sds((s, r, c), BF16), grid=(s, r // tm), num_prefetch=1,
                  in_specs=[pl.BlockSpec((None, None, tm, c), lambda i, j, cr: (i, cr[0], j, 0)), spec],
                  out_specs=spec, semantics=("parallel", "parallel"))(core, mine, theirs)


def sum_chips(wire, landed, place, dest, layer, n_layers, *, name, tm=256):
    _, r, c = wire.shape
    tm = min(tm, r)

    def body(place_ref, w_ref, l_ref, *rest):
        o_ref = rest[-1]
        o_ref[...] = ((w_ref[...].astype(F32) + l_ref[0].astype(F32)) + l_ref[1].astype(F32)) + l_ref[2].astype(F32)

    in_specs = [pl.BlockSpec((None, tm, c), lambda i, pr: (pr[0], i, 0)),
                pl.BlockSpec((3, tm, c), lambda i, pr: (0, i, 0))]
    args = [place, wire, landed]
    aliases = None
    if dest is not None:
        in_specs.append(ANY)
        args.append(dest)
        aliases = {3: 0}
    return _pcall(body, name=name, out_shape=_sds((n_layers, 2, r, c), F32), grid=(r // tm,), num_prefetch=1,
                  in_specs=in_specs,
                  out_specs=pl.BlockSpec((None, None, tm, c), lambda i, pr: (layer, pr[1], i, 0)),
                  aliases=aliases, semantics=("parallel",))(*args)


def pair_share(bufs, slots, *, name):
    n = len(bufs)

    def body(*refs):
        out = refs[n:2 * n]
        send, recv = refs[2 * n:]
        x, y, c, _, _ = _place()

        def share(i, half):
            o, l = slots[i]
            return pltpu.make_async_remote_copy(out[o].at[l, half], out[o].at[l, half], send.at[i], recv.at[i],
                                                device_id=(x, y, 1 - c), device_id_type=MESH)

        for i in range(len(slots)):
            share(i, c).start()
        for i in range(len(slots)):
            share(i, 1 - c).wait_recv()
            share(i, c).wait_send()

    res = _pcall(body, name=name, out_shape=[_sds(b.shape, F32) for b in bufs], in_specs=[ANY] * n,
                 out_specs=[ANY] * n,
                 scratch_shapes=[pltpu.SemaphoreType.DMA((len(slots),)), pltpu.SemaphoreType.DMA((len(slots),))],
                 aliases={o: o for o in range(n)}, side_effects=True)(*bufs)
    return list(res)


def all_reduce_small(packed, *, name):
    n_dev, r, c = packed.shape

    def body(in_ref, out_ref, land, send, recv):
        x, y, cc, _, _ = _place()
        me = 4 * x + 2 * y + cc
        peers = [(px, py, pc) for px in range(2) for py in range(2) for pc in range(2)]

        def scatter(d):
            return pltpu.make_async_remote_copy(in_ref.at[d], land.at[me], send.at[0, d], recv.at[0, me],
                                                device_id=peers[d], device_id_type=MESH)

        def gather(d):
            return pltpu.make_async_remote_copy(out_ref.at[me], out_ref.at[me], send.at[1, d], recv.at[1, me],
                                                device_id=peers[d], device_id_type=MESH)

        for d in range(n_dev):
            @pl.when(d != me)
            def _():
                scatter(d).start()
        land[me] = in_ref[me]
        for d in range(n_dev):
            @pl.when(d != me)
            def _():
                pltpu.make_async_remote_copy(in_ref.at[d], land.at[d], send.at[0, d], recv.at[0, d],
                                             device_id=peers[d], device_id_type=MESH).wait_recv()
        total = land[0]
        for d in range(1, n_dev):
            total = total + land[d]
        out_ref[me] = total
        for d in range(n_dev):
            @pl.when(d != me)
            def _():
                gather(d).start()
        for d in range(n_dev):
            @pl.when(d != me)
            def _():
                pltpu.make_async_remote_copy(out_ref.at[d], out_ref.at[d], send.at[1, d], recv.at[1, d],
                                             device_id=peers[d], device_id_type=MESH).wait_recv()
        for d in range(n_dev):
            @pl.when(d != me)
            def _():
                scatter(d).wait_send()
                gather(d).wait_send()

    vm = pl.BlockSpec(memory_space=pltpu.VMEM)
    return _pcall(body, name=name, out_shape=_sds(packed.shape, F32), in_specs=[vm], out_specs=vm,
                  scratch_shapes=[pltpu.VMEM(packed.shape, F32), pltpu.SemaphoreType.DMA((2, n_dev)),
                                  pltpu.SemaphoreType.DMA((2, n_dev))],
                  side_effects=True)(packed)


def adamw(w, g, m, v, *, name, tm=256):
    shape = w.shape
    cols = shape[-1]
    rows = 1
    for s in shape[:-1]:
        rows *= s
    tm = min(tm, rows)
    assert rows % tm == 0
    two_d = lambda a: a.reshape(rows, cols)

    def body(w_ref, g_ref, m_ref, v_ref, d_ref, mo_ref, vo_ref):
        gv = g_ref[...]
        m_new = ADAM_B1 * m_ref[...] + (1.0 - ADAM_B1) * gv
        v_new = ADAM_B2 * v_ref[...] + (1.0 - ADAM_B2) * (gv * gv)
        m_hat = m_new / (1.0 - ADAM_B1 ** ADAM_STEP)
        v_hat = v_new / (1.0 - ADAM_B2 ** ADAM_STEP)
        d_ref[...] = -ADAM_LR * (m_hat / (jnp.sqrt(v_hat) + ADAM_EPS) + ADAM_WD * w_ref[...])
        mo_ref[...] = m_new
        vo_ref[...] = v_new

    spec = pl.BlockSpec((tm, cols), lambda i: (i, 0))
    outs = _pcall(body, name=name, out_shape=[_sds((rows, cols), F32)] * 3, grid=(rows // tm,), in_specs=[spec] * 4,
                  out_specs=[spec] * 3, semantics=("parallel",))(two_d(w), two_d(g), two_d(m), two_d(v))
    return [o.reshape(shape) for o in outs]


WEIGHTS = ("ln_mix_a", "w_in_a", "g_v_a", "w_spatial", "b_spatial", "w_out_a", "ln_kv", "w_kv", "g_k", "ln_mix_b",
           "w_q", "g_q", "w_out_b", "ln_mlp", "w_up", "w_down", "ln_ple", "w_ple_gate", "w_ple_proj")
MATRICES = (("w_in_a", 1, True), ("w_out_a", 1, False), ("w_kv", 0, True), ("w_q", 1, False), ("w_out_b", 1, False),
            ("w_up", 2, True), ("w_down", 2, False), ("w_ple_gate", 2, False), ("w_ple_proj", 2, True))
FIRST_LAYER = ("w_in_a", "w_out_a", "w_kv", "w_up", "w_down", "w_ple_gate", "w_ple_proj")
REPLICATED = ("w_spatial", "b_spatial", "ln_kv", "g_k", "ln_mix_b", "g_q", "ln_mlp", "ln_ple")
SHARDED_VECTORS = ("ln_mix_a", "g_v_a")
SMALL_ROWS = 18


def kernel(x, p, ln_mix_a, w_in_a, g_v_a, w_spatial, b_spatial, w_out_a, ln_kv, w_kv, g_k, ln_mix_b, w_q, g_q, w_out_b, ln_mlp, w_up, w_down, ln_ple, w_ple_gate, w_ple_proj, loss_target, m_ln_mix_a, m_w_in_a, m_g_v_a, m_w_spatial, m_b_spatial, m_w_out_a, m_ln_kv, m_w_kv, m_g_k, m_ln_mix_b, m_w_q, m_g_q, m_w_out_b, m_ln_mlp, m_w_up, m_w_down, m_ln_ple, m_w_ple_gate, m_w_ple_proj, v_ln_mix_a, v_w_in_a, v_g_v_a, v_w_spatial, v_b_spatial, v_w_out_a, v_ln_kv, v_w_kv, v_g_k, v_ln_mix_b, v_w_q, v_g_q, v_w_out_b, v_ln_mlp, v_w_up, v_w_down, v_ln_ple, v_w_ple_gate, v_w_ple_proj):
    given = dict(locals())
    weights = {n: given[n] for n in WEIGHTS}
    shard = 2 * lax.axis_index("x") + lax.axis_index("y")
    core = lax.axis_index("c")
    shard_1 = shard.astype(jnp.int32).reshape(1)
    core_1 = core.astype(jnp.int32).reshape(1)
    place = jnp.stack([shard, core]).astype(jnp.int32)

    leaves = []
    for name, layers, cols in MATRICES:
        w3 = weights[name] if layers else weights[name][None]
        for layer in range(max(layers, 1)):
            leaves.append((name, layer, cols, cast_into_slot(w3, layer, shard_1, name=f"cast_{name}_{layer}")))
    first = [lf for lf in leaves if lf[0] in FIRST_LAYER and lf[1] == 0]
    second = [lf for lf in leaves if not (lf[0] in FIRST_LAYER and lf[1] == 0)]
    got_a, vec_a = gather_shards([lf[3] for lf in first], [ln_mix_a, g_v_a], name="gather_layer0")
    send_a, recv_a, flying, token = gather_start([lf[3] for lf in second], got_a[0], name="gather_layer1_start")

    def assemble(leaf_list, arrays):
        full = {}
        for (name, layer, cols, _), arr in zip(leaf_list, arrays):
            if not cols:
                arr = arr.reshape(N_SHARDS * arr.shape[1], arr.shape[2])
            full.setdefault(name, {})[layer] = arr
        return full

    full_a = assemble(first, got_a)
    w = {name: ((full_a[name][0],) if layers else full_a[name][0]) for name, layers, _ in MATRICES if name in full_a}
    w["ln_mix_a"] = vec_a[0].reshape(1, D_MODEL) + token[0, 0]
    w["g_v_a"] = vec_a[1].reshape(1, D_MODEL)
    for name in REPLICATED:
        w[name] = weights[name]

    class Late:
        def after_first_layer(self, x_done):
            self.passed = gather_pass_on(flying, send_a, recv_a, x_done, name="gather_layer1_pass_on")

        def second_layer_weights(self, k_done):
            bufs, send_b, recv_b, _ = self.passed
            got_b = gather_finish(bufs, send_b, recv_b, k_done, [lf[3].shape for lf in second],
                                  name="gather_layer1_finish")
            full_b = assemble(second, got_b)
            out = {}
            for name, layers, _ in MATRICES:
                if name in full_b:
                    both = {**full_a.get(name, {}), **full_b[name]}
                    out[name] = tuple(both[l] for l in sorted(both))
            return out

        def second_layer_grads(self, grads_late, dx_done):
            self.keys = sorted(grads_late)
            views = [view(k, grads_late[k]) for k in self.keys]
            self.pair, token = exchange_start(views, [(N_SHARDS,) + v.shape[2:] for v in views], F32, pair_plan,
                                              len(views), dx_done, name="grad_pair_start_1")
            return token

        def first_ple_backward_done(self, dx_done):
            mine, theirs = exchange_finish(self.pair, dx_done, name="grad_pair_finish_1")
            wire = [add_to_wire(a, b, core_1, name=f"grad_pair_sum_1_{i}") for i, (a, b) in enumerate(zip(mine, theirs))]
            self.chip, token = exchange_start(wire, [(3,) + v.shape[1:] for v in wire], BF16, chip_plan, 3 * len(wire),
                                              wire[-1], name="grad_chip_start_1")
            return token

    col_sharded = {name: cols for name, _, cols in MATRICES}
    layer_count = {name: max(layers, 1) for name, layers, _ in MATRICES}

    def view(key, arr):
        rows = arr.shape[-2] if col_sharded[key[0]] else arr.shape[0] // N_SHARDS
        return arr.reshape(N_SHARDS, 2, rows // 2, arr.shape[-1])

    t = x.shape[1]
    late = Late()
    loss_blk, dx, g = local_step(x[0], p.reshape(2, t, PLE_DIM), loss_target[0], w, late)
    loss = lax.psum(loss_blk[0, 0], ("x", "y", "c"))

    keys_early = [(name, layer) for name, layers, _ in MATRICES for layer in range(max(layers, 1))
                  if (name, layer) not in late.keys]
    views = [view(k, g[k[0]][k[1]] if layer_count[k[0]] == 2 else g[k[0]]) for k in keys_early]
    theirs = pair_exchange(views, name="grad_pair_exchange_0")
    wire_0 = [add_to_wire(a, b, core_1, name=f"grad_pair_sum_0_{i}") for i, (a, b) in enumerate(zip(views, theirs))]
    chip_0, token_0 = exchange_start(wire_0, [(3,) + v.shape[1:] for v in wire_0], BF16, chip_plan, 3 * len(wire_0),
                                     wire_0[-1], name="grad_chip_start_0")

    bufs = {}

    def sum_and_share(keys, wire, landed, tag):
        for i, (key, wv, lv) in enumerate(zip(keys, wire, landed)):
            name, layer = key
            bufs[name] = sum_chips(wv, lv, place, bufs.get(name), layer, layer_count[name],
                                   name=f"grad_chip_sum_{tag}_{i}")
        names = sorted({k[0] for k in keys})
        shared = pair_share([bufs[n] for n in names], [(names.index(k[0]), k[1]) for k in keys],
                            name=f"grad_pair_share_{tag}")
        bufs.update(zip(names, shared))

    wire_1, landed_1 = exchange_finish(late.chip, token_0, name="grad_chip_finish_1")
    sum_and_share(late.keys, wire_1, landed_1, 1)

    grads = {}
    small = REPLICATED + SHARDED_VECTORS
    flat = jnp.concatenate([g[n].reshape(-1) for n in small])
    room = 8 * SMALL_ROWS * D_MODEL
    flat = jnp.concatenate([flat, jnp.zeros((room - flat.shape[0],), F32)])
    reduced = all_reduce_small(flat.reshape(8, SMALL_ROWS, D_MODEL), name="grad_small_all_reduce").reshape(-1)
    at = 0
    for n in small:
        size = g[n].size
        piece = reduced[at:at + size]
        at += size
        if n in SHARDED_VECTORS:
            per = D_MODEL // N_SHARDS
            grads[n] = lax.dynamic_slice(piece, (shard * per,), (per,)).reshape(weights[n].shape)
        else:
            grads[n] = piece.reshape(weights[n].shape)

    wire_0, landed_0 = exchange_finish(chip_0, reduced, name="grad_chip_finish_0")
    sum_and_share(keys_early, wire_0, landed_0, 0)
    for name, _, _ in MATRICES:
        grads[name] = bufs[name].reshape(weights[name].shape)

    delta, new_m, new_v = {}, {}, {}
    for n in WEIGHTS:
        wn, gn, mn, vn = weights[n], grads[n], given["m_" + n], given["v_" + n]
        if wn.ndim == 1:
            wn, gn, mn, vn = (a.reshape(1, -1) for a in (wn, gn, mn, vn))
        outs = adamw(wn, gn, mn, vn, name=f"adamw_{n}")
        delta[n], new_m[n], new_v[n] = (o.reshape(weights[n].shape) for o in outs)
    return (loss, dx.reshape(x.shape), *[grads[n] for n in WEIGHTS], *[delta[n] for n in WEIGHTS],
            *[new_m[n] for n in WEIGHTS], *[new_v[n] for n in WEIGHTS])
```

```python
import jax
import jax.numpy as jnp
from jax import lax
from jax.experimental import pallas as pl
from jax.experimental.pallas import tpu as pltpu

F32 = jnp.float32
BF16 = jnp.bfloat16

D_MODEL = 1024
D_FF = 4096
PLE_DIM = 256
N_GROUPS = 8
CHUNK = 128
HEAD_DIM = 64
LANES = 128
ATT_BLOCK = 256
EPS = 1e-6
N_SHARDS = 4
VMEM_LIMIT = 56 * 1024 * 1024

ADAM_LR = 0.001
ADAM_B1 = 0.9
ADAM_B2 = 0.999
ADAM_EPS = 1e-08
ADAM_WD = 0.01
ADAM_STEP = 10

MESH = pl.DeviceIdType.MESH


def _pcall(body, *, name, out_shape, grid=None, in_specs=None, out_specs=None, scratch_shapes=(),
           semantics=None, aliases=None, side_effects=False, num_prefetch=0):
    params = dict(vmem_limit_bytes=VMEM_LIMIT)
    if semantics is not None:
        params["dimension_semantics"] = semantics
    if side_effects:
        params["has_side_effects"] = True
    kwargs = {}
    if aliases:
        kwargs["input_output_aliases"] = aliases
    if num_prefetch:
        spec = pltpu.PrefetchScalarGridSpec(num_scalar_prefetch=num_prefetch, grid=grid, in_specs=in_specs,
                                            out_specs=out_specs, scratch_shapes=list(scratch_shapes))
        return pl.pallas_call(body, name=name, out_shape=out_shape, grid_spec=spec,
                              compiler_params=pltpu.CompilerParams(**params), **kwargs)
    if grid is not None:
        kwargs["grid"] = grid
    if in_specs is not None:
        kwargs["in_specs"] = in_specs
    if out_specs is not None:
        kwargs["out_specs"] = out_specs
    if aliases:
        kwargs["input_output_aliases"] = aliases
    return pl.pallas_call(body, name=name, out_shape=out_shape, scratch_shapes=list(scratch_shapes),
                          compiler_params=pltpu.CompilerParams(**params), **kwargs)


def _sds(shape, dtype):
    return jax.ShapeDtypeStruct(shape, dtype)


_GELU_C = 0.7978845608028654
_GELU_A = 0.044715


def _gelu(x):
    inner = _GELU_C * (x + _GELU_A * (x * x * x))
    return 0.5 * x * (1.0 + jnp.tanh(inner))


def _gelu_grad(x):
    x2 = x * x
    t = jnp.tanh(_GELU_C * (x + _GELU_A * (x2 * x)))
    return 0.5 * (1.0 + t) + 0.5 * x * (1.0 - t * t) * (_GELU_C * (1.0 + 3.0 * _GELU_A * x2))


def _sigmoid(x):
    return 1.0 / (1.0 + jnp.exp(-x))


def _log_sigmoid(z):
    return jnp.minimum(z, 0.0) - jnp.log(1.0 + jnp.exp(-jnp.abs(z)))


def _dot(a, b):
    return jnp.dot(a, b, preferred_element_type=F32)


def _dot_nt(a, b):
    return lax.dot_general(a, b, (((1,), (1,)), ((), ())), preferred_element_type=F32)


def _dot_tn(a, b):
    return lax.dot_general(a, b, (((0,), (0,)), ((), ())), preferred_element_type=F32)


def _head_rstd(x):
    lane = lax.broadcasted_iota(jnp.int32, x.shape, 1)
    low = lane < HEAD_DIM
    sq = x * x
    s_lo = jnp.sum(jnp.where(low, sq, 0.0), axis=-1, keepdims=True)
    s_hi = jnp.sum(jnp.where(low, 0.0, sq), axis=-1, keepdims=True)
    ms = jnp.where(low, s_lo, s_hi) * (1.0 / HEAD_DIM)
    return lax.rsqrt(ms + EPS)


def _head_mean(x):
    lane = lax.broadcasted_iota(jnp.int32, x.shape, 1)
    low = lane < HEAD_DIM
    s_lo = jnp.sum(jnp.where(low, x, 0.0), axis=-1, keepdims=True)
    s_hi = jnp.sum(jnp.where(low, 0.0, x), axis=-1, keepdims=True)
    return jnp.where(low, s_lo, s_hi) * (1.0 / HEAD_DIM)


def _full(shape):
    zeros = (0,) * len(shape)
    return pl.BlockSpec(shape, lambda i: zeros)


def norm_matmul(x, g, w, *, name, epilogue="none", tm=512):
    t, d = x.shape
    sharded = w.ndim == 3
    per = w.shape[2] if sharded else w.shape[1]
    n = N_SHARDS * per if sharded else per
    tm = min(tm, t)

    def body(x_ref, g_ref, w_ref, h_ref, r_ref, *outs):
        xv = x_ref[...]
        r = lax.rsqrt(jnp.mean(xv * xv, axis=-1, keepdims=True) + EPS)
        h = ((xv * r) * g_ref[...]).astype(BF16)
        h_ref[...] = h
        r_ref[...] = r
        for s in range(N_SHARDS if sharded else 1):
            cols = slice(s * per, (s + 1) * per)
            y = _dot(h, w_ref[s] if sharded else w_ref[...])
            if epilogue == "none":
                outs[0][:, cols] = y
            else:
                a = jnp.maximum(y, 0.0)
                outs[0][:, cols] = a.astype(BF16)
                outs[1][:, cols] = (a * a).astype(BF16)

    row = lambda i: (i, 0)
    out_shape = [_sds((t, d), BF16), _sds((t, 1), F32)]
    out_specs = [pl.BlockSpec((tm, d), row), pl.BlockSpec((tm, 1), row)]
    if epilogue == "none":
        out_shape.append(_sds((t, n), F32))
        out_specs.append(pl.BlockSpec((tm, n), row))
    else:
        out_shape += [_sds((t, n), BF16), _sds((t, n), BF16)]
        out_specs += [pl.BlockSpec((tm, n), row)] * 2
    return _pcall(
        body, name=name, out_shape=out_shape, grid=(t // tm,),
        in_specs=[pl.BlockSpec((tm, d), row), _full((1, d)), _full(w.shape)],
        out_specs=out_specs, semantics=("parallel",))(x, g, w)


def matmul_residual(a, w, res, *, name, tm=512):
    t, k = a.shape
    n = w.shape[1]
    tm = min(tm, t)

    def body(a_ref, w_ref, res_ref, o_ref):
        o_ref[...] = res_ref[...] + _dot(a_ref[...], w_ref[...])

    row = lambda i: (i, 0)
    return _pcall(
        body, name=name, out_shape=_sds((t, n), F32), grid=(t // tm,),
        in_specs=[pl.BlockSpec((tm, k), row), _full(w.shape), pl.BlockSpec((tm, n), row)],
        out_specs=pl.BlockSpec((tm, n), row), semantics=("parallel",))(a, w, res)


def ple_forward(x, g, w_gate, p, w_proj, *, name, tm=256):
    t, d = x.shape
    tm = min(tm, t)

    def body(x_ref, g_ref, wg_ref, p_ref, wp_ref, h_ref, r_ref, gate_ref, pp_ref, o_ref):
        xv = x_ref[...]
        r = lax.rsqrt(jnp.mean(xv * xv, axis=-1, keepdims=True) + EPS)
        h = ((xv * r) * g_ref[...]).astype(BF16)
        h_ref[...] = h
        r_ref[...] = r
        gate = _sigmoid(_dot(h, wg_ref[...]))
        gate_ref[...] = gate
        pb = p_ref[...].astype(BF16)
        per = d // N_SHARDS
        for s in range(N_SHARDS):
            cols = slice(s * per, (s + 1) * per)
            pp = _dot(pb, wp_ref[s])
            pp_ref[:, cols] = pp.astype(BF16)
            o_ref[:, cols] = xv[:, cols] + pp * gate[:, cols]

    row = lambda i: (i, 0)
    fixed = lambda i: (0, 0)
    return _pcall(
        body, name=name,
        out_shape=[_sds((t, d), BF16), _sds((t, 1), F32), _sds((t, d), F32), _sds((t, d), BF16), _sds((t, d), F32)],
        grid=(t // tm,),
        in_specs=[pl.BlockSpec((tm, d), row), pl.BlockSpec((1, d), fixed), pl.BlockSpec((d, d), fixed),
                  pl.BlockSpec((tm, PLE_DIM), row),
                  pl.BlockSpec((N_SHARDS, PLE_DIM, d // N_SHARDS), lambda i: (0, 0, 0))],
        out_specs=[pl.BlockSpec((tm, d), row), pl.BlockSpec((tm, 1), row), pl.BlockSpec((tm, d), row),
                   pl.BlockSpec((tm, d), row), pl.BlockSpec((tm, d), row)],
        semantics=("parallel",))(x, g, w_gate, p, w_proj)


def _tril_mask():
    r = lax.broadcasted_iota(jnp.int32, (CHUNK, CHUNK), 0)
    c = lax.broadcasted_iota(jnp.int32, (CHUNK, CHUNK), 1)
    return c <= r


def _sgu_common(pre_ref, gv_ref, ws_ref):
    pre = pre_ref[...]
    pre_u, pre_v = pre[:, :D_MODEL], pre[:, D_MODEL:]
    u = _gelu(pre_u)
    v = _gelu(pre_v)
    r = lax.rsqrt(jnp.mean(v * v, axis=-1, keepdims=True) + EPS)
    vhat = v * r
    vn = (vhat * gv_ref[...]).astype(BF16)
    tril = _tril_mask()
    wm = [jnp.where(tril, ws_ref[g], 0.0).astype(BF16) for g in range(N_GROUPS)]
    return pre_u, pre_v, u, r, vhat, vn, wm, tril


def sgu_forward(pre, g_v, w_s, b_full, *, name):
    t = pre.shape[0]

    def body(pre_ref, gv_ref, ws_ref, b_ref, y_ref):
        _, _, u, _, _, vn, wm, _ = _sgu_common(pre_ref, gv_ref, ws_ref)
        for g in range(N_GROUPS):
            cols = slice(g * LANES, (g + 1) * LANES)
            mix = _dot(wm[g], vn[:, cols]) + b_ref[:, cols]
            y_ref[:, cols] = (u[:, cols] * mix).astype(BF16)

    return _pcall(
        body, name=name, out_shape=_sds((t, D_MODEL), BF16), grid=(t // CHUNK,),
        in_specs=[pl.BlockSpec((CHUNK, 2 * D_MODEL), lambda i: (i, 0)), pl.BlockSpec((1, D_MODEL), lambda i: (0, 0)),
                  pl.BlockSpec((N_GROUPS, CHUNK, CHUNK), lambda i: (0, 0, 0)),
                  pl.BlockSpec((CHUNK, D_MODEL), lambda i: (0, 0))],
        out_specs=pl.BlockSpec((CHUNK, D_MODEL), lambda i: (i, 0)),
        semantics=("parallel",))(pre, g_v, w_s, b_full)


def head_norm(pre, g128, *, name, col_block=0, scale=1.0, passthrough=False, tm=512):
    t = pre.shape[0]
    tm = min(tm, t)

    def body(*refs):
        if passthrough:
            x_ref, v_ref, g_ref, o_ref, vo_ref = refs
            vo_ref[...] = v_ref[...].astype(BF16)
        else:
            x_ref, g_ref, o_ref = refs
        g = g_ref[...] * scale
        for b in range(D_MODEL // LANES):
            cols = slice(b * LANES, (b + 1) * LANES)
            xv = x_ref[:, cols]
            o_ref[:, cols] = ((xv * _head_rstd(xv)) * g).astype(BF16)

    x_spec = pl.BlockSpec((tm, D_MODEL), lambda i: (i, col_block))
    g_spec = pl.BlockSpec((1, LANES), lambda i: (0, 0))
    o_spec = pl.BlockSpec((tm, D_MODEL), lambda i: (i, 0))
    if passthrough:
        return _pcall(body, name=name, out_shape=[_sds((t, D_MODEL), BF16)] * 2, grid=(t // tm,),
                      in_specs=[x_spec, pl.BlockSpec((tm, D_MODEL), lambda i: (i, 1)), g_spec],
                      out_specs=[o_spec, o_spec], semantics=("parallel",))(pre, pre, g128)
    return _pcall(body, name=name, out_shape=_sds((t, D_MODEL), BF16), grid=(t // tm,),
                  in_specs=[x_spec, g_spec], out_specs=o_spec, semantics=("parallel",))(pre, g128)


def _suffix_matrix(n):
    r = lax.broadcasted_iota(jnp.int32, (n, n), 0)
    c = lax.broadcasted_iota(jnp.int32, (n, n), 1)
    return jnp.where(r > c, 1.0, 0.0).astype(BF16)


def _prefix_matrix(n):
    r = lax.broadcasted_iota(jnp.int32, (n, n), 0)
    c = lax.broadcasted_iota(jnp.int32, (n, n), 1)
    return jnp.where(r < c, 1.0, 0.0).astype(BF16)


def _block_cumsum(a, tri):
    return _dot(a.astype(BF16), tri)


def _stacked_causal(n):
    r = lax.broadcasted_iota(jnp.int32, (2 * n, n), 0)
    c = lax.broadcasted_iota(jnp.int32, (2 * n, n), 1)
    return c < jnp.where(r >= n, r - n, r)


def _stack_heads(a, low):
    zero = jnp.zeros_like(a)
    return jnp.concatenate([jnp.where(low, a, zero), jnp.where(low, zero, a)], axis=0)


def stick_breaking_forward(q, k, v, *, name):
    t = q.shape[0]
    blk = min(ATT_BLOCK, t)
    nq = t // blk

    def body(q_ref, k_ref, v_ref, o_ref):
        i = pl.program_id(1)
        low = lax.broadcasted_iota(jnp.int32, (blk, LANES), 1) < HEAD_DIM
        tri = _suffix_matrix(blk)
        causal = _stacked_causal(blk)
        qs = _stack_heads(q_ref[...], low)

        def block(j, carry, acc, masked):
            rows = pl.ds(pl.multiple_of(j * blk, blk), blk)
            z = _dot_nt(qs, k_ref[rows, :])
            ls = _log_sigmoid(z)
            lg = ls - z
            if masked:
                lg = jnp.where(causal, lg, 0.0)
            s = ls + _block_cumsum(lg, tri) + carry
            a = jnp.exp(s)
            if masked:
                a = jnp.where(causal, a, 0.0)
            acc = acc + _dot(a.astype(BF16), v_ref[rows, :])
            return carry + jnp.sum(lg, axis=-1, keepdims=True), acc

        state = block(i, jnp.zeros((2 * blk, 1), F32), jnp.zeros((2 * blk, LANES), F32), True)

        def two_blocks(n, st):
            st = block(i - 1 - 2 * n, st[0], st[1], False)
            return block(i - 2 - 2 * n, st[0], st[1], False)

        state = lax.fori_loop(0, i // 2, two_blocks, state)
        _, acc = lax.fori_loop(0, i % 2, lambda n, st: block(0, st[0], st[1], False), state)
        o_ref[...] = jnp.where(low, acc[:blk], acc[blk:]).astype(BF16)

    return _pcall(
        body, name=name, out_shape=_sds((t, D_MODEL), BF16), grid=(D_MODEL // LANES, nq),
        in_specs=[pl.BlockSpec((blk, LANES), lambda p, i: (i, p)), pl.BlockSpec((t, LANES), lambda p, i: (0, p)),
                  pl.BlockSpec((t, LANES), lambda p, i: (0, p))],
        out_specs=pl.BlockSpec((blk, LANES), lambda p, i: (i, p)),
        semantics=("parallel", "arbitrary"))(q, k, v)


def loss_forward(x, target, *, name, tm=512):
    t, d = x.shape
    tm = min(tm, t)

    def body(x_ref, t_ref, l_ref, dx_ref):
        @pl.when(pl.program_id(0) == 0)
        def _():
            l_ref[...] = jnp.zeros_like(l_ref)

        diff = x_ref[...] - t_ref[...]
        dx_ref[...] = diff * (1.0 / d)
        l_ref[...] += 0.5 * jnp.sum(jnp.mean(diff * diff, axis=-1, keepdims=True))

    return _pcall(
        body, name=name, out_shape=[_sds((8, LANES), F32), _sds((t, d), F32)], grid=(t // tm,),
        in_specs=[pl.BlockSpec((tm, d), lambda i: (i, 0))] * 2,
        out_specs=[pl.BlockSpec((8, LANES), lambda i: (0, 0)), pl.BlockSpec((tm, d), lambda i: (i, 0))],
        semantics=("arbitrary",))(x, target)


def matmul_nt(dy, w, *, name, mul=None, out_dtype=F32, tm=512):
    t, n = dy.shape
    k = w.shape[0]
    tm = min(tm, t)

    def body(*refs):
        if mul is None:
            dy_ref, w_ref, o_ref = refs
        else:
            dy_ref, w_ref, m_ref, o_ref = refs
        y = _dot_nt(dy_ref[...].astype(BF16), w_ref[...])
        if mul is not None:
            y = y * (2.0 * m_ref[...].astype(F32))
        o_ref[...] = y.astype(out_dtype)

    row = lambda i: (i, 0)
    in_specs = [pl.BlockSpec((tm, n), row), _full(w.shape)]
    args = [dy, w]
    if mul is not None:
        in_specs.append(pl.BlockSpec((tm, k), row))
        args.append(mul)
    return _pcall(body, name=name, out_shape=_sds((t, k), out_dtype), grid=(t // tm,), in_specs=in_specs,
                  out_specs=pl.BlockSpec((tm, k), row), semantics=("parallel",))(*args)


def matmul_tn(a, dy, *, name, col_shards, tk=512):
    t, k = a.shape
    n = dy.shape[1]
    if col_shards:
        tn = n // N_SHARDS

        def body(a_ref, dy_ref, o_ref):
            o_ref[...] = _dot_tn(a_ref[...].astype(BF16), dy_ref[...].astype(BF16))

        return _pcall(body, name=name, out_shape=_sds((N_SHARDS, k, tn), F32), grid=(N_SHARDS,),
                      in_specs=[_full((t, k)), pl.BlockSpec((t, tn), lambda j: (0, j))],
                      out_specs=pl.BlockSpec((None, k, tn), lambda j: (j, 0, 0)), semantics=("parallel",))(a, dy)

    tk = min(tk, k)

    def body(a_ref, dy_ref, o_ref, dy_bf):
        @pl.when(pl.program_id(0) == 0)
        def _():
            dy_bf[...] = dy_ref[...].astype(BF16)

        o_ref[...] = _dot_tn(a_ref[...].astype(BF16), dy_bf[...])

    return _pcall(body, name=name, out_shape=_sds((k, n), F32), grid=(k // tk,),
                  in_specs=[pl.BlockSpec((t, tk), lambda i: (0, i)), _full((t, n))],
                  out_specs=pl.BlockSpec((tk, n), lambda i: (i, 0)),
                  scratch_shapes=[pltpu.VMEM((t, n), BF16)], semantics=("arbitrary",))(a, dy)


def norm_backward(dpre, w, x, g, rstd, dx_out, *, name, tm=512):
    t, d = x.shape
    n = dpre.shape[1]
    tm = min(tm, t)
    if w.ndim == 3:
        w_spec = pl.BlockSpec(w.shape, lambda i: (0, 0, 0))
    else:
        w_spec = pl.BlockSpec(w.shape, lambda i: (0, 0))

    def body(dp_ref, w_ref, x_ref, g_ref, r_ref, dxo_ref, dx_ref, dg_ref):
        @pl.when(pl.program_id(0) == 0)
        def _():
            dg_ref[...] = jnp.zeros_like(dg_ref)

        if w.ndim == 3:
            per = n // N_SHARDS
            dh = _dot_nt(dp_ref[:, 0:per], w_ref[0])
            for s in range(1, N_SHARDS):
                dh = dh + _dot_nt(dp_ref[:, s * per:(s + 1) * per], w_ref[s])
        else:
            dh = _dot_nt(dp_ref[...], w_ref[...])
        r = r_ref[...]
        xn = x_ref[...] * r
        dg_ref[...] += jnp.sum(dh * xn, axis=0, keepdims=True)
        dxn = dh * g_ref[...]
        dx = r * (dxn - xn * jnp.mean(dxn * xn, axis=-1, keepdims=True))
        dx_ref[...] = dxo_ref[...] + dx

    row = lambda i: (i, 0)
    fixed = lambda i: (0, 0)
    return _pcall(
        body, name=name, out_shape=[_sds((t, d), F32), _sds((1, d), F32)], grid=(t // tm,),
        in_specs=[pl.BlockSpec((tm, n), row), w_spec, pl.BlockSpec((tm, d), row),
                  pl.BlockSpec((1, d), fixed), pl.BlockSpec((tm, 1), row), pl.BlockSpec((tm, d), row)],
        out_specs=[pl.BlockSpec((tm, d), row), pl.BlockSpec((1, d), fixed)],
        semantics=("arbitrary",))(dpre, w, x, g, rstd, dx_out)


def ple_backward(dx, gate, pp, *, name, tm=512):
    t, d = dx.shape
    tm = min(tm, t)

    def body(dx_ref, gate_ref, pp_ref, dg_ref, dp_ref):
        dxv = dx_ref[...]
        gate = gate_ref[...]
        dg_ref[...] = (dxv * pp_ref[...].astype(F32) * (gate * (1.0 - gate))).astype(BF16)
        dp_ref[...] = (dxv * gate).astype(BF16)

    spec = pl.BlockSpec((tm, d), lambda i: (i, 0))
    return _pcall(body, name=name, out_shape=[_sds((t, d), BF16)] * 2, grid=(t // tm,), in_specs=[spec] * 3,
                  out_specs=[spec] * 2, semantics=("parallel",))(dx, gate, pp)


def sgu_backward(dy, pre, g_v, w_s, b_full, *, name):
    t = pre.shape[0]
    n_chunks = t // CHUNK

    def body(dy_ref, pre_ref, gv_ref, ws_ref, b_ref, dpre_ref, dws_ref, db_ref, dgv_ref, dvn_s, dbf_s):
        step = pl.program_id(0)

        @pl.when(step == 0)
        def _():
            dws_ref[...] = jnp.zeros_like(dws_ref)
            dgv_ref[...] = jnp.zeros_like(dgv_ref)
            dbf_s[...] = jnp.zeros_like(dbf_s)

        pre_u, pre_v, u, r, vhat, vn, wm, tril = _sgu_common(pre_ref, gv_ref, ws_ref)
        dyv = dy_ref[...]
        for g in range(N_GROUPS):
            cols = slice(g * LANES, (g + 1) * LANES)
            mix = _dot(wm[g], vn[:, cols]) + b_ref[:, cols]
            dmix = dyv[:, cols] * u[:, cols]
            dmix_b = dmix.astype(BF16)
            du = dyv[:, cols] * mix
            dpre_ref[:, cols] = (du * _gelu_grad(pre_u[:, cols])).astype(BF16)
            dws_ref[g] += jnp.where(tril, _dot_nt(dmix_b, vn[:, cols]), 0.0)
            dbf_s[:, cols] += dmix
            dvn_s[:, cols] = _dot_tn(wm[g], dmix_b)
        dvn = dvn_s[...]
        dgv_ref[...] += jnp.sum(dvn * vhat, axis=0, keepdims=True)
        dxn = dvn * gv_ref[...]
        dv = r * (dxn - vhat * jnp.mean(dxn * vhat, axis=-1, keepdims=True))
        dpre_ref[:, D_MODEL:] = (dv * _gelu_grad(pre_v)).astype(BF16)

        @pl.when(step == n_chunks - 1)
        def _():
            lane = lax.broadcasted_iota(jnp.int32, (CHUNK, LANES), 1)
            acc = jnp.zeros((CHUNK, LANES), F32)
            for g in range(N_GROUPS):
                s = jnp.sum(dbf_s[:, g * LANES:(g + 1) * LANES], axis=-1, keepdims=True)
                acc = jnp.where(lane == g, s, acc)
            db_ref[...] = acc

    fixed2 = lambda i: (0, 0)
    return _pcall(
        body, name=name,
        out_shape=[_sds((t, 2 * D_MODEL), BF16), _sds((N_GROUPS, CHUNK, CHUNK), F32), _sds((CHUNK, LANES), F32),
                   _sds((1, D_MODEL), F32)],
        grid=(n_chunks,),
        in_specs=[pl.BlockSpec((CHUNK, D_MODEL), lambda i: (i, 0)), pl.BlockSpec((CHUNK, 2 * D_MODEL), lambda i: (i, 0)),
                  pl.BlockSpec((1, D_MODEL), fixed2), pl.BlockSpec((N_GROUPS, CHUNK, CHUNK), lambda i: (0, 0, 0)),
                  pl.BlockSpec((CHUNK, D_MODEL), fixed2)],
        out_specs=[pl.BlockSpec((CHUNK, 2 * D_MODEL), lambda i: (i, 0)),
                   pl.BlockSpec((N_GROUPS, CHUNK, CHUNK), lambda i: (0, 0, 0)), pl.BlockSpec((CHUNK, LANES), fixed2),
                   pl.BlockSpec((1, D_MODEL), fixed2)],
        scratch_shapes=[pltpu.VMEM((CHUNK, D_MODEL), F32), pltpu.VMEM((CHUNK, D_MODEL), F32)],
        semantics=("arbitrary",))(dy, pre, g_v, w_s, b_full)


def head_norm_backward(dy, pre, g128, *, name, col_block=0, scale=1.0, passthrough=None, tm=512):
    t = dy.shape[0]
    tm = min(tm, t)
    width = 2 * D_MODEL if passthrough is not None else D_MODEL

    def body(*refs):
        if passthrough is not None:
            dy_ref, x_ref, g_ref, dv_ref, o_ref, dg_ref = refs
            o_ref[:, D_MODEL:] = dv_ref[...].astype(BF16)
        else:
            dy_ref, x_ref, g_ref, o_ref, dg_ref = refs

        @pl.when(pl.program_id(0) == 0)
        def _():
            dg_ref[...] = jnp.zeros_like(dg_ref)

        g = g_ref[...]
        dg = jnp.zeros((1, LANES), F32)
        for b in range(D_MODEL // LANES):
            cols = slice(b * LANES, (b + 1) * LANES)
            xv = x_ref[:, cols]
            r = _head_rstd(xv)
            xn = xv * r
            dyv = dy_ref[:, cols] * scale
            dg = dg + jnp.sum(dyv * xn, axis=0, keepdims=True)
            dxn = dyv * g
            o_ref[:, cols] = (r * (dxn - xn * _head_mean(dxn * xn))).astype(BF16)
        dg_ref[...] += dg

    row = lambda i: (i, 0)
    in_specs = [pl.BlockSpec((tm, D_MODEL), row), pl.BlockSpec((tm, D_MODEL), lambda i: (i, col_block)),
                pl.BlockSpec((1, LANES), lambda i: (0, 0))]
    args = [dy, pre, g128]
    if passthrough is not None:
        in_specs.append(pl.BlockSpec((tm, D_MODEL), row))
        args.append(passthrough)
    return _pcall(body, name=name, out_shape=[_sds((t, width), BF16), _sds((1, LANES), F32)], grid=(t // tm,),
                  in_specs=in_specs,
                  out_specs=[pl.BlockSpec((tm, width), row), pl.BlockSpec((1, LANES), lambda i: (0, 0))],
                  semantics=("arbitrary",))(*args)


def stick_breaking_backward(q, k, v, do, *, name):
    t = q.shape[0]
    blk = min(ATT_BLOCK, t)
    nq = t // blk

    def body(q_ref, k_ref, v_ref, do_ref, dq_ref, dk_ref, dv_ref, s_buf, sg_buf):
        i = pl.program_id(1)

        @pl.when(i == 0)
        def _():
            dk_ref[...] = jnp.zeros_like(dk_ref)
            dv_ref[...] = jnp.zeros_like(dv_ref)

        low = lax.broadcasted_iota(jnp.int32, (blk, LANES), 1) < HEAD_DIM
        suffix = _suffix_matrix(blk)
        prefix = _prefix_matrix(blk)
        causal = _stacked_causal(blk)
        qs = _stack_heads(q_ref[...], low)
        dos = _stack_heads(do_ref[...], low)

        def log_weights(j, carry, masked):
            rows = pl.ds(pl.multiple_of(j * blk, blk), blk)
            z = _dot_nt(qs, k_ref[rows, :])
            ls = _log_sigmoid(z)
            lg = ls - z
            if masked:
                lg = jnp.where(causal, lg, 0.0)
            s_buf[j] = ls + _block_cumsum(lg, suffix) + carry
            sg_buf[j] = jnp.exp(ls)
            return carry + jnp.sum(lg, axis=-1, keepdims=True)

        carry = log_weights(i, jnp.zeros((2 * blk, 1), F32), True)
        carry = lax.fori_loop(0, i // 2, lambda n, c: log_weights(i - 2 - 2 * n, log_weights(i - 1 - 2 * n, c, False),
                                                                  False), carry)
        lax.fori_loop(0, i % 2, lambda n, c: log_weights(0, c, False), carry)

        def grads(j, pcarry, dq_acc, masked):
            rows = pl.ds(pl.multiple_of(j * blk, blk), blk)
            a = jnp.exp(s_buf[j])
            if masked:
                a = jnp.where(causal, a, 0.0)
            sg = sg_buf[j]
            ds = _dot_nt(dos, v_ref[rows, :]) * a
            before = _block_cumsum(ds, prefix) + pcarry
            if masked:
                before = jnp.where(causal, before, 0.0)
            dz = (ds - sg * (ds + before)).astype(BF16)
            dq_acc = dq_acc + _dot(dz, k_ref[rows, :])
            dk_ref[rows, :] += _dot_tn(dz, qs)
            dv_ref[rows, :] += _dot_tn(a.astype(BF16), dos)
            return pcarry + jnp.sum(ds, axis=-1, keepdims=True), dq_acc

        def two_blocks(n, st):
            st = grads(2 * n, st[0], st[1], False)
            return grads(2 * n + 1, st[0], st[1], False)

        state = lax.fori_loop(0, i // 2, two_blocks,
                              (jnp.zeros((2 * blk, 1), F32), jnp.zeros((2 * blk, LANES), F32)))
        state = lax.fori_loop(0, i % 2, lambda n, st: grads(i - 1, st[0], st[1], False), state)
        _, dq_acc = grads(i, state[0], state[1], True)
        dq_ref[...] = jnp.where(low, dq_acc[:blk], dq_acc[blk:])

    full = pl.BlockSpec((t, LANES), lambda p, i: (0, p))
    qblk = pl.BlockSpec((blk, LANES), lambda p, i: (i, p))
    return _pcall(
        body, name=name, out_shape=[_sds((t, D_MODEL), F32)] * 3, grid=(D_MODEL // LANES, nq),
        in_specs=[qblk, full, full, qblk], out_specs=[qblk, full, full],
        scratch_shapes=[pltpu.VMEM((nq, 2 * blk, blk), F32), pltpu.VMEM((nq, 2 * blk, blk), F32)],
        semantics=("parallel", "arbitrary"))(q, k, v, do)


def _mlp_forward(x, g, w_up, w_down, tag):
    h, r, a, a2 = norm_matmul(x, g, w_up, name=f"mlp_up_{tag}", epilogue="relu2")
    return matmul_residual(a2, w_down, x, name=f"mlp_down_{tag}"), (x, h, r, a, a2)


def _mlp_backward(dx, saved, g, w_up, w_down, tag):
    x, h, r, a, a2 = saved
    d_w_down = matmul_tn(a2, dx, name=f"d_w_down_{tag}", col_shards=False)
    dpre = matmul_nt(dx, w_down, name=f"d_mlp_act_{tag}", mul=a, out_dtype=BF16)
    d_w_up = matmul_tn(h, dpre, name=f"d_w_up_{tag}", col_shards=True)
    dx, d_g = norm_backward(dpre, w_up, x, g, r, dx, name=f"d_mlp_norm_{tag}")
    return dx, d_w_up, d_w_down, d_g


def _ple_backward(dx, saved, p, g, w_gate, tag):
    x, h, r, gate, pp = saved
    dgate, dproj = ple_backward(dx, gate, pp, name=f"d_ple_{tag}")
    d_w_proj = matmul_tn(p, dproj, name=f"d_w_ple_proj_{tag}", col_shards=True)
    d_w_gate = matmul_tn(h, dgate, name=f"d_w_ple_gate_{tag}", col_shards=False)
    dx, d_g = norm_backward(dgate, w_gate, x, g, r, dx, name=f"d_ple_norm_{tag}")
    return dx, d_w_gate, d_w_proj, d_g


def local_step(x, p, target, w, late=None):
    row = lambda v: v.reshape(1, -1)
    g128 = lambda v: jnp.tile(v.reshape(1, HEAD_DIM), (1, 2))
    scale = HEAD_DIM ** -0.5
    b_full = jnp.repeat(jnp.transpose(w["b_spatial"][0]), LANES, axis=1)
    w_s = w["w_spatial"][0]

    x0 = x
    h_a, r_a, pre_a = norm_matmul(x0, row(w["ln_mix_a"][0]), w["w_in_a"][0], name="sgu_in")
    y_a = sgu_forward(pre_a, row(w["g_v_a"][0]), w_s, b_full, name="sgu_mix")
    x1 = matmul_residual(y_a, w["w_out_a"][0], x0, name="sgu_out")
    x2, mlp0 = _mlp_forward(x1, row(w["ln_mlp"][0]), w["w_up"][0], w["w_down"][0], 0)
    ple0 = ple_forward(x2, row(w["ln_ple"][0]), w["w_ple_gate"][0], p[0], w["w_ple_proj"][0], name="ple_0")
    x3 = ple0[4]
    if late is not None:
        late.after_first_layer(x3)
    h_kv, r_kv, kv_pre = norm_matmul(x3, row(w["ln_kv"]), w["w_kv"], name="kv_proj")
    k_n, v_b = head_norm(kv_pre, g128(w["g_k"]), name="k_norm", passthrough=True)
    if late is not None:
        w = {**w, **late.second_layer_weights(k_n)}
    h_q, r_q, q_pre = norm_matmul(x3, row(w["ln_mix_b"][0]), w["w_q"][0], name="q_proj")
    q_n = head_norm(q_pre, g128(w["g_q"][0]), name="q_norm", scale=scale)
    o = stick_breaking_forward(q_n, k_n, v_b, name="sb_fwd")
    x4 = matmul_residual(o, w["w_out_b"][0], x3, name="sb_out")
    x5, mlp1 = _mlp_forward(x4, row(w["ln_mlp"][1]), w["w_up"][1], w["w_down"][1], 1)
    ple1 = ple_forward(x5, row(w["ln_ple"][1]), w["w_ple_gate"][1], p[1], w["w_ple_proj"][1], name="ple_1")
    x6 = ple1[4]
    loss_blk, dx = loss_forward(x6, target, name="loss")

    g = {}
    dx, dwg1, dwp1, dlnp1 = _ple_backward(dx, (x5,) + tuple(ple1[:4]), p[1], row(w["ln_ple"][1]), w["w_ple_gate"][1], 1)
    dx, dwu1, dwd1, dlnm1 = _mlp_backward(dx, mlp1, row(w["ln_mlp"][1]), w["w_up"][1], w["w_down"][1], 1)
    g["w_out_b"] = matmul_tn(o, dx, name="d_w_out_b", col_shards=False)
    do = matmul_nt(dx, w["w_out_b"][0], name="d_sb_out", out_dtype=BF16)
    dq_n, dk_n, dv = stick_breaking_backward(q_n, k_n, v_b, do, name="sb_bwd")
    dq_pre, dgq = head_norm_backward(dq_n, q_pre, g128(w["g_q"][0]), name="d_q_norm", scale=scale)
    dkv_pre, dgk = head_norm_backward(dk_n, kv_pre, g128(w["g_k"]), name="d_k_norm", passthrough=dv)
    g["w_q"] = matmul_tn(h_q, dq_pre, name="d_w_q", col_shards=False)
    g["w_kv"] = matmul_tn(h_kv, dkv_pre, name="d_w_kv", col_shards=True)
    dx, g["ln_mix_b"] = norm_backward(dq_pre, w["w_q"][0], x3, row(w["ln_mix_b"][0]), r_q, dx, name="d_q_in")
    dx, g["ln_kv"] = norm_backward(dkv_pre, w["w_kv"], x3, row(w["ln_kv"]), r_kv, dx, name="d_kv_in")
    g["g_q"] = dgq[:, :HEAD_DIM] + dgq[:, HEAD_DIM:]
    g["g_k"] = (dgk[:, :HEAD_DIM] + dgk[:, HEAD_DIM:]).reshape(HEAD_DIM)
    g["ln_kv"] = g["ln_kv"].reshape(D_MODEL)
    ln_ple0, ln_mlp0 = row(w["ln_ple"][0]), row(w["ln_mlp"][0])
    if late is not None:
        ln_ple0 = ln_ple0 + late.second_layer_grads(
            {("w_kv", 0): g["w_kv"], ("w_q", 0): g["w_q"], ("w_out_b", 0): g["w_out_b"], ("w_up", 1): dwu1,
             ("w_down", 1): dwd1, ("w_ple_gate", 1): dwg1, ("w_ple_proj", 1): dwp1}, dx)[0, 0]
    dx, dwg0, dwp0, dlnp0 = _ple_backward(dx, (x2,) + tuple(ple0[:4]), p[0], ln_ple0, w["w_ple_gate"][0], 0)
    if late is not None:
        ln_mlp0 = ln_mlp0 + late.first_ple_backward_done(dx)[0, 0]
    dx, dwu0, dwd0, dlnm0 = _mlp_backward(dx, mlp0, ln_mlp0, w["w_up"][0], w["w_down"][0], 0)
    g["w_out_a"] = matmul_tn(y_a, dx, name="d_w_out_a", col_shards=False)
    dy_a = matmul_nt(dx, w["w_out_a"][0], name="d_sgu_out")
    dpre_a, dws, db, g["g_v_a"] = sgu_backward(dy_a, pre_a, row(w["g_v_a"][0]), w_s, b_full, name="d_sgu_mix")
    g["w_in_a"] = matmul_tn(h_a, dpre_a, name="d_w_in_a", col_shards=True)
    dx, g["ln_mix_a"] = norm_backward(dpre_a, w["w_in_a"][0], x0, row(w["ln_mix_a"][0]), r_a, dx, name="d_sgu_in")
    g["w_spatial"] = dws[None]
    g["b_spatial"] = jnp.transpose(db[:, :N_GROUPS])[None]
    g["w_up"] = (dwu0, dwu1)
    g["w_down"] = (dwd0, dwd1)
    g["w_ple_gate"] = (dwg0, dwg1)
    g["w_ple_proj"] = (dwp0, dwp1)
    g["ln_mlp"] = jnp.concatenate([dlnm0, dlnm1], axis=0)
    g["ln_ple"] = jnp.concatenate([dlnp0, dlnp1], axis=0)
    return loss_blk, dx, g


ANY = pl.BlockSpec(memory_space=pl.ANY)


def _place():
    x, y, c = lax.axis_index("x"), lax.axis_index("y"), lax.axis_index("c")
    others = [(1 - x, y), (x, 1 - y), (1 - x, 1 - y)]
    return x, y, c, 2 * x + y, others


def cast_into_slot(w3, layer, slot, *, name, tm=256):
    _, r, c = w3.shape
    tm = min(tm, r)

    def body(slot_ref, w_ref, o_ref):
        o_ref[...] = w_ref[...].astype(BF16)

    return _pcall(body, name=name, out_shape=_sds((N_SHARDS, r, c), BF16), grid=(r // tm,), num_prefetch=1,
                  in_specs=[pl.BlockSpec((None, tm, c), lambda i, s: (layer, i, 0))],
                  out_specs=pl.BlockSpec((None, tm, c), lambda i, s: (s[0], i, 0)),
                  semantics=("parallel",))(slot, w3)


def gather_shards(mats, vecs, *, name):
    nm, nv = len(mats), len(vecs)
    halves = [m.reshape(N_SHARDS, 2, m.shape[1] // 2, m.shape[2]) for m in mats]

    def body(*refs):
        vsrc = refs[nm:nm + nv]
        out, vout = refs[nm + nv:2 * nm + nv], refs[2 * nm + nv:2 * (nm + nv)]
        send, recv, vsend, vrecv, loc = refs[2 * (nm + nv):]
        x, y, c, s_me, others = _place()
        sib = (x, y, 1 - c)

        def ici(l, k):
            ox, oy = others[k]
            return pltpu.make_async_remote_copy(out[l].at[s_me, c], out[l].at[s_me, c], send.at[l, k], recv.at[l, k],
                                                device_id=(ox, oy, c), device_id_type=MESH)

        def landed(l, k, half):
            ox, oy = others[k]
            return out[l].at[2 * ox + oy, half]

        def passed_on(l, k):
            return pltpu.make_async_remote_copy(landed(l, k, c), landed(l, k, c), send.at[l, 3 + k], recv.at[l, 3 + k],
                                                device_id=sib, device_id_type=MESH)

        def vec(l, k):
            ox, oy = others[k]
            return pltpu.make_async_remote_copy(vsrc[l], vout[l].at[s_me], vsend.at[l, k], vrecv.at[l, k],
                                                device_id=(ox, oy, c), device_id_type=MESH)

        for l in range(nm):
            for k in range(3):
                ici(l, k).start()
        for l in range(nv):
            for k in range(3):
                vec(l, k).start()
        for l in range(nv):
            own = pltpu.make_async_copy(vsrc[l], vout[l].at[s_me], loc)
            own.start()
            own.wait()
        for l in range(nm):
            for k in range(3):
                pltpu.make_async_remote_copy(landed(l, k, c), landed(l, k, c), send.at[l, k], recv.at[l, k],
                                             device_id=sib, device_id_type=MESH).wait_recv()
                passed_on(l, k).start()
        for l in range(nm):
            for k in range(3):
                pltpu.make_async_remote_copy(landed(l, k, 1 - c), landed(l, k, 1 - c), send.at[l, 3 + k],
                                             recv.at[l, 3 + k], device_id=sib, device_id_type=MESH).wait_recv()
        for l in range(nv):
            for k in range(3):
                ox, oy = others[k]
                pltpu.make_async_remote_copy(vsrc[l], vout[l].at[2 * ox + oy], vsend.at[l, k], vrecv.at[l, k],
                                             device_id=sib, device_id_type=MESH).wait_recv()
        for l in range(nm):
            for k in range(3):
                ici(l, k).wait_send()
                passed_on(l, k).wait_send()
        for l in range(nv):
            for k in range(3):
                vec(l, k).wait_send()

    out_shape = [_sds(h.shape, BF16) for h in halves] + [_sds((N_SHARDS,) + v.shape, F32) for v in vecs]
    res = _pcall(body, name=name, out_shape=out_shape, in_specs=[ANY] * (nm + nv), out_specs=[ANY] * (nm + nv),
                 scratch_shapes=[pltpu.SemaphoreType.DMA((nm, 6)), pltpu.SemaphoreType.DMA((nm, 6)),
                                 pltpu.SemaphoreType.DMA((max(nv, 1), 3)), pltpu.SemaphoreType.DMA((max(nv, 1), 3)),
                                 pltpu.SemaphoreType.DMA(())],
                 aliases={l: l for l in range(nm)}, side_effects=True)(*halves, *vecs)
    return [r.reshape(m.shape) for r, m in zip(res[:nm], mats)], list(res[nm:])


HBM = pl.BlockSpec(memory_space=pltpu.HBM)
SEM = pl.BlockSpec(memory_space=pltpu.SEMAPHORE)
DATAFLOW = pltpu.SideEffectType.DATAFLOW_SIDE_EFFECTING


def _split_call(body, *, name, out_shape, in_specs, out_specs, aliases):
    return pl.pallas_call(body, name=name, out_shape=out_shape, in_specs=in_specs, out_specs=out_specs,
                          input_output_aliases=aliases,
                          compiler_params=pltpu.CompilerParams(has_side_effects=DATAFLOW))


def _token_shape():
    return jax.ShapeDtypeStruct((8, LANES), F32)


def gather_start(mats, after, *, name):
    n = len(mats)
    halves = [pltpu.with_memory_space_constraint(m.reshape(N_SHARDS, 2, m.shape[1] // 2, m.shape[2]), pltpu.HBM)
              for m in mats]

    def body(*refs):
        send, recv = refs[n + 1], refs[n + 2]
        out, token = refs[n + 3:2 * n + 3], refs[2 * n + 3]
        x, y, c, s_me, others = _place()
        for l in range(n):
            for k in range(3):
                ox, oy = others[k]
                pltpu.make_async_remote_copy(out[l].at[s_me, c], out[l].at[s_me, c], send.at[3 * l + k],
                                             recv.at[3 * l + k], device_id=(ox, oy, c), device_id_type=MESH).start()
        token[...] = jnp.zeros_like(token)

    res = _split_call(
        body, name=name,
        out_shape=(pltpu.SemaphoreType.DMA((3 * n,)), pltpu.SemaphoreType.DMA((3 * n,)),
                   *[pltpu.HBM(h.shape, BF16) for h in halves], _token_shape()),
        in_specs=[HBM] * n + [ANY], out_specs=(SEM, SEM, *[HBM] * n, pl.BlockSpec(memory_space=pltpu.VMEM)),
        aliases={l: 2 + l for l in range(n)})(*halves, after)
    return res[0], res[1], list(res[2:2 + n]), res[2 + n]


def gather_pass_on(bufs, send_a, recv_a, after, *, name):
    n = len(bufs)

    def body(*refs):
        send_a, recv_a = refs[n], refs[n + 1]
        out = refs[n + 3:2 * n + 3]
        send_b, recv_b, token = refs[2 * n + 3:]
        x, y, c, s_me, others = _place()
        for l in range(n):
            for k in range(3):
                ox, oy = others[k]
                landed, i = out[l].at[2 * ox + oy, c], 3 * l + k
                pltpu.make_async_remote_copy(landed, landed, send_a.at[i], recv_a.at[i],
                                             device_id=(x, y, 1 - c), device_id_type=MESH).wait_recv()
                pltpu.make_async_remote_copy(landed, landed, send_b.at[i], recv_b.at[i],
                                             device_id=(x, y, 1 - c), device_id_type=MESH).start()
        for l in range(n):
            for k in range(3):
                mine, i = out[l].at[s_me, c], 3 * l + k
                pltpu.make_async_remote_copy(mine, mine, send_a.at[i], recv_a.at[i],
                                             device_id=(x, y, 1 - c), device_id_type=MESH).wait_send()
        token[...] = jnp.zeros_like(token)

    res = _split_call(
        body, name=name,
        out_shape=(*[pltpu.HBM(b.shape, BF16) for b in bufs], pltpu.SemaphoreType.DMA((3 * n,)),
                   pltpu.SemaphoreType.DMA((3 * n,)), _token_shape()),
        in_specs=[HBM] * n + [SEM, SEM, ANY],
        out_specs=(*[HBM] * n, SEM, SEM, pl.BlockSpec(memory_space=pltpu.VMEM)),
        aliases={l: l for l in range(n)})(*bufs, send_a, recv_a, after)
    return list(res[:n]), res[n], res[n + 1], res[n + 2]


def gather_finish(bufs, send_b, recv_b, after, shapes, *, name):
    n = len(bufs)

    def body(*refs):
        send_b, recv_b = refs[n], refs[n + 1]
        out = refs[n + 3:]
        x, y, c, _, others = _place()
        for l in range(n):
            for k in range(3):
                ox, oy = others[k]
                theirs, mine, i = out[l].at[2 * ox + oy, 1 - c], out[l].at[2 * ox + oy, c], 3 * l + k
                pltpu.make_async_remote_copy(theirs, theirs, send_b.at[i], recv_b.at[i],
                                             device_id=(x, y, 1 - c), device_id_type=MESH).wait_recv()
                pltpu.make_async_remote_copy(mine, mine, send_b.at[i], recv_b.at[i],
                                             device_id=(x, y, 1 - c), device_id_type=MESH).wait_send()

    res = _split_call(
        body, name=name, out_shape=tuple(pltpu.HBM(b.shape, BF16) for b in bufs),
        in_specs=[HBM] * n + [SEM, SEM, ANY], out_specs=tuple([HBM] * n),
        aliases={l: l for l in range(n)})(*bufs, send_b, recv_b, after)
    return [r.reshape(s) for r, s in zip(res, shapes)]


def exchange_start(srcs, dst_shapes, dst_dtype, plan, count, after, *, name):
    n, m = len(srcs), len(dst_shapes)
    srcs = [pltpu.with_memory_space_constraint(s, pltpu.HBM) for s in srcs]
    lands = [pltpu.with_memory_space_constraint(lax.empty(s, dst_dtype), pltpu.HBM) for s in dst_shapes]

    def body(*refs):
        send, recv = refs[n + m + 1], refs[n + m + 2]
        src, dst, token = refs[n + m + 3:2 * n + m + 3], refs[2 * n + m + 3:2 * (n + m) + 3], refs[2 * (n + m) + 3]
        for i, (s, d, dev) in enumerate(plan(_place(), src, dst)):
            pltpu.make_async_remote_copy(s, d, send.at[i], recv.at[i], device_id=dev, device_id_type=MESH).start()
        token[...] = jnp.zeros_like(token)

    res = _split_call(
        body, name=name,
        out_shape=(pltpu.SemaphoreType.DMA((count,)), pltpu.SemaphoreType.DMA((count,)),
                   *[pltpu.HBM(s.shape, s.dtype) for s in srcs], *[pltpu.HBM(s, dst_dtype) for s in dst_shapes],
                   _token_shape()),
        in_specs=[HBM] * (n + m) + [ANY],
        out_specs=(SEM, SEM, *[HBM] * (n + m), pl.BlockSpec(memory_space=pltpu.VMEM)),
        aliases={i: 2 + i for i in range(n + m)})(*srcs, *lands, after)
    return (list(res[2:2 + n]), list(res[2 + n:2 + n + m]), res[0], res[1], plan), res[2 + n + m]


def exchange_finish(state, after, *, name):
    srcs, lands, send, recv, plan = state
    n, m = len(srcs), len(lands)

    def body(*refs):
        send, recv = refs[n + m], refs[n + m + 1]
        src, dst = refs[n + m + 3:2 * n + m + 3], refs[2 * n + m + 3:]
        for i, (s, d, dev) in enumerate(plan(_place(), src, dst)):
            pltpu.make_async_remote_copy(s, d, send.at[i], recv.at[i], device_id=dev, device_id_type=MESH).wait()

    res = _split_call(
        body, name=name,
        out_shape=tuple(pltpu.HBM(a.shape, a.dtype) for a in srcs + lands),
        in_specs=[HBM] * (n + m) + [SEM, SEM, ANY], out_specs=tuple([HBM] * (n + m)),
        aliases={i: i for i in range(n + m)})(*srcs, *lands, send, recv, after)
    return list(res[:n]), list(res[n:])


def pair_plan(place, src, dst):
    x, y, c, _, _ = place
    return [(s.at[:, 1 - c], d, (x, y, 1 - c)) for s, d in zip(src, dst)]


def chip_plan(place, src, dst):
    x, y, c, _, others = place
    return [(s.at[2 * ox + oy], d.at[k], (ox, oy, c)) for s, d in zip(src, dst) for k, (ox, oy) in enumerate(others)]


def pair_exchange(grads, *, name):
    n = len(grads)

    def body(*refs):
        src, got = refs[:n], refs[n:2 * n]
        send, recv = refs[2 * n:]
        x, y, c, _, _ = _place()

        def swap(l):
            return pltpu.make_async_remote_copy(src[l].at[:, 1 - c], got[l], send.at[l], recv.at[l],
                                                device_id=(x, y, 1 - c), device_id_type=MESH)

        for l in range(n):
            swap(l).start()
        for l in range(n):
            swap(l).wait()

    res = _pcall(body, name=name, out_shape=[_sds((N_SHARDS,) + g.shape[2:], F32) for g in grads],
                 in_specs=[ANY] * n, out_specs=[ANY] * n,
                 scratch_shapes=[pltpu.SemaphoreType.DMA((n,)), pltpu.SemaphoreType.DMA((n,))],
                 side_effects=True)(*grads)
    return list(res)


def add_to_wire(mine, theirs, core, *, name, tm=256):
    s, _, r, c = mine.shape
    tm = min(tm, r)

    def body(core_ref, a_ref, b_ref, o_ref):
        o_ref[...] = (a_ref[...] + b_ref[...]).astype(BF16)

    spec = pl.BlockSpec((None, tm, c), lambda i, j, cr: (i, j, 0))
    return _pcall(body, name=name, out_shape=_sds((s, r, c), BF16), grid=(s, r // tm), num_prefetch=1,
                  in_specs=[pl.BlockSpec((None, None, tm, c), lambda i, j, cr: (i, cr[0], j, 0)), spec],
                  out_specs=spec, semantics=("parallel", "parallel"))(core, mine, theirs)


def sum_chips(wire, landed, place, dest, layer, n_layers, *, name, tm=256):
    _, r, c = wire.shape
    tm = min(tm, r)

    def body(place_ref, w_ref, l_ref, *rest):
        o_ref = rest[-1]
        o_ref[...] = ((w_ref[...].astype(F32) + l_ref[0].astype(F32)) + l_ref[1].astype(F32)) + l_ref[2].astype(F32)

    in_specs = [pl.BlockSpec((None, tm, c), lambda i, pr: (pr[0], i, 0)),
                pl.BlockSpec((3, tm, c), lambda i, pr: (0, i, 0))]
    args = [place, wire, landed]
    aliases = None
    if dest is not None:
        in_specs.append(ANY)
        args.append(dest)
        aliases = {3: 0}
    return _pcall(body, name=name, out_shape=_sds((n_layers, 2, r, c), F32), grid=(r // tm,), num_prefetch=1,
                  in_specs=in_specs,
                  out_specs=pl.BlockSpec((None, None, tm, c), lambda i, pr: (layer, pr[1], i, 0)),
                  aliases=aliases, semantics=("parallel",))(*args)


def pair_share(bufs, slots, *, name):
    n = len(bufs)

    def body(*refs):
        out = refs[n:2 * n]
        send, recv = refs[2 * n:]
        x, y, c, _, _ = _place()

        def share(i, half):
            o, l = slots[i]
            return pltpu.make_async_remote_copy(out[o].at[l, half], out[o].at[l, half], send.at[i], recv.at[i],
                                                device_id=(x, y, 1 - c), device_id_type=MESH)

        for i in range(len(slots)):
            share(i, c).start()
        for i in range(len(slots)):
            share(i, 1 - c).wait_recv()
            share(i, c).wait_send()

    res = _pcall(body, name=name, out_shape=[_sds(b.shape, F32) for b in bufs], in_specs=[ANY] * n,
                 out_specs=[ANY] * n,
                 scratch_shapes=[pltpu.SemaphoreType.DMA((len(slots),)), pltpu.SemaphoreType.DMA((len(slots),))],
                 aliases={o: o for o in range(n)}, side_effects=True)(*bufs)
    return list(res)


def all_reduce_small(packed, *, name):
    n_dev, r, c = packed.shape

    def body(in_ref, out_ref, land, send, recv):
        x, y, cc, _, _ = _place()
        me = 4 * x + 2 * y + cc
        peers = [(px, py, pc) for px in range(2) for py in range(2) for pc in range(2)]

        def scatter(d):
            return pltpu.make_async_remote_copy(in_ref.at[d], land.at[me], send.at[0, d], recv.at[0, me],
                                                device_id=peers[d], device_id_type=MESH)

        def gather(d):
            return pltpu.make_async_remote_copy(out_ref.at[me], out_ref.at[me], send.at[1, d], recv.at[1, me],
                                                device_id=peers[d], device_id_type=MESH)

        for d in range(n_dev):
            @pl.when(d != me)
            def _():
                scatter(d).start()
        land[me] = in_ref[me]
        for d in range(n_dev):
            @pl.when(d != me)
            def _():
                pltpu.make_async_remote_copy(in_ref.at[d], land.at[d], send.at[0, d], recv.at[0, d],
                                             device_id=peers[d], device_id_type=MESH).wait_recv()
        total = land[0]
        for d in range(1, n_dev):
            total = total + land[d]
        out_ref[me] = total
        for d in range(n_dev):
            @pl.when(d != me)
            def _():
                gather(d).start()
        for d in range(n_dev):
            @pl.when(d != me)
            def _():
                pltpu.make_async_remote_copy(out_ref.at[d], out_ref.at[d], send.at[1, d], recv.at[1, d],
                                             device_id=peers[d], device_id_type=MESH).wait_recv()
        for d in range(n_dev):
            @pl.when(d != me)
            def _():
                scatter(d).wait_send()
                gather(d).wait_send()

    vm = pl.BlockSpec(memory_space=pltpu.VMEM)
    return _pcall(body, name=name, out_shape=_sds(packed.shape, F32), in_specs=[vm], out_specs=vm,
                  scratch_shapes=[pltpu.VMEM(packed.shape, F32), pltpu.SemaphoreType.DMA((2, n_dev)),
                                  pltpu.SemaphoreType.DMA((2, n_dev))],
                  side_effects=True)(packed)


def adamw(w, g, m, v, *, name, part=None, dest=None, tm=256):
    shape = w.shape
    cols = shape[-1]
    rows = 1
    for s in shape[:-1]:
        rows *= s
    first, count = 0, rows
    if part is not None:
        count = rows // part[1]
        first = part[0] * count
    tm = min(tm, count)
    assert count % tm == 0
    two_d = lambda a: a.reshape(rows, cols)

    def body(w_ref, g_ref, m_ref, v_ref, *rest):
        d_ref, mo_ref, vo_ref = rest[-3:]
        gv = g_ref[...]
        m_new = ADAM_B1 * m_ref[...] + (1.0 - ADAM_B1) * gv
        v_new = ADAM_B2 * v_ref[...] + (1.0 - ADAM_B2) * (gv * gv)
        m_hat = m_new / (1.0 - ADAM_B1 ** ADAM_STEP)
        v_hat = v_new / (1.0 - ADAM_B2 ** ADAM_STEP)
        d_ref[...] = -ADAM_LR * (m_hat / (jnp.sqrt(v_hat) + ADAM_EPS) + ADAM_WD * w_ref[...])
        mo_ref[...] = m_new
        vo_ref[...] = v_new

    spec = pl.BlockSpec((tm, cols), lambda i: (first // tm + i, 0))
    args = [two_d(w), two_d(g), two_d(m), two_d(v)]
    in_specs = [spec] * 4
    aliases = None
    if dest is not None:
        args += [two_d(d) for d in dest]
        in_specs = in_specs + [ANY] * 3
        aliases = {4: 0, 5: 1, 6: 2}
    outs = _pcall(body, name=name, out_shape=[_sds((rows, cols), F32)] * 3, grid=(count // tm,), in_specs=in_specs,
                  out_specs=[spec] * 3, aliases=aliases, semantics=("parallel",))(*args)
    return [o.reshape(shape) for o in outs]


WEIGHTS = ("ln_mix_a", "w_in_a", "g_v_a", "w_spatial", "b_spatial", "w_out_a", "ln_kv", "w_kv", "g_k", "ln_mix_b",
           "w_q", "g_q", "w_out_b", "ln_mlp", "w_up", "w_down", "ln_ple", "w_ple_gate", "w_ple_proj")
MATRICES = (("w_in_a", 1, True), ("w_out_a", 1, False), ("w_kv", 0, True), ("w_q", 1, False), ("w_out_b", 1, False),
            ("w_up", 2, True), ("w_down", 2, False), ("w_ple_gate", 2, False), ("w_ple_proj", 2, True))
FIRST_LAYER = ("w_in_a", "w_out_a", "w_kv", "w_up", "w_down", "w_ple_gate", "w_ple_proj")
REPLICATED = ("w_spatial", "b_spatial", "ln_kv", "g_k", "ln_mix_b", "g_q", "ln_mlp", "ln_ple")
SHARDED_VECTORS = ("ln_mix_a", "g_v_a")
SMALL_ROWS = 18


def kernel(x, p, ln_mix_a, w_in_a, g_v_a, w_spatial, b_spatial, w_out_a, ln_kv, w_kv, g_k, ln_mix_b, w_q, g_q, w_out_b, ln_mlp, w_up, w_down, ln_ple, w_ple_gate, w_ple_proj, loss_target, m_ln_mix_a, m_w_in_a, m_g_v_a, m_w_spatial, m_b_spatial, m_w_out_a, m_ln_kv, m_w_kv, m_g_k, m_ln_mix_b, m_w_q, m_g_q, m_w_out_b, m_ln_mlp, m_w_up, m_w_down, m_ln_ple, m_w_ple_gate, m_w_ple_proj, v_ln_mix_a, v_w_in_a, v_g_v_a, v_w_spatial, v_b_spatial, v_w_out_a, v_ln_kv, v_w_kv, v_g_k, v_ln_mix_b, v_w_q, v_g_q, v_w_out_b, v_ln_mlp, v_w_up, v_w_down, v_ln_ple, v_w_ple_gate, v_w_ple_proj):
    given = dict(locals())
    weights = {n: given[n] for n in WEIGHTS}
    shard = 2 * lax.axis_index("x") + lax.axis_index("y")
    core = lax.axis_index("c")
    shard_1 = shard.astype(jnp.int32).reshape(1)
    core_1 = core.astype(jnp.int32).reshape(1)
    place = jnp.stack([shard, core]).astype(jnp.int32)

    leaves = []
    for name, layers, cols in MATRICES:
        w3 = weights[name] if layers else weights[name][None]
        for layer in range(max(layers, 1)):
            leaves.append((name, layer, cols, cast_into_slot(w3, layer, shard_1, name=f"cast_{name}_{layer}")))
    first = [lf for lf in leaves if lf[0] in FIRST_LAYER and lf[1] == 0]
    second = [lf for lf in leaves if not (lf[0] in FIRST_LAYER and lf[1] == 0)]
    got_a, vec_a = gather_shards([lf[3] for lf in first], [ln_mix_a, g_v_a], name="gather_layer0")
    send_a, recv_a, flying, token = gather_start([lf[3] for lf in second], got_a[0], name="gather_layer1_start")

    def assemble(leaf_list, arrays):
        full = {}
        for (name, layer, cols, _), arr in zip(leaf_list, arrays):
            if not cols:
                arr = arr.reshape(N_SHARDS * arr.shape[1], arr.shape[2])
            full.setdefault(name, {})[layer] = arr
        return full

    full_a = assemble(first, got_a)
    w = {name: ((full_a[name][0],) if layers else full_a[name][0]) for name, layers, _ in MATRICES if name in full_a}
    w["ln_mix_a"] = vec_a[0].reshape(1, D_MODEL) + token[0, 0]
    w["g_v_a"] = vec_a[1].reshape(1, D_MODEL)
    for name in REPLICATED:
        w[name] = weights[name]

    class Late:
        def after_first_layer(self, x_done):
            self.passed = gather_pass_on(flying, send_a, recv_a, x_done, name="gather_layer1_pass_on")

        def second_layer_weights(self, k_done):
            bufs, send_b, recv_b, _ = self.passed
            got_b = gather_finish(bufs, send_b, recv_b, k_done, [lf[3].shape for lf in second],
                                  name="gather_layer1_finish")
            full_b = assemble(second, got_b)
            out = {}
            for name, layers, _ in MATRICES:
                if name in full_b:
                    both = {**full_a.get(name, {}), **full_b[name]}
                    out[name] = tuple(both[l] for l in sorted(both))
            return out

        def second_layer_grads(self, grads_late, dx_done):
            self.keys = sorted(grads_late)
            views = [view(k, grads_late[k]) for k in self.keys]
            self.pair, token = exchange_start(views, [(N_SHARDS,) + v.shape[2:] for v in views], F32, pair_plan,
                                              len(views), dx_done, name="grad_pair_start_1")
            return token

        def first_ple_backward_done(self, dx_done):
            mine, theirs = exchange_finish(self.pair, dx_done, name="grad_pair_finish_1")
            wire = [add_to_wire(a, b, core_1, name=f"grad_pair_sum_1_{i}") for i, (a, b) in enumerate(zip(mine, theirs))]
            self.chip, token = exchange_start(wire, [(3,) + v.shape[1:] for v in wire], BF16, chip_plan, 3 * len(wire),
                                              wire[-1], name="grad_chip_start_1")
            return token

    col_sharded = {name: cols for name, _, cols in MATRICES}
    layer_count = {name: max(layers, 1) for name, layers, _ in MATRICES}

    def view(key, arr):
        rows = arr.shape[-2] if col_sharded[key[0]] else arr.shape[0] // N_SHARDS
        return arr.reshape(N_SHARDS, 2, rows // 2, arr.shape[-1])

    t = x.shape[1]
    late = Late()
    loss_blk, dx, g = local_step(x[0], p.reshape(2, t, PLE_DIM), loss_target[0], w, late)
    loss = lax.psum(loss_blk[0, 0], ("x", "y", "c"))

    keys_early = [(name, layer) for name, layers, _ in MATRICES for layer in range(max(layers, 1))
                  if (name, layer) not in late.keys]
    views = [view(k, g[k[0]][k[1]] if layer_count[k[0]] == 2 else g[k[0]]) for k in keys_early]
    theirs = pair_exchange(views, name="grad_pair_exchange_0")
    wire_0 = [add_to_wire(a, b, core_1, name=f"grad_pair_sum_0_{i}") for i, (a, b) in enumerate(zip(views, theirs))]

    grads = {}
    small = REPLICATED + SHARDED_VECTORS
    flat = jnp.concatenate([g[n].reshape(-1) for n in small])
    room = 8 * SMALL_ROWS * D_MODEL
    flat = jnp.concatenate([flat, jnp.zeros((room - flat.shape[0],), F32)])
    reduced = all_reduce_small(flat.reshape(8, SMALL_ROWS, D_MODEL), name="grad_small_all_reduce").reshape(-1)
    at = 0
    for n in small:
        size = g[n].size
        piece = reduced[at:at + size]
        at += size
        if n in SHARDED_VECTORS:
            per = D_MODEL // N_SHARDS
            grads[n] = lax.dynamic_slice(piece, (shard * per,), (per,)).reshape(weights[n].shape)
        else:
            grads[n] = piece.reshape(weights[n].shape)

    chip_0, token_0 = exchange_start(wire_0, [(3,) + v.shape[1:] for v in wire_0], BF16, chip_plan, 3 * len(wire_0),
                                     reduced, name="grad_chip_start_0")

    bufs = {}

    def sum_and_share(keys, wire, landed, tag):
        for i, (key, wv, lv) in enumerate(zip(keys, wire, landed)):
            name, layer = key
            bufs[name] = sum_chips(wv, lv, place, bufs.get(name), layer, layer_count[name],
                                   name=f"grad_chip_sum_{tag}_{i}")
        names = sorted({k[0] for k in keys})
        shared = pair_share([bufs[n] for n in names], [(names.index(k[0]), k[1]) for k in keys],
                            name=f"grad_pair_share_{tag}")
        bufs.update(zip(names, shared))

    wire_1, landed_1 = exchange_finish(late.chip, token_0, name="grad_chip_finish_1")
    sum_and_share(late.keys, wire_1, landed_1, 1)

    updates = {}

    def update(n, gn, part=None):
        wn, mn, vn = weights[n], given["m_" + n], given["v_" + n]
        if wn.ndim == 1:
            wn, gn, mn, vn = (a.reshape(1, -1) for a in (wn, gn, mn, vn))
        tag = "" if part is None else f"_{part[0]}"
        updates[n] = adamw(wn, gn.reshape(wn.shape), mn, vn, name=f"adamw_{n}{tag}", part=part, dest=updates.get(n))

    for n in small:
        update(n, grads[n])
    for name, layer in late.keys:
        update(name, bufs[name], (layer, layer_count[name]) if layer_count[name] == 2 else None)

    wire_0, landed_0 = exchange_finish(chip_0, updates[late.keys[-1][0]][0], name="grad_chip_finish_0")
    sum_and_share(keys_early, wire_0, landed_0, 0)
    for name, layer in keys_early:
        update(name, bufs[name], (layer, layer_count[name]) if layer_count[name] == 2 else None)
    for name, _, _ in MATRICES:
        grads[name] = bufs[name].reshape(weights[name].shape)
    delta = {n: updates[n][0].reshape(weights[n].shape) for n in WEIGHTS}
    new_m = {n: updates[n][1].reshape(weights[n].shape) for n in WEIGHTS}
    new_v = {n: updates[n][2].reshape(weights[n].shape) for n in WEIGHTS}
    return (loss, dx.reshape(x.shape), *[grads[n] for n in WEIGHTS], *[delta[n] for n in WEIGHTS],
            *[new_m[n] for n in WEIGHTS], *[new_v[n] for n in WEIGHTS])
```

```python
import jax
import jax.numpy as jnp
from jax import lax
from jax.experimental import pallas as pl
from jax.experimental.pallas import tpu as pltpu

F32 = jnp.float32
BF16 = jnp.bfloat16

D_MODEL = 1024
D_FF = 4096
PLE_DIM = 256
N_GROUPS = 8
CHUNK = 128
HEAD_DIM = 64
LANES = 128
ATT_BLOCK = 256
EPS = 1e-6
N_SHARDS = 4
VMEM_LIMIT = 56 * 1024 * 1024

ADAM_LR = 0.001
ADAM_B1 = 0.9
ADAM_B2 = 0.999
ADAM_EPS = 1e-08
ADAM_WD = 0.01
ADAM_STEP = 10

MESH = pl.DeviceIdType.MESH


def _pcall(body, *, name, out_shape, grid=None, in_specs=None, out_specs=None, scratch_shapes=(),
           semantics=None, aliases=None, side_effects=False, num_prefetch=0):
    params = dict(vmem_limit_bytes=VMEM_LIMIT)
    if semantics is not None:
        params["dimension_semantics"] = semantics
    if side_effects:
        params["has_side_effects"] = True
    kwargs = {}
    if aliases:
        kwargs["input_output_aliases"] = aliases
    if num_prefetch:
        spec = pltpu.PrefetchScalarGridSpec(num_scalar_prefetch=num_prefetch, grid=grid, in_specs=in_specs,
                                            out_specs=out_specs, scratch_shapes=list(scratch_shapes))
        return pl.pallas_call(body, name=name, out_shape=out_shape, grid_spec=spec,
                              compiler_params=pltpu.CompilerParams(**params), **kwargs)
    if grid is not None:
        kwargs["grid"] = grid
    if in_specs is not None:
        kwargs["in_specs"] = in_specs
    if out_specs is not None:
        kwargs["out_specs"] = out_specs
    if aliases:
        kwargs["input_output_aliases"] = aliases
    return pl.pallas_call(body, name=name, out_shape=out_shape, scratch_shapes=list(scratch_shapes),
                          compiler_params=pltpu.CompilerParams(**params), **kwargs)


def _sds(shape, dtype):
    return jax.ShapeDtypeStruct(shape, dtype)


_GELU_C = 0.7978845608028654
_GELU_A = 0.044715


def _gelu(x):
    inner = _GELU_C * (x + _GELU_A * (x * x * x))
    return 0.5 * x * (1.0 + jnp.tanh(inner))


def _gelu_grad(x):
    x2 = x * x
    t = jnp.tanh(_GELU_C * (x + _GELU_A * (x2 * x)))
    return 0.5 * (1.0 + t) + 0.5 * x * (1.0 - t * t) * (_GELU_C * (1.0 + 3.0 * _GELU_A * x2))


def _sigmoid(x):
    return 1.0 / (1.0 + jnp.exp(-x))


def _log_sigmoid(z):
    return jnp.minimum(z, 0.0) - jnp.log(1.0 + jnp.exp(-jnp.abs(z)))


def _dot(a, b):
    return jnp.dot(a, b, preferred_element_type=F32)


def _dot_nt(a, b):
    return lax.dot_general(a, b, (((1,), (1,)), ((), ())), preferred_element_type=F32)


def _dot_tn(a, b):
    return lax.dot_general(a, b, (((0,), (0,)), ((), ())), preferred_element_type=F32)


def _head_rstd(x):
    lane = lax.broadcasted_iota(jnp.int32, x.shape, 1)
    low = lane < HEAD_DIM
    sq = x * x
    s_lo = jnp.sum(jnp.where(low, sq, 0.0), axis=-1, keepdims=True)
    s_hi = jnp.sum(jnp.where(low, 0.0, sq), axis=-1, keepdims=True)
    ms = jnp.where(low, s_lo, s_hi) * (1.0 / HEAD_DIM)
    return lax.rsqrt(ms + EPS)


def _head_mean(x):
    lane = lax.broadcasted_iota(jnp.int32, x.shape, 1)
    low = lane < HEAD_DIM
    s_lo = jnp.sum(jnp.where(low, x, 0.0), axis=-1, keepdims=True)
    s_hi = jnp.sum(jnp.where(low, 0.0, x), axis=-1, keepdims=True)
    return jnp.where(low, s_lo, s_hi) * (1.0 / HEAD_DIM)


def _full(shape):
    zeros = (0,) * len(shape)
    return pl.BlockSpec(shape, lambda i: zeros)


def norm_matmul(x, g, w, *, name, epilogue="none", tm=512):
    t, d = x.shape
    sharded = w.ndim == 3
    per = w.shape[2] if sharded else w.shape[1]
    n = N_SHARDS * per if sharded else per
    tm = min(tm, t)

    def body(x_ref, g_ref, w_ref, h_ref, r_ref, *outs):
        xv = x_ref[...]
        r = lax.rsqrt(jnp.mean(xv * xv, axis=-1, keepdims=True) + EPS)
        h = ((xv * r) * g_ref[...]).astype(BF16)
        h_ref[...] = h
        r_ref[...] = r
        for s in range(N_SHARDS if sharded else 1):
            cols = slice(s * per, (s + 1) * per)
            y = _dot(h, w_ref[s] if sharded else w_ref[...])
            if epilogue == "none":
                outs[0][:, cols] = y
            else:
                a = jnp.maximum(y, 0.0)
                outs[0][:, cols] = a.astype(BF16)
                outs[1][:, cols] = (a * a).astype(BF16)

    row = lambda i: (i, 0)
    out_shape = [_sds((t, d), BF16), _sds((t, 1), F32)]
    out_specs = [pl.BlockSpec((tm, d), row), pl.BlockSpec((tm, 1), row)]
    if epilogue == "none":
        out_shape.append(_sds((t, n), F32))
        out_specs.append(pl.BlockSpec((tm, n), row))
    else:
        out_shape += [_sds((t, n), BF16), _sds((t, n), BF16)]
        out_specs += [pl.BlockSpec((tm, n), row)] * 2
    return _pcall(
        body, name=name, out_shape=out_shape, grid=(t // tm,),
        in_specs=[pl.BlockSpec((tm, d), row), _full((1, d)), _full(w.shape)],
        out_specs=out_specs, semantics=("parallel",))(x, g, w)


def matmul_residual(a, w, res, *, name, tm=512):
    t, k = a.shape
    n = w.shape[1]
    tm = min(tm, t)

    def body(a_ref, w_ref, res_ref, o_ref):
        o_ref[...] = res_ref[...] + _dot(a_ref[...], w_ref[...])

    row = lambda i: (i, 0)
    return _pcall(
        body, name=name, out_shape=_sds((t, n), F32), grid=(t // tm,),
        in_specs=[pl.BlockSpec((tm, k), row), _full(w.shape), pl.BlockSpec((tm, n), row)],
        out_specs=pl.BlockSpec((tm, n), row), semantics=("parallel",))(a, w, res)


def ple_forward(x, g, w_gate, p, w_proj, *, name, tm=256):
    t, d = x.shape
    tm = min(tm, t)

    def body(x_ref, g_ref, wg_ref, p_ref, wp_ref, h_ref, r_ref, gate_ref, pp_ref, o_ref):
        xv = x_ref[...]
        r = lax.rsqrt(jnp.mean(xv * xv, axis=-1, keepdims=True) + EPS)
        h = ((xv * r) * g_ref[...]).astype(BF16)
        h_ref[...] = h
        r_ref[...] = r
        gate = _sigmoid(_dot(h, wg_ref[...]))
        gate_ref[...] = gate
        pb = p_ref[...].astype(BF16)
        per = d // N_SHARDS
        for s in range(N_SHARDS):
            cols = slice(s * per, (s + 1) * per)
            pp = _dot(pb, wp_ref[s])
            pp_ref[:, cols] = pp.astype(BF16)
            o_ref[:, cols] = xv[:, cols] + pp * gate[:, cols]

    row = lambda i: (i, 0)
    fixed = lambda i: (0, 0)
    return _pcall(
        body, name=name,
        out_shape=[_sds((t, d), BF16), _sds((t, 1), F32), _sds((t, d), F32), _sds((t, d), BF16), _sds((t, d), F32)],
        grid=(t // tm,),
        in_specs=[pl.BlockSpec((tm, d), row), pl.BlockSpec((1, d), fixed), pl.BlockSpec((d, d), fixed),
                  pl.BlockSpec((tm, PLE_DIM), row),
                  pl.BlockSpec((N_SHARDS, PLE_DIM, d // N_SHARDS), lambda i: (0, 0, 0))],
        out_specs=[pl.BlockSpec((tm, d), row), pl.BlockSpec((tm, 1), row), pl.BlockSpec((tm, d), row),
                   pl.BlockSpec((tm, d), row), pl.BlockSpec((tm, d), row)],
        semantics=("parallel",))(x, g, w_gate, p, w_proj)


def _tril_mask():
    r = lax.broadcasted_iota(jnp.int32, (CHUNK, CHUNK), 0)
    c = lax.broadcasted_iota(jnp.int32, (CHUNK, CHUNK), 1)
    return c <= r


def _sgu_common(pre_ref, gv_ref, ws_ref):
    pre = pre_ref[...]
    pre_u, pre_v = pre[:, :D_MODEL], pre[:, D_MODEL:]
    u = _gelu(pre_u)
    v = _gelu(pre_v)
    r = lax.rsqrt(jnp.mean(v * v, axis=-1, keepdims=True) + EPS)
    vhat = v * r
    vn = (vhat * gv_ref[...]).astype(BF16)
    tril = _tril_mask()
    wm = [jnp.where(tril, ws_ref[g], 0.0).astype(BF16) for g in range(N_GROUPS)]
    return pre_u, pre_v, u, r, vhat, vn, wm, tril


def sgu_forward(pre, g_v, w_s, b_full, *, name):
    t = pre.shape[0]

    def body(pre_ref, gv_ref, ws_ref, b_ref, y_ref):
        _, _, u, _, _, vn, wm, _ = _sgu_common(pre_ref, gv_ref, ws_ref)
        for g in range(N_GROUPS):
            cols = slice(g * LANES, (g + 1) * LANES)
            mix = _dot(wm[g], vn[:, cols]) + b_ref[:, cols]
            y_ref[:, cols] = (u[:, cols] * mix).astype(BF16)

    return _pcall(
        body, name=name, out_shape=_sds((t, D_MODEL), BF16), grid=(t // CHUNK,),
        in_specs=[pl.BlockSpec((CHUNK, 2 * D_MODEL), lambda i: (i, 0)), pl.BlockSpec((1, D_MODEL), lambda i: (0, 0)),
                  pl.BlockSpec((N_GROUPS, CHUNK, CHUNK), lambda i: (0, 0, 0)),
                  pl.BlockSpec((CHUNK, D_MODEL), lambda i: (0, 0))],
        out_specs=pl.BlockSpec((CHUNK, D_MODEL), lambda i: (i, 0)),
        semantics=("parallel",))(pre, g_v, w_s, b_full)


def head_norm(pre, g128, *, name, col_block=0, scale=1.0, passthrough=False, tm=512):
    t = pre.shape[0]
    tm = min(tm, t)

    def body(*refs):
        if passthrough:
            x_ref, v_ref, g_ref, o_ref, vo_ref = refs
            vo_ref[...] = v_ref[...].astype(BF16)
        else:
            x_ref, g_ref, o_ref = refs
        g = g_ref[...] * scale
        for b in range(D_MODEL // LANES):
            cols = slice(b * LANES, (b + 1) * LANES)
            xv = x_ref[:, cols]
            o_ref[:, cols] = ((xv * _head_rstd(xv)) * g).astype(BF16)

    x_spec = pl.BlockSpec((tm, D_MODEL), lambda i: (i, col_block))
    g_spec = pl.BlockSpec((1, LANES), lambda i: (0, 0))
    o_spec = pl.BlockSpec((tm, D_MODEL), lambda i: (i, 0))
    if passthrough:
        return _pcall(body, name=name, out_shape=[_sds((t, D_MODEL), BF16)] * 2, grid=(t // tm,),
                      in_specs=[x_spec, pl.BlockSpec((tm, D_MODEL), lambda i: (i, 1)), g_spec],
                      out_specs=[o_spec, o_spec], semantics=("parallel",))(pre, pre, g128)
    return _pcall(body, name=name, out_shape=_sds((t, D_MODEL), BF16), grid=(t // tm,),
                  in_specs=[x_spec, g_spec], out_specs=o_spec, semantics=("parallel",))(pre, g128)


def _suffix_matrix(n):
    r = lax.broadcasted_iota(jnp.int32, (n, n), 0)
    c = lax.broadcasted_iota(jnp.int32, (n, n), 1)
    return jnp.where(r > c, 1.0, 0.0).astype(BF16)


def _prefix_matrix(n):
    r = lax.broadcasted_iota(jnp.int32, (n, n), 0)
    c = lax.broadcasted_iota(jnp.int32, (n, n), 1)
    return jnp.where(r < c, 1.0, 0.0).astype(BF16)


def _block_cumsum(a, tri):
    return _dot(a.astype(BF16), tri)


def _stacked_causal(n):
    r = lax.broadcasted_iota(jnp.int32, (2 * n, n), 0)
    c = lax.broadcasted_iota(jnp.int32, (2 * n, n), 1)
    return c < jnp.where(r >= n, r - n, r)


def _stack_heads(a, low):
    zero = jnp.zeros_like(a)
    return jnp.concatenate([jnp.where(low, a, zero), jnp.where(low, zero, a)], axis=0)


def stick_breaking_forward(q, k, v, *, name):
    t = q.shape[0]
    blk = min(ATT_BLOCK, t)
    nq = t // blk

    def body(q_ref, k_ref, v_ref, o_ref):
        i = pl.program_id(1)
        low = lax.broadcasted_iota(jnp.int32, (blk, LANES), 1) < HEAD_DIM
        tri = _suffix_matrix(blk)
        causal = _stacked_causal(blk)
        qs = _stack_heads(q_ref[...], low)

        def block(j, carry, acc, masked):
            rows = pl.ds(pl.multiple_of(j * blk, blk), blk)
            z = _dot_nt(qs, k_ref[rows, :])
            ls = _log_sigmoid(z)
            lg = ls - z
            if masked:
                lg = jnp.where(causal, lg, 0.0)
            s = ls + _block_cumsum(lg, tri) + carry
            a = jnp.exp(s)
            if masked:
                a = jnp.where(causal, a, 0.0)
            acc = acc + _dot(a.astype(BF16), v_ref[rows, :])
            return carry + jnp.sum(lg, axis=-1, keepdims=True), acc

        state = block(i, jnp.zeros((2 * blk, 1), F32), jnp.zeros((2 * blk, LANES), F32), True)

        def two_blocks(n, st):
            st = block(i - 1 - 2 * n, st[0], st[1], False)
            return block(i - 2 - 2 * n, st[0], st[1], False)

        state = lax.fori_loop(0, i // 2, two_blocks, state)
        _, acc = lax.fori_loop(0, i % 2, lambda n, st: block(0, st[0], st[1], False), state)
        o_ref[...] = jnp.where(low, acc[:blk], acc[blk:]).astype(BF16)

    return _pcall(
        body, name=name, out_shape=_sds((t, D_MODEL), BF16), grid=(D_MODEL // LANES, nq),
        in_specs=[pl.BlockSpec((blk, LANES), lambda p, i: (i, p)), pl.BlockSpec((t, LANES), lambda p, i: (0, p)),
                  pl.BlockSpec((t, LANES), lambda p, i: (0, p))],
        out_specs=pl.BlockSpec((blk, LANES), lambda p, i: (i, p)),
        semantics=("parallel", "arbitrary"))(q, k, v)


def loss_forward(x, target, *, name, tm=512):
    t, d = x.shape
    tm = min(tm, t)

    def body(x_ref, t_ref, l_ref, dx_ref):
        @pl.when(pl.program_id(0) == 0)
        def _():
            l_ref[...] = jnp.zeros_like(l_ref)

        diff = x_ref[...] - t_ref[...]
        dx_ref[...] = diff * (1.0 / d)
        l_ref[...] += 0.5 * jnp.sum(jnp.mean(diff * diff, axis=-1, keepdims=True))

    return _pcall(
        body, name=name, out_shape=[_sds((8, LANES), F32), _sds((t, d), F32)], grid=(t // tm,),
        in_specs=[pl.BlockSpec((tm, d), lambda i: (i, 0))] * 2,
        out_specs=[pl.BlockSpec((8, LANES), lambda i: (0, 0)), pl.BlockSpec((tm, d), lambda i: (i, 0))],
        semantics=("arbitrary",))(x, target)


def matmul_nt(dy, w, *, name, mul=None, out_dtype=F32, tm=512):
    t, n = dy.shape
    k = w.shape[0]
    tm = min(tm, t)

    def body(*refs):
        if mul is None:
            dy_ref, w_ref, o_ref = refs
        else:
            dy_ref, w_ref, m_ref, o_ref = refs
        y = _dot_nt(dy_ref[...].astype(BF16), w_ref[...])
        if mul is not None:
            y = y * (2.0 * m_ref[...].astype(F32))
        o_ref[...] = y.astype(out_dtype)

    row = lambda i: (i, 0)
    in_specs = [pl.BlockSpec((tm, n), row), _full(w.shape)]
    args = [dy, w]
    if mul is not None:
        in_specs.append(pl.BlockSpec((tm, k), row))
        args.append(mul)
    return _pcall(body, name=name, out_shape=_sds((t, k), out_dtype), grid=(t // tm,), in_specs=in_specs,
                  out_specs=pl.BlockSpec((tm, k), row), semantics=("parallel",))(*args)


def matmul_tn(a, dy, *, name, col_shards, tk=512):
    t, k = a.shape
    n = dy.shape[1]
    if col_shards:
        tn = n // N_SHARDS

        def body(a_ref, dy_ref, o_ref):
            o_ref[...] = _dot_tn(a_ref[...].astype(BF16), dy_ref[...].astype(BF16))

        return _pcall(body, name=name, out_shape=_sds((N_SHARDS, k, tn), F32), grid=(N_SHARDS,),
                      in_specs=[_full((t, k)), pl.BlockSpec((t, tn), lambda j: (0, j))],
                      out_specs=pl.BlockSpec((None, k, tn), lambda j: (j, 0, 0)), semantics=("parallel",))(a, dy)

    tk = min(tk, k)

    def body(a_ref, dy_ref, o_ref, dy_bf):
        @pl.when(pl.program_id(0) == 0)
        def _():
            dy_bf[...] = dy_ref[...].astype(BF16)

        o_ref[...] = _dot_tn(a_ref[...].astype(BF16), dy_bf[...])

    return _pcall(body, name=name, out_shape=_sds((k, n), F32), grid=(k // tk,),
                  in_specs=[pl.BlockSpec((t, tk), lambda i: (0, i)), _full((t, n))],
                  out_specs=pl.BlockSpec((tk, n), lambda i: (i, 0)),
                  scratch_shapes=[pltpu.VMEM((t, n), BF16)], semantics=("arbitrary",))(a, dy)


def norm_backward(dpre, w, x, g, rstd, dx_out, *, name, tm=512):
    t, d = x.shape
    n = dpre.shape[1]
    tm = min(tm, t)
    if w.ndim == 3:
        w_spec = pl.BlockSpec(w.shape, lambda i: (0, 0, 0))
    else:
        w_spec = pl.BlockSpec(w.shape, lambda i: (0, 0))

    def body(dp_ref, w_ref, x_ref, g_ref, r_ref, dxo_ref, dx_ref, dg_ref):
        @pl.when(pl.program_id(0) == 0)
        def _():
            dg_ref[...] = jnp.zeros_like(dg_ref)

        if w.ndim == 3:
            per = n // N_SHARDS
            dh = _dot_nt(dp_ref[:, 0:per], w_ref[0])
            for s in range(1, N_SHARDS):
                dh = dh + _dot_nt(dp_ref[:, s * per:(s + 1) * per], w_ref[s])
        else:
            dh = _dot_nt(dp_ref[...], w_ref[...])
        r = r_ref[...]
        xn = x_ref[...] * r
        dg_ref[...] += jnp.sum(dh * xn, axis=0, keepdims=True)
        dxn = dh * g_ref[...]
        dx = r * (dxn - xn * jnp.mean(dxn * xn, axis=-1, keepdims=True))
        dx_ref[...] = dxo_ref[...] + dx

    row = lambda i: (i, 0)
    fixed = lambda i: (0, 0)
    return _pcall(
        body, name=name, out_shape=[_sds((t, d), F32), _sds((1, d), F32)], grid=(t // tm,),
        in_specs=[pl.BlockSpec((tm, n), row), w_spec, pl.BlockSpec((tm, d), row),
                  pl.BlockSpec((1, d), fixed), pl.BlockSpec((tm, 1), row), pl.BlockSpec((tm, d), row)],
        out_specs=[pl.BlockSpec((tm, d), row), pl.BlockSpec((1, d), fixed)],
        semantics=("arbitrary",))(dpre, w, x, g, rstd, dx_out)


def ple_backward(dx, gate, pp, *, name, tm=512):
    t, d = dx.shape
    tm = min(tm, t)

    def body(dx_ref, gate_ref, pp_ref, dg_ref, dp_ref):
        dxv = dx_ref[...]
        gate = gate_ref[...]
        dg_ref[...] = (dxv * pp_ref[...].astype(F32) * (gate * (1.0 - gate))).astype(BF16)
        dp_ref[...] = (dxv * gate).astype(BF16)

    spec = pl.BlockSpec((tm, d), lambda i: (i, 0))
    return _pcall(body, name=name, out_shape=[_sds((t, d), BF16)] * 2, grid=(t // tm,), in_specs=[spec] * 3,
                  out_specs=[spec] * 2, semantics=("parallel",))(dx, gate, pp)


def sgu_backward(dy, pre, g_v, w_s, b_full, *, name):
    t = pre.shape[0]
    n_chunks = t // CHUNK

    def body(dy_ref, pre_ref, gv_ref, ws_ref, b_ref, dpre_ref, dws_ref, db_ref, dgv_ref, dvn_s, dbf_s):
        step = pl.program_id(0)

        @pl.when(step == 0)
        def _():
            dws_ref[...] = jnp.zeros_like(dws_ref)
            dgv_ref[...] = jnp.zeros_like(dgv_ref)
            dbf_s[...] = jnp.zeros_like(dbf_s)

        pre_u, pre_v, u, r, vhat, vn, wm, tril = _sgu_common(pre_ref, gv_ref, ws_ref)
        dyv = dy_ref[...]
        for g in range(N_GROUPS):
            cols = slice(g * LANES, (g + 1) * LANES)
            mix = _dot(wm[g], vn[:, cols]) + b_ref[:, cols]
            dmix = dyv[:, cols] * u[:, cols]
            dmix_b = dmix.astype(BF16)
            du = dyv[:, cols] * mix
            dpre_ref[:, cols] = (du * _gelu_grad(pre_u[:, cols])).astype(BF16)
            dws_ref[g] += jnp.where(tril, _dot_nt(dmix_b, vn[:, cols]), 0.0)
            dbf_s[:, cols] += dmix
            dvn_s[:, cols] = _dot_tn(wm[g], dmix_b)
        dvn = dvn_s[...]
        dgv_ref[...] += jnp.sum(dvn * vhat, axis=0, keepdims=True)
        dxn = dvn * gv_ref[...]
        dv = r * (dxn - vhat * jnp.mean(dxn * vhat, axis=-1, keepdims=True))
        dpre_ref[:, D_MODEL:] = (dv * _gelu_grad(pre_v)).astype(BF16)

        @pl.when(step == n_chunks - 1)
        def _():
            lane = lax.broadcasted_iota(jnp.int32, (CHUNK, LANES), 1)
            acc = jnp.zeros((CHUNK, LANES), F32)
            for g in range(N_GROUPS):
                s = jnp.sum(dbf_s[:, g * LANES:(g + 1) * LANES], axis=-1, keepdims=True)
                acc = jnp.where(lane == g, s, acc)
            db_ref[...] = acc

    fixed2 = lambda i: (0, 0)
    return _pcall(
        body, name=name,
        out_shape=[_sds((t, 2 * D_MODEL), BF16), _sds((N_GROUPS, CHUNK, CHUNK), F32), _sds((CHUNK, LANES), F32),
                   _sds((1, D_MODEL), F32)],
        grid=(n_chunks,),
        in_specs=[pl.BlockSpec((CHUNK, D_MODEL), lambda i: (i, 0)), pl.BlockSpec((CHUNK, 2 * D_MODEL), lambda i: (i, 0)),
                  pl.BlockSpec((1, D_MODEL), fixed2), pl.BlockSpec((N_GROUPS, CHUNK, CHUNK), lambda i: (0, 0, 0)),
                  pl.BlockSpec((CHUNK, D_MODEL), fixed2)],
        out_specs=[pl.BlockSpec((CHUNK, 2 * D_MODEL), lambda i: (i, 0)),
                   pl.BlockSpec((N_GROUPS, CHUNK, CHUNK), lambda i: (0, 0, 0)), pl.BlockSpec((CHUNK, LANES), fixed2),
                   pl.BlockSpec((1, D_MODEL), fixed2)],
        scratch_shapes=[pltpu.VMEM((CHUNK, D_MODEL), F32), pltpu.VMEM((CHUNK, D_MODEL), F32)],
        semantics=("arbitrary",))(dy, pre, g_v, w_s, b_full)


def head_norm_backward(dy, pre, g128, *, name, col_block=0, scale=1.0, passthrough=None, tm=512):
    t = dy.shape[0]
    tm = min(tm, t)
    width = 2 * D_MODEL if passthrough is not None else D_MODEL

    def body(*refs):
        if passthrough is not None:
            dy_ref, x_ref, g_ref, dv_ref, o_ref, dg_ref = refs
            o_ref[:, D_MODEL:] = dv_ref[...].astype(BF16)
        else:
            dy_ref, x_ref, g_ref, o_ref, dg_ref = refs

        @pl.when(pl.program_id(0) == 0)
        def _():
            dg_ref[...] = jnp.zeros_like(dg_ref)

        g = g_ref[...]
        dg = jnp.zeros((1, LANES), F32)
        for b in range(D_MODEL // LANES):
            cols = slice(b * LANES, (b + 1) * LANES)
            xv = x_ref[:, cols]
            r = _head_rstd(xv)
            xn = xv * r
            dyv = dy_ref[:, cols] * scale
            dg = dg + jnp.sum(dyv * xn, axis=0, keepdims=True)
            dxn = dyv * g
            o_ref[:, cols] = (r * (dxn - xn * _head_mean(dxn * xn))).astype(BF16)
        dg_ref[...] += dg

    row = lambda i: (i, 0)
    in_specs = [pl.BlockSpec((tm, D_MODEL), row), pl.BlockSpec((tm, D_MODEL), lambda i: (i, col_block)),
                pl.BlockSpec((1, LANES), lambda i: (0, 0))]
    args = [dy, pre, g128]
    if passthrough is not None:
        in_specs.append(pl.BlockSpec((tm, D_MODEL), row))
        args.append(passthrough)
    return _pcall(body, name=name, out_shape=[_sds((t, width), BF16), _sds((1, LANES), F32)], grid=(t // tm,),
                  in_specs=in_specs,
                  out_specs=[pl.BlockSpec((tm, width), row), pl.BlockSpec((1, LANES), lambda i: (0, 0))],
                  semantics=("arbitrary",))(*args)


def stick_breaking_backward(q, k, v, do, *, name):
    t = q.shape[0]
    blk = min(ATT_BLOCK, t)
    nq = t // blk

    def body(q_ref, k_ref, v_ref, do_ref, dq_ref, dk_ref, dv_ref, s_buf, sg_buf):
        i = pl.program_id(1)

        @pl.when(i == 0)
        def _():
            dk_ref[...] = jnp.zeros_like(dk_ref)
            dv_ref[...] = jnp.zeros_like(dv_ref)

        low = lax.broadcasted_iota(jnp.int32, (blk, LANES), 1) < HEAD_DIM
        suffix = _suffix_matrix(blk)
        prefix = _prefix_matrix(blk)
        causal = _stacked_causal(blk)
        qs = _stack_heads(q_ref[...], low)
        dos = _stack_heads(do_ref[...], low)

        def log_weights(j, carry, masked):
            rows = pl.ds(pl.multiple_of(j * blk, blk), blk)
            z = _dot_nt(qs, k_ref[rows, :])
            ls = _log_sigmoid(z)
            lg = ls - z
            if masked:
                lg = jnp.where(causal, lg, 0.0)
            s_buf[j] = ls + _block_cumsum(lg, suffix) + carry
            sg_buf[j] = jnp.exp(ls)
            return carry + jnp.sum(lg, axis=-1, keepdims=True)

        carry = log_weights(i, jnp.zeros((2 * blk, 1), F32), True)
        carry = lax.fori_loop(0, i // 2, lambda n, c: log_weights(i - 2 - 2 * n, log_weights(i - 1 - 2 * n, c, False),
                                                                  False), carry)
        lax.fori_loop(0, i % 2, lambda n, c: log_weights(0, c, False), carry)

        def grads(j, pcarry, dq_acc, masked):
            rows = pl.ds(pl.multiple_of(j * blk, blk), blk)
            a = jnp.exp(s_buf[j])
            if masked:
                a = jnp.where(causal, a, 0.0)
            sg = sg_buf[j]
            ds = _dot_nt(dos, v_ref[rows, :]) * a
            before = _block_cumsum(ds, prefix) + pcarry
            if masked:
                before = jnp.where(causal, before, 0.0)
            dz = (ds - sg * (ds + before)).astype(BF16)
            dq_acc = dq_acc + _dot(dz, k_ref[rows, :])
            dk_ref[rows, :] += _dot_tn(dz, qs)
            dv_ref[rows, :] += _dot_tn(a.astype(BF16), dos)
            return pcarry + jnp.sum(ds, axis=-1, keepdims=True), dq_acc

        def two_blocks(n, st):
            st = grads(2 * n, st[0], st[1], False)
            return grads(2 * n + 1, st[0], st[1], False)

        state = lax.fori_loop(0, i // 2, two_blocks,
                              (jnp.zeros((2 * blk, 1), F32), jnp.zeros((2 * blk, LANES), F32)))
        state = lax.fori_loop(0, i % 2, lambda n, st: grads(i - 1, st[0], st[1], False), state)
        _, dq_acc = grads(i, state[0], state[1], True)
        dq_ref[...] = jnp.where(low, dq_acc[:blk], dq_acc[blk:])

    full = pl.BlockSpec((t, LANES), lambda p, i: (0, p))
    qblk = pl.BlockSpec((blk, LANES), lambda p, i: (i, p))
    return _pcall(
        body, name=name, out_shape=[_sds((t, D_MODEL), F32)] * 3, grid=(D_MODEL // LANES, nq),
        in_specs=[qblk, full, full, qblk], out_specs=[qblk, full, full],
        scratch_shapes=[pltpu.VMEM((nq, 2 * blk, blk), F32), pltpu.VMEM((nq, 2 * blk, blk), F32)],
        semantics=("parallel", "arbitrary"))(q, k, v, do)


def _mlp_forward(x, g, w_up, w_down, tag):
    h, r, a, a2 = norm_matmul(x, g, w_up, name=f"mlp_up_{tag}", epilogue="relu2")
    return matmul_residual(a2, w_down, x, name=f"mlp_down_{tag}"), (x, h, r, a, a2)


def _mlp_backward(dx, saved, g, w_up, w_down, tag):
    x, h, r, a, a2 = saved
    d_w_down = matmul_tn(a2, dx, name=f"d_w_down_{tag}", col_shards=False)
    dpre = matmul_nt(dx, w_down, name=f"d_mlp_act_{tag}", mul=a, out_dtype=BF16)
    d_w_up = matmul_tn(h, dpre, name=f"d_w_up_{tag}", col_shards=True)
    dx, d_g = norm_backward(dpre, w_up, x, g, r, dx, name=f"d_mlp_norm_{tag}")
    return dx, d_w_up, d_w_down, d_g


def _ple_backward(dx, saved, p, g, w_gate, tag):
    x, h, r, gate, pp = saved
    dgate, dproj = ple_backward(dx, gate, pp, name=f"d_ple_{tag}")
    d_w_proj = matmul_tn(p, dproj, name=f"d_w_ple_proj_{tag}", col_shards=True)
    d_w_gate = matmul_tn(h, dgate, name=f"d_w_ple_gate_{tag}", col_shards=False)
    dx, d_g = norm_backward(dgate, w_gate, x, g, r, dx, name=f"d_ple_norm_{tag}")
    return dx, d_w_gate, d_w_proj, d_g


def local_step(x, p, target, w, late=None):
    row = lambda v: v.reshape(1, -1)
    g128 = lambda v: jnp.tile(v.reshape(1, HEAD_DIM), (1, 2))
    scale = HEAD_DIM ** -0.5
    b_full = jnp.repeat(jnp.transpose(w["b_spatial"][0]), LANES, axis=1)
    w_s = w["w_spatial"][0]

    w = dict(w)

    def fetch(name, after):
        if late is not None:
            w.update(late.first_layer_weights(name, after))
        return w[name] if name == "w_kv" else w[name][0]

    x0 = x
    h_a, r_a, pre_a = norm_matmul(x0, row(w["ln_mix_a"][0]), fetch("w_in_a", x0), name="sgu_in")
    y_a = sgu_forward(pre_a, row(w["g_v_a"][0]), w_s, b_full, name="sgu_mix")
    x1 = matmul_residual(y_a, fetch("w_out_a", y_a), x0, name="sgu_out")
    h_m, r_m, a_m, a2_m = norm_matmul(x1, row(w["ln_mlp"][0]), fetch("w_up", x1), name="mlp_up_0", epilogue="relu2")
    x2 = matmul_residual(a2_m, fetch("w_down", a2_m), x1, name="mlp_down_0")
    mlp0 = (x1, h_m, r_m, a_m, a2_m)
    ple0 = ple_forward(x2, row(w["ln_ple"][0]), fetch("w_ple_gate", x2), p[0], fetch("w_ple_proj", x2), name="ple_0")
    x3 = ple0[4]
    fetch("w_kv", x3)
    if late is not None:
        late.after_first_layer(x3)
    h_kv, r_kv, kv_pre = norm_matmul(x3, row(w["ln_kv"]), w["w_kv"], name="kv_proj")
    k_n, v_b = head_norm(kv_pre, g128(w["g_k"]), name="k_norm", passthrough=True)
    if late is not None:
        w = {**w, **late.second_layer_weights(k_n)}
    h_q, r_q, q_pre = norm_matmul(x3, row(w["ln_mix_b"][0]), w["w_q"][0], name="q_proj")
    q_n = head_norm(q_pre, g128(w["g_q"][0]), name="q_norm", scale=scale)
    o = stick_breaking_forward(q_n, k_n, v_b, name="sb_fwd")
    x4 = matmul_residual(o, w["w_out_b"][0], x3, name="sb_out")
    x5, mlp1 = _mlp_forward(x4, row(w["ln_mlp"][1]), w["w_up"][1], w["w_down"][1], 1)
    ple1 = ple_forward(x5, row(w["ln_ple"][1]), w["w_ple_gate"][1], p[1], w["w_ple_proj"][1], name="ple_1")
    x6 = ple1[4]
    loss_blk, dx = loss_forward(x6, target, name="loss")

    g = {}
    dx, dwg1, dwp1, dlnp1 = _ple_backward(dx, (x5,) + tuple(ple1[:4]), p[1], row(w["ln_ple"][1]), w["w_ple_gate"][1], 1)
    dx, dwu1, dwd1, dlnm1 = _mlp_backward(dx, mlp1, row(w["ln_mlp"][1]), w["w_up"][1], w["w_down"][1], 1)
    g["w_out_b"] = matmul_tn(o, dx, name="d_w_out_b", col_shards=False)
    do = matmul_nt(dx, w["w_out_b"][0], name="d_sb_out", out_dtype=BF16)
    dq_n, dk_n, dv = stick_breaking_backward(q_n, k_n, v_b, do, name="sb_bwd")
    dq_pre, dgq = head_norm_backward(dq_n, q_pre, g128(w["g_q"][0]), name="d_q_norm", scale=scale)
    dkv_pre, dgk = head_norm_backward(dk_n, kv_pre, g128(w["g_k"]), name="d_k_norm", passthrough=dv)
    g["w_q"] = matmul_tn(h_q, dq_pre, name="d_w_q", col_shards=False)
    g["w_kv"] = matmul_tn(h_kv, dkv_pre, name="d_w_kv", col_shards=True)
    dx, g["ln_mix_b"] = norm_backward(dq_pre, w["w_q"][0], x3, row(w["ln_mix_b"][0]), r_q, dx, name="d_q_in")
    dx, g["ln_kv"] = norm_backward(dkv_pre, w["w_kv"], x3, row(w["ln_kv"]), r_kv, dx, name="d_kv_in")
    g["g_q"] = dgq[:, :HEAD_DIM] + dgq[:, HEAD_DIM:]
    g["g_k"] = (dgk[:, :HEAD_DIM] + dgk[:, HEAD_DIM:]).reshape(HEAD_DIM)
    g["ln_kv"] = g["ln_kv"].reshape(D_MODEL)
    ln_ple0, ln_mlp0 = row(w["ln_ple"][0]), row(w["ln_mlp"][0])
    if late is not None:
        ln_ple0 = ln_ple0 + late.second_layer_grads(
            {("w_kv", 0): g["w_kv"], ("w_q", 0): g["w_q"], ("w_out_b", 0): g["w_out_b"], ("w_up", 1): dwu1,
             ("w_down", 1): dwd1, ("w_ple_gate", 1): dwg1, ("w_ple_proj", 1): dwp1}, dx)[0, 0]
    dx, dwg0, dwp0, dlnp0 = _ple_backward(dx, (x2,) + tuple(ple0[:4]), p[0], ln_ple0, w["w_ple_gate"][0], 0)
    if late is not None:
        ln_mlp0 = ln_mlp0 + late.first_ple_backward_done(dx)[0, 0]
    dx, dwu0, dwd0, dlnm0 = _mlp_backward(dx, mlp0, ln_mlp0, w["w_up"][0], w["w_down"][0], 0)
    g["w_out_a"] = matmul_tn(y_a, dx, name="d_w_out_a", col_shards=False)
    dy_a = matmul_nt(dx, w["w_out_a"][0], name="d_sgu_out")
    dpre_a, dws, db, g["g_v_a"] = sgu_backward(dy_a, pre_a, row(w["g_v_a"][0]), w_s, b_full, name="d_sgu_mix")
    g["w_in_a"] = matmul_tn(h_a, dpre_a, name="d_w_in_a", col_shards=True)
    dx, g["ln_mix_a"] = norm_backward(dpre_a, w["w_in_a"][0], x0, row(w["ln_mix_a"][0]), r_a, dx, name="d_sgu_in")
    g["w_spatial"] = dws[None]
    g["b_spatial"] = jnp.transpose(db[:, :N_GROUPS])[None]
    g["w_up"] = (dwu0, dwu1)
    g["w_down"] = (dwd0, dwd1)
    g["w_ple_gate"] = (dwg0, dwg1)
    g["w_ple_proj"] = (dwp0, dwp1)
    g["ln_mlp"] = jnp.concatenate([dlnm0, dlnm1], axis=0)
    g["ln_ple"] = jnp.concatenate([dlnp0, dlnp1], axis=0)
    return loss_blk, dx, g


ANY = pl.BlockSpec(memory_space=pl.ANY)


def _place():
    x, y, c = lax.axis_index("x"), lax.axis_index("y"), lax.axis_index("c")
    others = [(1 - x, y), (x, 1 - y), (1 - x, 1 - y)]
    return x, y, c, 2 * x + y, others


def cast_into_slot(w3, layer, slot, *, name, tm=256):
    _, r, c = w3.shape
    tm = min(tm, r)

    def body(slot_ref, w_ref, o_ref):
        o_ref[...] = w_ref[...].astype(BF16)

    return _pcall(body, name=name, out_shape=_sds((N_SHARDS, r, c), BF16), grid=(r // tm,), num_prefetch=1,
                  in_specs=[pl.BlockSpec((None, tm, c), lambda i, s: (layer, i, 0))],
                  out_specs=pl.BlockSpec((None, tm, c), lambda i, s: (s[0], i, 0)),
                  semantics=("parallel",))(slot, w3)


def gather_shards(mats, vecs, *, name):
    nm, nv = len(mats), len(vecs)
    halves = [m.reshape(N_SHARDS, 2, m.shape[1] // 2, m.shape[2]) for m in mats]

    def body(*refs):
        vsrc = refs[nm:nm + nv]
        out, vout = refs[nm + nv:2 * nm + nv], refs[2 * nm + nv:2 * (nm + nv)]
        send, recv, vsend, vrecv, loc = refs[2 * (nm + nv):]
        x, y, c, s_me, others = _place()
        sib = (x, y, 1 - c)

        def ici(l, k):
            ox, oy = others[k]
            return pltpu.make_async_remote_copy(out[l].at[s_me, c], out[l].at[s_me, c], send.at[l, k], recv.at[l, k],
                                                device_id=(ox, oy, c), device_id_type=MESH)

        def landed(l, k, half):
            ox, oy = others[k]
            return out[l].at[2 * ox + oy, half]

        def passed_on(l, k):
            return pltpu.make_async_remote_copy(landed(l, k, c), landed(l, k, c), send.at[l, 3 + k], recv.at[l, 3 + k],
                                                device_id=sib, device_id_type=MESH)

        def vec(l, k):
            ox, oy = others[k]
            return pltpu.make_async_remote_copy(vsrc[l], vout[l].at[s_me], vsend.at[l, k], vrecv.at[l, k],
                                                device_id=(ox, oy, c), device_id_type=MESH)

        for l in range(nm):
            for k in range(3):
                ici(l, k).start()
        for l in range(nv):
            for k in range(3):
                vec(l, k).start()
        for l in range(nv):
            own = pltpu.make_async_copy(vsrc[l], vout[l].at[s_me], loc)
            own.start()
            own.wait()
        for l in range(nm):
            for k in range(3):
                pltpu.make_async_remote_copy(landed(l, k, c), landed(l, k, c), send.at[l, k], recv.at[l, k],
                                             device_id=sib, device_id_type=MESH).wait_recv()
                passed_on(l, k).start()
        for l in range(nm):
            for k in range(3):
                pltpu.make_async_remote_copy(landed(l, k, 1 - c), landed(l, k, 1 - c), send.at[l, 3 + k],
                                             recv.at[l, 3 + k], device_id=sib, device_id_type=MESH).wait_recv()
        for l in range(nv):
            for k in range(3):
                ox, oy = others[k]
                pltpu.make_async_remote_copy(vsrc[l], vout[l].at[2 * ox + oy], vsend.at[l, k], vrecv.at[l, k],
                                             device_id=sib, device_id_type=MESH).wait_recv()
        for l in range(nm):
            for k in range(3):
                ici(l, k).wait_send()
                passed_on(l, k).wait_send()
        for l in range(nv):
            for k in range(3):
                vec(l, k).wait_send()

    out_shape = [_sds(h.shape, BF16) for h in halves] + [_sds((N_SHARDS,) + v.shape, F32) for v in vecs]
    res = _pcall(body, name=name, out_shape=out_shape, in_specs=[ANY] * (nm + nv), out_specs=[ANY] * (nm + nv),
                 scratch_shapes=[pltpu.SemaphoreType.DMA((max(nm, 1), 6)), pltpu.SemaphoreType.DMA((max(nm, 1), 6)),
                                 pltpu.SemaphoreType.DMA((max(nv, 1), 3)), pltpu.SemaphoreType.DMA((max(nv, 1), 3)),
                                 pltpu.SemaphoreType.DMA(())],
                 aliases={l: l for l in range(nm)}, side_effects=True)(*halves, *vecs)
    return [r.reshape(m.shape) for r, m in zip(res[:nm], mats)], list(res[nm:])


HBM = pl.BlockSpec(memory_space=pltpu.HBM)
SEM = pl.BlockSpec(memory_space=pltpu.SEMAPHORE)
DATAFLOW = pltpu.SideEffectType.DATAFLOW_SIDE_EFFECTING


def _split_call(body, *, name, out_shape, in_specs, out_specs, aliases):
    return pl.pallas_call(body, name=name, out_shape=out_shape, in_specs=in_specs, out_specs=out_specs,
                          input_output_aliases=aliases,
                          compiler_params=pltpu.CompilerParams(has_side_effects=DATAFLOW))


def _token_shape():
    return jax.ShapeDtypeStruct((8, LANES), F32)


def gather_start(mats, after, *, name):
    n = len(mats)
    halves = [pltpu.with_memory_space_constraint(m.reshape(N_SHARDS, 2, m.shape[1] // 2, m.shape[2]), pltpu.HBM)
              for m in mats]

    def body(*refs):
        send, recv = refs[n + 1], refs[n + 2]
        out, token = refs[n + 3:2 * n + 3], refs[2 * n + 3]
        x, y, c, s_me, others = _place()
        for l in range(n):
            for k in range(3):
                ox, oy = others[k]
                pltpu.make_async_remote_copy(out[l].at[s_me, c], out[l].at[s_me, c], send.at[3 * l + k],
                                             recv.at[3 * l + k], device_id=(ox, oy, c), device_id_type=MESH).start()
        token[...] = jnp.zeros_like(token)

    res = _split_call(
        body, name=name,
        out_shape=(pltpu.SemaphoreType.DMA((3 * n,)), pltpu.SemaphoreType.DMA((3 * n,)),
                   *[pltpu.HBM(h.shape, BF16) for h in halves], _token_shape()),
        in_specs=[HBM] * n + [ANY], out_specs=(SEM, SEM, *[HBM] * n, pl.BlockSpec(memory_space=pltpu.VMEM)),
        aliases={l: 2 + l for l in range(n)})(*halves, after)
    return res[0], res[1], list(res[2:2 + n]), res[2 + n]


def gather_pass_on(bufs, send_a, recv_a, after, *, name, base=0):
    n = len(bufs)

    def body(*refs):
        send_a, recv_a = refs[n], refs[n + 1]
        out = refs[n + 3:2 * n + 3]
        send_b, recv_b, token = refs[2 * n + 3:]
        x, y, c, s_me, others = _place()
        for l in range(n):
            for k in range(3):
                ox, oy = others[k]
                landed, i = out[l].at[2 * ox + oy, c], 3 * l + k
                pltpu.make_async_remote_copy(landed, landed, send_a.at[3 * base + i], recv_a.at[3 * base + i],
                                             device_id=(x, y, 1 - c), device_id_type=MESH).wait_recv()
                pltpu.make_async_remote_copy(landed, landed, send_b.at[i], recv_b.at[i],
                                             device_id=(x, y, 1 - c), device_id_type=MESH).start()
        for l in range(n):
            for k in range(3):
                mine, i = out[l].at[s_me, c], 3 * (base + l) + k
                pltpu.make_async_remote_copy(mine, mine, send_a.at[i], recv_a.at[i],
                                             device_id=(x, y, 1 - c), device_id_type=MESH).wait_send()
        token[...] = jnp.zeros_like(token)

    res = _split_call(
        body, name=name,
        out_shape=(*[pltpu.HBM(b.shape, BF16) for b in bufs], pltpu.SemaphoreType.DMA((3 * n,)),
                   pltpu.SemaphoreType.DMA((3 * n,)), _token_shape()),
        in_specs=[HBM] * n + [SEM, SEM, ANY],
        out_specs=(*[HBM] * n, SEM, SEM, pl.BlockSpec(memory_space=pltpu.VMEM)),
        aliases={l: l for l in range(n)})(*bufs, send_a, recv_a, after)
    return list(res[:n]), res[n], res[n + 1], res[n + 2]


def gather_finish(bufs, send_b, recv_b, after, shapes, *, name):
    n = len(bufs)

    def body(*refs):
        send_b, recv_b = refs[n], refs[n + 1]
        out = refs[n + 3:]
        x, y, c, _, others = _place()
        for l in range(n):
            for k in range(3):
                ox, oy = others[k]
                theirs, mine, i = out[l].at[2 * ox + oy, 1 - c], out[l].at[2 * ox + oy, c], 3 * l + k
                pltpu.make_async_remote_copy(theirs, theirs, send_b.at[i], recv_b.at[i],
                                             device_id=(x, y, 1 - c), device_id_type=MESH).wait_recv()
                pltpu.make_async_remote_copy(mine, mine, send_b.at[i], recv_b.at[i],
                                             device_id=(x, y, 1 - c), device_id_type=MESH).wait_send()

    res = _split_call(
        body, name=name, out_shape=tuple(pltpu.HBM(b.shape, BF16) for b in bufs),
        in_specs=[HBM] * n + [SEM, SEM, ANY], out_specs=tuple([HBM] * n),
        aliases={l: l for l in range(n)})(*bufs, send_b, recv_b, after)
    return [r.reshape(s) for r, s in zip(res, shapes)]


def exchange_start(srcs, dst_shapes, dst_dtype, plan, count, after, *, name):
    n, m = len(srcs), len(dst_shapes)
    srcs = [pltpu.with_memory_space_constraint(s, pltpu.HBM) for s in srcs]
    lands = [pltpu.with_memory_space_constraint(lax.empty(s, dst_dtype), pltpu.HBM) for s in dst_shapes]

    def body(*refs):
        send, recv = refs[n + m + 1], refs[n + m + 2]
        src, dst, token = refs[n + m + 3:2 * n + m + 3], refs[2 * n + m + 3:2 * (n + m) + 3], refs[2 * (n + m) + 3]
        for i, (s, d, dev) in enumerate(plan(_place(), src, dst)):
            pltpu.make_async_remote_copy(s, d, send.at[i], recv.at[i], device_id=dev, device_id_type=MESH).start()
        token[...] = jnp.zeros_like(token)

    res = _split_call(
        body, name=name,
        out_shape=(pltpu.SemaphoreType.DMA((count,)), pltpu.SemaphoreType.DMA((count,)),
                   *[pltpu.HBM(s.shape, s.dtype) for s in srcs], *[pltpu.HBM(s, dst_dtype) for s in dst_shapes],
                   _token_shape()),
        in_specs=[HBM] * (n + m) + [ANY],
        out_specs=(SEM, SEM, *[HBM] * (n + m), pl.BlockSpec(memory_space=pltpu.VMEM)),
        aliases={i: 2 + i for i in range(n + m)})(*srcs, *lands, after)
    return (list(res[2:2 + n]), list(res[2 + n:2 + n + m]), res[0], res[1], plan), res[2 + n + m]


def exchange_finish(state, after, *, name):
    srcs, lands, send, recv, plan = state
    n, m = len(srcs), len(lands)

    def body(*refs):
        send, recv = refs[n + m], refs[n + m + 1]
        src, dst = refs[n + m + 3:2 * n + m + 3], refs[2 * n + m + 3:]
        for i, (s, d, dev) in enumerate(plan(_place(), src, dst)):
            pltpu.make_async_remote_copy(s, d, send.at[i], recv.at[i], device_id=dev, device_id_type=MESH).wait()

    res = _split_call(
        body, name=name,
        out_shape=tuple(pltpu.HBM(a.shape, a.dtype) for a in srcs + lands),
        in_specs=[HBM] * (n + m) + [SEM, SEM, ANY], out_specs=tuple([HBM] * (n + m)),
        aliases={i: i for i in range(n + m)})(*srcs, *lands, send, recv, after)
    return list(res[:n]), list(res[n:])


def pair_plan(place, src, dst):
    x, y, c, _, _ = place
    return [(s.at[:, 1 - c], d, (x, y, 1 - c)) for s, d in zip(src, dst)]


def chip_plan(place, src, dst):
    x, y, c, _, others = place
    return [(s.at[2 * ox + oy], d.at[k], (ox, oy, c)) for s, d in zip(src, dst) for k, (ox, oy) in enumerate(others)]


def pair_exchange(grads, *, name):
    n = len(grads)

    def body(*refs):
        src, got = refs[:n], refs[n:2 * n]
        send, recv = refs[2 * n:]
        x, y, c, _, _ = _place()

        def swap(l):
            return pltpu.make_async_remote_copy(src[l].at[:, 1 - c], got[l], send.at[l], recv.at[l],
                                                device_id=(x, y, 1 - c), device_id_type=MESH)

        for l in range(n):
            swap(l).start()
        for l in range(n):
            swap(l).wait()

    res = _pcall(body, name=name, out_shape=[_sds((N_SHARDS,) + g.shape[2:], F32) for g in grads],
                 in_specs=[ANY] * n, out_specs=[ANY] * n,
                 scratch_shapes=[pltpu.SemaphoreType.DMA((n,)), pltpu.SemaphoreType.DMA((n,))],
                 side_effects=True)(*grads)
    return list(res)


def add_to_wire(mine, theirs, core, *, name, tm=256):
    s, _, r, c = mine.shape
    tm = min(tm, r)

    def body(core_ref, a_ref, b_ref, o_ref):
        o_ref[...] = (a_ref[...] + b_ref[...]).astype(BF16)

    spec = pl.BlockSpec((None, tm, c), lambda i, j, cr: (i, j, 0))
    return _pcall(body, name=name, out_shape=_sds((s, r, c), BF16), grid=(s, r // tm), num_prefetch=1,
                  in_specs=[pl.BlockSpec((None, None, tm, c), lambda i, j, cr: (i, cr[0], j, 0)), spec],
                  out_specs=spec, semantics=("parallel", "parallel"))(core, mine, theirs)


def sum_chips(wire, landed, place, dest, layer, n_layers, *, name, tm=256):
    _, r, c = wire.shape
    tm = min(tm, r)

    def body(place_ref, w_ref, l_ref, *rest):
        o_ref = rest[-1]
        o_ref[...] = ((w_ref[...].astype(F32) + l_ref[0].astype(F32)) + l_ref[1].astype(F32)) + l_ref[2].astype(F32)

    in_specs = [pl.BlockSpec((None, tm, c), lambda i, pr: (pr[0], i, 0)),
                pl.BlockSpec((3, tm, c), lambda i, pr: (0, i, 0))]
    args = [place, wire, landed]
    aliases = None
    if dest is not None:
        in_specs.append(ANY)
        args.append(dest)
        aliases = {3: 0}
    return _pcall(body, name=name, out_shape=_sds((n_layers, 2, r, c), F32), grid=(r // tm,), num_prefetch=1,
                  in_specs=in_specs,
                  out_specs=pl.BlockSpec((None, None, tm, c), lambda i, pr: (layer, pr[1], i, 0)),
                  aliases=aliases, semantics=("parallel",))(*args)


def pair_share(bufs, slots, *, name):
    n = len(bufs)

    def body(*refs):
        out = refs[n:2 * n]
        send, recv = refs[2 * n:]
        x, y, c, _, _ = _place()

        def share(i, half):
            o, l = slots[i]
            return pltpu.make_async_remote_copy(out[o].at[l, half], out[o].at[l, half], send.at[i], recv.at[i],
                                                device_id=(x, y, 1 - c), device_id_type=MESH)

        for i in range(len(slots)):
            share(i, c).start()
        for i in range(len(slots)):
            share(i, 1 - c).wait_recv()
            share(i, c).wait_send()

    res = _pcall(body, name=name, out_shape=[_sds(b.shape, F32) for b in bufs], in_specs=[ANY] * n,
                 out_specs=[ANY] * n,
                 scratch_shapes=[pltpu.SemaphoreType.DMA((len(slots),)), pltpu.SemaphoreType.DMA((len(slots),))],
                 aliases={o: o for o in range(n)}, side_effects=True)(*bufs)
    return list(res)


def all_reduce_small(packed, *, name):
    n_dev, r, c = packed.shape

    def body(in_ref, out_ref, land, send, recv):
        x, y, cc, _, _ = _place()
        me = 4 * x + 2 * y + cc
        peers = [(px, py, pc) for px in range(2) for py in range(2) for pc in range(2)]

        def scatter(d):
            return pltpu.make_async_remote_copy(in_ref.at[d], land.at[me], send.at[0, d], recv.at[0, me],
                                                device_id=peers[d], device_id_type=MESH)

        def gather(d):
            return pltpu.make_async_remote_copy(out_ref.at[me], out_ref.at[me], send.at[1, d], recv.at[1, me],
                                                device_id=peers[d], device_id_type=MESH)

        for d in range(n_dev):
            @pl.when(d != me)
            def _():
                scatter(d).start()
        land[me] = in_ref[me]
        for d in range(n_dev):
            @pl.when(d != me)
            def _():
                pltpu.make_async_remote_copy(in_ref.at[d], land.at[d], send.at[0, d], recv.at[0, d],
                                             device_id=peers[d], device_id_type=MESH).wait_recv()
        total = land[0]
        for d in range(1, n_dev):
            total = total + land[d]
        out_ref[me] = total
        for d in range(n_dev):
            @pl.when(d != me)
            def _():
                gather(d).start()
        for d in range(n_dev):
            @pl.when(d != me)
            def _():
                pltpu.make_async_remote_copy(out_ref.at[d], out_ref.at[d], send.at[1, d], recv.at[1, d],
                                             device_id=peers[d], device_id_type=MESH).wait_recv()
        for d in range(n_dev):
            @pl.when(d != me)
            def _():
                scatter(d).wait_send()
                gather(d).wait_send()

    vm = pl.BlockSpec(memory_space=pltpu.VMEM)
    return _pcall(body, name=name, out_shape=_sds(packed.shape, F32), in_specs=[vm], out_specs=vm,
                  scratch_shapes=[pltpu.VMEM(packed.shape, F32), pltpu.SemaphoreType.DMA((2, n_dev)),
                                  pltpu.SemaphoreType.DMA((2, n_dev))],
                  side_effects=True)(packed)


def adamw(w, g, m, v, *, name, part=None, dest=None, tm=256):
    shape = w.shape
    cols = shape[-1]
    rows = 1
    for s in shape[:-1]:
        rows *= s
    first, count = 0, rows
    if part is not None:
        count = rows // part[1]
        first = part[0] * count
    tm = min(tm, count)
    assert count % tm == 0
    two_d = lambda a: a.reshape(rows, cols)

    def body(w_ref, g_ref, m_ref, v_ref, *rest):
        d_ref, mo_ref, vo_ref = rest[-3:]
        gv = g_ref[...]
        m_new = ADAM_B1 * m_ref[...] + (1.0 - ADAM_B1) * gv
        v_new = ADAM_B2 * v_ref[...] + (1.0 - ADAM_B2) * (gv * gv)
        m_hat = m_new / (1.0 - ADAM_B1 ** ADAM_STEP)
        v_hat = v_new / (1.0 - ADAM_B2 ** ADAM_STEP)
        d_ref[...] = -ADAM_LR * (m_hat / (jnp.sqrt(v_hat) + ADAM_EPS) + ADAM_WD * w_ref[...])
        mo_ref[...] = m_new
        vo_ref[...] = v_new

    spec = pl.BlockSpec((tm, cols), lambda i: (first // tm + i, 0))
    args = [two_d(w), two_d(g), two_d(m), two_d(v)]
    in_specs = [spec] * 4
    aliases = None
    if dest is not None:
        args += [two_d(d) for d in dest]
        in_specs = in_specs + [ANY] * 3
        aliases = {4: 0, 5: 1, 6: 2}
    outs = _pcall(body, name=name, out_shape=[_sds((rows, cols), F32)] * 3, grid=(count // tm,), in_specs=in_specs,
                  out_specs=[spec] * 3, aliases=aliases, semantics=("parallel",))(*args)
    return [o.reshape(shape) for o in outs]


WEIGHTS = ("ln_mix_a", "w_in_a", "g_v_a", "w_spatial", "b_spatial", "w_out_a", "ln_kv", "w_kv", "g_k", "ln_mix_b",
           "w_q", "g_q", "w_out_b", "ln_mlp", "w_up", "w_down", "ln_ple", "w_ple_gate", "w_ple_proj")
MATRICES = (("w_in_a", 1, True), ("w_out_a", 1, False), ("w_kv", 0, True), ("w_q", 1, False), ("w_out_b", 1, False),
            ("w_up", 2, True), ("w_down", 2, False), ("w_ple_gate", 2, False), ("w_ple_proj", 2, True))
FIRST_LAYER_STAGES = (("w_in_a",), ("w_out_a",), ("w_up",), ("w_down",), ("w_ple_gate", "w_ple_proj", "w_kv"))
FIRST_LAYER = tuple(name for stage in FIRST_LAYER_STAGES for name in stage)
REPLICATED = ("w_spatial", "b_spatial", "ln_kv", "g_k", "ln_mix_b", "g_q", "ln_mlp", "ln_ple")
SHARDED_VECTORS = ("ln_mix_a", "g_v_a")
SMALL_ROWS = 18


def kernel(x, p, ln_mix_a, w_in_a, g_v_a, w_spatial, b_spatial, w_out_a, ln_kv, w_kv, g_k, ln_mix_b, w_q, g_q, w_out_b, ln_mlp, w_up, w_down, ln_ple, w_ple_gate, w_ple_proj, loss_target, m_ln_mix_a, m_w_in_a, m_g_v_a, m_w_spatial, m_b_spatial, m_w_out_a, m_ln_kv, m_w_kv, m_g_k, m_ln_mix_b, m_w_q, m_g_q, m_w_out_b, m_ln_mlp, m_w_up, m_w_down, m_ln_ple, m_w_ple_gate, m_w_ple_proj, v_ln_mix_a, v_w_in_a, v_g_v_a, v_w_spatial, v_b_spatial, v_w_out_a, v_ln_kv, v_w_kv, v_g_k, v_ln_mix_b, v_w_q, v_g_q, v_w_out_b, v_ln_mlp, v_w_up, v_w_down, v_ln_ple, v_w_ple_gate, v_w_ple_proj):
    given = dict(locals())
    weights = {n: given[n] for n in WEIGHTS}
    shard = 2 * lax.axis_index("x") + lax.axis_index("y")
    core = lax.axis_index("c")
    shard_1 = shard.astype(jnp.int32).reshape(1)
    core_1 = core.astype(jnp.int32).reshape(1)
    place = jnp.stack([shard, core]).astype(jnp.int32)

    leaves = []
    for name, layers, cols in MATRICES:
        w3 = weights[name] if layers else weights[name][None]
        for layer in range(max(layers, 1)):
            leaves.append((name, layer, cols, cast_into_slot(w3, layer, shard_1, name=f"cast_{name}_{layer}")))
    first = [lf for stage in FIRST_LAYER_STAGES for name in stage for lf in leaves if lf[0] == name and lf[1] == 0]
    second = [lf for lf in leaves if not (lf[0] in FIRST_LAYER and lf[1] == 0)]
    _, vec_a = gather_shards([], [ln_mix_a, g_v_a], name="gather_vectors")
    send_0, recv_0, flying_0, token_0a = gather_start([lf[3] for lf in first], vec_a[0], name="gather_layer0_start")
    send_a, recv_a, flying, token = gather_start([lf[3] for lf in second], token_0a, name="gather_layer1_start")

    def assemble(leaf_list, arrays):
        full = {}
        for (name, layer, cols, _), arr in zip(leaf_list, arrays):
            if not cols:
                arr = arr.reshape(N_SHARDS * arr.shape[1], arr.shape[2])
            full.setdefault(name, {})[layer] = arr
        return full

    w = {"ln_mix_a": vec_a[0].reshape(1, D_MODEL) + token[0, 0],
         "g_v_a": vec_a[1].reshape(1, D_MODEL)}
    for name in REPLICATED:
        w[name] = weights[name]

    class Late:
        def __init__(self):
            self.full_a = {}

        def first_layer_weights(self, name, after):
            if name in self.full_a:
                return {}
            stage = [name in s for s in FIRST_LAYER_STAGES].index(True)
            base = sum(len(s) for s in FIRST_LAYER_STAGES[:stage])
            members = first[base:base + len(FIRST_LAYER_STAGES[stage])]
            bufs, send_b, recv_b, tok = gather_pass_on(flying_0[base:base + len(members)], send_0, recv_0, after,
                                                       name=f"gather_layer0_pass_on_{stage}", base=base)
            got = gather_finish(bufs, send_b, recv_b, tok, [lf[3].shape for lf in members],
                                name=f"gather_layer0_finish_{stage}")
            self.full_a.update(assemble(members, got))
            return {n: (self.full_a[n][0] if n == "w_kv" else (self.full_a[n][0],)) for n in FIRST_LAYER_STAGES[stage]}

        def after_first_layer(self, x_done):
            self.passed = gather_pass_on(flying, send_a, recv_a, x_done, name="gather_layer1_pass_on")

        def second_layer_weights(self, k_done):
            bufs, send_b, recv_b, _ = self.passed
            got_b = gather_finish(bufs, send_b, recv_b, k_done, [lf[3].shape for lf in second],
                                  name="gather_layer1_finish")
            full_b = assemble(second, got_b)
            out = {}
            for name, layers, _ in MATRICES:
                if name in full_b:
                    both = {**self.full_a.get(name, {}), **full_b[name]}
                    out[name] = tuple(both[l] for l in sorted(both))
            return out

        def second_layer_grads(self, grads_late, dx_done):
            self.keys = sorted(grads_late)
            views = [view(k, grads_late[k]) for k in self.keys]
            self.pair, token = exchange_start(views, [(N_SHARDS,) + v.shape[2:] for v in views], F32, pair_plan,
                                              len(views), dx_done, name="grad_pair_start_1")
            return token

        def first_ple_backward_done(self, dx_done):
            mine, theirs = exchange_finish(self.pair, dx_done, name="grad_pair_finish_1")
            wire = [add_to_wire(a, b, core_1, name=f"grad_pair_sum_1_{i}") for i, (a, b) in enumerate(zip(mine, theirs))]
            self.chip, token = exchange_start(wire, [(3,) + v.shape[1:] for v in wire], BF16, chip_plan, 3 * len(wire),
                                              wire[-1], name="grad_chip_start_1")
            return token

    col_sharded = {name: cols for name, _, cols in MATRICES}
    layer_count = {name: max(layers, 1) for name, layers, _ in MATRICES}

    def view(key, arr):
        rows = arr.shape[-2] if col_sharded[key[0]] else arr.shape[0] // N_SHARDS
        return arr.reshape(N_SHARDS, 2, rows // 2, arr.shape[-1])

    t = x.shape[1]
    late = Late()
    loss_blk, dx, g = local_step(x[0], p.reshape(2, t, PLE_DIM), loss_target[0], w, late)
    loss = lax.psum(loss_blk[0, 0], ("x", "y", "c"))

    keys_early = [(name, layer) for name, layers, _ in MATRICES for layer in range(max(layers, 1))
                  if (name, layer) not in late.keys]
    views = [view(k, g[k[0]][k[1]] if layer_count[k[0]] == 2 else g[k[0]]) for k in keys_early]
    theirs = pair_exchange(views, name="grad_pair_exchange_0")
    wire_0 = [add_to_wire(a, b, core_1, name=f"grad_pair_sum_0_{i}") for i, (a, b) in enumerate(zip(views, theirs))]

    grads = {}
    small = REPLICATED + SHARDED_VECTORS
    flat = jnp.concatenate([g[n].reshape(-1) for n in small])
    room = 8 * SMALL_ROWS * D_MODEL
    flat = jnp.concatenate([flat, jnp.zeros((room - flat.shape[0],), F32)])
    reduced = all_reduce_small(flat.reshape(8, SMALL_ROWS, D_MODEL), name="grad_small_all_reduce").reshape(-1)
    at = 0
    for n in small:
        size = g[n].size
        piece = reduced[at:at + size]
        at += size
        if n in SHARDED_VECTORS:
            per = D_MODEL // N_SHARDS
            grads[n] = lax.dynamic_slice(piece, (shard * per,), (per,)).reshape(weights[n].shape)
        else:
            grads[n] = piece.reshape(weights[n].shape)

    chip_0, token_0 = exchange_start(wire_0, [(3,) + v.shape[1:] for v in wire_0], BF16, chip_plan, 3 * len(wire_0),
                                     reduced, name="grad_chip_start_0")

    bufs = {}

    def sum_and_share(keys, wire, landed, tag):
        for i, (key, wv, lv) in enumerate(zip(keys, wire, landed)):
            name, layer = key
            bufs[name] = sum_chips(wv, lv, place, bufs.get(name), layer, layer_count[name],
                                   name=f"grad_chip_sum_{tag}_{i}")
        names = sorted({k[0] for k in keys})
        shared = pair_share([bufs[n] for n in names], [(names.index(k[0]), k[1]) for k in keys],
                            name=f"grad_pair_share_{tag}")
        bufs.update(zip(names, shared))

    wire_1, landed_1 = exchange_finish(late.chip, token_0, name="grad_chip_finish_1")
    sum_and_share(late.keys, wire_1, landed_1, 1)

    updates = {}

    def update(n, gn, part=None):
        wn, mn, vn = weights[n], given["m_" + n], given["v_" + n]
        if wn.ndim == 1:
            wn, gn, mn, vn = (a.reshape(1, -1) for a in (wn, gn, mn, vn))
        tag = "" if part is None else f"_{part[0]}"
        updates[n] = adamw(wn, gn.reshape(wn.shape), mn, vn, name=f"adamw_{n}{tag}", part=part, dest=updates.get(n))

    for n in small:
        update(n, grads[n])
    for name, layer in late.keys:
        update(name, bufs[name], (layer, layer_count[name]) if layer_count[name] == 2 else None)

    wire_0, landed_0 = exchange_finish(chip_0, updates[late.keys[-1][0]][0], name="grad_chip_finish_0")
    sum_and_share(keys_early, wire_0, landed_0, 0)
    for name, layer in keys_early:
        update(name, bufs[name], (layer, layer_count[name]) if layer_count[name] == 2 else None)
    for name, _, _ in MATRICES:
        grads[name] = bufs[name].reshape(weights[name].shape)
    delta = {n: updates[n][0].reshape(weights[n].shape) for n in WEIGHTS}
    new_m = {n: updates[n][1].reshape(weights[n].shape) for n in WEIGHTS}
    new_v = {n: updates[n][2].reshape(weights[n].shape) for n in WEIGHTS}
    return (loss, dx.reshape(x.shape), *[grads[n] for n in WEIGHTS], *[delta[n] for n in WEIGHTS],
            *[new_m[n] for n in WEIGHTS], *[new_v[n] for n in WEIGHTS])
```

```python
import jax
import jax.numpy as jnp
from jax import lax
from jax.experimental import pallas as pl
from jax.experimental.pallas import tpu as pltpu

F32 = jnp.float32
BF16 = jnp.bfloat16

D_MODEL = 1024
D_FF = 4096
PLE_DIM = 256
N_GROUPS = 8
CHUNK = 128
HEAD_DIM = 64
LANES = 128
ATT_BLOCK = 256
EPS = 1e-6
N_SHARDS = 4
VMEM_LIMIT = 56 * 1024 * 1024

ADAM_LR = 0.001
ADAM_B1 = 0.9
ADAM_B2 = 0.999
ADAM_EPS = 1e-08
ADAM_WD = 0.01
ADAM_STEP = 10

MESH = pl.DeviceIdType.MESH


def _pcall(body, *, name, out_shape, grid=None, in_specs=None, out_specs=None, scratch_shapes=(),
           semantics=None, aliases=None, side_effects=False, num_prefetch=0):
    params = dict(vmem_limit_bytes=VMEM_LIMIT)
    if semantics is not None:
        params["dimension_semantics"] = semantics
    if side_effects:
        params["has_side_effects"] = True
    kwargs = {}
    if aliases:
        kwargs["input_output_aliases"] = aliases
    if num_prefetch:
        spec = pltpu.PrefetchScalarGridSpec(num_scalar_prefetch=num_prefetch, grid=grid, in_specs=in_specs,
                                            out_specs=out_specs, scratch_shapes=list(scratch_shapes))
        return pl.pallas_call(body, name=name, out_shape=out_shape, grid_spec=spec,
                              compiler_params=pltpu.CompilerParams(**params), **kwargs)
    if grid is not None:
        kwargs["grid"] = grid
    if in_specs is not None:
        kwargs["in_specs"] = in_specs
    if out_specs is not None:
        kwargs["out_specs"] = out_specs
    if aliases:
        kwargs["input_output_aliases"] = aliases
    return pl.pallas_call(body, name=name, out_shape=out_shape, scratch_shapes=list(scratch_shapes),
                          compiler_params=pltpu.CompilerParams(**params), **kwargs)


def _sds(shape, dtype):
    return jax.ShapeDtypeStruct(shape, dtype)


_GELU_C = 0.7978845608028654
_GELU_A = 0.044715


def _gelu(x):
    inner = _GELU_C * (x + _GELU_A * (x * x * x))
    return 0.5 * x * (1.0 + jnp.tanh(inner))


def _gelu_grad(x):
    x2 = x * x
    t = jnp.tanh(_GELU_C * (x + _GELU_A * (x2 * x)))
    return 0.5 * (1.0 + t) + 0.5 * x * (1.0 - t * t) * (_GELU_C * (1.0 + 3.0 * _GELU_A * x2))


def _sigmoid(x):
    return 1.0 / (1.0 + jnp.exp(-x))


def _log_sigmoid(z):
    return jnp.minimum(z, 0.0) - jnp.log(1.0 + jnp.exp(-jnp.abs(z)))


def _dot(a, b):
    return jnp.dot(a, b, preferred_element_type=F32)


def _dot_nt(a, b):
    return lax.dot_general(a, b, (((1,), (1,)), ((), ())), preferred_element_type=F32)


def _dot_tn(a, b):
    return lax.dot_general(a, b, (((0,), (0,)), ((), ())), preferred_element_type=F32)


def _head_rstd(x):
    lane = lax.broadcasted_iota(jnp.int32, x.shape, 1)
    low = lane < HEAD_DIM
    sq = x * x
    s_lo = jnp.sum(jnp.where(low, sq, 0.0), axis=-1, keepdims=True)
    s_hi = jnp.sum(jnp.where(low, 0.0, sq), axis=-1, keepdims=True)
    ms = jnp.where(low, s_lo, s_hi) * (1.0 / HEAD_DIM)
    return lax.rsqrt(ms + EPS)


def _head_mean(x):
    lane = lax.broadcasted_iota(jnp.int32, x.shape, 1)
    low = lane < HEAD_DIM
    s_lo = jnp.sum(jnp.where(low, x, 0.0), axis=-1, keepdims=True)
    s_hi = jnp.sum(jnp.where(low, 0.0, x), axis=-1, keepdims=True)
    return jnp.where(low, s_lo, s_hi) * (1.0 / HEAD_DIM)


def _full(shape):
    zeros = (0,) * len(shape)
    return pl.BlockSpec(shape, lambda i: zeros)


def norm_matmul(x, g, w, *, name, epilogue="none", tm=512):
    t, d = x.shape
    sharded = w.ndim == 3
    per = w.shape[2] if sharded else w.shape[1]
    n = N_SHARDS * per if sharded else per
    tm = min(tm, t)

    def body(x_ref, g_ref, w_ref, h_ref, r_ref, *outs):
        xv = x_ref[...]
        r = lax.rsqrt(jnp.mean(xv * xv, axis=-1, keepdims=True) + EPS)
        h = ((xv * r) * g_ref[...]).astype(BF16)
        h_ref[...] = h
        r_ref[...] = r
        for s in range(N_SHARDS if sharded else 1):
            cols = slice(s * per, (s + 1) * per)
            y = _dot(h, w_ref[s] if sharded else w_ref[...])
            if epilogue == "none":
                outs[0][:, cols] = y
            else:
                a = jnp.maximum(y, 0.0)
                outs[0][:, cols] = a.astype(BF16)
                outs[1][:, cols] = (a * a).astype(BF16)

    row = lambda i: (i, 0)
    out_shape = [_sds((t, d), BF16), _sds((t, 1), F32)]
    out_specs = [pl.BlockSpec((tm, d), row), pl.BlockSpec((tm, 1), row)]
    if epilogue == "none":
        out_shape.append(_sds((t, n), F32))
        out_specs.append(pl.BlockSpec((tm, n), row))
    else:
        out_shape += [_sds((t, n), BF16), _sds((t, n), BF16)]
        out_specs += [pl.BlockSpec((tm, n), row)] * 2
    return _pcall(
        body, name=name, out_shape=out_shape, grid=(t // tm,),
        in_specs=[pl.BlockSpec((tm, d), row), _full((1, d)), _full(w.shape)],
        out_specs=out_specs, semantics=("parallel",))(x, g, w)


def matmul_residual(a, w, res, *, name, tm=512):
    t, k = a.shape
    n = w.shape[1]
    tm = min(tm, t)

    def body(a_ref, w_ref, res_ref, o_ref):
        o_ref[...] = res_ref[...] + _dot(a_ref[...], w_ref[...])

    row = lambda i: (i, 0)
    return _pcall(
        body, name=name, out_shape=_sds((t, n), F32), grid=(t // tm,),
        in_specs=[pl.BlockSpec((tm, k), row), _full(w.shape), pl.BlockSpec((tm, n), row)],
        out_specs=pl.BlockSpec((tm, n), row), semantics=("parallel",))(a, w, res)


def ple_forward(x, g, w_gate, p, w_proj, *, name, tm=256):
    t, d = x.shape
    tm = min(tm, t)

    def body(x_ref, g_ref, wg_ref, p_ref, wp_ref, h_ref, r_ref, gate_ref, pp_ref, o_ref):
        xv = x_ref[...]
        r = lax.rsqrt(jnp.mean(xv * xv, axis=-1, keepdims=True) + EPS)
        h = ((xv * r) * g_ref[...]).astype(BF16)
        h_ref[...] = h
        r_ref[...] = r
        gate = _sigmoid(_dot(h, wg_ref[...]))
        gate_ref[...] = gate
        pb = p_ref[...].astype(BF16)
        per = d // N_SHARDS
        for s in range(N_SHARDS):
            cols = slice(s * per, (s + 1) * per)
            pp = _dot(pb, wp_ref[s])
            pp_ref[:, cols] = pp.astype(BF16)
            o_ref[:, cols] = xv[:, cols] + pp * gate[:, cols]

    row = lambda i: (i, 0)
    fixed = lambda i: (0, 0)
    return _pcall(
        body, name=name,
        out_shape=[_sds((t, d), BF16), _sds((t, 1), F32), _sds((t, d), F32), _sds((t, d), BF16), _sds((t, d), F32)],
        grid=(t // tm,),
        in_specs=[pl.BlockSpec((tm, d), row), pl.BlockSpec((1, d), fixed), pl.BlockSpec((d, d), fixed),
                  pl.BlockSpec((tm, PLE_DIM), row),
                  pl.BlockSpec((N_SHARDS, PLE_DIM, d // N_SHARDS), lambda i: (0, 0, 0))],
        out_specs=[pl.BlockSpec((tm, d), row), pl.BlockSpec((tm, 1), row), pl.BlockSpec((tm, d), row),
                   pl.BlockSpec((tm, d), row), pl.BlockSpec((tm, d), row)],
        semantics=("parallel",))(x, g, w_gate, p, w_proj)


def _tril_mask():
    r = lax.broadcasted_iota(jnp.int32, (CHUNK, CHUNK), 0)
    c = lax.broadcasted_iota(jnp.int32, (CHUNK, CHUNK), 1)
    return c <= r


def _sgu_common(pre_ref, gv_ref, ws_ref):
    pre = pre_ref[...]
    pre_u, pre_v = pre[:, :D_MODEL], pre[:, D_MODEL:]
    u = _gelu(pre_u)
    v = _gelu(pre_v)
    r = lax.rsqrt(jnp.mean(v * v, axis=-1, keepdims=True) + EPS)
    vhat = v * r
    vn = (vhat * gv_ref[...]).astype(BF16)
    tril = _tril_mask()
    wm = [jnp.where(tril, ws_ref[g], 0.0).astype(BF16) for g in range(N_GROUPS)]
    return pre_u, pre_v, u, r, vhat, vn, wm, tril


def sgu_forward(pre, g_v, w_s, b_full, *, name):
    t = pre.shape[0]

    def body(pre_ref, gv_ref, ws_ref, b_ref, y_ref):
        _, _, u, _, _, vn, wm, _ = _sgu_common(pre_ref, gv_ref, ws_ref)
        for g in range(N_GROUPS):
            cols = slice(g * LANES, (g + 1) * LANES)
            mix = _dot(wm[g], vn[:, cols]) + b_ref[:, cols]
            y_ref[:, cols] = (u[:, cols] * mix).astype(BF16)

    return _pcall(
        body, name=name, out_shape=_sds((t, D_MODEL), BF16), grid=(t // CHUNK,),
        in_specs=[pl.BlockSpec((CHUNK, 2 * D_MODEL), lambda i: (i, 0)), pl.BlockSpec((1, D_MODEL), lambda i: (0, 0)),
                  pl.BlockSpec((N_GROUPS, CHUNK, CHUNK), lambda i: (0, 0, 0)),
                  pl.BlockSpec((CHUNK, D_MODEL), lambda i: (0, 0))],
        out_specs=pl.BlockSpec((CHUNK, D_MODEL), lambda i: (i, 0)),
        semantics=("parallel",))(pre, g_v, w_s, b_full)


def head_norm(pre, g128, *, name, col_block=0, scale=1.0, passthrough=False, tm=512):
    t = pre.shape[0]
    tm = min(tm, t)

    def body(*refs):
        if passthrough:
            x_ref, v_ref, g_ref, o_ref, vo_ref = refs
            vo_ref[...] = v_ref[...].astype(BF16)
        else:
            x_ref, g_ref, o_ref = refs
        g = g_ref[...] * scale
        for b in range(D_MODEL // LANES):
            cols = slice(b * LANES, (b + 1) * LANES)
            xv = x_ref[:, cols]
            o_ref[:, cols] = ((xv * _head_rstd(xv)) * g).astype(BF16)

    x_spec = pl.BlockSpec((tm, D_MODEL), lambda i: (i, col_block))
    g_spec = pl.BlockSpec((1, LANES), lambda i: (0, 0))
    o_spec = pl.BlockSpec((tm, D_MODEL), lambda i: (i, 0))
    if passthrough:
        return _pcall(body, name=name, out_shape=[_sds((t, D_MODEL), BF16)] * 2, grid=(t // tm,),
                      in_specs=[x_spec, pl.BlockSpec((tm, D_MODEL), lambda i: (i, 1)), g_spec],
                      out_specs=[o_spec, o_spec], semantics=("parallel",))(pre, pre, g128)
    return _pcall(body, name=name, out_shape=_sds((t, D_MODEL), BF16), grid=(t // tm,),
                  in_specs=[x_spec, g_spec], out_specs=o_spec, semantics=("parallel",))(pre, g128)


def _suffix_matrix(n):
    r = lax.broadcasted_iota(jnp.int32, (n, n), 0)
    c = lax.broadcasted_iota(jnp.int32, (n, n), 1)
    return jnp.where(r > c, 1.0, 0.0).astype(BF16)


def _prefix_matrix(n):
    r = lax.broadcasted_iota(jnp.int32, (n, n), 0)
    c = lax.broadcasted_iota(jnp.int32, (n, n), 1)
    return jnp.where(r < c, 1.0, 0.0).astype(BF16)


def _block_cumsum(a, tri):
    return _dot(a.astype(BF16), tri)


def _stacked_causal(n):
    r = lax.broadcasted_iota(jnp.int32, (2 * n, n), 0)
    c = lax.broadcasted_iota(jnp.int32, (2 * n, n), 1)
    return c < jnp.where(r >= n, r - n, r)


def _stack_heads(a, low):
    zero = jnp.zeros_like(a)
    return jnp.concatenate([jnp.where(low, a, zero), jnp.where(low, zero, a)], axis=0)


def stick_breaking_forward(q, k, v, *, name):
    t = q.shape[0]
    blk = min(ATT_BLOCK, t)
    nq = t // blk

    def body(q_ref, k_ref, v_ref, o_ref):
        i = pl.program_id(1)
        low = lax.broadcasted_iota(jnp.int32, (blk, LANES), 1) < HEAD_DIM
        tri = _suffix_matrix(blk)
        causal = _stacked_causal(blk)
        qs = _stack_heads(q_ref[...], low)

        def block(j, carry, acc, masked):
            rows = pl.ds(pl.multiple_of(j * blk, blk), blk)
            z = _dot_nt(qs, k_ref[rows, :])
            ls = _log_sigmoid(z)
            lg = ls - z
            if masked:
                lg = jnp.where(causal, lg, 0.0)
            s = ls + _block_cumsum(lg, tri) + carry
            a = jnp.exp(s)
            if masked:
                a = jnp.where(causal, a, 0.0)
            acc = acc + _dot(a.astype(BF16), v_ref[rows, :])
            return carry + jnp.sum(lg, axis=-1, keepdims=True), acc

        state = block(i, jnp.zeros((2 * blk, 1), F32), jnp.zeros((2 * blk, LANES), F32), True)

        def two_blocks(n, st):
            st = block(i - 1 - 2 * n, st[0], st[1], False)
            return block(i - 2 - 2 * n, st[0], st[1], False)

        state = lax.fori_loop(0, i // 2, two_blocks, state)
        _, acc = lax.fori_loop(0, i % 2, lambda n, st: block(0, st[0], st[1], False), state)
        o_ref[...] = jnp.where(low, acc[:blk], acc[blk:]).astype(BF16)

    return _pcall(
        body, name=name, out_shape=_sds((t, D_MODEL), BF16), grid=(D_MODEL // LANES, nq),
        in_specs=[pl.BlockSpec((blk, LANES), lambda p, i: (i, p)), pl.BlockSpec((t, LANES), lambda p, i: (0, p)),
                  pl.BlockSpec((t, LANES), lambda p, i: (0, p))],
        out_specs=pl.BlockSpec((blk, LANES), lambda p, i: (i, p)),
        semantics=("parallel", "arbitrary"))(q, k, v)


def loss_forward(x, target, *, name, tm=512):
    t, d = x.shape
    tm = min(tm, t)

    def body(x_ref, t_ref, l_ref, dx_ref):
        @pl.when(pl.program_id(0) == 0)
        def _():
            l_ref[...] = jnp.zeros_like(l_ref)

        diff = x_ref[...] - t_ref[...]
        dx_ref[...] = diff * (1.0 / d)
        l_ref[...] += 0.5 * jnp.sum(jnp.mean(diff * diff, axis=-1, keepdims=True))

    return _pcall(
        body, name=name, out_shape=[_sds((8, LANES), F32), _sds((t, d), F32)], grid=(t // tm,),
        in_specs=[pl.BlockSpec((tm, d), lambda i: (i, 0))] * 2,
        out_specs=[pl.BlockSpec((8, LANES), lambda i: (0, 0)), pl.BlockSpec((tm, d), lambda i: (i, 0))],
        semantics=("arbitrary",))(x, target)


def matmul_nt(dy, w, *, name, mul=None, out_dtype=F32, tm=512):
    t, n = dy.shape
    k = w.shape[0]
    tm = min(tm, t)

    def body(*refs):
        if mul is None:
            dy_ref, w_ref, o_ref = refs
        else:
            dy_ref, w_ref, m_ref, o_ref = refs
        y = _dot_nt(dy_ref[...].astype(BF16), w_ref[...])
        if mul is not None:
            y = y * (2.0 * m_ref[...].astype(F32))
        o_ref[...] = y.astype(out_dtype)

    row = lambda i: (i, 0)
    in_specs = [pl.BlockSpec((tm, n), row), _full(w.shape)]
    args = [dy, w]
    if mul is not None:
        in_specs.append(pl.BlockSpec((tm, k), row))
        args.append(mul)
    return _pcall(body, name=name, out_shape=_sds((t, k), out_dtype), grid=(t // tm,), in_specs=in_specs,
                  out_specs=pl.BlockSpec((tm, k), row), semantics=("parallel",))(*args)


def matmul_tn(a, dy, *, name, col_shards, tk=512):
    t, k = a.shape
    n = dy.shape[1]
    if col_shards:
        tn = n // N_SHARDS

        def body(a_ref, dy_ref, o_ref):
            o_ref[...] = _dot_tn(a_ref[...].astype(BF16), dy_ref[...].astype(BF16))

        return _pcall(body, name=name, out_shape=_sds((N_SHARDS, k, tn), F32), grid=(N_SHARDS,),
                      in_specs=[_full((t, k)), pl.BlockSpec((t, tn), lambda j: (0, j))],
                      out_specs=pl.BlockSpec((None, k, tn), lambda j: (j, 0, 0)), semantics=("parallel",))(a, dy)

    tk = min(tk, k)

    def body(a_ref, dy_ref, o_ref, dy_bf):
        @pl.when(pl.program_id(0) == 0)
        def _():
            dy_bf[...] = dy_ref[...].astype(BF16)

        o_ref[...] = _dot_tn(a_ref[...].astype(BF16), dy_bf[...])

    return _pcall(body, name=name, out_shape=_sds((k, n), F32), grid=(k // tk,),
                  in_specs=[pl.BlockSpec((t, tk), lambda i: (0, i)), _full((t, n))],
                  out_specs=pl.BlockSpec((tk, n), lambda i: (i, 0)),
                  scratch_shapes=[pltpu.VMEM((t, n), BF16)], semantics=("arbitrary",))(a, dy)


def norm_backward(dpre, w, x, g, rstd, dx_out, *, name, tm=512):
    t, d = x.shape
    n = dpre.shape[1]
    tm = min(tm, t)
    if w.ndim == 3:
        w_spec = pl.BlockSpec(w.shape, lambda i: (0, 0, 0))
    else:
        w_spec = pl.BlockSpec(w.shape, lambda i: (0, 0))

    def body(dp_ref, w_ref, x_ref, g_ref, r_ref, dxo_ref, dx_ref, dg_ref):
        @pl.when(pl.program_id(0) == 0)
        def _():
            dg_ref[...] = jnp.zeros_like(dg_ref)

        if w.ndim == 3:
            per = n // N_SHARDS
            dh = _dot_nt(dp_ref[:, 0:per], w_ref[0])
            for s in range(1, N_SHARDS):
                dh = dh + _dot_nt(dp_ref[:, s * per:(s + 1) * per], w_ref[s])
        else:
            dh = _dot_nt(dp_ref[...], w_ref[...])
        r = r_ref[...]
        xn = x_ref[...] * r
        dg_ref[...] += jnp.sum(dh * xn, axis=0, keepdims=True)
        dxn = dh * g_ref[...]
        dx = r * (dxn - xn * jnp.mean(dxn * xn, axis=-1, keepdims=True))
        dx_ref[...] = dxo_ref[...] + dx

    row = lambda i: (i, 0)
    fixed = lambda i: (0, 0)
    return _pcall(
        body, name=name, out_shape=[_sds((t, d), F32), _sds((1, d), F32)], grid=(t // tm,),
        in_specs=[pl.BlockSpec((tm, n), row), w_spec, pl.BlockSpec((tm, d), row),
                  pl.BlockSpec((1, d), fixed), pl.BlockSpec((tm, 1), row), pl.BlockSpec((tm, d), row)],
        out_specs=[pl.BlockSpec((tm, d), row), pl.BlockSpec((1, d), fixed)],
        semantics=("arbitrary",))(dpre, w, x, g, rstd, dx_out)


def ple_backward(dx, gate, pp, *, name, tm=512):
    t, d = dx.shape
    tm = min(tm, t)

    def body(dx_ref, gate_ref, pp_ref, dg_ref, dp_ref):
        dxv = dx_ref[...]
        gate = gate_ref[...]
        dg_ref[...] = (dxv * pp_ref[...].astype(F32) * (gate * (1.0 - gate))).astype(BF16)
        dp_ref[...] = (dxv * gate).astype(BF16)

    spec = pl.BlockSpec((tm, d), lambda i: (i, 0))
    return _pcall(body, name=name, out_shape=[_sds((t, d), BF16)] * 2, grid=(t // tm,), in_specs=[spec] * 3,
                  out_specs=[spec] * 2, semantics=("parallel",))(dx, gate, pp)


def sgu_backward(dy, pre, g_v, w_s, b_full, *, name):
    t = pre.shape[0]
    n_chunks = t // CHUNK

    def body(dy_ref, pre_ref, gv_ref, ws_ref, b_ref, dpre_ref, dws_ref, db_ref, dgv_ref, dvn_s, dbf_s):
        step = pl.program_id(0)

        @pl.when(step == 0)
        def _():
            dws_ref[...] = jnp.zeros_like(dws_ref)
            dgv_ref[...] = jnp.zeros_like(dgv_ref)
            dbf_s[...] = jnp.zeros_like(dbf_s)

        pre_u, pre_v, u, r, vhat, vn, wm, tril = _sgu_common(pre_ref, gv_ref, ws_ref)
        dyv = dy_ref[...]
        for g in range(N_GROUPS):
            cols = slice(g * LANES, (g + 1) * LANES)
            mix = _dot(wm[g], vn[:, cols]) + b_ref[:, cols]
            dmix = dyv[:, cols] * u[:, cols]
            dmix_b = dmix.astype(BF16)
            du = dyv[:, cols] * mix
            dpre_ref[:, cols] = (du * _gelu_grad(pre_u[:, cols])).astype(BF16)
            dws_ref[g] += jnp.where(tril, _dot_nt(dmix_b, vn[:, cols]), 0.0)
            dbf_s[:, cols] += dmix
            dvn_s[:, cols] = _dot_tn(wm[g], dmix_b)
        dvn = dvn_s[...]
        dgv_ref[...] += jnp.sum(dvn * vhat, axis=0, keepdims=True)
        dxn = dvn * gv_ref[...]
        dv = r * (dxn - vhat * jnp.mean(dxn * vhat, axis=-1, keepdims=True))
        dpre_ref[:, D_MODEL:] = (dv * _gelu_grad(pre_v)).astype(BF16)

        @pl.when(step == n_chunks - 1)
        def _():
            lane = lax.broadcasted_iota(jnp.int32, (CHUNK, LANES), 1)
            acc = jnp.zeros((CHUNK, LANES), F32)
            for g in range(N_GROUPS):
                s = jnp.sum(dbf_s[:, g * LANES:(g + 1) * LANES], axis=-1, keepdims=True)
                acc = jnp.where(lane == g, s, acc)
            db_ref[...] = acc

    fixed2 = lambda i: (0, 0)
    return _pcall(
        body, name=name,
        out_shape=[_sds((t, 2 * D_MODEL), BF16), _sds((N_GROUPS, CHUNK, CHUNK), F32), _sds((CHUNK, LANES), F32),
                   _sds((1, D_MODEL), F32)],
        grid=(n_chunks,),
        in_specs=[pl.BlockSpec((CHUNK, D_MODEL), lambda i: (i, 0)), pl.BlockSpec((CHUNK, 2 * D_MODEL), lambda i: (i, 0)),
                  pl.BlockSpec((1, D_MODEL), fixed2), pl.BlockSpec((N_GROUPS, CHUNK, CHUNK), lambda i: (0, 0, 0)),
                  pl.BlockSpec((CHUNK, D_MODEL), fixed2)],
        out_specs=[pl.BlockSpec((CHUNK, 2 * D_MODEL), lambda i: (i, 0)),
                   pl.BlockSpec((N_GROUPS, CHUNK, CHUNK), lambda i: (0, 0, 0)), pl.BlockSpec((CHUNK, LANES), fixed2),
                   pl.BlockSpec((1, D_MODEL), fixed2)],
        scratch_shapes=[pltpu.VMEM((CHUNK, D_MODEL), F32), pltpu.VMEM((CHUNK, D_MODEL), F32)],
        semantics=("arbitrary",))(dy, pre, g_v, w_s, b_full)


def head_norm_backward(dy, pre, g128, *, name, col_block=0, scale=1.0, passthrough=None, tm=512):
    t = dy.shape[0]
    tm = min(tm, t)
    width = 2 * D_MODEL if passthrough is not None else D_MODEL

    def body(*refs):
        if passthrough is not None:
            dy_ref, x_ref, g_ref, dv_ref, o_ref, dg_ref = refs
            o_ref[:, D_MODEL:] = dv_ref[...].astype(BF16)
        else:
            dy_ref, x_ref, g_ref, o_ref, dg_ref = refs

        @pl.when(pl.program_id(0) == 0)
        def _():
            dg_ref[...] = jnp.zeros_like(dg_ref)

        g = g_ref[...]
        dg = jnp.zeros((1, LANES), F32)
        for b in range(D_MODEL // LANES):
            cols = slice(b * LANES, (b + 1) * LANES)
            xv = x_ref[:, cols]
            r = _head_rstd(xv)
            xn = xv * r
            dyv = dy_ref[:, cols] * scale
            dg = dg + jnp.sum(dyv * xn, axis=0, keepdims=True)
            dxn = dyv * g
            o_ref[:, cols] = (r * (dxn - xn * _head_mean(dxn * xn))).astype(BF16)
        dg_ref[...] += dg

    row = lambda i: (i, 0)
    in_specs = [pl.BlockSpec((tm, D_MODEL), row), pl.BlockSpec((tm, D_MODEL), lambda i: (i, col_block)),
                pl.BlockSpec((1, LANES), lambda i: (0, 0))]
    args = [dy, pre, g128]
    if passthrough is not None:
        in_specs.append(pl.BlockSpec((tm, D_MODEL), row))
        args.append(passthrough)
    return _pcall(body, name=name, out_shape=[_sds((t, width), BF16), _sds((1, LANES), F32)], grid=(t // tm,),
                  in_specs=in_specs,
                  out_specs=[pl.BlockSpec((tm, width), row), pl.BlockSpec((1, LANES), lambda i: (0, 0))],
                  semantics=("arbitrary",))(*args)


def stick_breaking_backward(q, k, v, do, *, name):
    t = q.shape[0]
    blk = min(ATT_BLOCK, t)
    nq = t // blk

    def body(q_ref, k_ref, v_ref, do_ref, dq_ref, dk_ref, dv_ref, s_buf, sg_buf):
        i = pl.program_id(1)

        @pl.when(i == 0)
        def _():
            dk_ref[...] = jnp.zeros_like(dk_ref)
            dv_ref[...] = jnp.zeros_like(dv_ref)

        low = lax.broadcasted_iota(jnp.int32, (blk, LANES), 1) < HEAD_DIM
        suffix = _suffix_matrix(blk)
        prefix = _prefix_matrix(blk)
        causal = _stacked_causal(blk)
        qs = _stack_heads(q_ref[...], low)
        dos = _stack_heads(do_ref[...], low)

        def log_weights(j, carry, masked):
            rows = pl.ds(pl.multiple_of(j * blk, blk), blk)
            z = _dot_nt(qs, k_ref[rows, :])
            ls = _log_sigmoid(z)
            lg = ls - z
            if masked:
                lg = jnp.where(causal, lg, 0.0)
            s_buf[j] = ls + _block_cumsum(lg, suffix) + carry
            sg_buf[j] = jnp.exp(ls)
            return carry + jnp.sum(lg, axis=-1, keepdims=True)

        carry = log_weights(i, jnp.zeros((2 * blk, 1), F32), True)
        carry = lax.fori_loop(0, i // 2, lambda n, c: log_weights(i - 2 - 2 * n, log_weights(i - 1 - 2 * n, c, False),
                                                                  False), carry)
        lax.fori_loop(0, i % 2, lambda n, c: log_weights(0, c, False), carry)

        def grads(j, pcarry, dq_acc, masked):
            rows = pl.ds(pl.multiple_of(j * blk, blk), blk)
            a = jnp.exp(s_buf[j])
            if masked:
                a = jnp.where(causal, a, 0.0)
            sg = sg_buf[j]
            ds = _dot_nt(dos, v_ref[rows, :]) * a
            before = _block_cumsum(ds, prefix) + pcarry
            if masked:
                before = jnp.where(causal, before, 0.0)
            dz = (ds - sg * (ds + before)).astype(BF16)
            dq_acc = dq_acc + _dot(dz, k_ref[rows, :])
            dk_ref[rows, :] += _dot_tn(dz, qs)
            dv_ref[rows, :] += _dot_tn(a.astype(BF16), dos)
            return pcarry + jnp.sum(ds, axis=-1, keepdims=True), dq_acc

        def two_blocks(n, st):
            st = grads(2 * n, st[0], st[1], False)
            return grads(2 * n + 1, st[0], st[1], False)

        state = lax.fori_loop(0, i // 2, two_blocks,
                              (jnp.zeros((2 * blk, 1), F32), jnp.zeros((2 * blk, LANES), F32)))
        state = lax.fori_loop(0, i % 2, lambda n, st: grads(i - 1, st[0], st[1], False), state)
        _, dq_acc = grads(i, state[0], state[1], True)
        dq_ref[...] = jnp.where(low, dq_acc[:blk], dq_acc[blk:])

    full = pl.BlockSpec((t, LANES), lambda p, i: (0, p))
    qblk = pl.BlockSpec((blk, LANES), lambda p, i: (i, p))
    return _pcall(
        body, name=name, out_shape=[_sds((t, D_MODEL), F32)] * 3, grid=(D_MODEL // LANES, nq),
        in_specs=[qblk, full, full, qblk], out_specs=[qblk, full, full],
        scratch_shapes=[pltpu.VMEM((nq, 2 * blk, blk), F32), pltpu.VMEM((nq, 2 * blk, blk), F32)],
        semantics=("parallel", "arbitrary"))(q, k, v, do)


def _mlp_backward(dx, saved, g, w_up, w_down, tag):
    x, h, r, a, a2 = saved
    d_w_down = matmul_tn(a2, dx, name=f"d_w_down_{tag}", col_shards=False)
    dpre = matmul_nt(dx, w_down, name=f"d_mlp_act_{tag}", mul=a, out_dtype=BF16)
    d_w_up = matmul_tn(h, dpre, name=f"d_w_up_{tag}", col_shards=True)
    dx, d_g = norm_backward(dpre, w_up, x, g, r, dx, name=f"d_mlp_norm_{tag}")
    return dx, d_w_up, d_w_down, d_g


def _ple_backward(dx, saved, p, g, w_gate, tag):
    x, h, r, gate, pp = saved
    dgate, dproj = ple_backward(dx, gate, pp, name=f"d_ple_{tag}")
    d_w_proj = matmul_tn(p, dproj, name=f"d_w_ple_proj_{tag}", col_shards=True)
    d_w_gate = matmul_tn(h, dgate, name=f"d_w_ple_gate_{tag}", col_shards=False)
    dx, d_g = norm_backward(dgate, w_gate, x, g, r, dx, name=f"d_ple_norm_{tag}")
    return dx, d_w_gate, d_w_proj, d_g


def local_step(x, p, target, w, late=None):
    row = lambda v: v.reshape(1, -1)
    g128 = lambda v: jnp.tile(v.reshape(1, HEAD_DIM), (1, 2))
    scale = HEAD_DIM ** -0.5
    b_full = jnp.repeat(jnp.transpose(w["b_spatial"][0]), LANES, axis=1)
    w_s = w["w_spatial"][0]

    mats = {}
    for name, value in w.items():
        if isinstance(value, tuple):
            mats.update({(name, layer): v for layer, v in enumerate(value)})
    if "w_kv" in w:
        mats[("w_kv", 0)] = w["w_kv"]

    def fetch(name, layer, after):
        if (name, layer) not in mats:
            mats.update(late.weights(name, layer, after))
        return mats[(name, layer)]

    def mlp_forward(x_in, layer):
        h, r, a, a2 = norm_matmul(x_in, row(w["ln_mlp"][layer]), fetch("w_up", layer, x_in), name=f"mlp_up_{layer}",
                                  epilogue="relu2")
        return matmul_residual(a2, fetch("w_down", layer, a2), x_in, name=f"mlp_down_{layer}"), (x_in, h, r, a, a2)

    def ple(x_in, layer):
        return ple_forward(x_in, row(w["ln_ple"][layer]), fetch("w_ple_gate", layer, x_in), p[layer],
                           fetch("w_ple_proj", layer, x_in), name=f"ple_{layer}")

    x0 = x
    h_a, r_a, pre_a = norm_matmul(x0, row(w["ln_mix_a"][0]), fetch("w_in_a", 0, x0), name="sgu_in")
    y_a = sgu_forward(pre_a, row(w["g_v_a"][0]), w_s, b_full, name="sgu_mix")
    x1 = matmul_residual(y_a, fetch("w_out_a", 0, y_a), x0, name="sgu_out")
    x2, mlp0 = mlp_forward(x1, 0)
    ple0 = ple(x2, 0)
    x3 = ple0[4]
    h_kv, r_kv, kv_pre = norm_matmul(x3, row(w["ln_kv"]), fetch("w_kv", 0, x3), name="kv_proj")
    k_n, v_b = head_norm(kv_pre, g128(w["g_k"]), name="k_norm", passthrough=True)
    h_q, r_q, q_pre = norm_matmul(x3, row(w["ln_mix_b"][0]), fetch("w_q", 0, k_n), name="q_proj")
    q_n = head_norm(q_pre, g128(w["g_q"][0]), name="q_norm", scale=scale)
    o = stick_breaking_forward(q_n, k_n, v_b, name="sb_fwd")
    x4 = matmul_residual(o, fetch("w_out_b", 0, o), x3, name="sb_out")
    x5, mlp1 = mlp_forward(x4, 1)
    ple1 = ple(x5, 1)
    x6 = ple1[4]
    loss_blk, dx = loss_forward(x6, target, name="loss")

    g = {}
    dx, dwg1, dwp1, dlnp1 = _ple_backward(dx, (x5,) + tuple(ple1[:4]), p[1], row(w["ln_ple"][1]),
                                          mats[("w_ple_gate", 1)], 1)
    dx, dwu1, dwd1, dlnm1 = _mlp_backward(dx, mlp1, row(w["ln_mlp"][1]), mats[("w_up", 1)], mats[("w_down", 1)], 1)
    g["w_out_b"] = matmul_tn(o, dx, name="d_w_out_b", col_shards=False)
    do = matmul_nt(dx, mats[("w_out_b", 0)], name="d_sb_out", out_dtype=BF16)
    dq_n, dk_n, dv = stick_breaking_backward(q_n, k_n, v_b, do, name="sb_bwd")
    dq_pre, dgq = head_norm_backward(dq_n, q_pre, g128(w["g_q"][0]), name="d_q_norm", scale=scale)
    dkv_pre, dgk = head_norm_backward(dk_n, kv_pre, g128(w["g_k"]), name="d_k_norm", passthrough=dv)
    g["w_q"] = matmul_tn(h_q, dq_pre, name="d_w_q", col_shards=False)
    g["w_kv"] = matmul_tn(h_kv, dkv_pre, name="d_w_kv", col_shards=True)
    dx, g["ln_mix_b"] = norm_backward(dq_pre, mats[("w_q", 0)], x3, row(w["ln_mix_b"][0]), r_q, dx, name="d_q_in")
    dx, g["ln_kv"] = norm_backward(dkv_pre, mats[("w_kv", 0)], x3, row(w["ln_kv"]), r_kv, dx, name="d_kv_in")
    g["g_q"] = dgq[:, :HEAD_DIM] + dgq[:, HEAD_DIM:]
    g["g_k"] = (dgk[:, :HEAD_DIM] + dgk[:, HEAD_DIM:]).reshape(HEAD_DIM)
    g["ln_kv"] = g["ln_kv"].reshape(D_MODEL)
    ln_ple0, ln_mlp0 = row(w["ln_ple"][0]), row(w["ln_mlp"][0])
    if late is not None:
        ln_ple0 = ln_ple0 + late.second_layer_grads(
            {("w_kv", 0): g["w_kv"], ("w_q", 0): g["w_q"], ("w_out_b", 0): g["w_out_b"], ("w_up", 1): dwu1,
             ("w_down", 1): dwd1, ("w_ple_gate", 1): dwg1, ("w_ple_proj", 1): dwp1}, dx)[0, 0]
    dx, dwg0, dwp0, dlnp0 = _ple_backward(dx, (x2,) + tuple(ple0[:4]), p[0], ln_ple0, mats[("w_ple_gate", 0)], 0)
    if late is not None:
        ln_mlp0 = ln_mlp0 + late.first_ple_backward_done(dx)[0, 0]
    dx, dwu0, dwd0, dlnm0 = _mlp_backward(dx, mlp0, ln_mlp0, mats[("w_up", 0)], mats[("w_down", 0)], 0)
    g["w_out_a"] = matmul_tn(y_a, dx, name="d_w_out_a", col_shards=False)
    dy_a = matmul_nt(dx, mats[("w_out_a", 0)], name="d_sgu_out")
    dpre_a, dws, db, g["g_v_a"] = sgu_backward(dy_a, pre_a, row(w["g_v_a"][0]), w_s, b_full, name="d_sgu_mix")
    g["w_in_a"] = matmul_tn(h_a, dpre_a, name="d_w_in_a", col_shards=True)
    dx, g["ln_mix_a"] = norm_backward(dpre_a, mats[("w_in_a", 0)], x0, row(w["ln_mix_a"][0]), r_a, dx, name="d_sgu_in")
    g["w_spatial"] = dws[None]
    g["b_spatial"] = jnp.transpose(db[:, :N_GROUPS])[None]
    g["w_up"] = (dwu0, dwu1)
    g["w_down"] = (dwd0, dwd1)
    g["w_ple_gate"] = (dwg0, dwg1)
    g["w_ple_proj"] = (dwp0, dwp1)
    g["ln_mlp"] = jnp.concatenate([dlnm0, dlnm1], axis=0)
    g["ln_ple"] = jnp.concatenate([dlnp0, dlnp1], axis=0)
    return loss_blk, dx, g


ANY = pl.BlockSpec(memory_space=pl.ANY)


def _place():
    x, y, c = lax.axis_index("x"), lax.axis_index("y"), lax.axis_index("c")
    others = [(1 - x, y), (x, 1 - y), (1 - x, 1 - y)]
    return x, y, c, 2 * x + y, others


def cast_into_slot(w3, layer, slot, *, name, tm=256):
    _, r, c = w3.shape
    tm = min(tm, r)

    def body(slot_ref, w_ref, o_ref):
        o_ref[...] = w_ref[...].astype(BF16)

    return _pcall(body, name=name, out_shape=_sds((N_SHARDS, r, c), BF16), grid=(r // tm,), num_prefetch=1,
                  in_specs=[pl.BlockSpec((None, tm, c), lambda i, s: (layer, i, 0))],
                  out_specs=pl.BlockSpec((None, tm, c), lambda i, s: (s[0], i, 0)),
                  semantics=("parallel",))(slot, w3)


def gather_shards(mats, vecs, *, name):
    nm, nv = len(mats), len(vecs)
    halves = [m.reshape(N_SHARDS, 2, m.shape[1] // 2, m.shape[2]) for m in mats]

    def body(*refs):
        vsrc = refs[nm:nm + nv]
        out, vout = refs[nm + nv:2 * nm + nv], refs[2 * nm + nv:2 * (nm + nv)]
        send, recv, vsend, vrecv, loc = refs[2 * (nm + nv):]
        x, y, c, s_me, others = _place()
        sib = (x, y, 1 - c)

        def ici(l, k):
            ox, oy = others[k]
            return pltpu.make_async_remote_copy(out[l].at[s_me, c], out[l].at[s_me, c], send.at[l, k], recv.at[l, k],
                                                device_id=(ox, oy, c), device_id_type=MESH)

        def landed(l, k, half):
            ox, oy = others[k]
            return out[l].at[2 * ox + oy, half]

        def passed_on(l, k):
            return pltpu.make_async_remote_copy(landed(l, k, c), landed(l, k, c), send.at[l, 3 + k], recv.at[l, 3 + k],
                                                device_id=sib, device_id_type=MESH)

        def vec(l, k):
            ox, oy = others[k]
            return pltpu.make_async_remote_copy(vsrc[l], vout[l].at[s_me], vsend.at[l, k], vrecv.at[l, k],
                                                device_id=(ox, oy, c), device_id_type=MESH)

        for l in range(nm):
            for k in range(3):
                ici(l, k).start()
        for l in range(nv):
            for k in range(3):
                vec(l, k).start()
        for l in range(nv):
            own = pltpu.make_async_copy(vsrc[l], vout[l].at[s_me], loc)
            own.start()
            own.wait()
        for l in range(nm):
            for k in range(3):
                pltpu.make_async_remote_copy(landed(l, k, c), landed(l, k, c), send.at[l, k], recv.at[l, k],
                                             device_id=sib, device_id_type=MESH).wait_recv()
                passed_on(l, k).start()
        for l in range(nm):
            for k in range(3):
                pltpu.make_async_remote_copy(landed(l, k, 1 - c), landed(l, k, 1 - c), send.at[l, 3 + k],
                                             recv.at[l, 3 + k], device_id=sib, device_id_type=MESH).wait_recv()
        for l in range(nv):
            for k in range(3):
                ox, oy = others[k]
                pltpu.make_async_remote_copy(vsrc[l], vout[l].at[2 * ox + oy], vsend.at[l, k], vrecv.at[l, k],
                                             device_id=sib, device_id_type=MESH).wait_recv()
        for l in range(nm):
            for k in range(3):
                ici(l, k).wait_send()
                passed_on(l, k).wait_send()
        for l in range(nv):
            for k in range(3):
                vec(l, k).wait_send()

    out_shape = [_sds(h.shape, BF16) for h in halves] + [_sds((N_SHARDS,) + v.shape, F32) for v in vecs]
    res = _pcall(body, name=name, out_shape=out_shape, in_specs=[ANY] * (nm + nv), out_specs=[ANY] * (nm + nv),
                 scratch_shapes=[pltpu.SemaphoreType.DMA((max(nm, 1), 6)), pltpu.SemaphoreType.DMA((max(nm, 1), 6)),
                                 pltpu.SemaphoreType.DMA((max(nv, 1), 3)), pltpu.SemaphoreType.DMA((max(nv, 1), 3)),
                                 pltpu.SemaphoreType.DMA(())],
                 aliases={l: l for l in range(nm)}, side_effects=True)(*halves, *vecs)
    return [r.reshape(m.shape) for r, m in zip(res[:nm], mats)], list(res[nm:])


HBM = pl.BlockSpec(memory_space=pltpu.HBM)
SEM = pl.BlockSpec(memory_space=pltpu.SEMAPHORE)
DATAFLOW = pltpu.SideEffectType.DATAFLOW_SIDE_EFFECTING


def _split_call(body, *, name, out_shape, in_specs, out_specs, aliases):
    return pl.pallas_call(body, name=name, out_shape=out_shape, in_specs=in_specs, out_specs=out_specs,
                          input_output_aliases=aliases,
                          compiler_params=pltpu.CompilerParams(has_side_effects=DATAFLOW))


def _token_shape():
    return jax.ShapeDtypeStruct((8, LANES), F32)


def gather_start(mats, after, *, name):
    n = len(mats)
    halves = [pltpu.with_memory_space_constraint(m.reshape(N_SHARDS, 2, m.shape[1] // 2, m.shape[2]), pltpu.HBM)
              for m in mats]

    def body(*refs):
        send, recv = refs[n + 1], refs[n + 2]
        out, token = refs[n + 3:2 * n + 3], refs[2 * n + 3]
        x, y, c, s_me, others = _place()
        for l in range(n):
            for k in range(3):
                ox, oy = others[k]
                pltpu.make_async_remote_copy(out[l].at[s_me, c], out[l].at[s_me, c], send.at[3 * l + k],
                                             recv.at[3 * l + k], device_id=(ox, oy, c), device_id_type=MESH).start()
        token[...] = jnp.zeros_like(token)

    res = _split_call(
        body, name=name,
        out_shape=(pltpu.SemaphoreType.DMA((3 * n,)), pltpu.SemaphoreType.DMA((3 * n,)),
                   *[pltpu.HBM(h.shape, BF16) for h in halves], _token_shape()),
        in_specs=[HBM] * n + [ANY], out_specs=(SEM, SEM, *[HBM] * n, pl.BlockSpec(memory_space=pltpu.VMEM)),
        aliases={l: 2 + l for l in range(n)})(*halves, after)
    return res[0], res[1], list(res[2:2 + n]), res[2 + n]


def gather_pass_on(bufs, send_a, recv_a, after, *, name, base=0):
    n = len(bufs)

    def body(*refs):
        send_a, recv_a = refs[n], refs[n + 1]
        out = refs[n + 3:2 * n + 3]
        send_b, recv_b, token = refs[2 * n + 3:]
        x, y, c, s_me, others = _place()
        for l in range(n):
            for k in range(3):
                ox, oy = others[k]
                landed, i = out[l].at[2 * ox + oy, c], 3 * l + k
                pltpu.make_async_remote_copy(landed, landed, send_a.at[3 * base + i], recv_a.at[3 * base + i],
                                             device_id=(x, y, 1 - c), device_id_type=MESH).wait_recv()
                pltpu.make_async_remote_copy(landed, landed, send_b.at[i], recv_b.at[i],
                                             device_id=(x, y, 1 - c), device_id_type=MESH).start()
        for l in range(n):
            for k in range(3):
                mine, i = out[l].at[s_me, c], 3 * (base + l) + k
                pltpu.make_async_remote_copy(mine, mine, send_a.at[i], recv_a.at[i],
                                             device_id=(x, y, 1 - c), device_id_type=MESH).wait_send()
        token[...] = jnp.zeros_like(token)

    res = _split_call(
        body, name=name,
        out_shape=(*[pltpu.HBM(b.shape, BF16) for b in bufs], pltpu.SemaphoreType.DMA((3 * n,)),
                   pltpu.SemaphoreType.DMA((3 * n,)), _token_shape()),
        in_specs=[HBM] * n + [SEM, SEM, ANY],
        out_specs=(*[HBM] * n, SEM, SEM, pl.BlockSpec(memory_space=pltpu.VMEM)),
        aliases={l: l for l in range(n)})(*bufs, send_a, recv_a, after)
    return list(res[:n]), res[n], res[n + 1], res[n + 2]


def gather_finish(bufs, send_b, recv_b, after, shapes, *, name):
    n = len(bufs)

    def body(*refs):
        send_b, recv_b = refs[n], refs[n + 1]
        out = refs[n + 3:]
        x, y, c, _, others = _place()
        for l in range(n):
            for k in range(3):
                ox, oy = others[k]
                theirs, mine, i = out[l].at[2 * ox + oy, 1 - c], out[l].at[2 * ox + oy, c], 3 * l + k
                pltpu.make_async_remote_copy(theirs, theirs, send_b.at[i], recv_b.at[i],
                                             device_id=(x, y, 1 - c), device_id_type=MESH).wait_recv()
                pltpu.make_async_remote_copy(mine, mine, send_b.at[i], recv_b.at[i],
                                             device_id=(x, y, 1 - c), device_id_type=MESH).wait_send()

    res = _split_call(
        body, name=name, out_shape=tuple(pltpu.HBM(b.shape, BF16) for b in bufs),
        in_specs=[HBM] * n + [SEM, SEM, ANY], out_specs=tuple([HBM] * n),
        aliases={l: l for l in range(n)})(*bufs, send_b, recv_b, after)
    return [r.reshape(s) for r, s in zip(res, shapes)]


def exchange_start(srcs, dst_shapes, dst_dtype, plan, count, after, *, name):
    n, m = len(srcs), len(dst_shapes)
    srcs = [pltpu.with_memory_space_constraint(s, pltpu.HBM) for s in srcs]
    lands = [pltpu.with_memory_space_constraint(lax.empty(s, dst_dtype), pltpu.HBM) for s in dst_shapes]

    def body(*refs):
        send, recv = refs[n + m + 1], refs[n + m + 2]
        src, dst, token = refs[n + m + 3:2 * n + m + 3], refs[2 * n + m + 3:2 * (n + m) + 3], refs[2 * (n + m) + 3]
        for i, (s, d, dev) in enumerate(plan(_place(), src, dst)):
            pltpu.make_async_remote_copy(s, d, send.at[i], recv.at[i], device_id=dev, device_id_type=MESH).start()
        token[...] = jnp.zeros_like(token)

    res = _split_call(
        body, name=name,
        out_shape=(pltpu.SemaphoreType.DMA((count,)), pltpu.SemaphoreType.DMA((count,)),
                   *[pltpu.HBM(s.shape, s.dtype) for s in srcs], *[pltpu.HBM(s, dst_dtype) for s in dst_shapes],
                   _token_shape()),
        in_specs=[HBM] * (n + m) + [ANY],
        out_specs=(SEM, SEM, *[HBM] * (n + m), pl.BlockSpec(memory_space=pltpu.VMEM)),
        aliases={i: 2 + i for i in range(n + m)})(*srcs, *lands, after)
    return (list(res[2:2 + n]), list(res[2 + n:2 + n + m]), res[0], res[1], plan), res[2 + n + m]


def exchange_finish(state, after, *, name):
    srcs, lands, send, recv, plan = state
    n, m = len(srcs), len(lands)

    def body(*refs):
        send, recv = refs[n + m], refs[n + m + 1]
        src, dst = refs[n + m + 3:2 * n + m + 3], refs[2 * n + m + 3:]
        for i, (s, d, dev) in enumerate(plan(_place(), src, dst)):
            pltpu.make_async_remote_copy(s, d, send.at[i], recv.at[i], device_id=dev, device_id_type=MESH).wait()

    res = _split_call(
        body, name=name,
        out_shape=tuple(pltpu.HBM(a.shape, a.dtype) for a in srcs + lands),
        in_specs=[HBM] * (n + m) + [SEM, SEM, ANY], out_specs=tuple([HBM] * (n + m)),
        aliases={i: i for i in range(n + m)})(*srcs, *lands, send, recv, after)
    return list(res[:n]), list(res[n:])


def pair_plan(place, src, dst):
    x, y, c, _, _ = place
    return [(s.at[:, 1 - c], d, (x, y, 1 - c)) for s, d in zip(src, dst)]


def chip_plan(place, src, dst):
    x, y, c, _, others = place
    return [(s.at[2 * ox + oy], d.at[k], (ox, oy, c)) for s, d in zip(src, dst) for k, (ox, oy) in enumerate(others)]


def pair_exchange(grads, *, name):
    n = len(grads)

    def body(*refs):
        src, got = refs[:n], refs[n:2 * n]
        send, recv = refs[2 * n:]
        x, y, c, _, _ = _place()

        def swap(l):
            return pltpu.make_async_remote_copy(src[l].at[:, 1 - c], got[l], send.at[l], recv.at[l],
                                                device_id=(x, y, 1 - c), device_id_type=MESH)

        for l in range(n):
            swap(l).start()
        for l in range(n):
            swap(l).wait()

    res = _pcall(body, name=name, out_shape=[_sds((N_SHARDS,) + g.shape[2:], F32) for g in grads],
                 in_specs=[ANY] * n, out_specs=[ANY] * n,
                 scratch_shapes=[pltpu.SemaphoreType.DMA((n,)), pltpu.SemaphoreType.DMA((n,))],
                 side_effects=True)(*grads)
    return list(res)


def add_to_wire(mine, theirs, core, *, name, tm=256):
    s, _, r, c = mine.shape
    tm = min(tm, r)

    def body(core_ref, a_ref, b_ref, o_ref):
        o_ref[...] = (a_ref[...] + b_ref[...]).astype(BF16)

    spec = pl.BlockSpec((None, tm, c), lambda i, j, cr: (i, j, 0))
    return _pcall(body, name=name, out_shape=_sds((s, r, c), BF16), grid=(s, r // tm), num_prefetch=1,
                  in_specs=[pl.BlockSpec((None, None, tm, c), lambda i, j, cr: (i, cr[0], j, 0)), spec],
                  out_specs=spec, semantics=("parallel", "parallel"))(core, mine, theirs)


def sum_chips(wire, landed, place, dest, layer, n_layers, *, name, tm=256):
    _, r, c = wire.shape
    tm = min(tm, r)

    def body(place_ref, w_ref, l_ref, *rest):
        o_ref = rest[-1]
        o_ref[...] = ((w_ref[...].astype(F32) + l_ref[0].astype(F32)) + l_ref[1].astype(F32)) + l_ref[2].astype(F32)

    in_specs = [pl.BlockSpec((None, tm, c), lambda i, pr: (pr[0], i, 0)),
                pl.BlockSpec((3, tm, c), lambda i, pr: (0, i, 0))]
    args = [place, wire, landed]
    aliases = None
    if dest is not None:
        in_specs.append(ANY)
        args.append(dest)
        aliases = {3: 0}
    return _pcall(body, name=name, out_shape=_sds((n_layers, 2, r, c), F32), grid=(r // tm,), num_prefetch=1,
                  in_specs=in_specs,
                  out_specs=pl.BlockSpec((None, None, tm, c), lambda i, pr: (layer, pr[1], i, 0)),
                  aliases=aliases, semantics=("parallel",))(*args)


def pair_share(bufs, slots, *, name):
    n = len(bufs)

    def body(*refs):
        out = refs[n:2 * n]
        send, recv = refs[2 * n:]
        x, y, c, _, _ = _place()

        def share(i, half):
            o, l = slots[i]
            return pltpu.make_async_remote_copy(out[o].at[l, half], out[o].at[l, half], send.at[i], recv.at[i],
                                                device_id=(x, y, 1 - c), device_id_type=MESH)

        for i in range(len(slots)):
            share(i, c).start()
        for i in range(len(slots)):
            share(i, 1 - c).wait_recv()
            share(i, c).wait_send()

    res = _pcall(body, name=name, out_shape=[_sds(b.shape, F32) for b in bufs], in_specs=[ANY] * n,
                 out_specs=[ANY] * n,
                 scratch_shapes=[pltpu.SemaphoreType.DMA((len(slots),)), pltpu.SemaphoreType.DMA((len(slots),))],
                 aliases={o: o for o in range(n)}, side_effects=True)(*bufs)
    return list(res)


def all_reduce_small(packed, *, name):
    n_dev, r, c = packed.shape

    def body(in_ref, out_ref, land, send, recv):
        x, y, cc, _, _ = _place()
        me = 4 * x + 2 * y + cc
        peers = [(px, py, pc) for px in range(2) for py in range(2) for pc in range(2)]

        def scatter(d):
            return pltpu.make_async_remote_copy(in_ref.at[d], land.at[me], send.at[0, d], recv.at[0, me],
                                                device_id=peers[d], device_id_type=MESH)

        def gather(d):
            return pltpu.make_async_remote_copy(out_ref.at[me], out_ref.at[me], send.at[1, d], recv.at[1, me],
                                                device_id=peers[d], device_id_type=MESH)

        for d in range(n_dev):
            @pl.when(d != me)
            def _():
                scatter(d).start()
        land[me] = in_ref[me]
        for d in range(n_dev):
            @pl.when(d != me)
            def _():
                pltpu.make_async_remote_copy(in_ref.at[d], land.at[d], send.at[0, d], recv.at[0, d],
                                             device_id=peers[d], device_id_type=MESH).wait_recv()
        total = land[0]
        for d in range(1, n_dev):
            total = total + land[d]
        out_ref[me] = total
        for d in range(n_dev):
            @pl.when(d != me)
            def _():
                gather(d).start()
        for d in range(n_dev):
            @pl.when(d != me)
            def _():
                pltpu.make_async_remote_copy(out_ref.at[d], out_ref.at[d], send.at[1, d], recv.at[1, d],
                                             device_id=peers[d], device_id_type=MESH).wait_recv()
        for d in range(n_dev):
            @pl.when(d != me)
            def _():
                scatter(d).wait_send()
                gather(d).wait_send()

    vm = pl.BlockSpec(memory_space=pltpu.VMEM)
    return _pcall(body, name=name, out_shape=_sds(packed.shape, F32), in_specs=[vm], out_specs=vm,
                  scratch_shapes=[pltpu.VMEM(packed.shape, F32), pltpu.SemaphoreType.DMA((2, n_dev)),
                                  pltpu.SemaphoreType.DMA((2, n_dev))],
                  side_effects=True)(packed)


def adamw(w, g, m, v, *, name, part=None, dest=None, tm=256):
    shape = w.shape
    cols = shape[-1]
    rows = 1
    for s in shape[:-1]:
        rows *= s
    first, count = 0, rows
    if part is not None:
        count = rows // part[1]
        first = part[0] * count
    tm = min(tm, count)
    assert count % tm == 0
    two_d = lambda a: a.reshape(rows, cols)

    def body(w_ref, g_ref, m_ref, v_ref, *rest):
        d_ref, mo_ref, vo_ref = rest[-3:]
        gv = g_ref[...]
        m_new = ADAM_B1 * m_ref[...] + (1.0 - ADAM_B1) * gv
        v_new = ADAM_B2 * v_ref[...] + (1.0 - ADAM_B2) * (gv * gv)
        m_hat = m_new / (1.0 - ADAM_B1 ** ADAM_STEP)
        v_hat = v_new / (1.0 - ADAM_B2 ** ADAM_STEP)
        d_ref[...] = -ADAM_LR * (m_hat / (jnp.sqrt(v_hat) + ADAM_EPS) + ADAM_WD * w_ref[...])
        mo_ref[...] = m_new
        vo_ref[...] = v_new

    spec = pl.BlockSpec((tm, cols), lambda i: (first // tm + i, 0))
    args = [two_d(w), two_d(g), two_d(m), two_d(v)]
    in_specs = [spec] * 4
    aliases = None
    if dest is not None:
        args += [two_d(d) for d in dest]
        in_specs = in_specs + [ANY] * 3
        aliases = {4: 0, 5: 1, 6: 2}
    outs = _pcall(body, name=name, out_shape=[_sds((rows, cols), F32)] * 3, grid=(count // tm,), in_specs=in_specs,
                  out_specs=[spec] * 3, aliases=aliases, semantics=("parallel",))(*args)
    return [o.reshape(shape) for o in outs]


WEIGHTS = ("ln_mix_a", "w_in_a", "g_v_a", "w_spatial", "b_spatial", "w_out_a", "ln_kv", "w_kv", "g_k", "ln_mix_b",
           "w_q", "g_q", "w_out_b", "ln_mlp", "w_up", "w_down", "ln_ple", "w_ple_gate", "w_ple_proj")
MATRICES = (("w_in_a", 1, True), ("w_out_a", 1, False), ("w_kv", 0, True), ("w_q", 1, False), ("w_out_b", 1, False),
            ("w_up", 2, True), ("w_down", 2, False), ("w_ple_gate", 2, False), ("w_ple_proj", 2, True))
GATHER_STAGES = ((("w_in_a", 0),), (("w_out_a", 0),), (("w_up", 0),), (("w_down", 0),),
                 (("w_ple_gate", 0), ("w_ple_proj", 0), ("w_kv", 0)), (("w_q", 0),), (("w_out_b", 0),), (("w_up", 1),),
                 (("w_down", 1),), (("w_ple_gate", 1), ("w_ple_proj", 1)))
REPLICATED = ("w_spatial", "b_spatial", "ln_kv", "g_k", "ln_mix_b", "g_q", "ln_mlp", "ln_ple")
SHARDED_VECTORS = ("ln_mix_a", "g_v_a")
SMALL_ROWS = 18


def kernel(x, p, ln_mix_a, w_in_a, g_v_a, w_spatial, b_spatial, w_out_a, ln_kv, w_kv, g_k, ln_mix_b, w_q, g_q, w_out_b, ln_mlp, w_up, w_down, ln_ple, w_ple_gate, w_ple_proj, loss_target, m_ln_mix_a, m_w_in_a, m_g_v_a, m_w_spatial, m_b_spatial, m_w_out_a, m_ln_kv, m_w_kv, m_g_k, m_ln_mix_b, m_w_q, m_g_q, m_w_out_b, m_ln_mlp, m_w_up, m_w_down, m_ln_ple, m_w_ple_gate, m_w_ple_proj, v_ln_mix_a, v_w_in_a, v_g_v_a, v_w_spatial, v_b_spatial, v_w_out_a, v_ln_kv, v_w_kv, v_g_k, v_ln_mix_b, v_w_q, v_g_q, v_w_out_b, v_ln_mlp, v_w_up, v_w_down, v_ln_ple, v_w_ple_gate, v_w_ple_proj):
    given = dict(locals())
    weights = {n: given[n] for n in WEIGHTS}
    shard = 2 * lax.axis_index("x") + lax.axis_index("y")
    core = lax.axis_index("c")
    shard_1 = shard.astype(jnp.int32).reshape(1)
    core_1 = core.astype(jnp.int32).reshape(1)
    place = jnp.stack([shard, core]).astype(jnp.int32)

    leaves = []
    for name, layers, cols in MATRICES:
        w3 = weights[name] if layers else weights[name][None]
        for layer in range(max(layers, 1)):
            leaves.append((name, layer, cols, cast_into_slot(w3, layer, shard_1, name=f"cast_{name}_{layer}")))
    by_key = {(lf[0], lf[1]): lf for lf in leaves}
    ordered = [by_key[key] for stage in GATHER_STAGES for key in stage]
    _, vec_a = gather_shards([], [ln_mix_a, g_v_a], name="gather_vectors")
    send_a, recv_a, flying, token = gather_start([lf[3] for lf in ordered], vec_a[0], name="gather_start")

    w = {"ln_mix_a": vec_a[0].reshape(1, D_MODEL) + token[0, 0],
         "g_v_a": vec_a[1].reshape(1, D_MODEL)}
    for name in REPLICATED:
        w[name] = weights[name]

    class Late:
        def weights(self, name, layer, after):
            stage = [(name, layer) in s for s in GATHER_STAGES].index(True)
            base = sum(len(s) for s in GATHER_STAGES[:stage])
            members = ordered[base:base + len(GATHER_STAGES[stage])]
            bufs, send_b, recv_b, tok = gather_pass_on(flying[base:base + len(members)], send_a, recv_a, after,
                                                       name=f"gather_pass_on_{stage}", base=base)
            got = gather_finish(bufs, send_b, recv_b, tok, [lf[3].shape for lf in members],
                                name=f"gather_finish_{stage}")
            out = {}
            for (leaf_name, leaf_layer, cols, _), arr in zip(members, got):
                out[(leaf_name, leaf_layer)] = arr if cols else arr.reshape(N_SHARDS * arr.shape[1], arr.shape[2])
            return out

        def second_layer_grads(self, grads_late, dx_done):
            self.keys = sorted(grads_late)
            views = [view(k, grads_late[k]) for k in self.keys]
            self.pair, token = exchange_start(views, [(N_SHARDS,) + v.shape[2:] for v in views], F32, pair_plan,
                                              len(views), dx_done, name="grad_pair_start_1")
            return token

        def first_ple_backward_done(self, dx_done):
            mine, theirs = exchange_finish(self.pair, dx_done, name="grad_pair_finish_1")
            wire = [add_to_wire(a, b, core_1, name=f"grad_pair_sum_1_{i}") for i, (a, b) in enumerate(zip(mine, theirs))]
            self.chip, token = exchange_start(wire, [(3,) + v.shape[1:] for v in wire], BF16, chip_plan, 3 * len(wire),
                                              wire[-1], name="grad_chip_start_1")
            return token

    col_sharded = {name: cols for name, _, cols in MATRICES}
    layer_count = {name: max(layers, 1) for name, layers, _ in MATRICES}

    def view(key, arr):
        rows = arr.shape[-2] if col_sharded[key[0]] else arr.shape[0] // N_SHARDS
        return arr.reshape(N_SHARDS, 2, rows // 2, arr.shape[-1])

    t = x.shape[1]
    late = Late()
    loss_blk, dx, g = local_step(x[0], p.reshape(2, t, PLE_DIM), loss_target[0], w, late)
    loss = lax.psum(loss_blk[0, 0], ("x", "y", "c"))

    keys_early = [(name, layer) for name, layers, _ in MATRICES for layer in range(max(layers, 1))
                  if (name, layer) not in late.keys]
    views = [view(k, g[k[0]][k[1]] if layer_count[k[0]] == 2 else g[k[0]]) for k in keys_early]
    theirs = pair_exchange(views, name="grad_pair_exchange_0")
    wire_0 = [add_to_wire(a, b, core_1, name=f"grad_pair_sum_0_{i}") for i, (a, b) in enumerate(zip(views, theirs))]

    grads = {}
    small = REPLICATED + SHARDED_VECTORS
    flat = jnp.concatenate([g[n].reshape(-1) for n in small])
    room = 8 * SMALL_ROWS * D_MODEL
    flat = jnp.concatenate([flat, jnp.zeros((room - flat.shape[0],), F32)])
    reduced = all_reduce_small(flat.reshape(8, SMALL_ROWS, D_MODEL), name="grad_small_all_reduce").reshape(-1)
    at = 0
    for n in small:
        size = g[n].size
        piece = reduced[at:at + size]
        at += size
        if n in SHARDED_VECTORS:
            per = D_MODEL // N_SHARDS
            grads[n] = lax.dynamic_slice(piece, (shard * per,), (per,)).reshape(weights[n].shape)
        else:
            grads[n] = piece.reshape(weights[n].shape)

    chip_0, token_0 = exchange_start(wire_0, [(3,) + v.shape[1:] for v in wire_0], BF16, chip_plan, 3 * len(wire_0),
                                     reduced, name="grad_chip_start_0")

    bufs = {}

    def sum_and_share(keys, wire, landed, tag):
        for i, (key, wv, lv) in enumerate(zip(keys, wire, landed)):
            name, layer = key
            bufs[name] = sum_chips(wv, lv, place, bufs.get(name), layer, layer_count[name],
                                   name=f"grad_chip_sum_{tag}_{i}")
        names = sorted({k[0] for k in keys})
        shared = pair_share([bufs[n] for n in names], [(names.index(k[0]), k[1]) for k in keys],
                            name=f"grad_pair_share_{tag}")
        bufs.update(zip(names, shared))

    wire_1, landed_1 = exchange_finish(late.chip, token_0, name="grad_chip_finish_1")
    sum_and_share(late.keys, wire_1, landed_1, 1)

    updates = {}

    def update(n, gn, part=None):
        wn, mn, vn = weights[n], given["m_" + n], given["v_" + n]
        if wn.ndim == 1:
            wn, gn, mn, vn = (a.reshape(1, -1) for a in (wn, gn, mn, vn))
        tag = "" if part is None else f"_{part[0]}"
        updates[n] = adamw(wn, gn.reshape(wn.shape), mn, vn, name=f"adamw_{n}{tag}", part=part, dest=updates.get(n))

    for n in small:
        update(n, grads[n])
    for name, layer in late.keys:
        update(name, bufs[name], (layer, layer_count[name]) if layer_count[name] == 2 else None)

    wire_0, landed_0 = exchange_finish(chip_0, updates[late.keys[-1][0]][0], name="grad_chip_finish_0")
    sum_and_share(keys_early, wire_0, landed_0, 0)
    for name, layer in keys_early:
        update(name, bufs[name], (layer, layer_count[name]) if layer_count[name] == 2 else None)
    for name, _, _ in MATRICES:
        grads[name] = bufs[name].reshape(weights[name].shape)
    delta = {n: updates[n][0].reshape(weights[n].shape) for n in WEIGHTS}
    new_m = {n: updates[n][1].reshape(weights[n].shape) for n in WEIGHTS}
    new_v = {n: updates[n][2].reshape(weights[n].shape) for n in WEIGHTS}
    return (loss, dx.reshape(x.shape), *[grads[n] for n in WEIGHTS], *[delta[n] for n in WEIGHTS],
            *[new_m[n] for n in WEIGHTS], *[new_v[n] for n in WEIGHTS])
```

```python
import jax
import jax.numpy as jnp
from jax import lax
from jax.experimental import pallas as pl
from jax.experimental.pallas import tpu as pltpu

F32 = jnp.float32
BF16 = jnp.bfloat16

D_MODEL = 1024
D_FF = 4096
PLE_DIM = 256
N_GROUPS = 8
CHUNK = 128
HEAD_DIM = 64
LANES = 128
ATT_BLOCK = 256
EPS = 1e-6
N_SHARDS = 4
VMEM_LIMIT = 56 * 1024 * 1024

ADAM_LR = 0.001
ADAM_B1 = 0.9
ADAM_B2 = 0.999
ADAM_EPS = 1e-08
ADAM_WD = 0.01
ADAM_STEP = 10

MESH = pl.DeviceIdType.MESH


def _pcall(body, *, name, out_shape, grid=None, in_specs=None, out_specs=None, scratch_shapes=(),
           semantics=None, aliases=None, side_effects=False, num_prefetch=0):
    params = dict(vmem_limit_bytes=VMEM_LIMIT)
    if semantics is not None:
        params["dimension_semantics"] = semantics
    if side_effects:
        params["has_side_effects"] = True
    kwargs = {}
    if aliases:
        kwargs["input_output_aliases"] = aliases
    if num_prefetch:
        spec = pltpu.PrefetchScalarGridSpec(num_scalar_prefetch=num_prefetch, grid=grid, in_specs=in_specs,
                                            out_specs=out_specs, scratch_shapes=list(scratch_shapes))
        return pl.pallas_call(body, name=name, out_shape=out_shape, grid_spec=spec,
                              compiler_params=pltpu.CompilerParams(**params), **kwargs)
    if grid is not None:
        kwargs["grid"] = grid
    if in_specs is not None:
        kwargs["in_specs"] = in_specs
    if out_specs is not None:
        kwargs["out_specs"] = out_specs
    if aliases:
        kwargs["input_output_aliases"] = aliases
    return pl.pallas_call(body, name=name, out_shape=out_shape, scratch_shapes=list(scratch_shapes),
                          compiler_params=pltpu.CompilerParams(**params), **kwargs)


def _sds(shape, dtype):
    return jax.ShapeDtypeStruct(shape, dtype)


_GELU_C = 0.7978845608028654
_GELU_A = 0.044715


def _gelu(x):
    inner = _GELU_C * (x + _GELU_A * (x * x * x))
    return 0.5 * x * (1.0 + jnp.tanh(inner))


def _gelu_grad(x):
    x2 = x * x
    t = jnp.tanh(_GELU_C * (x + _GELU_A * (x2 * x)))
    return 0.5 * (1.0 + t) + 0.5 * x * (1.0 - t * t) * (_GELU_C * (1.0 + 3.0 * _GELU_A * x2))


def _sigmoid(x):
    return 1.0 / (1.0 + jnp.exp(-x))


def _log_sigmoid(z):
    return jnp.minimum(z, 0.0) - jnp.log(1.0 + jnp.exp(-jnp.abs(z)))


def _dot(a, b):
    return jnp.dot(a, b, preferred_element_type=F32)


def _dot_nt(a, b):
    return lax.dot_general(a, b, (((1,), (1,)), ((), ())), preferred_element_type=F32)


def _dot_tn(a, b):
    return lax.dot_general(a, b, (((0,), (0,)), ((), ())), preferred_element_type=F32)


def _head_rstd(x):
    lane = lax.broadcasted_iota(jnp.int32, x.shape, 1)
    low = lane < HEAD_DIM
    sq = x * x
    s_lo = jnp.sum(jnp.where(low, sq, 0.0), axis=-1, keepdims=True)
    s_hi = jnp.sum(jnp.where(low, 0.0, sq), axis=-1, keepdims=True)
    ms = jnp.where(low, s_lo, s_hi) * (1.0 / HEAD_DIM)
    return lax.rsqrt(ms + EPS)


def _head_mean(x):
    lane = lax.broadcasted_iota(jnp.int32, x.shape, 1)
    low = lane < HEAD_DIM
    s_lo = jnp.sum(jnp.where(low, x, 0.0), axis=-1, keepdims=True)
    s_hi = jnp.sum(jnp.where(low, 0.0, x), axis=-1, keepdims=True)
    return jnp.where(low, s_lo, s_hi) * (1.0 / HEAD_DIM)


def _full(shape):
    zeros = (0,) * len(shape)
    return pl.BlockSpec(shape, lambda i: zeros)


def norm_matmul(x, g, w, *, name, epilogue="none", tm=512):
    t, d = x.shape
    sharded = w.ndim == 3
    per = w.shape[2] if sharded else w.shape[1]
    n = N_SHARDS * per if sharded else per
    tm = min(tm, t)

    def body(x_ref, g_ref, w_ref, h_ref, r_ref, *outs):
        xv = x_ref[...]
        r = lax.rsqrt(jnp.mean(xv * xv, axis=-1, keepdims=True) + EPS)
        h = ((xv * r) * g_ref[...]).astype(BF16)
        h_ref[...] = h
        r_ref[...] = r
        for s in range(N_SHARDS if sharded else 1):
            cols = slice(s * per, (s + 1) * per)
            y = _dot(h, w_ref[s] if sharded else w_ref[...])
            if epilogue == "none":
                outs[0][:, cols] = y
            else:
                a = jnp.maximum(y, 0.0)
                outs[0][:, cols] = a.astype(BF16)
                outs[1][:, cols] = (a * a).astype(BF16)

    row = lambda i: (i, 0)
    out_shape = [_sds((t, d), BF16), _sds((t, 1), F32)]
    out_specs = [pl.BlockSpec((tm, d), row), pl.BlockSpec((tm, 1), row)]
    if epilogue == "none":
        out_shape.append(_sds((t, n), F32))
        out_specs.append(pl.BlockSpec((tm, n), row))
    else:
        out_shape += [_sds((t, n), BF16), _sds((t, n), BF16)]
        out_specs += [pl.BlockSpec((tm, n), row)] * 2
    return _pcall(
        body, name=name, out_shape=out_shape, grid=(t // tm,),
        in_specs=[pl.BlockSpec((tm, d), row), _full((1, d)), _full(w.shape)],
        out_specs=out_specs, semantics=("parallel",))(x, g, w)


def matmul_residual(a, w, res, *, name, tm=512):
    t, k = a.shape
    n = w.shape[1]
    tm = min(tm, t)

    def body(a_ref, w_ref, res_ref, o_ref):
        o_ref[...] = res_ref[...] + _dot(a_ref[...], w_ref[...])

    row = lambda i: (i, 0)
    return _pcall(
        body, name=name, out_shape=_sds((t, n), F32), grid=(t // tm,),
        in_specs=[pl.BlockSpec((tm, k), row), _full(w.shape), pl.BlockSpec((tm, n), row)],
        out_specs=pl.BlockSpec((tm, n), row), semantics=("parallel",))(a, w, res)


def ple_forward(x, g, w_gate, p, w_proj, *, name, tm=256):
    t, d = x.shape
    tm = min(tm, t)

    def body(x_ref, g_ref, wg_ref, p_ref, wp_ref, h_ref, r_ref, gate_ref, pp_ref, o_ref):
        xv = x_ref[...]
        r = lax.rsqrt(jnp.mean(xv * xv, axis=-1, keepdims=True) + EPS)
        h = ((xv * r) * g_ref[...]).astype(BF16)
        h_ref[...] = h
        r_ref[...] = r
        gate = _sigmoid(_dot(h, wg_ref[...]))
        gate_ref[...] = gate
        pb = p_ref[...].astype(BF16)
        per = d // N_SHARDS
        for s in range(N_SHARDS):
            cols = slice(s * per, (s + 1) * per)
            pp = _dot(pb, wp_ref[s])
            pp_ref[:, cols] = pp.astype(BF16)
            o_ref[:, cols] = xv[:, cols] + pp * gate[:, cols]

    row = lambda i: (i, 0)
    fixed = lambda i: (0, 0)
    return _pcall(
        body, name=name,
        out_shape=[_sds((t, d), BF16), _sds((t, 1), F32), _sds((t, d), F32), _sds((t, d), BF16), _sds((t, d), F32)],
        grid=(t // tm,),
        in_specs=[pl.BlockSpec((tm, d), row), pl.BlockSpec((1, d), fixed), pl.BlockSpec((d, d), fixed),
                  pl.BlockSpec((tm, PLE_DIM), row),
                  pl.BlockSpec((N_SHARDS, PLE_DIM, d // N_SHARDS), lambda i: (0, 0, 0))],
        out_specs=[pl.BlockSpec((tm, d), row), pl.BlockSpec((tm, 1), row), pl.BlockSpec((tm, d), row),
                   pl.BlockSpec((tm, d), row), pl.BlockSpec((tm, d), row)],
        semantics=("parallel",))(x, g, w_gate, p, w_proj)


def _tril_mask():
    r = lax.broadcasted_iota(jnp.int32, (CHUNK, CHUNK), 0)
    c = lax.broadcasted_iota(jnp.int32, (CHUNK, CHUNK), 1)
    return c <= r


def _sgu_common(pre_ref, gv_ref, ws_ref):
    pre = pre_ref[...]
    pre_u, pre_v = pre[:, :D_MODEL], pre[:, D_MODEL:]
    u = _gelu(pre_u)
    v = _gelu(pre_v)
    r = lax.rsqrt(jnp.mean(v * v, axis=-1, keepdims=True) + EPS)
    vhat = v * r
    vn = (vhat * gv_ref[...]).astype(BF16)
    tril = _tril_mask()
    wm = [jnp.where(tril, ws_ref[g], 0.0).astype(BF16) for g in range(N_GROUPS)]
    return pre_u, pre_v, u, r, vhat, vn, wm, tril


def sgu_forward(pre, g_v, w_s, b_full, *, name):
    t = pre.shape[0]

    def body(pre_ref, gv_ref, ws_ref, b_ref, y_ref):
        _, _, u, _, _, vn, wm, _ = _sgu_common(pre_ref, gv_ref, ws_ref)
        for g in range(N_GROUPS):
            cols = slice(g * LANES, (g + 1) * LANES)
            mix = _dot(wm[g], vn[:, cols]) + b_ref[:, cols]
            y_ref[:, cols] = (u[:, cols] * mix).astype(BF16)

    return _pcall(
        body, name=name, out_shape=_sds((t, D_MODEL), BF16), grid=(t // CHUNK,),
        in_specs=[pl.BlockSpec((CHUNK, 2 * D_MODEL), lambda i: (i, 0)), pl.BlockSpec((1, D_MODEL), lambda i: (0, 0)),
                  pl.BlockSpec((N_GROUPS, CHUNK, CHUNK), lambda i: (0, 0, 0)),
                  pl.BlockSpec((CHUNK, D_MODEL), lambda i: (0, 0))],
        out_specs=pl.BlockSpec((CHUNK, D_MODEL), lambda i: (i, 0)),
        semantics=("parallel",))(pre, g_v, w_s, b_full)


def head_norm(pre, g128, *, name, col_block=0, scale=1.0, passthrough=False, tm=512):
    t = pre.shape[0]
    tm = min(tm, t)

    def body(*refs):
        if passthrough:
            x_ref, v_ref, g_ref, o_ref, vo_ref = refs
            vo_ref[...] = v_ref[...].astype(BF16)
        else:
            x_ref, g_ref, o_ref = refs
        g = g_ref[...] * scale
        for b in range(D_MODEL // LANES):
            cols = slice(b * LANES, (b + 1) * LANES)
            xv = x_ref[:, cols]
            o_ref[:, cols] = ((xv * _head_rstd(xv)) * g).astype(BF16)

    x_spec = pl.BlockSpec((tm, D_MODEL), lambda i: (i, col_block))
    g_spec = pl.BlockSpec((1, LANES), lambda i: (0, 0))
    o_spec = pl.BlockSpec((tm, D_MODEL), lambda i: (i, 0))
    if passthrough:
        return _pcall(body, name=name, out_shape=[_sds((t, D_MODEL), BF16)] * 2, grid=(t // tm,),
                      in_specs=[x_spec, pl.BlockSpec((tm, D_MODEL), lambda i: (i, 1)), g_spec],
                      out_specs=[o_spec, o_spec], semantics=("parallel",))(pre, pre, g128)
    return _pcall(body, name=name, out_shape=_sds((t, D_MODEL), BF16), grid=(t // tm,),
                  in_specs=[x_spec, g_spec], out_specs=o_spec, semantics=("parallel",))(pre, g128)


def _suffix_matrix(n):
    r = lax.broadcasted_iota(jnp.int32, (n, n), 0)
    c = lax.broadcasted_iota(jnp.int32, (n, n), 1)
    return jnp.where(r > c, 1.0, 0.0).astype(BF16)


def _prefix_matrix(n):
    r = lax.broadcasted_iota(jnp.int32, (n, n), 0)
    c = lax.broadcasted_iota(jnp.int32, (n, n), 1)
    return jnp.where(r < c, 1.0, 0.0).astype(BF16)


def _block_cumsum(a, tri):
    return _dot(a.astype(BF16), tri)


def _stacked_causal(n):
    r = lax.broadcasted_iota(jnp.int32, (2 * n, n), 0)
    c = lax.broadcasted_iota(jnp.int32, (2 * n, n), 1)
    return c < jnp.where(r >= n, r - n, r)


def _stack_heads(a, low):
    zero = jnp.zeros_like(a)
    return jnp.concatenate([jnp.where(low, a, zero), jnp.where(low, zero, a)], axis=0)


def stick_breaking_forward(q, k, v, *, name):
    t = q.shape[0]
    blk = min(ATT_BLOCK, t)
    nq = t // blk

    def body(q_ref, k_ref, v_ref, o_ref):
        i = pl.program_id(1)
        low = lax.broadcasted_iota(jnp.int32, (blk, LANES), 1) < HEAD_DIM
        tri = _suffix_matrix(blk)
        causal = _stacked_causal(blk)
        qs = _stack_heads(q_ref[...], low)

        def block(j, carry, acc, masked):
            rows = pl.ds(pl.multiple_of(j * blk, blk), blk)
            z = _dot_nt(qs, k_ref[rows, :])
            ls = _log_sigmoid(z)
            lg = ls - z
            if masked:
                lg = jnp.where(causal, lg, 0.0)
            s = ls + _block_cumsum(lg, tri) + carry
            a = jnp.exp(s)
            if masked:
                a = jnp.where(causal, a, 0.0)
            acc = acc + _dot(a.astype(BF16), v_ref[rows, :])
            return carry + jnp.sum(lg, axis=-1, keepdims=True), acc

        state = block(i, jnp.zeros((2 * blk, 1), F32), jnp.zeros((2 * blk, LANES), F32), True)

        def two_blocks(n, st):
            st = block(i - 1 - 2 * n, st[0], st[1], False)
            return block(i - 2 - 2 * n, st[0], st[1], False)

        state = lax.fori_loop(0, i // 2, two_blocks, state)
        _, acc = lax.fori_loop(0, i % 2, lambda n, st: block(0, st[0], st[1], False), state)
        o_ref[...] = jnp.where(low, acc[:blk], acc[blk:]).astype(BF16)

    return _pcall(
        body, name=name, out_shape=_sds((t, D_MODEL), BF16), grid=(D_MODEL // LANES, nq),
        in_specs=[pl.BlockSpec((blk, LANES), lambda p, i: (i, p)), pl.BlockSpec((t, LANES), lambda p, i: (0, p)),
                  pl.BlockSpec((t, LANES), lambda p, i: (0, p))],
        out_specs=pl.BlockSpec((blk, LANES), lambda p, i: (i, p)),
        semantics=("parallel", "arbitrary"))(q, k, v)


def loss_forward(x, target, *, name, tm=512):
    t, d = x.shape
    tm = min(tm, t)

    def body(x_ref, t_ref, l_ref, dx_ref):
        @pl.when(pl.program_id(0) == 0)
        def _():
            l_ref[...] = jnp.zeros_like(l_ref)

        diff = x_ref[...] - t_ref[...]
        dx_ref[...] = diff * (1.0 / d)
        l_ref[...] += 0.5 * jnp.sum(jnp.mean(diff * diff, axis=-1, keepdims=True))

    return _pcall(
        body, name=name, out_shape=[_sds((8, LANES), F32), _sds((t, d), F32)], grid=(t // tm,),
        in_specs=[pl.BlockSpec((tm, d), lambda i: (i, 0))] * 2,
        out_specs=[pl.BlockSpec((8, LANES), lambda i: (0, 0)), pl.BlockSpec((tm, d), lambda i: (i, 0))],
        semantics=("arbitrary",))(x, target)


def matmul_nt(dy, w, *, name, mul=None, out_dtype=F32, tm=512):
    t, n = dy.shape
    k = w.shape[0]
    tm = min(tm, t)

    def body(*refs):
        if mul is None:
            dy_ref, w_ref, o_ref = refs
        else:
            dy_ref, w_ref, m_ref, o_ref = refs
        y = _dot_nt(dy_ref[...].astype(BF16), w_ref[...])
        if mul is not None:
            y = y * (2.0 * m_ref[...].astype(F32))
        o_ref[...] = y.astype(out_dtype)

    row = lambda i: (i, 0)
    in_specs = [pl.BlockSpec((tm, n), row), _full(w.shape)]
    args = [dy, w]
    if mul is not None:
        in_specs.append(pl.BlockSpec((tm, k), row))
        args.append(mul)
    return _pcall(body, name=name, out_shape=_sds((t, k), out_dtype), grid=(t // tm,), in_specs=in_specs,
                  out_specs=pl.BlockSpec((tm, k), row), semantics=("parallel",))(*args)


def matmul_tn(a, dy, *, name, col_shards, tk=512):
    t, k = a.shape
    n = dy.shape[1]
    if col_shards:
        tn = n // N_SHARDS

        def body(a_ref, dy_ref, o_ref):
            o_ref[...] = _dot_tn(a_ref[...].astype(BF16), dy_ref[...].astype(BF16))

        return _pcall(body, name=name, out_shape=_sds((N_SHARDS, k, tn), F32), grid=(N_SHARDS,),
                      in_specs=[_full((t, k)), pl.BlockSpec((t, tn), lambda j: (0, j))],
                      out_specs=pl.BlockSpec((None, k, tn), lambda j: (j, 0, 0)), semantics=("parallel",))(a, dy)

    tk = min(tk, k)

    def body(a_ref, dy_ref, o_ref, dy_bf):
        @pl.when(pl.program_id(0) == 0)
        def _():
            dy_bf[...] = dy_ref[...].astype(BF16)

        o_ref[...] = _dot_tn(a_ref[...].astype(BF16), dy_bf[...])

    return _pcall(body, name=name, out_shape=_sds((k, n), F32), grid=(k // tk,),
                  in_specs=[pl.BlockSpec((t, tk), lambda i: (0, i)), _full((t, n))],
                  out_specs=pl.BlockSpec((tk, n), lambda i: (i, 0)),
                  scratch_shapes=[pltpu.VMEM((t, n), BF16)], semantics=("arbitrary",))(a, dy)


def norm_backward(dpre, w, x, g, rstd, dx_out, *, name, tm=512):
    t, d = x.shape
    n = dpre.shape[1]
    tm = min(tm, t)
    if w.ndim == 3:
        w_spec = pl.BlockSpec(w.shape, lambda i: (0, 0, 0))
    else:
        w_spec = pl.BlockSpec(w.shape, lambda i: (0, 0))

    def body(dp_ref, w_ref, x_ref, g_ref, r_ref, dxo_ref, dx_ref, dg_ref):
        @pl.when(pl.program_id(0) == 0)
        def _():
            dg_ref[...] = jnp.zeros_like(dg_ref)

        if w.ndim == 3:
            per = n // N_SHARDS
            dh = _dot_nt(dp_ref[:, 0:per], w_ref[0])
            for s in range(1, N_SHARDS):
                dh = dh + _dot_nt(dp_ref[:, s * per:(s + 1) * per], w_ref[s])
        else:
            dh = _dot_nt(dp_ref[...], w_ref[...])
        r = r_ref[...]
        xn = x_ref[...] * r
        dg_ref[...] += jnp.sum(dh * xn, axis=0, keepdims=True)
        dxn = dh * g_ref[...]
        dx = r * (dxn - xn * jnp.mean(dxn * xn, axis=-1, keepdims=True))
        dx_ref[...] = dxo_ref[...] + dx

    row = lambda i: (i, 0)
    fixed = lambda i: (0, 0)
    return _pcall(
        body, name=name, out_shape=[_sds((t, d), F32), _sds((1, d), F32)], grid=(t // tm,),
        in_specs=[pl.BlockSpec((tm, n), row), w_spec, pl.BlockSpec((tm, d), row),
                  pl.BlockSpec((1, d), fixed), pl.BlockSpec((tm, 1), row), pl.BlockSpec((tm, d), row)],
        out_specs=[pl.BlockSpec((tm, d), row), pl.BlockSpec((1, d), fixed)],
        semantics=("arbitrary",))(dpre, w, x, g, rstd, dx_out)


def ple_backward(dx, gate, pp, *, name, tm=512):
    t, d = dx.shape
    tm = min(tm, t)

    def body(dx_ref, gate_ref, pp_ref, dg_ref, dp_ref):
        dxv = dx_ref[...]
        gate = gate_ref[...]
        dg_ref[...] = (dxv * pp_ref[...].astype(F32) * (gate * (1.0 - gate))).astype(BF16)
        dp_ref[...] = (dxv * gate).astype(BF16)

    spec = pl.BlockSpec((tm, d), lambda i: (i, 0))
    return _pcall(body, name=name, out_shape=[_sds((t, d), BF16)] * 2, grid=(t // tm,), in_specs=[spec] * 3,
                  out_specs=[spec] * 2, semantics=("parallel",))(dx, gate, pp)


def sgu_backward(dy, pre, g_v, w_s, b_full, *, name):
    t = pre.shape[0]
    n_chunks = t // CHUNK

    def body(dy_ref, pre_ref, gv_ref, ws_ref, b_ref, dpre_ref, dws_ref, db_ref, dgv_ref, dvn_s, dbf_s):
        step = pl.program_id(0)

        @pl.when(step == 0)
        def _():
            dws_ref[...] = jnp.zeros_like(dws_ref)
            dgv_ref[...] = jnp.zeros_like(dgv_ref)
            dbf_s[...] = jnp.zeros_like(dbf_s)

        pre_u, pre_v, u, r, vhat, vn, wm, tril = _sgu_common(pre_ref, gv_ref, ws_ref)
        dyv = dy_ref[...]
        for g in range(N_GROUPS):
            cols = slice(g * LANES, (g + 1) * LANES)
            mix = _dot(wm[g], vn[:, cols]) + b_ref[:, cols]
            dmix = dyv[:, cols] * u[:, cols]
            dmix_b = dmix.astype(BF16)
            du = dyv[:, cols] * mix
            dpre_ref[:, cols] = (du * _gelu_grad(pre_u[:, cols])).astype(BF16)
            dws_ref[g] += jnp.where(tril, _dot_nt(dmix_b, vn[:, cols]), 0.0)
            dbf_s[:, cols] += dmix
            dvn_s[:, cols] = _dot_tn(wm[g], dmix_b)
        dvn = dvn_s[...]
        dgv_ref[...] += jnp.sum(dvn * vhat, axis=0, keepdims=True)
        dxn = dvn * gv_ref[...]
        dv = r * (dxn - vhat * jnp.mean(dxn * vhat, axis=-1, keepdims=True))
        dpre_ref[:, D_MODEL:] = (dv * _gelu_grad(pre_v)).astype(BF16)

        @pl.when(step == n_chunks - 1)
        def _():
            lane = lax.broadcasted_iota(jnp.int32, (CHUNK, LANES), 1)
            acc = jnp.zeros((CHUNK, LANES), F32)
            for g in range(N_GROUPS):
                s = jnp.sum(dbf_s[:, g * LANES:(g + 1) * LANES], axis=-1, keepdims=True)
                acc = jnp.where(lane == g, s, acc)
            db_ref[...] = acc

    fixed2 = lambda i: (0, 0)
    return _pcall(
        body, name=name,
        out_shape=[_sds((t, 2 * D_MODEL), BF16), _sds((N_GROUPS, CHUNK, CHUNK), F32), _sds((CHUNK, LANES), F32),
                   _sds((1, D_MODEL), F32)],
        grid=(n_chunks,),
        in_specs=[pl.BlockSpec((CHUNK, D_MODEL), lambda i: (i, 0)), pl.BlockSpec((CHUNK, 2 * D_MODEL), lambda i: (i, 0)),
                  pl.BlockSpec((1, D_MODEL), fixed2), pl.BlockSpec((N_GROUPS, CHUNK, CHUNK), lambda i: (0, 0, 0)),
                  pl.BlockSpec((CHUNK, D_MODEL), fixed2)],
        out_specs=[pl.BlockSpec((CHUNK, 2 * D_MODEL), lambda i: (i, 0)),
                   pl.BlockSpec((N_GROUPS, CHUNK, CHUNK), lambda i: (0, 0, 0)), pl.BlockSpec((CHUNK, LANES), fixed2),
                   pl.BlockSpec((1, D_MODEL), fixed2)],
        scratch_shapes=[pltpu.VMEM((CHUNK, D_MODEL), F32), pltpu.VMEM((CHUNK, D_MODEL), F32)],
        semantics=("arbitrary",))(dy, pre, g_v, w_s, b_full)


def head_norm_backward(dy, pre, g128, *, name, col_block=0, scale=1.0, passthrough=None, tm=512):
    t = dy.shape[0]
    tm = min(tm, t)
    width = 2 * D_MODEL if passthrough is not None else D_MODEL

    def body(*refs):
        if passthrough is not None:
            dy_ref, x_ref, g_ref, dv_ref, o_ref, dg_ref = refs
            o_ref[:, D_MODEL:] = dv_ref[...].astype(BF16)
        else:
            dy_ref, x_ref, g_ref, o_ref, dg_ref = refs

        @pl.when(pl.program_id(0) == 0)
        def _():
            dg_ref[...] = jnp.zeros_like(dg_ref)

        g = g_ref[...]
        dg = jnp.zeros((1, LANES), F32)
        for b in range(D_MODEL // LANES):
            cols = slice(b * LANES, (b + 1) * LANES)
            xv = x_ref[:, cols]
            r = _head_rstd(xv)
            xn = xv * r
            dyv = dy_ref[:, cols] * scale
            dg = dg + jnp.sum(dyv * xn, axis=0, keepdims=True)
            dxn = dyv * g
            o_ref[:, cols] = (r * (dxn - xn * _head_mean(dxn * xn))).astype(BF16)
        dg_ref[...] += dg

    row = lambda i: (i, 0)
    in_specs = [pl.BlockSpec((tm, D_MODEL), row), pl.BlockSpec((tm, D_MODEL), lambda i: (i, col_block)),
                pl.BlockSpec((1, LANES), lambda i: (0, 0))]
    args = [dy, pre, g128]
    if passthrough is not None:
        in_specs.append(pl.BlockSpec((tm, D_MODEL), row))
        args.append(passthrough)
    return _pcall(body, name=name, out_shape=[_sds((t, width), BF16), _sds((1, LANES), F32)], grid=(t // tm,),
                  in_specs=in_specs,
                  out_specs=[pl.BlockSpec((tm, width), row), pl.BlockSpec((1, LANES), lambda i: (0, 0))],
                  semantics=("arbitrary",))(*args)


def stick_breaking_backward(q, k, v, do, *, name):
    t = q.shape[0]
    blk = min(ATT_BLOCK, t)
    nq = t // blk

    def body(q_ref, k_ref, v_ref, do_ref, dq_ref, dk_ref, dv_ref, s_buf, sg_buf):
        i = pl.program_id(1)

        @pl.when(i == 0)
        def _():
            dk_ref[...] = jnp.zeros_like(dk_ref)
            dv_ref[...] = jnp.zeros_like(dv_ref)

        low = lax.broadcasted_iota(jnp.int32, (blk, LANES), 1) < HEAD_DIM
        suffix = _suffix_matrix(blk)
        prefix = _prefix_matrix(blk)
        causal = _stacked_causal(blk)
        qs = _stack_heads(q_ref[...], low)
        dos = _stack_heads(do_ref[...], low)

        def log_weights(j, carry, masked):
            rows = pl.ds(pl.multiple_of(j * blk, blk), blk)
            z = _dot_nt(qs, k_ref[rows, :])
            ls = _log_sigmoid(z)
            lg = ls - z
            if masked:
                lg = jnp.where(causal, lg, 0.0)
            s_buf[j] = ls + _block_cumsum(lg, suffix) + carry
            sg_buf[j] = jnp.exp(ls)
            return carry + jnp.sum(lg, axis=-1, keepdims=True)

        carry = log_weights(i, jnp.zeros((2 * blk, 1), F32), True)
        carry = lax.fori_loop(0, i // 2, lambda n, c: log_weights(i - 2 - 2 * n, log_weights(i - 1 - 2 * n, c, False),
                                                                  False), carry)
        lax.fori_loop(0, i % 2, lambda n, c: log_weights(0, c, False), carry)

        def grads(j, pcarry, dq_acc, masked):
            rows = pl.ds(pl.multiple_of(j * blk, blk), blk)
            a = jnp.exp(s_buf[j])
            if masked:
                a = jnp.where(causal, a, 0.0)
            sg = sg_buf[j]
            ds = _dot_nt(dos, v_ref[rows, :]) * a
            before = _block_cumsum(ds, prefix) + pcarry
            if masked:
                before = jnp.where(causal, before, 0.0)
            dz = (ds - sg * (ds + before)).astype(BF16)
            dq_acc = dq_acc + _dot(dz, k_ref[rows, :])
            dk_ref[rows, :] += _dot_tn(dz, qs)
            dv_ref[rows, :] += _dot_tn(a.astype(BF16), dos)
            return pcarry + jnp.sum(ds, axis=-1, keepdims=True), dq_acc

        def two_blocks(n, st):
            st = grads(2 * n, st[0], st[1], False)
            return grads(2 * n + 1, st[0], st[1], False)

        state = lax.fori_loop(0, i // 2, two_blocks,
                              (jnp.zeros((2 * blk, 1), F32), jnp.zeros((2 * blk, LANES), F32)))
        state = lax.fori_loop(0, i % 2, lambda n, st: grads(i - 1, st[0], st[1], False), state)
        _, dq_acc = grads(i, state[0], state[1], True)
        dq_ref[...] = jnp.where(low, dq_acc[:blk], dq_acc[blk:])

    full = pl.BlockSpec((t, LANES), lambda p, i: (0, p))
    qblk = pl.BlockSpec((blk, LANES), lambda p, i: (i, p))
    return _pcall(
        body, name=name, out_shape=[_sds((t, D_MODEL), F32)] * 3, grid=(D_MODEL // LANES, nq),
        in_specs=[qblk, full, full, qblk], out_specs=[qblk, full, full],
        scratch_shapes=[pltpu.VMEM((nq, 2 * blk, blk), F32), pltpu.VMEM((nq, 2 * blk, blk), F32)],
        semantics=("parallel", "arbitrary"))(q, k, v, do)


def _mlp_backward(dx, saved, g, w_up, w_down, tag):
    x, h, r, a, a2 = saved
    d_w_down = matmul_tn(a2, dx, name=f"d_w_down_{tag}", col_shards=False)
    dpre = matmul_nt(dx, w_down, name=f"d_mlp_act_{tag}", mul=a, out_dtype=BF16)
    d_w_up = matmul_tn(h, dpre, name=f"d_w_up_{tag}", col_shards=True)
    dx, d_g = norm_backward(dpre, w_up, x, g, r, dx, name=f"d_mlp_norm_{tag}")
    return dx, d_w_up, d_w_down, d_g


def _ple_backward(dx, saved, p, g, w_gate, tag):
    x, h, r, gate, pp = saved
    dgate, dproj = ple_backward(dx, gate, pp, name=f"d_ple_{tag}")
    d_w_proj = matmul_tn(p, dproj, name=f"d_w_ple_proj_{tag}", col_shards=True)
    d_w_gate = matmul_tn(h, dgate, name=f"d_w_ple_gate_{tag}", col_shards=False)
    dx, d_g = norm_backward(dgate, w_gate, x, g, r, dx, name=f"d_ple_norm_{tag}")
    return dx, d_w_gate, d_w_proj, d_g


def local_step(x, p, target, w, late=None):
    row = lambda v: v.reshape(1, -1)
    g128 = lambda v: jnp.tile(v.reshape(1, HEAD_DIM), (1, 2))
    scale = HEAD_DIM ** -0.5
    b_full = jnp.repeat(jnp.transpose(w["b_spatial"][0]), LANES, axis=1)
    w_s = w["w_spatial"][0]

    mats = {}
    for name, value in w.items():
        if isinstance(value, tuple):
            mats.update({(name, layer): v for layer, v in enumerate(value)})
    if "w_kv" in w:
        mats[("w_kv", 0)] = w["w_kv"]

    def fetch(name, layer, after):
        if (name, layer) not in mats:
            mats.update(late.weights(name, layer, after))
        return mats[(name, layer)]

    def mlp_forward(x_in, layer):
        h, r, a, a2 = norm_matmul(x_in, row(w["ln_mlp"][layer]), fetch("w_up", layer, x_in), name=f"mlp_up_{layer}",
                                  epilogue="relu2")
        return matmul_residual(a2, fetch("w_down", layer, a2), x_in, name=f"mlp_down_{layer}"), (x_in, h, r, a, a2)

    def ple(x_in, layer):
        return ple_forward(x_in, row(w["ln_ple"][layer]), fetch("w_ple_gate", layer, x_in), p[layer],
                           fetch("w_ple_proj", layer, x_in), name=f"ple_{layer}")

    x0 = x
    h_a, r_a, pre_a = norm_matmul(x0, row(w["ln_mix_a"][0]), fetch("w_in_a", 0, x0), name="sgu_in")
    y_a = sgu_forward(pre_a, row(w["g_v_a"][0]), w_s, b_full, name="sgu_mix")
    x1 = matmul_residual(y_a, fetch("w_out_a", 0, y_a), x0, name="sgu_out")
    x2, mlp0 = mlp_forward(x1, 0)
    ple0 = ple(x2, 0)
    x3 = ple0[4]
    h_kv, r_kv, kv_pre = norm_matmul(x3, row(w["ln_kv"]), fetch("w_kv", 0, x3), name="kv_proj")
    k_n, v_b = head_norm(kv_pre, g128(w["g_k"]), name="k_norm", passthrough=True)
    h_q, r_q, q_pre = norm_matmul(x3, row(w["ln_mix_b"][0]), fetch("w_q", 0, k_n), name="q_proj")
    q_n = head_norm(q_pre, g128(w["g_q"][0]), name="q_norm", scale=scale)
    o = stick_breaking_forward(q_n, k_n, v_b, name="sb_fwd")
    x4 = matmul_residual(o, fetch("w_out_b", 0, o), x3, name="sb_out")
    x5, mlp1 = mlp_forward(x4, 1)
    ple1 = ple(x5, 1)
    x6 = ple1[4]
    loss_blk, dx = loss_forward(x6, target, name="loss")

    g = {}
    dx, dwg1, dwp1, dlnp1 = _ple_backward(dx, (x5,) + tuple(ple1[:4]), p[1], row(w["ln_ple"][1]),
                                          mats[("w_ple_gate", 1)], 1)
    dx, dwu1, dwd1, dlnm1 = _mlp_backward(dx, mlp1, row(w["ln_mlp"][1]), mats[("w_up", 1)], mats[("w_down", 1)], 1)
    g["w_out_b"] = matmul_tn(o, dx, name="d_w_out_b", col_shards=False)
    do = matmul_nt(dx, mats[("w_out_b", 0)], name="d_sb_out", out_dtype=BF16)
    dq_n, dk_n, dv = stick_breaking_backward(q_n, k_n, v_b, do, name="sb_bwd")
    dq_pre, dgq = head_norm_backward(dq_n, q_pre, g128(w["g_q"][0]), name="d_q_norm", scale=scale)
    dkv_pre, dgk = head_norm_backward(dk_n, kv_pre, g128(w["g_k"]), name="d_k_norm", passthrough=dv)
    g["w_q"] = matmul_tn(h_q, dq_pre, name="d_w_q", col_shards=False)
    g["w_kv"] = matmul_tn(h_kv, dkv_pre, name="d_w_kv", col_shards=True)
    dx, g["ln_mix_b"] = norm_backward(dq_pre, mats[("w_q", 0)], x3, row(w["ln_mix_b"][0]), r_q, dx, name="d_q_in")
    dx, g["ln_kv"] = norm_backward(dkv_pre, mats[("w_kv", 0)], x3, row(w["ln_kv"]), r_kv, dx, name="d_kv_in")
    g["g_q"] = dgq[:, :HEAD_DIM] + dgq[:, HEAD_DIM:]
    g["g_k"] = (dgk[:, :HEAD_DIM] + dgk[:, HEAD_DIM:]).reshape(HEAD_DIM)
    g["ln_kv"] = g["ln_kv"].reshape(D_MODEL)
    ln_ple0, ln_mlp0, g_v0, ln_mix0 = (row(w["ln_ple"][0]), row(w["ln_mlp"][0]), row(w["g_v_a"][0]),
                                       row(w["ln_mix_a"][0]))
    if late is not None:
        ln_ple0 = ln_ple0 + late.pair_start(
            {("w_kv", 0): g["w_kv"], ("w_q", 0): g["w_q"], ("w_out_b", 0): g["w_out_b"], ("w_up", 1): dwu1,
             ("w_down", 1): dwd1, ("w_ple_gate", 1): dwg1, ("w_ple_proj", 1): dwp1}, dx)[0, 0]
    dx, dwg0, dwp0, dlnp0 = _ple_backward(dx, (x2,) + tuple(ple0[:4]), p[0], ln_ple0, mats[("w_ple_gate", 0)], 0)
    if late is not None:
        ln_mlp0 = ln_mlp0 + late.chip_start(dx)[0, 0]
    dx, dwu0, dwd0, dlnm0 = _mlp_backward(dx, mlp0, ln_mlp0, mats[("w_up", 0)], mats[("w_down", 0)], 0)
    if late is not None:
        g_v0 = g_v0 + late.pair_start({("w_up", 0): dwu0, ("w_down", 0): dwd0, ("w_ple_gate", 0): dwg0,
                                       ("w_ple_proj", 0): dwp0}, dx)[0, 0]
    g["w_out_a"] = matmul_tn(y_a, dx, name="d_w_out_a", col_shards=False)
    dy_a = matmul_nt(dx, mats[("w_out_a", 0)], name="d_sgu_out")
    dpre_a, dws, db, g["g_v_a"] = sgu_backward(dy_a, pre_a, g_v0, w_s, b_full, name="d_sgu_mix")
    if late is not None:
        ln_mix0 = ln_mix0 + late.chip_start(dpre_a)[0, 0]
    g["w_in_a"] = matmul_tn(h_a, dpre_a, name="d_w_in_a", col_shards=True)
    dx, g["ln_mix_a"] = norm_backward(dpre_a, mats[("w_in_a", 0)], x0, ln_mix0, r_a, dx, name="d_sgu_in")
    g["w_spatial"] = dws[None]
    g["b_spatial"] = jnp.transpose(db[:, :N_GROUPS])[None]
    g["w_up"] = (dwu0, dwu1)
    g["w_down"] = (dwd0, dwd1)
    g["w_ple_gate"] = (dwg0, dwg1)
    g["w_ple_proj"] = (dwp0, dwp1)
    g["ln_mlp"] = jnp.concatenate([dlnm0, dlnm1], axis=0)
    g["ln_ple"] = jnp.concatenate([dlnp0, dlnp1], axis=0)
    return loss_blk, dx, g


ANY = pl.BlockSpec(memory_space=pl.ANY)


def _place():
    x, y, c = lax.axis_index("x"), lax.axis_index("y"), lax.axis_index("c")
    others = [(1 - x, y), (x, 1 - y), (1 - x, 1 - y)]
    return x, y, c, 2 * x + y, others


def cast_into_slot(w3, layer, slot, *, name, tm=256):
    _, r, c = w3.shape
    tm = min(tm, r)

    def body(slot_ref, w_ref, o_ref):
        o_ref[...] = w_ref[...].astype(BF16)

    return _pcall(body, name=name, out_shape=_sds((N_SHARDS, r, c), BF16), grid=(r // tm,), num_prefetch=1,
                  in_specs=[pl.BlockSpec((None, tm, c), lambda i, s: (layer, i, 0))],
                  out_specs=pl.BlockSpec((None, tm, c), lambda i, s: (s[0], i, 0)),
                  semantics=("parallel",))(slot, w3)


def gather_shards(mats, vecs, *, name):
    nm, nv = len(mats), len(vecs)
    halves = [m.reshape(N_SHARDS, 2, m.shape[1] // 2, m.shape[2]) for m in mats]

    def body(*refs):
        vsrc = refs[nm:nm + nv]
        out, vout = refs[nm + nv:2 * nm + nv], refs[2 * nm + nv:2 * (nm + nv)]
        send, recv, vsend, vrecv, loc = refs[2 * (nm + nv):]
        x, y, c, s_me, others = _place()
        sib = (x, y, 1 - c)

        def ici(l, k):
            ox, oy = others[k]
            return pltpu.make_async_remote_copy(out[l].at[s_me, c], out[l].at[s_me, c], send.at[l, k], recv.at[l, k],
                                                device_id=(ox, oy, c), device_id_type=MESH)

        def landed(l, k, half):
            ox, oy = others[k]
            return out[l].at[2 * ox + oy, half]

        def passed_on(l, k):
            return pltpu.make_async_remote_copy(landed(l, k, c), landed(l, k, c), send.at[l, 3 + k], recv.at[l, 3 + k],
                                                device_id=sib, device_id_type=MESH)

        def vec(l, k):
            ox, oy = others[k]
            return pltpu.make_async_remote_copy(vsrc[l], vout[l].at[s_me], vsend.at[l, k], vrecv.at[l, k],
                                                device_id=(ox, oy, c), device_id_type=MESH)

        for l in range(nm):
            for k in range(3):
                ici(l, k).start()
        for l in range(nv):
            for k in range(3):
                vec(l, k).start()
        for l in range(nv):
            own = pltpu.make_async_copy(vsrc[l], vout[l].at[s_me], loc)
            own.start()
            own.wait()
        for l in range(nm):
            for k in range(3):
                pltpu.make_async_remote_copy(landed(l, k, c), landed(l, k, c), send.at[l, k], recv.at[l, k],
                                             device_id=sib, device_id_type=MESH).wait_recv()
                passed_on(l, k).start()
        for l in range(nm):
            for k in range(3):
                pltpu.make_async_remote_copy(landed(l, k, 1 - c), landed(l, k, 1 - c), send.at[l, 3 + k],
                                             recv.at[l, 3 + k], device_id=sib, device_id_type=MESH).wait_recv()
        for l in range(nv):
            for k in range(3):
                ox, oy = others[k]
                pltpu.make_async_remote_copy(vsrc[l], vout[l].at[2 * ox + oy], vsend.at[l, k], vrecv.at[l, k],
                                             device_id=sib, device_id_type=MESH).wait_recv()
        for l in range(nm):
            for k in range(3):
                ici(l, k).wait_send()
                passed_on(l, k).wait_send()
        for l in range(nv):
            for k in range(3):
                vec(l, k).wait_send()

    out_shape = [_sds(h.shape, BF16) for h in halves] + [_sds((N_SHARDS,) + v.shape, F32) for v in vecs]
    res = _pcall(body, name=name, out_shape=out_shape, in_specs=[ANY] * (nm + nv), out_specs=[ANY] * (nm + nv),
                 scratch_shapes=[pltpu.SemaphoreType.DMA((max(nm, 1), 6)), pltpu.SemaphoreType.DMA((max(nm, 1), 6)),
                                 pltpu.SemaphoreType.DMA((max(nv, 1), 3)), pltpu.SemaphoreType.DMA((max(nv, 1), 3)),
                                 pltpu.SemaphoreType.DMA(())],
                 aliases={l: l for l in range(nm)}, side_effects=True)(*halves, *vecs)
    return [r.reshape(m.shape) for r, m in zip(res[:nm], mats)], list(res[nm:])


HBM = pl.BlockSpec(memory_space=pltpu.HBM)
SEM = pl.BlockSpec(memory_space=pltpu.SEMAPHORE)
DATAFLOW = pltpu.SideEffectType.DATAFLOW_SIDE_EFFECTING


def _split_call(body, *, name, out_shape, in_specs, out_specs, aliases):
    return pl.pallas_call(body, name=name, out_shape=out_shape, in_specs=in_specs, out_specs=out_specs,
                          input_output_aliases=aliases,
                          compiler_params=pltpu.CompilerParams(has_side_effects=DATAFLOW))


def _token_shape():
    return jax.ShapeDtypeStruct((8, LANES), F32)


def gather_start(mats, after, *, name):
    n = len(mats)
    halves = [pltpu.with_memory_space_constraint(m.reshape(N_SHARDS, 2, m.shape[1] // 2, m.shape[2]), pltpu.HBM)
              for m in mats]

    def body(*refs):
        send, recv = refs[n + 1], refs[n + 2]
        out, token = refs[n + 3:2 * n + 3], refs[2 * n + 3]
        x, y, c, s_me, others = _place()
        for l in range(n):
            for k in range(3):
                ox, oy = others[k]
                pltpu.make_async_remote_copy(out[l].at[s_me, c], out[l].at[s_me, c], send.at[3 * l + k],
                                             recv.at[3 * l + k], device_id=(ox, oy, c), device_id_type=MESH).start()
        token[...] = jnp.zeros_like(token)

    res = _split_call(
        body, name=name,
        out_shape=(pltpu.SemaphoreType.DMA((3 * n,)), pltpu.SemaphoreType.DMA((3 * n,)),
                   *[pltpu.HBM(h.shape, BF16) for h in halves], _token_shape()),
        in_specs=[HBM] * n + [ANY], out_specs=(SEM, SEM, *[HBM] * n, pl.BlockSpec(memory_space=pltpu.VMEM)),
        aliases={l: 2 + l for l in range(n)})(*halves, after)
    return res[0], res[1], list(res[2:2 + n]), res[2 + n]


def gather_pass_on(bufs, send_a, recv_a, after, *, name, base=0):
    n = len(bufs)

    def body(*refs):
        send_a, recv_a = refs[n], refs[n + 1]
        out = refs[n + 3:2 * n + 3]
        send_b, recv_b, token = refs[2 * n + 3:]
        x, y, c, s_me, others = _place()
        for l in range(n):
            for k in range(3):
                ox, oy = others[k]
                landed, i = out[l].at[2 * ox + oy, c], 3 * l + k
                pltpu.make_async_remote_copy(landed, landed, send_a.at[3 * base + i], recv_a.at[3 * base + i],
                                             device_id=(x, y, 1 - c), device_id_type=MESH).wait_recv()
                pltpu.make_async_remote_copy(landed, landed, send_b.at[i], recv_b.at[i],
                                             device_id=(x, y, 1 - c), device_id_type=MESH).start()
        for l in range(n):
            for k in range(3):
                mine, i = out[l].at[s_me, c], 3 * (base + l) + k
                pltpu.make_async_remote_copy(mine, mine, send_a.at[i], recv_a.at[i],
                                             device_id=(x, y, 1 - c), device_id_type=MESH).wait_send()
        token[...] = jnp.zeros_like(token)

    res = _split_call(
        body, name=name,
        out_shape=(*[pltpu.HBM(b.shape, BF16) for b in bufs], pltpu.SemaphoreType.DMA((3 * n,)),
                   pltpu.SemaphoreType.DMA((3 * n,)), _token_shape()),
        in_specs=[HBM] * n + [SEM, SEM, ANY],
        out_specs=(*[HBM] * n, SEM, SEM, pl.BlockSpec(memory_space=pltpu.VMEM)),
        aliases={l: l for l in range(n)})(*bufs, send_a, recv_a, after)
    return list(res[:n]), res[n], res[n + 1], res[n + 2]


def gather_finish(bufs, send_b, recv_b, after, shapes, *, name):
    n = len(bufs)

    def body(*refs):
        send_b, recv_b = refs[n], refs[n + 1]
        out = refs[n + 3:]
        x, y, c, _, others = _place()
        for l in range(n):
            for k in range(3):
                ox, oy = others[k]
                theirs, mine, i = out[l].at[2 * ox + oy, 1 - c], out[l].at[2 * ox + oy, c], 3 * l + k
                pltpu.make_async_remote_copy(theirs, theirs, send_b.at[i], recv_b.at[i],
                                             device_id=(x, y, 1 - c), device_id_type=MESH).wait_recv()
                pltpu.make_async_remote_copy(mine, mine, send_b.at[i], recv_b.at[i],
                                             device_id=(x, y, 1 - c), device_id_type=MESH).wait_send()

    res = _split_call(
        body, name=name, out_shape=tuple(pltpu.HBM(b.shape, BF16) for b in bufs),
        in_specs=[HBM] * n + [SEM, SEM, ANY], out_specs=tuple([HBM] * n),
        aliases={l: l for l in range(n)})(*bufs, send_b, recv_b, after)
    return [r.reshape(s) for r, s in zip(res, shapes)]


def exchange_start(srcs, dst_shapes, dst_dtype, plan, count, after, *, name):
    n, m = len(srcs), len(dst_shapes)
    srcs = [pltpu.with_memory_space_constraint(s, pltpu.HBM) for s in srcs]
    lands = [pltpu.with_memory_space_constraint(lax.empty(s, dst_dtype), pltpu.HBM) for s in dst_shapes]

    def body(*refs):
        send, recv = refs[n + m + 1], refs[n + m + 2]
        src, dst, token = refs[n + m + 3:2 * n + m + 3], refs[2 * n + m + 3:2 * (n + m) + 3], refs[2 * (n + m) + 3]
        for i, (s, d, dev) in enumerate(plan(_place(), src, dst)):
            pltpu.make_async_remote_copy(s, d, send.at[i], recv.at[i], device_id=dev, device_id_type=MESH).start()
        token[...] = jnp.zeros_like(token)

    res = _split_call(
        body, name=name,
        out_shape=(pltpu.SemaphoreType.DMA((count,)), pltpu.SemaphoreType.DMA((count,)),
                   *[pltpu.HBM(s.shape, s.dtype) for s in srcs], *[pltpu.HBM(s, dst_dtype) for s in dst_shapes],
                   _token_shape()),
        in_specs=[HBM] * (n + m) + [ANY],
        out_specs=(SEM, SEM, *[HBM] * (n + m), pl.BlockSpec(memory_space=pltpu.VMEM)),
        aliases={i: 2 + i for i in range(n + m)})(*srcs, *lands, after)
    return (list(res[2:2 + n]), list(res[2 + n:2 + n + m]), res[0], res[1], plan), res[2 + n + m]


def exchange_finish(state, after, *, name):
    srcs, lands, send, recv, plan = state
    n, m = len(srcs), len(lands)

    def body(*refs):
        send, recv = refs[n + m], refs[n + m + 1]
        src, dst = refs[n + m + 3:2 * n + m + 3], refs[2 * n + m + 3:]
        for i, (s, d, dev) in enumerate(plan(_place(), src, dst)):
            pltpu.make_async_remote_copy(s, d, send.at[i], recv.at[i], device_id=dev, device_id_type=MESH).wait()

    res = _split_call(
        body, name=name,
        out_shape=tuple(pltpu.HBM(a.shape, a.dtype) for a in srcs + lands),
        in_specs=[HBM] * (n + m) + [SEM, SEM, ANY], out_specs=tuple([HBM] * (n + m)),
        aliases={i: i for i in range(n + m)})(*srcs, *lands, send, recv, after)
    return list(res[:n]), list(res[n:])


def pair_plan(place, src, dst):
    x, y, c, _, _ = place
    return [(s.at[:, 1 - c], d, (x, y, 1 - c)) for s, d in zip(src, dst)]


def chip_plan(place, src, dst):
    x, y, c, _, others = place
    return [(s.at[2 * ox + oy], d.at[k], (ox, oy, c)) for s, d in zip(src, dst) for k, (ox, oy) in enumerate(others)]


def pair_exchange(grads, *, name):
    n = len(grads)

    def body(*refs):
        src, got = refs[:n], refs[n:2 * n]
        send, recv = refs[2 * n:]
        x, y, c, _, _ = _place()

        def swap(l):
            return pltpu.make_async_remote_copy(src[l].at[:, 1 - c], got[l], send.at[l], recv.at[l],
                                                device_id=(x, y, 1 - c), device_id_type=MESH)

        for l in range(n):
            swap(l).start()
        for l in range(n):
            swap(l).wait()

    res = _pcall(body, name=name, out_shape=[_sds((N_SHARDS,) + g.shape[2:], F32) for g in grads],
                 in_specs=[ANY] * n, out_specs=[ANY] * n,
                 scratch_shapes=[pltpu.SemaphoreType.DMA((n,)), pltpu.SemaphoreType.DMA((n,))],
                 side_effects=True)(*grads)
    return list(res)


def add_to_wire(mine, theirs, core, *, name, tm=256):
    s, _, r, c = mine.shape
    tm = min(tm, r)

    def body(core_ref, a_ref, b_ref, o_ref):
        o_ref[...] = (a_ref[...] + b_ref[...]).astype(BF16)

    spec = pl.BlockSpec((None, tm, c), lambda i, j, cr: (i, j, 0))
    return _pcall(body, name=name, out_shape=_sds((s, r, c), BF16), grid=(s, r // tm), num_prefetch=1,
                  in_specs=[pl.BlockSpec((None, None, tm, c), lambda i, j, cr: (i, cr[0], j, 0)), spec],
                  out_specs=spec, semantics=("parallel", "parallel"))(core, mine, theirs)


def sum_chips(wire, landed, place, dest, layer, n_layers, *, name, tm=256):
    _, r, c = wire.shape
    tm = min(tm, r)

    def body(place_ref, w_ref, l_ref, *rest):
        o_ref = rest[-1]
        o_ref[...] = ((w_ref[...].astype(F32) + l_ref[0].astype(F32)) + l_ref[1].astype(F32)) + l_ref[2].astype(F32)

    in_specs = [pl.BlockSpec((None, tm, c), lambda i, pr: (pr[0], i, 0)),
                pl.BlockSpec((3, tm, c), lambda i, pr: (0, i, 0))]
    args = [place, wire, landed]
    aliases = None
    if dest is not None:
        in_specs.append(ANY)
        args.append(dest)
        aliases = {3: 0}
    return _pcall(body, name=name, out_shape=_sds((n_layers, 2, r, c), F32), grid=(r // tm,), num_prefetch=1,
                  in_specs=in_specs,
                  out_specs=pl.BlockSpec((None, None, tm, c), lambda i, pr: (layer, pr[1], i, 0)),
                  aliases=aliases, semantics=("parallel",))(*args)


def pair_share(bufs, slots, *, name):
    n = len(bufs)

    def body(*refs):
        out = refs[n:2 * n]
        send, recv = refs[2 * n:]
        x, y, c, _, _ = _place()

        def share(i, half):
            o, l = slots[i]
            return pltpu.make_async_remote_copy(out[o].at[l, half], out[o].at[l, half], send.at[i], recv.at[i],
                                                device_id=(x, y, 1 - c), device_id_type=MESH)

        for i in range(len(slots)):
            share(i, c).start()
        for i in range(len(slots)):
            share(i, 1 - c).wait_recv()
            share(i, c).wait_send()

    res = _pcall(body, name=name, out_shape=[_sds(b.shape, F32) for b in bufs], in_specs=[ANY] * n,
                 out_specs=[ANY] * n,
                 scratch_shapes=[pltpu.SemaphoreType.DMA((len(slots),)), pltpu.SemaphoreType.DMA((len(slots),))],
                 aliases={o: o for o in range(n)}, side_effects=True)(*bufs)
    return list(res)


def all_reduce_small(packed, *, name):
    n_dev, r, c = packed.shape

    def body(in_ref, out_ref, land, send, recv):
        x, y, cc, _, _ = _place()
        me = 4 * x + 2 * y + cc
        peers = [(px, py, pc) for px in range(2) for py in range(2) for pc in range(2)]

        def scatter(d):
            return pltpu.make_async_remote_copy(in_ref.at[d], land.at[me], send.at[0, d], recv.at[0, me],
                                                device_id=peers[d], device_id_type=MESH)

        def gather(d):
            return pltpu.make_async_remote_copy(out_ref.at[me], out_ref.at[me], send.at[1, d], recv.at[1, me],
                                                device_id=peers[d], device_id_type=MESH)

        for d in range(n_dev):
            @pl.when(d != me)
            def _():
                scatter(d).start()
        land[me] = in_ref[me]
        for d in range(n_dev):
            @pl.when(d != me)
            def _():
                pltpu.make_async_remote_copy(in_ref.at[d], land.at[d], send.at[0, d], recv.at[0, d],
                                             device_id=peers[d], device_id_type=MESH).wait_recv()
        total = land[0]
        for d in range(1, n_dev):
            total = total + land[d]
        out_ref[me] = total
        for d in range(n_dev):
            @pl.when(d != me)
            def _():
                gather(d).start()
        for d in range(n_dev):
            @pl.when(d != me)
            def _():
                pltpu.make_async_remote_copy(out_ref.at[d], out_ref.at[d], send.at[1, d], recv.at[1, d],
                                             device_id=peers[d], device_id_type=MESH).wait_recv()
        for d in range(n_dev):
            @pl.when(d != me)
            def _():
                scatter(d).wait_send()
                gather(d).wait_send()

    vm = pl.BlockSpec(memory_space=pltpu.VMEM)
    return _pcall(body, name=name, out_shape=_sds(packed.shape, F32), in_specs=[vm], out_specs=vm,
                  scratch_shapes=[pltpu.VMEM(packed.shape, F32), pltpu.SemaphoreType.DMA((2, n_dev)),
                                  pltpu.SemaphoreType.DMA((2, n_dev))],
                  side_effects=True)(packed)


def adamw(w, g, m, v, *, name, part=None, dest=None, tm=256):
    shape = w.shape
    cols = shape[-1]
    rows = 1
    for s in shape[:-1]:
        rows *= s
    first, count = 0, rows
    if part is not None:
        count = rows // part[1]
        first = part[0] * count
    tm = min(tm, count)
    assert count % tm == 0
    two_d = lambda a: a.reshape(rows, cols)

    def body(w_ref, g_ref, m_ref, v_ref, *rest):
        d_ref, mo_ref, vo_ref = rest[-3:]
        gv = g_ref[...]
        m_new = ADAM_B1 * m_ref[...] + (1.0 - ADAM_B1) * gv
        v_new = ADAM_B2 * v_ref[...] + (1.0 - ADAM_B2) * (gv * gv)
        m_hat = m_new / (1.0 - ADAM_B1 ** ADAM_STEP)
        v_hat = v_new / (1.0 - ADAM_B2 ** ADAM_STEP)
        d_ref[...] = -ADAM_LR * (m_hat / (jnp.sqrt(v_hat) + ADAM_EPS) + ADAM_WD * w_ref[...])
        mo_ref[...] = m_new
        vo_ref[...] = v_new

    spec = pl.BlockSpec((tm, cols), lambda i: (first // tm + i, 0))
    args = [two_d(w), two_d(g), two_d(m), two_d(v)]
    in_specs = [spec] * 4
    aliases = None
    if dest is not None:
        args += [two_d(d) for d in dest]
        in_specs = in_specs + [ANY] * 3
        aliases = {4: 0, 5: 1, 6: 2}
    outs = _pcall(body, name=name, out_shape=[_sds((rows, cols), F32)] * 3, grid=(count // tm,), in_specs=in_specs,
                  out_specs=[spec] * 3, aliases=aliases, semantics=("parallel",))(*args)
    return [o.reshape(shape) for o in outs]


WEIGHTS = ("ln_mix_a", "w_in_a", "g_v_a", "w_spatial", "b_spatial", "w_out_a", "ln_kv", "w_kv", "g_k", "ln_mix_b",
           "w_q", "g_q", "w_out_b", "ln_mlp", "w_up", "w_down", "ln_ple", "w_ple_gate", "w_ple_proj")
MATRICES = (("w_in_a", 1, True), ("w_out_a", 1, False), ("w_kv", 0, True), ("w_q", 1, False), ("w_out_b", 1, False),
            ("w_up", 2, True), ("w_down", 2, False), ("w_ple_gate", 2, False), ("w_ple_proj", 2, True))
GATHER_STAGES = ((("w_in_a", 0),), (("w_out_a", 0),), (("w_up", 0),), (("w_down", 0),),
                 (("w_ple_gate", 0), ("w_ple_proj", 0), ("w_kv", 0)), (("w_q", 0),), (("w_out_b", 0),), (("w_up", 1),),
                 (("w_down", 1),), (("w_ple_gate", 1), ("w_ple_proj", 1)))
REPLICATED = ("w_spatial", "b_spatial", "ln_kv", "g_k", "ln_mix_b", "g_q", "ln_mlp", "ln_ple")
SHARDED_VECTORS = ("ln_mix_a", "g_v_a")
SMALL_ROWS = 18


def kernel(x, p, ln_mix_a, w_in_a, g_v_a, w_spatial, b_spatial, w_out_a, ln_kv, w_kv, g_k, ln_mix_b, w_q, g_q, w_out_b, ln_mlp, w_up, w_down, ln_ple, w_ple_gate, w_ple_proj, loss_target, m_ln_mix_a, m_w_in_a, m_g_v_a, m_w_spatial, m_b_spatial, m_w_out_a, m_ln_kv, m_w_kv, m_g_k, m_ln_mix_b, m_w_q, m_g_q, m_w_out_b, m_ln_mlp, m_w_up, m_w_down, m_ln_ple, m_w_ple_gate, m_w_ple_proj, v_ln_mix_a, v_w_in_a, v_g_v_a, v_w_spatial, v_b_spatial, v_w_out_a, v_ln_kv, v_w_kv, v_g_k, v_ln_mix_b, v_w_q, v_g_q, v_w_out_b, v_ln_mlp, v_w_up, v_w_down, v_ln_ple, v_w_ple_gate, v_w_ple_proj):
    given = dict(locals())
    weights = {n: given[n] for n in WEIGHTS}
    shard = 2 * lax.axis_index("x") + lax.axis_index("y")
    core = lax.axis_index("c")
    shard_1 = shard.astype(jnp.int32).reshape(1)
    core_1 = core.astype(jnp.int32).reshape(1)
    place = jnp.stack([shard, core]).astype(jnp.int32)

    leaves = []
    for name, layers, cols in MATRICES:
        w3 = weights[name] if layers else weights[name][None]
        for layer in range(max(layers, 1)):
            leaves.append((name, layer, cols, cast_into_slot(w3, layer, shard_1, name=f"cast_{name}_{layer}")))
    by_key = {(lf[0], lf[1]): lf for lf in leaves}
    ordered = [by_key[key] for stage in GATHER_STAGES for key in stage]
    _, vec_a = gather_shards([], [ln_mix_a, g_v_a], name="gather_vectors")
    send_a, recv_a, flying, token = gather_start([lf[3] for lf in ordered], vec_a[0], name="gather_start")

    w = {"ln_mix_a": vec_a[0].reshape(1, D_MODEL) + token[0, 0],
         "g_v_a": vec_a[1].reshape(1, D_MODEL)}
    for name in REPLICATED:
        w[name] = weights[name]

    class Late:
        def weights(self, name, layer, after):
            stage = [(name, layer) in s for s in GATHER_STAGES].index(True)
            base = sum(len(s) for s in GATHER_STAGES[:stage])
            members = ordered[base:base + len(GATHER_STAGES[stage])]
            bufs, send_b, recv_b, tok = gather_pass_on(flying[base:base + len(members)], send_a, recv_a, after,
                                                       name=f"gather_pass_on_{stage}", base=base)
            got = gather_finish(bufs, send_b, recv_b, tok, [lf[3].shape for lf in members],
                                name=f"gather_finish_{stage}")
            out = {}
            for (leaf_name, leaf_layer, cols, _), arr in zip(members, got):
                out[(leaf_name, leaf_layer)] = arr if cols else arr.reshape(N_SHARDS * arr.shape[1], arr.shape[2])
            return out

        groups = []

        def pair_start(self, grads_done, after):
            self.keys = sorted(grads_done)
            views = [view(k, grads_done[k]) for k in self.keys]
            self.pair, token = exchange_start(views, [(N_SHARDS,) + v.shape[2:] for v in views], F32, pair_plan,
                                              len(views), after, name=f"grad_pair_start_{len(self.groups)}")
            return token

        def chip_start(self, after):
            tag = len(self.groups)
            mine, theirs = exchange_finish(self.pair, after, name=f"grad_pair_finish_{tag}")
            wire = [add_to_wire(a, b, core_1, name=f"grad_pair_sum_{tag}_{i}")
                    for i, (a, b) in enumerate(zip(mine, theirs))]
            chip, token = exchange_start(wire, [(3,) + v.shape[1:] for v in wire], BF16, chip_plan, 3 * len(wire),
                                         wire[-1], name=f"grad_chip_start_{tag}")
            self.groups.append((self.keys, chip))
            return token

    col_sharded = {name: cols for name, _, cols in MATRICES}
    layer_count = {name: max(layers, 1) for name, layers, _ in MATRICES}

    def view(key, arr):
        rows = arr.shape[-2] if col_sharded[key[0]] else arr.shape[0] // N_SHARDS
        return arr.reshape(N_SHARDS, 2, rows // 2, arr.shape[-1])

    t = x.shape[1]
    late = Late()
    loss_blk, dx, g = local_step(x[0], p.reshape(2, t, PLE_DIM), loss_target[0], w, late)
    loss = lax.psum(loss_blk[0, 0], ("x", "y", "c"))

    sent = {k for keys, _ in late.groups for k in keys}
    keys_last = [(name, layer) for name, layers, _ in MATRICES for layer in range(max(layers, 1))
                 if (name, layer) not in sent]
    views = [view(k, g[k[0]][k[1]] if layer_count[k[0]] == 2 else g[k[0]]) for k in keys_last]

    grads = {}
    small = REPLICATED + SHARDED_VECTORS
    flat = jnp.concatenate([g[n].reshape(-1) for n in small])
    room = 8 * SMALL_ROWS * D_MODEL
    flat = jnp.concatenate([flat, jnp.zeros((room - flat.shape[0],), F32)])
    flat, views = lax.optimization_barrier((flat, views))
    reduced = all_reduce_small(flat.reshape(8, SMALL_ROWS, D_MODEL), name="grad_small_all_reduce").reshape(-1)
    theirs = pair_exchange(views, name="grad_pair_exchange_last")
    wire_0 = [add_to_wire(a, b, core_1, name=f"grad_pair_sum_last_{i}") for i, (a, b) in enumerate(zip(views, theirs))]
    at = 0
    for n in small:
        size = g[n].size
        piece = reduced[at:at + size]
        at += size
        if n in SHARDED_VECTORS:
            per = D_MODEL // N_SHARDS
            grads[n] = lax.dynamic_slice(piece, (shard * per,), (per,)).reshape(weights[n].shape)
        else:
            grads[n] = piece.reshape(weights[n].shape)

    chip_0, token_0 = exchange_start(wire_0, [(3,) + v.shape[1:] for v in wire_0], BF16, chip_plan, 3 * len(wire_0),
                                     reduced, name="grad_chip_start_last")

    bufs = {}

    def sum_and_share(keys, wire, landed, tag):
        for i, (key, wv, lv) in enumerate(zip(keys, wire, landed)):
            name, layer = key
            bufs[name] = sum_chips(wv, lv, place, bufs.get(name), layer, layer_count[name],
                                   name=f"grad_chip_sum_{tag}_{i}")
        names = sorted({k[0] for k in keys})
        shared = pair_share([bufs[n] for n in names], [(names.index(k[0]), k[1]) for k in keys],
                            name=f"grad_pair_share_{tag}")
        bufs.update(zip(names, shared))

    updates = {}

    def update(n, gn, part=None):
        wn, mn, vn = weights[n], given["m_" + n], given["v_" + n]
        if wn.ndim == 1:
            wn, gn, mn, vn = (a.reshape(1, -1) for a in (wn, gn, mn, vn))
        tag = "" if part is None else f"_{part[0]}"
        updates[n] = adamw(wn, gn.reshape(wn.shape), mn, vn, name=f"adamw_{n}{tag}", part=part, dest=updates.get(n))

    for n in small:
        update(n, grads[n])
    after = token_0
    for tag, (keys, chip) in enumerate(late.groups + [(keys_last, chip_0)]):
        wire, landed = exchange_finish(chip, after, name=f"grad_chip_finish_{tag}")
        sum_and_share(keys, wire, landed, tag)
        for name, layer in keys:
            update(name, bufs[name], (layer, layer_count[name]) if layer_count[name] == 2 else None)
        after = updates[keys[-1][0]][0]
    for name, _, _ in MATRICES:
        grads[name] = bufs[name].reshape(weights[name].shape)
    delta = {n: updates[n][0].reshape(weights[n].shape) for n in WEIGHTS}
    new_m = {n: updates[n][1].reshape(weights[n].shape) for n in WEIGHTS}
    new_v = {n: updates[n][2].reshape(weights[n].shape) for n in WEIGHTS}
    return (loss, dx.reshape(x.shape), *[grads[n] for n in WEIGHTS], *[delta[n] for n in WEIGHTS],
            *[new_m[n] for n in WEIGHTS], *[new_v[n] for n in WEIGHTS])
```

```python
import jax
import jax.numpy as jnp
from jax import lax
from jax.experimental import pallas as pl
from jax.experimental.pallas import tpu as pltpu

F32 = jnp.float32
BF16 = jnp.bfloat16

D_MODEL = 1024
D_FF = 4096
PLE_DIM = 256
N_GROUPS = 8
CHUNK = 128
HEAD_DIM = 64
LANES = 128
ATT_BLOCK = 256
EPS = 1e-6
N_SHARDS = 4
VMEM_LIMIT = 56 * 1024 * 1024

ADAM_LR = 0.001
ADAM_B1 = 0.9
ADAM_B2 = 0.999
ADAM_EPS = 1e-08
ADAM_WD = 0.01
ADAM_STEP = 10

MESH = pl.DeviceIdType.MESH


def _pcall(body, *, name, out_shape, grid=None, in_specs=None, out_specs=None, scratch_shapes=(),
           semantics=None, aliases=None, side_effects=False, num_prefetch=0):
    params = dict(vmem_limit_bytes=VMEM_LIMIT)
    if semantics is not None:
        params["dimension_semantics"] = semantics
    if side_effects:
        params["has_side_effects"] = True
    kwargs = {}
    if aliases:
        kwargs["input_output_aliases"] = aliases
    if num_prefetch:
        spec = pltpu.PrefetchScalarGridSpec(num_scalar_prefetch=num_prefetch, grid=grid, in_specs=in_specs,
                                            out_specs=out_specs, scratch_shapes=list(scratch_shapes))
        return pl.pallas_call(body, name=name, out_shape=out_shape, grid_spec=spec,
                              compiler_params=pltpu.CompilerParams(**params), **kwargs)
    if grid is not None:
        kwargs["grid"] = grid
    if in_specs is not None:
        kwargs["in_specs"] = in_specs
    if out_specs is not None:
        kwargs["out_specs"] = out_specs
    if aliases:
        kwargs["input_output_aliases"] = aliases
    return pl.pallas_call(body, name=name, out_shape=out_shape, scratch_shapes=list(scratch_shapes),
                          compiler_params=pltpu.CompilerParams(**params), **kwargs)


def _sds(shape, dtype):
    return jax.ShapeDtypeStruct(shape, dtype)


_GELU_C = 0.7978845608028654
_GELU_A = 0.044715


def _gelu(x):
    inner = _GELU_C * (x + _GELU_A * (x * x * x))
    return 0.5 * x * (1.0 + jnp.tanh(inner))


def _gelu_grad(x):
    x2 = x * x
    t = jnp.tanh(_GELU_C * (x + _GELU_A * (x2 * x)))
    return 0.5 * (1.0 + t) + 0.5 * x * (1.0 - t * t) * (_GELU_C * (1.0 + 3.0 * _GELU_A * x2))


def _sigmoid(x):
    return 1.0 / (1.0 + jnp.exp(-x))


def _log_sigmoid(z):
    return jnp.minimum(z, 0.0) - jnp.log(1.0 + jnp.exp(-jnp.abs(z)))


def _dot(a, b):
    return jnp.dot(a, b, preferred_element_type=F32)


def _dot_nt(a, b):
    return lax.dot_general(a, b, (((1,), (1,)), ((), ())), preferred_element_type=F32)


def _dot_tn(a, b):
    return lax.dot_general(a, b, (((0,), (0,)), ((), ())), preferred_element_type=F32)


def _head_rstd(x):
    lane = lax.broadcasted_iota(jnp.int32, x.shape, 1)
    low = lane < HEAD_DIM
    sq = x * x
    s_lo = jnp.sum(jnp.where(low, sq, 0.0), axis=-1, keepdims=True)
    s_hi = jnp.sum(jnp.where(low, 0.0, sq), axis=-1, keepdims=True)
    ms = jnp.where(low, s_lo, s_hi) * (1.0 / HEAD_DIM)
    return lax.rsqrt(ms + EPS)


def _head_mean(x):
    lane = lax.broadcasted_iota(jnp.int32, x.shape, 1)
    low = lane < HEAD_DIM
    s_lo = jnp.sum(jnp.where(low, x, 0.0), axis=-1, keepdims=True)
    s_hi = jnp.sum(jnp.where(low, 0.0, x), axis=-1, keepdims=True)
    return jnp.where(low, s_lo, s_hi) * (1.0 / HEAD_DIM)


def _full(shape):
    zeros = (0,) * len(shape)
    return pl.BlockSpec(shape, lambda i: zeros)


def norm_matmul(x, g, w, *, name, epilogue="none", tm=512):
    t, d = x.shape
    sharded = w.ndim == 3
    per = w.shape[2] if sharded else w.shape[1]
    n = N_SHARDS * per if sharded else per
    tm = min(tm, t)

    def body(x_ref, g_ref, w_ref, h_ref, r_ref, *outs):
        xv = x_ref[...]
        r = lax.rsqrt(jnp.mean(xv * xv, axis=-1, keepdims=True) + EPS)
        h = ((xv * r) * g_ref[...]).astype(BF16)
        h_ref[...] = h
        r_ref[...] = r
        for s in range(N_SHARDS if sharded else 1):
            cols = slice(s * per, (s + 1) * per)
            y = _dot(h, w_ref[s] if sharded else w_ref[...])
            if epilogue == "none":
                outs[0][:, cols] = y
            else:
                a = jnp.maximum(y, 0.0)
                outs[0][:, cols] = a.astype(BF16)
                outs[1][:, cols] = (a * a).astype(BF16)

    row = lambda i: (i, 0)
    out_shape = [_sds((t, d), BF16), _sds((t, 1), F32)]
    out_specs = [pl.BlockSpec((tm, d), row), pl.BlockSpec((tm, 1), row)]
    if epilogue == "none":
        out_shape.append(_sds((t, n), F32))
        out_specs.append(pl.BlockSpec((tm, n), row))
    else:
        out_shape += [_sds((t, n), BF16), _sds((t, n), BF16)]
        out_specs += [pl.BlockSpec((tm, n), row)] * 2
    return _pcall(
        body, name=name, out_shape=out_shape, grid=(t // tm,),
        in_specs=[pl.BlockSpec((tm, d), row), _full((1, d)), _full(w.shape)],
        out_specs=out_specs, semantics=("parallel",))(x, g, w)


def matmul_residual(a, w, res, *, name, tm=512):
    t, k = a.shape
    n = w.shape[1]
    tm = min(tm, t)

    def body(a_ref, w_ref, res_ref, o_ref):
        o_ref[...] = res_ref[...] + _dot(a_ref[...], w_ref[...])

    row = lambda i: (i, 0)
    return _pcall(
        body, name=name, out_shape=_sds((t, n), F32), grid=(t // tm,),
        in_specs=[pl.BlockSpec((tm, k), row), _full(w.shape), pl.BlockSpec((tm, n), row)],
        out_specs=pl.BlockSpec((tm, n), row), semantics=("parallel",))(a, w, res)


def ple_forward(x, g, w_gate, p, w_proj, *, name, tm=256):
    t, d = x.shape
    tm = min(tm, t)

    def body(x_ref, g_ref, wg_ref, p_ref, wp_ref, h_ref, r_ref, gate_ref, pp_ref, o_ref):
        xv = x_ref[...]
        r = lax.rsqrt(jnp.mean(xv * xv, axis=-1, keepdims=True) + EPS)
        h = ((xv * r) * g_ref[...]).astype(BF16)
        h_ref[...] = h
        r_ref[...] = r
        gate = _sigmoid(_dot(h, wg_ref[...]))
        gate_ref[...] = gate
        pb = p_ref[...].astype(BF16)
        per = d // N_SHARDS
        for s in range(N_SHARDS):
            cols = slice(s * per, (s + 1) * per)
            pp = _dot(pb, wp_ref[s])
            pp_ref[:, cols] = pp.astype(BF16)
            o_ref[:, cols] = xv[:, cols] + pp * gate[:, cols]

    row = lambda i: (i, 0)
    fixed = lambda i: (0, 0)
    return _pcall(
        body, name=name,
        out_shape=[_sds((t, d), BF16), _sds((t, 1), F32), _sds((t, d), F32), _sds((t, d), BF16), _sds((t, d), F32)],
        grid=(t // tm,),
        in_specs=[pl.BlockSpec((tm, d), row), pl.BlockSpec((1, d), fixed), pl.BlockSpec((d, d), fixed),
                  pl.BlockSpec((tm, PLE_DIM), row),
                  pl.BlockSpec((N_SHARDS, PLE_DIM, d // N_SHARDS), lambda i: (0, 0, 0))],
        out_specs=[pl.BlockSpec((tm, d), row), pl.BlockSpec((tm, 1), row), pl.BlockSpec((tm, d), row),
                   pl.BlockSpec((tm, d), row), pl.BlockSpec((tm, d), row)],
        semantics=("parallel",))(x, g, w_gate, p, w_proj)


def _tril_mask():
    r = lax.broadcasted_iota(jnp.int32, (CHUNK, CHUNK), 0)
    c = lax.broadcasted_iota(jnp.int32, (CHUNK, CHUNK), 1)
    return c <= r


def _sgu_common(pre_ref, gv_ref, ws_ref):
    pre = pre_ref[...]
    pre_u, pre_v = pre[:, :D_MODEL], pre[:, D_MODEL:]
    u = _gelu(pre_u)
    v = _gelu(pre_v)
    r = lax.rsqrt(jnp.mean(v * v, axis=-1, keepdims=True) + EPS)
    vhat = v * r
    vn = (vhat * gv_ref[...]).astype(BF16)
    tril = _tril_mask()
    wm = [jnp.where(tril, ws_ref[g], 0.0).astype(BF16) for g in range(N_GROUPS)]
    return pre_u, pre_v, u, r, vhat, vn, wm, tril


def sgu_forward(pre, g_v, w_s, b_full, *, name):
    t = pre.shape[0]

    def body(pre_ref, gv_ref, ws_ref, b_ref, y_ref):
        _, _, u, _, _, vn, wm, _ = _sgu_common(pre_ref, gv_ref, ws_ref)
        for g in range(N_GROUPS):
            cols = slice(g * LANES, (g + 1) * LANES)
            mix = _dot(wm[g], vn[:, cols]) + b_ref[:, cols]
            y_ref[:, cols] = (u[:, cols] * mix).astype(BF16)

    return _pcall(
        body, name=name, out_shape=_sds((t, D_MODEL), BF16), grid=(t // CHUNK,),
        in_specs=[pl.BlockSpec((CHUNK, 2 * D_MODEL), lambda i: (i, 0)), pl.BlockSpec((1, D_MODEL), lambda i: (0, 0)),
                  pl.BlockSpec((N_GROUPS, CHUNK, CHUNK), lambda i: (0, 0, 0)),
                  pl.BlockSpec((CHUNK, D_MODEL), lambda i: (0, 0))],
        out_specs=pl.BlockSpec((CHUNK, D_MODEL), lambda i: (i, 0)),
        semantics=("parallel",))(pre, g_v, w_s, b_full)


def head_norm(pre, g128, *, name, col_block=0, scale=1.0, passthrough=False, tm=512):
    t = pre.shape[0]
    tm = min(tm, t)

    def body(*refs):
        if passthrough:
            x_ref, v_ref, g_ref, o_ref, vo_ref = refs
            vo_ref[...] = v_ref[...].astype(BF16)
        else:
            x_ref, g_ref, o_ref = refs
        g = g_ref[...] * scale
        for b in range(D_MODEL // LANES):
            cols = slice(b * LANES, (b + 1) * LANES)
            xv = x_ref[:, cols]
            o_ref[:, cols] = ((xv * _head_rstd(xv)) * g).astype(BF16)

    x_spec = pl.BlockSpec((tm, D_MODEL), lambda i: (i, col_block))
    g_spec = pl.BlockSpec((1, LANES), lambda i: (0, 0))
    o_spec = pl.BlockSpec((tm, D_MODEL), lambda i: (i, 0))
    if passthrough:
        return _pcall(body, name=name, out_shape=[_sds((t, D_MODEL), BF16)] * 2, grid=(t // tm,),
                      in_specs=[x_spec, pl.BlockSpec((tm, D_MODEL), lambda i: (i, 1)), g_spec],
                      out_specs=[o_spec, o_spec], semantics=("parallel",))(pre, pre, g128)
    return _pcall(body, name=name, out_shape=_sds((t, D_MODEL), BF16), grid=(t // tm,),
                  in_specs=[x_spec, g_spec], out_specs=o_spec, semantics=("parallel",))(pre, g128)


def _suffix_matrix(n):
    r = lax.broadcasted_iota(jnp.int32, (n, n), 0)
    c = lax.broadcasted_iota(jnp.int32, (n, n), 1)
    return jnp.where(r > c, 1.0, 0.0).astype(BF16)


def _prefix_matrix(n):
    r = lax.broadcasted_iota(jnp.int32, (n, n), 0)
    c = lax.broadcasted_iota(jnp.int32, (n, n), 1)
    return jnp.where(r < c, 1.0, 0.0).astype(BF16)


def _block_cumsum(a, tri):
    return _dot(a.astype(BF16), tri)


def _stacked_causal(n):
    r = lax.broadcasted_iota(jnp.int32, (2 * n, n), 0)
    c = lax.broadcasted_iota(jnp.int32, (2 * n, n), 1)
    return c < jnp.where(r >= n, r - n, r)


def _stack_heads(a, low):
    zero = jnp.zeros_like(a)
    return jnp.concatenate([jnp.where(low, a, zero), jnp.where(low, zero, a)], axis=0)


def stick_breaking_forward(q, k, v, *, name):
    t = q.shape[0]
    blk = min(ATT_BLOCK, t)
    nq = t // blk

    def body(q_ref, k_ref, v_ref, o_ref):
        i = pl.program_id(1)
        low = lax.broadcasted_iota(jnp.int32, (blk, LANES), 1) < HEAD_DIM
        tri = _suffix_matrix(blk)
        causal = _stacked_causal(blk)
        qs = _stack_heads(q_ref[...], low)

        def block(j, carry, acc, masked):
            rows = pl.ds(pl.multiple_of(j * blk, blk), blk)
            z = _dot_nt(qs, k_ref[rows, :])
            ls = _log_sigmoid(z)
            lg = ls - z
            if masked:
                lg = jnp.where(causal, lg, 0.0)
            s = ls + _block_cumsum(lg, tri) + carry
            a = jnp.exp(s)
            if masked:
                a = jnp.where(causal, a, 0.0)
            acc = acc + _dot(a.astype(BF16), v_ref[rows, :])
            return carry + jnp.sum(lg, axis=-1, keepdims=True), acc

        state = block(i, jnp.zeros((2 * blk, 1), F32), jnp.zeros((2 * blk, LANES), F32), True)

        def two_blocks(n, st):
            st = block(i - 1 - 2 * n, st[0], st[1], False)
            return block(i - 2 - 2 * n, st[0], st[1], False)

        state = lax.fori_loop(0, i // 2, two_blocks, state)
        _, acc = lax.fori_loop(0, i % 2, lambda n, st: block(0, st[0], st[1], False), state)
        o_ref[...] = jnp.where(low, acc[:blk], acc[blk:]).astype(BF16)

    return _pcall(
        body, name=name, out_shape=_sds((t, D_MODEL), BF16), grid=(D_MODEL // LANES, nq),
        in_specs=[pl.BlockSpec((blk, LANES), lambda p, i: (i, p)), pl.BlockSpec((t, LANES), lambda p, i: (0, p)),
                  pl.BlockSpec((t, LANES), lambda p, i: (0, p))],
        out_specs=pl.BlockSpec((blk, LANES), lambda p, i: (i, p)),
        semantics=("parallel", "arbitrary"))(q, k, v)


def loss_forward(x, target, *, name, tm=512):
    t, d = x.shape
    tm = min(tm, t)

    def body(x_ref, t_ref, l_ref, dx_ref):
        @pl.when(pl.program_id(0) == 0)
        def _():
            l_ref[...] = jnp.zeros_like(l_ref)

        diff = x_ref[...] - t_ref[...]
        dx_ref[...] = diff * (1.0 / d)
        l_ref[...] += 0.5 * jnp.sum(jnp.mean(diff * diff, axis=-1, keepdims=True))

    return _pcall(
        body, name=name, out_shape=[_sds((8, LANES), F32), _sds((t, d), F32)], grid=(t // tm,),
        in_specs=[pl.BlockSpec((tm, d), lambda i: (i, 0))] * 2,
        out_specs=[pl.BlockSpec((8, LANES), lambda i: (0, 0)), pl.BlockSpec((tm, d), lambda i: (i, 0))],
        semantics=("arbitrary",))(x, target)


def matmul_nt(dy, w, *, name, mul=None, out_dtype=F32, tm=512):
    t, n = dy.shape
    k = w.shape[0]
    tm = min(tm, t)

    def body(*refs):
        if mul is None:
            dy_ref, w_ref, o_ref = refs
        else:
            dy_ref, w_ref, m_ref, o_ref = refs
        y = _dot_nt(dy_ref[...].astype(BF16), w_ref[...])
        if mul is not None:
            y = y * (2.0 * m_ref[...].astype(F32))
        o_ref[...] = y.astype(out_dtype)

    row = lambda i: (i, 0)
    in_specs = [pl.BlockSpec((tm, n), row), _full(w.shape)]
    args = [dy, w]
    if mul is not None:
        in_specs.append(pl.BlockSpec((tm, k), row))
        args.append(mul)
    return _pcall(body, name=name, out_shape=_sds((t, k), out_dtype), grid=(t // tm,), in_specs=in_specs,
                  out_specs=pl.BlockSpec((tm, k), row), semantics=("parallel",))(*args)


def matmul_tn(a, dy, *, name, col_shards, tk=512):
    t, k = a.shape
    n = dy.shape[1]
    if col_shards:
        tn = n // N_SHARDS

        def body(a_ref, dy_ref, o_ref):
            o_ref[...] = _dot_tn(a_ref[...].astype(BF16), dy_ref[...].astype(BF16))

        return _pcall(body, name=name, out_shape=_sds((N_SHARDS, k, tn), F32), grid=(N_SHARDS,),
                      in_specs=[_full((t, k)), pl.BlockSpec((t, tn), lambda j: (0, j))],
                      out_specs=pl.BlockSpec((None, k, tn), lambda j: (j, 0, 0)), semantics=("parallel",))(a, dy)

    tk = min(tk, k)

    def body(a_ref, dy_ref, o_ref, dy_bf):
        @pl.when(pl.program_id(0) == 0)
        def _():
            dy_bf[...] = dy_ref[...].astype(BF16)

        o_ref[...] = _dot_tn(a_ref[...].astype(BF16), dy_bf[...])

    return _pcall(body, name=name, out_shape=_sds((k, n), F32), grid=(k // tk,),
                  in_specs=[pl.BlockSpec((t, tk), lambda i: (0, i)), _full((t, n))],
                  out_specs=pl.BlockSpec((tk, n), lambda i: (i, 0)),
                  scratch_shapes=[pltpu.VMEM((t, n), BF16)], semantics=("arbitrary",))(a, dy)


def norm_backward(dpre, w, x, g, rstd, dx_out, *, name, tm=512):
    t, d = x.shape
    n = dpre.shape[1]
    tm = min(tm, t)
    if w.ndim == 3:
        w_spec = pl.BlockSpec(w.shape, lambda i: (0, 0, 0))
    else:
        w_spec = pl.BlockSpec(w.shape, lambda i: (0, 0))

    def body(dp_ref, w_ref, x_ref, g_ref, r_ref, dxo_ref, dx_ref, dg_ref):
        @pl.when(pl.program_id(0) == 0)
        def _():
            dg_ref[...] = jnp.zeros_like(dg_ref)

        if w.ndim == 3:
            per = n // N_SHARDS
            dh = _dot_nt(dp_ref[:, 0:per], w_ref[0])
            for s in range(1, N_SHARDS):
                dh = dh + _dot_nt(dp_ref[:, s * per:(s + 1) * per], w_ref[s])
        else:
            dh = _dot_nt(dp_ref[...], w_ref[...])
        r = r_ref[...]
        xn = x_ref[...] * r
        dg_ref[...] += jnp.sum(dh * xn, axis=0, keepdims=True)
        dxn = dh * g_ref[...]
        dx = r * (dxn - xn * jnp.mean(dxn * xn, axis=-1, keepdims=True))
        dx_ref[...] = dxo_ref[...] + dx

    row = lambda i: (i, 0)
    fixed = lambda i: (0, 0)
    return _pcall(
        body, name=name, out_shape=[_sds((t, d), F32), _sds((1, d), F32)], grid=(t // tm,),
        in_specs=[pl.BlockSpec((tm, n), row), w_spec, pl.BlockSpec((tm, d), row),
                  pl.BlockSpec((1, d), fixed), pl.BlockSpec((tm, 1), row), pl.BlockSpec((tm, d), row)],
        out_specs=[pl.BlockSpec((tm, d), row), pl.BlockSpec((1, d), fixed)],
        semantics=("arbitrary",))(dpre, w, x, g, rstd, dx_out)


def ple_backward(dx, gate, pp, *, name, tm=512):
    t, d = dx.shape
    tm = min(tm, t)

    def body(dx_ref, gate_ref, pp_ref, dg_ref, dp_ref):
        dxv = dx_ref[...]
        gate = gate_ref[...]
        dg_ref[...] = (dxv * pp_ref[...].astype(F32) * (gate * (1.0 - gate))).astype(BF16)
        dp_ref[...] = (dxv * gate).astype(BF16)

    spec = pl.BlockSpec((tm, d), lambda i: (i, 0))
    return _pcall(body, name=name, out_shape=[_sds((t, d), BF16)] * 2, grid=(t // tm,), in_specs=[spec] * 3,
                  out_specs=[spec] * 2, semantics=("parallel",))(dx, gate, pp)


def sgu_backward(dy, pre, g_v, w_s, b_full, *, name):
    t = pre.shape[0]
    n_chunks = t // CHUNK

    def body(dy_ref, pre_ref, gv_ref, ws_ref, b_ref, dpre_ref, dws_ref, db_ref, dgv_ref, dvn_s, dbf_s):
        step = pl.program_id(0)

        @pl.when(step == 0)
        def _():
            dws_ref[...] = jnp.zeros_like(dws_ref)
            dgv_ref[...] = jnp.zeros_like(dgv_ref)
            dbf_s[...] = jnp.zeros_like(dbf_s)

        pre_u, pre_v, u, r, vhat, vn, wm, tril = _sgu_common(pre_ref, gv_ref, ws_ref)
        dyv = dy_ref[...]
        for g in range(N_GROUPS):
            cols = slice(g * LANES, (g + 1) * LANES)
            mix = _dot(wm[g], vn[:, cols]) + b_ref[:, cols]
            dmix = dyv[:, cols] * u[:, cols]
            dmix_b = dmix.astype(BF16)
            du = dyv[:, cols] * mix
            dpre_ref[:, cols] = (du * _gelu_grad(pre_u[:, cols])).astype(BF16)
            dws_ref[g] += jnp.where(tril, _dot_nt(dmix_b, vn[:, cols]), 0.0)
            dbf_s[:, cols] += dmix
            dvn_s[:, cols] = _dot_tn(wm[g], dmix_b)
        dvn = dvn_s[...]
        dgv_ref[...] += jnp.sum(dvn * vhat, axis=0, keepdims=True)
        dxn = dvn * gv_ref[...]
        dv = r * (dxn - vhat * jnp.mean(dxn * vhat, axis=-1, keepdims=True))
        dpre_ref[:, D_MODEL:] = (dv * _gelu_grad(pre_v)).astype(BF16)

        @pl.when(step == n_chunks - 1)
        def _():
            lane = lax.broadcasted_iota(jnp.int32, (CHUNK, LANES), 1)
            acc = jnp.zeros((CHUNK, LANES), F32)
            for g in range(N_GROUPS):
                s = jnp.sum(dbf_s[:, g * LANES:(g + 1) * LANES], axis=-1, keepdims=True)
                acc = jnp.where(lane == g, s, acc)
            db_ref[...] = acc

    fixed2 = lambda i: (0, 0)
    return _pcall(
        body, name=name,
        out_shape=[_sds((t, 2 * D_MODEL), BF16), _sds((N_GROUPS, CHUNK, CHUNK), F32), _sds((CHUNK, LANES), F32),
                   _sds((1, D_MODEL), F32)],
        grid=(n_chunks,),
        in_specs=[pl.BlockSpec((CHUNK, D_MODEL), lambda i: (i, 0)), pl.BlockSpec((CHUNK, 2 * D_MODEL), lambda i: (i, 0)),
                  pl.BlockSpec((1, D_MODEL), fixed2), pl.BlockSpec((N_GROUPS, CHUNK, CHUNK), lambda i: (0, 0, 0)),
                  pl.BlockSpec((CHUNK, D_MODEL), fixed2)],
        out_specs=[pl.BlockSpec((CHUNK, 2 * D_MODEL), lambda i: (i, 0)),
                   pl.BlockSpec((N_GROUPS, CHUNK, CHUNK), lambda i: (0, 0, 0)), pl.BlockSpec((CHUNK, LANES), fixed2),
                   pl.BlockSpec((1, D_MODEL), fixed2)],
        scratch_shapes=[pltpu.VMEM((CHUNK, D_MODEL), F32), pltpu.VMEM((CHUNK, D_MODEL), F32)],
        semantics=("arbitrary",))(dy, pre, g_v, w_s, b_full)


def head_norm_backward(dy, pre, g128, *, name, col_block=0, scale=1.0, passthrough=None, tm=512):
    t = dy.shape[0]
    tm = min(tm, t)
    width = 2 * D_MODEL if passthrough is not None else D_MODEL

    def body(*refs):
        if passthrough is not None:
            dy_ref, x_ref, g_ref, dv_ref, o_ref, dg_ref = refs
            o_ref[:, D_MODEL:] = dv_ref[...].astype(BF16)
        else:
            dy_ref, x_ref, g_ref, o_ref, dg_ref = refs

        @pl.when(pl.program_id(0) == 0)
        def _():
            dg_ref[...] = jnp.zeros_like(dg_ref)

        g = g_ref[...]
        dg = jnp.zeros((1, LANES), F32)
        for b in range(D_MODEL // LANES):
            cols = slice(b * LANES, (b + 1) * LANES)
            xv = x_ref[:, cols]
            r = _head_rstd(xv)
            xn = xv * r
            dyv = dy_ref[:, cols] * scale
            dg = dg + jnp.sum(dyv * xn, axis=0, keepdims=True)
            dxn = dyv * g
            o_ref[:, cols] = (r * (dxn - xn * _head_mean(dxn * xn))).astype(BF16)
        dg_ref[...] += dg

    row = lambda i: (i, 0)
    in_specs = [pl.BlockSpec((tm, D_MODEL), row), pl.BlockSpec((tm, D_MODEL), lambda i: (i, col_block)),
                pl.BlockSpec((1, LANES), lambda i: (0, 0))]
    args = [dy, pre, g128]
    if passthrough is not None:
        in_specs.append(pl.BlockSpec((tm, D_MODEL), row))
        args.append(passthrough)
    return _pcall(body, name=name, out_shape=[_sds((t, width), BF16), _sds((1, LANES), F32)], grid=(t // tm,),
                  in_specs=in_specs,
                  out_specs=[pl.BlockSpec((tm, width), row), pl.BlockSpec((1, LANES), lambda i: (0, 0))],
                  semantics=("arbitrary",))(*args)


def stick_breaking_backward(q, k, v, do, *, name):
    t = q.shape[0]
    blk = min(ATT_BLOCK, t)
    nq = t // blk

    def body(q_ref, k_ref, v_ref, do_ref, dq_ref, dk_ref, dv_ref, s_buf, sg_buf):
        i = pl.program_id(1)

        @pl.when(i == 0)
        def _():
            dk_ref[...] = jnp.zeros_like(dk_ref)
            dv_ref[...] = jnp.zeros_like(dv_ref)

        low = lax.broadcasted_iota(jnp.int32, (blk, LANES), 1) < HEAD_DIM
        suffix = _suffix_matrix(blk)
        prefix = _prefix_matrix(blk)
        causal = _stacked_causal(blk)
        qs = _stack_heads(q_ref[...], low)
        dos = _stack_heads(do_ref[...], low)

        def log_weights(j, carry, masked):
            rows = pl.ds(pl.multiple_of(j * blk, blk), blk)
            z = _dot_nt(qs, k_ref[rows, :])
            ls = _log_sigmoid(z)
            lg = ls - z
            if masked:
                lg = jnp.where(causal, lg, 0.0)
            s_buf[j] = ls + _block_cumsum(lg, suffix) + carry
            sg_buf[j] = jnp.exp(ls)
            return carry + jnp.sum(lg, axis=-1, keepdims=True)

        carry = log_weights(i, jnp.zeros((2 * blk, 1), F32), True)
        carry = lax.fori_loop(0, i // 2, lambda n, c: log_weights(i - 2 - 2 * n, log_weights(i - 1 - 2 * n, c, False),
                                                                  False), carry)
        lax.fori_loop(0, i % 2, lambda n, c: log_weights(0, c, False), carry)

        def grads(j, pcarry, dq_acc, masked):
            rows = pl.ds(pl.multiple_of(j * blk, blk), blk)
            a = jnp.exp(s_buf[j])
            if masked:
                a = jnp.where(causal, a, 0.0)
            sg = sg_buf[j]
            ds = _dot_nt(dos, v_ref[rows, :]) * a
            before = _block_cumsum(ds, prefix) + pcarry
            if masked:
                before = jnp.where(causal, before, 0.0)
            dz = (ds - sg * (ds + before)).astype(BF16)
            dq_acc = dq_acc + _dot(dz, k_ref[rows, :])
            dk_ref[rows, :] += _dot_tn(dz, qs)
            dv_ref[rows, :] += _dot_tn(a.astype(BF16), dos)
            return pcarry + jnp.sum(ds, axis=-1, keepdims=True), dq_acc

        def two_blocks(n, st):
            st = grads(2 * n, st[0], st[1], False)
            return grads(2 * n + 1, st[0], st[1], False)

        state = lax.fori_loop(0, i // 2, two_blocks,
                              (jnp.zeros((2 * blk, 1), F32), jnp.zeros((2 * blk, LANES), F32)))
        state = lax.fori_loop(0, i % 2, lambda n, st: grads(i - 1, st[0], st[1], False), state)
        _, dq_acc = grads(i, state[0], state[1], True)
        dq_ref[...] = jnp.where(low, dq_acc[:blk], dq_acc[blk:])

    full = pl.BlockSpec((t, LANES), lambda p, i: (0, p))
    qblk = pl.BlockSpec((blk, LANES), lambda p, i: (i, p))
    return _pcall(
        body, name=name, out_shape=[_sds((t, D_MODEL), F32)] * 3, grid=(D_MODEL // LANES, nq),
        in_specs=[qblk, full, full, qblk], out_specs=[qblk, full, full],
        scratch_shapes=[pltpu.VMEM((nq, 2 * blk, blk), F32), pltpu.VMEM((nq, 2 * blk, blk), F32)],
        semantics=("parallel", "arbitrary"))(q, k, v, do)


def _mlp_backward(dx, saved, g, w_up, w_down, tag):
    x, h, r, a, a2 = saved
    d_w_down = matmul_tn(a2, dx, name=f"d_w_down_{tag}", col_shards=False)
    dpre = matmul_nt(dx, w_down, name=f"d_mlp_act_{tag}", mul=a, out_dtype=BF16)
    d_w_up = matmul_tn(h, dpre, name=f"d_w_up_{tag}", col_shards=True)
    dx, d_g = norm_backward(dpre, w_up, x, g, r, dx, name=f"d_mlp_norm_{tag}")
    return dx, d_w_up, d_w_down, d_g


def _ple_backward(dx, saved, p, g, w_gate, tag):
    x, h, r, gate, pp = saved
    dgate, dproj = ple_backward(dx, gate, pp, name=f"d_ple_{tag}")
    d_w_proj = matmul_tn(p, dproj, name=f"d_w_ple_proj_{tag}", col_shards=True)
    d_w_gate = matmul_tn(h, dgate, name=f"d_w_ple_gate_{tag}", col_shards=False)
    dx, d_g = norm_backward(dgate, w_gate, x, g, r, dx, name=f"d_ple_norm_{tag}")
    return dx, d_w_gate, d_w_proj, d_g


def local_step(x, p, target, w, late=None):
    row = lambda v: v.reshape(1, -1)
    g128 = lambda v: jnp.tile(v.reshape(1, HEAD_DIM), (1, 2))
    scale = HEAD_DIM ** -0.5
    b_full = jnp.repeat(jnp.transpose(w["b_spatial"][0]), LANES, axis=1)
    w_s = w["w_spatial"][0]

    mats = {}
    for name, value in w.items():
        if isinstance(value, tuple):
            mats.update({(name, layer): v for layer, v in enumerate(value)})
    if "w_kv" in w:
        mats[("w_kv", 0)] = w["w_kv"]

    def fetch(name, layer, after):
        if (name, layer) not in mats:
            mats.update(late.weights(name, layer, after))
        return mats[(name, layer)]

    def mlp_forward(x_in, layer):
        h, r, a, a2 = norm_matmul(x_in, row(w["ln_mlp"][layer]), fetch("w_up", layer, x_in), name=f"mlp_up_{layer}",
                                  epilogue="relu2")
        return matmul_residual(a2, fetch("w_down", layer, a2), x_in, name=f"mlp_down_{layer}"), (x_in, h, r, a, a2)

    def ple(x_in, layer):
        return ple_forward(x_in, row(w["ln_ple"][layer]), fetch("w_ple_gate", layer, x_in), p[layer],
                           fetch("w_ple_proj", layer, x_in), name=f"ple_{layer}")

    x0 = x
    h_a, r_a, pre_a = norm_matmul(x0, row(w["ln_mix_a"][0]), fetch("w_in_a", 0, x0), name="sgu_in")
    y_a = sgu_forward(pre_a, row(w["g_v_a"][0]), w_s, b_full, name="sgu_mix")
    x1 = matmul_residual(y_a, fetch("w_out_a", 0, y_a), x0, name="sgu_out")
    x2, mlp0 = mlp_forward(x1, 0)
    ple0 = ple(x2, 0)
    x3 = ple0[4]
    h_kv, r_kv, kv_pre = norm_matmul(x3, row(w["ln_kv"]), fetch("w_kv", 0, x3), name="kv_proj")
    k_n, v_b = head_norm(kv_pre, g128(w["g_k"]), name="k_norm", passthrough=True)
    h_q, r_q, q_pre = norm_matmul(x3, row(w["ln_mix_b"][0]), fetch("w_q", 0, k_n), name="q_proj")
    q_n = head_norm(q_pre, g128(w["g_q"][0]), name="q_norm", scale=scale)
    o = stick_breaking_forward(q_n, k_n, v_b, name="sb_fwd")
    x4 = matmul_residual(o, fetch("w_out_b", 0, o), x3, name="sb_out")
    x5, mlp1 = mlp_forward(x4, 1)
    ple1 = ple(x5, 1)
    x6 = ple1[4]
    loss_blk, dx = loss_forward(x6, target, name="loss")

    g = {}
    dx, dwg1, dwp1, dlnp1 = _ple_backward(dx, (x5,) + tuple(ple1[:4]), p[1], row(w["ln_ple"][1]),
                                          mats[("w_ple_gate", 1)], 1)
    dx, dwu1, dwd1, dlnm1 = _mlp_backward(dx, mlp1, row(w["ln_mlp"][1]), mats[("w_up", 1)], mats[("w_down", 1)], 1)
    g["w_out_b"] = matmul_tn(o, dx, name="d_w_out_b", col_shards=False)
    do = matmul_nt(dx, mats[("w_out_b", 0)], name="d_sb_out", out_dtype=BF16)
    dq_n, dk_n, dv = stick_breaking_backward(q_n, k_n, v_b, do, name="sb_bwd")
    dq_pre, dgq = head_norm_backward(dq_n, q_pre, g128(w["g_q"][0]), name="d_q_norm", scale=scale)
    dkv_pre, dgk = head_norm_backward(dk_n, kv_pre, g128(w["g_k"]), name="d_k_norm", passthrough=dv)
    g["w_q"] = matmul_tn(h_q, dq_pre, name="d_w_q", col_shards=False)
    g["w_kv"] = matmul_tn(h_kv, dkv_pre, name="d_w_kv", col_shards=True)
    dx, g["ln_mix_b"] = norm_backward(dq_pre, mats[("w_q", 0)], x3, row(w["ln_mix_b"][0]), r_q, dx, name="d_q_in")
    dx, g["ln_kv"] = norm_backward(dkv_pre, mats[("w_kv", 0)], x3, row(w["ln_kv"]), r_kv, dx, name="d_kv_in")
    g["g_q"] = dgq[:, :HEAD_DIM] + dgq[:, HEAD_DIM:]
    g["g_k"] = (dgk[:, :HEAD_DIM] + dgk[:, HEAD_DIM:]).reshape(HEAD_DIM)
    g["ln_kv"] = g["ln_kv"].reshape(D_MODEL)
    ln_ple0, ln_mlp0, g_v0, ln_mix0 = (row(w["ln_ple"][0]), row(w["ln_mlp"][0]), row(w["g_v_a"][0]),
                                       row(w["ln_mix_a"][0]))
    if late is not None:
        ln_ple0 = ln_ple0 + late.pair_start(
            {("w_kv", 0): g["w_kv"], ("w_q", 0): g["w_q"], ("w_out_b", 0): g["w_out_b"], ("w_up", 1): dwu1,
             ("w_down", 1): dwd1, ("w_ple_gate", 1): dwg1, ("w_ple_proj", 1): dwp1}, dx)[0, 0]
    dx, dwg0, dwp0, dlnp0 = _ple_backward(dx, (x2,) + tuple(ple0[:4]), p[0], ln_ple0, mats[("w_ple_gate", 0)], 0)
    if late is not None:
        ln_mlp0 = ln_mlp0 + late.chip_start(dx)[0, 0]
    dx, dwu0, dwd0, dlnm0 = _mlp_backward(dx, mlp0, ln_mlp0, mats[("w_up", 0)], mats[("w_down", 0)], 0)
    if late is not None:
        g_v0 = g_v0 + late.pair_start({("w_up", 0): dwu0, ("w_down", 0): dwd0, ("w_ple_gate", 0): dwg0,
                                       ("w_ple_proj", 0): dwp0}, dx)[0, 0]
    g["w_out_a"] = matmul_tn(y_a, dx, name="d_w_out_a", col_shards=False)
    dy_a = matmul_nt(dx, mats[("w_out_a", 0)], name="d_sgu_out")
    dpre_a, dws, db, g["g_v_a"] = sgu_backward(dy_a, pre_a, g_v0, w_s, b_full, name="d_sgu_mix")
    if late is not None:
        ln_mix0 = ln_mix0 + late.chip_start(dpre_a)[0, 0]
    g["w_in_a"] = matmul_tn(h_a, dpre_a, name="d_w_in_a", col_shards=True)
    dx, g["ln_mix_a"] = norm_backward(dpre_a, mats[("w_in_a", 0)], x0, ln_mix0, r_a, dx, name="d_sgu_in")
    g["w_spatial"] = dws[None]
    g["b_spatial"] = jnp.transpose(db[:, :N_GROUPS])[None]
    g["w_up"] = (dwu0, dwu1)
    g["w_down"] = (dwd0, dwd1)
    g["w_ple_gate"] = (dwg0, dwg1)
    g["w_ple_proj"] = (dwp0, dwp1)
    g["ln_mlp"] = jnp.concatenate([dlnm0, dlnm1], axis=0)
    g["ln_ple"] = jnp.concatenate([dlnp0, dlnp1], axis=0)
    return loss_blk, dx, g


ANY = pl.BlockSpec(memory_space=pl.ANY)


def _place():
    x, y, c = lax.axis_index("x"), lax.axis_index("y"), lax.axis_index("c")
    others = [(1 - x, y), (x, 1 - y), (1 - x, 1 - y)]
    return x, y, c, 2 * x + y, others


def cast_into_slot(w3, layer, slot, *, name, tm=256):
    _, r, c = w3.shape
    tm = min(tm, r)

    def body(slot_ref, w_ref, o_ref):
        o_ref[...] = w_ref[...].astype(BF16)

    return _pcall(body, name=name, out_shape=_sds((N_SHARDS, r, c), BF16), grid=(r // tm,), num_prefetch=1,
                  in_specs=[pl.BlockSpec((None, tm, c), lambda i, s: (layer, i, 0))],
                  out_specs=pl.BlockSpec((None, tm, c), lambda i, s: (s[0], i, 0)),
                  semantics=("parallel",))(slot, w3)


def gather_shards(mats, vecs, *, name):
    nm, nv = len(mats), len(vecs)
    halves = [m.reshape(N_SHARDS, 2, m.shape[1] // 2, m.shape[2]) for m in mats]

    def body(*refs):
        vsrc = refs[nm:nm + nv]
        out, vout = refs[nm + nv:2 * nm + nv], refs[2 * nm + nv:2 * (nm + nv)]
        send, recv, vsend, vrecv, loc = refs[2 * (nm + nv):]
        x, y, c, s_me, others = _place()
        sib = (x, y, 1 - c)

        def ici(l, k):
            ox, oy = others[k]
            return pltpu.make_async_remote_copy(out[l].at[s_me, c], out[l].at[s_me, c], send.at[l, k], recv.at[l, k],
                                                device_id=(ox, oy, c), device_id_type=MESH)

        def landed(l, k, half):
            ox, oy = others[k]
            return out[l].at[2 * ox + oy, half]

        def passed_on(l, k):
            return pltpu.make_async_remote_copy(landed(l, k, c), landed(l, k, c), send.at[l, 3 + k], recv.at[l, 3 + k],
                                                device_id=sib, device_id_type=MESH)

        def vec(l, k):
            ox, oy = others[k]
            return pltpu.make_async_remote_copy(vsrc[l], vout[l].at[s_me], vsend.at[l, k], vrecv.at[l, k],
                                                device_id=(ox, oy, c), device_id_type=MESH)

        for l in range(nm):
            for k in range(3):
                ici(l, k).start()
        for l in range(nv):
            for k in range(3):
                vec(l, k).start()
        for l in range(nv):
            own = pltpu.make_async_copy(vsrc[l], vout[l].at[s_me], loc)
            own.start()
            own.wait()
        for l in range(nm):
            for k in range(3):
                pltpu.make_async_remote_copy(landed(l, k, c), landed(l, k, c), send.at[l, k], recv.at[l, k],
                                             device_id=sib, device_id_type=MESH).wait_recv()
                passed_on(l, k).start()
        for l in range(nm):
            for k in range(3):
                pltpu.make_async_remote_copy(landed(l, k, 1 - c), landed(l, k, 1 - c), send.at[l, 3 + k],
                                             recv.at[l, 3 + k], device_id=sib, device_id_type=MESH).wait_recv()
        for l in range(nv):
            for k in range(3):
                ox, oy = others[k]
                pltpu.make_async_remote_copy(vsrc[l], vout[l].at[2 * ox + oy], vsend.at[l, k], vrecv.at[l, k],
                                             device_id=sib, device_id_type=MESH).wait_recv()
        for l in range(nm):
            for k in range(3):
                ici(l, k).wait_send()
                passed_on(l, k).wait_send()
        for l in range(nv):
            for k in range(3):
                vec(l, k).wait_send()

    out_shape = [_sds(h.shape, BF16) for h in halves] + [_sds((N_SHARDS,) + v.shape, F32) for v in vecs]
    res = _pcall(body, name=name, out_shape=out_shape, in_specs=[ANY] * (nm + nv), out_specs=[ANY] * (nm + nv),
                 scratch_shapes=[pltpu.SemaphoreType.DMA((max(nm, 1), 6)), pltpu.SemaphoreType.DMA((max(nm, 1), 6)),
                                 pltpu.SemaphoreType.DMA((max(nv, 1), 3)), pltpu.SemaphoreType.DMA((max(nv, 1), 3)),
                                 pltpu.SemaphoreType.DMA(())],
                 aliases={l: l for l in range(nm)}, side_effects=True)(*halves, *vecs)
    return [r.reshape(m.shape) for r, m in zip(res[:nm], mats)], list(res[nm:])


HBM = pl.BlockSpec(memory_space=pltpu.HBM)
SEM = pl.BlockSpec(memory_space=pltpu.SEMAPHORE)
DATAFLOW = pltpu.SideEffectType.DATAFLOW_SIDE_EFFECTING


def _split_call(body, *, name, out_shape, in_specs, out_specs, aliases):
    return pl.pallas_call(body, name=name, out_shape=out_shape, in_specs=in_specs, out_specs=out_specs,
                          input_output_aliases=aliases,
                          compiler_params=pltpu.CompilerParams(has_side_effects=DATAFLOW))


def _token_shape():
    return jax.ShapeDtypeStruct((8, LANES), F32)


def gather_start(mats, after, *, name):
    n = len(mats)
    halves = [pltpu.with_memory_space_constraint(m.reshape(N_SHARDS, 2, m.shape[1] // 2, m.shape[2]), pltpu.HBM)
              for m in mats]

    def body(*refs):
        send, recv = refs[n + 1], refs[n + 2]
        out, token = refs[n + 3:2 * n + 3], refs[2 * n + 3]
        x, y, c, s_me, others = _place()
        for l in range(n):
            for k in range(3):
                ox, oy = others[k]
                pltpu.make_async_remote_copy(out[l].at[s_me, c], out[l].at[s_me, c], send.at[3 * l + k],
                                             recv.at[3 * l + k], device_id=(ox, oy, c), device_id_type=MESH).start()
        token[...] = jnp.zeros_like(token)

    res = _split_call(
        body, name=name,
        out_shape=(pltpu.SemaphoreType.DMA((3 * n,)), pltpu.SemaphoreType.DMA((3 * n,)),
                   *[pltpu.HBM(h.shape, BF16) for h in halves], _token_shape()),
        in_specs=[HBM] * n + [ANY], out_specs=(SEM, SEM, *[HBM] * n, pl.BlockSpec(memory_space=pltpu.VMEM)),
        aliases={l: 2 + l for l in range(n)})(*halves, after)
    return res[0], res[1], list(res[2:2 + n]), res[2 + n]


def gather_pass_on(bufs, send_a, recv_a, after, *, name, base=0):
    n = len(bufs)

    def body(*refs):
        send_a, recv_a = refs[n], refs[n + 1]
        out = refs[n + 3:2 * n + 3]
        send_b, recv_b, token = refs[2 * n + 3:]
        x, y, c, s_me, others = _place()
        for l in range(n):
            for k in range(3):
                ox, oy = others[k]
                landed, i = out[l].at[2 * ox + oy, c], 3 * l + k
                pltpu.make_async_remote_copy(landed, landed, send_a.at[3 * base + i], recv_a.at[3 * base + i],
                                             device_id=(x, y, 1 - c), device_id_type=MESH).wait_recv()
                pltpu.make_async_remote_copy(landed, landed, send_b.at[i], recv_b.at[i],
                                             device_id=(x, y, 1 - c), device_id_type=MESH).start()
        for l in range(n):
            for k in range(3):
                mine, i = out[l].at[s_me, c], 3 * (base + l) + k
                pltpu.make_async_remote_copy(mine, mine, send_a.at[i], recv_a.at[i],
                                             device_id=(x, y, 1 - c), device_id_type=MESH).wait_send()
        token[...] = jnp.zeros_like(token)

    res = _split_call(
        body, name=name,
        out_shape=(*[pltpu.HBM(b.shape, BF16) for b in bufs], pltpu.SemaphoreType.DMA((3 * n,)),
                   pltpu.SemaphoreType.DMA((3 * n,)), _token_shape()),
        in_specs=[HBM] * n + [SEM, SEM, ANY],
        out_specs=(*[HBM] * n, SEM, SEM, pl.BlockSpec(memory_space=pltpu.VMEM)),
        aliases={l: l for l in range(n)})(*bufs, send_a, recv_a, after)
    return list(res[:n]), res[n], res[n + 1], res[n + 2]


def gather_finish(bufs, send_b, recv_b, after, shapes, *, name):
    n = len(bufs)

    def body(*refs):
        send_b, recv_b = refs[n], refs[n + 1]
        out = refs[n + 3:]
        x, y, c, _, others = _place()
        for l in range(n):
            for k in range(3):
                ox, oy = others[k]
                theirs, mine, i = out[l].at[2 * ox + oy, 1 - c], out[l].at[2 * ox + oy, c], 3 * l + k
                pltpu.make_async_remote_copy(theirs, theirs, send_b.at[i], recv_b.at[i],
                                             device_id=(x, y, 1 - c), device_id_type=MESH).wait_recv()
                pltpu.make_async_remote_copy(mine, mine, send_b.at[i], recv_b.at[i],
                                             device_id=(x, y, 1 - c), device_id_type=MESH).wait_send()

    res = _split_call(
        body, name=name, out_shape=tuple(pltpu.HBM(b.shape, BF16) for b in bufs),
        in_specs=[HBM] * n + [SEM, SEM, ANY], out_specs=tuple([HBM] * n),
        aliases={l: l for l in range(n)})(*bufs, send_b, recv_b, after)
    return [r.reshape(s) for r, s in zip(res, shapes)]


def exchange_start(srcs, dst_shapes, dst_dtype, plan, count, after, *, name):
    n, m = len(srcs), len(dst_shapes)
    srcs = [pltpu.with_memory_space_constraint(s, pltpu.HBM) for s in srcs]
    lands = [pltpu.with_memory_space_constraint(lax.empty(s, dst_dtype), pltpu.HBM) for s in dst_shapes]

    def body(*refs):
        send, recv = refs[n + m + 1], refs[n + m + 2]
        src, dst, token = refs[n + m + 3:2 * n + m + 3], refs[2 * n + m + 3:2 * (n + m) + 3], refs[2 * (n + m) + 3]
        for i, (s, d, dev) in enumerate(plan(_place(), src, dst)):
            pltpu.make_async_remote_copy(s, d, send.at[i], recv.at[i], device_id=dev, device_id_type=MESH).start()
        token[...] = jnp.zeros_like(token)

    res = _split_call(
        body, name=name,
        out_shape=(pltpu.SemaphoreType.DMA((count,)), pltpu.SemaphoreType.DMA((count,)),
                   *[pltpu.HBM(s.shape, s.dtype) for s in srcs], *[pltpu.HBM(s, dst_dtype) for s in dst_shapes],
                   _token_shape()),
        in_specs=[HBM] * (n + m) + [ANY],
        out_specs=(SEM, SEM, *[HBM] * (n + m), pl.BlockSpec(memory_space=pltpu.VMEM)),
        aliases={i: 2 + i for i in range(n + m)})(*srcs, *lands, after)
    return (list(res[2:2 + n]), list(res[2 + n:2 + n + m]), res[0], res[1], plan), res[2 + n + m]


def exchange_finish(state, after, *, name):
    srcs, lands, send, recv, plan = state
    n, m = len(srcs), len(lands)

    def body(*refs):
        send, recv = refs[n + m], refs[n + m + 1]
        src, dst = refs[n + m + 3:2 * n + m + 3], refs[2 * n + m + 3:]
        for i, (s, d, dev) in enumerate(plan(_place(), src, dst)):
            pltpu.make_async_remote_copy(s, d, send.at[i], recv.at[i], device_id=dev, device_id_type=MESH).wait()

    res = _split_call(
        body, name=name,
        out_shape=tuple(pltpu.HBM(a.shape, a.dtype) for a in srcs + lands),
        in_specs=[HBM] * (n + m) + [SEM, SEM, ANY], out_specs=tuple([HBM] * (n + m)),
        aliases={i: i for i in range(n + m)})(*srcs, *lands, send, recv, after)
    return list(res[:n]), list(res[n:])


def pair_plan(place, src, dst):
    x, y, c, _, _ = place
    return [(s.at[:, 1 - c], d, (x, y, 1 - c)) for s, d in zip(src, dst)]


def chip_plan(place, src, dst):
    x, y, c, _, others = place
    return [(s.at[2 * ox + oy], d.at[k], (ox, oy, c)) for s, d in zip(src, dst) for k, (ox, oy) in enumerate(others)]


def pair_exchange(grads, *, name):
    n = len(grads)

    def body(*refs):
        src, got = refs[:n], refs[n:2 * n]
        send, recv = refs[2 * n:]
        x, y, c, _, _ = _place()

        def swap(l):
            return pltpu.make_async_remote_copy(src[l].at[:, 1 - c], got[l], send.at[l], recv.at[l],
                                                device_id=(x, y, 1 - c), device_id_type=MESH)

        for l in range(n):
            swap(l).start()
        for l in range(n):
            swap(l).wait()

    res = _pcall(body, name=name, out_shape=[_sds((N_SHARDS,) + g.shape[2:], F32) for g in grads],
                 in_specs=[ANY] * n, out_specs=[ANY] * n,
                 scratch_shapes=[pltpu.SemaphoreType.DMA((n,)), pltpu.SemaphoreType.DMA((n,))],
                 side_effects=True)(*grads)
    return list(res)


def add_to_wire(mine, theirs, core, *, name, tm=256):
    s, _, r, c = mine.shape
    tm = min(tm, r)

    def body(core_ref, a_ref, b_ref, o_ref):
        o_ref[...] = (a_ref[...] + b_ref[...]).astype(BF16)

    spec = pl.BlockSpec((None, tm, c), lambda i, j, cr: (i, j, 0))
    return _pcall(body, name=name, out_shape=_sds((s, r, c), BF16), grid=(s, r // tm), num_prefetch=1,
                  in_specs=[pl.BlockSpec((None, None, tm, c), lambda i, j, cr: (i, cr[0], j, 0)), spec],
                  out_specs=spec, semantics=("parallel", "parallel"))(core, mine, theirs)


def sum_chips(wire, landed, place, dest, layer, n_layers, *, name, tm=256):
    _, r, c = wire.shape
    tm = min(tm, r)

    def body(place_ref, w_ref, l_ref, *rest):
        o_ref = rest[-1]
        o_ref[...] = ((w_ref[...].astype(F32) + l_ref[0].astype(F32)) + l_ref[1].astype(F32)) + l_ref[2].astype(F32)

    in_specs = [pl.BlockSpec((None, tm, c), lambda i, pr: (pr[0], i, 0)),
                pl.BlockSpec((3, tm, c), lambda i, pr: (0, i, 0))]
    args = [place, wire, landed]
    aliases = None
    if dest is not None:
        in_specs.append(ANY)
        args.append(dest)
        aliases = {3: 0}
    return _pcall(body, name=name, out_shape=_sds((n_layers, 2, r, c), F32), grid=(r // tm,), num_prefetch=1,
                  in_specs=in_specs,
                  out_specs=pl.BlockSpec((None, None, tm, c), lambda i, pr: (layer, pr[1], i, 0)),
                  aliases=aliases, semantics=("parallel",))(*args)


def pair_share(bufs, slots, *, name):
    n = len(bufs)

    def body(*refs):
        out = refs[n:2 * n]
        send, recv = refs[2 * n:]
        x, y, c, _, _ = _place()

        def share(i, half):
            o, l = slots[i]
            return pltpu.make_async_remote_copy(out[o].at[l, half], out[o].at[l, half], send.at[i], recv.at[i],
                                                device_id=(x, y, 1 - c), device_id_type=MESH)

        for i in range(len(slots)):
            share(i, c).start()
        for i in range(len(slots)):
            share(i, 1 - c).wait_recv()
            share(i, c).wait_send()

    res = _pcall(body, name=name, out_shape=[_sds(b.shape, F32) for b in bufs], in_specs=[ANY] * n,
                 out_specs=[ANY] * n,
                 scratch_shapes=[pltpu.SemaphoreType.DMA((len(slots),)), pltpu.SemaphoreType.DMA((len(slots),))],
                 aliases={o: o for o in range(n)}, side_effects=True)(*bufs)
    return list(res)


def all_reduce_small(packed, *, name):
    n_dev, r, c = packed.shape

    def body(in_ref, out_ref, land, send, recv):
        x, y, cc, _, _ = _place()
        me = 4 * x + 2 * y + cc
        peers = [(px, py, pc) for px in range(2) for py in range(2) for pc in range(2)]

        def scatter(d):
            return pltpu.make_async_remote_copy(in_ref.at[d], land.at[me], send.at[0, d], recv.at[0, me],
                                                device_id=peers[d], device_id_type=MESH)

        def gather(d):
            return pltpu.make_async_remote_copy(out_ref.at[me], out_ref.at[me], send.at[1, d], recv.at[1, me],
                                                device_id=peers[d], device_id_type=MESH)

        for d in range(n_dev):
            @pl.when(d != me)
            def _():
                scatter(d).start()
        land[me] = in_ref[me]
        for d in range(n_dev):
            @pl.when(d != me)
            def _():
                pltpu.make_async_remote_copy(in_ref.at[d], land.at[d], send.at[0, d], recv.at[0, d],
                                             device_id=peers[d], device_id_type=MESH).wait_recv()
        total = land[0]
        for d in range(1, n_dev):
            total = total + land[d]
        out_ref[me] = total
        for d in range(n_dev):
            @pl.when(d != me)
            def _():
                gather(d).start()
        for d in range(n_dev):
            @pl.when(d != me)
            def _():
                pltpu.make_async_remote_copy(out_ref.at[d], out_ref.at[d], send.at[1, d], recv.at[1, d],
                                             device_id=peers[d], device_id_type=MESH).wait_recv()
        for d in range(n_dev):
            @pl.when(d != me)
            def _():
                scatter(d).wait_send()
                gather(d).wait_send()

    vm = pl.BlockSpec(memory_space=pltpu.VMEM)
    return _pcall(body, name=name, out_shape=_sds(packed.shape, F32), in_specs=[vm], out_specs=vm,
                  scratch_shapes=[pltpu.VMEM(packed.shape, F32), pltpu.SemaphoreType.DMA((2, n_dev)),
                                  pltpu.SemaphoreType.DMA((2, n_dev))],
                  side_effects=True)(packed)


def adamw(w, g, m, v, *, name, part=None, dest=None, tm=256):
    shape = w.shape
    cols = shape[-1]
    rows = 1
    for s in shape[:-1]:
        rows *= s
    first, count = 0, rows
    if part is not None:
        count = rows // part[1]
        first = part[0] * count
    tm = min(tm, count)
    assert count % tm == 0
    two_d = lambda a: a.reshape(rows, cols)

    def body(w_ref, g_ref, m_ref, v_ref, *rest):
        d_ref, mo_ref, vo_ref = rest[-3:]
        gv = g_ref[...]
        m_new = ADAM_B1 * m_ref[...] + (1.0 - ADAM_B1) * gv
        v_new = ADAM_B2 * v_ref[...] + (1.0 - ADAM_B2) * (gv * gv)
        m_hat = m_new / (1.0 - ADAM_B1 ** ADAM_STEP)
        v_hat = v_new / (1.0 - ADAM_B2 ** ADAM_STEP)
        d_ref[...] = -ADAM_LR * (m_hat / (jnp.sqrt(v_hat) + ADAM_EPS) + ADAM_WD * w_ref[...])
        mo_ref[...] = m_new
        vo_ref[...] = v_new

    spec = pl.BlockSpec((tm, cols), lambda i: (first // tm + i, 0))
    args = [two_d(w), two_d(g), two_d(m), two_d(v)]
    in_specs = [spec] * 4
    aliases = None
    if dest is not None:
        args += [two_d(d) for d in dest]
        in_specs = in_specs + [ANY] * 3
        aliases = {4: 0, 5: 1, 6: 2}
    outs = _pcall(body, name=name, out_shape=[_sds((rows, cols), F32)] * 3, grid=(count // tm,), in_specs=in_specs,
                  out_specs=[spec] * 3, aliases=aliases, semantics=("parallel",))(*args)
    return [o.reshape(shape) for o in outs]


WEIGHTS = ("ln_mix_a", "w_in_a", "g_v_a", "w_spatial", "b_spatial", "w_out_a", "ln_kv", "w_kv", "g_k", "ln_mix_b",
           "w_q", "g_q", "w_out_b", "ln_mlp", "w_up", "w_down", "ln_ple", "w_ple_gate", "w_ple_proj")
MATRICES = (("w_in_a", 1, True), ("w_out_a", 1, False), ("w_kv", 0, True), ("w_q", 1, False), ("w_out_b", 1, False),
            ("w_up", 2, True), ("w_down", 2, False), ("w_ple_gate", 2, False), ("w_ple_proj", 2, True))
GATHER_STAGES = ((("w_in_a", 0),), (("w_out_a", 0),), (("w_up", 0),), (("w_down", 0),),
                 (("w_ple_gate", 0), ("w_ple_proj", 0), ("w_kv", 0)), (("w_q", 0),), (("w_out_b", 0),), (("w_up", 1),),
                 (("w_down", 1),), (("w_ple_gate", 1), ("w_ple_proj", 1)))
REPLICATED = ("w_spatial", "b_spatial", "ln_kv", "g_k", "ln_mix_b", "g_q", "ln_mlp", "ln_ple")
SHARDED_VECTORS = ("ln_mix_a", "g_v_a")
SMALL_ROWS = 18


def kernel(x, p, ln_mix_a, w_in_a, g_v_a, w_spatial, b_spatial, w_out_a, ln_kv, w_kv, g_k, ln_mix_b, w_q, g_q, w_out_b, ln_mlp, w_up, w_down, ln_ple, w_ple_gate, w_ple_proj, loss_target, m_ln_mix_a, m_w_in_a, m_g_v_a, m_w_spatial, m_b_spatial, m_w_out_a, m_ln_kv, m_w_kv, m_g_k, m_ln_mix_b, m_w_q, m_g_q, m_w_out_b, m_ln_mlp, m_w_up, m_w_down, m_ln_ple, m_w_ple_gate, m_w_ple_proj, v_ln_mix_a, v_w_in_a, v_g_v_a, v_w_spatial, v_b_spatial, v_w_out_a, v_ln_kv, v_w_kv, v_g_k, v_ln_mix_b, v_w_q, v_g_q, v_w_out_b, v_ln_mlp, v_w_up, v_w_down, v_ln_ple, v_w_ple_gate, v_w_ple_proj):
    given = dict(locals())
    weights = {n: given[n] for n in WEIGHTS}
    shard = 2 * lax.axis_index("x") + lax.axis_index("y")
    core = lax.axis_index("c")
    shard_1 = shard.astype(jnp.int32).reshape(1)
    core_1 = core.astype(jnp.int32).reshape(1)
    place = jnp.stack([shard, core]).astype(jnp.int32)

    leaves = []
    for name, layers, cols in MATRICES:
        w3 = weights[name] if layers else weights[name][None]
        for layer in range(max(layers, 1)):
            leaves.append((name, layer, cols, cast_into_slot(w3, layer, shard_1, name=f"cast_{name}_{layer}")))
    by_key = {(lf[0], lf[1]): lf for lf in leaves}
    ordered = [by_key[key] for stage in GATHER_STAGES for key in stage]
    _, vec_a = gather_shards([], [ln_mix_a, g_v_a], name="gather_vectors")
    send_a, recv_a, flying, token = gather_start([lf[3] for lf in ordered], vec_a[0], name="gather_start")

    w = {"ln_mix_a": vec_a[0].reshape(1, D_MODEL) + token[0, 0],
         "g_v_a": vec_a[1].reshape(1, D_MODEL)}
    for name in REPLICATED:
        w[name] = weights[name]

    class Late:
        def weights(self, name, layer, after):
            stage = [(name, layer) in s for s in GATHER_STAGES].index(True)
            base = sum(len(s) for s in GATHER_STAGES[:stage])
            members = ordered[base:base + len(GATHER_STAGES[stage])]
            bufs, send_b, recv_b, tok = gather_pass_on(flying[base:base + len(members)], send_a, recv_a, after,
                                                       name=f"gather_pass_on_{stage}", base=base)
            got = gather_finish(bufs, send_b, recv_b, tok, [lf[3].shape for lf in members],
                                name=f"gather_finish_{stage}")
            out = {}
            for (leaf_name, leaf_layer, cols, _), arr in zip(members, got):
                out[(leaf_name, leaf_layer)] = arr if cols else arr.reshape(N_SHARDS * arr.shape[1], arr.shape[2])
            return out

        groups = []

        def pair_start(self, grads_done, after):
            self.keys = sorted(grads_done)
            views = [view(k, grads_done[k]) for k in self.keys]
            self.pair, token = exchange_start(views, [(N_SHARDS,) + v.shape[2:] for v in views], F32, pair_plan,
                                              len(views), after, name=f"grad_pair_start_{len(self.groups)}")
            return token

        def chip_start(self, after):
            tag = len(self.groups)
            mine, theirs = exchange_finish(self.pair, after, name=f"grad_pair_finish_{tag}")
            wire = [add_to_wire(a, b, core_1, name=f"grad_pair_sum_{tag}_{i}")
                    for i, (a, b) in enumerate(zip(mine, theirs))]
            chip, token = exchange_start(wire, [(3,) + v.shape[1:] for v in wire], BF16, chip_plan, 3 * len(wire),
                                         wire[-1], name=f"grad_chip_start_{tag}")
            self.groups.append((self.keys, chip))
            return token

    col_sharded = {name: cols for name, _, cols in MATRICES}
    layer_count = {name: max(layers, 1) for name, layers, _ in MATRICES}

    def view(key, arr):
        rows = arr.shape[-2] if col_sharded[key[0]] else arr.shape[0] // N_SHARDS
        return arr.reshape(N_SHARDS, 2, rows // 2, arr.shape[-1])

    t = x.shape[1]
    late = Late()
    loss_blk, dx, g = local_step(x[0], p.reshape(2, t, PLE_DIM), loss_target[0], w, late)
    loss = lax.psum(loss_blk[0, 0], ("x", "y", "c"))

    sent = {k for keys, _ in late.groups for k in keys}
    keys_last = [(name, layer) for name, layers, _ in MATRICES for layer in range(max(layers, 1))
                 if (name, layer) not in sent]
    views = [view(k, g[k[0]][k[1]] if layer_count[k[0]] == 2 else g[k[0]]) for k in keys_last]

    theirs = pair_exchange(views, name="grad_pair_exchange_last")
    wire_0 = [add_to_wire(a, b, core_1, name=f"grad_pair_sum_last_{i}") for i, (a, b) in enumerate(zip(views, theirs))]
    chip_0, token_0 = exchange_start(wire_0, [(3,) + v.shape[1:] for v in wire_0], BF16, chip_plan, 3 * len(wire_0),
                                     wire_0[-1], name="grad_chip_start_last")

    grads, bufs = {}, {}

    def sum_and_share(keys, wire, landed, tag):
        for i, (key, wv, lv) in enumerate(zip(keys, wire, landed)):
            name, layer = key
            bufs[name] = sum_chips(wv, lv, place, bufs.get(name), layer, layer_count[name],
                                   name=f"grad_chip_sum_{tag}_{i}")
        names = sorted({k[0] for k in keys})
        shared = pair_share([bufs[n] for n in names], [(names.index(k[0]), k[1]) for k in keys],
                            name=f"grad_pair_share_{tag}")
        bufs.update(zip(names, shared))

    updates = {}

    def update(n, gn, part=None):
        wn, mn, vn = weights[n], given["m_" + n], given["v_" + n]
        if wn.ndim == 1:
            wn, gn, mn, vn = (a.reshape(1, -1) for a in (wn, gn, mn, vn))
        tag = "" if part is None else f"_{part[0]}"
        updates[n] = adamw(wn, gn.reshape(wn.shape), mn, vn, name=f"adamw_{n}{tag}", part=part, dest=updates.get(n))

    after = token_0
    for tag, (keys, chip) in enumerate(late.groups + [(keys_last, chip_0)]):
        wire, landed = exchange_finish(chip, after, name=f"grad_chip_finish_{tag}")
        sum_and_share(keys, wire, landed, tag)
        for name, layer in keys:
            update(name, bufs[name], (layer, layer_count[name]) if layer_count[name] == 2 else None)
        after = updates[keys[-1][0]][0]

    small = REPLICATED + SHARDED_VECTORS
    flat = jnp.concatenate([g[n].reshape(-1) for n in small])
    room = 8 * SMALL_ROWS * D_MODEL
    flat = jnp.concatenate([flat, jnp.zeros((room - flat.shape[0],), F32)])
    flat, _ = lax.optimization_barrier((flat, after))
    reduced = all_reduce_small(flat.reshape(8, SMALL_ROWS, D_MODEL), name="grad_small_all_reduce").reshape(-1)
    at = 0
    for n in small:
        size = g[n].size
        piece = reduced[at:at + size]
        at += size
        if n in SHARDED_VECTORS:
            per = D_MODEL // N_SHARDS
            grads[n] = lax.dynamic_slice(piece, (shard * per,), (per,)).reshape(weights[n].shape)
        else:
            grads[n] = piece.reshape(weights[n].shape)
        update(n, grads[n])
    for name, _, _ in MATRICES:
        grads[name] = bufs[name].reshape(weights[name].shape)
    delta = {n: updates[n][0].reshape(weights[n].shape) for n in WEIGHTS}
    new_m = {n: updates[n][1].reshape(weights[n].shape) for n in WEIGHTS}
    new_v = {n: updates[n][2].reshape(weights[n].shape) for n in WEIGHTS}
    return (loss, dx.reshape(x.shape), *[grads[n] for n in WEIGHTS], *[delta[n] for n in WEIGHTS],
            *[new_m[n] for n in WEIGHTS], *[new_v[n] for n in WEIGHTS])
```

```python
import jax
import jax.numpy as jnp
from jax import lax
from jax.experimental import pallas as pl
from jax.experimental.pallas import tpu as pltpu

F32 = jnp.float32
BF16 = jnp.bfloat16

D_MODEL = 1024
D_FF = 4096
PLE_DIM = 256
N_GROUPS = 8
CHUNK = 128
HEAD_DIM = 64
LANES = 128
ATT_BLOCK = 256
EPS = 1e-6
N_SHARDS = 4
VMEM_LIMIT = 56 * 1024 * 1024

ADAM_LR = 0.001
ADAM_B1 = 0.9
ADAM_B2 = 0.999
ADAM_EPS = 1e-08
ADAM_WD = 0.01
ADAM_STEP = 10

MESH = pl.DeviceIdType.MESH


def _pcall(body, *, name, out_shape, grid=None, in_specs=None, out_specs=None, scratch_shapes=(),
           semantics=None, aliases=None, side_effects=False, num_prefetch=0):
    params = dict(vmem_limit_bytes=VMEM_LIMIT)
    if semantics is not None:
        params["dimension_semantics"] = semantics
    if side_effects:
        params["has_side_effects"] = True
    kwargs = {}
    if aliases:
        kwargs["input_output_aliases"] = aliases
    if num_prefetch:
        spec = pltpu.PrefetchScalarGridSpec(num_scalar_prefetch=num_prefetch, grid=grid, in_specs=in_specs,
                                            out_specs=out_specs, scratch_shapes=list(scratch_shapes))
        return pl.pallas_call(body, name=name, out_shape=out_shape, grid_spec=spec,
                              compiler_params=pltpu.CompilerParams(**params), **kwargs)
    if grid is not None:
        kwargs["grid"] = grid
    if in_specs is not None:
        kwargs["in_specs"] = in_specs
    if out_specs is not None:
        kwargs["out_specs"] = out_specs
    if aliases:
        kwargs["input_output_aliases"] = aliases
    return pl.pallas_call(body, name=name, out_shape=out_shape, scratch_shapes=list(scratch_shapes),
                          compiler_params=pltpu.CompilerParams(**params), **kwargs)


def _sds(shape, dtype):
    return jax.ShapeDtypeStruct(shape, dtype)


_GELU_C = 0.7978845608028654
_GELU_A = 0.044715


def _gelu(x):
    inner = _GELU_C * (x + _GELU_A * (x * x * x))
    return 0.5 * x * (1.0 + jnp.tanh(inner))


def _gelu_grad(x):
    x2 = x * x
    t = jnp.tanh(_GELU_C * (x + _GELU_A * (x2 * x)))
    return 0.5 * (1.0 + t) + 0.5 * x * (1.0 - t * t) * (_GELU_C * (1.0 + 3.0 * _GELU_A * x2))


def _sigmoid(x):
    return 1.0 / (1.0 + jnp.exp(-x))


def _log_sigmoid(z):
    return jnp.minimum(z, 0.0) - jnp.log(1.0 + jnp.exp(-jnp.abs(z)))


def _dot(a, b):
    return jnp.dot(a, b, preferred_element_type=F32)


def _dot_nt(a, b):
    return lax.dot_general(a, b, (((1,), (1,)), ((), ())), preferred_element_type=F32)


def _dot_tn(a, b):
    return lax.dot_general(a, b, (((0,), (0,)), ((), ())), preferred_element_type=F32)


def _head_rstd(x):
    lane = lax.broadcasted_iota(jnp.int32, x.shape, 1)
    low = lane < HEAD_DIM
    sq = x * x
    s_lo = jnp.sum(jnp.where(low, sq, 0.0), axis=-1, keepdims=True)
    s_hi = jnp.sum(jnp.where(low, 0.0, sq), axis=-1, keepdims=True)
    ms = jnp.where(low, s_lo, s_hi) * (1.0 / HEAD_DIM)
    return lax.rsqrt(ms + EPS)


def _head_mean(x):
    lane = lax.broadcasted_iota(jnp.int32, x.shape, 1)
    low = lane < HEAD_DIM
    s_lo = jnp.sum(jnp.where(low, x, 0.0), axis=-1, keepdims=True)
    s_hi = jnp.sum(jnp.where(low, 0.0, x), axis=-1, keepdims=True)
    return jnp.where(low, s_lo, s_hi) * (1.0 / HEAD_DIM)


def _full(shape):
    zeros = (0,) * len(shape)
    return pl.BlockSpec(shape, lambda i: zeros)


def norm_matmul(x, g, w, *, name, epilogue="none", tm=512):
    t, d = x.shape
    sharded = w.ndim == 3
    per = w.shape[2] if sharded else w.shape[1]
    n = N_SHARDS * per if sharded else per
    tm = min(tm, t)

    def body(x_ref, g_ref, w_ref, h_ref, r_ref, *outs):
        xv = x_ref[...]
        r = lax.rsqrt(jnp.mean(xv * xv, axis=-1, keepdims=True) + EPS)
        h = ((xv * r) * g_ref[...]).astype(BF16)
        h_ref[...] = h
        r_ref[...] = r
        for s in range(N_SHARDS if sharded else 1):
            cols = slice(s * per, (s + 1) * per)
            y = _dot(h, w_ref[s] if sharded else w_ref[...])
            if epilogue == "none":
                outs[0][:, cols] = y
            else:
                a = jnp.maximum(y, 0.0)
                outs[0][:, cols] = a.astype(BF16)
                outs[1][:, cols] = (a * a).astype(BF16)

    row = lambda i: (i, 0)
    out_shape = [_sds((t, d), BF16), _sds((t, 1), F32)]
    out_specs = [pl.BlockSpec((tm, d), row), pl.BlockSpec((tm, 1), row)]
    if epilogue == "none":
        out_shape.append(_sds((t, n), F32))
        out_specs.append(pl.BlockSpec((tm, n), row))
    else:
        out_shape += [_sds((t, n), BF16), _sds((t, n), BF16)]
        out_specs += [pl.BlockSpec((tm, n), row)] * 2
    return _pcall(
        body, name=name, out_shape=out_shape, grid=(t // tm,),
        in_specs=[pl.BlockSpec((tm, d), row), _full((1, d)), _full(w.shape)],
        out_specs=out_specs, semantics=("parallel",))(x, g, w)


def matmul_residual(a, w, res, *, name, tm=512):
    t, k = a.shape
    n = w.shape[1]
    tm = min(tm, t)

    def body(a_ref, w_ref, res_ref, o_ref):
        o_ref[...] = res_ref[...] + _dot(a_ref[...], w_ref[...])

    row = lambda i: (i, 0)
    return _pcall(
        body, name=name, out_shape=_sds((t, n), F32), grid=(t // tm,),
        in_specs=[pl.BlockSpec((tm, k), row), _full(w.shape), pl.BlockSpec((tm, n), row)],
        out_specs=pl.BlockSpec((tm, n), row), semantics=("parallel",))(a, w, res)


def ple_forward(x, g, w_gate, p, w_proj, *, name, tm=256):
    t, d = x.shape
    tm = min(tm, t)

    def body(x_ref, g_ref, wg_ref, p_ref, wp_ref, h_ref, r_ref, gate_ref, pp_ref, o_ref):
        xv = x_ref[...]
        r = lax.rsqrt(jnp.mean(xv * xv, axis=-1, keepdims=True) + EPS)
        h = ((xv * r) * g_ref[...]).astype(BF16)
        h_ref[...] = h
        r_ref[...] = r
        gate = _sigmoid(_dot(h, wg_ref[...]))
        gate_ref[...] = gate
        pb = p_ref[...].astype(BF16)
        per = d // N_SHARDS
        for s in range(N_SHARDS):
            cols = slice(s * per, (s + 1) * per)
            pp = _dot(pb, wp_ref[s])
            pp_ref[:, cols] = pp.astype(BF16)
            o_ref[:, cols] = xv[:, cols] + pp * gate[:, cols]

    row = lambda i: (i, 0)
    fixed = lambda i: (0, 0)
    return _pcall(
        body, name=name,
        out_shape=[_sds((t, d), BF16), _sds((t, 1), F32), _sds((t, d), F32), _sds((t, d), BF16), _sds((t, d), F32)],
        grid=(t // tm,),
        in_specs=[pl.BlockSpec((tm, d), row), pl.BlockSpec((1, d), fixed), pl.BlockSpec((d, d), fixed),
                  pl.BlockSpec((tm, PLE_DIM), row),
                  pl.BlockSpec((N_SHARDS, PLE_DIM, d // N_SHARDS), lambda i: (0, 0, 0))],
        out_specs=[pl.BlockSpec((tm, d), row), pl.BlockSpec((tm, 1), row), pl.BlockSpec((tm, d), row),
                   pl.BlockSpec((tm, d), row), pl.BlockSpec((tm, d), row)],
        semantics=("parallel",))(x, g, w_gate, p, w_proj)


def _tril_mask():
    r = lax.broadcasted_iota(jnp.int32, (CHUNK, CHUNK), 0)
    c = lax.broadcasted_iota(jnp.int32, (CHUNK, CHUNK), 1)
    return c <= r


def _sgu_common(pre_ref, gv_ref, ws_ref):
    pre = pre_ref[...]
    pre_u, pre_v = pre[:, :D_MODEL], pre[:, D_MODEL:]
    u = _gelu(pre_u)
    v = _gelu(pre_v)
    r = lax.rsqrt(jnp.mean(v * v, axis=-1, keepdims=True) + EPS)
    vhat = v * r
    vn = (vhat * gv_ref[...]).astype(BF16)
    tril = _tril_mask()
    wm = [jnp.where(tril, ws_ref[g], 0.0).astype(BF16) for g in range(N_GROUPS)]
    return pre_u, pre_v, u, r, vhat, vn, wm, tril


def sgu_forward(pre, g_v, w_s, b_full, *, name):
    t = pre.shape[0]

    def body(pre_ref, gv_ref, ws_ref, b_ref, y_ref):
        _, _, u, _, _, vn, wm, _ = _sgu_common(pre_ref, gv_ref, ws_ref)
        for g in range(N_GROUPS):
            cols = slice(g * LANES, (g + 1) * LANES)
            mix = _dot(wm[g], vn[:, cols]) + b_ref[:, cols]
            y_ref[:, cols] = (u[:, cols] * mix).astype(BF16)

    return _pcall(
        body, name=name, out_shape=_sds((t, D_MODEL), BF16), grid=(t // CHUNK,),
        in_specs=[pl.BlockSpec((CHUNK, 2 * D_MODEL), lambda i: (i, 0)), pl.BlockSpec((1, D_MODEL), lambda i: (0, 0)),
                  pl.BlockSpec((N_GROUPS, CHUNK, CHUNK), lambda i: (0, 0, 0)),
                  pl.BlockSpec((CHUNK, D_MODEL), lambda i: (0, 0))],
        out_specs=pl.BlockSpec((CHUNK, D_MODEL), lambda i: (i, 0)),
        semantics=("parallel",))(pre, g_v, w_s, b_full)


def head_norm(pre, g128, *, name, col_block=0, scale=1.0, passthrough=False, tm=512):
    t = pre.shape[0]
    tm = min(tm, t)

    def body(*refs):
        if passthrough:
            x_ref, v_ref, g_ref, o_ref, vo_ref = refs
            vo_ref[...] = v_ref[...].astype(BF16)
        else:
            x_ref, g_ref, o_ref = refs
        g = g_ref[...] * scale
        for b in range(D_MODEL // LANES):
            cols = slice(b * LANES, (b + 1) * LANES)
            xv = x_ref[:, cols]
            o_ref[:, cols] = ((xv * _head_rstd(xv)) * g).astype(BF16)

    x_spec = pl.BlockSpec((tm, D_MODEL), lambda i: (i, col_block))
    g_spec = pl.BlockSpec((1, LANES), lambda i: (0, 0))
    o_spec = pl.BlockSpec((tm, D_MODEL), lambda i: (i, 0))
    if passthrough:
        return _pcall(body, name=name, out_shape=[_sds((t, D_MODEL), BF16)] * 2, grid=(t // tm,),
                      in_specs=[x_spec, pl.BlockSpec((tm, D_MODEL), lambda i: (i, 1)), g_spec],
                      out_specs=[o_spec, o_spec], semantics=("parallel",))(pre, pre, g128)
    return _pcall(body, name=name, out_shape=_sds((t, D_MODEL), BF16), grid=(t // tm,),
                  in_specs=[x_spec, g_spec], out_specs=o_spec, semantics=("parallel",))(pre, g128)


def _suffix_matrix(n):
    r = lax.broadcasted_iota(jnp.int32, (n, n), 0)
    c = lax.broadcasted_iota(jnp.int32, (n, n), 1)
    return jnp.where(r > c, 1.0, 0.0).astype(BF16)


def _prefix_matrix(n):
    r = lax.broadcasted_iota(jnp.int32, (n, n), 0)
    c = lax.broadcasted_iota(jnp.int32, (n, n), 1)
    return jnp.where(r < c, 1.0, 0.0).astype(BF16)


def _block_cumsum(a, tri):
    return _dot(a.astype(BF16), tri)


def _stacked_causal(n):
    r = lax.broadcasted_iota(jnp.int32, (2 * n, n), 0)
    c = lax.broadcasted_iota(jnp.int32, (2 * n, n), 1)
    return c < jnp.where(r >= n, r - n, r)


def _stack_heads(a, low):
    zero = jnp.zeros_like(a)
    return jnp.concatenate([jnp.where(low, a, zero), jnp.where(low, zero, a)], axis=0)


def stick_breaking_forward(q, k, v, *, name):
    t = q.shape[0]
    blk = min(ATT_BLOCK, t)
    nq = t // blk

    def body(q_ref, k_ref, v_ref, o_ref):
        i = pl.program_id(1)
        low = lax.broadcasted_iota(jnp.int32, (blk, LANES), 1) < HEAD_DIM
        tri = _suffix_matrix(blk)
        causal = _stacked_causal(blk)
        qs = _stack_heads(q_ref[...], low)

        def block(j, carry, acc, masked):
            rows = pl.ds(pl.multiple_of(j * blk, blk), blk)
            z = _dot_nt(qs, k_ref[rows, :])
            ls = _log_sigmoid(z)
            lg = ls - z
            if masked:
                lg = jnp.where(causal, lg, 0.0)
            s = ls + _block_cumsum(lg, tri) + carry
            a = jnp.exp(s)
            if masked:
                a = jnp.where(causal, a, 0.0)
            acc = acc + _dot(a.astype(BF16), v_ref[rows, :])
            return carry + jnp.sum(lg, axis=-1, keepdims=True), acc

        state = block(i, jnp.zeros((2 * blk, 1), F32), jnp.zeros((2 * blk, LANES), F32), True)

        def two_blocks(n, st):
            st = block(i - 1 - 2 * n, st[0], st[1], False)
            return block(i - 2 - 2 * n, st[0], st[1], False)

        state = lax.fori_loop(0, i // 2, two_blocks, state)
        _, acc = lax.fori_loop(0, i % 2, lambda n, st: block(0, st[0], st[1], False), state)
        o_ref[...] = jnp.where(low, acc[:blk], acc[blk:]).astype(BF16)

    return _pcall(
        body, name=name, out_shape=_sds((t, D_MODEL), BF16), grid=(D_MODEL // LANES, nq),
        in_specs=[pl.BlockSpec((blk, LANES), lambda p, i: (i, p)), pl.BlockSpec((t, LANES), lambda p, i: (0, p)),
                  pl.BlockSpec((t, LANES), lambda p, i: (0, p))],
        out_specs=pl.BlockSpec((blk, LANES), lambda p, i: (i, p)),
        semantics=("parallel", "arbitrary"))(q, k, v)


def loss_forward(x, target, *, name, tm=512):
    t, d = x.shape
    tm = min(tm, t)

    def body(x_ref, t_ref, l_ref, dx_ref):
        @pl.when(pl.program_id(0) == 0)
        def _():
            l_ref[...] = jnp.zeros_like(l_ref)

        diff = x_ref[...] - t_ref[...]
        dx_ref[...] = diff * (1.0 / d)
        l_ref[...] += 0.5 * jnp.sum(jnp.mean(diff * diff, axis=-1, keepdims=True))

    return _pcall(
        body, name=name, out_shape=[_sds((8, LANES), F32), _sds((t, d), F32)], grid=(t // tm,),
        in_specs=[pl.BlockSpec((tm, d), lambda i: (i, 0))] * 2,
        out_specs=[pl.BlockSpec((8, LANES), lambda i: (0, 0)), pl.BlockSpec((tm, d), lambda i: (i, 0))],
        semantics=("arbitrary",))(x, target)


def matmul_nt(dy, w, *, name, mul=None, out_dtype=F32, tm=512):
    t, n = dy.shape
    k = w.shape[0]
    tm = min(tm, t)

    def body(*refs):
        if mul is None:
            dy_ref, w_ref, o_ref = refs
        else:
            dy_ref, w_ref, m_ref, o_ref = refs
        y = _dot_nt(dy_ref[...].astype(BF16), w_ref[...])
        if mul is not None:
            y = y * (2.0 * m_ref[...].astype(F32))
        o_ref[...] = y.astype(out_dtype)

    row = lambda i: (i, 0)
    in_specs = [pl.BlockSpec((tm, n), row), _full(w.shape)]
    args = [dy, w]
    if mul is not None:
        in_specs.append(pl.BlockSpec((tm, k), row))
        args.append(mul)
    return _pcall(body, name=name, out_shape=_sds((t, k), out_dtype), grid=(t // tm,), in_specs=in_specs,
                  out_specs=pl.BlockSpec((tm, k), row), semantics=("parallel",))(*args)


def matmul_tn(a, dy, *, name, col_shards, tk=512):
    t, k = a.shape
    n = dy.shape[1]
    if col_shards:
        tn = n // N_SHARDS

        def body(a_ref, dy_ref, o_ref):
            o_ref[...] = _dot_tn(a_ref[...].astype(BF16), dy_ref[...].astype(BF16))

        return _pcall(body, name=name, out_shape=_sds((N_SHARDS, k, tn), F32), grid=(N_SHARDS,),
                      in_specs=[_full((t, k)), pl.BlockSpec((t, tn), lambda j: (0, j))],
                      out_specs=pl.BlockSpec((None, k, tn), lambda j: (j, 0, 0)), semantics=("parallel",))(a, dy)

    tk = min(tk, k)

    def body(a_ref, dy_ref, o_ref, dy_bf):
        @pl.when(pl.program_id(0) == 0)
        def _():
            dy_bf[...] = dy_ref[...].astype(BF16)

        o_ref[...] = _dot_tn(a_ref[...].astype(BF16), dy_bf[...])

    return _pcall(body, name=name, out_shape=_sds((k, n), F32), grid=(k // tk,),
                  in_specs=[pl.BlockSpec((t, tk), lambda i: (0, i)), _full((t, n))],
                  out_specs=pl.BlockSpec((tk, n), lambda i: (i, 0)),
                  scratch_shapes=[pltpu.VMEM((t, n), BF16)], semantics=("arbitrary",))(a, dy)


def norm_backward(dpre, w, x, g, rstd, dx_out, *, name, tm=512):
    t, d = x.shape
    n = dpre.shape[1]
    tm = min(tm, t)
    if w.ndim == 3:
        w_spec = pl.BlockSpec(w.shape, lambda i: (0, 0, 0))
    else:
        w_spec = pl.BlockSpec(w.shape, lambda i: (0, 0))

    def body(dp_ref, w_ref, x_ref, g_ref, r_ref, dxo_ref, dx_ref, dg_ref):
        @pl.when(pl.program_id(0) == 0)
        def _():
            dg_ref[...] = jnp.zeros_like(dg_ref)

        if w.ndim == 3:
            per = n // N_SHARDS
            dh = _dot_nt(dp_ref[:, 0:per], w_ref[0])
            for s in range(1, N_SHARDS):
                dh = dh + _dot_nt(dp_ref[:, s * per:(s + 1) * per], w_ref[s])
        else:
            dh = _dot_nt(dp_ref[...], w_ref[...])
        r = r_ref[...]
        xn = x_ref[...] * r
        dg_ref[...] += jnp.sum(dh * xn, axis=0, keepdims=True)
        dxn = dh * g_ref[...]
        dx = r * (dxn - xn * jnp.mean(dxn * xn, axis=-1, keepdims=True))
        dx_ref[...] = dxo_ref[...] + dx

    row = lambda i: (i, 0)
    fixed = lambda i: (0, 0)
    return _pcall(
        body, name=name, out_shape=[_sds((t, d), F32), _sds((1, d), F32)], grid=(t // tm,),
        in_specs=[pl.BlockSpec((tm, n), row), w_spec, pl.BlockSpec((tm, d), row),
                  pl.BlockSpec((1, d), fixed), pl.BlockSpec((tm, 1), row), pl.BlockSpec((tm, d), row)],
        out_specs=[pl.BlockSpec((tm, d), row), pl.BlockSpec((1, d), fixed)],
        semantics=("arbitrary",))(dpre, w, x, g, rstd, dx_out)


def ple_backward(dx, gate, pp, *, name, tm=512):
    t, d = dx.shape
    tm = min(tm, t)

    def body(dx_ref, gate_ref, pp_ref, dg_ref, dp_ref):
        dxv = dx_ref[...]
        gate = gate_ref[...]
        dg_ref[...] = (dxv * pp_ref[...].astype(F32) * (gate * (1.0 - gate))).astype(BF16)
        dp_ref[...] = (dxv * gate).astype(BF16)

    spec = pl.BlockSpec((tm, d), lambda i: (i, 0))
    return _pcall(body, name=name, out_shape=[_sds((t, d), BF16)] * 2, grid=(t // tm,), in_specs=[spec] * 3,
                  out_specs=[spec] * 2, semantics=("parallel",))(dx, gate, pp)


def sgu_backward(dy, pre, g_v, w_s, b_full, *, name):
    t = pre.shape[0]
    n_chunks = t // CHUNK

    def body(dy_ref, pre_ref, gv_ref, ws_ref, b_ref, dpre_ref, dws_ref, db_ref, dgv_ref, dvn_s, dbf_s):
        step = pl.program_id(0)

        @pl.when(step == 0)
        def _():
            dws_ref[...] = jnp.zeros_like(dws_ref)
            dgv_ref[...] = jnp.zeros_like(dgv_ref)
            dbf_s[...] = jnp.zeros_like(dbf_s)

        pre_u, pre_v, u, r, vhat, vn, wm, tril = _sgu_common(pre_ref, gv_ref, ws_ref)
        dyv = dy_ref[...]
        for g in range(N_GROUPS):
            cols = slice(g * LANES, (g + 1) * LANES)
            mix = _dot(wm[g], vn[:, cols]) + b_ref[:, cols]
            dmix = dyv[:, cols] * u[:, cols]
            dmix_b = dmix.astype(BF16)
            du = dyv[:, cols] * mix
            dpre_ref[:, cols] = (du * _gelu_grad(pre_u[:, cols])).astype(BF16)
            dws_ref[g] += jnp.where(tril, _dot_nt(dmix_b, vn[:, cols]), 0.0)
            dbf_s[:, cols] += dmix
            dvn_s[:, cols] = _dot_tn(wm[g], dmix_b)
        dvn = dvn_s[...]
        dgv_ref[...] += jnp.sum(dvn * vhat, axis=0, keepdims=True)
        dxn = dvn * gv_ref[...]
        dv = r * (dxn - vhat * jnp.mean(dxn * vhat, axis=-1, keepdims=True))
        dpre_ref[:, D_MODEL:] = (dv * _gelu_grad(pre_v)).astype(BF16)

        @pl.when(step == n_chunks - 1)
        def _():
            lane = lax.broadcasted_iota(jnp.int32, (CHUNK, LANES), 1)
            acc = jnp.zeros((CHUNK, LANES), F32)
            for g in range(N_GROUPS):
                s = jnp.sum(dbf_s[:, g * LANES:(g + 1) * LANES], axis=-1, keepdims=True)
                acc = jnp.where(lane == g, s, acc)
            db_ref[...] = acc

    fixed2 = lambda i: (0, 0)
    return _pcall(
        body, name=name,
        out_shape=[_sds((t, 2 * D_MODEL), BF16), _sds((N_GROUPS, CHUNK, CHUNK), F32), _sds((CHUNK, LANES), F32),
                   _sds((1, D_MODEL), F32)],
        grid=(n_chunks,),
        in_specs=[pl.BlockSpec((CHUNK, D_MODEL), lambda i: (i, 0)), pl.BlockSpec((CHUNK, 2 * D_MODEL), lambda i: (i, 0)),
                  pl.BlockSpec((1, D_MODEL), fixed2), pl.BlockSpec((N_GROUPS, CHUNK, CHUNK), lambda i: (0, 0, 0)),
                  pl.BlockSpec((CHUNK, D_MODEL), fixed2)],
        out_specs=[pl.BlockSpec((CHUNK, 2 * D_MODEL), lambda i: (i, 0)),
                   pl.BlockSpec((N_GROUPS, CHUNK, CHUNK), lambda i: (0, 0, 0)), pl.BlockSpec((CHUNK, LANES), fixed2),
                   pl.BlockSpec((1, D_MODEL), fixed2)],
        scratch_shapes=[pltpu.VMEM((CHUNK, D_MODEL), F32), pltpu.VMEM((CHUNK, D_MODEL), F32)],
        semantics=("arbitrary",))(dy, pre, g_v, w_s, b_full)


def head_norm_backward(dy, pre, g128, *, name, col_block=0, scale=1.0, passthrough=None, tm=512):
    t = dy.shape[0]
    tm = min(tm, t)
    width = 2 * D_MODEL if passthrough is not None else D_MODEL

    def body(*refs):
        if passthrough is not None:
            dy_ref, x_ref, g_ref, dv_ref, o_ref, dg_ref = refs
            o_ref[:, D_MODEL:] = dv_ref[...].astype(BF16)
        else:
            dy_ref, x_ref, g_ref, o_ref, dg_ref = refs

        @pl.when(pl.program_id(0) == 0)
        def _():
            dg_ref[...] = jnp.zeros_like(dg_ref)

        g = g_ref[...]
        dg = jnp.zeros((1, LANES), F32)
        for b in range(D_MODEL // LANES):
            cols = slice(b * LANES, (b + 1) * LANES)
            xv = x_ref[:, cols]
            r = _head_rstd(xv)
            xn = xv * r
            dyv = dy_ref[:, cols] * scale
            dg = dg + jnp.sum(dyv * xn, axis=0, keepdims=True)
            dxn = dyv * g
            o_ref[:, cols] = (r * (dxn - xn * _head_mean(dxn * xn))).astype(BF16)
        dg_ref[...] += dg

    row = lambda i: (i, 0)
    in_specs = [pl.BlockSpec((tm, D_MODEL), row), pl.BlockSpec((tm, D_MODEL), lambda i: (i, col_block)),
                pl.BlockSpec((1, LANES), lambda i: (0, 0))]
    args = [dy, pre, g128]
    if passthrough is not None:
        in_specs.append(pl.BlockSpec((tm, D_MODEL), row))
        args.append(passthrough)
    return _pcall(body, name=name, out_shape=[_sds((t, width), BF16), _sds((1, LANES), F32)], grid=(t // tm,),
                  in_specs=in_specs,
                  out_specs=[pl.BlockSpec((tm, width), row), pl.BlockSpec((1, LANES), lambda i: (0, 0))],
                  semantics=("arbitrary",))(*args)


def stick_breaking_backward(q, k, v, do, *, name):
    t = q.shape[0]
    blk = min(ATT_BLOCK, t)
    nq = t // blk

    def body(q_ref, k_ref, v_ref, do_ref, dq_ref, dk_ref, dv_ref, s_buf, sg_buf):
        i = pl.program_id(1)

        @pl.when(i == 0)
        def _():
            dk_ref[...] = jnp.zeros_like(dk_ref)
            dv_ref[...] = jnp.zeros_like(dv_ref)

        low = lax.broadcasted_iota(jnp.int32, (blk, LANES), 1) < HEAD_DIM
        suffix = _suffix_matrix(blk)
        prefix = _prefix_matrix(blk)
        causal = _stacked_causal(blk)
        qs = _stack_heads(q_ref[...], low)
        dos = _stack_heads(do_ref[...], low)

        def log_weights(j, carry, masked):
            rows = pl.ds(pl.multiple_of(j * blk, blk), blk)
            z = _dot_nt(qs, k_ref[rows, :])
            ls = _log_sigmoid(z)
            lg = ls - z
            if masked:
                lg = jnp.where(causal, lg, 0.0)
            s_buf[j] = ls + _block_cumsum(lg, suffix) + carry
            sg_buf[j] = jnp.exp(ls)
            return carry + jnp.sum(lg, axis=-1, keepdims=True)

        carry = log_weights(i, jnp.zeros((2 * blk, 1), F32), True)
        carry = lax.fori_loop(0, i // 2, lambda n, c: log_weights(i - 2 - 2 * n, log_weights(i - 1 - 2 * n, c, False),
                                                                  False), carry)
        lax.fori_loop(0, i % 2, lambda n, c: log_weights(0, c, False), carry)

        def grads(j, pcarry, dq_acc, masked):
            rows = pl.ds(pl.multiple_of(j * blk, blk), blk)
            a = jnp.exp(s_buf[j])
            if masked:
                a = jnp.where(causal, a, 0.0)
            sg = sg_buf[j]
            ds = _dot_nt(dos, v_ref[rows, :]) * a
            before = _block_cumsum(ds, prefix) + pcarry
            if masked:
                before = jnp.where(causal, before, 0.0)
            dz = (ds - sg * (ds + before)).astype(BF16)
            dq_acc = dq_acc + _dot(dz, k_ref[rows, :])
            dk_ref[rows, :] += _dot_tn(dz, qs)
            dv_ref[rows, :] += _dot_tn(a.astype(BF16), dos)
            return pcarry + jnp.sum(ds, axis=-1, keepdims=True), dq_acc

        def two_blocks(n, st):
            st = grads(2 * n, st[0], st[1], False)
            return grads(2 * n + 1, st[0], st[1], False)

        state = lax.fori_loop(0, i // 2, two_blocks,
                              (jnp.zeros((2 * blk, 1), F32), jnp.zeros((2 * blk, LANES), F32)))
        state = lax.fori_loop(0, i % 2, lambda n, st: grads(i - 1, st[0], st[1], False), state)
        _, dq_acc = grads(i, state[0], state[1], True)
        dq_ref[...] = jnp.where(low, dq_acc[:blk], dq_acc[blk:])

    full = pl.BlockSpec((t, LANES), lambda p, i: (0, p))
    qblk = pl.BlockSpec((blk, LANES), lambda p, i: (i, p))
    return _pcall(
        body, name=name, out_shape=[_sds((t, D_MODEL), F32)] * 3, grid=(D_MODEL // LANES, nq),
        in_specs=[qblk, full, full, qblk], out_specs=[qblk, full, full],
        scratch_shapes=[pltpu.VMEM((nq, 2 * blk, blk), F32), pltpu.VMEM((nq, 2 * blk, blk), F32)],
        semantics=("parallel", "arbitrary"))(q, k, v, do)


def _mlp_backward(dx, saved, g, w_up, w_down, tag):
    x, h, r, a, a2 = saved
    d_w_down = matmul_tn(a2, dx, name=f"d_w_down_{tag}", col_shards=False)
    dpre = matmul_nt(dx, w_down, name=f"d_mlp_act_{tag}", mul=a, out_dtype=BF16)
    d_w_up = matmul_tn(h, dpre, name=f"d_w_up_{tag}", col_shards=True)
    dx, d_g = norm_backward(dpre, w_up, x, g, r, dx, name=f"d_mlp_norm_{tag}")
    return dx, d_w_up, d_w_down, d_g


def _ple_backward(dx, saved, p, g, w_gate, tag):
    x, h, r, gate, pp = saved
    dgate, dproj = ple_backward(dx, gate, pp, name=f"d_ple_{tag}")
    d_w_proj = matmul_tn(p, dproj, name=f"d_w_ple_proj_{tag}", col_shards=True)
    d_w_gate = matmul_tn(h, dgate, name=f"d_w_ple_gate_{tag}", col_shards=False)
    dx, d_g = norm_backward(dgate, w_gate, x, g, r, dx, name=f"d_ple_norm_{tag}")
    return dx, d_w_gate, d_w_proj, d_g


def local_step(x, p, target, w, late=None):
    row = lambda v: v.reshape(1, -1)
    g128 = lambda v: jnp.tile(v.reshape(1, HEAD_DIM), (1, 2))
    scale = HEAD_DIM ** -0.5
    b_full = jnp.repeat(jnp.transpose(w["b_spatial"][0]), LANES, axis=1)
    w_s = w["w_spatial"][0]

    mats = {}
    for name, value in w.items():
        if isinstance(value, tuple):
            mats.update({(name, layer): v for layer, v in enumerate(value)})
    if "w_kv" in w:
        mats[("w_kv", 0)] = w["w_kv"]

    def fetch(name, layer, after):
        if (name, layer) not in mats:
            mats.update(late.weights(name, layer, after))
        return mats[(name, layer)]

    def mlp_forward(x_in, layer):
        h, r, a, a2 = norm_matmul(x_in, row(w["ln_mlp"][layer]), fetch("w_up", layer, x_in), name=f"mlp_up_{layer}",
                                  epilogue="relu2")
        return matmul_residual(a2, fetch("w_down", layer, a2), x_in, name=f"mlp_down_{layer}"), (x_in, h, r, a, a2)

    def ple(x_in, layer):
        return ple_forward(x_in, row(w["ln_ple"][layer]), fetch("w_ple_gate", layer, x_in), p[layer],
                           fetch("w_ple_proj", layer, x_in), name=f"ple_{layer}")

    x0 = x
    h_a, r_a, pre_a = norm_matmul(x0, row(w["ln_mix_a"][0]), fetch("w_in_a", 0, x0), name="sgu_in")
    y_a = sgu_forward(pre_a, row(w["g_v_a"][0]), w_s, b_full, name="sgu_mix")
    x1 = matmul_residual(y_a, fetch("w_out_a", 0, y_a), x0, name="sgu_out")
    x2, mlp0 = mlp_forward(x1, 0)
    ple0 = ple(x2, 0)
    x3 = ple0[4]
    h_kv, r_kv, kv_pre = norm_matmul(x3, row(w["ln_kv"]), fetch("w_kv", 0, x3), name="kv_proj")
    k_n, v_b = head_norm(kv_pre, g128(w["g_k"]), name="k_norm", passthrough=True)
    h_q, r_q, q_pre = norm_matmul(x3, row(w["ln_mix_b"][0]), fetch("w_q", 0, k_n), name="q_proj")
    q_n = head_norm(q_pre, g128(w["g_q"][0]), name="q_norm", scale=scale)
    o = stick_breaking_forward(q_n, k_n, v_b, name="sb_fwd")
    x4 = matmul_residual(o, fetch("w_out_b", 0, o), x3, name="sb_out")
    x5, mlp1 = mlp_forward(x4, 1)
    ple1 = ple(x5, 1)
    x6 = ple1[4]
    loss_blk, dx = loss_forward(x6, target, name="loss")

    g = {}
    dx, dwg1, dwp1, dlnp1 = _ple_backward(dx, (x5,) + tuple(ple1[:4]), p[1], row(w["ln_ple"][1]),
                                          mats[("w_ple_gate", 1)], 1)
    dx, dwu1, dwd1, dlnm1 = _mlp_backward(dx, mlp1, row(w["ln_mlp"][1]), mats[("w_up", 1)], mats[("w_down", 1)], 1)
    g["w_out_b"] = matmul_tn(o, dx, name="d_w_out_b", col_shards=False)
    do = matmul_nt(dx, mats[("w_out_b", 0)], name="d_sb_out", out_dtype=BF16)
    dq_n, dk_n, dv = stick_breaking_backward(q_n, k_n, v_b, do, name="sb_bwd")
    dq_pre, dgq = head_norm_backward(dq_n, q_pre, g128(w["g_q"][0]), name="d_q_norm", scale=scale)
    dkv_pre, dgk = head_norm_backward(dk_n, kv_pre, g128(w["g_k"]), name="d_k_norm", passthrough=dv)
    g["w_q"] = matmul_tn(h_q, dq_pre, name="d_w_q", col_shards=False)
    g["w_kv"] = matmul_tn(h_kv, dkv_pre, name="d_w_kv", col_shards=True)
    dx, g["ln_mix_b"] = norm_backward(dq_pre, mats[("w_q", 0)], x3, row(w["ln_mix_b"][0]), r_q, dx, name="d_q_in")
    dx, g["ln_kv"] = norm_backward(dkv_pre, mats[("w_kv", 0)], x3, row(w["ln_kv"]), r_kv, dx, name="d_kv_in")
    g["g_q"] = dgq[:, :HEAD_DIM] + dgq[:, HEAD_DIM:]
    g["g_k"] = (dgk[:, :HEAD_DIM] + dgk[:, HEAD_DIM:]).reshape(HEAD_DIM)
    g["ln_kv"] = g["ln_kv"].reshape(D_MODEL)
    ln_ple0, ln_mlp0, g_v0, ln_mix0 = (row(w["ln_ple"][0]), row(w["ln_mlp"][0]), row(w["g_v_a"][0]),
                                       row(w["ln_mix_a"][0]))
    if late is not None:
        ln_ple0 = ln_ple0 + late.pair_start(
            {("w_kv", 0): g["w_kv"], ("w_q", 0): g["w_q"], ("w_out_b", 0): g["w_out_b"], ("w_up", 1): dwu1,
             ("w_down", 1): dwd1, ("w_ple_gate", 1): dwg1, ("w_ple_proj", 1): dwp1}, dx)[0, 0]
    dx, dwg0, dwp0, dlnp0 = _ple_backward(dx, (x2,) + tuple(ple0[:4]), p[0], ln_ple0, mats[("w_ple_gate", 0)], 0)
    if late is not None:
        ln_mlp0 = ln_mlp0 + late.chip_start(dx)[0, 0]
    dx, dwu0, dwd0, dlnm0 = _mlp_backward(dx, mlp0, ln_mlp0, mats[("w_up", 0)], mats[("w_down", 0)], 0)
    if late is not None:
        g_v0 = g_v0 + late.pair_start({("w_up", 0): dwu0, ("w_down", 0): dwd0, ("w_ple_gate", 0): dwg0,
                                       ("w_ple_proj", 0): dwp0}, dx)[0, 0]
    g["w_out_a"] = matmul_tn(y_a, dx, name="d_w_out_a", col_shards=False)
    dy_a = matmul_nt(dx, mats[("w_out_a", 0)], name="d_sgu_out")
    dpre_a, dws, db, g["g_v_a"] = sgu_backward(dy_a, pre_a, g_v0, w_s, b_full, name="d_sgu_mix")
    if late is not None:
        ln_mix0 = ln_mix0 + late.chip_start(dpre_a)[0, 0]
    g["w_in_a"] = matmul_tn(h_a, dpre_a, name="d_w_in_a", col_shards=True)
    dx, g["ln_mix_a"] = norm_backward(dpre_a, mats[("w_in_a", 0)], x0, ln_mix0, r_a, dx, name="d_sgu_in")
    g["w_spatial"] = dws[None]
    g["b_spatial"] = jnp.transpose(db[:, :N_GROUPS])[None]
    g["w_up"] = (dwu0, dwu1)
    g["w_down"] = (dwd0, dwd1)
    g["w_ple_gate"] = (dwg0, dwg1)
    g["w_ple_proj"] = (dwp0, dwp1)
    g["ln_mlp"] = jnp.concatenate([dlnm0, dlnm1], axis=0)
    g["ln_ple"] = jnp.concatenate([dlnp0, dlnp1], axis=0)
    return loss_blk, dx, g


ANY = pl.BlockSpec(memory_space=pl.ANY)


def _place():
    x, y, c = lax.axis_index("x"), lax.axis_index("y"), lax.axis_index("c")
    others = [(1 - x, y), (x, 1 - y), (1 - x, 1 - y)]
    return x, y, c, 2 * x + y, others


def cast_into_slot(w3, layer, slot, *, name, after=None, tm=256):
    _, r, c = w3.shape
    tm = min(tm, r)

    def body(slot_ref, w_ref, *rest):
        rest[-1][...] = w_ref[...].astype(BF16)

    in_specs = [pl.BlockSpec((None, tm, c), lambda i, s: (layer, i, 0))]
    args = [slot, w3]
    if after is not None:
        in_specs.append(ANY)
        args.append(after)
    return _pcall(body, name=name, out_shape=_sds((N_SHARDS, r, c), BF16), grid=(r // tm,), num_prefetch=1,
                  in_specs=in_specs, out_specs=pl.BlockSpec((None, tm, c), lambda i, s: (s[0], i, 0)),
                  semantics=("parallel",))(*args)


def gather_shards(mats, vecs, *, name):
    nm, nv = len(mats), len(vecs)
    halves = [m.reshape(N_SHARDS, 2, m.shape[1] // 2, m.shape[2]) for m in mats]

    def body(*refs):
        vsrc = refs[nm:nm + nv]
        out, vout = refs[nm + nv:2 * nm + nv], refs[2 * nm + nv:2 * (nm + nv)]
        send, recv, vsend, vrecv, loc = refs[2 * (nm + nv):]
        x, y, c, s_me, others = _place()
        sib = (x, y, 1 - c)

        def ici(l, k):
            ox, oy = others[k]
            return pltpu.make_async_remote_copy(out[l].at[s_me, c], out[l].at[s_me, c], send.at[l, k], recv.at[l, k],
                                                device_id=(ox, oy, c), device_id_type=MESH)

        def landed(l, k, half):
            ox, oy = others[k]
            return out[l].at[2 * ox + oy, half]

        def passed_on(l, k):
            return pltpu.make_async_remote_copy(landed(l, k, c), landed(l, k, c), send.at[l, 3 + k], recv.at[l, 3 + k],
                                                device_id=sib, device_id_type=MESH)

        def vec(l, k):
            ox, oy = others[k]
            return pltpu.make_async_remote_copy(vsrc[l], vout[l].at[s_me], vsend.at[l, k], vrecv.at[l, k],
                                                device_id=(ox, oy, c), device_id_type=MESH)

        for l in range(nm):
            for k in range(3):
                ici(l, k).start()
        for l in range(nv):
            for k in range(3):
                vec(l, k).start()
        for l in range(nv):
            own = pltpu.make_async_copy(vsrc[l], vout[l].at[s_me], loc)
            own.start()
            own.wait()
        for l in range(nm):
            for k in range(3):
                pltpu.make_async_remote_copy(landed(l, k, c), landed(l, k, c), send.at[l, k], recv.at[l, k],
                                             device_id=sib, device_id_type=MESH).wait_recv()
                passed_on(l, k).start()
        for l in range(nm):
            for k in range(3):
                pltpu.make_async_remote_copy(landed(l, k, 1 - c), landed(l, k, 1 - c), send.at[l, 3 + k],
                                             recv.at[l, 3 + k], device_id=sib, device_id_type=MESH).wait_recv()
        for l in range(nv):
            for k in range(3):
                ox, oy = others[k]
                pltpu.make_async_remote_copy(vsrc[l], vout[l].at[2 * ox + oy], vsend.at[l, k], vrecv.at[l, k],
                                             device_id=sib, device_id_type=MESH).wait_recv()
        for l in range(nm):
            for k in range(3):
                ici(l, k).wait_send()
                passed_on(l, k).wait_send()
        for l in range(nv):
            for k in range(3):
                vec(l, k).wait_send()

    out_shape = [_sds(h.shape, BF16) for h in halves] + [_sds((N_SHARDS,) + v.shape, F32) for v in vecs]
    res = _pcall(body, name=name, out_shape=out_shape, in_specs=[ANY] * (nm + nv), out_specs=[ANY] * (nm + nv),
                 scratch_shapes=[pltpu.SemaphoreType.DMA((max(nm, 1), 6)), pltpu.SemaphoreType.DMA((max(nm, 1), 6)),
                                 pltpu.SemaphoreType.DMA((max(nv, 1), 3)), pltpu.SemaphoreType.DMA((max(nv, 1), 3)),
                                 pltpu.SemaphoreType.DMA(())],
                 aliases={l: l for l in range(nm)}, side_effects=True)(*halves, *vecs)
    return [r.reshape(m.shape) for r, m in zip(res[:nm], mats)], list(res[nm:])


HBM = pl.BlockSpec(memory_space=pltpu.HBM)
SEM = pl.BlockSpec(memory_space=pltpu.SEMAPHORE)
DATAFLOW = pltpu.SideEffectType.DATAFLOW_SIDE_EFFECTING


def _split_call(body, *, name, out_shape, in_specs, out_specs, aliases):
    return pl.pallas_call(body, name=name, out_shape=out_shape, in_specs=in_specs, out_specs=out_specs,
                          input_output_aliases=aliases,
                          compiler_params=pltpu.CompilerParams(has_side_effects=DATAFLOW))


def _token_shape():
    return jax.ShapeDtypeStruct((8, LANES), F32)


def gather_start(mats, after, *, name):
    n = len(mats)
    halves = [pltpu.with_memory_space_constraint(m.reshape(N_SHARDS, 2, m.shape[1] // 2, m.shape[2]), pltpu.HBM)
              for m in mats]

    def body(*refs):
        send, recv = refs[n + 1], refs[n + 2]
        out, token = refs[n + 3:2 * n + 3], refs[2 * n + 3]
        x, y, c, s_me, others = _place()
        for l in range(n):
            for k in range(3):
                ox, oy = others[k]
                pltpu.make_async_remote_copy(out[l].at[s_me, c], out[l].at[s_me, c], send.at[3 * l + k],
                                             recv.at[3 * l + k], device_id=(ox, oy, c), device_id_type=MESH).start()
        token[...] = jnp.zeros_like(token)

    res = _split_call(
        body, name=name,
        out_shape=(pltpu.SemaphoreType.DMA((3 * n,)), pltpu.SemaphoreType.DMA((3 * n,)),
                   *[pltpu.HBM(h.shape, BF16) for h in halves], _token_shape()),
        in_specs=[HBM] * n + [ANY], out_specs=(SEM, SEM, *[HBM] * n, pl.BlockSpec(memory_space=pltpu.VMEM)),
        aliases={l: 2 + l for l in range(n)})(*halves, after)
    return res[0], res[1], list(res[2:2 + n]), res[2 + n]


def gather_pass_on(bufs, send_a, recv_a, after, *, name, base=0):
    n = len(bufs)

    def body(*refs):
        send_a, recv_a = refs[n], refs[n + 1]
        out = refs[n + 3:2 * n + 3]
        send_b, recv_b, token = refs[2 * n + 3:]
        x, y, c, s_me, others = _place()
        for l in range(n):
            for k in range(3):
                ox, oy = others[k]
                landed, i = out[l].at[2 * ox + oy, c], 3 * l + k
                pltpu.make_async_remote_copy(landed, landed, send_a.at[3 * base + i], recv_a.at[3 * base + i],
                                             device_id=(x, y, 1 - c), device_id_type=MESH).wait_recv()
                pltpu.make_async_remote_copy(landed, landed, send_b.at[i], recv_b.at[i],
                                             device_id=(x, y, 1 - c), device_id_type=MESH).start()
        for l in range(n):
            for k in range(3):
                mine, i = out[l].at[s_me, c], 3 * (base + l) + k
                pltpu.make_async_remote_copy(mine, mine, send_a.at[i], recv_a.at[i],
                                             device_id=(x, y, 1 - c), device_id_type=MESH).wait_send()
        token[...] = jnp.zeros_like(token)

    res = _split_call(
        body, name=name,
        out_shape=(*[pltpu.HBM(b.shape, BF16) for b in bufs], pltpu.SemaphoreType.DMA((3 * n,)),
                   pltpu.SemaphoreType.DMA((3 * n,)), _token_shape()),
        in_specs=[HBM] * n + [SEM, SEM, ANY],
        out_specs=(*[HBM] * n, SEM, SEM, pl.BlockSpec(memory_space=pltpu.VMEM)),
        aliases={l: l for l in range(n)})(*bufs, send_a, recv_a, after)
    return list(res[:n]), res[n], res[n + 1], res[n + 2]


def gather_finish(bufs, send_b, recv_b, after, shapes, *, name):
    n = len(bufs)

    def body(*refs):
        send_b, recv_b = refs[n], refs[n + 1]
        out = refs[n + 3:]
        x, y, c, _, others = _place()
        for l in range(n):
            for k in range(3):
                ox, oy = others[k]
                theirs, mine, i = out[l].at[2 * ox + oy, 1 - c], out[l].at[2 * ox + oy, c], 3 * l + k
                pltpu.make_async_remote_copy(theirs, theirs, send_b.at[i], recv_b.at[i],
                                             device_id=(x, y, 1 - c), device_id_type=MESH).wait_recv()
                pltpu.make_async_remote_copy(mine, mine, send_b.at[i], recv_b.at[i],
                                             device_id=(x, y, 1 - c), device_id_type=MESH).wait_send()

    res = _split_call(
        body, name=name, out_shape=tuple(pltpu.HBM(b.shape, BF16) for b in bufs),
        in_specs=[HBM] * n + [SEM, SEM, ANY], out_specs=tuple([HBM] * n),
        aliases={l: l for l in range(n)})(*bufs, send_b, recv_b, after)
    return [r.reshape(s) for r, s in zip(res, shapes)]


def exchange_start(srcs, dst_shapes, dst_dtype, plan, count, after, *, name):
    n, m = len(srcs), len(dst_shapes)
    srcs = [pltpu.with_memory_space_constraint(s, pltpu.HBM) for s in srcs]
    lands = [pltpu.with_memory_space_constraint(lax.empty(s, dst_dtype), pltpu.HBM) for s in dst_shapes]

    def body(*refs):
        send, recv = refs[n + m + 1], refs[n + m + 2]
        src, dst, token = refs[n + m + 3:2 * n + m + 3], refs[2 * n + m + 3:2 * (n + m) + 3], refs[2 * (n + m) + 3]
        for i, (s, d, dev) in enumerate(plan(_place(), src, dst)):
            pltpu.make_async_remote_copy(s, d, send.at[i], recv.at[i], device_id=dev, device_id_type=MESH).start()
        token[...] = jnp.zeros_like(token)

    res = _split_call(
        body, name=name,
        out_shape=(pltpu.SemaphoreType.DMA((count,)), pltpu.SemaphoreType.DMA((count,)),
                   *[pltpu.HBM(s.shape, s.dtype) for s in srcs], *[pltpu.HBM(s, dst_dtype) for s in dst_shapes],
                   _token_shape()),
        in_specs=[HBM] * (n + m) + [ANY],
        out_specs=(SEM, SEM, *[HBM] * (n + m), pl.BlockSpec(memory_space=pltpu.VMEM)),
        aliases={i: 2 + i for i in range(n + m)})(*srcs, *lands, after)
    return (list(res[2:2 + n]), list(res[2 + n:2 + n + m]), res[0], res[1], plan), res[2 + n + m]


def exchange_finish(state, after, *, name):
    srcs, lands, send, recv, plan = state
    n, m = len(srcs), len(lands)

    def body(*refs):
        send, recv = refs[n + m], refs[n + m + 1]
        src, dst = refs[n + m + 3:2 * n + m + 3], refs[2 * n + m + 3:]
        for i, (s, d, dev) in enumerate(plan(_place(), src, dst)):
            pltpu.make_async_remote_copy(s, d, send.at[i], recv.at[i], device_id=dev, device_id_type=MESH).wait()

    res = _split_call(
        body, name=name,
        out_shape=tuple(pltpu.HBM(a.shape, a.dtype) for a in srcs + lands),
        in_specs=[HBM] * (n + m) + [SEM, SEM, ANY], out_specs=tuple([HBM] * (n + m)),
        aliases={i: i for i in range(n + m)})(*srcs, *lands, send, recv, after)
    return list(res[:n]), list(res[n:])


def pair_plan(place, src, dst):
    x, y, c, _, _ = place
    return [(s.at[:, 1 - c], d, (x, y, 1 - c)) for s, d in zip(src, dst)]


def chip_plan(place, src, dst):
    x, y, c, _, others = place
    return [(s.at[2 * ox + oy], d.at[k], (ox, oy, c)) for s, d in zip(src, dst) for k, (ox, oy) in enumerate(others)]


def pair_exchange(grads, *, name):
    n = len(grads)

    def body(*refs):
        src, got = refs[:n], refs[n:2 * n]
        send, recv = refs[2 * n:]
        x, y, c, _, _ = _place()

        def swap(l):
            return pltpu.make_async_remote_copy(src[l].at[:, 1 - c], got[l], send.at[l], recv.at[l],
                                                device_id=(x, y, 1 - c), device_id_type=MESH)

        for l in range(n):
            swap(l).start()
        for l in range(n):
            swap(l).wait()

    res = _pcall(body, name=name, out_shape=[_sds((N_SHARDS,) + g.shape[2:], F32) for g in grads],
                 in_specs=[ANY] * n, out_specs=[ANY] * n,
                 scratch_shapes=[pltpu.SemaphoreType.DMA((n,)), pltpu.SemaphoreType.DMA((n,))],
                 side_effects=True)(*grads)
    return list(res)


def add_to_wire(mine, theirs, core, *, name, tm=256):
    s, _, r, c = mine.shape
    tm = min(tm, r)

    def body(core_ref, a_ref, b_ref, o_ref):
        o_ref[...] = (a_ref[...] + b_ref[...]).astype(BF16)

    spec = pl.BlockSpec((None, tm, c), lambda i, j, cr: (i, j, 0))
    return _pcall(body, name=name, out_shape=_sds((s, r, c), BF16), grid=(s, r // tm), num_prefetch=1,
                  in_specs=[pl.BlockSpec((None, None, tm, c), lambda i, j, cr: (i, cr[0], j, 0)), spec],
                  out_specs=spec, semantics=("parallel", "parallel"))(core, mine, theirs)


def sum_chips(wire, landed, place, dest, layer, n_layers, *, name, tm=256):
    _, r, c = wire.shape
    tm = min(tm, r)

    def body(place_ref, w_ref, l_ref, *rest):
        o_ref = rest[-1]
        o_ref[...] = ((w_ref[...].astype(F32) + l_ref[0].astype(F32)) + l_ref[1].astype(F32)) + l_ref[2].astype(F32)

    in_specs = [pl.BlockSpec((None, tm, c), lambda i, pr: (pr[0], i, 0)),
                pl.BlockSpec((3, tm, c), lambda i, pr: (0, i, 0))]
    args = [place, wire, landed]
    aliases = None
    if dest is not None:
        in_specs.append(ANY)
        args.append(dest)
        aliases = {3: 0}
    return _pcall(body, name=name, out_shape=_sds((n_layers, 2, r, c), F32), grid=(r // tm,), num_prefetch=1,
                  in_specs=in_specs,
                  out_specs=pl.BlockSpec((None, None, tm, c), lambda i, pr: (layer, pr[1], i, 0)),
                  aliases=aliases, semantics=("parallel",))(*args)


def pair_share(bufs, slots, *, name):
    n = len(bufs)

    def body(*refs):
        out = refs[n:2 * n]
        send, recv = refs[2 * n:]
        x, y, c, _, _ = _place()

        def share(i, half):
            o, l = slots[i]
            return pltpu.make_async_remote_copy(out[o].at[l, half], out[o].at[l, half], send.at[i], recv.at[i],
                                                device_id=(x, y, 1 - c), device_id_type=MESH)

        for i in range(len(slots)):
            share(i, c).start()
        for i in range(len(slots)):
            share(i, 1 - c).wait_recv()
            share(i, c).wait_send()

    res = _pcall(body, name=name, out_shape=[_sds(b.shape, F32) for b in bufs], in_specs=[ANY] * n,
                 out_specs=[ANY] * n,
                 scratch_shapes=[pltpu.SemaphoreType.DMA((len(slots),)), pltpu.SemaphoreType.DMA((len(slots),))],
                 aliases={o: o for o in range(n)}, side_effects=True)(*bufs)
    return list(res)


def all_reduce_small(packed, *, name):
    n_dev, r, c = packed.shape

    def body(in_ref, out_ref, land, send, recv):
        x, y, cc, _, _ = _place()
        me = 4 * x + 2 * y + cc
        peers = [(px, py, pc) for px in range(2) for py in range(2) for pc in range(2)]

        def scatter(d):
            return pltpu.make_async_remote_copy(in_ref.at[d], land.at[me], send.at[0, d], recv.at[0, me],
                                                device_id=peers[d], device_id_type=MESH)

        def gather(d):
            return pltpu.make_async_remote_copy(out_ref.at[me], out_ref.at[me], send.at[1, d], recv.at[1, me],
                                                device_id=peers[d], device_id_type=MESH)

        for d in range(n_dev):
            @pl.when(d != me)
            def _():
                scatter(d).start()
        land[me] = in_ref[me]
        for d in range(n_dev):
            @pl.when(d != me)
            def _():
                pltpu.make_async_remote_copy(in_ref.at[d], land.at[d], send.at[0, d], recv.at[0, d],
                                             device_id=peers[d], device_id_type=MESH).wait_recv()
        total = land[0]
        for d in range(1, n_dev):
            total = total + land[d]
        out_ref[me] = total
        for d in range(n_dev):
            @pl.when(d != me)
            def _():
                gather(d).start()
        for d in range(n_dev):
            @pl.when(d != me)
            def _():
                pltpu.make_async_remote_copy(out_ref.at[d], out_ref.at[d], send.at[1, d], recv.at[1, d],
                                             device_id=peers[d], device_id_type=MESH).wait_recv()
        for d in range(n_dev):
            @pl.when(d != me)
            def _():
                scatter(d).wait_send()
                gather(d).wait_send()

    vm = pl.BlockSpec(memory_space=pltpu.VMEM)
    return _pcall(body, name=name, out_shape=_sds(packed.shape, F32), in_specs=[vm], out_specs=vm,
                  scratch_shapes=[pltpu.VMEM(packed.shape, F32), pltpu.SemaphoreType.DMA((2, n_dev)),
                                  pltpu.SemaphoreType.DMA((2, n_dev))],
                  side_effects=True)(packed)


def adamw(w, g, m, v, *, name, part=None, dest=None, tm=256):
    shape = w.shape
    cols = shape[-1]
    rows = 1
    for s in shape[:-1]:
        rows *= s
    first, count = 0, rows
    if part is not None:
        count = rows // part[1]
        first = part[0] * count
    tm = min(tm, count)
    assert count % tm == 0
    two_d = lambda a: a.reshape(rows, cols)

    def body(w_ref, g_ref, m_ref, v_ref, *rest):
        d_ref, mo_ref, vo_ref = rest[-3:]
        gv = g_ref[...]
        m_new = ADAM_B1 * m_ref[...] + (1.0 - ADAM_B1) * gv
        v_new = ADAM_B2 * v_ref[...] + (1.0 - ADAM_B2) * (gv * gv)
        m_hat = m_new / (1.0 - ADAM_B1 ** ADAM_STEP)
        v_hat = v_new / (1.0 - ADAM_B2 ** ADAM_STEP)
        d_ref[...] = -ADAM_LR * (m_hat / (jnp.sqrt(v_hat) + ADAM_EPS) + ADAM_WD * w_ref[...])
        mo_ref[...] = m_new
        vo_ref[...] = v_new

    spec = pl.BlockSpec((tm, cols), lambda i: (first // tm + i, 0))
    args = [two_d(w), two_d(g), two_d(m), two_d(v)]
    in_specs = [spec] * 4
    aliases = None
    if dest is not None:
        args += [two_d(d) for d in dest]
        in_specs = in_specs + [ANY] * 3
        aliases = {4: 0, 5: 1, 6: 2}
    outs = _pcall(body, name=name, out_shape=[_sds((rows, cols), F32)] * 3, grid=(count // tm,), in_specs=in_specs,
                  out_specs=[spec] * 3, aliases=aliases, semantics=("parallel",))(*args)
    return [o.reshape(shape) for o in outs]


WEIGHTS = ("ln_mix_a", "w_in_a", "g_v_a", "w_spatial", "b_spatial", "w_out_a", "ln_kv", "w_kv", "g_k", "ln_mix_b",
           "w_q", "g_q", "w_out_b", "ln_mlp", "w_up", "w_down", "ln_ple", "w_ple_gate", "w_ple_proj")
MATRICES = (("w_in_a", 1, True), ("w_out_a", 1, False), ("w_kv", 0, True), ("w_q", 1, False), ("w_out_b", 1, False),
            ("w_up", 2, True), ("w_down", 2, False), ("w_ple_gate", 2, False), ("w_ple_proj", 2, True))
GATHER_STAGES = ((("w_in_a", 0),), (("w_out_a", 0),), (("w_up", 0),), (("w_down", 0),),
                 (("w_ple_gate", 0), ("w_ple_proj", 0), ("w_kv", 0)), (("w_q", 0),),
                 (("w_out_b", 0), ("w_up", 1), ("w_down", 1), ("w_ple_gate", 1), ("w_ple_proj", 1)))
REPLICATED = ("w_spatial", "b_spatial", "ln_kv", "g_k", "ln_mix_b", "g_q", "ln_mlp", "ln_ple")
SHARDED_VECTORS = ("ln_mix_a", "g_v_a")
SMALL_ROWS = 18


def kernel(x, p, ln_mix_a, w_in_a, g_v_a, w_spatial, b_spatial, w_out_a, ln_kv, w_kv, g_k, ln_mix_b, w_q, g_q, w_out_b, ln_mlp, w_up, w_down, ln_ple, w_ple_gate, w_ple_proj, loss_target, m_ln_mix_a, m_w_in_a, m_g_v_a, m_w_spatial, m_b_spatial, m_w_out_a, m_ln_kv, m_w_kv, m_g_k, m_ln_mix_b, m_w_q, m_g_q, m_w_out_b, m_ln_mlp, m_w_up, m_w_down, m_ln_ple, m_w_ple_gate, m_w_ple_proj, v_ln_mix_a, v_w_in_a, v_g_v_a, v_w_spatial, v_b_spatial, v_w_out_a, v_ln_kv, v_w_kv, v_g_k, v_ln_mix_b, v_w_q, v_g_q, v_w_out_b, v_ln_mlp, v_w_up, v_w_down, v_ln_ple, v_w_ple_gate, v_w_ple_proj):
    given = dict(locals())
    weights = {n: given[n] for n in WEIGHTS}
    shard = 2 * lax.axis_index("x") + lax.axis_index("y")
    core = lax.axis_index("c")
    shard_1 = shard.astype(jnp.int32).reshape(1)
    core_1 = core.astype(jnp.int32).reshape(1)
    place = jnp.stack([shard, core]).astype(jnp.int32)

    col_sharded = {name: cols for name, _, cols in MATRICES}
    layer_count = {name: max(layers, 1) for name, layers, _ in MATRICES}

    def cast(key, after):
        name, layer = key
        w3 = weights[name] if weights[name].ndim == 3 else weights[name][None]
        return (name, layer, col_sharded[name],
                cast_into_slot(w3, layer, shard_1, name=f"cast_{name}_{layer}", after=after))

    head = [cast(key, None) for key in GATHER_STAGES[0]]
    send_h, recv_h, flying_h, token_h = gather_start([lf[3] for lf in head], shard_1, name="gather_start_0")
    tail = [cast(key, token_h) for stage in GATHER_STAGES[1:] for key in stage]
    _, vec_a = gather_shards([], [ln_mix_a, g_v_a], name="gather_vectors")
    send_a, recv_a, flying, token = gather_start([lf[3] for lf in tail], vec_a[0], name="gather_start_1")

    w = {"ln_mix_a": vec_a[0].reshape(1, D_MODEL) + token[0, 0],
         "g_v_a": vec_a[1].reshape(1, D_MODEL)}
    for name in REPLICATED:
        w[name] = weights[name]

    class Late:
        def weights(self, name, layer, after):
            stage = [(name, layer) in s for s in GATHER_STAGES].index(True)
            if stage == 0:
                base, members, sems, fly = 0, head, (send_h, recv_h), flying_h
            else:
                base = sum(len(s) for s in GATHER_STAGES[1:stage])
                members, sems, fly = tail[base:base + len(GATHER_STAGES[stage])], (send_a, recv_a), flying
            bufs, send_b, recv_b, tok = gather_pass_on(fly[base:base + len(members)], sems[0], sems[1], after,
                                                       name=f"gather_pass_on_{stage}", base=base)
            got = gather_finish(bufs, send_b, recv_b, tok, [lf[3].shape for lf in members],
                                name=f"gather_finish_{stage}")
            out = {}
            for (leaf_name, leaf_layer, cols, _), arr in zip(members, got):
                out[(leaf_name, leaf_layer)] = arr if cols else arr.reshape(N_SHARDS * arr.shape[1], arr.shape[2])
            return out

        groups = []

        def pair_start(self, grads_done, after):
            self.keys = sorted(grads_done)
            views = [view(k, grads_done[k]) for k in self.keys]
            self.pair, token = exchange_start(views, [(N_SHARDS,) + v.shape[2:] for v in views], F32, pair_plan,
                                              len(views), after, name=f"grad_pair_start_{len(self.groups)}")
            return token

        def chip_start(self, after):
            tag = len(self.groups)
            mine, theirs = exchange_finish(self.pair, after, name=f"grad_pair_finish_{tag}")
            wire = [add_to_wire(a, b, core_1, name=f"grad_pair_sum_{tag}_{i}")
                    for i, (a, b) in enumerate(zip(mine, theirs))]
            chip, token = exchange_start(wire, [(3,) + v.shape[1:] for v in wire], BF16, chip_plan, 3 * len(wire),
                                         wire[-1], name=f"grad_chip_start_{tag}")
            self.groups.append((self.keys, chip))
            return token

    def view(key, arr):
        rows = arr.shape[-2] if col_sharded[key[0]] else arr.shape[0] // N_SHARDS
        return arr.reshape(N_SHARDS, 2, rows // 2, arr.shape[-1])

    t = x.shape[1]
    late = Late()
    loss_blk, dx, g = local_step(x[0], p.reshape(2, t, PLE_DIM), loss_target[0], w, late)
    loss = lax.psum(loss_blk[0, 0], ("x", "y", "c"))

    sent = {k for keys, _ in late.groups for k in keys}
    keys_last = [(name, layer) for name, layers, _ in MATRICES for layer in range(max(layers, 1))
                 if (name, layer) not in sent]
    views = [view(k, g[k[0]][k[1]] if layer_count[k[0]] == 2 else g[k[0]]) for k in keys_last]

    theirs = pair_exchange(views, name="grad_pair_exchange_last")
    wire_0 = [add_to_wire(a, b, core_1, name=f"grad_pair_sum_last_{i}") for i, (a, b) in enumerate(zip(views, theirs))]
    chip_0, token_0 = exchange_start(wire_0, [(3,) + v.shape[1:] for v in wire_0], BF16, chip_plan, 3 * len(wire_0),
                                     wire_0[-1], name="grad_chip_start_last")

    grads, bufs = {}, {}

    def sum_and_share(keys, wire, landed, tag):
        for i, (key, wv, lv) in enumerate(zip(keys, wire, landed)):
            name, layer = key
            bufs[name] = sum_chips(wv, lv, place, bufs.get(name), layer, layer_count[name],
                                   name=f"grad_chip_sum_{tag}_{i}")
        names = sorted({k[0] for k in keys})
        shared = pair_share([bufs[n] for n in names], [(names.index(k[0]), k[1]) for k in keys],
                            name=f"grad_pair_share_{tag}")
        bufs.update(zip(names, shared))

    updates = {}

    def update(n, gn, part=None):
        wn, mn, vn = weights[n], given["m_" + n], given["v_" + n]
        if wn.ndim == 1:
            wn, gn, mn, vn = (a.reshape(1, -1) for a in (wn, gn, mn, vn))
        tag = "" if part is None else f"_{part[0]}"
        updates[n] = adamw(wn, gn.reshape(wn.shape), mn, vn, name=f"adamw_{n}{tag}", part=part, dest=updates.get(n))

    after = token_0
    for tag, (keys, chip) in enumerate(late.groups + [(keys_last, chip_0)]):
        wire, landed = exchange_finish(chip, after, name=f"grad_chip_finish_{tag}")
        sum_and_share(keys, wire, landed, tag)
        for name, layer in keys:
            update(name, bufs[name], (layer, layer_count[name]) if layer_count[name] == 2 else None)
        after = updates[keys[-1][0]][0]

    small = REPLICATED + SHARDED_VECTORS
    flat = jnp.concatenate([g[n].reshape(-1) for n in small])
    room = 8 * SMALL_ROWS * D_MODEL
    flat = jnp.concatenate([flat, jnp.zeros((room - flat.shape[0],), F32)])
    flat, _ = lax.optimization_barrier((flat, after))
    reduced = all_reduce_small(flat.reshape(8, SMALL_ROWS, D_MODEL), name="grad_small_all_reduce").reshape(-1)
    at = 0
    for n in small:
        size = g[n].size
        piece = reduced[at:at + size]
        at += size
        if n in SHARDED_VECTORS:
            per = D_MODEL // N_SHARDS
            grads[n] = lax.dynamic_slice(piece, (shard * per,), (per,)).reshape(weights[n].shape)
        else:
            grads[n] = piece.reshape(weights[n].shape)
        update(n, grads[n])
    for name, _, _ in MATRICES:
        grads[name] = bufs[name].reshape(weights[name].shape)
    delta = {n: updates[n][0].reshape(weights[n].shape) for n in WEIGHTS}
    new_m = {n: updates[n][1].reshape(weights[n].shape) for n in WEIGHTS}
    new_v = {n: updates[n][2].reshape(weights[n].shape) for n in WEIGHTS}
    return (loss, dx.reshape(x.shape), *[grads[n] for n in WEIGHTS], *[delta[n] for n in WEIGHTS],
            *[new_m[n] for n in WEIGHTS], *[new_v[n] for n in WEIGHTS])
```

```python
import jax
import jax.numpy as jnp
from jax import lax
from jax.experimental import pallas as pl
from jax.experimental.pallas import tpu as pltpu

F32 = jnp.float32
BF16 = jnp.bfloat16

D_MODEL = 1024
D_FF = 4096
PLE_DIM = 256
N_GROUPS = 8
CHUNK = 128
HEAD_DIM = 64
LANES = 128
ATT_BLOCK = 256
EPS = 1e-6
N_SHARDS = 4
VMEM_LIMIT = 56 * 1024 * 1024

ADAM_LR = 0.001
ADAM_B1 = 0.9
ADAM_B2 = 0.999
ADAM_EPS = 1e-08
ADAM_WD = 0.01
ADAM_STEP = 10

MESH = pl.DeviceIdType.MESH


def _pcall(body, *, name, out_shape, grid=None, in_specs=None, out_specs=None, scratch_shapes=(),
           semantics=None, aliases=None, side_effects=False, num_prefetch=0):
    params = dict(vmem_limit_bytes=VMEM_LIMIT)
    if semantics is not None:
        params["dimension_semantics"] = semantics
    if side_effects:
        params["has_side_effects"] = True
    kwargs = {}
    if aliases:
        kwargs["input_output_aliases"] = aliases
    if num_prefetch:
        spec = pltpu.PrefetchScalarGridSpec(num_scalar_prefetch=num_prefetch, grid=grid, in_specs=in_specs,
                                            out_specs=out_specs, scratch_shapes=list(scratch_shapes))
        return pl.pallas_call(body, name=name, out_shape=out_shape, grid_spec=spec,
                              compiler_params=pltpu.CompilerParams(**params), **kwargs)
    if grid is not None:
        kwargs["grid"] = grid
    if in_specs is not None:
        kwargs["in_specs"] = in_specs
    if out_specs is not None:
        kwargs["out_specs"] = out_specs
    if aliases:
        kwargs["input_output_aliases"] = aliases
    return pl.pallas_call(body, name=name, out_shape=out_shape, scratch_shapes=list(scratch_shapes),
                          compiler_params=pltpu.CompilerParams(**params), **kwargs)


def _sds(shape, dtype):
    return jax.ShapeDtypeStruct(shape, dtype)


_GELU_C = 0.7978845608028654
_GELU_A = 0.044715


def _gelu(x):
    inner = _GELU_C * (x + _GELU_A * (x * x * x))
    return 0.5 * x * (1.0 + jnp.tanh(inner))


def _gelu_grad(x):
    x2 = x * x
    t = jnp.tanh(_GELU_C * (x + _GELU_A * (x2 * x)))
    return 0.5 * (1.0 + t) + 0.5 * x * (1.0 - t * t) * (_GELU_C * (1.0 + 3.0 * _GELU_A * x2))


def _sigmoid(x):
    return 1.0 / (1.0 + jnp.exp(-x))


def _log_sigmoid(z):
    return jnp.minimum(z, 0.0) - jnp.log(1.0 + jnp.exp(-jnp.abs(z)))


def _dot(a, b):
    return jnp.dot(a, b, preferred_element_type=F32)


def _dot_nt(a, b):
    return lax.dot_general(a, b, (((1,), (1,)), ((), ())), preferred_element_type=F32)


def _dot_tn(a, b):
    return lax.dot_general(a, b, (((0,), (0,)), ((), ())), preferred_element_type=F32)


def _head_rstd(x):
    lane = lax.broadcasted_iota(jnp.int32, x.shape, 1)
    low = lane < HEAD_DIM
    sq = x * x
    s_lo = jnp.sum(jnp.where(low, sq, 0.0), axis=-1, keepdims=True)
    s_hi = jnp.sum(jnp.where(low, 0.0, sq), axis=-1, keepdims=True)
    ms = jnp.where(low, s_lo, s_hi) * (1.0 / HEAD_DIM)
    return lax.rsqrt(ms + EPS)


def _head_mean(x):
    lane = lax.broadcasted_iota(jnp.int32, x.shape, 1)
    low = lane < HEAD_DIM
    s_lo = jnp.sum(jnp.where(low, x, 0.0), axis=-1, keepdims=True)
    s_hi = jnp.sum(jnp.where(low, 0.0, x), axis=-1, keepdims=True)
    return jnp.where(low, s_lo, s_hi) * (1.0 / HEAD_DIM)


def _full(shape):
    zeros = (0,) * len(shape)
    return pl.BlockSpec(shape, lambda i: zeros)


def norm_matmul(x, g, w, *, name, epilogue="none", tm=512):
    t, d = x.shape
    sharded = w.ndim == 3
    per = w.shape[2] if sharded else w.shape[1]
    n = N_SHARDS * per if sharded else per
    tm = min(tm, t)

    def body(x_ref, g_ref, w_ref, h_ref, r_ref, *outs):
        xv = x_ref[...]
        r = lax.rsqrt(jnp.mean(xv * xv, axis=-1, keepdims=True) + EPS)
        h = ((xv * r) * g_ref[...]).astype(BF16)
        h_ref[...] = h
        r_ref[...] = r
        for s in range(N_SHARDS if sharded else 1):
            cols = slice(s * per, (s + 1) * per)
            y = _dot(h, w_ref[s] if sharded else w_ref[...])
            if epilogue == "none":
                outs[0][:, cols] = y
            else:
                a = jnp.maximum(y, 0.0)
                outs[0][:, cols] = a.astype(BF16)
                outs[1][:, cols] = (a * a).astype(BF16)

    row = lambda i: (i, 0)
    out_shape = [_sds((t, d), BF16), _sds((t, 1), F32)]
    out_specs = [pl.BlockSpec((tm, d), row), pl.BlockSpec((tm, 1), row)]
    if epilogue == "none":
        out_shape.append(_sds((t, n), F32))
        out_specs.append(pl.BlockSpec((tm, n), row))
    else:
        out_shape += [_sds((t, n), BF16), _sds((t, n), BF16)]
        out_specs += [pl.BlockSpec((tm, n), row)] * 2
    return _pcall(
        body, name=name, out_shape=out_shape, grid=(t // tm,),
        in_specs=[pl.BlockSpec((tm, d), row), _full((1, d)), _full(w.shape)],
        out_specs=out_specs, semantics=("parallel",))(x, g, w)


def matmul_residual(a, w, res, *, name, tm=512):
    t, k = a.shape
    n = w.shape[1]
    tm = min(tm, t)

    def body(a_ref, w_ref, res_ref, o_ref):
        o_ref[...] = res_ref[...] + _dot(a_ref[...], w_ref[...])

    row = lambda i: (i, 0)
    return _pcall(
        body, name=name, out_shape=_sds((t, n), F32), grid=(t // tm,),
        in_specs=[pl.BlockSpec((tm, k), row), _full(w.shape), pl.BlockSpec((tm, n), row)],
        out_specs=pl.BlockSpec((tm, n), row), semantics=("parallel",))(a, w, res)


def ple_forward(x, g, w_gate, p, w_proj, *, name, tm=256):
    t, d = x.shape
    tm = min(tm, t)

    def body(x_ref, g_ref, wg_ref, p_ref, wp_ref, h_ref, r_ref, gate_ref, pp_ref, o_ref):
        xv = x_ref[...]
        r = lax.rsqrt(jnp.mean(xv * xv, axis=-1, keepdims=True) + EPS)
        h = ((xv * r) * g_ref[...]).astype(BF16)
        h_ref[...] = h
        r_ref[...] = r
        gate = _sigmoid(_dot(h, wg_ref[...]))
        gate_ref[...] = gate
        pb = p_ref[...].astype(BF16)
        per = d // N_SHARDS
        for s in range(N_SHARDS):
            cols = slice(s * per, (s + 1) * per)
            pp = _dot(pb, wp_ref[s])
            pp_ref[:, cols] = pp.astype(BF16)
            o_ref[:, cols] = xv[:, cols] + pp * gate[:, cols]

    row = lambda i: (i, 0)
    fixed = lambda i: (0, 0)
    return _pcall(
        body, name=name,
        out_shape=[_sds((t, d), BF16), _sds((t, 1), F32), _sds((t, d), F32), _sds((t, d), BF16), _sds((t, d), F32)],
        grid=(t // tm,),
        in_specs=[pl.BlockSpec((tm, d), row), pl.BlockSpec((1, d), fixed), pl.BlockSpec((d, d), fixed),
                  pl.BlockSpec((tm, PLE_DIM), row),
                  pl.BlockSpec((N_SHARDS, PLE_DIM, d // N_SHARDS), lambda i: (0, 0, 0))],
        out_specs=[pl.BlockSpec((tm, d), row), pl.BlockSpec((tm, 1), row), pl.BlockSpec((tm, d), row),
                   pl.BlockSpec((tm, d), row), pl.BlockSpec((tm, d), row)],
        semantics=("parallel",))(x, g, w_gate, p, w_proj)


def _tril_mask():
    r = lax.broadcasted_iota(jnp.int32, (CHUNK, CHUNK), 0)
    c = lax.broadcasted_iota(jnp.int32, (CHUNK, CHUNK), 1)
    return c <= r


def _sgu_common(pre_ref, gv_ref, ws_ref):
    pre = pre_ref[...]
    pre_u, pre_v = pre[:, :D_MODEL], pre[:, D_MODEL:]
    u = _gelu(pre_u)
    v = _gelu(pre_v)
    r = lax.rsqrt(jnp.mean(v * v, axis=-1, keepdims=True) + EPS)
    vhat = v * r
    vn = (vhat * gv_ref[...]).astype(BF16)
    tril = _tril_mask()
    wm = [jnp.where(tril, ws_ref[g], 0.0).astype(BF16) for g in range(N_GROUPS)]
    return pre_u, pre_v, u, r, vhat, vn, wm, tril


def sgu_forward(pre, g_v, w_s, b_full, *, name):
    t = pre.shape[0]

    def body(pre_ref, gv_ref, ws_ref, b_ref, y_ref):
        _, _, u, _, _, vn, wm, _ = _sgu_common(pre_ref, gv_ref, ws_ref)
        for g in range(N_GROUPS):
            cols = slice(g * LANES, (g + 1) * LANES)
            mix = _dot(wm[g], vn[:, cols]) + b_ref[:, cols]
            y_ref[:, cols] = (u[:, cols] * mix).astype(BF16)

    return _pcall(
        body, name=name, out_shape=_sds((t, D_MODEL), BF16), grid=(t // CHUNK,),
        in_specs=[pl.BlockSpec((CHUNK, 2 * D_MODEL), lambda i: (i, 0)), pl.BlockSpec((1, D_MODEL), lambda i: (0, 0)),
                  pl.BlockSpec((N_GROUPS, CHUNK, CHUNK), lambda i: (0, 0, 0)),
                  pl.BlockSpec((CHUNK, D_MODEL), lambda i: (0, 0))],
        out_specs=pl.BlockSpec((CHUNK, D_MODEL), lambda i: (i, 0)),
        semantics=("parallel",))(pre, g_v, w_s, b_full)


def head_norm(pre, g128, *, name, col_block=0, scale=1.0, passthrough=False, tm=512):
    t = pre.shape[0]
    tm = min(tm, t)

    def body(*refs):
        if passthrough:
            x_ref, v_ref, g_ref, o_ref, vo_ref = refs
            vo_ref[...] = v_ref[...].astype(BF16)
        else:
            x_ref, g_ref, o_ref = refs
        g = g_ref[...] * scale
        for b in range(D_MODEL // LANES):
            cols = slice(b * LANES, (b + 1) * LANES)
            xv = x_ref[:, cols]
            o_ref[:, cols] = ((xv * _head_rstd(xv)) * g).astype(BF16)

    x_spec = pl.BlockSpec((tm, D_MODEL), lambda i: (i, col_block))
    g_spec = pl.BlockSpec((1, LANES), lambda i: (0, 0))
    o_spec = pl.BlockSpec((tm, D_MODEL), lambda i: (i, 0))
    if passthrough:
        return _pcall(body, name=name, out_shape=[_sds((t, D_MODEL), BF16)] * 2, grid=(t // tm,),
                      in_specs=[x_spec, pl.BlockSpec((tm, D_MODEL), lambda i: (i, 1)), g_spec],
                      out_specs=[o_spec, o_spec], semantics=("parallel",))(pre, pre, g128)
    return _pcall(body, name=name, out_shape=_sds((t, D_MODEL), BF16), grid=(t // tm,),
                  in_specs=[x_spec, g_spec], out_specs=o_spec, semantics=("parallel",))(pre, g128)


def _suffix_matrix(n):
    r = lax.broadcasted_iota(jnp.int32, (n, n), 0)
    c = lax.broadcasted_iota(jnp.int32, (n, n), 1)
    return jnp.where(r > c, 1.0, 0.0).astype(BF16)


def _prefix_matrix(n):
    r = lax.broadcasted_iota(jnp.int32, (n, n), 0)
    c = lax.broadcasted_iota(jnp.int32, (n, n), 1)
    return jnp.where(r < c, 1.0, 0.0).astype(BF16)


def _block_cumsum(a, tri):
    return _dot(a.astype(BF16), tri)


def _stacked_causal(n):
    r = lax.broadcasted_iota(jnp.int32, (2 * n, n), 0)
    c = lax.broadcasted_iota(jnp.int32, (2 * n, n), 1)
    return c < jnp.where(r >= n, r - n, r)


def _stack_heads(a, low):
    zero = jnp.zeros_like(a)
    return jnp.concatenate([jnp.where(low, a, zero), jnp.where(low, zero, a)], axis=0)


def stick_breaking_forward(q, k, v, *, name):
    t = q.shape[0]
    blk = min(ATT_BLOCK, t)
    nq = t // blk

    def body(q_ref, k_ref, v_ref, o_ref):
        i = pl.program_id(1)
        low = lax.broadcasted_iota(jnp.int32, (blk, LANES), 1) < HEAD_DIM
        tri = _suffix_matrix(blk)
        causal = _stacked_causal(blk)
        qs = _stack_heads(q_ref[...], low)

        def block(j, carry, acc, masked):
            rows = pl.ds(pl.multiple_of(j * blk, blk), blk)
            z = _dot_nt(qs, k_ref[rows, :])
            ls = _log_sigmoid(z)
            lg = ls - z
            if masked:
                lg = jnp.where(causal, lg, 0.0)
            s = ls + _block_cumsum(lg, tri) + carry
            a = jnp.exp(s)
            if masked:
                a = jnp.where(causal, a, 0.0)
            acc = acc + _dot(a.astype(BF16), v_ref[rows, :])
            return carry + jnp.sum(lg, axis=-1, keepdims=True), acc

        state = block(i, jnp.zeros((2 * blk, 1), F32), jnp.zeros((2 * blk, LANES), F32), True)

        def two_blocks(n, st):
            st = block(i - 1 - 2 * n, st[0], st[1], False)
            return block(i - 2 - 2 * n, st[0], st[1], False)

        state = lax.fori_loop(0, i // 2, two_blocks, state)
        _, acc = lax.fori_loop(0, i % 2, lambda n, st: block(0, st[0], st[1], False), state)
        o_ref[...] = jnp.where(low, acc[:blk], acc[blk:]).astype(BF16)

    return _pcall(
        body, name=name, out_shape=_sds((t, D_MODEL), BF16), grid=(D_MODEL // LANES, nq),
        in_specs=[pl.BlockSpec((blk, LANES), lambda p, i: (i, p)), pl.BlockSpec((t, LANES), lambda p, i: (0, p)),
                  pl.BlockSpec((t, LANES), lambda p, i: (0, p))],
        out_specs=pl.BlockSpec((blk, LANES), lambda p, i: (i, p)),
        semantics=("parallel", "arbitrary"))(q, k, v)


def loss_forward(x, target, *, name, tm=512):
    t, d = x.shape
    tm = min(tm, t)

    def body(x_ref, t_ref, l_ref, dx_ref):
        @pl.when(pl.program_id(0) == 0)
        def _():
            l_ref[...] = jnp.zeros_like(l_ref)

        diff = x_ref[...] - t_ref[...]
        dx_ref[...] = diff * (1.0 / d)
        l_ref[...] += 0.5 * jnp.sum(jnp.mean(diff * diff, axis=-1, keepdims=True))

    return _pcall(
        body, name=name, out_shape=[_sds((8, LANES), F32), _sds((t, d), F32)], grid=(t // tm,),
        in_specs=[pl.BlockSpec((tm, d), lambda i: (i, 0))] * 2,
        out_specs=[pl.BlockSpec((8, LANES), lambda i: (0, 0)), pl.BlockSpec((tm, d), lambda i: (i, 0))],
        semantics=("arbitrary",))(x, target)


def matmul_nt(dy, w, *, name, mul=None, out_dtype=F32, tm=512):
    t, n = dy.shape
    k = w.shape[0]
    tm = min(tm, t)

    def body(*refs):
        if mul is None:
            dy_ref, w_ref, o_ref = refs
        else:
            dy_ref, w_ref, m_ref, o_ref = refs
        y = _dot_nt(dy_ref[...].astype(BF16), w_ref[...])
        if mul is not None:
            y = y * (2.0 * m_ref[...].astype(F32))
        o_ref[...] = y.astype(out_dtype)

    row = lambda i: (i, 0)
    in_specs = [pl.BlockSpec((tm, n), row), _full(w.shape)]
    args = [dy, w]
    if mul is not None:
        in_specs.append(pl.BlockSpec((tm, k), row))
        args.append(mul)
    return _pcall(body, name=name, out_shape=_sds((t, k), out_dtype), grid=(t // tm,), in_specs=in_specs,
                  out_specs=pl.BlockSpec((tm, k), row), semantics=("parallel",))(*args)


def matmul_tn(a, dy, *, name, col_shards, tk=512):
    t, k = a.shape
    n = dy.shape[1]
    if col_shards:
        tn = n // N_SHARDS

        def body(a_ref, dy_ref, o_ref):
            o_ref[...] = _dot_tn(a_ref[...].astype(BF16), dy_ref[...].astype(BF16))

        return _pcall(body, name=name, out_shape=_sds((N_SHARDS, k, tn), F32), grid=(N_SHARDS,),
                      in_specs=[_full((t, k)), pl.BlockSpec((t, tn), lambda j: (0, j))],
                      out_specs=pl.BlockSpec((None, k, tn), lambda j: (j, 0, 0)), semantics=("parallel",))(a, dy)

    tk = min(tk, k)

    def body(a_ref, dy_ref, o_ref, dy_bf):
        @pl.when(pl.program_id(0) == 0)
        def _():
            dy_bf[...] = dy_ref[...].astype(BF16)

        o_ref[...] = _dot_tn(a_ref[...].astype(BF16), dy_bf[...])

    return _pcall(body, name=name, out_shape=_sds((k, n), F32), grid=(k // tk,),
                  in_specs=[pl.BlockSpec((t, tk), lambda i: (0, i)), _full((t, n))],
                  out_specs=pl.BlockSpec((tk, n), lambda i: (i, 0)),
                  scratch_shapes=[pltpu.VMEM((t, n), BF16)], semantics=("arbitrary",))(a, dy)


def norm_backward(dpre, w, x, g, rstd, dx_out, *, name, tm=512):
    t, d = x.shape
    n = dpre.shape[1]
    tm = min(tm, t)
    if w.ndim == 3:
        w_spec = pl.BlockSpec(w.shape, lambda i: (0, 0, 0))
    else:
        w_spec = pl.BlockSpec(w.shape, lambda i: (0, 0))

    def body(dp_ref, w_ref, x_ref, g_ref, r_ref, dxo_ref, dx_ref, dg_ref):
        @pl.when(pl.program_id(0) == 0)
        def _():
            dg_ref[...] = jnp.zeros_like(dg_ref)

        if w.ndim == 3:
            per = n // N_SHARDS
            dh = _dot_nt(dp_ref[:, 0:per], w_ref[0])
            for s in range(1, N_SHARDS):
                dh = dh + _dot_nt(dp_ref[:, s * per:(s + 1) * per], w_ref[s])
        else:
            dh = _dot_nt(dp_ref[...], w_ref[...])
        r = r_ref[...]
        xn = x_ref[...] * r
        dg_ref[...] += jnp.sum(dh * xn, axis=0, keepdims=True)
        dxn = dh * g_ref[...]
        dx = r * (dxn - xn * jnp.mean(dxn * xn, axis=-1, keepdims=True))
        dx_ref[...] = dxo_ref[...] + dx

    row = lambda i: (i, 0)
    fixed = lambda i: (0, 0)
    return _pcall(
        body, name=name, out_shape=[_sds((t, d), F32), _sds((1, d), F32)], grid=(t // tm,),
        in_specs=[pl.BlockSpec((tm, n), row), w_spec, pl.BlockSpec((tm, d), row),
                  pl.BlockSpec((1, d), fixed), pl.BlockSpec((tm, 1), row), pl.BlockSpec((tm, d), row)],
        out_specs=[pl.BlockSpec((tm, d), row), pl.BlockSpec((1, d), fixed)],
        semantics=("arbitrary",))(dpre, w, x, g, rstd, dx_out)


def ple_backward(dx, gate, pp, *, name, tm=512):
    t, d = dx.shape
    tm = min(tm, t)

    def body(dx_ref, gate_ref, pp_ref, dg_ref, dp_ref):
        dxv = dx_ref[...]
        gate = gate_ref[...]
        dg_ref[...] = (dxv * pp_ref[...].astype(F32) * (gate * (1.0 - gate))).astype(BF16)
        dp_ref[...] = (dxv * gate).astype(BF16)

    spec = pl.BlockSpec((tm, d), lambda i: (i, 0))
    return _pcall(body, name=name, out_shape=[_sds((t, d), BF16)] * 2, grid=(t // tm,), in_specs=[spec] * 3,
                  out_specs=[spec] * 2, semantics=("parallel",))(dx, gate, pp)


def sgu_backward(dy, pre, g_v, w_s, b_full, *, name):
    t = pre.shape[0]
    n_chunks = t // CHUNK

    def body(dy_ref, pre_ref, gv_ref, ws_ref, b_ref, dpre_ref, dws_ref, db_ref, dgv_ref, dvn_s, dbf_s):
        step = pl.program_id(0)

        @pl.when(step == 0)
        def _():
            dws_ref[...] = jnp.zeros_like(dws_ref)
            dgv_ref[...] = jnp.zeros_like(dgv_ref)
            dbf_s[...] = jnp.zeros_like(dbf_s)

        pre_u, pre_v, u, r, vhat, vn, wm, tril = _sgu_common(pre_ref, gv_ref, ws_ref)
        dyv = dy_ref[...]
        for g in range(N_GROUPS):
            cols = slice(g * LANES, (g + 1) * LANES)
            mix = _dot(wm[g], vn[:, cols]) + b_ref[:, cols]
            dmix = dyv[:, cols] * u[:, cols]
            dmix_b = dmix.astype(BF16)
            du = dyv[:, cols] * mix
            dpre_ref[:, cols] = (du * _gelu_grad(pre_u[:, cols])).astype(BF16)
            dws_ref[g] += jnp.where(tril, _dot_nt(dmix_b, vn[:, cols]), 0.0)
            dbf_s[:, cols] += dmix
            dvn_s[:, cols] = _dot_tn(wm[g], dmix_b)
        dvn = dvn_s[...]
        dgv_ref[...] += jnp.sum(dvn * vhat, axis=0, keepdims=True)
        dxn = dvn * gv_ref[...]
        dv = r * (dxn - vhat * jnp.mean(dxn * vhat, axis=-1, keepdims=True))
        dpre_ref[:, D_MODEL:] = (dv * _gelu_grad(pre_v)).astype(BF16)

        @pl.when(step == n_chunks - 1)
        def _():
            lane = lax.broadcasted_iota(jnp.int32, (CHUNK, LANES), 1)
            acc = jnp.zeros((CHUNK, LANES), F32)
            for g in range(N_GROUPS):
                s = jnp.sum(dbf_s[:, g * LANES:(g + 1) * LANES], axis=-1, keepdims=True)
                acc = jnp.where(lane == g, s, acc)
            db_ref[...] = acc

    fixed2 = lambda i: (0, 0)
    return _pcall(
        body, name=name,
        out_shape=[_sds((t, 2 * D_MODEL), BF16), _sds((N_GROUPS, CHUNK, CHUNK), F32), _sds((CHUNK, LANES), F32),
                   _sds((1, D_MODEL), F32)],
        grid=(n_chunks,),
        in_specs=[pl.BlockSpec((CHUNK, D_MODEL), lambda i: (i, 0)), pl.BlockSpec((CHUNK, 2 * D_MODEL), lambda i: (i, 0)),
                  pl.BlockSpec((1, D_MODEL), fixed2), pl.BlockSpec((N_GROUPS, CHUNK, CHUNK), lambda i: (0, 0, 0)),
                  pl.BlockSpec((CHUNK, D_MODEL), fixed2)],
        out_specs=[pl.BlockSpec((CHUNK, 2 * D_MODEL), lambda i: (i, 0)),
                   pl.BlockSpec((N_GROUPS, CHUNK, CHUNK), lambda i: (0, 0, 0)), pl.BlockSpec((CHUNK, LANES), fixed2),
                   pl.BlockSpec((1, D_MODEL), fixed2)],
        scratch_shapes=[pltpu.VMEM((CHUNK, D_MODEL), F32), pltpu.VMEM((CHUNK, D_MODEL), F32)],
        semantics=("arbitrary",))(dy, pre, g_v, w_s, b_full)


def head_norm_backward(dy, pre, g128, *, name, col_block=0, scale=1.0, passthrough=None, tm=512):
    t = dy.shape[0]
    tm = min(tm, t)
    width = 2 * D_MODEL if passthrough is not None else D_MODEL

    def body(*refs):
        if passthrough is not None:
            dy_ref, x_ref, g_ref, dv_ref, o_ref, dg_ref = refs
            o_ref[:, D_MODEL:] = dv_ref[...].astype(BF16)
        else:
            dy_ref, x_ref, g_ref, o_ref, dg_ref = refs

        @pl.when(pl.program_id(0) == 0)
        def _():
            dg_ref[...] = jnp.zeros_like(dg_ref)

        g = g_ref[...]
        dg = jnp.zeros((1, LANES), F32)
        for b in range(D_MODEL // LANES):
            cols = slice(b * LANES, (b + 1) * LANES)
            xv = x_ref[:, cols]
            r = _head_rstd(xv)
            xn = xv * r
            dyv = dy_ref[:, cols] * scale
            dg = dg + jnp.sum(dyv * xn, axis=0, keepdims=True)
            dxn = dyv * g
            o_ref[:, cols] = (r * (dxn - xn * _head_mean(dxn * xn))).astype(BF16)
        dg_ref[...] += dg

    row = lambda i: (i, 0)
    in_specs = [pl.BlockSpec((tm, D_MODEL), row), pl.BlockSpec((tm, D_MODEL), lambda i: (i, col_block)),
                pl.BlockSpec((1, LANES), lambda i: (0, 0))]
    args = [dy, pre, g128]
    if passthrough is not None:
        in_specs.append(pl.BlockSpec((tm, D_MODEL), row))
        args.append(passthrough)
    return _pcall(body, name=name, out_shape=[_sds((t, width), BF16), _sds((1, LANES), F32)], grid=(t // tm,),
                  in_specs=in_specs,
                  out_specs=[pl.BlockSpec((tm, width), row), pl.BlockSpec((1, LANES), lambda i: (0, 0))],
                  semantics=("arbitrary",))(*args)


def stick_breaking_backward(q, k, v, do, *, name):
    t = q.shape[0]
    blk = min(ATT_BLOCK, t)
    nq = t // blk

    def body(q_ref, k_ref, v_ref, do_ref, dq_ref, dk_ref, dv_ref, s_buf, sg_buf):
        i = pl.program_id(1)

        @pl.when(i == 0)
        def _():
            dk_ref[...] = jnp.zeros_like(dk_ref)
            dv_ref[...] = jnp.zeros_like(dv_ref)

        low = lax.broadcasted_iota(jnp.int32, (blk, LANES), 1) < HEAD_DIM
        suffix = _suffix_matrix(blk)
        prefix = _prefix_matrix(blk)
        causal = _stacked_causal(blk)
        qs = _stack_heads(q_ref[...], low)
        dos = _stack_heads(do_ref[...], low)

        def log_weights(j, carry, masked):
            rows = pl.ds(pl.multiple_of(j * blk, blk), blk)
            z = _dot_nt(qs, k_ref[rows, :])
            ls = _log_sigmoid(z)
            lg = ls - z
            if masked:
                lg = jnp.where(causal, lg, 0.0)
            s_buf[j] = ls + _block_cumsum(lg, suffix) + carry
            sg_buf[j] = jnp.exp(ls)
            return carry + jnp.sum(lg, axis=-1, keepdims=True)

        carry = log_weights(i, jnp.zeros((2 * blk, 1), F32), True)
        carry = lax.fori_loop(0, i // 2, lambda n, c: log_weights(i - 2 - 2 * n, log_weights(i - 1 - 2 * n, c, False),
                                                                  False), carry)
        lax.fori_loop(0, i % 2, lambda n, c: log_weights(0, c, False), carry)

        def grads(j, pcarry, dq_acc, masked):
            rows = pl.ds(pl.multiple_of(j * blk, blk), blk)
            a = jnp.exp(s_buf[j])
            if masked:
                a = jnp.where(causal, a, 0.0)
            sg = sg_buf[j]
            ds = _dot_nt(dos, v_ref[rows, :]) * a
            before = _block_cumsum(ds, prefix) + pcarry
            if masked:
                before = jnp.where(causal, before, 0.0)
            dz = (ds - sg * (ds + before)).astype(BF16)
            dq_acc = dq_acc + _dot(dz, k_ref[rows, :])
            dk_ref[rows, :] += _dot_tn(dz, qs)
            dv_ref[rows, :] += _dot_tn(a.astype(BF16), dos)
            return pcarry + jnp.sum(ds, axis=-1, keepdims=True), dq_acc

        def two_blocks(n, st):
            st = grads(2 * n, st[0], st[1], False)
            return grads(2 * n + 1, st[0], st[1], False)

        state = lax.fori_loop(0, i // 2, two_blocks,
                              (jnp.zeros((2 * blk, 1), F32), jnp.zeros((2 * blk, LANES), F32)))
        state = lax.fori_loop(0, i % 2, lambda n, st: grads(i - 1, st[0], st[1], False), state)
        _, dq_acc = grads(i, state[0], state[1], True)
        dq_ref[...] = jnp.where(low, dq_acc[:blk], dq_acc[blk:])

    full = pl.BlockSpec((t, LANES), lambda p, i: (0, p))
    qblk = pl.BlockSpec((blk, LANES), lambda p, i: (i, p))
    return _pcall(
        body, name=name, out_shape=[_sds((t, D_MODEL), F32)] * 3, grid=(D_MODEL // LANES, nq),
        in_specs=[qblk, full, full, qblk], out_specs=[qblk, full, full],
        scratch_shapes=[pltpu.VMEM((nq, 2 * blk, blk), F32), pltpu.VMEM((nq, 2 * blk, blk), F32)],
        semantics=("parallel", "arbitrary"))(q, k, v, do)


def _mlp_backward(dx, saved, g, w_up, w_down, tag):
    x, h, r, a, a2 = saved
    d_w_down = matmul_tn(a2, dx, name=f"d_w_down_{tag}", col_shards=False)
    dpre = matmul_nt(dx, w_down, name=f"d_mlp_act_{tag}", mul=a, out_dtype=BF16)
    d_w_up = matmul_tn(h, dpre, name=f"d_w_up_{tag}", col_shards=True)
    dx, d_g = norm_backward(dpre, w_up, x, g, r, dx, name=f"d_mlp_norm_{tag}")
    return dx, d_w_up, d_w_down, d_g


def _ple_backward(dx, saved, p, g, w_gate, tag):
    x, h, r, gate, pp = saved
    dgate, dproj = ple_backward(dx, gate, pp, name=f"d_ple_{tag}")
    d_w_proj = matmul_tn(p, dproj, name=f"d_w_ple_proj_{tag}", col_shards=True)
    d_w_gate = matmul_tn(h, dgate, name=f"d_w_ple_gate_{tag}", col_shards=False)
    dx, d_g = norm_backward(dgate, w_gate, x, g, r, dx, name=f"d_ple_norm_{tag}")
    return dx, d_w_gate, d_w_proj, d_g


def local_step(x, p, target, w, late=None):
    row = lambda v: v.reshape(1, -1)
    g128 = lambda v: jnp.tile(v.reshape(1, HEAD_DIM), (1, 2))
    scale = HEAD_DIM ** -0.5
    b_full = jnp.repeat(jnp.transpose(w["b_spatial"][0]), LANES, axis=1)
    w_s = w["w_spatial"][0]

    mats = {}
    for name, value in w.items():
        if isinstance(value, tuple):
            mats.update({(name, layer): v for layer, v in enumerate(value)})
    if "w_kv" in w:
        mats[("w_kv", 0)] = w["w_kv"]

    def fetch(name, layer, after):
        if (name, layer) not in mats:
            mats.update(late.weights(name, layer, after))
        return mats[(name, layer)]

    def mlp_forward(x_in, layer):
        h, r, a, a2 = norm_matmul(x_in, row(w["ln_mlp"][layer]), fetch("w_up", layer, x_in), name=f"mlp_up_{layer}",
                                  epilogue="relu2")
        return matmul_residual(a2, fetch("w_down", layer, a2), x_in, name=f"mlp_down_{layer}"), (x_in, h, r, a, a2)

    def ple(x_in, layer):
        return ple_forward(x_in, row(w["ln_ple"][layer]), fetch("w_ple_gate", layer, x_in), p[layer],
                           fetch("w_ple_proj", layer, x_in), name=f"ple_{layer}")

    x0 = x
    h_a, r_a, pre_a = norm_matmul(x0, row(w["ln_mix_a"][0]), fetch("w_in_a", 0, x0), name="sgu_in")
    y_a = sgu_forward(pre_a, row(w["g_v_a"][0]), w_s, b_full, name="sgu_mix")
    x1 = matmul_residual(y_a, fetch("w_out_a", 0, y_a), x0, name="sgu_out")
    x2, mlp0 = mlp_forward(x1, 0)
    ple0 = ple(x2, 0)
    x3 = ple0[4]
    h_kv, r_kv, kv_pre = norm_matmul(x3, row(w["ln_kv"]), fetch("w_kv", 0, x3), name="kv_proj")
    k_n, v_b = head_norm(kv_pre, g128(w["g_k"]), name="k_norm", passthrough=True)
    h_q, r_q, q_pre = norm_matmul(x3, row(w["ln_mix_b"][0]), fetch("w_q", 0, k_n), name="q_proj")
    q_n = head_norm(q_pre, g128(w["g_q"][0]), name="q_norm", scale=scale)
    o = stick_breaking_forward(q_n, k_n, v_b, name="sb_fwd")
    x4 = matmul_residual(o, fetch("w_out_b", 0, o), x3, name="sb_out")
    x5, mlp1 = mlp_forward(x4, 1)
    ple1 = ple(x5, 1)
    x6 = ple1[4]
    loss_blk, dx = loss_forward(x6, target, name="loss")

    g = {}
    dx, dwg1, dwp1, dlnp1 = _ple_backward(dx, (x5,) + tuple(ple1[:4]), p[1], row(w["ln_ple"][1]),
                                          mats[("w_ple_gate", 1)], 1)
    dx, dwu1, dwd1, dlnm1 = _mlp_backward(dx, mlp1, row(w["ln_mlp"][1]), mats[("w_up", 1)], mats[("w_down", 1)], 1)
    g["w_out_b"] = matmul_tn(o, dx, name="d_w_out_b", col_shards=False)
    do = matmul_nt(dx, mats[("w_out_b", 0)], name="d_sb_out", out_dtype=BF16)
    dq_n, dk_n, dv = stick_breaking_backward(q_n, k_n, v_b, do, name="sb_bwd")
    dq_pre, dgq = head_norm_backward(dq_n, q_pre, g128(w["g_q"][0]), name="d_q_norm", scale=scale)
    dkv_pre, dgk = head_norm_backward(dk_n, kv_pre, g128(w["g_k"]), name="d_k_norm", passthrough=dv)
    g["w_q"] = matmul_tn(h_q, dq_pre, name="d_w_q", col_shards=False)
    g["w_kv"] = matmul_tn(h_kv, dkv_pre, name="d_w_kv", col_shards=True)
    dx, g["ln_mix_b"] = norm_backward(dq_pre, mats[("w_q", 0)], x3, row(w["ln_mix_b"][0]), r_q, dx, name="d_q_in")
    dx, g["ln_kv"] = norm_backward(dkv_pre, mats[("w_kv", 0)], x3, row(w["ln_kv"]), r_kv, dx, name="d_kv_in")
    g["g_q"] = dgq[:, :HEAD_DIM] + dgq[:, HEAD_DIM:]
    g["g_k"] = (dgk[:, :HEAD_DIM] + dgk[:, HEAD_DIM:]).reshape(HEAD_DIM)
    g["ln_kv"] = g["ln_kv"].reshape(D_MODEL)
    ln_ple0, ln_mlp0, g_v0, ln_mix0 = (row(w["ln_ple"][0]), row(w["ln_mlp"][0]), row(w["g_v_a"][0]),
                                       row(w["ln_mix_a"][0]))
    if late is not None:
        ln_ple0 = ln_ple0 + late.pair_start(
            {("w_kv", 0): g["w_kv"], ("w_q", 0): g["w_q"], ("w_out_b", 0): g["w_out_b"], ("w_up", 1): dwu1,
             ("w_down", 1): dwd1, ("w_ple_gate", 1): dwg1, ("w_ple_proj", 1): dwp1}, dx)[0, 0]
    dx, dwg0, dwp0, dlnp0 = _ple_backward(dx, (x2,) + tuple(ple0[:4]), p[0], ln_ple0, mats[("w_ple_gate", 0)], 0)
    if late is not None:
        ln_mlp0 = ln_mlp0 + late.chip_start(dx)[0, 0]
    dx, dwu0, dwd0, dlnm0 = _mlp_backward(dx, mlp0, ln_mlp0, mats[("w_up", 0)], mats[("w_down", 0)], 0)
    if late is not None:
        g_v0 = g_v0 + late.pair_start({("w_up", 0): dwu0, ("w_down", 0): dwd0, ("w_ple_gate", 0): dwg0,
                                       ("w_ple_proj", 0): dwp0}, dx)[0, 0]
    g["w_out_a"] = matmul_tn(y_a, dx, name="d_w_out_a", col_shards=False)
    dy_a = matmul_nt(dx, mats[("w_out_a", 0)], name="d_sgu_out")
    dpre_a, dws, db, g["g_v_a"] = sgu_backward(dy_a, pre_a, g_v0, w_s, b_full, name="d_sgu_mix")
    if late is not None:
        ln_mix0 = ln_mix0 + late.chip_start(dpre_a)[0, 0]
    g["w_in_a"] = matmul_tn(h_a, dpre_a, name="d_w_in_a", col_shards=True)
    dx, g["ln_mix_a"] = norm_backward(dpre_a, mats[("w_in_a", 0)], x0, ln_mix0, r_a, dx, name="d_sgu_in")
    g["w_spatial"] = dws[None]
    g["b_spatial"] = jnp.transpose(db[:, :N_GROUPS])[None]
    g["w_up"] = (dwu0, dwu1)
    g["w_down"] = (dwd0, dwd1)
    g["w_ple_gate"] = (dwg0, dwg1)
    g["w_ple_proj"] = (dwp0, dwp1)
    g["ln_mlp"] = jnp.concatenate([dlnm0, dlnm1], axis=0)
    g["ln_ple"] = jnp.concatenate([dlnp0, dlnp1], axis=0)
    return loss_blk, dx, g


ANY = pl.BlockSpec(memory_space=pl.ANY)


def _place():
    x, y, c = lax.axis_index("x"), lax.axis_index("y"), lax.axis_index("c")
    others = [(1 - x, y), (x, 1 - y), (1 - x, 1 - y)]
    return x, y, c, 2 * x + y, others


def cast_into_slot(w3, layer, slot, *, name, after=None, tm=512):
    _, r, c = w3.shape
    tm = min(tm, r)

    def body(slot_ref, w_ref, *rest):
        rest[-1][...] = w_ref[...].astype(BF16)

    in_specs = [pl.BlockSpec((None, tm, c), lambda i, s: (layer, i, 0))]
    args = [slot, w3]
    if after is not None:
        in_specs.append(ANY)
        args.append(after)
    return _pcall(body, name=name, out_shape=_sds((N_SHARDS, r, c), BF16), grid=(r // tm,), num_prefetch=1,
                  in_specs=in_specs, out_specs=pl.BlockSpec((None, tm, c), lambda i, s: (s[0], i, 0)),
                  semantics=("parallel",))(*args)


def gather_shards(mats, vecs, *, name):
    nm, nv = len(mats), len(vecs)
    halves = [m.reshape(N_SHARDS, 2, m.shape[1] // 2, m.shape[2]) for m in mats]

    def body(*refs):
        vsrc = refs[nm:nm + nv]
        out, vout = refs[nm + nv:2 * nm + nv], refs[2 * nm + nv:2 * (nm + nv)]
        send, recv, vsend, vrecv, loc = refs[2 * (nm + nv):]
        x, y, c, s_me, others = _place()
        sib = (x, y, 1 - c)

        def ici(l, k):
            ox, oy = others[k]
            return pltpu.make_async_remote_copy(out[l].at[s_me, c], out[l].at[s_me, c], send.at[l, k], recv.at[l, k],
                                                device_id=(ox, oy, c), device_id_type=MESH)

        def landed(l, k, half):
            ox, oy = others[k]
            return out[l].at[2 * ox + oy, half]

        def passed_on(l, k):
            return pltpu.make_async_remote_copy(landed(l, k, c), landed(l, k, c), send.at[l, 3 + k], recv.at[l, 3 + k],
                                                device_id=sib, device_id_type=MESH)

        def vec(l, k):
            ox, oy = others[k]
            return pltpu.make_async_remote_copy(vsrc[l], vout[l].at[s_me], vsend.at[l, k], vrecv.at[l, k],
                                                device_id=(ox, oy, c), device_id_type=MESH)

        for l in range(nm):
            for k in range(3):
                ici(l, k).start()
        for l in range(nv):
            for k in range(3):
                vec(l, k).start()
        for l in range(nv):
            own = pltpu.make_async_copy(vsrc[l], vout[l].at[s_me], loc)
            own.start()
            own.wait()
        for l in range(nm):
            for k in range(3):
                pltpu.make_async_remote_copy(landed(l, k, c), landed(l, k, c), send.at[l, k], recv.at[l, k],
                                             device_id=sib, device_id_type=MESH).wait_recv()
                passed_on(l, k).start()
        for l in range(nm):
            for k in range(3):
                pltpu.make_async_remote_copy(landed(l, k, 1 - c), landed(l, k, 1 - c), send.at[l, 3 + k],
                                             recv.at[l, 3 + k], device_id=sib, device_id_type=MESH).wait_recv()
        for l in range(nv):
            for k in range(3):
                ox, oy = others[k]
                pltpu.make_async_remote_copy(vsrc[l], vout[l].at[2 * ox + oy], vsend.at[l, k], vrecv.at[l, k],
                                             device_id=sib, device_id_type=MESH).wait_recv()
        for l in range(nm):
            for k in range(3):
                ici(l, k).wait_send()
                passed_on(l, k).wait_send()
        for l in range(nv):
            for k in range(3):
                vec(l, k).wait_send()

    out_shape = [_sds(h.shape, BF16) for h in halves] + [_sds((N_SHARDS,) + v.shape, F32) for v in vecs]
    res = _pcall(body, name=name, out_shape=out_shape, in_specs=[ANY] * (nm + nv), out_specs=[ANY] * (nm + nv),
                 scratch_shapes=[pltpu.SemaphoreType.DMA((max(nm, 1), 6)), pltpu.SemaphoreType.DMA((max(nm, 1), 6)),
                                 pltpu.SemaphoreType.DMA((max(nv, 1), 3)), pltpu.SemaphoreType.DMA((max(nv, 1), 3)),
                                 pltpu.SemaphoreType.DMA(())],
                 aliases={l: l for l in range(nm)}, side_effects=True)(*halves, *vecs)
    return [r.reshape(m.shape) for r, m in zip(res[:nm], mats)], list(res[nm:])


HBM = pl.BlockSpec(memory_space=pltpu.HBM)
SEM = pl.BlockSpec(memory_space=pltpu.SEMAPHORE)
DATAFLOW = pltpu.SideEffectType.DATAFLOW_SIDE_EFFECTING


def _split_call(body, *, name, out_shape, in_specs, out_specs, aliases):
    return pl.pallas_call(body, name=name, out_shape=out_shape, in_specs=in_specs, out_specs=out_specs,
                          input_output_aliases=aliases,
                          compiler_params=pltpu.CompilerParams(has_side_effects=DATAFLOW))


def _token_shape():
    return jax.ShapeDtypeStruct((8, LANES), F32)


def gather_start(mats, after, *, name):
    n = len(mats)
    halves = [pltpu.with_memory_space_constraint(m.reshape(N_SHARDS, 2, m.shape[1] // 2, m.shape[2]), pltpu.HBM)
              for m in mats]

    def body(*refs):
        send, recv = refs[n + 1], refs[n + 2]
        out, token = refs[n + 3:2 * n + 3], refs[2 * n + 3]
        x, y, c, s_me, others = _place()
        for l in range(n):
            for k in range(3):
                ox, oy = others[k]
                pltpu.make_async_remote_copy(out[l].at[s_me, c], out[l].at[s_me, c], send.at[3 * l + k],
                                             recv.at[3 * l + k], device_id=(ox, oy, c), device_id_type=MESH).start()
        token[...] = jnp.zeros_like(token)

    res = _split_call(
        body, name=name,
        out_shape=(pltpu.SemaphoreType.DMA((3 * n,)), pltpu.SemaphoreType.DMA((3 * n,)),
                   *[pltpu.HBM(h.shape, BF16) for h in halves], _token_shape()),
        in_specs=[HBM] * n + [ANY], out_specs=(SEM, SEM, *[HBM] * n, pl.BlockSpec(memory_space=pltpu.VMEM)),
        aliases={l: 2 + l for l in range(n)})(*halves, after)
    return res[0], res[1], list(res[2:2 + n]), res[2 + n]


def gather_pass_on(bufs, send_a, recv_a, after, *, name, base=0):
    n = len(bufs)

    def body(*refs):
        send_a, recv_a = refs[n], refs[n + 1]
        out = refs[n + 3:2 * n + 3]
        send_b, recv_b, token = refs[2 * n + 3:]
        x, y, c, s_me, others = _place()
        for l in range(n):
            for k in range(3):
                ox, oy = others[k]
                landed, i = out[l].at[2 * ox + oy, c], 3 * l + k
                pltpu.make_async_remote_copy(landed, landed, send_a.at[3 * base + i], recv_a.at[3 * base + i],
                                             device_id=(x, y, 1 - c), device_id_type=MESH).wait_recv()
                pltpu.make_async_remote_copy(landed, landed, send_b.at[i], recv_b.at[i],
                                             device_id=(x, y, 1 - c), device_id_type=MESH).start()
        for l in range(n):
            for k in range(3):
                mine, i = out[l].at[s_me, c], 3 * (base + l) + k
                pltpu.make_async_remote_copy(mine, mine, send_a.at[i], recv_a.at[i],
                                             device_id=(x, y, 1 - c), device_id_type=MESH).wait_send()
        token[...] = jnp.zeros_like(token)

    res = _split_call(
        body, name=name,
        out_shape=(*[pltpu.HBM(b.shape, BF16) for b in bufs], pltpu.SemaphoreType.DMA((3 * n,)),
                   pltpu.SemaphoreType.DMA((3 * n,)), _token_shape()),
        in_specs=[HBM] * n + [SEM, SEM, ANY],
        out_specs=(*[HBM] * n, SEM, SEM, pl.BlockSpec(memory_space=pltpu.VMEM)),
        aliases={l: l for l in range(n)})(*bufs, send_a, recv_a, after)
    return list(res[:n]), res[n], res[n + 1], res[n + 2]


def gather_finish(bufs, send_b, recv_b, after, shapes, *, name):
    n = len(bufs)

    def body(*refs):
        send_b, recv_b = refs[n], refs[n + 1]
        out = refs[n + 3:]
        x, y, c, _, others = _place()
        for l in range(n):
            for k in range(3):
                ox, oy = others[k]
                theirs, mine, i = out[l].at[2 * ox + oy, 1 - c], out[l].at[2 * ox + oy, c], 3 * l + k
                pltpu.make_async_remote_copy(theirs, theirs, send_b.at[i], recv_b.at[i],
                                             device_id=(x, y, 1 - c), device_id_type=MESH).wait_recv()
                pltpu.make_async_remote_copy(mine, mine, send_b.at[i], recv_b.at[i],
                                             device_id=(x, y, 1 - c), device_id_type=MESH).wait_send()

    res = _split_call(
        body, name=name, out_shape=tuple(pltpu.HBM(b.shape, BF16) for b in bufs),
        in_specs=[HBM] * n + [SEM, SEM, ANY], out_specs=tuple([HBM] * n),
        aliases={l: l for l in range(n)})(*bufs, send_b, recv_b, after)
    return [r.reshape(s) for r, s in zip(res, shapes)]


def exchange_start(srcs, dst_shapes, dst_dtype, plan, count, after, *, name):
    n, m = len(srcs), len(dst_shapes)
    srcs = [pltpu.with_memory_space_constraint(s, pltpu.HBM) for s in srcs]
    lands = [pltpu.with_memory_space_constraint(lax.empty(s, dst_dtype), pltpu.HBM) for s in dst_shapes]

    def body(*refs):
        send, recv = refs[n + m + 1], refs[n + m + 2]
        src, dst, token = refs[n + m + 3:2 * n + m + 3], refs[2 * n + m + 3:2 * (n + m) + 3], refs[2 * (n + m) + 3]
        for i, (s, d, dev) in enumerate(plan(_place(), src, dst)):
            pltpu.make_async_remote_copy(s, d, send.at[i], recv.at[i], device_id=dev, device_id_type=MESH).start()
        token[...] = jnp.zeros_like(token)

    res = _split_call(
        body, name=name,
        out_shape=(pltpu.SemaphoreType.DMA((count,)), pltpu.SemaphoreType.DMA((count,)),
                   *[pltpu.HBM(s.shape, s.dtype) for s in srcs], *[pltpu.HBM(s, dst_dtype) for s in dst_shapes],
                   _token_shape()),
        in_specs=[HBM] * (n + m) + [ANY],
        out_specs=(SEM, SEM, *[HBM] * (n + m), pl.BlockSpec(memory_space=pltpu.VMEM)),
        aliases={i: 2 + i for i in range(n + m)})(*srcs, *lands, after)
    return (list(res[2:2 + n]), list(res[2 + n:2 + n + m]), res[0], res[1], plan), res[2 + n + m]


def exchange_finish(state, after, *, name):
    srcs, lands, send, recv, plan = state
    n, m = len(srcs), len(lands)

    def body(*refs):
        send, recv = refs[n + m], refs[n + m + 1]
        src, dst = refs[n + m + 3:2 * n + m + 3], refs[2 * n + m + 3:]
        for i, (s, d, dev) in enumerate(plan(_place(), src, dst)):
            pltpu.make_async_remote_copy(s, d, send.at[i], recv.at[i], device_id=dev, device_id_type=MESH).wait()

    res = _split_call(
        body, name=name,
        out_shape=tuple(pltpu.HBM(a.shape, a.dtype) for a in srcs + lands),
        in_specs=[HBM] * (n + m) + [SEM, SEM, ANY], out_specs=tuple([HBM] * (n + m)),
        aliases={i: i for i in range(n + m)})(*srcs, *lands, send, recv, after)
    return list(res[:n]), list(res[n:])


def pair_plan(place, src, dst):
    x, y, c, _, _ = place
    return [(s.at[:, 1 - c], d, (x, y, 1 - c)) for s, d in zip(src, dst)]


def chip_plan(place, src, dst):
    x, y, c, _, others = place
    return [(s.at[2 * ox + oy], d.at[k], (ox, oy, c)) for s, d in zip(src, dst) for k, (ox, oy) in enumerate(others)]


def pair_exchange(grads, *, name):
    n = len(grads)

    def body(*refs):
        src, got = refs[:n], refs[n:2 * n]
        send, recv = refs[2 * n:]
        x, y, c, _, _ = _place()

        def swap(l):
            return pltpu.make_async_remote_copy(src[l].at[:, 1 - c], got[l], send.at[l], recv.at[l],
                                                device_id=(x, y, 1 - c), device_id_type=MESH)

        for l in range(n):
            swap(l).start()
        for l in range(n):
            swap(l).wait()

    res = _pcall(body, name=name, out_shape=[_sds((N_SHARDS,) + g.shape[2:], F32) for g in grads],
                 in_specs=[ANY] * n, out_specs=[ANY] * n,
                 scratch_shapes=[pltpu.SemaphoreType.DMA((n,)), pltpu.SemaphoreType.DMA((n,))],
                 side_effects=True)(*grads)
    return list(res)


def add_to_wire(mine, theirs, core, *, name, tm=512):
    s, _, r, c = mine.shape
    tm = min(tm, r)

    def body(core_ref, a_ref, b_ref, o_ref):
        o_ref[...] = (a_ref[...] + b_ref[...]).astype(BF16)

    spec = pl.BlockSpec((None, tm, c), lambda i, j, cr: (i, j, 0))
    return _pcall(body, name=name, out_shape=_sds((s, r, c), BF16), grid=(s, r // tm), num_prefetch=1,
                  in_specs=[pl.BlockSpec((None, None, tm, c), lambda i, j, cr: (i, cr[0], j, 0)), spec],
                  out_specs=spec, semantics=("parallel", "parallel"))(core, mine, theirs)


def sum_chips(wire, landed, place, dest, layer, n_layers, *, name, tm=512):
    _, r, c = wire.shape
    tm = min(tm, r)

    def body(place_ref, w_ref, l_ref, *rest):
        o_ref = rest[-1]
        o_ref[...] = ((w_ref[...].astype(F32) + l_ref[0].astype(F32)) + l_ref[1].astype(F32)) + l_ref[2].astype(F32)

    in_specs = [pl.BlockSpec((None, tm, c), lambda i, pr: (pr[0], i, 0)),
                pl.BlockSpec((3, tm, c), lambda i, pr: (0, i, 0))]
    args = [place, wire, landed]
    aliases = None
    if dest is not None:
        in_specs.append(ANY)
        args.append(dest)
        aliases = {3: 0}
    return _pcall(body, name=name, out_shape=_sds((n_layers, 2, r, c), F32), grid=(r // tm,), num_prefetch=1,
                  in_specs=in_specs,
                  out_specs=pl.BlockSpec((None, None, tm, c), lambda i, pr: (layer, pr[1], i, 0)),
                  aliases=aliases, semantics=("parallel",))(*args)


def pair_share(bufs, slots, *, name):
    n = len(bufs)

    def body(*refs):
        out = refs[n:2 * n]
        send, recv = refs[2 * n:]
        x, y, c, _, _ = _place()

        def share(i, half):
            o, l = slots[i]
            return pltpu.make_async_remote_copy(out[o].at[l, half], out[o].at[l, half], send.at[i], recv.at[i],
                                                device_id=(x, y, 1 - c), device_id_type=MESH)

        for i in range(len(slots)):
            share(i, c).start()
        for i in range(len(slots)):
            share(i, 1 - c).wait_recv()
            share(i, c).wait_send()

    res = _pcall(body, name=name, out_shape=[_sds(b.shape, F32) for b in bufs], in_specs=[ANY] * n,
                 out_specs=[ANY] * n,
                 scratch_shapes=[pltpu.SemaphoreType.DMA((len(slots),)), pltpu.SemaphoreType.DMA((len(slots),))],
                 aliases={o: o for o in range(n)}, side_effects=True)(*bufs)
    return list(res)


def all_reduce_small(packed, *, name):
    n_dev, r, c = packed.shape

    def body(in_ref, out_ref, land, send, recv):
        x, y, cc, _, _ = _place()
        me = 4 * x + 2 * y + cc
        peers = [(px, py, pc) for px in range(2) for py in range(2) for pc in range(2)]

        def scatter(d):
            return pltpu.make_async_remote_copy(in_ref.at[d], land.at[me], send.at[0, d], recv.at[0, me],
                                                device_id=peers[d], device_id_type=MESH)

        def gather(d):
            return pltpu.make_async_remote_copy(out_ref.at[me], out_ref.at[me], send.at[1, d], recv.at[1, me],
                                                device_id=peers[d], device_id_type=MESH)

        for d in range(n_dev):
            @pl.when(d != me)
            def _():
                scatter(d).start()
        land[me] = in_ref[me]
        for d in range(n_dev):
            @pl.when(d != me)
            def _():
                pltpu.make_async_remote_copy(in_ref.at[d], land.at[d], send.at[0, d], recv.at[0, d],
                                             device_id=peers[d], device_id_type=MESH).wait_recv()
        total = land[0]
        for d in range(1, n_dev):
            total = total + land[d]
        out_ref[me] = total
        for d in range(n_dev):
            @pl.when(d != me)
            def _():
                gather(d).start()
        for d in range(n_dev):
            @pl.when(d != me)
            def _():
                pltpu.make_async_remote_copy(out_ref.at[d], out_ref.at[d], send.at[1, d], recv.at[1, d],
                                             device_id=peers[d], device_id_type=MESH).wait_recv()
        for d in range(n_dev):
            @pl.when(d != me)
            def _():
                scatter(d).wait_send()
                gather(d).wait_send()

    vm = pl.BlockSpec(memory_space=pltpu.VMEM)
    return _pcall(body, name=name, out_shape=_sds(packed.shape, F32), in_specs=[vm], out_specs=vm,
                  scratch_shapes=[pltpu.VMEM(packed.shape, F32), pltpu.SemaphoreType.DMA((2, n_dev)),
                                  pltpu.SemaphoreType.DMA((2, n_dev))],
                  side_effects=True)(packed)


def adamw(w, g, m, v, *, name, part=None, dest=None, tm=512):
    shape = w.shape
    cols = shape[-1]
    rows = 1
    for s in shape[:-1]:
        rows *= s
    first, count = 0, rows
    if part is not None:
        count = rows // part[1]
        first = part[0] * count
    tm = min(tm, count)
    assert count % tm == 0
    two_d = lambda a: a.reshape(rows, cols)

    def body(w_ref, g_ref, m_ref, v_ref, *rest):
        d_ref, mo_ref, vo_ref = rest[-3:]
        gv = g_ref[...]
        m_new = ADAM_B1 * m_ref[...] + (1.0 - ADAM_B1) * gv
        v_new = ADAM_B2 * v_ref[...] + (1.0 - ADAM_B2) * (gv * gv)
        m_hat = m_new / (1.0 - ADAM_B1 ** ADAM_STEP)
        v_hat = v_new / (1.0 - ADAM_B2 ** ADAM_STEP)
        d_ref[...] = -ADAM_LR * (m_hat / (jnp.sqrt(v_hat) + ADAM_EPS) + ADAM_WD * w_ref[...])
        mo_ref[...] = m_new
        vo_ref[...] = v_new

    spec = pl.BlockSpec((tm, cols), lambda i: (first // tm + i, 0))
    args = [two_d(w), two_d(g), two_d(m), two_d(v)]
    in_specs = [spec] * 4
    aliases = None
    if dest is not None:
        args += [two_d(d) for d in dest]
        in_specs = in_specs + [ANY] * 3
        aliases = {4: 0, 5: 1, 6: 2}
    outs = _pcall(body, name=name, out_shape=[_sds((rows, cols), F32)] * 3, grid=(count // tm,), in_specs=in_specs,
                  out_specs=[spec] * 3, aliases=aliases, semantics=("parallel",))(*args)
    return [o.reshape(shape) for o in outs]


WEIGHTS = ("ln_mix_a", "w_in_a", "g_v_a", "w_spatial", "b_spatial", "w_out_a", "ln_kv", "w_kv", "g_k", "ln_mix_b",
           "w_q", "g_q", "w_out_b", "ln_mlp", "w_up", "w_down", "ln_ple", "w_ple_gate", "w_ple_proj")
MATRICES = (("w_in_a", 1, True), ("w_out_a", 1, False), ("w_kv", 0, True), ("w_q", 1, False), ("w_out_b", 1, False),
            ("w_up", 2, True), ("w_down", 2, False), ("w_ple_gate", 2, False), ("w_ple_proj", 2, True))
GATHER_STAGES = ((("w_in_a", 0),), (("w_out_a", 0),), (("w_up", 0),), (("w_down", 0),),
                 (("w_ple_gate", 0), ("w_ple_proj", 0), ("w_kv", 0)), (("w_q", 0),),
                 (("w_out_b", 0), ("w_up", 1), ("w_down", 1), ("w_ple_gate", 1), ("w_ple_proj", 1)))
REPLICATED = ("w_spatial", "b_spatial", "ln_kv", "g_k", "ln_mix_b", "g_q", "ln_mlp", "ln_ple")
SHARDED_VECTORS = ("ln_mix_a", "g_v_a")
SMALL_ROWS = 18


def kernel(x, p, ln_mix_a, w_in_a, g_v_a, w_spatial, b_spatial, w_out_a, ln_kv, w_kv, g_k, ln_mix_b, w_q, g_q, w_out_b, ln_mlp, w_up, w_down, ln_ple, w_ple_gate, w_ple_proj, loss_target, m_ln_mix_a, m_w_in_a, m_g_v_a, m_w_spatial, m_b_spatial, m_w_out_a, m_ln_kv, m_w_kv, m_g_k, m_ln_mix_b, m_w_q, m_g_q, m_w_out_b, m_ln_mlp, m_w_up, m_w_down, m_ln_ple, m_w_ple_gate, m_w_ple_proj, v_ln_mix_a, v_w_in_a, v_g_v_a, v_w_spatial, v_b_spatial, v_w_out_a, v_ln_kv, v_w_kv, v_g_k, v_ln_mix_b, v_w_q, v_g_q, v_w_out_b, v_ln_mlp, v_w_up, v_w_down, v_ln_ple, v_w_ple_gate, v_w_ple_proj):
    given = dict(locals())
    weights = {n: given[n] for n in WEIGHTS}
    shard = 2 * lax.axis_index("x") + lax.axis_index("y")
    core = lax.axis_index("c")
    shard_1 = shard.astype(jnp.int32).reshape(1)
    core_1 = core.astype(jnp.int32).reshape(1)
    place = jnp.stack([shard, core]).astype(jnp.int32)

    col_sharded = {name: cols for name, _, cols in MATRICES}
    layer_count = {name: max(layers, 1) for name, layers, _ in MATRICES}

    def cast(key, after):
        name, layer = key
        w3 = weights[name] if weights[name].ndim == 3 else weights[name][None]
        return (name, layer, col_sharded[name],
                cast_into_slot(w3, layer, shard_1, name=f"cast_{name}_{layer}", after=after))

    head = [cast(key, None) for key in GATHER_STAGES[0]]
    send_h, recv_h, flying_h, token_h = gather_start([lf[3] for lf in head], shard_1, name="gather_start_0")
    tail = [cast(key, token_h) for stage in GATHER_STAGES[1:] for key in stage]
    _, vec_a = gather_shards([], [ln_mix_a, g_v_a], name="gather_vectors")
    send_a, recv_a, flying, token = gather_start([lf[3] for lf in tail], vec_a[0], name="gather_start_1")

    w = {"ln_mix_a": vec_a[0].reshape(1, D_MODEL) + token[0, 0],
         "g_v_a": vec_a[1].reshape(1, D_MODEL)}
    for name in REPLICATED:
        w[name] = weights[name]

    class Late:
        def weights(self, name, layer, after):
            stage = [(name, layer) in s for s in GATHER_STAGES].index(True)
            if stage == 0:
                base, members, sems, fly = 0, head, (send_h, recv_h), flying_h
            else:
                base = sum(len(s) for s in GATHER_STAGES[1:stage])
                members, sems, fly = tail[base:base + len(GATHER_STAGES[stage])], (send_a, recv_a), flying
            bufs, send_b, recv_b, tok = gather_pass_on(fly[base:base + len(members)], sems[0], sems[1], after,
                                                       name=f"gather_pass_on_{stage}", base=base)
            got = gather_finish(bufs, send_b, recv_b, tok, [lf[3].shape for lf in members],
                                name=f"gather_finish_{stage}")
            out = {}
            for (leaf_name, leaf_layer, cols, _), arr in zip(members, got):
                out[(leaf_name, leaf_layer)] = arr if cols else arr.reshape(N_SHARDS * arr.shape[1], arr.shape[2])
            return out

        groups = []

        def pair_start(self, grads_done, after):
            self.keys = sorted(grads_done)
            views = [view(k, grads_done[k]) for k in self.keys]
            self.pair, token = exchange_start(views, [(N_SHARDS,) + v.shape[2:] for v in views], F32, pair_plan,
                                              len(views), after, name=f"grad_pair_start_{len(self.groups)}")
            return token

        def chip_start(self, after):
            tag = len(self.groups)
            mine, theirs = exchange_finish(self.pair, after, name=f"grad_pair_finish_{tag}")
            wire = [add_to_wire(a, b, core_1, name=f"grad_pair_sum_{tag}_{i}")
                    for i, (a, b) in enumerate(zip(mine, theirs))]
            chip, token = exchange_start(wire, [(3,) + v.shape[1:] for v in wire], BF16, chip_plan, 3 * len(wire),
                                         wire[-1], name=f"grad_chip_start_{tag}")
            self.groups.append((self.keys, chip))
            return token

    def view(key, arr):
        rows = arr.shape[-2] if col_sharded[key[0]] else arr.shape[0] // N_SHARDS
        return arr.reshape(N_SHARDS, 2, rows // 2, arr.shape[-1])

    t = x.shape[1]
    late = Late()
    loss_blk, dx, g = local_step(x[0], p.reshape(2, t, PLE_DIM), loss_target[0], w, late)
    loss = lax.psum(loss_blk[0, 0], ("x", "y", "c"))

    sent = {k for keys, _ in late.groups for k in keys}
    keys_last = [(name, layer) for name, layers, _ in MATRICES for layer in range(max(layers, 1))
                 if (name, layer) not in sent]
    views = [view(k, g[k[0]][k[1]] if layer_count[k[0]] == 2 else g[k[0]]) for k in keys_last]

    theirs = pair_exchange(views, name="grad_pair_exchange_last")
    wire_0 = [add_to_wire(a, b, core_1, name=f"grad_pair_sum_last_{i}") for i, (a, b) in enumerate(zip(views, theirs))]
    chip_0, token_0 = exchange_start(wire_0, [(3,) + v.shape[1:] for v in wire_0], BF16, chip_plan, 3 * len(wire_0),
                                     wire_0[-1], name="grad_chip_start_last")

    grads, bufs = {}, {}

    def sum_and_share(keys, wire, landed, tag):
        for i, (key, wv, lv) in enumerate(zip(keys, wire, landed)):
            name, layer = key
            bufs[name] = sum_chips(wv, lv, place, bufs.get(name), layer, layer_count[name],
                                   name=f"grad_chip_sum_{tag}_{i}")
        names = sorted({k[0] for k in keys})
        shared = pair_share([bufs[n] for n in names], [(names.index(k[0]), k[1]) for k in keys],
                            name=f"grad_pair_share_{tag}")
        bufs.update(zip(names, shared))

    updates = {}

    def update(n, gn, part=None):
        wn, mn, vn = weights[n], given["m_" + n], given["v_" + n]
        if wn.ndim == 1:
            wn, gn, mn, vn = (a.reshape(1, -1) for a in (wn, gn, mn, vn))
        tag = "" if part is None else f"_{part[0]}"
        updates[n] = adamw(wn, gn.reshape(wn.shape), mn, vn, name=f"adamw_{n}{tag}", part=part, dest=updates.get(n))

    after = token_0
    for tag, (keys, chip) in enumerate(late.groups + [(keys_last, chip_0)]):
        wire, landed = exchange_finish(chip, after, name=f"grad_chip_finish_{tag}")
        sum_and_share(keys, wire, landed, tag)
        for name, layer in keys:
            update(name, bufs[name], (layer, layer_count[name]) if layer_count[name] == 2 else None)
        after = updates[keys[-1][0]][0]

    small = REPLICATED + SHARDED_VECTORS
    flat = jnp.concatenate([g[n].reshape(-1) for n in small])
    room = 8 * SMALL_ROWS * D_MODEL
    flat = jnp.concatenate([flat, jnp.zeros((room - flat.shape[0],), F32)])
    flat, _ = lax.optimization_barrier((flat, after))
    reduced = all_reduce_small(flat.reshape(8, SMALL_ROWS, D_MODEL), name="grad_small_all_reduce").reshape(-1)
    at = 0
    for n in small:
        size = g[n].size
        piece = reduced[at:at + size]
        at += size
        if n in SHARDED_VECTORS:
            per = D_MODEL // N_SHARDS
            grads[n] = lax.dynamic_slice(piece, (shard * per,), (per,)).reshape(weights[n].shape)
        else:
            grads[n] = piece.reshape(weights[n].shape)
        update(n, grads[n])
    for name, _, _ in MATRICES:
        grads[name] = bufs[name].reshape(weights[name].shape)
    delta = {n: updates[n][0].reshape(weights[n].shape) for n in WEIGHTS}
    new_m = {n: updates[n][1].reshape(weights[n].shape) for n in WEIGHTS}
    new_v = {n: updates[n][2].reshape(weights[n].shape) for n in WEIGHTS}
    return (loss, dx.reshape(x.shape), *[grads[n] for n in WEIGHTS], *[delta[n] for n in WEIGHTS],
            *[new_m[n] for n in WEIGHTS], *[new_v[n] for n in WEIGHTS])
```

```python
import jax
import jax.numpy as jnp
from jax import lax
from jax.experimental import pallas as pl
from jax.experimental.pallas import tpu as pltpu

F32 = jnp.float32
BF16 = jnp.bfloat16

D_MODEL = 1024
D_FF = 4096
PLE_DIM = 256
N_GROUPS = 8
CHUNK = 128
HEAD_DIM = 64
LANES = 128
ATT_K_BLOCK = 256
ATT_Q_BLOCK = 512
EPS = 1e-6
N_SHARDS = 4
VMEM_LIMIT = 56 * 1024 * 1024

ADAM_LR = 0.001
ADAM_B1 = 0.9
ADAM_B2 = 0.999
ADAM_EPS = 1e-08
ADAM_WD = 0.01
ADAM_STEP = 10

MESH = pl.DeviceIdType.MESH


def _pcall(body, *, name, out_shape, grid=None, in_specs=None, out_specs=None, scratch_shapes=(),
           semantics=None, aliases=None, side_effects=False, num_prefetch=0):
    params = dict(vmem_limit_bytes=VMEM_LIMIT)
    if semantics is not None:
        params["dimension_semantics"] = semantics
    if side_effects:
        params["has_side_effects"] = True
    kwargs = {}
    if aliases:
        kwargs["input_output_aliases"] = aliases
    if num_prefetch:
        spec = pltpu.PrefetchScalarGridSpec(num_scalar_prefetch=num_prefetch, grid=grid, in_specs=in_specs,
                                            out_specs=out_specs, scratch_shapes=list(scratch_shapes))
        return pl.pallas_call(body, name=name, out_shape=out_shape, grid_spec=spec,
                              compiler_params=pltpu.CompilerParams(**params), **kwargs)
    if grid is not None:
        kwargs["grid"] = grid
    if in_specs is not None:
        kwargs["in_specs"] = in_specs
    if out_specs is not None:
        kwargs["out_specs"] = out_specs
    if aliases:
        kwargs["input_output_aliases"] = aliases
    return pl.pallas_call(body, name=name, out_shape=out_shape, scratch_shapes=list(scratch_shapes),
                          compiler_params=pltpu.CompilerParams(**params), **kwargs)


def _sds(shape, dtype):
    return jax.ShapeDtypeStruct(shape, dtype)


_GELU_C = 0.7978845608028654
_GELU_A = 0.044715


def _gelu(x):
    inner = _GELU_C * (x + _GELU_A * (x * x * x))
    return 0.5 * x * (1.0 + jnp.tanh(inner))


def _gelu_grad(x):
    x2 = x * x
    t = jnp.tanh(_GELU_C * (x + _GELU_A * (x2 * x)))
    return 0.5 * (1.0 + t) + 0.5 * x * (1.0 - t * t) * (_GELU_C * (1.0 + 3.0 * _GELU_A * x2))


def _sigmoid(x):
    return 1.0 / (1.0 + jnp.exp(-x))


def _log_sigmoid(z):
    return jnp.minimum(z, 0.0) - jnp.log(1.0 + jnp.exp(-jnp.abs(z)))


def _dot(a, b):
    return jnp.dot(a, b, preferred_element_type=F32)


def _dot_nt(a, b):
    return lax.dot_general(a, b, (((1,), (1,)), ((), ())), preferred_element_type=F32)


def _dot_tn(a, b):
    return lax.dot_general(a, b, (((0,), (0,)), ((), ())), preferred_element_type=F32)


def _head_rstd(x):
    lane = lax.broadcasted_iota(jnp.int32, x.shape, 1)
    low = lane < HEAD_DIM
    sq = x * x
    s_lo = jnp.sum(jnp.where(low, sq, 0.0), axis=-1, keepdims=True)
    s_hi = jnp.sum(jnp.where(low, 0.0, sq), axis=-1, keepdims=True)
    ms = jnp.where(low, s_lo, s_hi) * (1.0 / HEAD_DIM)
    return lax.rsqrt(ms + EPS)


def _head_mean(x):
    lane = lax.broadcasted_iota(jnp.int32, x.shape, 1)
    low = lane < HEAD_DIM
    s_lo = jnp.sum(jnp.where(low, x, 0.0), axis=-1, keepdims=True)
    s_hi = jnp.sum(jnp.where(low, 0.0, x), axis=-1, keepdims=True)
    return jnp.where(low, s_lo, s_hi) * (1.0 / HEAD_DIM)


def _full(shape):
    zeros = (0,) * len(shape)
    return pl.BlockSpec(shape, lambda i: zeros)


def norm_matmul(x, g, w, *, name, epilogue="none", tm=512):
    t, d = x.shape
    sharded = w.ndim == 3
    per = w.shape[2] if sharded else w.shape[1]
    n = N_SHARDS * per if sharded else per
    tm = min(tm, t)

    def body(x_ref, g_ref, w_ref, h_ref, r_ref, *outs):
        xv = x_ref[...]
        r = lax.rsqrt(jnp.mean(xv * xv, axis=-1, keepdims=True) + EPS)
        h = ((xv * r) * g_ref[...]).astype(BF16)
        h_ref[...] = h
        r_ref[...] = r
        for s in range(N_SHARDS if sharded else 1):
            cols = slice(s * per, (s + 1) * per)
            y = _dot(h, w_ref[s] if sharded else w_ref[...])
            if epilogue == "none":
                outs[0][:, cols] = y
            else:
                a = jnp.maximum(y, 0.0)
                outs[0][:, cols] = a.astype(BF16)
                outs[1][:, cols] = (a * a).astype(BF16)

    row = lambda i: (i, 0)
    out_shape = [_sds((t, d), BF16), _sds((t, 1), F32)]
    out_specs = [pl.BlockSpec((tm, d), row), pl.BlockSpec((tm, 1), row)]
    if epilogue == "none":
        out_shape.append(_sds((t, n), F32))
        out_specs.append(pl.BlockSpec((tm, n), row))
    else:
        out_shape += [_sds((t, n), BF16), _sds((t, n), BF16)]
        out_specs += [pl.BlockSpec((tm, n), row)] * 2
    return _pcall(
        body, name=name, out_shape=out_shape, grid=(t // tm,),
        in_specs=[pl.BlockSpec((tm, d), row), _full((1, d)), _full(w.shape)],
        out_specs=out_specs, semantics=("parallel",))(x, g, w)


def matmul_residual(a, w, res, *, name, tm=512):
    t, k = a.shape
    n = w.shape[1]
    tm = min(tm, t)

    def body(a_ref, w_ref, res_ref, o_ref):
        o_ref[...] = res_ref[...] + _dot(a_ref[...], w_ref[...])

    row = lambda i: (i, 0)
    return _pcall(
        body, name=name, out_shape=_sds((t, n), F32), grid=(t // tm,),
        in_specs=[pl.BlockSpec((tm, k), row), _full(w.shape), pl.BlockSpec((tm, n), row)],
        out_specs=pl.BlockSpec((tm, n), row), semantics=("parallel",))(a, w, res)


def ple_forward(x, g, w_gate, p, w_proj, *, name, tm=256):
    t, d = x.shape
    tm = min(tm, t)

    def body(x_ref, g_ref, wg_ref, p_ref, wp_ref, h_ref, r_ref, gate_ref, pp_ref, o_ref):
        xv = x_ref[...]
        r = lax.rsqrt(jnp.mean(xv * xv, axis=-1, keepdims=True) + EPS)
        h = ((xv * r) * g_ref[...]).astype(BF16)
        h_ref[...] = h
        r_ref[...] = r
        gate = _sigmoid(_dot(h, wg_ref[...]))
        gate_ref[...] = gate
        pb = p_ref[...].astype(BF16)
        per = d // N_SHARDS
        for s in range(N_SHARDS):
            cols = slice(s * per, (s + 1) * per)
            pp = _dot(pb, wp_ref[s])
            pp_ref[:, cols] = pp.astype(BF16)
            o_ref[:, cols] = xv[:, cols] + pp * gate[:, cols]

    row = lambda i: (i, 0)
    fixed = lambda i: (0, 0)
    return _pcall(
        body, name=name,
        out_shape=[_sds((t, d), BF16), _sds((t, 1), F32), _sds((t, d), F32), _sds((t, d), BF16), _sds((t, d), F32)],
        grid=(t // tm,),
        in_specs=[pl.BlockSpec((tm, d), row), pl.BlockSpec((1, d), fixed), pl.BlockSpec((d, d), fixed),
                  pl.BlockSpec((tm, PLE_DIM), row),
                  pl.BlockSpec((N_SHARDS, PLE_DIM, d // N_SHARDS), lambda i: (0, 0, 0))],
        out_specs=[pl.BlockSpec((tm, d), row), pl.BlockSpec((tm, 1), row), pl.BlockSpec((tm, d), row),
                   pl.BlockSpec((tm, d), row), pl.BlockSpec((tm, d), row)],
        semantics=("parallel",))(x, g, w_gate, p, w_proj)


def _tril_mask():
    r = lax.broadcasted_iota(jnp.int32, (CHUNK, CHUNK), 0)
    c = lax.broadcasted_iota(jnp.int32, (CHUNK, CHUNK), 1)
    return c <= r


def _sgu_common(pre_ref, gv_ref, ws_ref):
    pre = pre_ref[...]
    pre_u, pre_v = pre[:, :D_MODEL], pre[:, D_MODEL:]
    u = _gelu(pre_u)
    v = _gelu(pre_v)
    r = lax.rsqrt(jnp.mean(v * v, axis=-1, keepdims=True) + EPS)
    vhat = v * r
    vn = (vhat * gv_ref[...]).astype(BF16)
    tril = _tril_mask()
    wm = [jnp.where(tril, ws_ref[g], 0.0).astype(BF16) for g in range(N_GROUPS)]
    return pre_u, pre_v, u, r, vhat, vn, wm, tril


def sgu_forward(pre, g_v, w_s, b_full, *, name):
    t = pre.shape[0]

    def body(pre_ref, gv_ref, ws_ref, b_ref, y_ref):
        _, _, u, _, _, vn, wm, _ = _sgu_common(pre_ref, gv_ref, ws_ref)
        for g in range(N_GROUPS):
            cols = slice(g * LANES, (g + 1) * LANES)
            mix = _dot(wm[g], vn[:, cols]) + b_ref[:, cols]
            y_ref[:, cols] = (u[:, cols] * mix).astype(BF16)

    return _pcall(
        body, name=name, out_shape=_sds((t, D_MODEL), BF16), grid=(t // CHUNK,),
        in_specs=[pl.BlockSpec((CHUNK, 2 * D_MODEL), lambda i: (i, 0)), pl.BlockSpec((1, D_MODEL), lambda i: (0, 0)),
                  pl.BlockSpec((N_GROUPS, CHUNK, CHUNK), lambda i: (0, 0, 0)),
                  pl.BlockSpec((CHUNK, D_MODEL), lambda i: (0, 0))],
        out_specs=pl.BlockSpec((CHUNK, D_MODEL), lambda i: (i, 0)),
        semantics=("parallel",))(pre, g_v, w_s, b_full)


def head_norm(pre, g128, *, name, col_block=0, scale=1.0, passthrough=False, tm=512):
    t = pre.shape[0]
    tm = min(tm, t)

    def body(*refs):
        if passthrough:
            x_ref, v_ref, g_ref, o_ref, vo_ref = refs
            vo_ref[...] = v_ref[...].astype(BF16)
        else:
            x_ref, g_ref, o_ref = refs
        g = g_ref[...] * scale
        for b in range(D_MODEL // LANES):
            cols = slice(b * LANES, (b + 1) * LANES)
            xv = x_ref[:, cols]
            o_ref[:, cols] = ((xv * _head_rstd(xv)) * g).astype(BF16)

    x_spec = pl.BlockSpec((tm, D_MODEL), lambda i: (i, col_block))
    g_spec = pl.BlockSpec((1, LANES), lambda i: (0, 0))
    o_spec = pl.BlockSpec((tm, D_MODEL), lambda i: (i, 0))
    if passthrough:
        return _pcall(body, name=name, out_shape=[_sds((t, D_MODEL), BF16)] * 2, grid=(t // tm,),
                      in_specs=[x_spec, pl.BlockSpec((tm, D_MODEL), lambda i: (i, 1)), g_spec],
                      out_specs=[o_spec, o_spec], semantics=("parallel",))(pre, pre, g128)
    return _pcall(body, name=name, out_shape=_sds((t, D_MODEL), BF16), grid=(t // tm,),
                  in_specs=[x_spec, g_spec], out_specs=o_spec, semantics=("parallel",))(pre, g128)


def _suffix_matrix(n):
    r = lax.broadcasted_iota(jnp.int32, (n, n), 0)
    c = lax.broadcasted_iota(jnp.int32, (n, n), 1)
    return jnp.where(r > c, 1.0, 0.0).astype(BF16)


def _prefix_matrix(n):
    r = lax.broadcasted_iota(jnp.int32, (n, n), 0)
    c = lax.broadcasted_iota(jnp.int32, (n, n), 1)
    return jnp.where(r < c, 1.0, 0.0).astype(BF16)


def _block_cumsum(a, tri):
    return _dot(a.astype(BF16), tri)


def _stacked_causal(nq, nk, shift):
    r = lax.broadcasted_iota(jnp.int32, (2 * nq, nk), 0)
    c = lax.broadcasted_iota(jnp.int32, (2 * nq, nk), 1)
    return c + shift < jnp.where(r >= nq, r - nq, r)


def _att_blocks(t):
    bq, bk = min(ATT_Q_BLOCK, t), min(ATT_K_BLOCK, t)
    return bq, bk, bq // bk


def _stack_heads(a, low):
    zero = jnp.zeros_like(a)
    return jnp.concatenate([jnp.where(low, a, zero), jnp.where(low, zero, a)], axis=0)


def stick_breaking_forward(q, k, v, *, name):
    t = q.shape[0]
    bq, bk, ratio = _att_blocks(t)

    def body(q_ref, k_ref, v_ref, o_ref):
        i = pl.program_id(1)
        low = lax.broadcasted_iota(jnp.int32, (bq, LANES), 1) < HEAD_DIM
        tri = _suffix_matrix(bk)
        qs = _stack_heads(q_ref[...], low)

        def block(j, carry, acc, causal=None):
            rows = pl.ds(pl.multiple_of(j * bk, bk), bk)
            z = _dot_nt(qs, k_ref[rows, :])
            ls = _log_sigmoid(z)
            lg = ls - z
            if causal is not None:
                lg = jnp.where(causal, lg, 0.0)
            s = ls + _block_cumsum(lg, tri) + carry
            a = jnp.exp(s)
            if causal is not None:
                a = jnp.where(causal, a, 0.0)
            acc = acc + _dot(a.astype(BF16), v_ref[rows, :])
            return carry + jnp.sum(lg, axis=-1, keepdims=True), acc

        state = (jnp.zeros((2 * bq, 1), F32), jnp.zeros((2 * bq, LANES), F32))
        for m in reversed(range(ratio)):
            state = block(ratio * i + m, state[0], state[1], _stacked_causal(bq, bk, m * bk))
        first = ratio * i

        def two_blocks(n, st):
            st = block(first - 1 - 2 * n, st[0], st[1])
            return block(first - 2 - 2 * n, st[0], st[1])

        state = lax.fori_loop(0, first // 2, two_blocks, state)
        _, acc = lax.fori_loop(0, first % 2, lambda n, st: block(0, st[0], st[1]), state)
        o_ref[...] = jnp.where(low, acc[:bq], acc[bq:]).astype(BF16)

    return _pcall(
        body, name=name, out_shape=_sds((t, D_MODEL), BF16), grid=(D_MODEL // LANES, t // bq),
        in_specs=[pl.BlockSpec((bq, LANES), lambda p, i: (i, p)), pl.BlockSpec((t, LANES), lambda p, i: (0, p)),
                  pl.BlockSpec((t, LANES), lambda p, i: (0, p))],
        out_specs=pl.BlockSpec((bq, LANES), lambda p, i: (i, p)),
        semantics=("parallel", "arbitrary"))(q, k, v)


def loss_forward(x, target, *, name, tm=512):
    t, d = x.shape
    tm = min(tm, t)

    def body(x_ref, t_ref, l_ref, dx_ref):
        @pl.when(pl.program_id(0) == 0)
        def _():
            l_ref[...] = jnp.zeros_like(l_ref)

        diff = x_ref[...] - t_ref[...]
        dx_ref[...] = diff * (1.0 / d)
        l_ref[...] += 0.5 * jnp.sum(jnp.mean(diff * diff, axis=-1, keepdims=True))

    return _pcall(
        body, name=name, out_shape=[_sds((8, LANES), F32), _sds((t, d), F32)], grid=(t // tm,),
        in_specs=[pl.BlockSpec((tm, d), lambda i: (i, 0))] * 2,
        out_specs=[pl.BlockSpec((8, LANES), lambda i: (0, 0)), pl.BlockSpec((tm, d), lambda i: (i, 0))],
        semantics=("arbitrary",))(x, target)


def matmul_nt(dy, w, *, name, mul=None, out_dtype=F32, tm=512):
    t, n = dy.shape
    k = w.shape[0]
    tm = min(tm, t)

    def body(*refs):
        if mul is None:
            dy_ref, w_ref, o_ref = refs
        else:
            dy_ref, w_ref, m_ref, o_ref = refs
        y = _dot_nt(dy_ref[...].astype(BF16), w_ref[...])
        if mul is not None:
            y = y * (2.0 * m_ref[...].astype(F32))
        o_ref[...] = y.astype(out_dtype)

    row = lambda i: (i, 0)
    in_specs = [pl.BlockSpec((tm, n), row), _full(w.shape)]
    args = [dy, w]
    if mul is not None:
        in_specs.append(pl.BlockSpec((tm, k), row))
        args.append(mul)
    return _pcall(body, name=name, out_shape=_sds((t, k), out_dtype), grid=(t // tm,), in_specs=in_specs,
                  out_specs=pl.BlockSpec((tm, k), row), semantics=("parallel",))(*args)


def matmul_tn(a, dy, *, name, col_shards, tk=512):
    t, k = a.shape
    n = dy.shape[1]
    if col_shards:
        tn = n // N_SHARDS

        def body(a_ref, dy_ref, o_ref):
            o_ref[...] = _dot_tn(a_ref[...].astype(BF16), dy_ref[...].astype(BF16))

        return _pcall(body, name=name, out_shape=_sds((N_SHARDS, k, tn), F32), grid=(N_SHARDS,),
                      in_specs=[_full((t, k)), pl.BlockSpec((t, tn), lambda j: (0, j))],
                      out_specs=pl.BlockSpec((None, k, tn), lambda j: (j, 0, 0)), semantics=("parallel",))(a, dy)

    tk = min(tk, k)

    def body(a_ref, dy_ref, o_ref, dy_bf):
        @pl.when(pl.program_id(0) == 0)
        def _():
            dy_bf[...] = dy_ref[...].astype(BF16)

        o_ref[...] = _dot_tn(a_ref[...].astype(BF16), dy_bf[...])

    return _pcall(body, name=name, out_shape=_sds((k, n), F32), grid=(k // tk,),
                  in_specs=[pl.BlockSpec((t, tk), lambda i: (0, i)), _full((t, n))],
                  out_specs=pl.BlockSpec((tk, n), lambda i: (i, 0)),
                  scratch_shapes=[pltpu.VMEM((t, n), BF16)], semantics=("arbitrary",))(a, dy)


def norm_backward(dpre, w, x, g, rstd, dx_out, *, name, tm=512):
    t, d = x.shape
    n = dpre.shape[1]
    tm = min(tm, t)
    if w.ndim == 3:
        w_spec = pl.BlockSpec(w.shape, lambda i: (0, 0, 0))
    else:
        w_spec = pl.BlockSpec(w.shape, lambda i: (0, 0))

    def body(dp_ref, w_ref, x_ref, g_ref, r_ref, dxo_ref, dx_ref, dg_ref):
        @pl.when(pl.program_id(0) == 0)
        def _():
            dg_ref[...] = jnp.zeros_like(dg_ref)

        if w.ndim == 3:
            per = n // N_SHARDS
            dh = _dot_nt(dp_ref[:, 0:per], w_ref[0])
            for s in range(1, N_SHARDS):
                dh = dh + _dot_nt(dp_ref[:, s * per:(s + 1) * per], w_ref[s])
        else:
            dh = _dot_nt(dp_ref[...], w_ref[...])
        r = r_ref[...]
        xn = x_ref[...] * r
        dg_ref[...] += jnp.sum(dh * xn, axis=0, keepdims=True)
        dxn = dh * g_ref[...]
        dx = r * (dxn - xn * jnp.mean(dxn * xn, axis=-1, keepdims=True))
        dx_ref[...] = dxo_ref[...] + dx

    row = lambda i: (i, 0)
    fixed = lambda i: (0, 0)
    return _pcall(
        body, name=name, out_shape=[_sds((t, d), F32), _sds((1, d), F32)], grid=(t // tm,),
        in_specs=[pl.BlockSpec((tm, n), row), w_spec, pl.BlockSpec((tm, d), row),
                  pl.BlockSpec((1, d), fixed), pl.BlockSpec((tm, 1), row), pl.BlockSpec((tm, d), row)],
        out_specs=[pl.BlockSpec((tm, d), row), pl.BlockSpec((1, d), fixed)],
        semantics=("arbitrary",))(dpre, w, x, g, rstd, dx_out)


def ple_backward(dx, gate, pp, *, name, tm=512):
    t, d = dx.shape
    tm = min(tm, t)

    def body(dx_ref, gate_ref, pp_ref, dg_ref, dp_ref):
        dxv = dx_ref[...]
        gate = gate_ref[...]
        dg_ref[...] = (dxv * pp_ref[...].astype(F32) * (gate * (1.0 - gate))).astype(BF16)
        dp_ref[...] = (dxv * gate).astype(BF16)

    spec = pl.BlockSpec((tm, d), lambda i: (i, 0))
    return _pcall(body, name=name, out_shape=[_sds((t, d), BF16)] * 2, grid=(t // tm,), in_specs=[spec] * 3,
                  out_specs=[spec] * 2, semantics=("parallel",))(dx, gate, pp)


def sgu_backward(dy, pre, g_v, w_s, b_full, *, name):
    t = pre.shape[0]
    n_chunks = t // CHUNK

    def body(dy_ref, pre_ref, gv_ref, ws_ref, b_ref, dpre_ref, dws_ref, db_ref, dgv_ref, dvn_s, dbf_s):
        step = pl.program_id(0)

        @pl.when(step == 0)
        def _():
            dws_ref[...] = jnp.zeros_like(dws_ref)
            dgv_ref[...] = jnp.zeros_like(dgv_ref)
            dbf_s[...] = jnp.zeros_like(dbf_s)

        pre_u, pre_v, u, r, vhat, vn, wm, tril = _sgu_common(pre_ref, gv_ref, ws_ref)
        dyv = dy_ref[...]
        for g in range(N_GROUPS):
            cols = slice(g * LANES, (g + 1) * LANES)
            mix = _dot(wm[g], vn[:, cols]) + b_ref[:, cols]
            dmix = dyv[:, cols] * u[:, cols]
            dmix_b = dmix.astype(BF16)
            du = dyv[:, cols] * mix
            dpre_ref[:, cols] = (du * _gelu_grad(pre_u[:, cols])).astype(BF16)
            dws_ref[g] += jnp.where(tril, _dot_nt(dmix_b, vn[:, cols]), 0.0)
            dbf_s[:, cols] += dmix
            dvn_s[:, cols] = _dot_tn(wm[g], dmix_b)
        dvn = dvn_s[...]
        dgv_ref[...] += jnp.sum(dvn * vhat, axis=0, keepdims=True)
        dxn = dvn * gv_ref[...]
        dv = r * (dxn - vhat * jnp.mean(dxn * vhat, axis=-1, keepdims=True))
        dpre_ref[:, D_MODEL:] = (dv * _gelu_grad(pre_v)).astype(BF16)

        @pl.when(step == n_chunks - 1)
        def _():
            lane = lax.broadcasted_iota(jnp.int32, (CHUNK, LANES), 1)
            acc = jnp.zeros((CHUNK, LANES), F32)
            for g in range(N_GROUPS):
                s = jnp.sum(dbf_s[:, g * LANES:(g + 1) * LANES], axis=-1, keepdims=True)
                acc = jnp.where(lane == g, s, acc)
            db_ref[...] = acc

    fixed2 = lambda i: (0, 0)
    return _pcall(
        body, name=name,
        out_shape=[_sds((t, 2 * D_MODEL), BF16), _sds((N_GROUPS, CHUNK, CHUNK), F32), _sds((CHUNK, LANES), F32),
                   _sds((1, D_MODEL), F32)],
        grid=(n_chunks,),
        in_specs=[pl.BlockSpec((CHUNK, D_MODEL), lambda i: (i, 0)), pl.BlockSpec((CHUNK, 2 * D_MODEL), lambda i: (i, 0)),
                  pl.BlockSpec((1, D_MODEL), fixed2), pl.BlockSpec((N_GROUPS, CHUNK, CHUNK), lambda i: (0, 0, 0)),
                  pl.BlockSpec((CHUNK, D_MODEL), fixed2)],
        out_specs=[pl.BlockSpec((CHUNK, 2 * D_MODEL), lambda i: (i, 0)),
                   pl.BlockSpec((N_GROUPS, CHUNK, CHUNK), lambda i: (0, 0, 0)), pl.BlockSpec((CHUNK, LANES), fixed2),
                   pl.BlockSpec((1, D_MODEL), fixed2)],
        scratch_shapes=[pltpu.VMEM((CHUNK, D_MODEL), F32), pltpu.VMEM((CHUNK, D_MODEL), F32)],
        semantics=("arbitrary",))(dy, pre, g_v, w_s, b_full)


def head_norm_backward(dy, pre, g128, *, name, col_block=0, scale=1.0, passthrough=None, tm=512):
    t = dy.shape[0]
    tm = min(tm, t)
    width = 2 * D_MODEL if passthrough is not None else D_MODEL

    def body(*refs):
        if passthrough is not None:
            dy_ref, x_ref, g_ref, dv_ref, o_ref, dg_ref = refs
            o_ref[:, D_MODEL:] = dv_ref[...].astype(BF16)
        else:
            dy_ref, x_ref, g_ref, o_ref, dg_ref = refs

        @pl.when(pl.program_id(0) == 0)
        def _():
            dg_ref[...] = jnp.zeros_like(dg_ref)

        g = g_ref[...]
        dg = jnp.zeros((1, LANES), F32)
        for b in range(D_MODEL // LANES):
            cols = slice(b * LANES, (b + 1) * LANES)
            xv = x_ref[:, cols]
            r = _head_rstd(xv)
            xn = xv * r
            dyv = dy_ref[:, cols] * scale
            dg = dg + jnp.sum(dyv * xn, axis=0, keepdims=True)
            dxn = dyv * g
            o_ref[:, cols] = (r * (dxn - xn * _head_mean(dxn * xn))).astype(BF16)
        dg_ref[...] += dg

    row = lambda i: (i, 0)
    in_specs = [pl.BlockSpec((tm, D_MODEL), row), pl.BlockSpec((tm, D_MODEL), lambda i: (i, col_block)),
                pl.BlockSpec((1, LANES), lambda i: (0, 0))]
    args = [dy, pre, g128]
    if passthrough is not None:
        in_specs.append(pl.BlockSpec((tm, D_MODEL), row))
        args.append(passthrough)
    return _pcall(body, name=name, out_shape=[_sds((t, width), BF16), _sds((1, LANES), F32)], grid=(t // tm,),
                  in_specs=in_specs,
                  out_specs=[pl.BlockSpec((tm, width), row), pl.BlockSpec((1, LANES), lambda i: (0, 0))],
                  semantics=("arbitrary",))(*args)


def stick_breaking_backward(q, k, v, do, *, name):
    t = q.shape[0]
    bq, bk, ratio = _att_blocks(t)

    def body(q_ref, k_ref, v_ref, do_ref, dq_ref, dk_ref, dv_ref, s_buf, sg_buf):
        i = pl.program_id(1)

        @pl.when(i == 0)
        def _():
            dk_ref[...] = jnp.zeros_like(dk_ref)
            dv_ref[...] = jnp.zeros_like(dv_ref)

        low = lax.broadcasted_iota(jnp.int32, (bq, LANES), 1) < HEAD_DIM
        suffix = _suffix_matrix(bk)
        prefix = _prefix_matrix(bk)
        qs = _stack_heads(q_ref[...], low)
        dos = _stack_heads(do_ref[...], low)
        first = ratio * i

        def log_weights(j, carry, causal=None):
            rows = pl.ds(pl.multiple_of(j * bk, bk), bk)
            z = _dot_nt(qs, k_ref[rows, :])
            ls = _log_sigmoid(z)
            lg = ls - z
            if causal is not None:
                lg = jnp.where(causal, lg, 0.0)
            s_buf[j] = ls + _block_cumsum(lg, suffix) + carry
            sg_buf[j] = jnp.exp(ls)
            return carry + jnp.sum(lg, axis=-1, keepdims=True)

        carry = jnp.zeros((2 * bq, 1), F32)
        for m in reversed(range(ratio)):
            carry = log_weights(first + m, carry, _stacked_causal(bq, bk, m * bk))
        carry = lax.fori_loop(0, first // 2,
                              lambda n, c: log_weights(first - 2 - 2 * n, log_weights(first - 1 - 2 * n, c)), carry)
        lax.fori_loop(0, first % 2, lambda n, c: log_weights(0, c), carry)

        def grads(j, pcarry, dq_acc, causal=None):
            rows = pl.ds(pl.multiple_of(j * bk, bk), bk)
            a = jnp.exp(s_buf[j])
            if causal is not None:
                a = jnp.where(causal, a, 0.0)
            sg = sg_buf[j]
            ds = _dot_nt(dos, v_ref[rows, :]) * a
            before = _block_cumsum(ds, prefix) + pcarry
            if causal is not None:
                before = jnp.where(causal, before, 0.0)
            dz = (ds - sg * (ds + before)).astype(BF16)
            dq_acc = dq_acc + _dot(dz, k_ref[rows, :])
            dk_ref[rows, :] += _dot_tn(dz, qs)
            dv_ref[rows, :] += _dot_tn(a.astype(BF16), dos)
            return pcarry + jnp.sum(ds, axis=-1, keepdims=True), dq_acc

        def two_blocks(n, st):
            st = grads(2 * n, st[0], st[1])
            return grads(2 * n + 1, st[0], st[1])

        state = lax.fori_loop(0, first // 2, two_blocks,
                              (jnp.zeros((2 * bq, 1), F32), jnp.zeros((2 * bq, LANES), F32)))
        state = lax.fori_loop(0, first % 2, lambda n, st: grads(first - 1, st[0], st[1]), state)
        for m in range(ratio):
            state = grads(first + m, state[0], state[1], _stacked_causal(bq, bk, m * bk))
        dq_ref[...] = jnp.where(low, state[1][:bq], state[1][bq:])

    full = pl.BlockSpec((t, LANES), lambda p, i: (0, p))
    qblk = pl.BlockSpec((bq, LANES), lambda p, i: (i, p))
    return _pcall(
        body, name=name, out_shape=[_sds((t, D_MODEL), F32)] * 3, grid=(D_MODEL // LANES, t // bq),
        in_specs=[qblk, full, full, qblk], out_specs=[qblk, full, full],
        scratch_shapes=[pltpu.VMEM((t // bk, 2 * bq, bk), F32), pltpu.VMEM((t // bk, 2 * bq, bk), F32)],
        semantics=("parallel", "arbitrary"))(q, k, v, do)


def _mlp_backward(dx, saved, g, w_up, w_down, tag):
    x, h, r, a, a2 = saved
    d_w_down = matmul_tn(a2, dx, name=f"d_w_down_{tag}", col_shards=False)
    dpre = matmul_nt(dx, w_down, name=f"d_mlp_act_{tag}", mul=a, out_dtype=BF16)
    d_w_up = matmul_tn(h, dpre, name=f"d_w_up_{tag}", col_shards=True)
    dx, d_g = norm_backward(dpre, w_up, x, g, r, dx, name=f"d_mlp_norm_{tag}")
    return dx, d_w_up, d_w_down, d_g


def _ple_backward(dx, saved, p, g, w_gate, tag):
    x, h, r, gate, pp = saved
    dgate, dproj = ple_backward(dx, gate, pp, name=f"d_ple_{tag}")
    d_w_proj = matmul_tn(p, dproj, name=f"d_w_ple_proj_{tag}", col_shards=True)
    d_w_gate = matmul_tn(h, dgate, name=f"d_w_ple_gate_{tag}", col_shards=False)
    dx, d_g = norm_backward(dgate, w_gate, x, g, r, dx, name=f"d_ple_norm_{tag}")
    return dx, d_w_gate, d_w_proj, d_g


def local_step(x, p, target, w, late=None):
    row = lambda v: v.reshape(1, -1)
    g128 = lambda v: jnp.tile(v.reshape(1, HEAD_DIM), (1, 2))
    scale = HEAD_DIM ** -0.5
    b_full = jnp.repeat(jnp.transpose(w["b_spatial"][0]), LANES, axis=1)
    w_s = w["w_spatial"][0]

    mats = {}
    for name, value in w.items():
        if isinstance(value, tuple):
            mats.update({(name, layer): v for layer, v in enumerate(value)})
    if "w_kv" in w:
        mats[("w_kv", 0)] = w["w_kv"]

    def fetch(name, layer, after):
        if (name, layer) not in mats:
            mats.update(late.weights(name, layer, after))
        return mats[(name, layer)]

    def mlp_forward(x_in, layer):
        h, r, a, a2 = norm_matmul(x_in, row(w["ln_mlp"][layer]), fetch("w_up", layer, x_in), name=f"mlp_up_{layer}",
                                  epilogue="relu2")
        return matmul_residual(a2, fetch("w_down", layer, a2), x_in, name=f"mlp_down_{layer}"), (x_in, h, r, a, a2)

    def ple(x_in, layer):
        return ple_forward(x_in, row(w["ln_ple"][layer]), fetch("w_ple_gate", layer, x_in), p[layer],
                           fetch("w_ple_proj", layer, x_in), name=f"ple_{layer}")

    x0 = x
    h_a, r_a, pre_a = norm_matmul(x0, row(w["ln_mix_a"][0]), fetch("w_in_a", 0, x0), name="sgu_in")
    y_a = sgu_forward(pre_a, row(w["g_v_a"][0]), w_s, b_full, name="sgu_mix")
    x1 = matmul_residual(y_a, fetch("w_out_a", 0, y_a), x0, name="sgu_out")
    x2, mlp0 = mlp_forward(x1, 0)
    ple0 = ple(x2, 0)
    x3 = ple0[4]
    h_kv, r_kv, kv_pre = norm_matmul(x3, row(w["ln_kv"]), fetch("w_kv", 0, x3), name="kv_proj")
    k_n, v_b = head_norm(kv_pre, g128(w["g_k"]), name="k_norm", passthrough=True)
    h_q, r_q, q_pre = norm_matmul(x3, row(w["ln_mix_b"][0]), fetch("w_q", 0, k_n), name="q_proj")
    q_n = head_norm(q_pre, g128(w["g_q"][0]), name="q_norm", scale=scale)
    o = stick_breaking_forward(q_n, k_n, v_b, name="sb_fwd")
    x4 = matmul_residual(o, fetch("w_out_b", 0, o), x3, name="sb_out")
    x5, mlp1 = mlp_forward(x4, 1)
    ple1 = ple(x5, 1)
    x6 = ple1[4]
    loss_blk, dx = loss_forward(x6, target, name="loss")

    g = {}
    dx, dwg1, dwp1, dlnp1 = _ple_backward(dx, (x5,) + tuple(ple1[:4]), p[1], row(w["ln_ple"][1]),
                                          mats[("w_ple_gate", 1)], 1)
    dx, dwu1, dwd1, dlnm1 = _mlp_backward(dx, mlp1, row(w["ln_mlp"][1]), mats[("w_up", 1)], mats[("w_down", 1)], 1)
    g["w_out_b"] = matmul_tn(o, dx, name="d_w_out_b", col_shards=False)
    do = matmul_nt(dx, mats[("w_out_b", 0)], name="d_sb_out", out_dtype=BF16)
    dq_n, dk_n, dv = stick_breaking_backward(q_n, k_n, v_b, do, name="sb_bwd")
    dq_pre, dgq = head_norm_backward(dq_n, q_pre, g128(w["g_q"][0]), name="d_q_norm", scale=scale)
    dkv_pre, dgk = head_norm_backward(dk_n, kv_pre, g128(w["g_k"]), name="d_k_norm", passthrough=dv)
    g["w_q"] = matmul_tn(h_q, dq_pre, name="d_w_q", col_shards=False)
    g["w_kv"] = matmul_tn(h_kv, dkv_pre, name="d_w_kv", col_shards=True)
    dx, g["ln_mix_b"] = norm_backward(dq_pre, mats[("w_q", 0)], x3, row(w["ln_mix_b"][0]), r_q, dx, name="d_q_in")
    dx, g["ln_kv"] = norm_backward(dkv_pre, mats[("w_kv", 0)], x3, row(w["ln_kv"]), r_kv, dx, name="d_kv_in")
    g["g_q"] = dgq[:, :HEAD_DIM] + dgq[:, HEAD_DIM:]
    g["g_k"] = (dgk[:, :HEAD_DIM] + dgk[:, HEAD_DIM:]).reshape(HEAD_DIM)
    g["ln_kv"] = g["ln_kv"].reshape(D_MODEL)
    ln_ple0, ln_mlp0, g_v0, ln_mix0 = (row(w["ln_ple"][0]), row(w["ln_mlp"][0]), row(w["g_v_a"][0]),
                                       row(w["ln_mix_a"][0]))
    if late is not None:
        ln_ple0 = ln_ple0 + late.pair_start(
            {("w_kv", 0): g["w_kv"], ("w_q", 0): g["w_q"], ("w_out_b", 0): g["w_out_b"], ("w_up", 1): dwu1,
             ("w_down", 1): dwd1, ("w_ple_gate", 1): dwg1, ("w_ple_proj", 1): dwp1}, dx)[0, 0]
    dx, dwg0, dwp0, dlnp0 = _ple_backward(dx, (x2,) + tuple(ple0[:4]), p[0], ln_ple0, mats[("w_ple_gate", 0)], 0)
    if late is not None:
        ln_mlp0 = ln_mlp0 + late.chip_start(dx)[0, 0]
    dx, dwu0, dwd0, dlnm0 = _mlp_backward(dx, mlp0, ln_mlp0, mats[("w_up", 0)], mats[("w_down", 0)], 0)
    if late is not None:
        g_v0 = g_v0 + late.pair_start({("w_up", 0): dwu0, ("w_down", 0): dwd0, ("w_ple_gate", 0): dwg0,
                                       ("w_ple_proj", 0): dwp0}, dx)[0, 0]
    g["w_out_a"] = matmul_tn(y_a, dx, name="d_w_out_a", col_shards=False)
    dy_a = matmul_nt(dx, mats[("w_out_a", 0)], name="d_sgu_out")
    dpre_a, dws, db, g["g_v_a"] = sgu_backward(dy_a, pre_a, g_v0, w_s, b_full, name="d_sgu_mix")
    if late is not None:
        ln_mix0 = ln_mix0 + late.chip_start(dpre_a)[0, 0]
    g["w_in_a"] = matmul_tn(h_a, dpre_a, name="d_w_in_a", col_shards=True)
    dx, g["ln_mix_a"] = norm_backward(dpre_a, mats[("w_in_a", 0)], x0, ln_mix0, r_a, dx, name="d_sgu_in")
    g["w_spatial"] = dws[None]
    g["b_spatial"] = jnp.transpose(db[:, :N_GROUPS])[None]
    g["w_up"] = (dwu0, dwu1)
    g["w_down"] = (dwd0, dwd1)
    g["w_ple_gate"] = (dwg0, dwg1)
    g["w_ple_proj"] = (dwp0, dwp1)
    g["ln_mlp"] = jnp.concatenate([dlnm0, dlnm1], axis=0)
    g["ln_ple"] = jnp.concatenate([dlnp0, dlnp1], axis=0)
    return loss_blk, dx, g


ANY = pl.BlockSpec(memory_space=pl.ANY)


def _place():
    x, y, c = lax.axis_index("x"), lax.axis_index("y"), lax.axis_index("c")
    others = [(1 - x, y), (x, 1 - y), (1 - x, 1 - y)]
    return x, y, c, 2 * x + y, others


def cast_into_slot(w3, layer, slot, *, name, after=None, tm=512):
    _, r, c = w3.shape
    tm = min(tm, r)

    def body(slot_ref, w_ref, *rest):
        rest[-1][...] = w_ref[...].astype(BF16)

    in_specs = [pl.BlockSpec((None, tm, c), lambda i, s: (layer, i, 0))]
    args = [slot, w3]
    if after is not None:
        in_specs.append(ANY)
        args.append(after)
    return _pcall(body, name=name, out_shape=_sds((N_SHARDS, r, c), BF16), grid=(r // tm,), num_prefetch=1,
                  in_specs=in_specs, out_specs=pl.BlockSpec((None, tm, c), lambda i, s: (s[0], i, 0)),
                  semantics=("parallel",))(*args)


def gather_shards(mats, vecs, *, name):
    nm, nv = len(mats), len(vecs)
    halves = [m.reshape(N_SHARDS, 2, m.shape[1] // 2, m.shape[2]) for m in mats]

    def body(*refs):
        vsrc = refs[nm:nm + nv]
        out, vout = refs[nm + nv:2 * nm + nv], refs[2 * nm + nv:2 * (nm + nv)]
        send, recv, vsend, vrecv, loc = refs[2 * (nm + nv):]
        x, y, c, s_me, others = _place()
        sib = (x, y, 1 - c)

        def ici(l, k):
            ox, oy = others[k]
            return pltpu.make_async_remote_copy(out[l].at[s_me, c], out[l].at[s_me, c], send.at[l, k], recv.at[l, k],
                                                device_id=(ox, oy, c), device_id_type=MESH)

        def landed(l, k, half):
            ox, oy = others[k]
            return out[l].at[2 * ox + oy, half]

        def passed_on(l, k):
            return pltpu.make_async_remote_copy(landed(l, k, c), landed(l, k, c), send.at[l, 3 + k], recv.at[l, 3 + k],
                                                device_id=sib, device_id_type=MESH)

        def vec(l, k):
            ox, oy = others[k]
            return pltpu.make_async_remote_copy(vsrc[l], vout[l].at[s_me], vsend.at[l, k], vrecv.at[l, k],
                                                device_id=(ox, oy, c), device_id_type=MESH)

        for l in range(nm):
            for k in range(3):
                ici(l, k).start()
        for l in range(nv):
            for k in range(3):
                vec(l, k).start()
        for l in range(nv):
            own = pltpu.make_async_copy(vsrc[l], vout[l].at[s_me], loc)
            own.start()
            own.wait()
        for l in range(nm):
            for k in range(3):
                pltpu.make_async_remote_copy(landed(l, k, c), landed(l, k, c), send.at[l, k], recv.at[l, k],
                                             device_id=sib, device_id_type=MESH).wait_recv()
                passed_on(l, k).start()
        for l in range(nm):
            for k in range(3):
                pltpu.make_async_remote_copy(landed(l, k, 1 - c), landed(l, k, 1 - c), send.at[l, 3 + k],
                                             recv.at[l, 3 + k], device_id=sib, device_id_type=MESH).wait_recv()
        for l in range(nv):
            for k in range(3):
                ox, oy = others[k]
                pltpu.make_async_remote_copy(vsrc[l], vout[l].at[2 * ox + oy], vsend.at[l, k], vrecv.at[l, k],
                                             device_id=sib, device_id_type=MESH).wait_recv()
        for l in range(nm):
            for k in range(3):
                ici(l, k).wait_send()
                passed_on(l, k).wait_send()
        for l in range(nv):
            for k in range(3):
                vec(l, k).wait_send()

    out_shape = [_sds(h.shape, BF16) for h in halves] + [_sds((N_SHARDS,) + v.shape, F32) for v in vecs]
    res = _pcall(body, name=name, out_shape=out_shape, in_specs=[ANY] * (nm + nv), out_specs=[ANY] * (nm + nv),
                 scratch_shapes=[pltpu.SemaphoreType.DMA((max(nm, 1), 6)), pltpu.SemaphoreType.DMA((max(nm, 1), 6)),
                                 pltpu.SemaphoreType.DMA((max(nv, 1), 3)), pltpu.SemaphoreType.DMA((max(nv, 1), 3)),
                                 pltpu.SemaphoreType.DMA(())],
                 aliases={l: l for l in range(nm)}, side_effects=True)(*halves, *vecs)
    return [r.reshape(m.shape) for r, m in zip(res[:nm], mats)], list(res[nm:])


HBM = pl.BlockSpec(memory_space=pltpu.HBM)
SEM = pl.BlockSpec(memory_space=pltpu.SEMAPHORE)
DATAFLOW = pltpu.SideEffectType.DATAFLOW_SIDE_EFFECTING


def _split_call(body, *, name, out_shape, in_specs, out_specs, aliases):
    return pl.pallas_call(body, name=name, out_shape=out_shape, in_specs=in_specs, out_specs=out_specs,
                          input_output_aliases=aliases,
                          compiler_params=pltpu.CompilerParams(has_side_effects=DATAFLOW))


def _token_shape():
    return jax.ShapeDtypeStruct((8, LANES), F32)


def gather_start(mats, after, *, name):
    n = len(mats)
    halves = [pltpu.with_memory_space_constraint(m.reshape(N_SHARDS, 2, m.shape[1] // 2, m.shape[2]), pltpu.HBM)
              for m in mats]

    def body(*refs):
        send, recv = refs[n + 1], refs[n + 2]
        out, token = refs[n + 3:2 * n + 3], refs[2 * n + 3]
        x, y, c, s_me, others = _place()
        for l in range(n):
            for k in range(3):
                ox, oy = others[k]
                pltpu.make_async_remote_copy(out[l].at[s_me, c], out[l].at[s_me, c], send.at[3 * l + k],
                                             recv.at[3 * l + k], device_id=(ox, oy, c), device_id_type=MESH).start()
        token[...] = jnp.zeros_like(token)

    res = _split_call(
        body, name=name,
        out_shape=(pltpu.SemaphoreType.DMA((3 * n,)), pltpu.SemaphoreType.DMA((3 * n,)),
                   *[pltpu.HBM(h.shape, BF16) for h in halves], _token_shape()),
        in_specs=[HBM] * n + [ANY], out_specs=(SEM, SEM, *[HBM] * n, pl.BlockSpec(memory_space=pltpu.VMEM)),
        aliases={l: 2 + l for l in range(n)})(*halves, after)
    return res[0], res[1], list(res[2:2 + n]), res[2 + n]


def gather_pass_on(bufs, send_a, recv_a, after, *, name, base=0):
    n = len(bufs)

    def body(*refs):
        send_a, recv_a = refs[n], refs[n + 1]
        out = refs[n + 3:2 * n + 3]
        send_b, recv_b, token = refs[2 * n + 3:]
        x, y, c, s_me, others = _place()
        for l in range(n):
            for k in range(3):
                ox, oy = others[k]
                landed, i = out[l].at[2 * ox + oy, c], 3 * l + k
                pltpu.make_async_remote_copy(landed, landed, send_a.at[3 * base + i], recv_a.at[3 * base + i],
                                             device_id=(x, y, 1 - c), device_id_type=MESH).wait_recv()
                pltpu.make_async_remote_copy(landed, landed, send_b.at[i], recv_b.at[i],
                                             device_id=(x, y, 1 - c), device_id_type=MESH).start()
        for l in range(n):
            for k in range(3):
                mine, i = out[l].at[s_me, c], 3 * (base + l) + k
                pltpu.make_async_remote_copy(mine, mine, send_a.at[i], recv_a.at[i],
                                             device_id=(x, y, 1 - c), device_id_type=MESH).wait_send()
        token[...] = jnp.zeros_like(token)

    res = _split_call(
        body, name=name,
        out_shape=(*[pltpu.HBM(b.shape, BF16) for b in bufs], pltpu.SemaphoreType.DMA((3 * n,)),
                   pltpu.SemaphoreType.DMA((3 * n,)), _token_shape()),
        in_specs=[HBM] * n + [SEM, SEM, ANY],
        out_specs=(*[HBM] * n, SEM, SEM, pl.BlockSpec(memory_space=pltpu.VMEM)),
        aliases={l: l for l in range(n)})(*bufs, send_a, recv_a, after)
    return list(res[:n]), res[n], res[n + 1], res[n + 2]


def gather_finish(bufs, send_b, recv_b, after, shapes, *, name):
    n = len(bufs)

    def body(*refs):
        send_b, recv_b = refs[n], refs[n + 1]
        out = refs[n + 3:]
        x, y, c, _, others = _place()
        for l in range(n):
            for k in range(3):
                ox, oy = others[k]
                theirs, mine, i = out[l].at[2 * ox + oy, 1 - c], out[l].at[2 * ox + oy, c], 3 * l + k
                pltpu.make_async_remote_copy(theirs, theirs, send_b.at[i], recv_b.at[i],
                                             device_id=(x, y, 1 - c), device_id_type=MESH).wait_recv()
                pltpu.make_async_remote_copy(mine, mine, send_b.at[i], recv_b.at[i],
                                             device_id=(x, y, 1 - c), device_id_type=MESH).wait_send()

    res = _split_call(
        body, name=name, out_shape=tuple(pltpu.HBM(b.shape, BF16) for b in bufs),
        in_specs=[HBM] * n + [SEM, SEM, ANY], out_specs=tuple([HBM] * n),
        aliases={l: l for l in range(n)})(*bufs, send_b, recv_b, after)
    return [r.reshape(s) for r, s in zip(res, shapes)]


def exchange_start(srcs, dst_shapes, dst_dtype, plan, count, after, *, name):
    n, m = len(srcs), len(dst_shapes)
    srcs = [pltpu.with_memory_space_constraint(s, pltpu.HBM) for s in srcs]
    lands = [pltpu.with_memory_space_constraint(lax.empty(s, dst_dtype), pltpu.HBM) for s in dst_shapes]

    def body(*refs):
        send, recv = refs[n + m + 1], refs[n + m + 2]
        src, dst, token = refs[n + m + 3:2 * n + m + 3], refs[2 * n + m + 3:2 * (n + m) + 3], refs[2 * (n + m) + 3]
        for i, (s, d, dev) in enumerate(plan(_place(), src, dst)):
            pltpu.make_async_remote_copy(s, d, send.at[i], recv.at[i], device_id=dev, device_id_type=MESH).start()
        token[...] = jnp.zeros_like(token)

    res = _split_call(
        body, name=name,
        out_shape=(pltpu.SemaphoreType.DMA((count,)), pltpu.SemaphoreType.DMA((count,)),
                   *[pltpu.HBM(s.shape, s.dtype) for s in srcs], *[pltpu.HBM(s, dst_dtype) for s in dst_shapes],
                   _token_shape()),
        in_specs=[HBM] * (n + m) + [ANY],
        out_specs=(SEM, SEM, *[HBM] * (n + m), pl.BlockSpec(memory_space=pltpu.VMEM)),
        aliases={i: 2 + i for i in range(n + m)})(*srcs, *lands, after)
    return (list(res[2:2 + n]), list(res[2 + n:2 + n + m]), res[0], res[1], plan), res[2 + n + m]


def exchange_finish(state, after, *, name):
    srcs, lands, send, recv, plan = state
    n, m = len(srcs), len(lands)

    def body(*refs):
        send, recv = refs[n + m], refs[n + m + 1]
        src, dst = refs[n + m + 3:2 * n + m + 3], refs[2 * n + m + 3:]
        for i, (s, d, dev) in enumerate(plan(_place(), src, dst)):
            pltpu.make_async_remote_copy(s, d, send.at[i], recv.at[i], device_id=dev, device_id_type=MESH).wait()

    res = _split_call(
        body, name=name,
        out_shape=tuple(pltpu.HBM(a.shape, a.dtype) for a in srcs + lands),
        in_specs=[HBM] * (n + m) + [SEM, SEM, ANY], out_specs=tuple([HBM] * (n + m)),
        aliases={i: i for i in range(n + m)})(*srcs, *lands, send, recv, after)
    return list(res[:n]), list(res[n:])


def pair_plan(place, src, dst):
    x, y, c, _, _ = place
    return [(s.at[:, 1 - c], d, (x, y, 1 - c)) for s, d in zip(src, dst)]


def chip_plan(place, src, dst):
    x, y, c, _, others = place
    return [(s.at[2 * ox + oy], d.at[k], (ox, oy, c)) for s, d in zip(src, dst) for k, (ox, oy) in enumerate(others)]


def pair_exchange(grads, *, name):
    n = len(grads)

    def body(*refs):
        src, got = refs[:n], refs[n:2 * n]
        send, recv = refs[2 * n:]
        x, y, c, _, _ = _place()

        def swap(l):
            return pltpu.make_async_remote_copy(src[l].at[:, 1 - c], got[l], send.at[l], recv.at[l],
                                                device_id=(x, y, 1 - c), device_id_type=MESH)

        for l in range(n):
            swap(l).start()
        for l in range(n):
            swap(l).wait()

    res = _pcall(body, name=name, out_shape=[_sds((N_SHARDS,) + g.shape[2:], F32) for g in grads],
                 in_specs=[ANY] * n, out_specs=[ANY] * n,
                 scratch_shapes=[pltpu.SemaphoreType.DMA((n,)), pltpu.SemaphoreType.DMA((n,))],
                 side_effects=True)(*grads)
    return list(res)


def add_to_wire(mine, theirs, core, *, name, tm=512):
    s, _, r, c = mine.shape
    tm = min(tm, r)

    def body(core_ref, a_ref, b_ref, o_ref):
        o_ref[...] = (a_ref[...] + b_ref[...]).astype(BF16)

    spec = pl.BlockSpec((None, tm, c), lambda i, j, cr: (i, j, 0))
    return _pcall(body, name=name, out_shape=_sds((s, r, c), BF16), grid=(s, r // tm), num_prefetch=1,
                  in_specs=[pl.BlockSpec((None, None, tm, c), lambda i, j, cr: (i, cr[0], j, 0)), spec],
                  out_specs=spec, semantics=("parallel", "parallel"))(core, mine, theirs)


def sum_chips(wire, landed, place, dest, layer, n_layers, *, name, tm=512):
    _, r, c = wire.shape
    tm = min(tm, r)

    def body(place_ref, w_ref, l_ref, *rest):
        o_ref = rest[-1]
        o_ref[...] = ((w_ref[...].astype(F32) + l_ref[0].astype(F32)) + l_ref[1].astype(F32)) + l_ref[2].astype(F32)

    in_specs = [pl.BlockSpec((None, tm, c), lambda i, pr: (pr[0], i, 0)),
                pl.BlockSpec((3, tm, c), lambda i, pr: (0, i, 0))]
    args = [place, wire, landed]
    aliases = None
    if dest is not None:
        in_specs.append(ANY)
        args.append(dest)
        aliases = {3: 0}
    return _pcall(body, name=name, out_shape=_sds((n_layers, 2, r, c), F32), grid=(r // tm,), num_prefetch=1,
                  in_specs=in_specs,
                  out_specs=pl.BlockSpec((None, None, tm, c), lambda i, pr: (layer, pr[1], i, 0)),
                  aliases=aliases, semantics=("parallel",))(*args)


def pair_share(bufs, slots, *, name):
    n = len(bufs)

    def body(*refs):
        out = refs[n:2 * n]
        send, recv = refs[2 * n:]
        x, y, c, _, _ = _place()

        def share(i, half):
            o, l = slots[i]
            return pltpu.make_async_remote_copy(out[o].at[l, half], out[o].at[l, half], send.at[i], recv.at[i],
                                                device_id=(x, y, 1 - c), device_id_type=MESH)

        for i in range(len(slots)):
            share(i, c).start()
        for i in range(len(slots)):
            share(i, 1 - c).wait_recv()
            share(i, c).wait_send()

    res = _pcall(body, name=name, out_shape=[_sds(b.shape, F32) for b in bufs], in_specs=[ANY] * n,
                 out_specs=[ANY] * n,
                 scratch_shapes=[pltpu.SemaphoreType.DMA((len(slots),)), pltpu.SemaphoreType.DMA((len(slots),))],
                 aliases={o: o for o in range(n)}, side_effects=True)(*bufs)
    return list(res)


def all_reduce_small(packed, *, name):
    n_dev, r, c = packed.shape

    def body(in_ref, out_ref, land, send, recv):
        x, y, cc, _, _ = _place()
        me = 4 * x + 2 * y + cc
        peers = [(px, py, pc) for px in range(2) for py in range(2) for pc in range(2)]

        def scatter(d):
            return pltpu.make_async_remote_copy(in_ref.at[d], land.at[me], send.at[0, d], recv.at[0, me],
                                                device_id=peers[d], device_id_type=MESH)

        def gather(d):
            return pltpu.make_async_remote_copy(out_ref.at[me], out_ref.at[me], send.at[1, d], recv.at[1, me],
                                                device_id=peers[d], device_id_type=MESH)

        for d in range(n_dev):
            @pl.when(d != me)
            def _():
                scatter(d).start()
        land[me] = in_ref[me]
        for d in range(n_dev):
            @pl.when(d != me)
            def _():
                pltpu.make_async_remote_copy(in_ref.at[d], land.at[d], send.at[0, d], recv.at[0, d],
                                             device_id=peers[d], device_id_type=MESH).wait_recv()
        total = land[0]
        for d in range(1, n_dev):
            total = total + land[d]
        out_ref[me] = total
        for d in range(n_dev):
            @pl.when(d != me)
            def _():
                gather(d).start()
        for d in range(n_dev):
            @pl.when(d != me)
            def _():
                pltpu.make_async_remote_copy(out_ref.at[d], out_ref.at[d], send.at[1, d], recv.at[1, d],
                                             device_id=peers[d], device_id_type=MESH).wait_recv()
        for d in range(n_dev):
            @pl.when(d != me)
            def _():
                scatter(d).wait_send()
                gather(d).wait_send()

    vm = pl.BlockSpec(memory_space=pltpu.VMEM)
    return _pcall(body, name=name, out_shape=_sds(packed.shape, F32), in_specs=[vm], out_specs=vm,
                  scratch_shapes=[pltpu.VMEM(packed.shape, F32), pltpu.SemaphoreType.DMA((2, n_dev)),
                                  pltpu.SemaphoreType.DMA((2, n_dev))],
                  side_effects=True)(packed)


def adamw(w, g, m, v, *, name, part=None, dest=None, tm=512):
    shape = w.shape
    cols = shape[-1]
    rows = 1
    for s in shape[:-1]:
        rows *= s
    first, count = 0, rows
    if part is not None:
        count = rows // part[1]
        first = part[0] * count
    tm = min(tm, count)
    assert count % tm == 0
    two_d = lambda a: a.reshape(rows, cols)

    def body(w_ref, g_ref, m_ref, v_ref, *rest):
        d_ref, mo_ref, vo_ref = rest[-3:]
        gv = g_ref[...]
        m_new = ADAM_B1 * m_ref[...] + (1.0 - ADAM_B1) * gv
        v_new = ADAM_B2 * v_ref[...] + (1.0 - ADAM_B2) * (gv * gv)
        m_hat = m_new / (1.0 - ADAM_B1 ** ADAM_STEP)
        v_hat = v_new / (1.0 - ADAM_B2 ** ADAM_STEP)
        d_ref[...] = -ADAM_LR * (m_hat / (jnp.sqrt(v_hat) + ADAM_EPS) + ADAM_WD * w_ref[...])
        mo_ref[...] = m_new
        vo_ref[...] = v_new

    spec = pl.BlockSpec((tm, cols), lambda i: (first // tm + i, 0))
    args = [two_d(w), two_d(g), two_d(m), two_d(v)]
    in_specs = [spec] * 4
    aliases = None
    if dest is not None:
        args += [two_d(d) for d in dest]
        in_specs = in_specs + [ANY] * 3
        aliases = {4: 0, 5: 1, 6: 2}
    outs = _pcall(body, name=name, out_shape=[_sds((rows, cols), F32)] * 3, grid=(count // tm,), in_specs=in_specs,
                  out_specs=[spec] * 3, aliases=aliases, semantics=("parallel",))(*args)
    return [o.reshape(shape) for o in outs]


WEIGHTS = ("ln_mix_a", "w_in_a", "g_v_a", "w_spatial", "b_spatial", "w_out_a", "ln_kv", "w_kv", "g_k", "ln_mix_b",
           "w_q", "g_q", "w_out_b", "ln_mlp", "w_up", "w_down", "ln_ple", "w_ple_gate", "w_ple_proj")
MATRICES = (("w_in_a", 1, True), ("w_out_a", 1, False), ("w_kv", 0, True), ("w_q", 1, False), ("w_out_b", 1, False),
            ("w_up", 2, True), ("w_down", 2, False), ("w_ple_gate", 2, False), ("w_ple_proj", 2, True))
GATHER_STAGES = ((("w_in_a", 0),), (("w_out_a", 0),), (("w_up", 0),), (("w_down", 0),),
                 (("w_ple_gate", 0), ("w_ple_proj", 0), ("w_kv", 0)), (("w_q", 0),),
                 (("w_out_b", 0), ("w_up", 1), ("w_down", 1), ("w_ple_gate", 1), ("w_ple_proj", 1)))
REPLICATED = ("w_spatial", "b_spatial", "ln_kv", "g_k", "ln_mix_b", "g_q", "ln_mlp", "ln_ple")
SHARDED_VECTORS = ("ln_mix_a", "g_v_a")
SMALL_ROWS = 18


def kernel(x, p, ln_mix_a, w_in_a, g_v_a, w_spatial, b_spatial, w_out_a, ln_kv, w_kv, g_k, ln_mix_b, w_q, g_q, w_out_b, ln_mlp, w_up, w_down, ln_ple, w_ple_gate, w_ple_proj, loss_target, m_ln_mix_a, m_w_in_a, m_g_v_a, m_w_spatial, m_b_spatial, m_w_out_a, m_ln_kv, m_w_kv, m_g_k, m_ln_mix_b, m_w_q, m_g_q, m_w_out_b, m_ln_mlp, m_w_up, m_w_down, m_ln_ple, m_w_ple_gate, m_w_ple_proj, v_ln_mix_a, v_w_in_a, v_g_v_a, v_w_spatial, v_b_spatial, v_w_out_a, v_ln_kv, v_w_kv, v_g_k, v_ln_mix_b, v_w_q, v_g_q, v_w_out_b, v_ln_mlp, v_w_up, v_w_down, v_ln_ple, v_w_ple_gate, v_w_ple_proj):
    given = dict(locals())
    weights = {n: given[n] for n in WEIGHTS}
    shard = 2 * lax.axis_index("x") + lax.axis_index("y")
    core = lax.axis_index("c")
    shard_1 = shard.astype(jnp.int32).reshape(1)
    core_1 = core.astype(jnp.int32).reshape(1)
    place = jnp.stack([shard, core]).astype(jnp.int32)

    col_sharded = {name: cols for name, _, cols in MATRICES}
    layer_count = {name: max(layers, 1) for name, layers, _ in MATRICES}

    def cast(key, after):
        name, layer = key
        w3 = weights[name] if weights[name].ndim == 3 else weights[name][None]
        return (name, layer, col_sharded[name],
                cast_into_slot(w3, layer, shard_1, name=f"cast_{name}_{layer}", after=after))

    head = [cast(key, None) for key in GATHER_STAGES[0]]
    send_h, recv_h, flying_h, token_h = gather_start([lf[3] for lf in head], shard_1, name="gather_start_0")
    tail = [cast(key, token_h) for stage in GATHER_STAGES[1:] for key in stage]
    _, vec_a = gather_shards([], [ln_mix_a, g_v_a], name="gather_vectors")
    send_a, recv_a, flying, token = gather_start([lf[3] for lf in tail], vec_a[0], name="gather_start_1")

    w = {"ln_mix_a": vec_a[0].reshape(1, D_MODEL) + token[0, 0],
         "g_v_a": vec_a[1].reshape(1, D_MODEL)}
    for name in REPLICATED:
        w[name] = weights[name]

    class Late:
        def weights(self, name, layer, after):
            stage = [(name, layer) in s for s in GATHER_STAGES].index(True)
            if stage == 0:
                base, members, sems, fly = 0, head, (send_h, recv_h), flying_h
            else:
                base = sum(len(s) for s in GATHER_STAGES[1:stage])
                members, sems, fly = tail[base:base + len(GATHER_STAGES[stage])], (send_a, recv_a), flying
            bufs, send_b, recv_b, tok = gather_pass_on(fly[base:base + len(members)], sems[0], sems[1], after,
                                                       name=f"gather_pass_on_{stage}", base=base)
            got = gather_finish(bufs, send_b, recv_b, tok, [lf[3].shape for lf in members],
                                name=f"gather_finish_{stage}")
            out = {}
            for (leaf_name, leaf_layer, cols, _), arr in zip(members, got):
                out[(leaf_name, leaf_layer)] = arr if cols else arr.reshape(N_SHARDS * arr.shape[1], arr.shape[2])
            return out

        groups = []

        def pair_start(self, grads_done, after):
            self.keys = sorted(grads_done)
            views = [view(k, grads_done[k]) for k in self.keys]
            self.pair, token = exchange_start(views, [(N_SHARDS,) + v.shape[2:] for v in views], F32, pair_plan,
                                              len(views), after, name=f"grad_pair_start_{len(self.groups)}")
            return token

        def chip_start(self, after):
            tag = len(self.groups)
            mine, theirs = exchange_finish(self.pair, after, name=f"grad_pair_finish_{tag}")
            wire = [add_to_wire(a, b, core_1, name=f"grad_pair_sum_{tag}_{i}")
                    for i, (a, b) in enumerate(zip(mine, theirs))]
            chip, token = exchange_start(wire, [(3,) + v.shape[1:] for v in wire], BF16, chip_plan, 3 * len(wire),
                                         wire[-1], name=f"grad_chip_start_{tag}")
            self.groups.append((self.keys, chip))
            return token

    def view(key, arr):
        rows = arr.shape[-2] if col_sharded[key[0]] else arr.shape[0] // N_SHARDS
        return arr.reshape(N_SHARDS, 2, rows // 2, arr.shape[-1])

    t = x.shape[1]
    late = Late()
    loss_blk, dx, g = local_step(x[0], p.reshape(2, t, PLE_DIM), loss_target[0], w, late)
    loss = lax.psum(loss_blk[0, 0], ("x", "y", "c"))

    sent = {k for keys, _ in late.groups for k in keys}
    keys_last = [(name, layer) for name, layers, _ in MATRICES for layer in range(max(layers, 1))
                 if (name, layer) not in sent]
    views = [view(k, g[k[0]][k[1]] if layer_count[k[0]] == 2 else g[k[0]]) for k in keys_last]

    theirs = pair_exchange(views, name="grad_pair_exchange_last")
    wire_0 = [add_to_wire(a, b, core_1, name=f"grad_pair_sum_last_{i}") for i, (a, b) in enumerate(zip(views, theirs))]
    chip_0, token_0 = exchange_start(wire_0, [(3,) + v.shape[1:] for v in wire_0], BF16, chip_plan, 3 * len(wire_0),
                                     wire_0[-1], name="grad_chip_start_last")

    grads, bufs = {}, {}

    def sum_and_share(keys, wire, landed, tag):
        for i, (key, wv, lv) in enumerate(zip(keys, wire, landed)):
            name, layer = key
            bufs[name] = sum_chips(wv, lv, place, bufs.get(name), layer, layer_count[name],
                                   name=f"grad_chip_sum_{tag}_{i}")
        names = sorted({k[0] for k in keys})
        shared = pair_share([bufs[n] for n in names], [(names.index(k[0]), k[1]) for k in keys],
                            name=f"grad_pair_share_{tag}")
        bufs.update(zip(names, shared))

    updates = {}

    def update(n, gn, part=None):
        wn, mn, vn = weights[n], given["m_" + n], given["v_" + n]
        if wn.ndim == 1:
            wn, gn, mn, vn = (a.reshape(1, -1) for a in (wn, gn, mn, vn))
        tag = "" if part is None else f"_{part[0]}"
        updates[n] = adamw(wn, gn.reshape(wn.shape), mn, vn, name=f"adamw_{n}{tag}", part=part, dest=updates.get(n))

    after = token_0
    for tag, (keys, chip) in enumerate(late.groups + [(keys_last, chip_0)]):
        wire, landed = exchange_finish(chip, after, name=f"grad_chip_finish_{tag}")
        sum_and_share(keys, wire, landed, tag)
        for name, layer in keys:
            update(name, bufs[name], (layer, layer_count[name]) if layer_count[name] == 2 else None)
        after = updates[keys[-1][0]][0]

    small = REPLICATED + SHARDED_VECTORS
    flat = jnp.concatenate([g[n].reshape(-1) for n in small])
    room = 8 * SMALL_ROWS * D_MODEL
    flat = jnp.concatenate([flat, jnp.zeros((room - flat.shape[0],), F32)])
    flat, _ = lax.optimization_barrier((flat, after))
    reduced = all_reduce_small(flat.reshape(8, SMALL_ROWS, D_MODEL), name="grad_small_all_reduce").reshape(-1)
    at = 0
    for n in small:
        size = g[n].size
        piece = reduced[at:at + size]
        at += size
        if n in SHARDED_VECTORS:
            per = D_MODEL // N_SHARDS
            grads[n] = lax.dynamic_slice(piece, (shard * per,), (per,)).reshape(weights[n].shape)
        else:
            grads[n] = piece.reshape(weights[n].shape)
        update(n, grads[n])
    for name, _, _ in MATRICES:
        grads[name] = bufs[name].reshape(weights[name].shape)
    delta = {n: updates[n][0].reshape(weights[n].shape) for n in WEIGHTS}
    new_m = {n: updates[n][1].reshape(weights[n].shape) for n in WEIGHTS}
    new_v = {n: updates[n][2].reshape(weights[n].shape) for n in WEIGHTS}
    return (loss, dx.reshape(x.shape), *[grads[n] for n in WEIGHTS], *[delta[n] for n in WEIGHTS],
            *[new_m[n] for n in WEIGHTS], *[new_v[n] for n in WEIGHTS])
```

```python
import jax
import jax.numpy as jnp
from jax import lax
from jax.experimental import pallas as pl
from jax.experimental.pallas import tpu as pltpu

F32 = jnp.float32
BF16 = jnp.bfloat16

D_MODEL = 1024
D_FF = 4096
PLE_DIM = 256
N_GROUPS = 8
CHUNK = 128
HEAD_DIM = 64
LANES = 128
ATT_K_BLOCK = 256
ATT_Q_BLOCK = 512
EPS = 1e-6
N_SHARDS = 4
VMEM_LIMIT = 56 * 1024 * 1024

ADAM_LR = 0.001
ADAM_B1 = 0.9
ADAM_B2 = 0.999
ADAM_EPS = 1e-08
ADAM_WD = 0.01
ADAM_STEP = 10

MESH = pl.DeviceIdType.MESH


def _pcall(body, *, name, out_shape, grid=None, in_specs=None, out_specs=None, scratch_shapes=(),
           semantics=None, aliases=None, side_effects=False, num_prefetch=0):
    params = dict(vmem_limit_bytes=VMEM_LIMIT)
    if semantics is not None:
        params["dimension_semantics"] = semantics
    if side_effects:
        params["has_side_effects"] = True
    kwargs = {}
    if aliases:
        kwargs["input_output_aliases"] = aliases
    if num_prefetch:
        spec = pltpu.PrefetchScalarGridSpec(num_scalar_prefetch=num_prefetch, grid=grid, in_specs=in_specs,
                                            out_specs=out_specs, scratch_shapes=list(scratch_shapes))
        return pl.pallas_call(body, name=name, out_shape=out_shape, grid_spec=spec,
                              compiler_params=pltpu.CompilerParams(**params), **kwargs)
    if grid is not None:
        kwargs["grid"] = grid
    if in_specs is not None:
        kwargs["in_specs"] = in_specs
    if out_specs is not None:
        kwargs["out_specs"] = out_specs
    if aliases:
        kwargs["input_output_aliases"] = aliases
    return pl.pallas_call(body, name=name, out_shape=out_shape, scratch_shapes=list(scratch_shapes),
                          compiler_params=pltpu.CompilerParams(**params), **kwargs)


def _sds(shape, dtype):
    return jax.ShapeDtypeStruct(shape, dtype)


_GELU_C = 0.7978845608028654
_GELU_A = 0.044715


def _gelu(x):
    inner = _GELU_C * (x + _GELU_A * (x * x * x))
    return 0.5 * x * (1.0 + jnp.tanh(inner))


def _gelu_grad(x):
    x2 = x * x
    t = jnp.tanh(_GELU_C * (x + _GELU_A * (x2 * x)))
    return 0.5 * (1.0 + t) + 0.5 * x * (1.0 - t * t) * (_GELU_C * (1.0 + 3.0 * _GELU_A * x2))


def _sigmoid(x):
    return 1.0 / (1.0 + jnp.exp(-x))


def _log_sigmoid(z):
    return jnp.minimum(z, 0.0) - jnp.log(1.0 + jnp.exp(-jnp.abs(z)))


def _dot(a, b):
    return jnp.dot(a, b, preferred_element_type=F32)


def _dot_nt(a, b):
    return lax.dot_general(a, b, (((1,), (1,)), ((), ())), preferred_element_type=F32)


def _dot_tn(a, b):
    return lax.dot_general(a, b, (((0,), (0,)), ((), ())), preferred_element_type=F32)


def _head_rstd(x):
    lane = lax.broadcasted_iota(jnp.int32, x.shape, 1)
    low = lane < HEAD_DIM
    sq = x * x
    s_lo = jnp.sum(jnp.where(low, sq, 0.0), axis=-1, keepdims=True)
    s_hi = jnp.sum(jnp.where(low, 0.0, sq), axis=-1, keepdims=True)
    ms = jnp.where(low, s_lo, s_hi) * (1.0 / HEAD_DIM)
    return lax.rsqrt(ms + EPS)


def _head_mean(x):
    lane = lax.broadcasted_iota(jnp.int32, x.shape, 1)
    low = lane < HEAD_DIM
    s_lo = jnp.sum(jnp.where(low, x, 0.0), axis=-1, keepdims=True)
    s_hi = jnp.sum(jnp.where(low, 0.0, x), axis=-1, keepdims=True)
    return jnp.where(low, s_lo, s_hi) * (1.0 / HEAD_DIM)


def _full(shape):
    zeros = (0,) * len(shape)
    return pl.BlockSpec(shape, lambda i: zeros)


def norm_matmul(x, g, w, *, name, epilogue="none", tm=512):
    t, d = x.shape
    sharded = w.ndim == 3
    per = w.shape[2] if sharded else w.shape[1]
    n = N_SHARDS * per if sharded else per
    tm = min(tm, t)

    def body(x_ref, g_ref, w_ref, h_ref, r_ref, *outs):
        xv = x_ref[...]
        r = lax.rsqrt(jnp.mean(xv * xv, axis=-1, keepdims=True) + EPS)
        h = ((xv * r) * g_ref[...]).astype(BF16)
        h_ref[...] = h
        r_ref[...] = r
        for s in range(N_SHARDS if sharded else 1):
            cols = slice(s * per, (s + 1) * per)
            y = _dot(h, w_ref[s] if sharded else w_ref[...])
            if epilogue == "none":
                outs[0][:, cols] = y
            else:
                a = jnp.maximum(y, 0.0)
                outs[0][:, cols] = a.astype(BF16)
                outs[1][:, cols] = (a * a).astype(BF16)

    row = lambda i: (i, 0)
    out_shape = [_sds((t, d), BF16), _sds((t, 1), F32)]
    out_specs = [pl.BlockSpec((tm, d), row), pl.BlockSpec((tm, 1), row)]
    if epilogue == "none":
        out_shape.append(_sds((t, n), F32))
        out_specs.append(pl.BlockSpec((tm, n), row))
    else:
        out_shape += [_sds((t, n), BF16), _sds((t, n), BF16)]
        out_specs += [pl.BlockSpec((tm, n), row)] * 2
    return _pcall(
        body, name=name, out_shape=out_shape, grid=(t // tm,),
        in_specs=[pl.BlockSpec((tm, d), row), _full((1, d)), _full(w.shape)],
        out_specs=out_specs, semantics=("parallel",))(x, g, w)


def matmul_residual(a, w, res, *, name, tm=512):
    t, k = a.shape
    n = w.shape[1]
    tm = min(tm, t)

    def body(a_ref, w_ref, res_ref, o_ref):
        o_ref[...] = res_ref[...] + _dot(a_ref[...], w_ref[...])

    row = lambda i: (i, 0)
    return _pcall(
        body, name=name, out_shape=_sds((t, n), F32), grid=(t // tm,),
        in_specs=[pl.BlockSpec((tm, k), row), _full(w.shape), pl.BlockSpec((tm, n), row)],
        out_specs=pl.BlockSpec((tm, n), row), semantics=("parallel",))(a, w, res)


def ple_forward(x, g, w_gate, p, w_proj, *, name, tm=256):
    t, d = x.shape
    tm = min(tm, t)

    def body(x_ref, g_ref, wg_ref, p_ref, wp_ref, h_ref, r_ref, gate_ref, pp_ref, o_ref):
        xv = x_ref[...]
        r = lax.rsqrt(jnp.mean(xv * xv, axis=-1, keepdims=True) + EPS)
        h = ((xv * r) * g_ref[...]).astype(BF16)
        h_ref[...] = h
        r_ref[...] = r
        gate = _sigmoid(_dot(h, wg_ref[...]))
        gate_ref[...] = gate
        pb = p_ref[...].astype(BF16)
        per = d // N_SHARDS
        for s in range(N_SHARDS):
            cols = slice(s * per, (s + 1) * per)
            pp = _dot(pb, wp_ref[s])
            pp_ref[:, cols] = pp.astype(BF16)
            o_ref[:, cols] = xv[:, cols] + pp * gate[:, cols]

    row = lambda i: (i, 0)
    fixed = lambda i: (0, 0)
    return _pcall(
        body, name=name,
        out_shape=[_sds((t, d), BF16), _sds((t, 1), F32), _sds((t, d), F32), _sds((t, d), BF16), _sds((t, d), F32)],
        grid=(t // tm,),
        in_specs=[pl.BlockSpec((tm, d), row), pl.BlockSpec((1, d), fixed), pl.BlockSpec((d, d), fixed),
                  pl.BlockSpec((tm, PLE_DIM), row),
                  pl.BlockSpec((N_SHARDS, PLE_DIM, d // N_SHARDS), lambda i: (0, 0, 0))],
        out_specs=[pl.BlockSpec((tm, d), row), pl.BlockSpec((tm, 1), row), pl.BlockSpec((tm, d), row),
                   pl.BlockSpec((tm, d), row), pl.BlockSpec((tm, d), row)],
        semantics=("parallel",))(x, g, w_gate, p, w_proj)


def _tril_mask():
    r = lax.broadcasted_iota(jnp.int32, (CHUNK, CHUNK), 0)
    c = lax.broadcasted_iota(jnp.int32, (CHUNK, CHUNK), 1)
    return c <= r


def _sgu_common(pre_ref, gv_ref, ws_ref):
    pre = pre_ref[...]
    pre_u, pre_v = pre[:, :D_MODEL], pre[:, D_MODEL:]
    u = _gelu(pre_u)
    v = _gelu(pre_v)
    r = lax.rsqrt(jnp.mean(v * v, axis=-1, keepdims=True) + EPS)
    vhat = v * r
    vn = (vhat * gv_ref[...]).astype(BF16)
    tril = _tril_mask()
    wm = [jnp.where(tril, ws_ref[g], 0.0).astype(BF16) for g in range(N_GROUPS)]
    return pre_u, pre_v, u, r, vhat, vn, wm, tril


def sgu_forward(pre, g_v, w_s, b_full, *, name):
    t = pre.shape[0]

    def body(pre_ref, gv_ref, ws_ref, b_ref, y_ref):
        _, _, u, _, _, vn, wm, _ = _sgu_common(pre_ref, gv_ref, ws_ref)
        for g in range(N_GROUPS):
            cols = slice(g * LANES, (g + 1) * LANES)
            mix = _dot(wm[g], vn[:, cols]) + b_ref[:, cols]
            y_ref[:, cols] = (u[:, cols] * mix).astype(BF16)

    return _pcall(
        body, name=name, out_shape=_sds((t, D_MODEL), BF16), grid=(t // CHUNK,),
        in_specs=[pl.BlockSpec((CHUNK, 2 * D_MODEL), lambda i: (i, 0)), pl.BlockSpec((1, D_MODEL), lambda i: (0, 0)),
                  pl.BlockSpec((N_GROUPS, CHUNK, CHUNK), lambda i: (0, 0, 0)),
                  pl.BlockSpec((CHUNK, D_MODEL), lambda i: (0, 0))],
        out_specs=pl.BlockSpec((CHUNK, D_MODEL), lambda i: (i, 0)),
        semantics=("parallel",))(pre, g_v, w_s, b_full)


def head_norm(pre, g128, *, name, col_block=0, scale=1.0, passthrough=False, tm=512):
    t = pre.shape[0]
    tm = min(tm, t)

    def body(*refs):
        if passthrough:
            x_ref, v_ref, g_ref, o_ref, vo_ref = refs
            vo_ref[...] = v_ref[...].astype(BF16)
        else:
            x_ref, g_ref, o_ref = refs
        g = g_ref[...] * scale
        for b in range(D_MODEL // LANES):
            cols = slice(b * LANES, (b + 1) * LANES)
            xv = x_ref[:, cols]
            o_ref[:, cols] = ((xv * _head_rstd(xv)) * g).astype(BF16)

    x_spec = pl.BlockSpec((tm, D_MODEL), lambda i: (i, col_block))
    g_spec = pl.BlockSpec((1, LANES), lambda i: (0, 0))
    o_spec = pl.BlockSpec((tm, D_MODEL), lambda i: (i, 0))
    if passthrough:
        return _pcall(body, name=name, out_shape=[_sds((t, D_MODEL), BF16)] * 2, grid=(t // tm,),
                      in_specs=[x_spec, pl.BlockSpec((tm, D_MODEL), lambda i: (i, 1)), g_spec],
                      out_specs=[o_spec, o_spec], semantics=("parallel",))(pre, pre, g128)
    return _pcall(body, name=name, out_shape=_sds((t, D_MODEL), BF16), grid=(t // tm,),
                  in_specs=[x_spec, g_spec], out_specs=o_spec, semantics=("parallel",))(pre, g128)


def _suffix_matrix(n):
    r = lax.broadcasted_iota(jnp.int32, (n, n), 0)
    c = lax.broadcasted_iota(jnp.int32, (n, n), 1)
    return jnp.where(r > c, 1.0, 0.0).astype(BF16)


def _prefix_matrix(n):
    r = lax.broadcasted_iota(jnp.int32, (n, n), 0)
    c = lax.broadcasted_iota(jnp.int32, (n, n), 1)
    return jnp.where(r < c, 1.0, 0.0).astype(BF16)


def _block_cumsum(a, tri):
    return _dot(a.astype(BF16), tri)


def _stacked_causal(nq, nk, shift):
    r = lax.broadcasted_iota(jnp.int32, (2 * nq, nk), 0)
    c = lax.broadcasted_iota(jnp.int32, (2 * nq, nk), 1)
    return c + shift < jnp.where(r >= nq, r - nq, r)


def _att_blocks(t):
    bq, bk = min(ATT_Q_BLOCK, t), min(ATT_K_BLOCK, t)
    return bq, bk, bq // bk


def _stack_heads(a, low):
    zero = jnp.zeros_like(a)
    return jnp.concatenate([jnp.where(low, a, zero), jnp.where(low, zero, a)], axis=0)


def stick_breaking_forward(q, k, v, *, name):
    t = q.shape[0]
    bq, bk, ratio = _att_blocks(t)

    def body(q_ref, k_ref, v_ref, o_ref):
        i = pl.program_id(1)
        low = lax.broadcasted_iota(jnp.int32, (bq, LANES), 1) < HEAD_DIM
        tri = _suffix_matrix(bk)
        qs = _stack_heads(q_ref[...], low)

        def block(j, carry, acc, causal=None):
            rows = pl.ds(pl.multiple_of(j * bk, bk), bk)
            z = _dot_nt(qs, k_ref[rows, :])
            ls = _log_sigmoid(z)
            lg = ls - z
            if causal is not None:
                lg = jnp.where(causal, lg, 0.0)
            s = ls + _block_cumsum(lg, tri) + carry
            a = jnp.exp(s)
            if causal is not None:
                a = jnp.where(causal, a, 0.0)
            acc = acc + _dot(a.astype(BF16), v_ref[rows, :])
            return carry + jnp.sum(lg, axis=-1, keepdims=True), acc

        state = (jnp.zeros((2 * bq, 1), F32), jnp.zeros((2 * bq, LANES), F32))
        for m in reversed(range(ratio)):
            state = block(ratio * i + m, state[0], state[1], _stacked_causal(bq, bk, m * bk))
        first = ratio * i

        def two_blocks(n, st):
            st = block(first - 1 - 2 * n, st[0], st[1])
            return block(first - 2 - 2 * n, st[0], st[1])

        state = lax.fori_loop(0, first // 2, two_blocks, state)
        _, acc = lax.fori_loop(0, first % 2, lambda n, st: block(0, st[0], st[1]), state)
        o_ref[...] = jnp.where(low, acc[:bq], acc[bq:]).astype(BF16)

    return _pcall(
        body, name=name, out_shape=_sds((t, D_MODEL), BF16), grid=(D_MODEL // LANES, t // bq),
        in_specs=[pl.BlockSpec((bq, LANES), lambda p, i: (i, p)), pl.BlockSpec((t, LANES), lambda p, i: (0, p)),
                  pl.BlockSpec((t, LANES), lambda p, i: (0, p))],
        out_specs=pl.BlockSpec((bq, LANES), lambda p, i: (i, p)),
        semantics=("parallel", "arbitrary"))(q, k, v)


def loss_forward(x, target, *, name, tm=512):
    t, d = x.shape
    tm = min(tm, t)

    def body(x_ref, t_ref, l_ref, dx_ref):
        @pl.when(pl.program_id(0) == 0)
        def _():
            l_ref[...] = jnp.zeros_like(l_ref)

        diff = x_ref[...] - t_ref[...]
        dx_ref[...] = diff * (1.0 / d)
        l_ref[...] += 0.5 * jnp.sum(jnp.mean(diff * diff, axis=-1, keepdims=True))

    return _pcall(
        body, name=name, out_shape=[_sds((8, LANES), F32), _sds((t, d), F32)], grid=(t // tm,),
        in_specs=[pl.BlockSpec((tm, d), lambda i: (i, 0))] * 2,
        out_specs=[pl.BlockSpec((8, LANES), lambda i: (0, 0)), pl.BlockSpec((tm, d), lambda i: (i, 0))],
        semantics=("arbitrary",))(x, target)


def matmul_nt(dy, w, *, name, mul=None, out_dtype=F32, tm=512):
    t, n = dy.shape
    k = w.shape[0]
    tm = min(tm, t)

    def body(*refs):
        if mul is None:
            dy_ref, w_ref, o_ref = refs
        else:
            dy_ref, w_ref, m_ref, o_ref = refs
        y = _dot_nt(dy_ref[...].astype(BF16), w_ref[...])
        if mul is not None:
            y = y * (2.0 * m_ref[...].astype(F32))
        o_ref[...] = y.astype(out_dtype)

    row = lambda i: (i, 0)
    in_specs = [pl.BlockSpec((tm, n), row), _full(w.shape)]
    args = [dy, w]
    if mul is not None:
        in_specs.append(pl.BlockSpec((tm, k), row))
        args.append(mul)
    return _pcall(body, name=name, out_shape=_sds((t, k), out_dtype), grid=(t // tm,), in_specs=in_specs,
                  out_specs=pl.BlockSpec((tm, k), row), semantics=("parallel",))(*args)


def matmul_tn(a, dy, *, name, col_shards, tk=512):
    t, k = a.shape
    n = dy.shape[1]
    if col_shards:
        tn = n // N_SHARDS

        def body(a_ref, dy_ref, o_ref):
            o_ref[...] = _dot_tn(a_ref[...].astype(BF16), dy_ref[...].astype(BF16))

        return _pcall(body, name=name, out_shape=_sds((N_SHARDS, k, tn), F32), grid=(N_SHARDS,),
                      in_specs=[_full((t, k)), pl.BlockSpec((t, tn), lambda j: (0, j))],
                      out_specs=pl.BlockSpec((None, k, tn), lambda j: (j, 0, 0)), semantics=("parallel",))(a, dy)

    tk = min(tk, k)

    def body(a_ref, dy_ref, o_ref, dy_bf):
        @pl.when(pl.program_id(0) == 0)
        def _():
            dy_bf[...] = dy_ref[...].astype(BF16)

        o_ref[...] = _dot_tn(a_ref[...].astype(BF16), dy_bf[...])

    return _pcall(body, name=name, out_shape=_sds((k, n), F32), grid=(k // tk,),
                  in_specs=[pl.BlockSpec((t, tk), lambda i: (0, i)), _full((t, n))],
                  out_specs=pl.BlockSpec((tk, n), lambda i: (i, 0)),
                  scratch_shapes=[pltpu.VMEM((t, n), BF16)], semantics=("arbitrary",))(a, dy)


def norm_backward(dpre, w, x, g, rstd, dx_out, *, name, tm=512):
    t, d = x.shape
    n = dpre.shape[1]
    tm = min(tm, t)
    if w.ndim == 3:
        w_spec = pl.BlockSpec(w.shape, lambda i: (0, 0, 0))
    else:
        w_spec = pl.BlockSpec(w.shape, lambda i: (0, 0))

    def body(dp_ref, w_ref, x_ref, g_ref, r_ref, dxo_ref, dx_ref, dg_ref):
        @pl.when(pl.program_id(0) == 0)
        def _():
            dg_ref[...] = jnp.zeros_like(dg_ref)

        if w.ndim == 3:
            per = n // N_SHARDS
            dh = _dot_nt(dp_ref[:, 0:per], w_ref[0])
            for s in range(1, N_SHARDS):
                dh = dh + _dot_nt(dp_ref[:, s * per:(s + 1) * per], w_ref[s])
        else:
            dh = _dot_nt(dp_ref[...], w_ref[...])
        r = r_ref[...]
        xn = x_ref[...] * r
        dg_ref[...] += jnp.sum(dh * xn, axis=0, keepdims=True)
        dxn = dh * g_ref[...]
        dx = r * (dxn - xn * jnp.mean(dxn * xn, axis=-1, keepdims=True))
        dx_ref[...] = dxo_ref[...] + dx

    row = lambda i: (i, 0)
    fixed = lambda i: (0, 0)
    return _pcall(
        body, name=name, out_shape=[_sds((t, d), F32), _sds((1, d), F32)], grid=(t // tm,),
        in_specs=[pl.BlockSpec((tm, n), row), w_spec, pl.BlockSpec((tm, d), row),
                  pl.BlockSpec((1, d), fixed), pl.BlockSpec((tm, 1), row), pl.BlockSpec((tm, d), row)],
        out_specs=[pl.BlockSpec((tm, d), row), pl.BlockSpec((1, d), fixed)],
        semantics=("arbitrary",))(dpre, w, x, g, rstd, dx_out)


def ple_backward(dx, gate, pp, *, name, tm=512):
    t, d = dx.shape
    tm = min(tm, t)

    def body(dx_ref, gate_ref, pp_ref, dg_ref, dp_ref):
        dxv = dx_ref[...]
        gate = gate_ref[...]
        dg_ref[...] = (dxv * pp_ref[...].astype(F32) * (gate * (1.0 - gate))).astype(BF16)
        dp_ref[...] = (dxv * gate).astype(BF16)

    spec = pl.BlockSpec((tm, d), lambda i: (i, 0))
    return _pcall(body, name=name, out_shape=[_sds((t, d), BF16)] * 2, grid=(t // tm,), in_specs=[spec] * 3,
                  out_specs=[spec] * 2, semantics=("parallel",))(dx, gate, pp)


def sgu_backward(dy, pre, g_v, w_s, b_full, *, name):
    t = pre.shape[0]
    n_chunks = t // CHUNK

    def body(dy_ref, pre_ref, gv_ref, ws_ref, b_ref, dpre_ref, dws_ref, db_ref, dgv_ref, dvn_s, dbf_s):
        step = pl.program_id(0)

        @pl.when(step == 0)
        def _():
            dws_ref[...] = jnp.zeros_like(dws_ref)
            dgv_ref[...] = jnp.zeros_like(dgv_ref)
            dbf_s[...] = jnp.zeros_like(dbf_s)

        pre_u, pre_v, u, r, vhat, vn, wm, tril = _sgu_common(pre_ref, gv_ref, ws_ref)
        dyv = dy_ref[...]
        for g in range(N_GROUPS):
            cols = slice(g * LANES, (g + 1) * LANES)
            mix = _dot(wm[g], vn[:, cols]) + b_ref[:, cols]
            dmix = dyv[:, cols] * u[:, cols]
            dmix_b = dmix.astype(BF16)
            du = dyv[:, cols] * mix
            dpre_ref[:, cols] = (du * _gelu_grad(pre_u[:, cols])).astype(BF16)
            dws_ref[g] += jnp.where(tril, _dot_nt(dmix_b, vn[:, cols]), 0.0)
            dbf_s[:, cols] += dmix
            dvn_s[:, cols] = _dot_tn(wm[g], dmix_b)
        dvn = dvn_s[...]
        dgv_ref[...] += jnp.sum(dvn * vhat, axis=0, keepdims=True)
        dxn = dvn * gv_ref[...]
        dv = r * (dxn - vhat * jnp.mean(dxn * vhat, axis=-1, keepdims=True))
        dpre_ref[:, D_MODEL:] = (dv * _gelu_grad(pre_v)).astype(BF16)

        @pl.when(step == n_chunks - 1)
        def _():
            lane = lax.broadcasted_iota(jnp.int32, (CHUNK, LANES), 1)
            acc = jnp.zeros((CHUNK, LANES), F32)
            for g in range(N_GROUPS):
                s = jnp.sum(dbf_s[:, g * LANES:(g + 1) * LANES], axis=-1, keepdims=True)
                acc = jnp.where(lane == g, s, acc)
            db_ref[...] = acc

    fixed2 = lambda i: (0, 0)
    return _pcall(
        body, name=name,
        out_shape=[_sds((t, 2 * D_MODEL), BF16), _sds((N_GROUPS, CHUNK, CHUNK), F32), _sds((CHUNK, LANES), F32),
                   _sds((1, D_MODEL), F32)],
        grid=(n_chunks,),
        in_specs=[pl.BlockSpec((CHUNK, D_MODEL), lambda i: (i, 0)), pl.BlockSpec((CHUNK, 2 * D_MODEL), lambda i: (i, 0)),
                  pl.BlockSpec((1, D_MODEL), fixed2), pl.BlockSpec((N_GROUPS, CHUNK, CHUNK), lambda i: (0, 0, 0)),
                  pl.BlockSpec((CHUNK, D_MODEL), fixed2)],
        out_specs=[pl.BlockSpec((CHUNK, 2 * D_MODEL), lambda i: (i, 0)),
                   pl.BlockSpec((N_GROUPS, CHUNK, CHUNK), lambda i: (0, 0, 0)), pl.BlockSpec((CHUNK, LANES), fixed2),
                   pl.BlockSpec((1, D_MODEL), fixed2)],
        scratch_shapes=[pltpu.VMEM((CHUNK, D_MODEL), F32), pltpu.VMEM((CHUNK, D_MODEL), F32)],
        semantics=("arbitrary",))(dy, pre, g_v, w_s, b_full)


def head_norm_backward(dy, pre, g128, *, name, col_block=0, scale=1.0, passthrough=None, tm=512):
    t = dy.shape[0]
    tm = min(tm, t)
    width = 2 * D_MODEL if passthrough is not None else D_MODEL

    def body(*refs):
        if passthrough is not None:
            dy_ref, x_ref, g_ref, dv_ref, o_ref, dg_ref = refs
            o_ref[:, D_MODEL:] = dv_ref[...].astype(BF16)
        else:
            dy_ref, x_ref, g_ref, o_ref, dg_ref = refs

        @pl.when(pl.program_id(0) == 0)
        def _():
            dg_ref[...] = jnp.zeros_like(dg_ref)

        g = g_ref[...]
        dg = jnp.zeros((1, LANES), F32)
        for b in range(D_MODEL // LANES):
            cols = slice(b * LANES, (b + 1) * LANES)
            xv = x_ref[:, cols]
            r = _head_rstd(xv)
            xn = xv * r
            dyv = dy_ref[:, cols] * scale
            dg = dg + jnp.sum(dyv * xn, axis=0, keepdims=True)
            dxn = dyv * g
            o_ref[:, cols] = (r * (dxn - xn * _head_mean(dxn * xn))).astype(BF16)
        dg_ref[...] += dg

    row = lambda i: (i, 0)
    in_specs = [pl.BlockSpec((tm, D_MODEL), row), pl.BlockSpec((tm, D_MODEL), lambda i: (i, col_block)),
                pl.BlockSpec((1, LANES), lambda i: (0, 0))]
    args = [dy, pre, g128]
    if passthrough is not None:
        in_specs.append(pl.BlockSpec((tm, D_MODEL), row))
        args.append(passthrough)
    return _pcall(body, name=name, out_shape=[_sds((t, width), BF16), _sds((1, LANES), F32)], grid=(t // tm,),
                  in_specs=in_specs,
                  out_specs=[pl.BlockSpec((tm, width), row), pl.BlockSpec((1, LANES), lambda i: (0, 0))],
                  semantics=("arbitrary",))(*args)


def stick_breaking_backward(q, k, v, do, *, name):
    t = q.shape[0]
    bq, bk, ratio = _att_blocks(t)

    def body(q_ref, k_ref, v_ref, do_ref, dq_ref, dk_ref, dv_ref, s_buf, sg_buf):
        i = pl.program_id(1)

        @pl.when(i == 0)
        def _():
            dk_ref[...] = jnp.zeros_like(dk_ref)
            dv_ref[...] = jnp.zeros_like(dv_ref)

        low = lax.broadcasted_iota(jnp.int32, (bq, LANES), 1) < HEAD_DIM
        suffix = _suffix_matrix(bk)
        prefix = _prefix_matrix(bk)
        qs = _stack_heads(q_ref[...], low)
        dos = _stack_heads(do_ref[...], low)
        first = ratio * i

        def log_weights(j, carry, causal=None):
            rows = pl.ds(pl.multiple_of(j * bk, bk), bk)
            z = _dot_nt(qs, k_ref[rows, :])
            ls = _log_sigmoid(z)
            lg = ls - z
            if causal is not None:
                lg = jnp.where(causal, lg, 0.0)
            s_buf[j] = ls + _block_cumsum(lg, suffix) + carry
            sg_buf[j] = jnp.exp(ls)
            return carry + jnp.sum(lg, axis=-1, keepdims=True)

        carry = jnp.zeros((2 * bq, 1), F32)
        for m in reversed(range(ratio)):
            carry = log_weights(first + m, carry, _stacked_causal(bq, bk, m * bk))
        carry = lax.fori_loop(0, first // 2,
                              lambda n, c: log_weights(first - 2 - 2 * n, log_weights(first - 1 - 2 * n, c)), carry)
        lax.fori_loop(0, first % 2, lambda n, c: log_weights(0, c), carry)

        def grads(j, pcarry, dq_acc, causal=None):
            rows = pl.ds(pl.multiple_of(j * bk, bk), bk)
            a = jnp.exp(s_buf[j])
            if causal is not None:
                a = jnp.where(causal, a, 0.0)
            sg = sg_buf[j]
            ds = _dot_nt(dos, v_ref[rows, :]) * a
            before = _block_cumsum(ds, prefix) + pcarry
            if causal is not None:
                before = jnp.where(causal, before, 0.0)
            dz = (ds - sg * (ds + before)).astype(BF16)
            dq_acc = dq_acc + _dot(dz, k_ref[rows, :])
            dk_ref[rows, :] += _dot_tn(dz, qs)
            dv_ref[rows, :] += _dot_tn(a.astype(BF16), dos)
            return pcarry + jnp.sum(ds, axis=-1, keepdims=True), dq_acc

        def two_blocks(n, st):
            st = grads(2 * n, st[0], st[1])
            return grads(2 * n + 1, st[0], st[1])

        state = lax.fori_loop(0, first // 2, two_blocks,
                              (jnp.zeros((2 * bq, 1), F32), jnp.zeros((2 * bq, LANES), F32)))
        state = lax.fori_loop(0, first % 2, lambda n, st: grads(first - 1, st[0], st[1]), state)
        for m in range(ratio):
            state = grads(first + m, state[0], state[1], _stacked_causal(bq, bk, m * bk))
        dq_ref[...] = jnp.where(low, state[1][:bq], state[1][bq:])

    full = pl.BlockSpec((t, LANES), lambda p, i: (0, p))
    qblk = pl.BlockSpec((bq, LANES), lambda p, i: (i, p))
    return _pcall(
        body, name=name, out_shape=[_sds((t, D_MODEL), F32)] * 3, grid=(D_MODEL // LANES, t // bq),
        in_specs=[qblk, full, full, qblk], out_specs=[qblk, full, full],
        scratch_shapes=[pltpu.VMEM((t // bk, 2 * bq, bk), F32), pltpu.VMEM((t // bk, 2 * bq, bk), F32)],
        semantics=("parallel", "arbitrary"))(q, k, v, do)


def _mlp_backward(dx, saved, g, w_up, w_down, tag):
    x, h, r, a, a2 = saved
    d_w_down = matmul_tn(a2, dx, name=f"d_w_down_{tag}", col_shards=False)
    dpre = matmul_nt(dx, w_down, name=f"d_mlp_act_{tag}", mul=a, out_dtype=BF16)
    d_w_up = matmul_tn(h, dpre, name=f"d_w_up_{tag}", col_shards=True)
    dx, d_g = norm_backward(dpre, w_up, x, g, r, dx, name=f"d_mlp_norm_{tag}")
    return dx, d_w_up, d_w_down, d_g


def _ple_backward(dx, saved, p, g, w_gate, tag):
    x, h, r, gate, pp = saved
    dgate, dproj = ple_backward(dx, gate, pp, name=f"d_ple_{tag}")
    d_w_proj = matmul_tn(p, dproj, name=f"d_w_ple_proj_{tag}", col_shards=True)
    d_w_gate = matmul_tn(h, dgate, name=f"d_w_ple_gate_{tag}", col_shards=False)
    dx, d_g = norm_backward(dgate, w_gate, x, g, r, dx, name=f"d_ple_norm_{tag}")
    return dx, d_w_gate, d_w_proj, d_g


def local_step(x, p, target, w, late=None):
    row = lambda v: v.reshape(1, -1)
    g128 = lambda v: jnp.tile(v.reshape(1, HEAD_DIM), (1, 2))
    scale = HEAD_DIM ** -0.5
    b_full = jnp.repeat(jnp.transpose(w["b_spatial"][0]), LANES, axis=1)
    w_s = w["w_spatial"][0]

    mats = {}
    for name, value in w.items():
        if isinstance(value, tuple):
            mats.update({(name, layer): v for layer, v in enumerate(value)})
    if "w_kv" in w:
        mats[("w_kv", 0)] = w["w_kv"]

    def fetch(name, layer, after):
        if (name, layer) not in mats:
            mats.update(late.weights(name, layer, after))
        return mats[(name, layer)]

    def mlp_forward(x_in, layer):
        h, r, a, a2 = norm_matmul(x_in, row(w["ln_mlp"][layer]), fetch("w_up", layer, x_in), name=f"mlp_up_{layer}",
                                  epilogue="relu2")
        return matmul_residual(a2, fetch("w_down", layer, a2), x_in, name=f"mlp_down_{layer}"), (x_in, h, r, a, a2)

    def ple(x_in, layer):
        return ple_forward(x_in, row(w["ln_ple"][layer]), fetch("w_ple_gate", layer, x_in), p[layer],
                           fetch("w_ple_proj", layer, x_in), name=f"ple_{layer}")

    x0 = x
    h_a, r_a, pre_a = norm_matmul(x0, row(w["ln_mix_a"][0]), fetch("w_in_a", 0, x0), name="sgu_in")
    y_a = sgu_forward(pre_a, row(w["g_v_a"][0]), w_s, b_full, name="sgu_mix")
    x1 = matmul_residual(y_a, fetch("w_out_a", 0, y_a), x0, name="sgu_out")
    x2, mlp0 = mlp_forward(x1, 0)
    ple0 = ple(x2, 0)
    x3 = ple0[4]
    h_kv, r_kv, kv_pre = norm_matmul(x3, row(w["ln_kv"]), fetch("w_kv", 0, x3), name="kv_proj")
    k_n, v_b = head_norm(kv_pre, g128(w["g_k"]), name="k_norm", passthrough=True)
    h_q, r_q, q_pre = norm_matmul(x3, row(w["ln_mix_b"][0]), fetch("w_q", 0, k_n), name="q_proj")
    q_n = head_norm(q_pre, g128(w["g_q"][0]), name="q_norm", scale=scale)
    o = stick_breaking_forward(q_n, k_n, v_b, name="sb_fwd")
    x4 = matmul_residual(o, fetch("w_out_b", 0, o), x3, name="sb_out")
    x5, mlp1 = mlp_forward(x4, 1)
    ple1 = ple(x5, 1)
    x6 = ple1[4]
    loss_blk, dx = loss_forward(x6, target, name="loss")

    g = {}
    dx, dwg1, dwp1, dlnp1 = _ple_backward(dx, (x5,) + tuple(ple1[:4]), p[1], row(w["ln_ple"][1]),
                                          mats[("w_ple_gate", 1)], 1)
    dx, dwu1, dwd1, dlnm1 = _mlp_backward(dx, mlp1, row(w["ln_mlp"][1]), mats[("w_up", 1)], mats[("w_down", 1)], 1)
    g["w_out_b"] = matmul_tn(o, dx, name="d_w_out_b", col_shards=False)
    do = matmul_nt(dx, mats[("w_out_b", 0)], name="d_sb_out", out_dtype=BF16)
    dq_n, dk_n, dv = stick_breaking_backward(q_n, k_n, v_b, do, name="sb_bwd")
    dq_pre, dgq = head_norm_backward(dq_n, q_pre, g128(w["g_q"][0]), name="d_q_norm", scale=scale)
    dkv_pre, dgk = head_norm_backward(dk_n, kv_pre, g128(w["g_k"]), name="d_k_norm", passthrough=dv)
    g["w_q"] = matmul_tn(h_q, dq_pre, name="d_w_q", col_shards=False)
    g["w_kv"] = matmul_tn(h_kv, dkv_pre, name="d_w_kv", col_shards=True)
    dx, g["ln_mix_b"] = norm_backward(dq_pre, mats[("w_q", 0)], x3, row(w["ln_mix_b"][0]), r_q, dx, name="d_q_in")
    dx, g["ln_kv"] = norm_backward(dkv_pre, mats[("w_kv", 0)], x3, row(w["ln_kv"]), r_kv, dx, name="d_kv_in")
    g["g_q"] = dgq[:, :HEAD_DIM] + dgq[:, HEAD_DIM:]
    g["g_k"] = (dgk[:, :HEAD_DIM] + dgk[:, HEAD_DIM:]).reshape(HEAD_DIM)
    g["ln_kv"] = g["ln_kv"].reshape(D_MODEL)
    ln_ple0, ln_mlp0, g_v0, ln_mix0 = (row(w["ln_ple"][0]), row(w["ln_mlp"][0]), row(w["g_v_a"][0]),
                                       row(w["ln_mix_a"][0]))
    if late is not None:
        ln_ple0 = ln_ple0 + late.pair_start(
            {("w_kv", 0): g["w_kv"], ("w_q", 0): g["w_q"], ("w_out_b", 0): g["w_out_b"], ("w_up", 1): dwu1,
             ("w_down", 1): dwd1, ("w_ple_gate", 1): dwg1, ("w_ple_proj", 1): dwp1}, dx)[0, 0]
    dx, dwg0, dwp0, dlnp0 = _ple_backward(dx, (x2,) + tuple(ple0[:4]), p[0], ln_ple0, mats[("w_ple_gate", 0)], 0)
    if late is not None:
        ln_mlp0 = ln_mlp0 + late.chip_start(dx)[0, 0]
    dx, dwu0, dwd0, dlnm0 = _mlp_backward(dx, mlp0, ln_mlp0, mats[("w_up", 0)], mats[("w_down", 0)], 0)
    if late is not None:
        g_v0 = g_v0 + late.pair_start({("w_up", 0): dwu0, ("w_down", 0): dwd0, ("w_ple_gate", 0): dwg0,
                                       ("w_ple_proj", 0): dwp0}, dx)[0, 0]
    g["w_out_a"] = matmul_tn(y_a, dx, name="d_w_out_a", col_shards=False)
    dy_a = matmul_nt(dx, mats[("w_out_a", 0)], name="d_sgu_out")
    dpre_a, dws, db, g["g_v_a"] = sgu_backward(dy_a, pre_a, g_v0, w_s, b_full, name="d_sgu_mix")
    if late is not None:
        ln_mix0 = ln_mix0 + late.chip_start(dpre_a)[0, 0]
    g["w_in_a"] = matmul_tn(h_a, dpre_a, name="d_w_in_a", col_shards=True)
    dx, g["ln_mix_a"] = norm_backward(dpre_a, mats[("w_in_a", 0)], x0, ln_mix0, r_a, dx, name="d_sgu_in")
    g["w_spatial"] = dws[None]
    g["b_spatial"] = jnp.transpose(db[:, :N_GROUPS])[None]
    g["w_up"] = (dwu0, dwu1)
    g["w_down"] = (dwd0, dwd1)
    g["w_ple_gate"] = (dwg0, dwg1)
    g["w_ple_proj"] = (dwp0, dwp1)
    g["ln_mlp"] = jnp.concatenate([dlnm0, dlnm1], axis=0)
    g["ln_ple"] = jnp.concatenate([dlnp0, dlnp1], axis=0)
    return loss_blk, dx, g


ANY = pl.BlockSpec(memory_space=pl.ANY)


def _place():
    x, y, c = lax.axis_index("x"), lax.axis_index("y"), lax.axis_index("c")
    others = [(1 - x, y), (x, 1 - y), (1 - x, 1 - y)]
    return x, y, c, 2 * x + y, others


def cast_into_slot(w3, layer, slot, *, name, after=None, tm=512):
    _, r, c = w3.shape
    tm = min(tm, r)

    def body(slot_ref, w_ref, *rest):
        rest[-1][...] = w_ref[...].astype(BF16)

    in_specs = [pl.BlockSpec((None, tm, c), lambda i, s: (layer, i, 0))]
    args = [slot, w3]
    if after is not None:
        in_specs.append(ANY)
        args.append(after)
    return _pcall(body, name=name, out_shape=_sds((N_SHARDS, r, c), BF16), grid=(r // tm,), num_prefetch=1,
                  in_specs=in_specs, out_specs=pl.BlockSpec((None, tm, c), lambda i, s: (s[0], i, 0)),
                  semantics=("parallel",))(*args)


def gather_shards(mats, vecs, *, name):
    nm, nv = len(mats), len(vecs)
    halves = [m.reshape(N_SHARDS, 2, m.shape[1] // 2, m.shape[2]) for m in mats]

    def body(*refs):
        vsrc = refs[nm:nm + nv]
        out, vout = refs[nm + nv:2 * nm + nv], refs[2 * nm + nv:2 * (nm + nv)]
        send, recv, vsend, vrecv, loc = refs[2 * (nm + nv):]
        x, y, c, s_me, others = _place()
        sib = (x, y, 1 - c)

        def ici(l, k):
            ox, oy = others[k]
            return pltpu.make_async_remote_copy(out[l].at[s_me, c], out[l].at[s_me, c], send.at[l, k], recv.at[l, k],
                                                device_id=(ox, oy, c), device_id_type=MESH)

        def landed(l, k, half):
            ox, oy = others[k]
            return out[l].at[2 * ox + oy, half]

        def passed_on(l, k):
            return pltpu.make_async_remote_copy(landed(l, k, c), landed(l, k, c), send.at[l, 3 + k], recv.at[l, 3 + k],
                                                device_id=sib, device_id_type=MESH)

        def vec(l, k):
            ox, oy = others[k]
            return pltpu.make_async_remote_copy(vsrc[l], vout[l].at[s_me], vsend.at[l, k], vrecv.at[l, k],
                                                device_id=(ox, oy, c), device_id_type=MESH)

        for l in range(nm):
            for k in range(3):
                ici(l, k).start()
        for l in range(nv):
            for k in range(3):
                vec(l, k).start()
        for l in range(nv):
            own = pltpu.make_async_copy(vsrc[l], vout[l].at[s_me], loc)
            own.start()
            own.wait()
        for l in range(nm):
            for k in range(3):
                pltpu.make_async_remote_copy(landed(l, k, c), landed(l, k, c), send.at[l, k], recv.at[l, k],
                                             device_id=sib, device_id_type=MESH).wait_recv()
                passed_on(l, k).start()
        for l in range(nm):
            for k in range(3):
                pltpu.make_async_remote_copy(landed(l, k, 1 - c), landed(l, k, 1 - c), send.at[l, 3 + k],
                                             recv.at[l, 3 + k], device_id=sib, device_id_type=MESH).wait_recv()
        for l in range(nv):
            for k in range(3):
                ox, oy = others[k]
                pltpu.make_async_remote_copy(vsrc[l], vout[l].at[2 * ox + oy], vsend.at[l, k], vrecv.at[l, k],
                                             device_id=sib, device_id_type=MESH).wait_recv()
        for l in range(nm):
            for k in range(3):
                ici(l, k).wait_send()
                passed_on(l, k).wait_send()
        for l in range(nv):
            for k in range(3):
                vec(l, k).wait_send()

    out_shape = [_sds(h.shape, BF16) for h in halves] + [_sds((N_SHARDS,) + v.shape, F32) for v in vecs]
    res = _pcall(body, name=name, out_shape=out_shape, in_specs=[ANY] * (nm + nv), out_specs=[ANY] * (nm + nv),
                 scratch_shapes=[pltpu.SemaphoreType.DMA((max(nm, 1), 6)), pltpu.SemaphoreType.DMA((max(nm, 1), 6)),
                                 pltpu.SemaphoreType.DMA((max(nv, 1), 3)), pltpu.SemaphoreType.DMA((max(nv, 1), 3)),
                                 pltpu.SemaphoreType.DMA(())],
                 aliases={l: l for l in range(nm)}, side_effects=True)(*halves, *vecs)
    return [r.reshape(m.shape) for r, m in zip(res[:nm], mats)], list(res[nm:])


HBM = pl.BlockSpec(memory_space=pltpu.HBM)
SEM = pl.BlockSpec(memory_space=pltpu.SEMAPHORE)
DATAFLOW = pltpu.SideEffectType.DATAFLOW_SIDE_EFFECTING


def _split_call(body, *, name, out_shape, in_specs, out_specs, aliases):
    return pl.pallas_call(body, name=name, out_shape=out_shape, in_specs=in_specs, out_specs=out_specs,
                          input_output_aliases=aliases,
                          compiler_params=pltpu.CompilerParams(has_side_effects=DATAFLOW))


def _token_shape():
    return jax.ShapeDtypeStruct((8, LANES), F32)


def gather_start(mats, after, *, name):
    n = len(mats)
    halves = [pltpu.with_memory_space_constraint(m.reshape(N_SHARDS, 2, m.shape[1] // 2, m.shape[2]), pltpu.HBM)
              for m in mats]

    def body(*refs):
        send, recv = refs[n + 1], refs[n + 2]
        out, token = refs[n + 3:2 * n + 3], refs[2 * n + 3]
        x, y, c, s_me, others = _place()
        for l in range(n):
            for k in range(3):
                ox, oy = others[k]
                pltpu.make_async_remote_copy(out[l].at[s_me, c], out[l].at[s_me, c], send.at[3 * l + k],
                                             recv.at[3 * l + k], device_id=(ox, oy, c), device_id_type=MESH).start()
        token[...] = jnp.zeros_like(token)

    res = _split_call(
        body, name=name,
        out_shape=(pltpu.SemaphoreType.DMA((3 * n,)), pltpu.SemaphoreType.DMA((3 * n,)),
                   *[pltpu.HBM(h.shape, BF16) for h in halves], _token_shape()),
        in_specs=[HBM] * n + [ANY], out_specs=(SEM, SEM, *[HBM] * n, pl.BlockSpec(memory_space=pltpu.VMEM)),
        aliases={l: 2 + l for l in range(n)})(*halves, after)
    return res[0], res[1], list(res[2:2 + n]), res[2 + n]


def gather_pass_on(bufs, send_a, recv_a, after, *, name, base=0):
    n = len(bufs)

    def body(*refs):
        send_a, recv_a = refs[n], refs[n + 1]
        out = refs[n + 3:2 * n + 3]
        send_b, recv_b, token = refs[2 * n + 3:]
        x, y, c, s_me, others = _place()
        for l in range(n):
            for k in range(3):
                ox, oy = others[k]
                landed, i = out[l].at[2 * ox + oy, c], 3 * l + k
                pltpu.make_async_remote_copy(landed, landed, send_a.at[3 * base + i], recv_a.at[3 * base + i],
                                             device_id=(x, y, 1 - c), device_id_type=MESH).wait_recv()
                pltpu.make_async_remote_copy(landed, landed, send_b.at[i], recv_b.at[i],
                                             device_id=(x, y, 1 - c), device_id_type=MESH).start()
        for l in range(n):
            for k in range(3):
                mine, i = out[l].at[s_me, c], 3 * (base + l) + k
                pltpu.make_async_remote_copy(mine, mine, send_a.at[i], recv_a.at[i],
                                             device_id=(x, y, 1 - c), device_id_type=MESH).wait_send()
        token[...] = jnp.zeros_like(token)

    res = _split_call(
        body, name=name,
        out_shape=(*[pltpu.HBM(b.shape, BF16) for b in bufs], pltpu.SemaphoreType.DMA((3 * n,)),
                   pltpu.SemaphoreType.DMA((3 * n,)), _token_shape()),
        in_specs=[HBM] * n + [SEM, SEM, ANY],
        out_specs=(*[HBM] * n, SEM, SEM, pl.BlockSpec(memory_space=pltpu.VMEM)),
        aliases={l: l for l in range(n)})(*bufs, send_a, recv_a, after)
    return list(res[:n]), res[n], res[n + 1], res[n + 2]


def gather_finish(bufs, send_b, recv_b, after, shapes, *, name):
    n = len(bufs)

    def body(*refs):
        send_b, recv_b = refs[n], refs[n + 1]
        out = refs[n + 3:]
        x, y, c, _, others = _place()
        for l in range(n):
            for k in range(3):
                ox, oy = others[k]
                theirs, mine, i = out[l].at[2 * ox + oy, 1 - c], out[l].at[2 * ox + oy, c], 3 * l + k
                pltpu.make_async_remote_copy(theirs, theirs, send_b.at[i], recv_b.at[i],
                                             device_id=(x, y, 1 - c), device_id_type=MESH).wait_recv()
                pltpu.make_async_remote_copy(mine, mine, send_b.at[i], recv_b.at[i],
                                             device_id=(x, y, 1 - c), device_id_type=MESH).wait_send()

    res = _split_call(
        body, name=name, out_shape=tuple(pltpu.HBM(b.shape, BF16) for b in bufs),
        in_specs=[HBM] * n + [SEM, SEM, ANY], out_specs=tuple([HBM] * n),
        aliases={l: l for l in range(n)})(*bufs, send_b, recv_b, after)
    return [r.reshape(s) for r, s in zip(res, shapes)]


def exchange_start(srcs, dst_shapes, dst_dtype, plan, count, after, *, name):
    n, m = len(srcs), len(dst_shapes)
    srcs = [pltpu.with_memory_space_constraint(s, pltpu.HBM) for s in srcs]
    lands = [pltpu.with_memory_space_constraint(lax.empty(s, dst_dtype), pltpu.HBM) for s in dst_shapes]

    def body(*refs):
        send, recv = refs[n + m + 1], refs[n + m + 2]
        src, dst, token = refs[n + m + 3:2 * n + m + 3], refs[2 * n + m + 3:2 * (n + m) + 3], refs[2 * (n + m) + 3]
        for i, (s, d, dev) in enumerate(plan(_place(), src, dst)):
            pltpu.make_async_remote_copy(s, d, send.at[i], recv.at[i], device_id=dev, device_id_type=MESH).start()
        token[...] = jnp.zeros_like(token)

    res = _split_call(
        body, name=name,
        out_shape=(pltpu.SemaphoreType.DMA((count,)), pltpu.SemaphoreType.DMA((count,)),
                   *[pltpu.HBM(s.shape, s.dtype) for s in srcs], *[pltpu.HBM(s, dst_dtype) for s in dst_shapes],
                   _token_shape()),
        in_specs=[HBM] * (n + m) + [ANY],
        out_specs=(SEM, SEM, *[HBM] * (n + m), pl.BlockSpec(memory_space=pltpu.VMEM)),
        aliases={i: 2 + i for i in range(n + m)})(*srcs, *lands, after)
    return (list(res[2:2 + n]), list(res[2 + n:2 + n + m]), res[0], res[1], plan), res[2 + n + m]


def exchange_finish(state, after, *, name):
    srcs, lands, send, recv, plan = state
    n, m = len(srcs), len(lands)

    def body(*refs):
        send, recv = refs[n + m], refs[n + m + 1]
        src, dst = refs[n + m + 3:2 * n + m + 3], refs[2 * n + m + 3:]
        for i, (s, d, dev) in enumerate(plan(_place(), src, dst)):
            pltpu.make_async_remote_copy(s, d, send.at[i], recv.at[i], device_id=dev, device_id_type=MESH).wait()

    res = _split_call(
        body, name=name,
        out_shape=tuple(pltpu.HBM(a.shape, a.dtype) for a in srcs + lands),
        in_specs=[HBM] * (n + m) + [SEM, SEM, ANY], out_specs=tuple([HBM] * (n + m)),
        aliases={i: i for i in range(n + m)})(*srcs, *lands, send, recv, after)
    return list(res[:n]), list(res[n:])


def pair_plan(place, src, dst):
    x, y, c, _, _ = place
    return [(s.at[:, 1 - c], d, (x, y, 1 - c)) for s, d in zip(src, dst)]


def chip_plan(place, src, dst):
    x, y, c, _, others = place
    return [(s.at[2 * ox + oy], d.at[k], (ox, oy, c)) for s, d in zip(src, dst) for k, (ox, oy) in enumerate(others)]


def pair_exchange(grads, *, name):
    n = len(grads)

    def body(*refs):
        src, got = refs[:n], refs[n:2 * n]
        send, recv = refs[2 * n:]
        x, y, c, _, _ = _place()

        def swap(l):
            return pltpu.make_async_remote_copy(src[l].at[:, 1 - c], got[l], send.at[l], recv.at[l],
                                                device_id=(x, y, 1 - c), device_id_type=MESH)

        for l in range(n):
            swap(l).start()
        for l in range(n):
            swap(l).wait()

    res = _pcall(body, name=name, out_shape=[_sds((N_SHARDS,) + g.shape[2:], F32) for g in grads],
                 in_specs=[ANY] * n, out_specs=[ANY] * n,
                 scratch_shapes=[pltpu.SemaphoreType.DMA((n,)), pltpu.SemaphoreType.DMA((n,))],
                 side_effects=True)(*grads)
    return list(res)


def add_to_wire(mine, theirs, core, *, name, tm=512):
    s, _, r, c = mine.shape
    tm = min(tm, r)

    def body(core_ref, a_ref, b_ref, o_ref):
        o_ref[...] = (a_ref[...] + b_ref[...]).astype(BF16)

    spec = pl.BlockSpec((None, tm, c), lambda i, j, cr: (i, j, 0))
    return _pcall(body, name=name, out_shape=_sds((s, r, c), BF16), grid=(s, r // tm), num_prefetch=1,
                  in_specs=[pl.BlockSpec((None, None, tm, c), lambda i, j, cr: (i, cr[0], j, 0)), spec],
                  out_specs=spec, semantics=("parallel", "parallel"))(core, mine, theirs)


def sum_chips(wire, landed, place, dest, layer, n_layers, *, name, tm=512):
    _, r, c = wire.shape
    tm = min(tm, r)

    def body(place_ref, w_ref, l_ref, *rest):
        o_ref = rest[-1]
        o_ref[...] = ((w_ref[...].astype(F32) + l_ref[0].astype(F32)) + l_ref[1].astype(F32)) + l_ref[2].astype(F32)

    in_specs = [pl.BlockSpec((None, tm, c), lambda i, pr: (pr[0], i, 0)),
                pl.BlockSpec((3, tm, c), lambda i, pr: (0, i, 0))]
    args = [place, wire, landed]
    aliases = None
    if dest is not None:
        in_specs.append(ANY)
        args.append(dest)
        aliases = {3: 0}
    return _pcall(body, name=name, out_shape=_sds((n_layers, 2, r, c), F32), grid=(r // tm,), num_prefetch=1,
                  in_specs=in_specs,
                  out_specs=pl.BlockSpec((None, None, tm, c), lambda i, pr: (layer, pr[1], i, 0)),
                  aliases=aliases, semantics=("parallel",))(*args)


def pair_share(bufs, slots, *, name):
    n = len(bufs)

    def body(*refs):
        out = refs[n:2 * n]
        send, recv = refs[2 * n:]
        x, y, c, _, _ = _place()

        def share(i, half):
            o, l = slots[i]
            return pltpu.make_async_remote_copy(out[o].at[l, half], out[o].at[l, half], send.at[i], recv.at[i],
                                                device_id=(x, y, 1 - c), device_id_type=MESH)

        for i in range(len(slots)):
            share(i, c).start()
        for i in range(len(slots)):
            share(i, 1 - c).wait_recv()
            share(i, c).wait_send()

    res = _pcall(body, name=name, out_shape=[_sds(b.shape, F32) for b in bufs], in_specs=[ANY] * n,
                 out_specs=[ANY] * n,
                 scratch_shapes=[pltpu.SemaphoreType.DMA((len(slots),)), pltpu.SemaphoreType.DMA((len(slots),))],
                 aliases={o: o for o in range(n)}, side_effects=True)(*bufs)
    return list(res)


def all_reduce_small(packed, *, name):
    n_dev, r, c = packed.shape

    def body(in_ref, out_ref, land, send, recv):
        x, y, cc, _, _ = _place()
        me = 4 * x + 2 * y + cc
        peers = [(px, py, pc) for px in range(2) for py in range(2) for pc in range(2)]

        def scatter(d):
            return pltpu.make_async_remote_copy(in_ref.at[d], land.at[me], send.at[0, d], recv.at[0, me],
                                                device_id=peers[d], device_id_type=MESH)

        def gather(d):
            return pltpu.make_async_remote_copy(out_ref.at[me], out_ref.at[me], send.at[1, d], recv.at[1, me],
                                                device_id=peers[d], device_id_type=MESH)

        for d in range(n_dev):
            @pl.when(d != me)
            def _():
                scatter(d).start()
        land[me] = in_ref[me]
        for d in range(n_dev):
            @pl.when(d != me)
            def _():
                pltpu.make_async_remote_copy(in_ref.at[d], land.at[d], send.at[0, d], recv.at[0, d],
                                             device_id=peers[d], device_id_type=MESH).wait_recv()
        total = land[0]
        for d in range(1, n_dev):
            total = total + land[d]
        out_ref[me] = total
        for d in range(n_dev):
            @pl.when(d != me)
            def _():
                gather(d).start()
        for d in range(n_dev):
            @pl.when(d != me)
            def _():
                pltpu.make_async_remote_copy(out_ref.at[d], out_ref.at[d], send.at[1, d], recv.at[1, d],
                                             device_id=peers[d], device_id_type=MESH).wait_recv()
        for d in range(n_dev):
            @pl.when(d != me)
            def _():
                scatter(d).wait_send()
                gather(d).wait_send()

    vm = pl.BlockSpec(memory_space=pltpu.VMEM)
    return _pcall(body, name=name, out_shape=_sds(packed.shape, F32), in_specs=[vm], out_specs=vm,
                  scratch_shapes=[pltpu.VMEM(packed.shape, F32), pltpu.SemaphoreType.DMA((2, n_dev)),
                                  pltpu.SemaphoreType.DMA((2, n_dev))],
                  side_effects=True)(packed)


def adamw(w, g, m, v, *, name, part=None, dest=None, tm=512):
    shape = w.shape
    cols = shape[-1]
    rows = 1
    for s in shape[:-1]:
        rows *= s
    first, count = 0, rows
    if part is not None:
        count = rows // part[1]
        first = part[0] * count
    tm = min(tm, count)
    assert count % tm == 0
    two_d = lambda a: a.reshape(rows, cols)

    def body(w_ref, g_ref, m_ref, v_ref, *rest):
        d_ref, mo_ref, vo_ref = rest[-3:]
        gv = g_ref[...]
        m_new = ADAM_B1 * m_ref[...] + (1.0 - ADAM_B1) * gv
        v_new = ADAM_B2 * v_ref[...] + (1.0 - ADAM_B2) * (gv * gv)
        m_hat = m_new / (1.0 - ADAM_B1 ** ADAM_STEP)
        v_hat = v_new / (1.0 - ADAM_B2 ** ADAM_STEP)
        d_ref[...] = -ADAM_LR * (m_hat / (jnp.sqrt(v_hat) + ADAM_EPS) + ADAM_WD * w_ref[...])
        mo_ref[...] = m_new
        vo_ref[...] = v_new

    spec = pl.BlockSpec((tm, cols), lambda i: (first // tm + i, 0))
    args = [two_d(w), two_d(g), two_d(m), two_d(v)]
    in_specs = [spec] * 4
    aliases = None
    if dest is not None:
        args += [two_d(d) for d in dest]
        in_specs = in_specs + [ANY] * 3
        aliases = {4: 0, 5: 1, 6: 2}
    outs = _pcall(body, name=name, out_shape=[_sds((rows, cols), F32)] * 3, grid=(count // tm,), in_specs=in_specs,
                  out_specs=[spec] * 3, aliases=aliases, semantics=("parallel",))(*args)
    return [o.reshape(shape) for o in outs]


WEIGHTS = ("ln_mix_a", "w_in_a", "g_v_a", "w_spatial", "b_spatial", "w_out_a", "ln_kv", "w_kv", "g_k", "ln_mix_b",
           "w_q", "g_q", "w_out_b", "ln_mlp", "w_up", "w_down", "ln_ple", "w_ple_gate", "w_ple_proj")
MATRICES = (("w_in_a", 1, True), ("w_out_a", 1, False), ("w_kv", 0, True), ("w_q", 1, False), ("w_out_b", 1, False),
            ("w_up", 2, True), ("w_down", 2, False), ("w_ple_gate", 2, False), ("w_ple_proj", 2, True))
GATHER_STAGES = ((("w_in_a", 0),), (("w_out_a", 0),), (("w_up", 0),), (("w_down", 0),),
                 (("w_ple_gate", 0), ("w_ple_proj", 0), ("w_kv", 0)), (("w_q", 0),),
                 (("w_out_b", 0), ("w_up", 1), ("w_down", 1), ("w_ple_gate", 1), ("w_ple_proj", 1)))
REPLICATED = ("w_spatial", "b_spatial", "ln_kv", "g_k", "ln_mix_b", "g_q", "ln_mlp", "ln_ple")
SHARDED_VECTORS = ("ln_mix_a", "g_v_a")
SMALL_ROWS = 18


def kernel(x, p, ln_mix_a, w_in_a, g_v_a, w_spatial, b_spatial, w_out_a, ln_kv, w_kv, g_k, ln_mix_b, w_q, g_q, w_out_b, ln_mlp, w_up, w_down, ln_ple, w_ple_gate, w_ple_proj, loss_target, m_ln_mix_a, m_w_in_a, m_g_v_a, m_w_spatial, m_b_spatial, m_w_out_a, m_ln_kv, m_w_kv, m_g_k, m_ln_mix_b, m_w_q, m_g_q, m_w_out_b, m_ln_mlp, m_w_up, m_w_down, m_ln_ple, m_w_ple_gate, m_w_ple_proj, v_ln_mix_a, v_w_in_a, v_g_v_a, v_w_spatial, v_b_spatial, v_w_out_a, v_ln_kv, v_w_kv, v_g_k, v_ln_mix_b, v_w_q, v_g_q, v_w_out_b, v_ln_mlp, v_w_up, v_w_down, v_ln_ple, v_w_ple_gate, v_w_ple_proj):
    given = dict(locals())
    weights = {n: given[n] for n in WEIGHTS}
    shard = 2 * lax.axis_index("x") + lax.axis_index("y")
    core = lax.axis_index("c")
    shard_1 = shard.astype(jnp.int32).reshape(1)
    core_1 = core.astype(jnp.int32).reshape(1)
    place = jnp.stack([shard, core]).astype(jnp.int32)

    col_sharded = {name: cols for name, _, cols in MATRICES}
    layer_count = {name: max(layers, 1) for name, layers, _ in MATRICES}

    def cast(key, after):
        name, layer = key
        w3 = weights[name] if weights[name].ndim == 3 else weights[name][None]
        return (name, layer, col_sharded[name],
                cast_into_slot(w3, layer, shard_1, name=f"cast_{name}_{layer}", after=after))

    head = [cast(key, None) for key in GATHER_STAGES[0]]
    send_h, recv_h, flying_h, token_h = gather_start([lf[3] for lf in head], shard_1, name="gather_start_0")
    tail = [cast(key, token_h) for stage in GATHER_STAGES[1:] for key in stage]
    _, vec_a = gather_shards([], [ln_mix_a, g_v_a], name="gather_vectors")
    send_a, recv_a, flying, token = gather_start([lf[3] for lf in tail], vec_a[0], name="gather_start_1")

    w = {"ln_mix_a": vec_a[0].reshape(1, D_MODEL) + token[0, 0],
         "g_v_a": vec_a[1].reshape(1, D_MODEL)}
    for name in REPLICATED:
        w[name] = weights[name]

    class Late:
        def weights(self, name, layer, after):
            stage = [(name, layer) in s for s in GATHER_STAGES].index(True)
            if stage == 0:
                base, members, sems, fly = 0, head, (send_h, recv_h), flying_h
            else:
                base = sum(len(s) for s in GATHER_STAGES[1:stage])
                members, sems, fly = tail[base:base + len(GATHER_STAGES[stage])], (send_a, recv_a), flying
            bufs, send_b, recv_b, tok = gather_pass_on(fly[base:base + len(members)], sems[0], sems[1], after,
                                                       name=f"gather_pass_on_{stage}", base=base)
            got = gather_finish(bufs, send_b, recv_b, tok, [lf[3].shape for lf in members],
                                name=f"gather_finish_{stage}")
            out = {}
            for (leaf_name, leaf_layer, cols, _), arr in zip(members, got):
                out[(leaf_name, leaf_layer)] = arr if cols else arr.reshape(N_SHARDS * arr.shape[1], arr.shape[2])
            return out

        groups = []

        def pair_start(self, grads_done, after):
            self.keys = sorted(grads_done)
            views = [view(k, grads_done[k]) for k in self.keys]
            self.pair, token = exchange_start(views, [(N_SHARDS,) + v.shape[2:] for v in views], F32, pair_plan,
                                              len(views), after, name=f"grad_pair_start_{len(self.groups)}")
            return token

        def chip_start(self, after):
            tag = len(self.groups)
            mine, theirs = exchange_finish(self.pair, after, name=f"grad_pair_finish_{tag}")
            wire = [add_to_wire(a, b, core_1, name=f"grad_pair_sum_{tag}_{i}")
                    for i, (a, b) in enumerate(zip(mine, theirs))]
            chip, token = exchange_start(wire, [(3,) + v.shape[1:] for v in wire], BF16, chip_plan, 3 * len(wire),
                                         theirs[-1], name=f"grad_chip_start_{tag}")
            self.groups.append((self.keys, chip))
            return token

    def view(key, arr):
        rows = arr.shape[-2] if col_sharded[key[0]] else arr.shape[0] // N_SHARDS
        return arr.reshape(N_SHARDS, 2, rows // 2, arr.shape[-1])

    t = x.shape[1]
    late = Late()
    loss_blk, dx, g = local_step(x[0], p.reshape(2, t, PLE_DIM), loss_target[0], w, late)

    sent = {k for keys, _ in late.groups for k in keys}
    keys_last = [(name, layer) for name, layers, _ in MATRICES for layer in range(max(layers, 1))
                 if (name, layer) not in sent]
    views = [view(k, g[k[0]][k[1]] if layer_count[k[0]] == 2 else g[k[0]]) for k in keys_last]

    theirs = pair_exchange(views, name="grad_pair_exchange_last")
    wire_0 = [add_to_wire(a, b, core_1, name=f"grad_pair_sum_last_{i}") for i, (a, b) in enumerate(zip(views, theirs))]
    chip_0, token_0 = exchange_start(wire_0, [(3,) + v.shape[1:] for v in wire_0], BF16, chip_plan, 3 * len(wire_0),
                                     theirs[-1], name="grad_chip_start_last")

    grads, bufs = {}, {}

    def sum_and_share(keys, wire, landed, tag):
        for i, (key, wv, lv) in enumerate(zip(keys, wire, landed)):
            name, layer = key
            bufs[name] = sum_chips(wv, lv, place, bufs.get(name), layer, layer_count[name],
                                   name=f"grad_chip_sum_{tag}_{i}")
        names = sorted({k[0] for k in keys})
        shared = pair_share([bufs[n] for n in names], [(names.index(k[0]), k[1]) for k in keys],
                            name=f"grad_pair_share_{tag}")
        bufs.update(zip(names, shared))

    updates = {}

    def update(n, gn, part=None):
        wn, mn, vn = weights[n], given["m_" + n], given["v_" + n]
        if wn.ndim == 1:
            wn, gn, mn, vn = (a.reshape(1, -1) for a in (wn, gn, mn, vn))
        tag = "" if part is None else f"_{part[0]}"
        updates[n] = adamw(wn, gn.reshape(wn.shape), mn, vn, name=f"adamw_{n}{tag}", part=part, dest=updates.get(n))

    after = token_0
    for tag, (keys, chip) in enumerate(late.groups + [(keys_last, chip_0)]):
        wire, landed = exchange_finish(chip, after, name=f"grad_chip_finish_{tag}")
        sum_and_share(keys, wire, landed, tag)
        for name, layer in keys:
            update(name, bufs[name], (layer, layer_count[name]) if layer_count[name] == 2 else None)
        after = updates[keys[-1][0]][0]

    small = REPLICATED + SHARDED_VECTORS
    flat = jnp.concatenate([g[n].reshape(-1) for n in small] + [loss_blk[0, :1]])
    room = 8 * SMALL_ROWS * D_MODEL
    flat = jnp.concatenate([flat, jnp.zeros((room - flat.shape[0],), F32)])
    flat, _ = lax.optimization_barrier((flat, after))
    reduced = all_reduce_small(flat.reshape(8, SMALL_ROWS, D_MODEL), name="grad_small_all_reduce").reshape(-1)
    loss = reduced[sum(g[n].size for n in small)]
    at = 0
    for n in small:
        size = g[n].size
        piece = reduced[at:at + size]
        at += size
        if n in SHARDED_VECTORS:
            per = D_MODEL // N_SHARDS
            grads[n] = lax.dynamic_slice(piece, (shard * per,), (per,)).reshape(weights[n].shape)
        else:
            grads[n] = piece.reshape(weights[n].shape)
        update(n, grads[n])
    for name, _, _ in MATRICES:
        grads[name] = bufs[name].reshape(weights[name].shape)
    delta = {n: updates[n][0].reshape(weights[n].shape) for n in WEIGHTS}
    new_m = {n: updates[n][1].reshape(weights[n].shape) for n in WEIGHTS}
    new_v = {n: updates[n][2].reshape(weights[n].shape) for n in WEIGHTS}
    return (loss, dx.reshape(x.shape), *[grads[n] for n in WEIGHTS], *[delta[n] for n in WEIGHTS],
            *[new_m[n] for n in WEIGHTS], *[new_v[n] for n in WEIGHTS])
```

```python
import jax
import jax.numpy as jnp
from jax import lax
from jax.experimental import pallas as pl
from jax.experimental.pallas import tpu as pltpu

F32 = jnp.float32
BF16 = jnp.bfloat16

D_MODEL = 1024
D_FF = 4096
PLE_DIM = 256
N_GROUPS = 8
CHUNK = 128
HEAD_DIM = 64
LANES = 128
ATT_K_BLOCK = 256
ATT_Q_BLOCK = 512
EPS = 1e-6
N_SHARDS = 4
VMEM_LIMIT = 56 * 1024 * 1024

ADAM_LR = 0.001
ADAM_B1 = 0.9
ADAM_B2 = 0.999
ADAM_EPS = 1e-08
ADAM_WD = 0.01
ADAM_STEP = 10

MESH = pl.DeviceIdType.MESH


def _pcall(body, *, name, out_shape, grid=None, in_specs=None, out_specs=None, scratch_shapes=(),
           semantics=None, aliases=None, side_effects=False, num_prefetch=0):
    params = dict(vmem_limit_bytes=VMEM_LIMIT)
    if semantics is not None:
        params["dimension_semantics"] = semantics
    if side_effects:
        params["has_side_effects"] = True
    kwargs = {}
    if aliases:
        kwargs["input_output_aliases"] = aliases
    if num_prefetch:
        spec = pltpu.PrefetchScalarGridSpec(num_scalar_prefetch=num_prefetch, grid=grid, in_specs=in_specs,
                                            out_specs=out_specs, scratch_shapes=list(scratch_shapes))
        return pl.pallas_call(body, name=name, out_shape=out_shape, grid_spec=spec,
                              compiler_params=pltpu.CompilerParams(**params), **kwargs)
    if grid is not None:
        kwargs["grid"] = grid
    if in_specs is not None:
        kwargs["in_specs"] = in_specs
    if out_specs is not None:
        kwargs["out_specs"] = out_specs
    if aliases:
        kwargs["input_output_aliases"] = aliases
    return pl.pallas_call(body, name=name, out_shape=out_shape, scratch_shapes=list(scratch_shapes),
                          compiler_params=pltpu.CompilerParams(**params), **kwargs)


def _sds(shape, dtype):
    return jax.ShapeDtypeStruct(shape, dtype)


_GELU_C = 0.7978845608028654
_GELU_A = 0.044715


def _gelu(x):
    inner = _GELU_C * (x + _GELU_A * (x * x * x))
    return 0.5 * x * (1.0 + jnp.tanh(inner))


def _gelu_grad(x):
    x2 = x * x
    t = jnp.tanh(_GELU_C * (x + _GELU_A * (x2 * x)))
    return 0.5 * (1.0 + t) + 0.5 * x * (1.0 - t * t) * (_GELU_C * (1.0 + 3.0 * _GELU_A * x2))


def _sigmoid(x):
    return 1.0 / (1.0 + jnp.exp(-x))


def _log_sigmoid(z):
    return jnp.minimum(z, 0.0) - jnp.log(1.0 + jnp.exp(-jnp.abs(z)))


def _dot(a, b):
    return jnp.dot(a, b, preferred_element_type=F32)


def _dot_nt(a, b):
    return lax.dot_general(a, b, (((1,), (1,)), ((), ())), preferred_element_type=F32)


def _dot_tn(a, b):
    return lax.dot_general(a, b, (((0,), (0,)), ((), ())), preferred_element_type=F32)


def _head_rstd(x):
    lane = lax.broadcasted_iota(jnp.int32, x.shape, 1)
    low = lane < HEAD_DIM
    sq = x * x
    s_lo = jnp.sum(jnp.where(low, sq, 0.0), axis=-1, keepdims=True)
    s_hi = jnp.sum(jnp.where(low, 0.0, sq), axis=-1, keepdims=True)
    ms = jnp.where(low, s_lo, s_hi) * (1.0 / HEAD_DIM)
    return lax.rsqrt(ms + EPS)


def _head_mean(x):
    lane = lax.broadcasted_iota(jnp.int32, x.shape, 1)
    low = lane < HEAD_DIM
    s_lo = jnp.sum(jnp.where(low, x, 0.0), axis=-1, keepdims=True)
    s_hi = jnp.sum(jnp.where(low, 0.0, x), axis=-1, keepdims=True)
    return jnp.where(low, s_lo, s_hi) * (1.0 / HEAD_DIM)


def _full(shape):
    zeros = (0,) * len(shape)
    return pl.BlockSpec(shape, lambda i: zeros)


def norm_matmul(x, g, w, *, name, epilogue="none", tm=512):
    t, d = x.shape
    sharded = w.ndim == 3
    per = w.shape[2] if sharded else w.shape[1]
    n = N_SHARDS * per if sharded else per
    tm = min(tm, t)

    def body(x_ref, g_ref, w_ref, h_ref, r_ref, *outs):
        xv = x_ref[...]
        r = lax.rsqrt(jnp.mean(xv * xv, axis=-1, keepdims=True) + EPS)
        h = ((xv * r) * g_ref[...]).astype(BF16)
        h_ref[...] = h
        r_ref[...] = r
        for s in range(N_SHARDS if sharded else 1):
            cols = slice(s * per, (s + 1) * per)
            y = _dot(h, w_ref[s] if sharded else w_ref[...])
            if epilogue == "none":
                outs[0][:, cols] = y
            else:
                a = jnp.maximum(y, 0.0)
                outs[0][:, cols] = a.astype(BF16)
                outs[1][:, cols] = (a * a).astype(BF16)

    row = lambda i: (i, 0)
    out_shape = [_sds((t, d), BF16), _sds((t, 1), F32)]
    out_specs = [pl.BlockSpec((tm, d), row), pl.BlockSpec((tm, 1), row)]
    if epilogue == "none":
        out_shape.append(_sds((t, n), F32))
        out_specs.append(pl.BlockSpec((tm, n), row))
    else:
        out_shape += [_sds((t, n), BF16), _sds((t, n), BF16)]
        out_specs += [pl.BlockSpec((tm, n), row)] * 2
    return _pcall(
        body, name=name, out_shape=out_shape, grid=(t // tm,),
        in_specs=[pl.BlockSpec((tm, d), row), _full((1, d)), _full(w.shape)],
        out_specs=out_specs, semantics=("parallel",))(x, g, w)


def matmul_residual(a, w, res, *, name, tm=512):
    t, k = a.shape
    n = w.shape[1]
    tm = min(tm, t)

    def body(a_ref, w_ref, res_ref, o_ref):
        o_ref[...] = res_ref[...] + _dot(a_ref[...], w_ref[...])

    row = lambda i: (i, 0)
    return _pcall(
        body, name=name, out_shape=_sds((t, n), F32), grid=(t // tm,),
        in_specs=[pl.BlockSpec((tm, k), row), _full(w.shape), pl.BlockSpec((tm, n), row)],
        out_specs=pl.BlockSpec((tm, n), row), semantics=("parallel",))(a, w, res)


def ple_forward(x, g, w_gate, p, w_proj, *, name, tm=256):
    t, d = x.shape
    tm = min(tm, t)

    def body(x_ref, g_ref, wg_ref, p_ref, wp_ref, h_ref, r_ref, gate_ref, pp_ref, o_ref):
        xv = x_ref[...]
        r = lax.rsqrt(jnp.mean(xv * xv, axis=-1, keepdims=True) + EPS)
        h = ((xv * r) * g_ref[...]).astype(BF16)
        h_ref[...] = h
        r_ref[...] = r
        gate = _sigmoid(_dot(h, wg_ref[...]))
        gate_ref[...] = gate
        pb = p_ref[...].astype(BF16)
        per = d // N_SHARDS
        for s in range(N_SHARDS):
            cols = slice(s * per, (s + 1) * per)
            pp = _dot(pb, wp_ref[s])
            pp_ref[:, cols] = pp.astype(BF16)
            o_ref[:, cols] = xv[:, cols] + pp * gate[:, cols]

    row = lambda i: (i, 0)
    fixed = lambda i: (0, 0)
    return _pcall(
        body, name=name,
        out_shape=[_sds((t, d), BF16), _sds((t, 1), F32), _sds((t, d), F32), _sds((t, d), BF16), _sds((t, d), F32)],
        grid=(t // tm,),
        in_specs=[pl.BlockSpec((tm, d), row), pl.BlockSpec((1, d), fixed), pl.BlockSpec((d, d), fixed),
                  pl.BlockSpec((tm, PLE_DIM), row),
                  pl.BlockSpec((N_SHARDS, PLE_DIM, d // N_SHARDS), lambda i: (0, 0, 0))],
        out_specs=[pl.BlockSpec((tm, d), row), pl.BlockSpec((tm, 1), row), pl.BlockSpec((tm, d), row),
                   pl.BlockSpec((tm, d), row), pl.BlockSpec((tm, d), row)],
        semantics=("parallel",))(x, g, w_gate, p, w_proj)


def _tril_mask():
    r = lax.broadcasted_iota(jnp.int32, (CHUNK, CHUNK), 0)
    c = lax.broadcasted_iota(jnp.int32, (CHUNK, CHUNK), 1)
    return c <= r


def _sgu_common(pre_ref, gv_ref, ws_ref):
    pre = pre_ref[...]
    pre_u, pre_v = pre[:, :D_MODEL], pre[:, D_MODEL:]
    u = _gelu(pre_u)
    v = _gelu(pre_v)
    r = lax.rsqrt(jnp.mean(v * v, axis=-1, keepdims=True) + EPS)
    vhat = v * r
    vn = (vhat * gv_ref[...]).astype(BF16)
    tril = _tril_mask()
    wm = [jnp.where(tril, ws_ref[g], 0.0).astype(BF16) for g in range(N_GROUPS)]
    return pre_u, pre_v, u, r, vhat, vn, wm, tril


def sgu_forward(pre, g_v, w_s, b_full, *, name):
    t = pre.shape[0]

    def body(pre_ref, gv_ref, ws_ref, b_ref, y_ref):
        _, _, u, _, _, vn, wm, _ = _sgu_common(pre_ref, gv_ref, ws_ref)
        for g in range(N_GROUPS):
            cols = slice(g * LANES, (g + 1) * LANES)
            mix = _dot(wm[g], vn[:, cols]) + b_ref[:, cols]
            y_ref[:, cols] = (u[:, cols] * mix).astype(BF16)

    return _pcall(
        body, name=name, out_shape=_sds((t, D_MODEL), BF16), grid=(t // CHUNK,),
        in_specs=[pl.BlockSpec((CHUNK, 2 * D_MODEL), lambda i: (i, 0)), pl.BlockSpec((1, D_MODEL), lambda i: (0, 0)),
                  pl.BlockSpec((N_GROUPS, CHUNK, CHUNK), lambda i: (0, 0, 0)),
                  pl.BlockSpec((CHUNK, D_MODEL), lambda i: (0, 0))],
        out_specs=pl.BlockSpec((CHUNK, D_MODEL), lambda i: (i, 0)),
        semantics=("parallel",))(pre, g_v, w_s, b_full)


def head_norm(pre, g128, *, name, col_block=0, scale=1.0, passthrough=False, tm=512):
    t = pre.shape[0]
    tm = min(tm, t)

    def body(*refs):
        if passthrough:
            x_ref, v_ref, g_ref, o_ref, vo_ref = refs
            vo_ref[...] = v_ref[...].astype(BF16)
        else:
            x_ref, g_ref, o_ref = refs
        g = g_ref[...] * scale
        for b in range(D_MODEL // LANES):
            cols = slice(b * LANES, (b + 1) * LANES)
            xv = x_ref[:, cols]
            o_ref[:, cols] = ((xv * _head_rstd(xv)) * g).astype(BF16)

    x_spec = pl.BlockSpec((tm, D_MODEL), lambda i: (i, col_block))
    g_spec = pl.BlockSpec((1, LANES), lambda i: (0, 0))
    o_spec = pl.BlockSpec((tm, D_MODEL), lambda i: (i, 0))
    if passthrough:
        return _pcall(body, name=name, out_shape=[_sds((t, D_MODEL), BF16)] * 2, grid=(t // tm,),
                      in_specs=[x_spec, pl.BlockSpec((tm, D_MODEL), lambda i: (i, 1)), g_spec],
                      out_specs=[o_spec, o_spec], semantics=("parallel",))(pre, pre, g128)
    return _pcall(body, name=name, out_shape=_sds((t, D_MODEL), BF16), grid=(t // tm,),
                  in_specs=[x_spec, g_spec], out_specs=o_spec, semantics=("parallel",))(pre, g128)


def _suffix_matrix(n):
    r = lax.broadcasted_iota(jnp.int32, (n, n), 0)
    c = lax.broadcasted_iota(jnp.int32, (n, n), 1)
    return jnp.where(r > c, 1.0, 0.0).astype(BF16)


def _prefix_matrix(n):
    r = lax.broadcasted_iota(jnp.int32, (n, n), 0)
    c = lax.broadcasted_iota(jnp.int32, (n, n), 1)
    return jnp.where(r < c, 1.0, 0.0).astype(BF16)


def _block_cumsum(a, tri):
    return _dot(a.astype(BF16), tri)


def _stacked_causal(nq, nk, shift):
    r = lax.broadcasted_iota(jnp.int32, (2 * nq, nk), 0)
    c = lax.broadcasted_iota(jnp.int32, (2 * nq, nk), 1)
    return c + shift < jnp.where(r >= nq, r - nq, r)


def _att_blocks(t):
    bq, bk = min(ATT_Q_BLOCK, t), min(ATT_K_BLOCK, t)
    return bq, bk, bq // bk


def _stack_heads(a, low):
    zero = jnp.zeros_like(a)
    return jnp.concatenate([jnp.where(low, a, zero), jnp.where(low, zero, a)], axis=0)


def stick_breaking_forward(q, k, v, *, name):
    t = q.shape[0]
    bq, bk, ratio = _att_blocks(t)

    def body(q_ref, k_ref, v_ref, o_ref):
        i = pl.program_id(1)
        low = lax.broadcasted_iota(jnp.int32, (bq, LANES), 1) < HEAD_DIM
        tri = _suffix_matrix(bk)
        qs = _stack_heads(q_ref[...], low)

        def block(j, carry, acc, causal=None):
            rows = pl.ds(pl.multiple_of(j * bk, bk), bk)
            z = _dot_nt(qs, k_ref[rows, :])
            ls = _log_sigmoid(z)
            lg = ls - z
            if causal is not None:
                lg = jnp.where(causal, lg, 0.0)
            s = ls + _block_cumsum(lg, tri) + carry
            a = jnp.exp(s)
            if causal is not None:
                a = jnp.where(causal, a, 0.0)
            acc = acc + _dot(a.astype(BF16), v_ref[rows, :])
            return carry + jnp.sum(lg, axis=-1, keepdims=True), acc

        state = (jnp.zeros((2 * bq, 1), F32), jnp.zeros((2 * bq, LANES), F32))
        for m in reversed(range(ratio)):
            state = block(ratio * i + m, state[0], state[1], _stacked_causal(bq, bk, m * bk))
        first = ratio * i

        def two_blocks(n, st):
            st = block(first - 1 - 2 * n, st[0], st[1])
            return block(first - 2 - 2 * n, st[0], st[1])

        state = lax.fori_loop(0, first // 2, two_blocks, state)
        _, acc = lax.fori_loop(0, first % 2, lambda n, st: block(0, st[0], st[1]), state)
        o_ref[...] = jnp.where(low, acc[:bq], acc[bq:]).astype(BF16)

    return _pcall(
        body, name=name, out_shape=_sds((t, D_MODEL), BF16), grid=(D_MODEL // LANES, t // bq),
        in_specs=[pl.BlockSpec((bq, LANES), lambda p, i: (i, p)), pl.BlockSpec((t, LANES), lambda p, i: (0, p)),
                  pl.BlockSpec((t, LANES), lambda p, i: (0, p))],
        out_specs=pl.BlockSpec((bq, LANES), lambda p, i: (i, p)),
        semantics=("parallel", "arbitrary"))(q, k, v)


def loss_forward(x, target, *, name, tm=512):
    t, d = x.shape
    tm = min(tm, t)

    def body(x_ref, t_ref, l_ref, dx_ref):
        @pl.when(pl.program_id(0) == 0)
        def _():
            l_ref[...] = jnp.zeros_like(l_ref)

        diff = x_ref[...] - t_ref[...]
        dx_ref[...] = diff * (1.0 / d)
        l_ref[...] += 0.5 * jnp.sum(jnp.mean(diff * diff, axis=-1, keepdims=True))

    return _pcall(
        body, name=name, out_shape=[_sds((8, LANES), F32), _sds((t, d), F32)], grid=(t // tm,),
        in_specs=[pl.BlockSpec((tm, d), lambda i: (i, 0))] * 2,
        out_specs=[pl.BlockSpec((8, LANES), lambda i: (0, 0)), pl.BlockSpec((tm, d), lambda i: (i, 0))],
        semantics=("arbitrary",))(x, target)


def matmul_nt(dy, w, *, name, mul=None, out_dtype=F32, tm=512):
    t, n = dy.shape
    k = w.shape[0]
    tm = min(tm, t)

    def body(*refs):
        if mul is None:
            dy_ref, w_ref, o_ref = refs
        else:
            dy_ref, w_ref, m_ref, o_ref = refs
        y = _dot_nt(dy_ref[...].astype(BF16), w_ref[...])
        if mul is not None:
            y = y * (2.0 * m_ref[...].astype(F32))
        o_ref[...] = y.astype(out_dtype)

    row = lambda i: (i, 0)
    in_specs = [pl.BlockSpec((tm, n), row), _full(w.shape)]
    args = [dy, w]
    if mul is not None:
        in_specs.append(pl.BlockSpec((tm, k), row))
        args.append(mul)
    return _pcall(body, name=name, out_shape=_sds((t, k), out_dtype), grid=(t // tm,), in_specs=in_specs,
                  out_specs=pl.BlockSpec((tm, k), row), semantics=("parallel",))(*args)


def matmul_tn(a, dy, *, name, col_shards, tk=512):
    t, k = a.shape
    n = dy.shape[1]
    if col_shards:
        tn = n // N_SHARDS

        def body(a_ref, dy_ref, o_ref):
            o_ref[...] = _dot_tn(a_ref[...].astype(BF16), dy_ref[...].astype(BF16))

        return _pcall(body, name=name, out_shape=_sds((N_SHARDS, k, tn), F32), grid=(N_SHARDS,),
                      in_specs=[_full((t, k)), pl.BlockSpec((t, tn), lambda j: (0, j))],
                      out_specs=pl.BlockSpec((None, k, tn), lambda j: (j, 0, 0)), semantics=("parallel",))(a, dy)

    tk = min(tk, k)

    def body(a_ref, dy_ref, o_ref, dy_bf):
        @pl.when(pl.program_id(0) == 0)
        def _():
            dy_bf[...] = dy_ref[...].astype(BF16)

        o_ref[...] = _dot_tn(a_ref[...].astype(BF16), dy_bf[...])

    return _pcall(body, name=name, out_shape=_sds((k, n), F32), grid=(k // tk,),
                  in_specs=[pl.BlockSpec((t, tk), lambda i: (0, i)), _full((t, n))],
                  out_specs=pl.BlockSpec((tk, n), lambda i: (i, 0)),
                  scratch_shapes=[pltpu.VMEM((t, n), BF16)], semantics=("arbitrary",))(a, dy)


def norm_backward(dpre, w, x, g, rstd, dx_out, *, name, tm=512):
    t, d = x.shape
    n = dpre.shape[1]
    tm = min(tm, t)
    if w.ndim == 3:
        w_spec = pl.BlockSpec(w.shape, lambda i: (0, 0, 0))
    else:
        w_spec = pl.BlockSpec(w.shape, lambda i: (0, 0))

    def body(dp_ref, w_ref, x_ref, g_ref, r_ref, dxo_ref, dx_ref, dg_ref):
        @pl.when(pl.program_id(0) == 0)
        def _():
            dg_ref[...] = jnp.zeros_like(dg_ref)

        if w.ndim == 3:
            per = n // N_SHARDS
            dh = _dot_nt(dp_ref[:, 0:per], w_ref[0])
            for s in range(1, N_SHARDS):
                dh = dh + _dot_nt(dp_ref[:, s * per:(s + 1) * per], w_ref[s])
        else:
            dh = _dot_nt(dp_ref[...], w_ref[...])
        r = r_ref[...]
        xn = x_ref[...] * r
        dg_ref[...] += jnp.sum(dh * xn, axis=0, keepdims=True)
        dxn = dh * g_ref[...]
        dx = r * (dxn - xn * jnp.mean(dxn * xn, axis=-1, keepdims=True))
        dx_ref[...] = dxo_ref[...] + dx

    row = lambda i: (i, 0)
    fixed = lambda i: (0, 0)
    return _pcall(
        body, name=name, out_shape=[_sds((t, d), F32), _sds((1, d), F32)], grid=(t // tm,),
        in_specs=[pl.BlockSpec((tm, n), row), w_spec, pl.BlockSpec((tm, d), row),
                  pl.BlockSpec((1, d), fixed), pl.BlockSpec((tm, 1), row), pl.BlockSpec((tm, d), row)],
        out_specs=[pl.BlockSpec((tm, d), row), pl.BlockSpec((1, d), fixed)],
        semantics=("arbitrary",))(dpre, w, x, g, rstd, dx_out)


def ple_backward(dx, gate, pp, *, name, tm=512):
    t, d = dx.shape
    tm = min(tm, t)

    def body(dx_ref, gate_ref, pp_ref, dg_ref, dp_ref):
        dxv = dx_ref[...]
        gate = gate_ref[...]
        dg_ref[...] = (dxv * pp_ref[...].astype(F32) * (gate * (1.0 - gate))).astype(BF16)
        dp_ref[...] = (dxv * gate).astype(BF16)

    spec = pl.BlockSpec((tm, d), lambda i: (i, 0))
    return _pcall(body, name=name, out_shape=[_sds((t, d), BF16)] * 2, grid=(t // tm,), in_specs=[spec] * 3,
                  out_specs=[spec] * 2, semantics=("parallel",))(dx, gate, pp)


def sgu_backward(dy, pre, g_v, w_s, b_full, *, name):
    t = pre.shape[0]
    n_chunks = t // CHUNK

    def body(dy_ref, pre_ref, gv_ref, ws_ref, b_ref, dpre_ref, dws_ref, db_ref, dgv_ref, dvn_s, dbf_s):
        step = pl.program_id(0)

        @pl.when(step == 0)
        def _():
            dws_ref[...] = jnp.zeros_like(dws_ref)
            dgv_ref[...] = jnp.zeros_like(dgv_ref)
            dbf_s[...] = jnp.zeros_like(dbf_s)

        pre_u, pre_v, u, r, vhat, vn, wm, tril = _sgu_common(pre_ref, gv_ref, ws_ref)
        dyv = dy_ref[...]
        for g in range(N_GROUPS):
            cols = slice(g * LANES, (g + 1) * LANES)
            mix = _dot(wm[g], vn[:, cols]) + b_ref[:, cols]
            dmix = dyv[:, cols] * u[:, cols]
            dmix_b = dmix.astype(BF16)
            du = dyv[:, cols] * mix
            dpre_ref[:, cols] = (du * _gelu_grad(pre_u[:, cols])).astype(BF16)
            dws_ref[g] += jnp.where(tril, _dot_nt(dmix_b, vn[:, cols]), 0.0)
            dbf_s[:, cols] += dmix
            dvn_s[:, cols] = _dot_tn(wm[g], dmix_b)
        dvn = dvn_s[...]
        dgv_ref[...] += jnp.sum(dvn * vhat, axis=0, keepdims=True)
        dxn = dvn * gv_ref[...]
        dv = r * (dxn - vhat * jnp.mean(dxn * vhat, axis=-1, keepdims=True))
        dpre_ref[:, D_MODEL:] = (dv * _gelu_grad(pre_v)).astype(BF16)

        @pl.when(step == n_chunks - 1)
        def _():
            lane = lax.broadcasted_iota(jnp.int32, (CHUNK, LANES), 1)
            acc = jnp.zeros((CHUNK, LANES), F32)
            for g in range(N_GROUPS):
                s = jnp.sum(dbf_s[:, g * LANES:(g + 1) * LANES], axis=-1, keepdims=True)
                acc = jnp.where(lane == g, s, acc)
            db_ref[...] = acc

    fixed2 = lambda i: (0, 0)
    return _pcall(
        body, name=name,
        out_shape=[_sds((t, 2 * D_MODEL), BF16), _sds((N_GROUPS, CHUNK, CHUNK), F32), _sds((CHUNK, LANES), F32),
                   _sds((1, D_MODEL), F32)],
        grid=(n_chunks,),
        in_specs=[pl.BlockSpec((CHUNK, D_MODEL), lambda i: (i, 0)), pl.BlockSpec((CHUNK, 2 * D_MODEL), lambda i: (i, 0)),
                  pl.BlockSpec((1, D_MODEL), fixed2), pl.BlockSpec((N_GROUPS, CHUNK, CHUNK), lambda i: (0, 0, 0)),
                  pl.BlockSpec((CHUNK, D_MODEL), fixed2)],
        out_specs=[pl.BlockSpec((CHUNK, 2 * D_MODEL), lambda i: (i, 0)),
                   pl.BlockSpec((N_GROUPS, CHUNK, CHUNK), lambda i: (0, 0, 0)), pl.BlockSpec((CHUNK, LANES), fixed2),
                   pl.BlockSpec((1, D_MODEL), fixed2)],
        scratch_shapes=[pltpu.VMEM((CHUNK, D_MODEL), F32), pltpu.VMEM((CHUNK, D_MODEL), F32)],
        semantics=("arbitrary",))(dy, pre, g_v, w_s, b_full)


def head_norm_backward(dy, pre, g128, *, name, col_block=0, scale=1.0, passthrough=None, tm=512):
    t = dy.shape[0]
    tm = min(tm, t)
    width = 2 * D_MODEL if passthrough is not None else D_MODEL

    def body(*refs):
        if passthrough is not None:
            dy_ref, x_ref, g_ref, dv_ref, o_ref, dg_ref = refs
            o_ref[:, D_MODEL:] = dv_ref[...].astype(BF16)
        else:
            dy_ref, x_ref, g_ref, o_ref, dg_ref = refs

        @pl.when(pl.program_id(0) == 0)
        def _():
            dg_ref[...] = jnp.zeros_like(dg_ref)

        g = g_ref[...]
        dg = jnp.zeros((1, LANES), F32)
        for b in range(D_MODEL // LANES):
            cols = slice(b * LANES, (b + 1) * LANES)
            xv = x_ref[:, cols]
            r = _head_rstd(xv)
            xn = xv * r
            dyv = dy_ref[:, cols] * scale
            dg = dg + jnp.sum(dyv * xn, axis=0, keepdims=True)
            dxn = dyv * g
            o_ref[:, cols] = (r * (dxn - xn * _head_mean(dxn * xn))).astype(BF16)
        dg_ref[...] += dg

    row = lambda i: (i, 0)
    in_specs = [pl.BlockSpec((tm, D_MODEL), row), pl.BlockSpec((tm, D_MODEL), lambda i: (i, col_block)),
                pl.BlockSpec((1, LANES), lambda i: (0, 0))]
    args = [dy, pre, g128]
    if passthrough is not None:
        in_specs.append(pl.BlockSpec((tm, D_MODEL), row))
        args.append(passthrough)
    return _pcall(body, name=name, out_shape=[_sds((t, width), BF16), _sds((1, LANES), F32)], grid=(t // tm,),
                  in_specs=in_specs,
                  out_specs=[pl.BlockSpec((tm, width), row), pl.BlockSpec((1, LANES), lambda i: (0, 0))],
                  semantics=("arbitrary",))(*args)


def stick_breaking_backward(q, k, v, do, *, name):
    t = q.shape[0]
    bq, bk, ratio = _att_blocks(t)

    def body(q_ref, k_ref, v_ref, do_ref, dq_ref, dk_ref, dv_ref, s_buf, sg_buf):
        i = pl.program_id(1)

        @pl.when(i == 0)
        def _():
            dk_ref[...] = jnp.zeros_like(dk_ref)
            dv_ref[...] = jnp.zeros_like(dv_ref)

        low = lax.broadcasted_iota(jnp.int32, (bq, LANES), 1) < HEAD_DIM
        suffix = _suffix_matrix(bk)
        prefix = _prefix_matrix(bk)
        qs = _stack_heads(q_ref[...], low)
        dos = _stack_heads(do_ref[...], low)
        first = ratio * i

        def log_weights(j, carry, causal=None):
            rows = pl.ds(pl.multiple_of(j * bk, bk), bk)
            z = _dot_nt(qs, k_ref[rows, :])
            ls = _log_sigmoid(z)
            lg = ls - z
            if causal is not None:
                lg = jnp.where(causal, lg, 0.0)
            s_buf[j] = ls + _block_cumsum(lg, suffix) + carry
            sg_buf[j] = jnp.exp(ls)
            return carry + jnp.sum(lg, axis=-1, keepdims=True)

        carry = jnp.zeros((2 * bq, 1), F32)
        for m in reversed(range(ratio)):
            carry = log_weights(first + m, carry, _stacked_causal(bq, bk, m * bk))
        carry = lax.fori_loop(0, first // 2,
                              lambda n, c: log_weights(first - 2 - 2 * n, log_weights(first - 1 - 2 * n, c)), carry)
        lax.fori_loop(0, first % 2, lambda n, c: log_weights(0, c), carry)

        def grads(j, pcarry, dq_acc, causal=None):
            rows = pl.ds(pl.multiple_of(j * bk, bk), bk)
            a = jnp.exp(s_buf[j])
            if causal is not None:
                a = jnp.where(causal, a, 0.0)
            sg = sg_buf[j]
            ds = _dot_nt(dos, v_ref[rows, :]) * a
            before = _block_cumsum(ds, prefix) + pcarry
            if causal is not None:
                before = jnp.where(causal, before, 0.0)
            dz = (ds - sg * (ds + before)).astype(BF16)
            dq_acc = dq_acc + _dot(dz, k_ref[rows, :])
            dk_ref[rows, :] += _dot_tn(dz, qs)
            dv_ref[rows, :] += _dot_tn(a.astype(BF16), dos)
            return pcarry + jnp.sum(ds, axis=-1, keepdims=True), dq_acc

        def two_blocks(n, st):
            st = grads(2 * n, st[0], st[1])
            return grads(2 * n + 1, st[0], st[1])

        state = lax.fori_loop(0, first // 2, two_blocks,
                              (jnp.zeros((2 * bq, 1), F32), jnp.zeros((2 * bq, LANES), F32)))
        state = lax.fori_loop(0, first % 2, lambda n, st: grads(first - 1, st[0], st[1]), state)
        for m in range(ratio):
            state = grads(first + m, state[0], state[1], _stacked_causal(bq, bk, m * bk))
        dq_ref[...] = jnp.where(low, state[1][:bq], state[1][bq:])

    full = pl.BlockSpec((t, LANES), lambda p, i: (0, p))
    qblk = pl.BlockSpec((bq, LANES), lambda p, i: (i, p))
    return _pcall(
        body, name=name, out_shape=[_sds((t, D_MODEL), F32)] * 3, grid=(D_MODEL // LANES, t // bq),
        in_specs=[qblk, full, full, qblk], out_specs=[qblk, full, full],
        scratch_shapes=[pltpu.VMEM((t // bk, 2 * bq, bk), F32), pltpu.VMEM((t // bk, 2 * bq, bk), F32)],
        semantics=("parallel", "arbitrary"))(q, k, v, do)


def _mlp_backward(dx, saved, g, w_up, w_down, tag):
    x, h, r, a, a2 = saved
    d_w_down = matmul_tn(a2, dx, name=f"d_w_down_{tag}", col_shards=False)
    dpre = matmul_nt(dx, w_down, name=f"d_mlp_act_{tag}", mul=a, out_dtype=BF16)
    d_w_up = matmul_tn(h, dpre, name=f"d_w_up_{tag}", col_shards=True)
    dx, d_g = norm_backward(dpre, w_up, x, g, r, dx, name=f"d_mlp_norm_{tag}")
    return dx, d_w_up, d_w_down, d_g


def _ple_backward(dx, saved, p, g, w_gate, tag):
    x, h, r, gate, pp = saved
    dgate, dproj = ple_backward(dx, gate, pp, name=f"d_ple_{tag}")
    d_w_proj = matmul_tn(p, dproj, name=f"d_w_ple_proj_{tag}", col_shards=True)
    d_w_gate = matmul_tn(h, dgate, name=f"d_w_ple_gate_{tag}", col_shards=False)
    dx, d_g = norm_backward(dgate, w_gate, x, g, r, dx, name=f"d_ple_norm_{tag}")
    return dx, d_w_gate, d_w_proj, d_g


def local_step(x, p, target, w, late=None):
    row = lambda v: v.reshape(1, -1)
    g128 = lambda v: jnp.tile(v.reshape(1, HEAD_DIM), (1, 2))
    scale = HEAD_DIM ** -0.5
    b_full = jnp.repeat(jnp.transpose(w["b_spatial"][0]), LANES, axis=1)
    w_s = w["w_spatial"][0]

    mats = {}
    for name, value in w.items():
        if isinstance(value, tuple):
            mats.update({(name, layer): v for layer, v in enumerate(value)})
    if "w_kv" in w:
        mats[("w_kv", 0)] = w["w_kv"]

    def fetch(name, layer, after):
        if (name, layer) not in mats:
            mats.update(late.weights(name, layer, after))
        return mats[(name, layer)]

    def mlp_forward(x_in, layer):
        h, r, a, a2 = norm_matmul(x_in, row(w["ln_mlp"][layer]), fetch("w_up", layer, x_in), name=f"mlp_up_{layer}",
                                  epilogue="relu2")
        return matmul_residual(a2, fetch("w_down", layer, a2), x_in, name=f"mlp_down_{layer}"), (x_in, h, r, a, a2)

    def ple(x_in, layer):
        return ple_forward(x_in, row(w["ln_ple"][layer]), fetch("w_ple_gate", layer, x_in), p[layer],
                           fetch("w_ple_proj", layer, x_in), name=f"ple_{layer}")

    x0 = x
    h_a, r_a, pre_a = norm_matmul(x0, row(w["ln_mix_a"][0]), fetch("w_in_a", 0, x0), name="sgu_in")
    y_a = sgu_forward(pre_a, row(w["g_v_a"][0]), w_s, b_full, name="sgu_mix")
    x1 = matmul_residual(y_a, fetch("w_out_a", 0, y_a), x0, name="sgu_out")
    x2, mlp0 = mlp_forward(x1, 0)
    ple0 = ple(x2, 0)
    x3 = ple0[4]
    h_kv, r_kv, kv_pre = norm_matmul(x3, row(w["ln_kv"]), fetch("w_kv", 0, x3), name="kv_proj")
    k_n, v_b = head_norm(kv_pre, g128(w["g_k"]), name="k_norm", passthrough=True)
    h_q, r_q, q_pre = norm_matmul(x3, row(w["ln_mix_b"][0]), fetch("w_q", 0, k_n), name="q_proj")
    q_n = head_norm(q_pre, g128(w["g_q"][0]), name="q_norm", scale=scale)
    o = stick_breaking_forward(q_n, k_n, v_b, name="sb_fwd")
    x4 = matmul_residual(o, fetch("w_out_b", 0, o), x3, name="sb_out")
    x5, mlp1 = mlp_forward(x4, 1)
    ple1 = ple(x5, 1)
    x6 = ple1[4]
    loss_blk, dx = loss_forward(x6, target, name="loss")

    g = {}
    dx, dwg1, dwp1, dlnp1 = _ple_backward(dx, (x5,) + tuple(ple1[:4]), p[1], row(w["ln_ple"][1]),
                                          mats[("w_ple_gate", 1)], 1)
    dx, dwu1, dwd1, dlnm1 = _mlp_backward(dx, mlp1, row(w["ln_mlp"][1]), mats[("w_up", 1)], mats[("w_down", 1)], 1)
    g["w_out_b"] = matmul_tn(o, dx, name="d_w_out_b", col_shards=False)
    do = matmul_nt(dx, mats[("w_out_b", 0)], name="d_sb_out", out_dtype=BF16)
    dq_n, dk_n, dv = stick_breaking_backward(q_n, k_n, v_b, do, name="sb_bwd")
    dq_pre, dgq = head_norm_backward(dq_n, q_pre, g128(w["g_q"][0]), name="d_q_norm", scale=scale)
    dkv_pre, dgk = head_norm_backward(dk_n, kv_pre, g128(w["g_k"]), name="d_k_norm", passthrough=dv)
    g["w_q"] = matmul_tn(h_q, dq_pre, name="d_w_q", col_shards=False)
    g["w_kv"] = matmul_tn(h_kv, dkv_pre, name="d_w_kv", col_shards=True)
    dx, g["ln_mix_b"] = norm_backward(dq_pre, mats[("w_q", 0)], x3, row(w["ln_mix_b"][0]), r_q, dx, name="d_q_in")
    dx, g["ln_kv"] = norm_backward(dkv_pre, mats[("w_kv", 0)], x3, row(w["ln_kv"]), r_kv, dx, name="d_kv_in")
    g["g_q"] = dgq[:, :HEAD_DIM] + dgq[:, HEAD_DIM:]
    g["g_k"] = (dgk[:, :HEAD_DIM] + dgk[:, HEAD_DIM:]).reshape(HEAD_DIM)
    g["ln_kv"] = g["ln_kv"].reshape(D_MODEL)
    def then(value, token):
        return lax.optimization_barrier((value, token))[0]

    if late is not None:
        dx = then(dx, late.pair_start(
            {("w_kv", 0): g["w_kv"], ("w_q", 0): g["w_q"], ("w_out_b", 0): g["w_out_b"], ("w_up", 1): dwu1,
             ("w_down", 1): dwd1, ("w_ple_gate", 1): dwg1, ("w_ple_proj", 1): dwp1}, dx))
    dx, dwg0, dwp0, dlnp0 = _ple_backward(dx, (x2,) + tuple(ple0[:4]), p[0], row(w["ln_ple"][0]),
                                          mats[("w_ple_gate", 0)], 0)
    if late is not None:
        dx = then(dx, late.chip_start(dx))
    dx, dwu0, dwd0, dlnm0 = _mlp_backward(dx, mlp0, row(w["ln_mlp"][0]), mats[("w_up", 0)], mats[("w_down", 0)], 0)
    if late is not None:
        dx = then(dx, late.pair_start({("w_up", 0): dwu0, ("w_down", 0): dwd0, ("w_ple_gate", 0): dwg0,
                                       ("w_ple_proj", 0): dwp0}, dx))
    g["w_out_a"] = matmul_tn(y_a, dx, name="d_w_out_a", col_shards=False)
    dy_a = matmul_nt(dx, mats[("w_out_a", 0)], name="d_sgu_out")
    dpre_a, dws, db, g["g_v_a"] = sgu_backward(dy_a, pre_a, row(w["g_v_a"][0]), w_s, b_full, name="d_sgu_mix")
    if late is not None:
        dpre_a = then(dpre_a, late.chip_start(dpre_a))
    g["w_in_a"] = matmul_tn(h_a, dpre_a, name="d_w_in_a", col_shards=True)
    dx, g["ln_mix_a"] = norm_backward(dpre_a, mats[("w_in_a", 0)], x0, row(w["ln_mix_a"][0]), r_a, dx, name="d_sgu_in")
    g["w_spatial"] = dws[None]
    g["b_spatial"] = jnp.transpose(db[:, :N_GROUPS])[None]
    g["w_up"] = (dwu0, dwu1)
    g["w_down"] = (dwd0, dwd1)
    g["w_ple_gate"] = (dwg0, dwg1)
    g["w_ple_proj"] = (dwp0, dwp1)
    g["ln_mlp"] = jnp.concatenate([dlnm0, dlnm1], axis=0)
    g["ln_ple"] = jnp.concatenate([dlnp0, dlnp1], axis=0)
    return loss_blk, dx, g


ANY = pl.BlockSpec(memory_space=pl.ANY)


def _place():
    x, y, c = lax.axis_index("x"), lax.axis_index("y"), lax.axis_index("c")
    others = [(1 - x, y), (x, 1 - y), (1 - x, 1 - y)]
    return x, y, c, 2 * x + y, others


def cast_into_slot(w3, layer, slot, *, name, after=None, tm=512):
    _, r, c = w3.shape
    tm = min(tm, r)

    def body(slot_ref, w_ref, *rest):
        rest[-1][...] = w_ref[...].astype(BF16)

    in_specs = [pl.BlockSpec((None, tm, c), lambda i, s: (layer, i, 0))]
    args = [slot, w3]
    if after is not None:
        in_specs.append(ANY)
        args.append(after)
    return _pcall(body, name=name, out_shape=_sds((N_SHARDS, r, c), BF16), grid=(r // tm,), num_prefetch=1,
                  in_specs=in_specs, out_specs=pl.BlockSpec((None, tm, c), lambda i, s: (s[0], i, 0)),
                  semantics=("parallel",))(*args)


def gather_shards(mats, vecs, *, name):
    nm, nv = len(mats), len(vecs)
    halves = [m.reshape(N_SHARDS, 2, m.shape[1] // 2, m.shape[2]) for m in mats]

    def body(*refs):
        vsrc = refs[nm:nm + nv]
        out, vout = refs[nm + nv:2 * nm + nv], refs[2 * nm + nv:2 * (nm + nv)]
        send, recv, vsend, vrecv, loc = refs[2 * (nm + nv):]
        x, y, c, s_me, others = _place()
        sib = (x, y, 1 - c)

        def ici(l, k):
            ox, oy = others[k]
            return pltpu.make_async_remote_copy(out[l].at[s_me, c], out[l].at[s_me, c], send.at[l, k], recv.at[l, k],
                                                device_id=(ox, oy, c), device_id_type=MESH)

        def landed(l, k, half):
            ox, oy = others[k]
            return out[l].at[2 * ox + oy, half]

        def passed_on(l, k):
            return pltpu.make_async_remote_copy(landed(l, k, c), landed(l, k, c), send.at[l, 3 + k], recv.at[l, 3 + k],
                                                device_id=sib, device_id_type=MESH)

        def vec(l, k):
            ox, oy = others[k]
            return pltpu.make_async_remote_copy(vsrc[l], vout[l].at[s_me], vsend.at[l, k], vrecv.at[l, k],
                                                device_id=(ox, oy, c), device_id_type=MESH)

        for l in range(nm):
            for k in range(3):
                ici(l, k).start()
        for l in range(nv):
            for k in range(3):
                vec(l, k).start()
        for l in range(nv):
            own = pltpu.make_async_copy(vsrc[l], vout[l].at[s_me], loc)
            own.start()
            own.wait()
        for l in range(nm):
            for k in range(3):
                pltpu.make_async_remote_copy(landed(l, k, c), landed(l, k, c), send.at[l, k], recv.at[l, k],
                                             device_id=sib, device_id_type=MESH).wait_recv()
                passed_on(l, k).start()
        for l in range(nm):
            for k in range(3):
                pltpu.make_async_remote_copy(landed(l, k, 1 - c), landed(l, k, 1 - c), send.at[l, 3 + k],
                                             recv.at[l, 3 + k], device_id=sib, device_id_type=MESH).wait_recv()
        for l in range(nv):
            for k in range(3):
                ox, oy = others[k]
                pltpu.make_async_remote_copy(vsrc[l], vout[l].at[2 * ox + oy], vsend.at[l, k], vrecv.at[l, k],
                                             device_id=sib, device_id_type=MESH).wait_recv()
        for l in range(nm):
            for k in range(3):
                ici(l, k).wait_send()
                passed_on(l, k).wait_send()
        for l in range(nv):
            for k in range(3):
                vec(l, k).wait_send()

    out_shape = [_sds(h.shape, BF16) for h in halves] + [_sds((N_SHARDS,) + v.shape, F32) for v in vecs]
    res = _pcall(body, name=name, out_shape=out_shape, in_specs=[ANY] * (nm + nv), out_specs=[ANY] * (nm + nv),
                 scratch_shapes=[pltpu.SemaphoreType.DMA((max(nm, 1), 6)), pltpu.SemaphoreType.DMA((max(nm, 1), 6)),
                                 pltpu.SemaphoreType.DMA((max(nv, 1), 3)), pltpu.SemaphoreType.DMA((max(nv, 1), 3)),
                                 pltpu.SemaphoreType.DMA(())],
                 aliases={l: l for l in range(nm)}, side_effects=True)(*halves, *vecs)
    return [r.reshape(m.shape) for r, m in zip(res[:nm], mats)], list(res[nm:])


HBM = pl.BlockSpec(memory_space=pltpu.HBM)
SEM = pl.BlockSpec(memory_space=pltpu.SEMAPHORE)
DATAFLOW = pltpu.SideEffectType.DATAFLOW_SIDE_EFFECTING


def _split_call(body, *, name, out_shape, in_specs, out_specs, aliases):
    return pl.pallas_call(body, name=name, out_shape=out_shape, in_specs=in_specs, out_specs=out_specs,
                          input_output_aliases=aliases,
                          compiler_params=pltpu.CompilerParams(has_side_effects=DATAFLOW))


def _token_shape():
    return jax.ShapeDtypeStruct((8, LANES), F32)


def gather_start(mats, after, *, name):
    n = len(mats)
    halves = [pltpu.with_memory_space_constraint(m.reshape(N_SHARDS, 2, m.shape[1] // 2, m.shape[2]), pltpu.HBM)
              for m in mats]

    def body(*refs):
        send, recv = refs[n + 1], refs[n + 2]
        out, token = refs[n + 3:2 * n + 3], refs[2 * n + 3]
        x, y, c, s_me, others = _place()
        for l in range(n):
            for k in range(3):
                ox, oy = others[k]
                pltpu.make_async_remote_copy(out[l].at[s_me, c], out[l].at[s_me, c], send.at[3 * l + k],
                                             recv.at[3 * l + k], device_id=(ox, oy, c), device_id_type=MESH).start()
        token[...] = jnp.zeros_like(token)

    res = _split_call(
        body, name=name,
        out_shape=(pltpu.SemaphoreType.DMA((3 * n,)), pltpu.SemaphoreType.DMA((3 * n,)),
                   *[pltpu.HBM(h.shape, BF16) for h in halves], _token_shape()),
        in_specs=[HBM] * n + [ANY], out_specs=(SEM, SEM, *[HBM] * n, pl.BlockSpec(memory_space=pltpu.VMEM)),
        aliases={l: 2 + l for l in range(n)})(*halves, after)
    return res[0], res[1], list(res[2:2 + n]), res[2 + n]


def gather_pass_on(bufs, send_a, recv_a, after, *, name, base=0):
    n = len(bufs)

    def body(*refs):
        send_a, recv_a = refs[n], refs[n + 1]
        out = refs[n + 3:2 * n + 3]
        send_b, recv_b, token = refs[2 * n + 3:]
        x, y, c, s_me, others = _place()
        for l in range(n):
            for k in range(3):
                ox, oy = others[k]
                landed, i = out[l].at[2 * ox + oy, c], 3 * l + k
                pltpu.make_async_remote_copy(landed, landed, send_a.at[3 * base + i], recv_a.at[3 * base + i],
                                             device_id=(x, y, 1 - c), device_id_type=MESH).wait_recv()
                pltpu.make_async_remote_copy(landed, landed, send_b.at[i], recv_b.at[i],
                                             device_id=(x, y, 1 - c), device_id_type=MESH).start()
        for l in range(n):
            for k in range(3):
                mine, i = out[l].at[s_me, c], 3 * (base + l) + k
                pltpu.make_async_remote_copy(mine, mine, send_a.at[i], recv_a.at[i],
                                             device_id=(x, y, 1 - c), device_id_type=MESH).wait_send()
        token[...] = jnp.zeros_like(token)

    res = _split_call(
        body, name=name,
        out_shape=(*[pltpu.HBM(b.shape, BF16) for b in bufs], pltpu.SemaphoreType.DMA((3 * n,)),
                   pltpu.SemaphoreType.DMA((3 * n,)), _token_shape()),
        in_specs=[HBM] * n + [SEM, SEM, ANY],
        out_specs=(*[HBM] * n, SEM, SEM, pl.BlockSpec(memory_space=pltpu.VMEM)),
        aliases={l: l for l in range(n)})(*bufs, send_a, recv_a, after)
    return list(res[:n]), res[n], res[n + 1], res[n + 2]


def gather_finish(bufs, send_b, recv_b, after, shapes, *, name):
    n = len(bufs)

    def body(*refs):
        send_b, recv_b = refs[n], refs[n + 1]
        out = refs[n + 3:]
        x, y, c, _, others = _place()
        for l in range(n):
            for k in range(3):
                ox, oy = others[k]
                theirs, mine, i = out[l].at[2 * ox + oy, 1 - c], out[l].at[2 * ox + oy, c], 3 * l + k
                pltpu.make_async_remote_copy(theirs, theirs, send_b.at[i], recv_b.at[i],
                                             device_id=(x, y, 1 - c), device_id_type=MESH).wait_recv()
                pltpu.make_async_remote_copy(mine, mine, send_b.at[i], recv_b.at[i],
                                             device_id=(x, y, 1 - c), device_id_type=MESH).wait_send()

    res = _split_call(
        body, name=name, out_shape=tuple(pltpu.HBM(b.shape, BF16) for b in bufs),
        in_specs=[HBM] * n + [SEM, SEM, ANY], out_specs=tuple([HBM] * n),
        aliases={l: l for l in range(n)})(*bufs, send_b, recv_b, after)
    return [r.reshape(s) for r, s in zip(res, shapes)]


def exchange_start(srcs, dst_shapes, dst_dtype, plan, count, after, *, name):
    n, m = len(srcs), len(dst_shapes)
    srcs = [pltpu.with_memory_space_constraint(s, pltpu.HBM) for s in srcs]
    lands = [pltpu.with_memory_space_constraint(lax.empty(s, dst_dtype), pltpu.HBM) for s in dst_shapes]

    def body(*refs):
        send, recv = refs[n + m + 1], refs[n + m + 2]
        src, dst, token = refs[n + m + 3:2 * n + m + 3], refs[2 * n + m + 3:2 * (n + m) + 3], refs[2 * (n + m) + 3]
        for i, (s, d, dev) in enumerate(plan(_place(), src, dst)):
            pltpu.make_async_remote_copy(s, d, send.at[i], recv.at[i], device_id=dev, device_id_type=MESH).start()
        token[...] = jnp.zeros_like(token)

    res = _split_call(
        body, name=name,
        out_shape=(pltpu.SemaphoreType.DMA((count,)), pltpu.SemaphoreType.DMA((count,)),
                   *[pltpu.HBM(s.shape, s.dtype) for s in srcs], *[pltpu.HBM(s, dst_dtype) for s in dst_shapes],
                   _token_shape()),
        in_specs=[HBM] * (n + m) + [ANY],
        out_specs=(SEM, SEM, *[HBM] * (n + m), pl.BlockSpec(memory_space=pltpu.VMEM)),
        aliases={i: 2 + i for i in range(n + m)})(*srcs, *lands, after)
    return (list(res[2:2 + n]), list(res[2 + n:2 + n + m]), res[0], res[1], plan), res[2 + n + m]


def exchange_finish(state, after, *, name):
    srcs, lands, send, recv, plan = state
    n, m = len(srcs), len(lands)

    def body(*refs):
        send, recv = refs[n + m], refs[n + m + 1]
        src, dst = refs[n + m + 3:2 * n + m + 3], refs[2 * n + m + 3:]
        for i, (s, d, dev) in enumerate(plan(_place(), src, dst)):
            pltpu.make_async_remote_copy(s, d, send.at[i], recv.at[i], device_id=dev, device_id_type=MESH).wait()

    res = _split_call(
        body, name=name,
        out_shape=tuple(pltpu.HBM(a.shape, a.dtype) for a in srcs + lands),
        in_specs=[HBM] * (n + m) + [SEM, SEM, ANY], out_specs=tuple([HBM] * (n + m)),
        aliases={i: i for i in range(n + m)})(*srcs, *lands, send, recv, after)
    return list(res[:n]), list(res[n:])


def pair_plan(place, src, dst):
    x, y, c, _, _ = place
    return [(s.at[:, 1 - c], d, (x, y, 1 - c)) for s, d in zip(src, dst)]


def chip_plan(place, src, dst):
    x, y, c, _, others = place
    return [(s.at[2 * ox + oy], d.at[k], (ox, oy, c)) for s, d in zip(src, dst) for k, (ox, oy) in enumerate(others)]


def pair_exchange(grads, *, name):
    n = len(grads)

    def body(*refs):
        src, got = refs[:n], refs[n:2 * n]
        send, recv = refs[2 * n:]
        x, y, c, _, _ = _place()

        def swap(l):
            return pltpu.make_async_remote_copy(src[l].at[:, 1 - c], got[l], send.at[l], recv.at[l],
                                                device_id=(x, y, 1 - c), device_id_type=MESH)

        for l in range(n):
            swap(l).start()
        for l in range(n):
            swap(l).wait()

    res = _pcall(body, name=name, out_shape=[_sds((N_SHARDS,) + g.shape[2:], F32) for g in grads],
                 in_specs=[ANY] * n, out_specs=[ANY] * n,
                 scratch_shapes=[pltpu.SemaphoreType.DMA((n,)), pltpu.SemaphoreType.DMA((n,))],
                 side_effects=True)(*grads)
    return list(res)


def add_to_wire(mine, theirs, core, *, name, tm=512):
    s, _, r, c = mine.shape
    tm = min(tm, r)

    def body(core_ref, a_ref, b_ref, o_ref):
        o_ref[...] = (a_ref[...] + b_ref[...]).astype(BF16)

    spec = pl.BlockSpec((None, tm, c), lambda i, j, cr: (i, j, 0))
    return _pcall(body, name=name, out_shape=_sds((s, r, c), BF16), grid=(s, r // tm), num_prefetch=1,
                  in_specs=[pl.BlockSpec((None, None, tm, c), lambda i, j, cr: (i, cr[0], j, 0)), spec],
                  out_specs=spec, semantics=("parallel", "parallel"))(core, mine, theirs)


def sum_chips(wire, landed, place, dest, layer, n_layers, *, name, tm=512):
    _, r, c = wire.shape
    tm = min(tm, r)

    def body(place_ref, w_ref, l_ref, *rest):
        o_ref = rest[-1]
        o_ref[...] = ((w_ref[...].astype(F32) + l_ref[0].astype(F32)) + l_ref[1].astype(F32)) + l_ref[2].astype(F32)

    in_specs = [pl.BlockSpec((None, tm, c), lambda i, pr: (pr[0], i, 0)),
                pl.BlockSpec((3, tm, c), lambda i, pr: (0, i, 0))]
    args = [place, wire, landed]
    aliases = None
    if dest is not None:
        in_specs.append(ANY)
        args.append(dest)
        aliases = {3: 0}
    return _pcall(body, name=name, out_shape=_sds((n_layers, 2, r, c), F32), grid=(r // tm,), num_prefetch=1,
                  in_specs=in_specs,
                  out_specs=pl.BlockSpec((None, None, tm, c), lambda i, pr: (layer, pr[1], i, 0)),
                  aliases=aliases, semantics=("parallel",))(*args)


def pair_share(bufs, slots, *, name):
    n = len(bufs)

    def body(*refs):
        out = refs[n:2 * n]
        send, recv = refs[2 * n:]
        x, y, c, _, _ = _place()

        def share(i, half):
            o, l = slots[i]
            return pltpu.make_async_remote_copy(out[o].at[l, half], out[o].at[l, half], send.at[i], recv.at[i],
                                                device_id=(x, y, 1 - c), device_id_type=MESH)

        for i in range(len(slots)):
            share(i, c).start()
        for i in range(len(slots)):
            share(i, 1 - c).wait_recv()
            share(i, c).wait_send()

    res = _pcall(body, name=name, out_shape=[_sds(b.shape, F32) for b in bufs], in_specs=[ANY] * n,
                 out_specs=[ANY] * n,
                 scratch_shapes=[pltpu.SemaphoreType.DMA((len(slots),)), pltpu.SemaphoreType.DMA((len(slots),))],
                 aliases={o: o for o in range(n)}, side_effects=True)(*bufs)
    return list(res)


def all_reduce_small(packed, *, name):
    n_dev, r, c = packed.shape

    def body(in_ref, out_ref, land, send, recv):
        x, y, cc, _, _ = _place()
        me = 4 * x + 2 * y + cc
        peers = [(px, py, pc) for px in range(2) for py in range(2) for pc in range(2)]

        def scatter(d):
            return pltpu.make_async_remote_copy(in_ref.at[d], land.at[me], send.at[0, d], recv.at[0, me],
                                                device_id=peers[d], device_id_type=MESH)

        def gather(d):
            return pltpu.make_async_remote_copy(out_ref.at[me], out_ref.at[me], send.at[1, d], recv.at[1, me],
                                                device_id=peers[d], device_id_type=MESH)

        for d in range(n_dev):
            @pl.when(d != me)
            def _():
                scatter(d).start()
        land[me] = in_ref[me]
        for d in range(n_dev):
            @pl.when(d != me)
            def _():
                pltpu.make_async_remote_copy(in_ref.at[d], land.at[d], send.at[0, d], recv.at[0, d],
                                             device_id=peers[d], device_id_type=MESH).wait_recv()
        total = land[0]
        for d in range(1, n_dev):
            total = total + land[d]
        out_ref[me] = total
        for d in range(n_dev):
            @pl.when(d != me)
            def _():
                gather(d).start()
        for d in range(n_dev):
            @pl.when(d != me)
            def _():
                pltpu.make_async_remote_copy(out_ref.at[d], out_ref.at[d], send.at[1, d], recv.at[1, d],
                                             device_id=peers[d], device_id_type=MESH).wait_recv()
        for d in range(n_dev):
            @pl.when(d != me)
            def _():
                scatter(d).wait_send()
                gather(d).wait_send()

    vm = pl.BlockSpec(memory_space=pltpu.VMEM)
    return _pcall(body, name=name, out_shape=_sds(packed.shape, F32), in_specs=[vm], out_specs=vm,
                  scratch_shapes=[pltpu.VMEM(packed.shape, F32), pltpu.SemaphoreType.DMA((2, n_dev)),
                                  pltpu.SemaphoreType.DMA((2, n_dev))],
                  side_effects=True)(packed)


def adamw(w, g, m, v, *, name, part=None, dest=None, tm=512):
    shape = w.shape
    cols = shape[-1]
    rows = 1
    for s in shape[:-1]:
        rows *= s
    first, count = 0, rows
    if part is not None:
        count = rows // part[1]
        first = part[0] * count
    tm = min(tm, count)
    assert count % tm == 0
    two_d = lambda a: a.reshape(rows, cols)

    def body(w_ref, g_ref, m_ref, v_ref, *rest):
        d_ref, mo_ref, vo_ref = rest[-3:]
        gv = g_ref[...]
        m_new = ADAM_B1 * m_ref[...] + (1.0 - ADAM_B1) * gv
        v_new = ADAM_B2 * v_ref[...] + (1.0 - ADAM_B2) * (gv * gv)
        m_hat = m_new / (1.0 - ADAM_B1 ** ADAM_STEP)
        v_hat = v_new / (1.0 - ADAM_B2 ** ADAM_STEP)
        d_ref[...] = -ADAM_LR * (m_hat / (jnp.sqrt(v_hat) + ADAM_EPS) + ADAM_WD * w_ref[...])
        mo_ref[...] = m_new
        vo_ref[...] = v_new

    spec = pl.BlockSpec((tm, cols), lambda i: (first // tm + i, 0))
    args = [two_d(w), two_d(g), two_d(m), two_d(v)]
    in_specs = [spec] * 4
    aliases = None
    if dest is not None:
        args += [two_d(d) for d in dest]
        in_specs = in_specs + [ANY] * 3
        aliases = {4: 0, 5: 1, 6: 2}
    outs = _pcall(body, name=name, out_shape=[_sds((rows, cols), F32)] * 3, grid=(count // tm,), in_specs=in_specs,
                  out_specs=[spec] * 3, aliases=aliases, semantics=("parallel",))(*args)
    return [o.reshape(shape) for o in outs]


WEIGHTS = ("ln_mix_a", "w_in_a", "g_v_a", "w_spatial", "b_spatial", "w_out_a", "ln_kv", "w_kv", "g_k", "ln_mix_b",
           "w_q", "g_q", "w_out_b", "ln_mlp", "w_up", "w_down", "ln_ple", "w_ple_gate", "w_ple_proj")
MATRICES = (("w_in_a", 1, True), ("w_out_a", 1, False), ("w_kv", 0, True), ("w_q", 1, False), ("w_out_b", 1, False),
            ("w_up", 2, True), ("w_down", 2, False), ("w_ple_gate", 2, False), ("w_ple_proj", 2, True))
GATHER_STAGES = ((("w_in_a", 0),), (("w_out_a", 0),), (("w_up", 0),), (("w_down", 0),),
                 (("w_ple_gate", 0), ("w_ple_proj", 0), ("w_kv", 0)), (("w_q", 0),),
                 (("w_out_b", 0), ("w_up", 1), ("w_down", 1), ("w_ple_gate", 1), ("w_ple_proj", 1)))
REPLICATED = ("w_spatial", "b_spatial", "ln_kv", "g_k", "ln_mix_b", "g_q", "ln_mlp", "ln_ple")
SHARDED_VECTORS = ("ln_mix_a", "g_v_a")
SMALL_ROWS = 18


def kernel(x, p, ln_mix_a, w_in_a, g_v_a, w_spatial, b_spatial, w_out_a, ln_kv, w_kv, g_k, ln_mix_b, w_q, g_q, w_out_b, ln_mlp, w_up, w_down, ln_ple, w_ple_gate, w_ple_proj, loss_target, m_ln_mix_a, m_w_in_a, m_g_v_a, m_w_spatial, m_b_spatial, m_w_out_a, m_ln_kv, m_w_kv, m_g_k, m_ln_mix_b, m_w_q, m_g_q, m_w_out_b, m_ln_mlp, m_w_up, m_w_down, m_ln_ple, m_w_ple_gate, m_w_ple_proj, v_ln_mix_a, v_w_in_a, v_g_v_a, v_w_spatial, v_b_spatial, v_w_out_a, v_ln_kv, v_w_kv, v_g_k, v_ln_mix_b, v_w_q, v_g_q, v_w_out_b, v_ln_mlp, v_w_up, v_w_down, v_ln_ple, v_w_ple_gate, v_w_ple_proj):
    given = dict(locals())
    weights = {n: given[n] for n in WEIGHTS}
    shard = 2 * lax.axis_index("x") + lax.axis_index("y")
    core = lax.axis_index("c")
    shard_1 = shard.astype(jnp.int32).reshape(1)
    core_1 = core.astype(jnp.int32).reshape(1)
    place = jnp.stack([shard, core]).astype(jnp.int32)

    col_sharded = {name: cols for name, _, cols in MATRICES}
    layer_count = {name: max(layers, 1) for name, layers, _ in MATRICES}

    def cast(key, after):
        name, layer = key
        w3 = weights[name] if weights[name].ndim == 3 else weights[name][None]
        return (name, layer, col_sharded[name],
                cast_into_slot(w3, layer, shard_1, name=f"cast_{name}_{layer}", after=after))

    head = [cast(key, None) for key in GATHER_STAGES[0]]
    send_h, recv_h, flying_h, token_h = gather_start([lf[3] for lf in head], shard_1, name="gather_start_0")
    tail = [cast(key, token_h) for stage in GATHER_STAGES[1:] for key in stage]
    _, vec_a = gather_shards([], [ln_mix_a, g_v_a], name="gather_vectors")
    send_a, recv_a, flying, token = gather_start([lf[3] for lf in tail], vec_a[0], name="gather_start_1")

    w = {"ln_mix_a": vec_a[0].reshape(1, D_MODEL) + token[0, 0],
         "g_v_a": vec_a[1].reshape(1, D_MODEL)}
    for name in REPLICATED:
        w[name] = weights[name]

    class Late:
        def weights(self, name, layer, after):
            stage = [(name, layer) in s for s in GATHER_STAGES].index(True)
            if stage == 0:
                base, members, sems, fly = 0, head, (send_h, recv_h), flying_h
            else:
                base = sum(len(s) for s in GATHER_STAGES[1:stage])
                members, sems, fly = tail[base:base + len(GATHER_STAGES[stage])], (send_a, recv_a), flying
            bufs, send_b, recv_b, tok = gather_pass_on(fly[base:base + len(members)], sems[0], sems[1], after,
                                                       name=f"gather_pass_on_{stage}", base=base)
            got = gather_finish(bufs, send_b, recv_b, tok, [lf[3].shape for lf in members],
                                name=f"gather_finish_{stage}")
            out = {}
            for (leaf_name, leaf_layer, cols, _), arr in zip(members, got):
                out[(leaf_name, leaf_layer)] = arr if cols else arr.reshape(N_SHARDS * arr.shape[1], arr.shape[2])
            return out

        groups = []

        def pair_start(self, grads_done, after):
            self.keys = sorted(grads_done)
            views = [view(k, grads_done[k]) for k in self.keys]
            self.pair, token = exchange_start(views, [(N_SHARDS,) + v.shape[2:] for v in views], F32, pair_plan,
                                              len(views), after, name=f"grad_pair_start_{len(self.groups)}")
            return token

        def chip_start(self, after):
            tag = len(self.groups)
            mine, theirs = exchange_finish(self.pair, after, name=f"grad_pair_finish_{tag}")
            wire = [add_to_wire(a, b, core_1, name=f"grad_pair_sum_{tag}_{i}")
                    for i, (a, b) in enumerate(zip(mine, theirs))]
            chip, token = exchange_start(wire, [(3,) + v.shape[1:] for v in wire], BF16, chip_plan, 3 * len(wire),
                                         theirs[-1], name=f"grad_chip_start_{tag}")
            self.groups.append((self.keys, chip))
            return token

    def view(key, arr):
        rows = arr.shape[-2] if col_sharded[key[0]] else arr.shape[0] // N_SHARDS
        return arr.reshape(N_SHARDS, 2, rows // 2, arr.shape[-1])

    t = x.shape[1]
    late = Late()
    loss_blk, dx, g = local_step(x[0], p.reshape(2, t, PLE_DIM), loss_target[0], w, late)

    sent = {k for keys, _ in late.groups for k in keys}
    keys_last = [(name, layer) for name, layers, _ in MATRICES for layer in range(max(layers, 1))
                 if (name, layer) not in sent]
    views = [view(k, g[k[0]][k[1]] if layer_count[k[0]] == 2 else g[k[0]]) for k in keys_last]

    theirs = pair_exchange(views, name="grad_pair_exchange_last")
    wire_0 = [add_to_wire(a, b, core_1, name=f"grad_pair_sum_last_{i}") for i, (a, b) in enumerate(zip(views, theirs))]
    chip_0, token_0 = exchange_start(wire_0, [(3,) + v.shape[1:] for v in wire_0], BF16, chip_plan, 3 * len(wire_0),
                                     theirs[-1], name="grad_chip_start_last")

    grads, bufs = {}, {}

    def sum_and_share(keys, wire, landed, tag):
        for i, (key, wv, lv) in enumerate(zip(keys, wire, landed)):
            name, layer = key
            bufs[name] = sum_chips(wv, lv, place, bufs.get(name), layer, layer_count[name],
                                   name=f"grad_chip_sum_{tag}_{i}")
        names = sorted({k[0] for k in keys})
        shared = pair_share([bufs[n] for n in names], [(names.index(k[0]), k[1]) for k in keys],
                            name=f"grad_pair_share_{tag}")
        bufs.update(zip(names, shared))

    updates = {}

    def update(n, gn, part=None):
        wn, mn, vn = weights[n], given["m_" + n], given["v_" + n]
        if wn.ndim == 1:
            wn, gn, mn, vn = (a.reshape(1, -1) for a in (wn, gn, mn, vn))
        tag = "" if part is None else f"_{part[0]}"
        updates[n] = adamw(wn, gn.reshape(wn.shape), mn, vn, name=f"adamw_{n}{tag}", part=part, dest=updates.get(n))

    after = token_0
    for tag, (keys, chip) in enumerate(late.groups + [(keys_last, chip_0)]):
        wire, landed = exchange_finish(chip, after, name=f"grad_chip_finish_{tag}")
        sum_and_share(keys, wire, landed, tag)
        for name, layer in keys:
            update(name, bufs[name], (layer, layer_count[name]) if layer_count[name] == 2 else None)
        after = updates[keys[-1][0]][0]

    small = REPLICATED + SHARDED_VECTORS
    flat = jnp.concatenate([g[n].reshape(-1) for n in small] + [loss_blk[0, :1]])
    room = 8 * SMALL_ROWS * D_MODEL
    flat = jnp.concatenate([flat, jnp.zeros((room - flat.shape[0],), F32)])
    flat, _ = lax.optimization_barrier((flat, after))
    reduced = all_reduce_small(flat.reshape(8, SMALL_ROWS, D_MODEL), name="grad_small_all_reduce").reshape(-1)
    loss = reduced[sum(g[n].size for n in small)]
    at = 0
    for n in small:
        size = g[n].size
        piece = reduced[at:at + size]
        at += size
        if n in SHARDED_VECTORS:
            per = D_MODEL // N_SHARDS
            grads[n] = lax.dynamic_slice(piece, (shard * per,), (per,)).reshape(weights[n].shape)
        else:
            grads[n] = piece.reshape(weights[n].shape)
        update(n, grads[n])
    for name, _, _ in MATRICES:
        grads[name] = bufs[name].reshape(weights[name].shape)
    delta = {n: updates[n][0].reshape(weights[n].shape) for n in WEIGHTS}
    new_m = {n: updates[n][1].reshape(weights[n].shape) for n in WEIGHTS}
    new_v = {n: updates[n][2].reshape(weights[n].shape) for n in WEIGHTS}
    return (loss, dx.reshape(x.shape), *[grads[n] for n in WEIGHTS], *[delta[n] for n in WEIGHTS],
            *[new_m[n] for n in WEIGHTS], *[new_v[n] for n in WEIGHTS])
```

```python
import jax
import jax.numpy as jnp
from jax import lax
from jax.experimental import pallas as pl
from jax.experimental.pallas import tpu as pltpu

F32 = jnp.float32
BF16 = jnp.bfloat16

D_MODEL = 1024
D_FF = 4096
PLE_DIM = 256
N_GROUPS = 8
CHUNK = 128
HEAD_DIM = 64
LANES = 128
ATT_K_BLOCK = 256
ATT_Q_BLOCK = 512
EPS = 1e-6
N_SHARDS = 4
VMEM_LIMIT = 56 * 1024 * 1024

ADAM_LR = 0.001
ADAM_B1 = 0.9
ADAM_B2 = 0.999
ADAM_EPS = 1e-08
ADAM_WD = 0.01
ADAM_STEP = 10

MESH = pl.DeviceIdType.MESH


_PREVIOUS = []


def _in_order(make, in_specs, args, views_of=()):
    previous = _PREVIOUS[-1] if _PREVIOUS else None
    if previous is not None and any(a is previous for a in (*args, *views_of)):
        previous = None
    if previous is None:
        result = make(lambda body: body, list(in_specs))(*args)
    else:
        count = len(args)

        def skip(body):
            return lambda *refs: body(*refs[:count], *refs[count + 1:])

        result = make(skip, list(in_specs) + [pl.BlockSpec(memory_space=pl.ANY)])(*args, previous)
    _PREVIOUS[:] = [jax.tree_util.tree_leaves(result)[-1]]
    return result


def _pcall(body, *, name, out_shape, grid=None, in_specs=None, out_specs=None, scratch_shapes=(),
           semantics=None, aliases=None, side_effects=False, num_prefetch=0):
    params = dict(vmem_limit_bytes=VMEM_LIMIT)
    if semantics is not None:
        params["dimension_semantics"] = semantics
    if side_effects:
        params["has_side_effects"] = True
    kwargs = {}
    if aliases:
        kwargs["input_output_aliases"] = aliases

    def make(wrap, specs):
        body_ = wrap(body)
        if num_prefetch:
            spec = pltpu.PrefetchScalarGridSpec(num_scalar_prefetch=num_prefetch, grid=grid, in_specs=specs,
                                                out_specs=out_specs, scratch_shapes=list(scratch_shapes))
            return pl.pallas_call(body_, name=name, out_shape=out_shape, grid_spec=spec,
                                  compiler_params=pltpu.CompilerParams(**params), **kwargs)
        more = dict(kwargs, in_specs=specs)
        if grid is not None:
            more["grid"] = grid
        if out_specs is not None:
            more["out_specs"] = out_specs
        return pl.pallas_call(body_, name=name, out_shape=out_shape, scratch_shapes=list(scratch_shapes),
                              compiler_params=pltpu.CompilerParams(**params), **more)

    return lambda *args: _in_order(make, in_specs, args)


def _sds(shape, dtype):
    return jax.ShapeDtypeStruct(shape, dtype)


_GELU_C = 0.7978845608028654
_GELU_A = 0.044715


def _gelu(x):
    inner = _GELU_C * (x + _GELU_A * (x * x * x))
    return 0.5 * x * (1.0 + jnp.tanh(inner))


def _gelu_grad(x):
    x2 = x * x
    t = jnp.tanh(_GELU_C * (x + _GELU_A * (x2 * x)))
    return 0.5 * (1.0 + t) + 0.5 * x * (1.0 - t * t) * (_GELU_C * (1.0 + 3.0 * _GELU_A * x2))


def _sigmoid(x):
    return 1.0 / (1.0 + jnp.exp(-x))


def _log_sigmoid(z):
    return jnp.minimum(z, 0.0) - jnp.log(1.0 + jnp.exp(-jnp.abs(z)))


def _dot(a, b):
    return jnp.dot(a, b, preferred_element_type=F32)


def _dot_nt(a, b):
    return lax.dot_general(a, b, (((1,), (1,)), ((), ())), preferred_element_type=F32)


def _dot_tn(a, b):
    return lax.dot_general(a, b, (((0,), (0,)), ((), ())), preferred_element_type=F32)


def _head_rstd(x):
    lane = lax.broadcasted_iota(jnp.int32, x.shape, 1)
    low = lane < HEAD_DIM
    sq = x * x
    s_lo = jnp.sum(jnp.where(low, sq, 0.0), axis=-1, keepdims=True)
    s_hi = jnp.sum(jnp.where(low, 0.0, sq), axis=-1, keepdims=True)
    ms = jnp.where(low, s_lo, s_hi) * (1.0 / HEAD_DIM)
    return lax.rsqrt(ms + EPS)


def _head_mean(x):
    lane = lax.broadcasted_iota(jnp.int32, x.shape, 1)
    low = lane < HEAD_DIM
    s_lo = jnp.sum(jnp.where(low, x, 0.0), axis=-1, keepdims=True)
    s_hi = jnp.sum(jnp.where(low, 0.0, x), axis=-1, keepdims=True)
    return jnp.where(low, s_lo, s_hi) * (1.0 / HEAD_DIM)


def _full(shape):
    zeros = (0,) * len(shape)
    return pl.BlockSpec(shape, lambda i: zeros)


def norm_matmul(x, g, w, *, name, epilogue="none", tm=512):
    t, d = x.shape
    sharded = w.ndim == 3
    per = w.shape[2] if sharded else w.shape[1]
    n = N_SHARDS * per if sharded else per
    tm = min(tm, t)

    def body(x_ref, g_ref, w_ref, h_ref, r_ref, *outs):
        xv = x_ref[...]
        r = lax.rsqrt(jnp.mean(xv * xv, axis=-1, keepdims=True) + EPS)
        h = ((xv * r) * g_ref[...]).astype(BF16)
        h_ref[...] = h
        r_ref[...] = r
        for s in range(N_SHARDS if sharded else 1):
            cols = slice(s * per, (s + 1) * per)
            y = _dot(h, w_ref[s] if sharded else w_ref[...])
            if epilogue == "none":
                outs[0][:, cols] = y
            else:
                a = jnp.maximum(y, 0.0)
                outs[0][:, cols] = a.astype(BF16)
                outs[1][:, cols] = (a * a).astype(BF16)

    row = lambda i: (i, 0)
    out_shape = [_sds((t, d), BF16), _sds((t, 1), F32)]
    out_specs = [pl.BlockSpec((tm, d), row), pl.BlockSpec((tm, 1), row)]
    if epilogue == "none":
        out_shape.append(_sds((t, n), F32))
        out_specs.append(pl.BlockSpec((tm, n), row))
    else:
        out_shape += [_sds((t, n), BF16), _sds((t, n), BF16)]
        out_specs += [pl.BlockSpec((tm, n), row)] * 2
    return _pcall(
        body, name=name, out_shape=out_shape, grid=(t // tm,),
        in_specs=[pl.BlockSpec((tm, d), row), _full((1, d)), _full(w.shape)],
        out_specs=out_specs, semantics=("parallel",))(x, g, w)


def matmul_residual(a, w, res, *, name, tm=512):
    t, k = a.shape
    n = w.shape[1]
    tm = min(tm, t)

    def body(a_ref, w_ref, res_ref, o_ref):
        o_ref[...] = res_ref[...] + _dot(a_ref[...], w_ref[...])

    row = lambda i: (i, 0)
    return _pcall(
        body, name=name, out_shape=_sds((t, n), F32), grid=(t // tm,),
        in_specs=[pl.BlockSpec((tm, k), row), _full(w.shape), pl.BlockSpec((tm, n), row)],
        out_specs=pl.BlockSpec((tm, n), row), semantics=("parallel",))(a, w, res)


def ple_forward(x, g, w_gate, p, w_proj, *, name, tm=256):
    t, d = x.shape
    tm = min(tm, t)

    def body(x_ref, g_ref, wg_ref, p_ref, wp_ref, h_ref, r_ref, gate_ref, pp_ref, o_ref):
        xv = x_ref[...]
        r = lax.rsqrt(jnp.mean(xv * xv, axis=-1, keepdims=True) + EPS)
        h = ((xv * r) * g_ref[...]).astype(BF16)
        h_ref[...] = h
        r_ref[...] = r
        gate = _sigmoid(_dot(h, wg_ref[...]))
        gate_ref[...] = gate
        pb = p_ref[...].astype(BF16)
        per = d // N_SHARDS
        for s in range(N_SHARDS):
            cols = slice(s * per, (s + 1) * per)
            pp = _dot(pb, wp_ref[s])
            pp_ref[:, cols] = pp.astype(BF16)
            o_ref[:, cols] = xv[:, cols] + pp * gate[:, cols]

    row = lambda i: (i, 0)
    fixed = lambda i: (0, 0)
    return _pcall(
        body, name=name,
        out_shape=[_sds((t, d), BF16), _sds((t, 1), F32), _sds((t, d), F32), _sds((t, d), BF16), _sds((t, d), F32)],
        grid=(t // tm,),
        in_specs=[pl.BlockSpec((tm, d), row), pl.BlockSpec((1, d), fixed), pl.BlockSpec((d, d), fixed),
                  pl.BlockSpec((tm, PLE_DIM), row),
                  pl.BlockSpec((N_SHARDS, PLE_DIM, d // N_SHARDS), lambda i: (0, 0, 0))],
        out_specs=[pl.BlockSpec((tm, d), row), pl.BlockSpec((tm, 1), row), pl.BlockSpec((tm, d), row),
                   pl.BlockSpec((tm, d), row), pl.BlockSpec((tm, d), row)],
        semantics=("parallel",))(x, g, w_gate, p, w_proj)


def _tril_mask():
    r = lax.broadcasted_iota(jnp.int32, (CHUNK, CHUNK), 0)
    c = lax.broadcasted_iota(jnp.int32, (CHUNK, CHUNK), 1)
    return c <= r


def _sgu_common(pre_ref, gv_ref, ws_ref):
    pre = pre_ref[...]
    pre_u, pre_v = pre[:, :D_MODEL], pre[:, D_MODEL:]
    u = _gelu(pre_u)
    v = _gelu(pre_v)
    r = lax.rsqrt(jnp.mean(v * v, axis=-1, keepdims=True) + EPS)
    vhat = v * r
    vn = (vhat * gv_ref[...]).astype(BF16)
    tril = _tril_mask()
    wm = [jnp.where(tril, ws_ref[g], 0.0).astype(BF16) for g in range(N_GROUPS)]
    return pre_u, pre_v, u, r, vhat, vn, wm, tril


def sgu_forward(pre, g_v, w_s, b_full, *, name):
    t = pre.shape[0]

    def body(pre_ref, gv_ref, ws_ref, b_ref, y_ref):
        _, _, u, _, _, vn, wm, _ = _sgu_common(pre_ref, gv_ref, ws_ref)
        for g in range(N_GROUPS):
            cols = slice(g * LANES, (g + 1) * LANES)
            mix = _dot(wm[g], vn[:, cols]) + b_ref[:, cols]
            y_ref[:, cols] = (u[:, cols] * mix).astype(BF16)

    return _pcall(
        body, name=name, out_shape=_sds((t, D_MODEL), BF16), grid=(t // CHUNK,),
        in_specs=[pl.BlockSpec((CHUNK, 2 * D_MODEL), lambda i: (i, 0)), pl.BlockSpec((1, D_MODEL), lambda i: (0, 0)),
                  pl.BlockSpec((N_GROUPS, CHUNK, CHUNK), lambda i: (0, 0, 0)),
                  pl.BlockSpec((CHUNK, D_MODEL), lambda i: (0, 0))],
        out_specs=pl.BlockSpec((CHUNK, D_MODEL), lambda i: (i, 0)),
        semantics=("parallel",))(pre, g_v, w_s, b_full)


def head_norm(pre, g128, *, name, col_block=0, scale=1.0, passthrough=False, tm=512):
    t = pre.shape[0]
    tm = min(tm, t)

    def body(*refs):
        if passthrough:
            x_ref, v_ref, g_ref, o_ref, vo_ref = refs
            vo_ref[...] = v_ref[...].astype(BF16)
        else:
            x_ref, g_ref, o_ref = refs
        g = g_ref[...] * scale
        for b in range(D_MODEL // LANES):
            cols = slice(b * LANES, (b + 1) * LANES)
            xv = x_ref[:, cols]
            o_ref[:, cols] = ((xv * _head_rstd(xv)) * g).astype(BF16)

    x_spec = pl.BlockSpec((tm, D_MODEL), lambda i: (i, col_block))
    g_spec = pl.BlockSpec((1, LANES), lambda i: (0, 0))
    o_spec = pl.BlockSpec((tm, D_MODEL), lambda i: (i, 0))
    if passthrough:
        return _pcall(body, name=name, out_shape=[_sds((t, D_MODEL), BF16)] * 2, grid=(t // tm,),
                      in_specs=[x_spec, pl.BlockSpec((tm, D_MODEL), lambda i: (i, 1)), g_spec],
                      out_specs=[o_spec, o_spec], semantics=("parallel",))(pre, pre, g128)
    return _pcall(body, name=name, out_shape=_sds((t, D_MODEL), BF16), grid=(t // tm,),
                  in_specs=[x_spec, g_spec], out_specs=o_spec, semantics=("parallel",))(pre, g128)


def _suffix_matrix(n):
    r = lax.broadcasted_iota(jnp.int32, (n, n), 0)
    c = lax.broadcasted_iota(jnp.int32, (n, n), 1)
    return jnp.where(r > c, 1.0, 0.0).astype(BF16)


def _prefix_matrix(n):
    r = lax.broadcasted_iota(jnp.int32, (n, n), 0)
    c = lax.broadcasted_iota(jnp.int32, (n, n), 1)
    return jnp.where(r < c, 1.0, 0.0).astype(BF16)


def _block_cumsum(a, tri):
    return _dot(a.astype(BF16), tri)


def _stacked_causal(nq, nk, shift):
    r = lax.broadcasted_iota(jnp.int32, (2 * nq, nk), 0)
    c = lax.broadcasted_iota(jnp.int32, (2 * nq, nk), 1)
    return c + shift < jnp.where(r >= nq, r - nq, r)


def _att_blocks(t):
    bq, bk = min(ATT_Q_BLOCK, t), min(ATT_K_BLOCK, t)
    return bq, bk, bq // bk


def _stack_heads(a, low):
    zero = jnp.zeros_like(a)
    return jnp.concatenate([jnp.where(low, a, zero), jnp.where(low, zero, a)], axis=0)


def stick_breaking_forward(q, k, v, *, name):
    t = q.shape[0]
    bq, bk, ratio = _att_blocks(t)

    def body(q_ref, k_ref, v_ref, o_ref):
        i = pl.program_id(1)
        low = lax.broadcasted_iota(jnp.int32, (bq, LANES), 1) < HEAD_DIM
        tri = _suffix_matrix(bk)
        qs = _stack_heads(q_ref[...], low)

        def block(j, carry, acc, causal=None):
            rows = pl.ds(pl.multiple_of(j * bk, bk), bk)
            z = _dot_nt(qs, k_ref[rows, :])
            ls = _log_sigmoid(z)
            lg = ls - z
            if causal is not None:
                lg = jnp.where(causal, lg, 0.0)
            s = ls + _block_cumsum(lg, tri) + carry
            a = jnp.exp(s)
            if causal is not None:
                a = jnp.where(causal, a, 0.0)
            acc = acc + _dot(a.astype(BF16), v_ref[rows, :])
            return carry + jnp.sum(lg, axis=-1, keepdims=True), acc

        state = (jnp.zeros((2 * bq, 1), F32), jnp.zeros((2 * bq, LANES), F32))
        for m in reversed(range(ratio)):
            state = block(ratio * i + m, state[0], state[1], _stacked_causal(bq, bk, m * bk))
        first = ratio * i

        def two_blocks(n, st):
            st = block(first - 1 - 2 * n, st[0], st[1])
            return block(first - 2 - 2 * n, st[0], st[1])

        state = lax.fori_loop(0, first // 2, two_blocks, state)
        _, acc = lax.fori_loop(0, first % 2, lambda n, st: block(0, st[0], st[1]), state)
        o_ref[...] = jnp.where(low, acc[:bq], acc[bq:]).astype(BF16)

    return _pcall(
        body, name=name, out_shape=_sds((t, D_MODEL), BF16), grid=(D_MODEL // LANES, t // bq),
        in_specs=[pl.BlockSpec((bq, LANES), lambda p, i: (i, p)), pl.BlockSpec((t, LANES), lambda p, i: (0, p)),
                  pl.BlockSpec((t, LANES), lambda p, i: (0, p))],
        out_specs=pl.BlockSpec((bq, LANES), lambda p, i: (i, p)),
        semantics=("parallel", "arbitrary"))(q, k, v)


def loss_forward(x, target, *, name, tm=512):
    t, d = x.shape
    tm = min(tm, t)

    def body(x_ref, t_ref, l_ref, dx_ref):
        @pl.when(pl.program_id(0) == 0)
        def _():
            l_ref[...] = jnp.zeros_like(l_ref)

        diff = x_ref[...] - t_ref[...]
        dx_ref[...] = diff * (1.0 / d)
        l_ref[...] += 0.5 * jnp.sum(jnp.mean(diff * diff, axis=-1, keepdims=True))

    return _pcall(
        body, name=name, out_shape=[_sds((8, LANES), F32), _sds((t, d), F32)], grid=(t // tm,),
        in_specs=[pl.BlockSpec((tm, d), lambda i: (i, 0))] * 2,
        out_specs=[pl.BlockSpec((8, LANES), lambda i: (0, 0)), pl.BlockSpec((tm, d), lambda i: (i, 0))],
        semantics=("arbitrary",))(x, target)


def matmul_nt(dy, w, *, name, mul=None, out_dtype=F32, tm=512):
    t, n = dy.shape
    k = w.shape[0]
    tm = min(tm, t)

    def body(*refs):
        if mul is None:
            dy_ref, w_ref, o_ref = refs
        else:
            dy_ref, w_ref, m_ref, o_ref = refs
        y = _dot_nt(dy_ref[...].astype(BF16), w_ref[...])
        if mul is not None:
            y = y * (2.0 * m_ref[...].astype(F32))
        o_ref[...] = y.astype(out_dtype)

    row = lambda i: (i, 0)
    in_specs = [pl.BlockSpec((tm, n), row), _full(w.shape)]
    args = [dy, w]
    if mul is not None:
        in_specs.append(pl.BlockSpec((tm, k), row))
        args.append(mul)
    return _pcall(body, name=name, out_shape=_sds((t, k), out_dtype), grid=(t // tm,), in_specs=in_specs,
                  out_specs=pl.BlockSpec((tm, k), row), semantics=("parallel",))(*args)


def matmul_tn(a, dy, *, name, col_shards, tk=512):
    t, k = a.shape
    n = dy.shape[1]
    if col_shards:
        tn = n // N_SHARDS

        def body(a_ref, dy_ref, o_ref):
            o_ref[...] = _dot_tn(a_ref[...].astype(BF16), dy_ref[...].astype(BF16))

        return _pcall(body, name=name, out_shape=_sds((N_SHARDS, k, tn), F32), grid=(N_SHARDS,),
                      in_specs=[_full((t, k)), pl.BlockSpec((t, tn), lambda j: (0, j))],
                      out_specs=pl.BlockSpec((None, k, tn), lambda j: (j, 0, 0)), semantics=("parallel",))(a, dy)

    tk = min(tk, k)

    def body(a_ref, dy_ref, o_ref, dy_bf):
        @pl.when(pl.program_id(0) == 0)
        def _():
            dy_bf[...] = dy_ref[...].astype(BF16)

        o_ref[...] = _dot_tn(a_ref[...].astype(BF16), dy_bf[...])

    return _pcall(body, name=name, out_shape=_sds((k, n), F32), grid=(k // tk,),
                  in_specs=[pl.BlockSpec((t, tk), lambda i: (0, i)), _full((t, n))],
                  out_specs=pl.BlockSpec((tk, n), lambda i: (i, 0)),
                  scratch_shapes=[pltpu.VMEM((t, n), BF16)], semantics=("arbitrary",))(a, dy)


def norm_backward(dpre, w, x, g, rstd, dx_out, *, name, tm=512):
    t, d = x.shape
    n = dpre.shape[1]
    tm = min(tm, t)
    if w.ndim == 3:
        w_spec = pl.BlockSpec(w.shape, lambda i: (0, 0, 0))
    else:
        w_spec = pl.BlockSpec(w.shape, lambda i: (0, 0))

    def body(dp_ref, w_ref, x_ref, g_ref, r_ref, dxo_ref, dx_ref, dg_ref):
        @pl.when(pl.program_id(0) == 0)
        def _():
            dg_ref[...] = jnp.zeros_like(dg_ref)

        if w.ndim == 3:
            per = n // N_SHARDS
            dh = _dot_nt(dp_ref[:, 0:per], w_ref[0])
            for s in range(1, N_SHARDS):
                dh = dh + _dot_nt(dp_ref[:, s * per:(s + 1) * per], w_ref[s])
        else:
            dh = _dot_nt(dp_ref[...], w_ref[...])
        r = r_ref[...]
        xn = x_ref[...] * r
        dg_ref[...] += jnp.sum(dh * xn, axis=0, keepdims=True)
        dxn = dh * g_ref[...]
        dx = r * (dxn - xn * jnp.mean(dxn * xn, axis=-1, keepdims=True))
        dx_ref[...] = dxo_ref[...] + dx

    row = lambda i: (i, 0)
    fixed = lambda i: (0, 0)
    return _pcall(
        body, name=name, out_shape=[_sds((t, d), F32), _sds((1, d), F32)], grid=(t // tm,),
        in_specs=[pl.BlockSpec((tm, n), row), w_spec, pl.BlockSpec((tm, d), row),
                  pl.BlockSpec((1, d), fixed), pl.BlockSpec((tm, 1), row), pl.BlockSpec((tm, d), row)],
        out_specs=[pl.BlockSpec((tm, d), row), pl.BlockSpec((1, d), fixed)],
        semantics=("arbitrary",))(dpre, w, x, g, rstd, dx_out)


def ple_backward(dx, gate, pp, *, name, tm=512):
    t, d = dx.shape
    tm = min(tm, t)

    def body(dx_ref, gate_ref, pp_ref, dg_ref, dp_ref):
        dxv = dx_ref[...]
        gate = gate_ref[...]
        dg_ref[...] = (dxv * pp_ref[...].astype(F32) * (gate * (1.0 - gate))).astype(BF16)
        dp_ref[...] = (dxv * gate).astype(BF16)

    spec = pl.BlockSpec((tm, d), lambda i: (i, 0))
    return _pcall(body, name=name, out_shape=[_sds((t, d), BF16)] * 2, grid=(t // tm,), in_specs=[spec] * 3,
                  out_specs=[spec] * 2, semantics=("parallel",))(dx, gate, pp)


def sgu_backward(dy, pre, g_v, w_s, b_full, *, name):
    t = pre.shape[0]
    n_chunks = t // CHUNK

    def body(dy_ref, pre_ref, gv_ref, ws_ref, b_ref, dpre_ref, dws_ref, db_ref, dgv_ref, dvn_s, dbf_s):
        step = pl.program_id(0)

        @pl.when(step == 0)
        def _():
            dws_ref[...] = jnp.zeros_like(dws_ref)
            dgv_ref[...] = jnp.zeros_like(dgv_ref)
            dbf_s[...] = jnp.zeros_like(dbf_s)

        pre_u, pre_v, u, r, vhat, vn, wm, tril = _sgu_common(pre_ref, gv_ref, ws_ref)
        dyv = dy_ref[...]
        for g in range(N_GROUPS):
            cols = slice(g * LANES, (g + 1) * LANES)
            mix = _dot(wm[g], vn[:, cols]) + b_ref[:, cols]
            dmix = dyv[:, cols] * u[:, cols]
            dmix_b = dmix.astype(BF16)
            du = dyv[:, cols] * mix
            dpre_ref[:, cols] = (du * _gelu_grad(pre_u[:, cols])).astype(BF16)
            dws_ref[g] += jnp.where(tril, _dot_nt(dmix_b, vn[:, cols]), 0.0)
            dbf_s[:, cols] += dmix
            dvn_s[:, cols] = _dot_tn(wm[g], dmix_b)
        dvn = dvn_s[...]
        dgv_ref[...] += jnp.sum(dvn * vhat, axis=0, keepdims=True)
        dxn = dvn * gv_ref[...]
        dv = r * (dxn - vhat * jnp.mean(dxn * vhat, axis=-1, keepdims=True))
        dpre_ref[:, D_MODEL:] = (dv * _gelu_grad(pre_v)).astype(BF16)

        @pl.when(step == n_chunks - 1)
        def _():
            lane = lax.broadcasted_iota(jnp.int32, (CHUNK, LANES), 1)
            acc = jnp.zeros((CHUNK, LANES), F32)
            for g in range(N_GROUPS):
                s = jnp.sum(dbf_s[:, g * LANES:(g + 1) * LANES], axis=-1, keepdims=True)
                acc = jnp.where(lane == g, s, acc)
            db_ref[...] = acc

    fixed2 = lambda i: (0, 0)
    return _pcall(
        body, name=name,
        out_shape=[_sds((t, 2 * D_MODEL), BF16), _sds((N_GROUPS, CHUNK, CHUNK), F32), _sds((CHUNK, LANES), F32),
                   _sds((1, D_MODEL), F32)],
        grid=(n_chunks,),
        in_specs=[pl.BlockSpec((CHUNK, D_MODEL), lambda i: (i, 0)), pl.BlockSpec((CHUNK, 2 * D_MODEL), lambda i: (i, 0)),
                  pl.BlockSpec((1, D_MODEL), fixed2), pl.BlockSpec((N_GROUPS, CHUNK, CHUNK), lambda i: (0, 0, 0)),
                  pl.BlockSpec((CHUNK, D_MODEL), fixed2)],
        out_specs=[pl.BlockSpec((CHUNK, 2 * D_MODEL), lambda i: (i, 0)),
                   pl.BlockSpec((N_GROUPS, CHUNK, CHUNK), lambda i: (0, 0, 0)), pl.BlockSpec((CHUNK, LANES), fixed2),
                   pl.BlockSpec((1, D_MODEL), fixed2)],
        scratch_shapes=[pltpu.VMEM((CHUNK, D_MODEL), F32), pltpu.VMEM((CHUNK, D_MODEL), F32)],
        semantics=("arbitrary",))(dy, pre, g_v, w_s, b_full)


def head_norm_backward(dy, pre, g128, *, name, col_block=0, scale=1.0, passthrough=None, tm=512):
    t = dy.shape[0]
    tm = min(tm, t)
    width = 2 * D_MODEL if passthrough is not None else D_MODEL

    def body(*refs):
        if passthrough is not None:
            dy_ref, x_ref, g_ref, dv_ref, o_ref, dg_ref = refs
            o_ref[:, D_MODEL:] = dv_ref[...].astype(BF16)
        else:
            dy_ref, x_ref, g_ref, o_ref, dg_ref = refs

        @pl.when(pl.program_id(0) == 0)
        def _():
            dg_ref[...] = jnp.zeros_like(dg_ref)

        g = g_ref[...]
        dg = jnp.zeros((1, LANES), F32)
        for b in range(D_MODEL // LANES):
            cols = slice(b * LANES, (b + 1) * LANES)
            xv = x_ref[:, cols]
            r = _head_rstd(xv)
            xn = xv * r
            dyv = dy_ref[:, cols] * scale
            dg = dg + jnp.sum(dyv * xn, axis=0, keepdims=True)
            dxn = dyv * g
            o_ref[:, cols] = (r * (dxn - xn * _head_mean(dxn * xn))).astype(BF16)
        dg_ref[...] += dg

    row = lambda i: (i, 0)
    in_specs = [pl.BlockSpec((tm, D_MODEL), row), pl.BlockSpec((tm, D_MODEL), lambda i: (i, col_block)),
                pl.BlockSpec((1, LANES), lambda i: (0, 0))]
    args = [dy, pre, g128]
    if passthrough is not None:
        in_specs.append(pl.BlockSpec((tm, D_MODEL), row))
        args.append(passthrough)
    return _pcall(body, name=name, out_shape=[_sds((t, width), BF16), _sds((1, LANES), F32)], grid=(t // tm,),
                  in_specs=in_specs,
                  out_specs=[pl.BlockSpec((tm, width), row), pl.BlockSpec((1, LANES), lambda i: (0, 0))],
                  semantics=("arbitrary",))(*args)


def stick_breaking_backward(q, k, v, do, *, name):
    t = q.shape[0]
    bq, bk, ratio = _att_blocks(t)

    def body(q_ref, k_ref, v_ref, do_ref, dq_ref, dk_ref, dv_ref, s_buf, sg_buf):
        i = pl.program_id(1)

        @pl.when(i == 0)
        def _():
            dk_ref[...] = jnp.zeros_like(dk_ref)
            dv_ref[...] = jnp.zeros_like(dv_ref)

        low = lax.broadcasted_iota(jnp.int32, (bq, LANES), 1) < HEAD_DIM
        suffix = _suffix_matrix(bk)
        prefix = _prefix_matrix(bk)
        qs = _stack_heads(q_ref[...], low)
        dos = _stack_heads(do_ref[...], low)
        first = ratio * i

        def log_weights(j, carry, causal=None):
            rows = pl.ds(pl.multiple_of(j * bk, bk), bk)
            z = _dot_nt(qs, k_ref[rows, :])
            ls = _log_sigmoid(z)
            lg = ls - z
            if causal is not None:
                lg = jnp.where(causal, lg, 0.0)
            s_buf[j] = ls + _block_cumsum(lg, suffix) + carry
            sg_buf[j] = jnp.exp(ls)
            return carry + jnp.sum(lg, axis=-1, keepdims=True)

        carry = jnp.zeros((2 * bq, 1), F32)
        for m in reversed(range(ratio)):
            carry = log_weights(first + m, carry, _stacked_causal(bq, bk, m * bk))
        carry = lax.fori_loop(0, first // 2,
                              lambda n, c: log_weights(first - 2 - 2 * n, log_weights(first - 1 - 2 * n, c)), carry)
        lax.fori_loop(0, first % 2, lambda n, c: log_weights(0, c), carry)

        def grads(j, pcarry, dq_acc, causal=None):
            rows = pl.ds(pl.multiple_of(j * bk, bk), bk)
            a = jnp.exp(s_buf[j])
            if causal is not None:
                a = jnp.where(causal, a, 0.0)
            sg = sg_buf[j]
            ds = _dot_nt(dos, v_ref[rows, :]) * a
            before = _block_cumsum(ds, prefix) + pcarry
            if causal is not None:
                before = jnp.where(causal, before, 0.0)
            dz = (ds - sg * (ds + before)).astype(BF16)
            dq_acc = dq_acc + _dot(dz, k_ref[rows, :])
            dk_ref[rows, :] += _dot_tn(dz, qs)
            dv_ref[rows, :] += _dot_tn(a.astype(BF16), dos)
            return pcarry + jnp.sum(ds, axis=-1, keepdims=True), dq_acc

        def two_blocks(n, st):
            st = grads(2 * n, st[0], st[1])
            return grads(2 * n + 1, st[0], st[1])

        state = lax.fori_loop(0, first // 2, two_blocks,
                              (jnp.zeros((2 * bq, 1), F32), jnp.zeros((2 * bq, LANES), F32)))
        state = lax.fori_loop(0, first % 2, lambda n, st: grads(first - 1, st[0], st[1]), state)
        for m in range(ratio):
            state = grads(first + m, state[0], state[1], _stacked_causal(bq, bk, m * bk))
        dq_ref[...] = jnp.where(low, state[1][:bq], state[1][bq:])

    full = pl.BlockSpec((t, LANES), lambda p, i: (0, p))
    qblk = pl.BlockSpec((bq, LANES), lambda p, i: (i, p))
    return _pcall(
        body, name=name, out_shape=[_sds((t, D_MODEL), F32)] * 3, grid=(D_MODEL // LANES, t // bq),
        in_specs=[qblk, full, full, qblk], out_specs=[qblk, full, full],
        scratch_shapes=[pltpu.VMEM((t // bk, 2 * bq, bk), F32), pltpu.VMEM((t // bk, 2 * bq, bk), F32)],
        semantics=("parallel", "arbitrary"))(q, k, v, do)


def _mlp_backward(dx, saved, g, w_up, w_down, tag):
    x, h, r, a, a2 = saved
    d_w_down = matmul_tn(a2, dx, name=f"d_w_down_{tag}", col_shards=False)
    dpre = matmul_nt(dx, w_down, name=f"d_mlp_act_{tag}", mul=a, out_dtype=BF16)
    d_w_up = matmul_tn(h, dpre, name=f"d_w_up_{tag}", col_shards=True)
    dx, d_g = norm_backward(dpre, w_up, x, g, r, dx, name=f"d_mlp_norm_{tag}")
    return dx, d_w_up, d_w_down, d_g


def _ple_backward(dx, saved, p, g, w_gate, tag):
    x, h, r, gate, pp = saved
    dgate, dproj = ple_backward(dx, gate, pp, name=f"d_ple_{tag}")
    d_w_proj = matmul_tn(p, dproj, name=f"d_w_ple_proj_{tag}", col_shards=True)
    d_w_gate = matmul_tn(h, dgate, name=f"d_w_ple_gate_{tag}", col_shards=False)
    dx, d_g = norm_backward(dgate, w_gate, x, g, r, dx, name=f"d_ple_norm_{tag}")
    return dx, d_w_gate, d_w_proj, d_g


def local_step(x, p, target, w, late=None):
    row = lambda v: v.reshape(1, -1)
    g128 = lambda v: jnp.tile(v.reshape(1, HEAD_DIM), (1, 2))
    scale = HEAD_DIM ** -0.5
    b_full = jnp.repeat(jnp.transpose(w["b_spatial"][0]), LANES, axis=1)
    w_s = w["w_spatial"][0]

    mats = {}
    for name, value in w.items():
        if isinstance(value, tuple):
            mats.update({(name, layer): v for layer, v in enumerate(value)})
    if "w_kv" in w:
        mats[("w_kv", 0)] = w["w_kv"]

    def fetch(name, layer, after):
        if (name, layer) not in mats:
            mats.update(late.weights(name, layer, after))
        return mats[(name, layer)]

    def mlp_forward(x_in, layer):
        h, r, a, a2 = norm_matmul(x_in, row(w["ln_mlp"][layer]), fetch("w_up", layer, x_in), name=f"mlp_up_{layer}",
                                  epilogue="relu2")
        return matmul_residual(a2, fetch("w_down", layer, a2), x_in, name=f"mlp_down_{layer}"), (x_in, h, r, a, a2)

    def ple(x_in, layer):
        return ple_forward(x_in, row(w["ln_ple"][layer]), fetch("w_ple_gate", layer, x_in), p[layer],
                           fetch("w_ple_proj", layer, x_in), name=f"ple_{layer}")

    x0 = x
    h_a, r_a, pre_a = norm_matmul(x0, row(w["ln_mix_a"][0]), fetch("w_in_a", 0, x0), name="sgu_in")
    y_a = sgu_forward(pre_a, row(w["g_v_a"][0]), w_s, b_full, name="sgu_mix")
    x1 = matmul_residual(y_a, fetch("w_out_a", 0, y_a), x0, name="sgu_out")
    x2, mlp0 = mlp_forward(x1, 0)
    ple0 = ple(x2, 0)
    x3 = ple0[4]
    h_kv, r_kv, kv_pre = norm_matmul(x3, row(w["ln_kv"]), fetch("w_kv", 0, x3), name="kv_proj")
    k_n, v_b = head_norm(kv_pre, g128(w["g_k"]), name="k_norm", passthrough=True)
    h_q, r_q, q_pre = norm_matmul(x3, row(w["ln_mix_b"][0]), fetch("w_q", 0, k_n), name="q_proj")
    q_n = head_norm(q_pre, g128(w["g_q"][0]), name="q_norm", scale=scale)
    o = stick_breaking_forward(q_n, k_n, v_b, name="sb_fwd")
    x4 = matmul_residual(o, fetch("w_out_b", 0, o), x3, name="sb_out")
    x5, mlp1 = mlp_forward(x4, 1)
    ple1 = ple(x5, 1)
    x6 = ple1[4]
    loss_blk, dx = loss_forward(x6, target, name="loss")

    g = {}
    dx, dwg1, dwp1, dlnp1 = _ple_backward(dx, (x5,) + tuple(ple1[:4]), p[1], row(w["ln_ple"][1]),
                                          mats[("w_ple_gate", 1)], 1)
    dx, dwu1, dwd1, dlnm1 = _mlp_backward(dx, mlp1, row(w["ln_mlp"][1]), mats[("w_up", 1)], mats[("w_down", 1)], 1)
    g["w_out_b"] = matmul_tn(o, dx, name="d_w_out_b", col_shards=False)
    do = matmul_nt(dx, mats[("w_out_b", 0)], name="d_sb_out", out_dtype=BF16)
    dq_n, dk_n, dv = stick_breaking_backward(q_n, k_n, v_b, do, name="sb_bwd")
    dq_pre, dgq = head_norm_backward(dq_n, q_pre, g128(w["g_q"][0]), name="d_q_norm", scale=scale)
    dkv_pre, dgk = head_norm_backward(dk_n, kv_pre, g128(w["g_k"]), name="d_k_norm", passthrough=dv)
    g["w_q"] = matmul_tn(h_q, dq_pre, name="d_w_q", col_shards=False)
    g["w_kv"] = matmul_tn(h_kv, dkv_pre, name="d_w_kv", col_shards=True)
    dx, g["ln_mix_b"] = norm_backward(dq_pre, mats[("w_q", 0)], x3, row(w["ln_mix_b"][0]), r_q, dx, name="d_q_in")
    dx, g["ln_kv"] = norm_backward(dkv_pre, mats[("w_kv", 0)], x3, row(w["ln_kv"]), r_kv, dx, name="d_kv_in")
    g["g_q"] = dgq[:, :HEAD_DIM] + dgq[:, HEAD_DIM:]
    g["g_k"] = (dgk[:, :HEAD_DIM] + dgk[:, HEAD_DIM:]).reshape(HEAD_DIM)
    g["ln_kv"] = g["ln_kv"].reshape(D_MODEL)
    if late is not None:
        late.pair_start({("w_kv", 0): g["w_kv"], ("w_q", 0): g["w_q"], ("w_out_b", 0): g["w_out_b"],
                         ("w_up", 1): dwu1, ("w_down", 1): dwd1, ("w_ple_gate", 1): dwg1, ("w_ple_proj", 1): dwp1}, dx)
    dx, dwg0, dwp0, dlnp0 = _ple_backward(dx, (x2,) + tuple(ple0[:4]), p[0], row(w["ln_ple"][0]),
                                          mats[("w_ple_gate", 0)], 0)
    if late is not None:
        late.chip_start(dx)
    dx, dwu0, dwd0, dlnm0 = _mlp_backward(dx, mlp0, row(w["ln_mlp"][0]), mats[("w_up", 0)], mats[("w_down", 0)], 0)
    if late is not None:
        late.pair_start({("w_up", 0): dwu0, ("w_down", 0): dwd0, ("w_ple_gate", 0): dwg0, ("w_ple_proj", 0): dwp0}, dx)
    g["w_out_a"] = matmul_tn(y_a, dx, name="d_w_out_a", col_shards=False)
    dy_a = matmul_nt(dx, mats[("w_out_a", 0)], name="d_sgu_out")
    dpre_a, dws, db, g["g_v_a"] = sgu_backward(dy_a, pre_a, row(w["g_v_a"][0]), w_s, b_full, name="d_sgu_mix")
    if late is not None:
        late.chip_start(dpre_a)
    g["w_in_a"] = matmul_tn(h_a, dpre_a, name="d_w_in_a", col_shards=True)
    dx, g["ln_mix_a"] = norm_backward(dpre_a, mats[("w_in_a", 0)], x0, row(w["ln_mix_a"][0]), r_a, dx, name="d_sgu_in")
    g["w_spatial"] = dws[None]
    g["b_spatial"] = jnp.transpose(db[:, :N_GROUPS])[None]
    g["w_up"] = (dwu0, dwu1)
    g["w_down"] = (dwd0, dwd1)
    g["w_ple_gate"] = (dwg0, dwg1)
    g["w_ple_proj"] = (dwp0, dwp1)
    g["ln_mlp"] = jnp.concatenate([dlnm0, dlnm1], axis=0)
    g["ln_ple"] = jnp.concatenate([dlnp0, dlnp1], axis=0)
    return loss_blk, dx, g


ANY = pl.BlockSpec(memory_space=pl.ANY)


def _place():
    x, y, c = lax.axis_index("x"), lax.axis_index("y"), lax.axis_index("c")
    others = [(1 - x, y), (x, 1 - y), (1 - x, 1 - y)]
    return x, y, c, 2 * x + y, others


def cast_into_slot(w3, layer, slot, *, name, after=None, tm=512):
    _, r, c = w3.shape
    tm = min(tm, r)

    def body(slot_ref, w_ref, *rest):
        rest[-1][...] = w_ref[...].astype(BF16)

    in_specs = [pl.BlockSpec((None, tm, c), lambda i, s: (layer, i, 0))]
    args = [slot, w3]
    if after is not None:
        in_specs.append(ANY)
        args.append(after)
    return _pcall(body, name=name, out_shape=_sds((N_SHARDS, r, c), BF16), grid=(r // tm,), num_prefetch=1,
                  in_specs=in_specs, out_specs=pl.BlockSpec((None, tm, c), lambda i, s: (s[0], i, 0)),
                  semantics=("parallel",))(*args)


def gather_shards(mats, vecs, *, name):
    nm, nv = len(mats), len(vecs)
    halves = [m.reshape(N_SHARDS, 2, m.shape[1] // 2, m.shape[2]) for m in mats]

    def body(*refs):
        vsrc = refs[nm:nm + nv]
        out, vout = refs[nm + nv:2 * nm + nv], refs[2 * nm + nv:2 * (nm + nv)]
        send, recv, vsend, vrecv, loc = refs[2 * (nm + nv):]
        x, y, c, s_me, others = _place()
        sib = (x, y, 1 - c)

        def ici(l, k):
            ox, oy = others[k]
            return pltpu.make_async_remote_copy(out[l].at[s_me, c], out[l].at[s_me, c], send.at[l, k], recv.at[l, k],
                                                device_id=(ox, oy, c), device_id_type=MESH)

        def landed(l, k, half):
            ox, oy = others[k]
            return out[l].at[2 * ox + oy, half]

        def passed_on(l, k):
            return pltpu.make_async_remote_copy(landed(l, k, c), landed(l, k, c), send.at[l, 3 + k], recv.at[l, 3 + k],
                                                device_id=sib, device_id_type=MESH)

        def vec(l, k):
            ox, oy = others[k]
            return pltpu.make_async_remote_copy(vsrc[l], vout[l].at[s_me], vsend.at[l, k], vrecv.at[l, k],
                                                device_id=(ox, oy, c), device_id_type=MESH)

        for l in range(nm):
            for k in range(3):
                ici(l, k).start()
        for l in range(nv):
            for k in range(3):
                vec(l, k).start()
        for l in range(nv):
            own = pltpu.make_async_copy(vsrc[l], vout[l].at[s_me], loc)
            own.start()
            own.wait()
        for l in range(nm):
            for k in range(3):
                pltpu.make_async_remote_copy(landed(l, k, c), landed(l, k, c), send.at[l, k], recv.at[l, k],
                                             device_id=sib, device_id_type=MESH).wait_recv()
                passed_on(l, k).start()
        for l in range(nm):
            for k in range(3):
                pltpu.make_async_remote_copy(landed(l, k, 1 - c), landed(l, k, 1 - c), send.at[l, 3 + k],
                                             recv.at[l, 3 + k], device_id=sib, device_id_type=MESH).wait_recv()
        for l in range(nv):
            for k in range(3):
                ox, oy = others[k]
                pltpu.make_async_remote_copy(vsrc[l], vout[l].at[2 * ox + oy], vsend.at[l, k], vrecv.at[l, k],
                                             device_id=sib, device_id_type=MESH).wait_recv()
        for l in range(nm):
            for k in range(3):
                ici(l, k).wait_send()
                passed_on(l, k).wait_send()
        for l in range(nv):
            for k in range(3):
                vec(l, k).wait_send()

    out_shape = [_sds(h.shape, BF16) for h in halves] + [_sds((N_SHARDS,) + v.shape, F32) for v in vecs]
    res = _pcall(body, name=name, out_shape=out_shape, in_specs=[ANY] * (nm + nv), out_specs=[ANY] * (nm + nv),
                 scratch_shapes=[pltpu.SemaphoreType.DMA((max(nm, 1), 6)), pltpu.SemaphoreType.DMA((max(nm, 1), 6)),
                                 pltpu.SemaphoreType.DMA((max(nv, 1), 3)), pltpu.SemaphoreType.DMA((max(nv, 1), 3)),
                                 pltpu.SemaphoreType.DMA(())],
                 aliases={l: l for l in range(nm)}, side_effects=True)(*halves, *vecs)
    return [r.reshape(m.shape) for r, m in zip(res[:nm], mats)], list(res[nm:])


HBM = pl.BlockSpec(memory_space=pltpu.HBM)
SEM = pl.BlockSpec(memory_space=pltpu.SEMAPHORE)
DATAFLOW = pltpu.SideEffectType.DATAFLOW_SIDE_EFFECTING


def _split_call(body, *, name, out_shape, in_specs, out_specs, aliases, views_of=()):
    def make(wrap, specs):
        body_ = wrap(body)
        return pl.pallas_call(body_, name=name, out_shape=out_shape, in_specs=specs, out_specs=out_specs,
                              input_output_aliases=aliases,
                              compiler_params=pltpu.CompilerParams(has_side_effects=DATAFLOW))

    return lambda *args: _in_order(make, in_specs, args, views_of)


def _token_shape():
    return jax.ShapeDtypeStruct((8, LANES), F32)


def gather_start(mats, after, *, name):
    n = len(mats)
    halves = [pltpu.with_memory_space_constraint(m.reshape(N_SHARDS, 2, m.shape[1] // 2, m.shape[2]), pltpu.HBM)
              for m in mats]

    def body(*refs):
        send, recv = refs[n + 1], refs[n + 2]
        out, token = refs[n + 3:2 * n + 3], refs[2 * n + 3]
        x, y, c, s_me, others = _place()
        for l in range(n):
            for k in range(3):
                ox, oy = others[k]
                pltpu.make_async_remote_copy(out[l].at[s_me, c], out[l].at[s_me, c], send.at[3 * l + k],
                                             recv.at[3 * l + k], device_id=(ox, oy, c), device_id_type=MESH).start()
        token[...] = jnp.zeros_like(token)

    res = _split_call(
        body, name=name,
        out_shape=(pltpu.SemaphoreType.DMA((3 * n,)), pltpu.SemaphoreType.DMA((3 * n,)),
                   *[pltpu.HBM(h.shape, BF16) for h in halves], _token_shape()),
        in_specs=[HBM] * n + [ANY], out_specs=(SEM, SEM, *[HBM] * n, pl.BlockSpec(memory_space=pltpu.VMEM)),
        aliases={l: 2 + l for l in range(n)}, views_of=mats)(*halves, after)
    return res[0], res[1], list(res[2:2 + n]), res[2 + n]


def gather_pass_on(bufs, send_a, recv_a, after, *, name, base=0):
    n = len(bufs)

    def body(*refs):
        send_a, recv_a = refs[n], refs[n + 1]
        out = refs[n + 3:2 * n + 3]
        send_b, recv_b, token = refs[2 * n + 3:]
        x, y, c, s_me, others = _place()
        for l in range(n):
            for k in range(3):
                ox, oy = others[k]
                landed, i = out[l].at[2 * ox + oy, c], 3 * l + k
                pltpu.make_async_remote_copy(landed, landed, send_a.at[3 * base + i], recv_a.at[3 * base + i],
                                             device_id=(x, y, 1 - c), device_id_type=MESH).wait_recv()
                pltpu.make_async_remote_copy(landed, landed, send_b.at[i], recv_b.at[i],
                                             device_id=(x, y, 1 - c), device_id_type=MESH).start()
        for l in range(n):
            for k in range(3):
                mine, i = out[l].at[s_me, c], 3 * (base + l) + k
                pltpu.make_async_remote_copy(mine, mine, send_a.at[i], recv_a.at[i],
                                             device_id=(x, y, 1 - c), device_id_type=MESH).wait_send()
        token[...] = jnp.zeros_like(token)

    res = _split_call(
        body, name=name,
        out_shape=(*[pltpu.HBM(b.shape, BF16) for b in bufs], pltpu.SemaphoreType.DMA((3 * n,)),
                   pltpu.SemaphoreType.DMA((3 * n,)), _token_shape()),
        in_specs=[HBM] * n + [SEM, SEM, ANY],
        out_specs=(*[HBM] * n, SEM, SEM, pl.BlockSpec(memory_space=pltpu.VMEM)),
        aliases={l: l for l in range(n)})(*bufs, send_a, recv_a, after)
    return list(res[:n]), res[n], res[n + 1], res[n + 2]


def gather_finish(bufs, send_b, recv_b, after, shapes, *, name):
    n = len(bufs)

    def body(*refs):
        send_b, recv_b = refs[n], refs[n + 1]
        out = refs[n + 3:]
        x, y, c, _, others = _place()
        for l in range(n):
            for k in range(3):
                ox, oy = others[k]
                theirs, mine, i = out[l].at[2 * ox + oy, 1 - c], out[l].at[2 * ox + oy, c], 3 * l + k
                pltpu.make_async_remote_copy(theirs, theirs, send_b.at[i], recv_b.at[i],
                                             device_id=(x, y, 1 - c), device_id_type=MESH).wait_recv()
                pltpu.make_async_remote_copy(mine, mine, send_b.at[i], recv_b.at[i],
                                             device_id=(x, y, 1 - c), device_id_type=MESH).wait_send()

    res = _split_call(
        body, name=name, out_shape=tuple(pltpu.HBM(b.shape, BF16) for b in bufs),
        in_specs=[HBM] * n + [SEM, SEM, ANY], out_specs=tuple([HBM] * n),
        aliases={l: l for l in range(n)})(*bufs, send_b, recv_b, after)
    return [r.reshape(s) for r, s in zip(res, shapes)]


def exchange_start(srcs, dst_shapes, dst_dtype, plan, count, after, *, name):
    n, m = len(srcs), len(dst_shapes)
    given = list(srcs)
    srcs = [pltpu.with_memory_space_constraint(s, pltpu.HBM) for s in srcs]
    lands = [pltpu.with_memory_space_constraint(lax.empty(s, dst_dtype), pltpu.HBM) for s in dst_shapes]

    def body(*refs):
        send, recv = refs[n + m + 1], refs[n + m + 2]
        src, dst, token = refs[n + m + 3:2 * n + m + 3], refs[2 * n + m + 3:2 * (n + m) + 3], refs[2 * (n + m) + 3]
        for i, (s, d, dev) in enumerate(plan(_place(), src, dst)):
            pltpu.make_async_remote_copy(s, d, send.at[i], recv.at[i], device_id=dev, device_id_type=MESH).start()
        token[...] = jnp.zeros_like(token)

    res = _split_call(
        body, name=name,
        out_shape=(pltpu.SemaphoreType.DMA((count,)), pltpu.SemaphoreType.DMA((count,)),
                   *[pltpu.HBM(s.shape, s.dtype) for s in srcs], *[pltpu.HBM(s, dst_dtype) for s in dst_shapes],
                   _token_shape()),
        in_specs=[HBM] * (n + m) + [ANY],
        out_specs=(SEM, SEM, *[HBM] * (n + m), pl.BlockSpec(memory_space=pltpu.VMEM)),
        aliases={i: 2 + i for i in range(n + m)}, views_of=given)(*srcs, *lands, after)
    return (list(res[2:2 + n]), list(res[2 + n:2 + n + m]), res[0], res[1], plan), res[2 + n + m]


def exchange_finish(state, after, *, name):
    srcs, lands, send, recv, plan = state
    n, m = len(srcs), len(lands)

    def body(*refs):
        send, recv = refs[n + m], refs[n + m + 1]
        src, dst = refs[n + m + 3:2 * n + m + 3], refs[2 * n + m + 3:]
        for i, (s, d, dev) in enumerate(plan(_place(), src, dst)):
            pltpu.make_async_remote_copy(s, d, send.at[i], recv.at[i], device_id=dev, device_id_type=MESH).wait()

    res = _split_call(
        body, name=name,
        out_shape=tuple(pltpu.HBM(a.shape, a.dtype) for a in srcs + lands),
        in_specs=[HBM] * (n + m) + [SEM, SEM, ANY], out_specs=tuple([HBM] * (n + m)),
        aliases={i: i for i in range(n + m)})(*srcs, *lands, send, recv, after)
    return list(res[:n]), list(res[n:])


def pair_plan(place, src, dst):
    x, y, c, _, _ = place
    return [(s.at[:, 1 - c], d, (x, y, 1 - c)) for s, d in zip(src, dst)]


def chip_plan(place, src, dst):
    x, y, c, _, others = place
    return [(s.at[2 * ox + oy], d.at[k], (ox, oy, c)) for s, d in zip(src, dst) for k, (ox, oy) in enumerate(others)]


def pair_exchange(grads, *, name):
    n = len(grads)

    def body(*refs):
        src, got = refs[:n], refs[n:2 * n]
        send, recv = refs[2 * n:]
        x, y, c, _, _ = _place()

        def swap(l):
            return pltpu.make_async_remote_copy(src[l].at[:, 1 - c], got[l], send.at[l], recv.at[l],
                                                device_id=(x, y, 1 - c), device_id_type=MESH)

        for l in range(n):
            swap(l).start()
        for l in range(n):
            swap(l).wait()

    res = _pcall(body, name=name, out_shape=[_sds((N_SHARDS,) + g.shape[2:], F32) for g in grads],
                 in_specs=[ANY] * n, out_specs=[ANY] * n,
                 scratch_shapes=[pltpu.SemaphoreType.DMA((n,)), pltpu.SemaphoreType.DMA((n,))],
                 side_effects=True)(*grads)
    return list(res)


def add_to_wire(mine, theirs, core, *, name, tm=512):
    s, _, r, c = mine.shape
    tm = min(tm, r)

    def body(core_ref, a_ref, b_ref, o_ref):
        o_ref[...] = (a_ref[...] + b_ref[...]).astype(BF16)

    spec = pl.BlockSpec((None, tm, c), lambda i, j, cr: (i, j, 0))
    return _pcall(body, name=name, out_shape=_sds((s, r, c), BF16), grid=(s, r // tm), num_prefetch=1,
                  in_specs=[pl.BlockSpec((None, None, tm, c), lambda i, j, cr: (i, cr[0], j, 0)), spec],
                  out_specs=spec, semantics=("parallel", "parallel"))(core, mine, theirs)


def sum_chips(wire, landed, place, dest, layer, n_layers, *, name, tm=512):
    _, r, c = wire.shape
    tm = min(tm, r)

    def body(place_ref, w_ref, l_ref, *rest):
        o_ref = rest[-1]
        o_ref[...] = ((w_ref[...].astype(F32) + l_ref[0].astype(F32)) + l_ref[1].astype(F32)) + l_ref[2].astype(F32)

    in_specs = [pl.BlockSpec((None, tm, c), lambda i, pr: (pr[0], i, 0)),
                pl.BlockSpec((3, tm, c), lambda i, pr: (0, i, 0))]
    args = [place, wire, landed]
    aliases = None
    if dest is not None:
        in_specs.append(ANY)
        args.append(dest)
        aliases = {3: 0}
    return _pcall(body, name=name, out_shape=_sds((n_layers, 2, r, c), F32), grid=(r // tm,), num_prefetch=1,
                  in_specs=in_specs,
                  out_specs=pl.BlockSpec((None, None, tm, c), lambda i, pr: (layer, pr[1], i, 0)),
                  aliases=aliases, semantics=("parallel",))(*args)


def pair_share(bufs, slots, *, name):
    n = len(bufs)

    def body(*refs):
        out = refs[n:2 * n]
        send, recv = refs[2 * n:]
        x, y, c, _, _ = _place()

        def share(i, half):
            o, l = slots[i]
            return pltpu.make_async_remote_copy(out[o].at[l, half], out[o].at[l, half], send.at[i], recv.at[i],
                                                device_id=(x, y, 1 - c), device_id_type=MESH)

        for i in range(len(slots)):
            share(i, c).start()
        for i in range(len(slots)):
            share(i, 1 - c).wait_recv()
            share(i, c).wait_send()

    res = _pcall(body, name=name, out_shape=[_sds(b.shape, F32) for b in bufs], in_specs=[ANY] * n,
                 out_specs=[ANY] * n,
                 scratch_shapes=[pltpu.SemaphoreType.DMA((len(slots),)), pltpu.SemaphoreType.DMA((len(slots),))],
                 aliases={o: o for o in range(n)}, side_effects=True)(*bufs)
    return list(res)


def all_reduce_small(packed, *, name):
    n_dev, r, c = packed.shape

    def body(in_ref, out_ref, land, send, recv):
        x, y, cc, _, _ = _place()
        me = 4 * x + 2 * y + cc
        peers = [(px, py, pc) for px in range(2) for py in range(2) for pc in range(2)]

        def scatter(d):
            return pltpu.make_async_remote_copy(in_ref.at[d], land.at[me], send.at[0, d], recv.at[0, me],
                                                device_id=peers[d], device_id_type=MESH)

        def gather(d):
            return pltpu.make_async_remote_copy(out_ref.at[me], out_ref.at[me], send.at[1, d], recv.at[1, me],
                                                device_id=peers[d], device_id_type=MESH)

        for d in range(n_dev):
            @pl.when(d != me)
            def _():
                scatter(d).start()
        land[me] = in_ref[me]
        for d in range(n_dev):
            @pl.when(d != me)
            def _():
                pltpu.make_async_remote_copy(in_ref.at[d], land.at[d], send.at[0, d], recv.at[0, d],
                                             device_id=peers[d], device_id_type=MESH).wait_recv()
        total = land[0]
        for d in range(1, n_dev):
            total = total + land[d]
        out_ref[me] = total
        for d in range(n_dev):
            @pl.when(d != me)
            def _():
                gather(d).start()
        for d in range(n_dev):
            @pl.when(d != me)
            def _():
                pltpu.make_async_remote_copy(out_ref.at[d], out_ref.at[d], send.at[1, d], recv.at[1, d],
                                             device_id=peers[d], device_id_type=MESH).wait_recv()
        for d in range(n_dev):
            @pl.when(d != me)
            def _():
                scatter(d).wait_send()
                gather(d).wait_send()

    vm = pl.BlockSpec(memory_space=pltpu.VMEM)
    return _pcall(body, name=name, out_shape=_sds(packed.shape, F32), in_specs=[vm], out_specs=vm,
                  scratch_shapes=[pltpu.VMEM(packed.shape, F32), pltpu.SemaphoreType.DMA((2, n_dev)),
                                  pltpu.SemaphoreType.DMA((2, n_dev))],
                  side_effects=True)(packed)


def adamw(w, g, m, v, *, name, part=None, dest=None, tm=512):
    shape = w.shape
    cols = shape[-1]
    rows = 1
    for s in shape[:-1]:
        rows *= s
    first, count = 0, rows
    if part is not None:
        count = rows // part[1]
        first = part[0] * count
    tm = min(tm, count)
    assert count % tm == 0
    two_d = lambda a: a.reshape(rows, cols)

    def body(w_ref, g_ref, m_ref, v_ref, *rest):
        d_ref, mo_ref, vo_ref = rest[-3:]
        gv = g_ref[...]
        m_new = ADAM_B1 * m_ref[...] + (1.0 - ADAM_B1) * gv
        v_new = ADAM_B2 * v_ref[...] + (1.0 - ADAM_B2) * (gv * gv)
        m_hat = m_new / (1.0 - ADAM_B1 ** ADAM_STEP)
        v_hat = v_new / (1.0 - ADAM_B2 ** ADAM_STEP)
        d_ref[...] = -ADAM_LR * (m_hat / (jnp.sqrt(v_hat) + ADAM_EPS) + ADAM_WD * w_ref[...])
        mo_ref[...] = m_new
        vo_ref[...] = v_new

    spec = pl.BlockSpec((tm, cols), lambda i: (first // tm + i, 0))
    args = [two_d(w), two_d(g), two_d(m), two_d(v)]
    in_specs = [spec] * 4
    aliases = None
    if dest is not None:
        args += [two_d(d) for d in dest]
        in_specs = in_specs + [ANY] * 3
        aliases = {4: 0, 5: 1, 6: 2}
    outs = _pcall(body, name=name, out_shape=[_sds((rows, cols), F32)] * 3, grid=(count // tm,), in_specs=in_specs,
                  out_specs=[spec] * 3, aliases=aliases, semantics=("parallel",))(*args)
    return [o.reshape(shape) for o in outs]


WEIGHTS = ("ln_mix_a", "w_in_a", "g_v_a", "w_spatial", "b_spatial", "w_out_a", "ln_kv", "w_kv", "g_k", "ln_mix_b",
           "w_q", "g_q", "w_out_b", "ln_mlp", "w_up", "w_down", "ln_ple", "w_ple_gate", "w_ple_proj")
MATRICES = (("w_in_a", 1, True), ("w_out_a", 1, False), ("w_kv", 0, True), ("w_q", 1, False), ("w_out_b", 1, False),
            ("w_up", 2, True), ("w_down", 2, False), ("w_ple_gate", 2, False), ("w_ple_proj", 2, True))
GATHER_STAGES = ((("w_in_a", 0),), (("w_out_a", 0),), (("w_up", 0),), (("w_down", 0),),
                 (("w_ple_gate", 0), ("w_ple_proj", 0), ("w_kv", 0)), (("w_q", 0),),
                 (("w_out_b", 0), ("w_up", 1), ("w_down", 1), ("w_ple_gate", 1), ("w_ple_proj", 1)))
REPLICATED = ("w_spatial", "b_spatial", "ln_kv", "g_k", "ln_mix_b", "g_q", "ln_mlp", "ln_ple")
SHARDED_VECTORS = ("ln_mix_a", "g_v_a")
SMALL_ROWS = 18


def kernel(x, p, ln_mix_a, w_in_a, g_v_a, w_spatial, b_spatial, w_out_a, ln_kv, w_kv, g_k, ln_mix_b, w_q, g_q, w_out_b, ln_mlp, w_up, w_down, ln_ple, w_ple_gate, w_ple_proj, loss_target, m_ln_mix_a, m_w_in_a, m_g_v_a, m_w_spatial, m_b_spatial, m_w_out_a, m_ln_kv, m_w_kv, m_g_k, m_ln_mix_b, m_w_q, m_g_q, m_w_out_b, m_ln_mlp, m_w_up, m_w_down, m_ln_ple, m_w_ple_gate, m_w_ple_proj, v_ln_mix_a, v_w_in_a, v_g_v_a, v_w_spatial, v_b_spatial, v_w_out_a, v_ln_kv, v_w_kv, v_g_k, v_ln_mix_b, v_w_q, v_g_q, v_w_out_b, v_ln_mlp, v_w_up, v_w_down, v_ln_ple, v_w_ple_gate, v_w_ple_proj):
    given = dict(locals())
    _PREVIOUS.clear()
    weights = {n: given[n] for n in WEIGHTS}
    shard = 2 * lax.axis_index("x") + lax.axis_index("y")
    core = lax.axis_index("c")
    shard_1 = shard.astype(jnp.int32).reshape(1)
    core_1 = core.astype(jnp.int32).reshape(1)
    place = jnp.stack([shard, core]).astype(jnp.int32)

    col_sharded = {name: cols for name, _, cols in MATRICES}
    layer_count = {name: max(layers, 1) for name, layers, _ in MATRICES}

    def cast(key, after):
        name, layer = key
        w3 = weights[name] if weights[name].ndim == 3 else weights[name][None]
        return (name, layer, col_sharded[name],
                cast_into_slot(w3, layer, shard_1, name=f"cast_{name}_{layer}", after=after))

    head = [cast(key, None) for key in GATHER_STAGES[0]]
    send_h, recv_h, flying_h, token_h = gather_start([lf[3] for lf in head], shard_1, name="gather_start_0")
    tail = [cast(key, token_h) for stage in GATHER_STAGES[1:] for key in stage]
    _, vec_a = gather_shards([], [ln_mix_a, g_v_a], name="gather_vectors")
    send_a, recv_a, flying, token = gather_start([lf[3] for lf in tail], vec_a[0], name="gather_start_1")

    w = {"ln_mix_a": vec_a[0].reshape(1, D_MODEL),
         "g_v_a": vec_a[1].reshape(1, D_MODEL)}
    for name in REPLICATED:
        w[name] = weights[name]

    class Late:
        def weights(self, name, layer, after):
            stage = [(name, layer) in s for s in GATHER_STAGES].index(True)
            if stage == 0:
                base, members, sems, fly = 0, head, (send_h, recv_h), flying_h
            else:
                base = sum(len(s) for s in GATHER_STAGES[1:stage])
                members, sems, fly = tail[base:base + len(GATHER_STAGES[stage])], (send_a, recv_a), flying
            bufs, send_b, recv_b, tok = gather_pass_on(fly[base:base + len(members)], sems[0], sems[1], after,
                                                       name=f"gather_pass_on_{stage}", base=base)
            got = gather_finish(bufs, send_b, recv_b, tok, [lf[3].shape for lf in members],
                                name=f"gather_finish_{stage}")
            out = {}
            for (leaf_name, leaf_layer, cols, _), arr in zip(members, got):
                out[(leaf_name, leaf_layer)] = arr if cols else arr.reshape(N_SHARDS * arr.shape[1], arr.shape[2])
            return out

        groups = []

        def pair_start(self, grads_done, after):
            self.keys = sorted(grads_done)
            views = [view(k, grads_done[k]) for k in self.keys]
            self.pair, token = exchange_start(views, [(N_SHARDS,) + v.shape[2:] for v in views], F32, pair_plan,
                                              len(views), after, name=f"grad_pair_start_{len(self.groups)}")
            return token

        def chip_start(self, after):
            tag = len(self.groups)
            mine, theirs = exchange_finish(self.pair, after, name=f"grad_pair_finish_{tag}")
            wire = [add_to_wire(a, b, core_1, name=f"grad_pair_sum_{tag}_{i}")
                    for i, (a, b) in enumerate(zip(mine, theirs))]
            chip, token = exchange_start(wire, [(3,) + v.shape[1:] for v in wire], BF16, chip_plan, 3 * len(wire),
                                         theirs[-1], name=f"grad_chip_start_{tag}")
            self.groups.append((self.keys, chip))
            return token

    def view(key, arr):
        rows = arr.shape[-2] if col_sharded[key[0]] else arr.shape[0] // N_SHARDS
        return arr.reshape(N_SHARDS, 2, rows // 2, arr.shape[-1])

    t = x.shape[1]
    late = Late()
    loss_blk, dx, g = local_step(x[0], p.reshape(2, t, PLE_DIM), loss_target[0], w, late)

    sent = {k for keys, _ in late.groups for k in keys}
    keys_last = [(name, layer) for name, layers, _ in MATRICES for layer in range(max(layers, 1))
                 if (name, layer) not in sent]
    views = [view(k, g[k[0]][k[1]] if layer_count[k[0]] == 2 else g[k[0]]) for k in keys_last]

    theirs = pair_exchange(views, name="grad_pair_exchange_last")
    wire_0 = [add_to_wire(a, b, core_1, name=f"grad_pair_sum_last_{i}") for i, (a, b) in enumerate(zip(views, theirs))]
    chip_0, token_0 = exchange_start(wire_0, [(3,) + v.shape[1:] for v in wire_0], BF16, chip_plan, 3 * len(wire_0),
                                     theirs[-1], name="grad_chip_start_last")

    grads, bufs = {}, {}

    def sum_and_share(keys, wire, landed, tag):
        for i, (key, wv, lv) in enumerate(zip(keys, wire, landed)):
            name, layer = key
            bufs[name] = sum_chips(wv, lv, place, bufs.get(name), layer, layer_count[name],
                                   name=f"grad_chip_sum_{tag}_{i}")
        names = sorted({k[0] for k in keys})
        shared = pair_share([bufs[n] for n in names], [(names.index(k[0]), k[1]) for k in keys],
                            name=f"grad_pair_share_{tag}")
        bufs.update(zip(names, shared))

    updates = {}

    def update(n, gn, part=None):
        wn, mn, vn = weights[n], given["m_" + n], given["v_" + n]
        if wn.ndim == 1:
            wn, gn, mn, vn = (a.reshape(1, -1) for a in (wn, gn, mn, vn))
        tag = "" if part is None else f"_{part[0]}"
        updates[n] = adamw(wn, gn.reshape(wn.shape), mn, vn, name=f"adamw_{n}{tag}", part=part, dest=updates.get(n))

    after = token_0
    for tag, (keys, chip) in enumerate(late.groups + [(keys_last, chip_0)]):
        wire, landed = exchange_finish(chip, after, name=f"grad_chip_finish_{tag}")
        sum_and_share(keys, wire, landed, tag)
        for name, layer in keys:
            update(name, bufs[name], (layer, layer_count[name]) if layer_count[name] == 2 else None)
        after = updates[keys[-1][0]][0]

    small = REPLICATED + SHARDED_VECTORS
    flat = jnp.concatenate([g[n].reshape(-1) for n in small] + [loss_blk[0, :1]])
    room = 8 * SMALL_ROWS * D_MODEL
    flat = jnp.concatenate([flat, jnp.zeros((room - flat.shape[0],), F32)])
    reduced = all_reduce_small(flat.reshape(8, SMALL_ROWS, D_MODEL), name="grad_small_all_reduce").reshape(-1)
    loss = reduced[sum(g[n].size for n in small)]
    at = 0
    for n in small:
        size = g[n].size
        piece = reduced[at:at + size]
        at += size
        if n in SHARDED_VECTORS:
            per = D_MODEL // N_SHARDS
            grads[n] = lax.dynamic_slice(piece, (shard * per,), (per,)).reshape(weights[n].shape)
        else:
            grads[n] = piece.reshape(weights[n].shape)
        update(n, grads[n])
    for name, _, _ in MATRICES:
        grads[name] = bufs[name].reshape(weights[name].shape)
    delta = {n: updates[n][0].reshape(weights[n].shape) for n in WEIGHTS}
    new_m = {n: updates[n][1].reshape(weights[n].shape) for n in WEIGHTS}
    new_v = {n: updates[n][2].reshape(weights[n].shape) for n in WEIGHTS}
    return (loss, dx.reshape(x.shape), *[grads[n] for n in WEIGHTS], *[delta[n] for n in WEIGHTS],
            *[new_m[n] for n in WEIGHTS], *[new_v[n] for n in WEIGHTS])
```

```python
import jax
import jax.numpy as jnp
from jax import lax
from jax.experimental import pallas as pl
from jax.experimental.pallas import tpu as pltpu

F32 = jnp.float32
BF16 = jnp.bfloat16

D_MODEL = 1024
D_FF = 4096
PLE_DIM = 256
N_GROUPS = 8
CHUNK = 128
HEAD_DIM = 64
LANES = 128
ATT_K_BLOCK = 256
ATT_Q_BLOCK = 512
EPS = 1e-6
N_SHARDS = 4
VMEM_LIMIT = 56 * 1024 * 1024

ADAM_LR = 0.001
ADAM_B1 = 0.9
ADAM_B2 = 0.999
ADAM_EPS = 1e-08
ADAM_WD = 0.01
ADAM_STEP = 10

MESH = pl.DeviceIdType.MESH


_PREVIOUS = []


def _in_order(make, in_specs, args, views_of=()):
    previous = _PREVIOUS[-1] if _PREVIOUS else None
    if previous is not None and any(a is previous for a in (*args, *views_of)):
        previous = None
    if previous is None:
        result = make(lambda body: body, list(in_specs))(*args)
    else:
        count = len(args)

        def skip(body):
            return lambda *refs: body(*refs[:count], *refs[count + 1:])

        result = make(skip, list(in_specs) + [pl.BlockSpec(memory_space=pl.ANY)])(*args, previous)
    _PREVIOUS[:] = [jax.tree_util.tree_leaves(result)[-1]]
    return result


def _pcall(body, *, name, out_shape, grid=None, in_specs=None, out_specs=None, scratch_shapes=(),
           semantics=None, aliases=None, side_effects=False, num_prefetch=0):
    params = dict(vmem_limit_bytes=VMEM_LIMIT)
    if semantics is not None:
        params["dimension_semantics"] = semantics
    if side_effects:
        params["has_side_effects"] = True
    kwargs = {}
    if aliases:
        kwargs["input_output_aliases"] = aliases

    def make(wrap, specs):
        body_ = wrap(body)
        if num_prefetch:
            spec = pltpu.PrefetchScalarGridSpec(num_scalar_prefetch=num_prefetch, grid=grid, in_specs=specs,
                                                out_specs=out_specs, scratch_shapes=list(scratch_shapes))
            return pl.pallas_call(body_, name=name, out_shape=out_shape, grid_spec=spec,
                                  compiler_params=pltpu.CompilerParams(**params), **kwargs)
        more = dict(kwargs, in_specs=specs)
        if grid is not None:
            more["grid"] = grid
        if out_specs is not None:
            more["out_specs"] = out_specs
        return pl.pallas_call(body_, name=name, out_shape=out_shape, scratch_shapes=list(scratch_shapes),
                              compiler_params=pltpu.CompilerParams(**params), **more)

    return lambda *args: _in_order(make, in_specs, args)


def _sds(shape, dtype):
    return jax.ShapeDtypeStruct(shape, dtype)


_GELU_C = 0.7978845608028654
_GELU_A = 0.044715


def _gelu(x):
    inner = _GELU_C * (x + _GELU_A * (x * x * x))
    return 0.5 * x * (1.0 + jnp.tanh(inner))


def _gelu_grad(x):
    x2 = x * x
    t = jnp.tanh(_GELU_C * (x + _GELU_A * (x2 * x)))
    return 0.5 * (1.0 + t) + 0.5 * x * (1.0 - t * t) * (_GELU_C * (1.0 + 3.0 * _GELU_A * x2))


def _sigmoid(x):
    return 1.0 / (1.0 + jnp.exp(-x))


def _log_sigmoid(z):
    return jnp.minimum(z, 0.0) - jnp.log(1.0 + jnp.exp(-jnp.abs(z)))


def _dot(a, b):
    return jnp.dot(a, b, preferred_element_type=F32)


def _dot_nt(a, b):
    return lax.dot_general(a, b, (((1,), (1,)), ((), ())), preferred_element_type=F32)


def _dot_tn(a, b):
    return lax.dot_general(a, b, (((0,), (0,)), ((), ())), preferred_element_type=F32)


def _head_rstd(x):
    lane = lax.broadcasted_iota(jnp.int32, x.shape, 1)
    low = lane < HEAD_DIM
    sq = x * x
    s_lo = jnp.sum(jnp.where(low, sq, 0.0), axis=-1, keepdims=True)
    s_hi = jnp.sum(jnp.where(low, 0.0, sq), axis=-1, keepdims=True)
    ms = jnp.where(low, s_lo, s_hi) * (1.0 / HEAD_DIM)
    return lax.rsqrt(ms + EPS)


def _head_mean(x):
    lane = lax.broadcasted_iota(jnp.int32, x.shape, 1)
    low = lane < HEAD_DIM
    s_lo = jnp.sum(jnp.where(low, x, 0.0), axis=-1, keepdims=True)
    s_hi = jnp.sum(jnp.where(low, 0.0, x), axis=-1, keepdims=True)
    return jnp.where(low, s_lo, s_hi) * (1.0 / HEAD_DIM)


def _full(shape):
    zeros = (0,) * len(shape)
    return pl.BlockSpec(shape, lambda i: zeros)


def norm_matmul(x, g, w, *, name, epilogue="none", tm=512):
    t, d = x.shape
    sharded = w.ndim == 3
    per = w.shape[2] if sharded else w.shape[1]
    n = N_SHARDS * per if sharded else per
    tm = min(tm, t)

    def body(x_ref, g_ref, w_ref, h_ref, r_ref, *outs):
        xv = x_ref[...]
        r = lax.rsqrt(jnp.mean(xv * xv, axis=-1, keepdims=True) + EPS)
        h = ((xv * r) * g_ref[...]).astype(BF16)
        h_ref[...] = h
        r_ref[...] = r
        for s in range(N_SHARDS if sharded else 1):
            cols = slice(s * per, (s + 1) * per)
            y = _dot(h, w_ref[s] if sharded else w_ref[...])
            if epilogue == "none":
                outs[0][:, cols] = y
            else:
                a = jnp.maximum(y, 0.0)
                outs[0][:, cols] = a.astype(BF16)
                outs[1][:, cols] = (a * a).astype(BF16)

    row = lambda i: (i, 0)
    out_shape = [_sds((t, d), BF16), _sds((t, 1), F32)]
    out_specs = [pl.BlockSpec((tm, d), row), pl.BlockSpec((tm, 1), row)]
    if epilogue == "none":
        out_shape.append(_sds((t, n), F32))
        out_specs.append(pl.BlockSpec((tm, n), row))
    else:
        out_shape += [_sds((t, n), BF16), _sds((t, n), BF16)]
        out_specs += [pl.BlockSpec((tm, n), row)] * 2
    return _pcall(
        body, name=name, out_shape=out_shape, grid=(t // tm,),
        in_specs=[pl.BlockSpec((tm, d), row), _full((1, d)), _full(w.shape)],
        out_specs=out_specs, semantics=("parallel",))(x, g, w)


def matmul_residual(a, w, res, *, name, tm=512):
    t, k = a.shape
    n = w.shape[1]
    tm = min(tm, t)

    def body(a_ref, w_ref, res_ref, o_ref):
        o_ref[...] = res_ref[...] + _dot(a_ref[...], w_ref[...])

    row = lambda i: (i, 0)
    return _pcall(
        body, name=name, out_shape=_sds((t, n), F32), grid=(t // tm,),
        in_specs=[pl.BlockSpec((tm, k), row), _full(w.shape), pl.BlockSpec((tm, n), row)],
        out_specs=pl.BlockSpec((tm, n), row), semantics=("parallel",))(a, w, res)


def ple_forward(x, g, w_gate, p, w_proj, *, name, tm=256):
    t, d = x.shape
    tm = min(tm, t)

    def body(x_ref, g_ref, wg_ref, p_ref, wp_ref, h_ref, r_ref, gate_ref, pp_ref, o_ref):
        xv = x_ref[...]
        r = lax.rsqrt(jnp.mean(xv * xv, axis=-1, keepdims=True) + EPS)
        h = ((xv * r) * g_ref[...]).astype(BF16)
        h_ref[...] = h
        r_ref[...] = r
        gate = _sigmoid(_dot(h, wg_ref[...]))
        gate_ref[...] = gate
        pb = p_ref[...].astype(BF16)
        per = d // N_SHARDS
        for s in range(N_SHARDS):
            cols = slice(s * per, (s + 1) * per)
            pp = _dot(pb, wp_ref[s])
            pp_ref[:, cols] = pp.astype(BF16)
            o_ref[:, cols] = xv[:, cols] + pp * gate[:, cols]

    row = lambda i: (i, 0)
    fixed = lambda i: (0, 0)
    return _pcall(
        body, name=name,
        out_shape=[_sds((t, d), BF16), _sds((t, 1), F32), _sds((t, d), F32), _sds((t, d), BF16), _sds((t, d), F32)],
        grid=(t // tm,),
        in_specs=[pl.BlockSpec((tm, d), row), pl.BlockSpec((1, d), fixed), pl.BlockSpec((d, d), fixed),
                  pl.BlockSpec((tm, PLE_DIM), row),
                  pl.BlockSpec((N_SHARDS, PLE_DIM, d // N_SHARDS), lambda i: (0, 0, 0))],
        out_specs=[pl.BlockSpec((tm, d), row), pl.BlockSpec((tm, 1), row), pl.BlockSpec((tm, d), row),
                   pl.BlockSpec((tm, d), row), pl.BlockSpec((tm, d), row)],
        semantics=("parallel",))(x, g, w_gate, p, w_proj)


def _tril_mask():
    r = lax.broadcasted_iota(jnp.int32, (CHUNK, CHUNK), 0)
    c = lax.broadcasted_iota(jnp.int32, (CHUNK, CHUNK), 1)
    return c <= r


def _sgu_common(pre_ref, gv_ref, ws_ref):
    pre = pre_ref[...]
    pre_u, pre_v = pre[:, :D_MODEL], pre[:, D_MODEL:]
    u = _gelu(pre_u)
    v = _gelu(pre_v)
    r = lax.rsqrt(jnp.mean(v * v, axis=-1, keepdims=True) + EPS)
    vhat = v * r
    vn = (vhat * gv_ref[...]).astype(BF16)
    tril = _tril_mask()
    wm = [jnp.where(tril, ws_ref[g], 0.0).astype(BF16) for g in range(N_GROUPS)]
    return pre_u, pre_v, u, r, vhat, vn, wm, tril


def sgu_forward(pre, g_v, w_s, b_full, *, name):
    t = pre.shape[0]

    def body(pre_ref, gv_ref, ws_ref, b_ref, y_ref):
        _, _, u, _, _, vn, wm, _ = _sgu_common(pre_ref, gv_ref, ws_ref)
        for g in range(N_GROUPS):
            cols = slice(g * LANES, (g + 1) * LANES)
            mix = _dot(wm[g], vn[:, cols]) + b_ref[:, cols]
            y_ref[:, cols] = (u[:, cols] * mix).astype(BF16)

    return _pcall(
        body, name=name, out_shape=_sds((t, D_MODEL), BF16), grid=(t // CHUNK,),
        in_specs=[pl.BlockSpec((CHUNK, 2 * D_MODEL), lambda i: (i, 0)), pl.BlockSpec((1, D_MODEL), lambda i: (0, 0)),
                  pl.BlockSpec((N_GROUPS, CHUNK, CHUNK), lambda i: (0, 0, 0)),
                  pl.BlockSpec((CHUNK, D_MODEL), lambda i: (0, 0))],
        out_specs=pl.BlockSpec((CHUNK, D_MODEL), lambda i: (i, 0)),
        semantics=("parallel",))(pre, g_v, w_s, b_full)


def head_norm(pre, g128, *, name, col_block=0, scale=1.0, passthrough=False, tm=512):
    t = pre.shape[0]
    tm = min(tm, t)

    def body(*refs):
        if passthrough:
            x_ref, v_ref, g_ref, o_ref, vo_ref = refs
            vo_ref[...] = v_ref[...].astype(BF16)
        else:
            x_ref, g_ref, o_ref = refs
        g = g_ref[...] * scale
        for b in range(D_MODEL // LANES):
            cols = slice(b * LANES, (b + 1) * LANES)
            xv = x_ref[:, cols]
            o_ref[:, cols] = ((xv * _head_rstd(xv)) * g).astype(BF16)

    x_spec = pl.BlockSpec((tm, D_MODEL), lambda i: (i, col_block))
    g_spec = pl.BlockSpec((1, LANES), lambda i: (0, 0))
    o_spec = pl.BlockSpec((tm, D_MODEL), lambda i: (i, 0))
    if passthrough:
        return _pcall(body, name=name, out_shape=[_sds((t, D_MODEL), BF16)] * 2, grid=(t // tm,),
                      in_specs=[x_spec, pl.BlockSpec((tm, D_MODEL), lambda i: (i, 1)), g_spec],
                      out_specs=[o_spec, o_spec], semantics=("parallel",))(pre, pre, g128)
    return _pcall(body, name=name, out_shape=_sds((t, D_MODEL), BF16), grid=(t // tm,),
                  in_specs=[x_spec, g_spec], out_specs=o_spec, semantics=("parallel",))(pre, g128)


def _suffix_matrix(n):
    r = lax.broadcasted_iota(jnp.int32, (n, n), 0)
    c = lax.broadcasted_iota(jnp.int32, (n, n), 1)
    return jnp.where(r > c, 1.0, 0.0).astype(BF16)


def _prefix_matrix(n):
    r = lax.broadcasted_iota(jnp.int32, (n, n), 0)
    c = lax.broadcasted_iota(jnp.int32, (n, n), 1)
    return jnp.where(r < c, 1.0, 0.0).astype(BF16)


def _block_cumsum(a, tri):
    return _dot(a.astype(BF16), tri)


def _stacked_causal(nq, nk, shift):
    r = lax.broadcasted_iota(jnp.int32, (2 * nq, nk), 0)
    c = lax.broadcasted_iota(jnp.int32, (2 * nq, nk), 1)
    return c + shift < jnp.where(r >= nq, r - nq, r)


def _att_blocks(t):
    bq, bk = min(ATT_Q_BLOCK, t), min(ATT_K_BLOCK, t)
    return bq, bk, bq // bk


def _stack_heads(a, low):
    zero = jnp.zeros_like(a)
    return jnp.concatenate([jnp.where(low, a, zero), jnp.where(low, zero, a)], axis=0)


def stick_breaking_forward(q, k, v, *, name):
    t = q.shape[0]
    bq, bk, ratio = _att_blocks(t)

    def body(q_ref, k_ref, v_ref, o_ref):
        i = pl.program_id(1)
        low = lax.broadcasted_iota(jnp.int32, (bq, LANES), 1) < HEAD_DIM
        tri = _suffix_matrix(bk)
        qs = _stack_heads(q_ref[...], low)

        def block(j, carry, acc, causal=None):
            rows = pl.ds(pl.multiple_of(j * bk, bk), bk)
            z = _dot_nt(qs, k_ref[rows, :])
            ls = _log_sigmoid(z)
            lg = ls - z
            if causal is not None:
                lg = jnp.where(causal, lg, 0.0)
            s = ls + _block_cumsum(lg, tri) + carry
            a = jnp.exp(s)
            if causal is not None:
                a = jnp.where(causal, a, 0.0)
            acc = acc + _dot(a.astype(BF16), v_ref[rows, :])
            return carry + jnp.sum(lg, axis=-1, keepdims=True), acc

        state = (jnp.zeros((2 * bq, 1), F32), jnp.zeros((2 * bq, LANES), F32))
        for m in reversed(range(ratio)):
            state = block(ratio * i + m, state[0], state[1], _stacked_causal(bq, bk, m * bk))
        first = ratio * i

        def two_blocks(n, st):
            st = block(first - 1 - 2 * n, st[0], st[1])
            return block(first - 2 - 2 * n, st[0], st[1])

        state = lax.fori_loop(0, first // 2, two_blocks, state)
        _, acc = lax.fori_loop(0, first % 2, lambda n, st: block(0, st[0], st[1]), state)
        o_ref[...] = jnp.where(low, acc[:bq], acc[bq:]).astype(BF16)

    return _pcall(
        body, name=name, out_shape=_sds((t, D_MODEL), BF16), grid=(D_MODEL // LANES, t // bq),
        in_specs=[pl.BlockSpec((bq, LANES), lambda p, i: (i, p)), pl.BlockSpec((t, LANES), lambda p, i: (0, p)),
                  pl.BlockSpec((t, LANES), lambda p, i: (0, p))],
        out_specs=pl.BlockSpec((bq, LANES), lambda p, i: (i, p)),
        semantics=("parallel", "arbitrary"))(q, k, v)


def loss_forward(x, target, *, name, tm=512):
    t, d = x.shape
    tm = min(tm, t)

    def body(x_ref, t_ref, l_ref, dx_ref):
        @pl.when(pl.program_id(0) == 0)
        def _():
            l_ref[...] = jnp.zeros_like(l_ref)

        diff = x_ref[...] - t_ref[...]
        dx_ref[...] = diff * (1.0 / d)
        l_ref[...] += 0.5 * jnp.sum(jnp.mean(diff * diff, axis=-1, keepdims=True))

    return _pcall(
        body, name=name, out_shape=[_sds((8, LANES), F32), _sds((t, d), F32)], grid=(t // tm,),
        in_specs=[pl.BlockSpec((tm, d), lambda i: (i, 0))] * 2,
        out_specs=[pl.BlockSpec((8, LANES), lambda i: (0, 0)), pl.BlockSpec((tm, d), lambda i: (i, 0))],
        semantics=("arbitrary",))(x, target)


def matmul_nt(dy, w, *, name, mul=None, out_dtype=F32, tm=512):
    t, n = dy.shape
    k = w.shape[0]
    tm = min(tm, t)

    def body(*refs):
        if mul is None:
            dy_ref, w_ref, o_ref = refs
        else:
            dy_ref, w_ref, m_ref, o_ref = refs
        y = _dot_nt(dy_ref[...].astype(BF16), w_ref[...])
        if mul is not None:
            y = y * (2.0 * m_ref[...].astype(F32))
        o_ref[...] = y.astype(out_dtype)

    row = lambda i: (i, 0)
    in_specs = [pl.BlockSpec((tm, n), row), _full(w.shape)]
    args = [dy, w]
    if mul is not None:
        in_specs.append(pl.BlockSpec((tm, k), row))
        args.append(mul)
    return _pcall(body, name=name, out_shape=_sds((t, k), out_dtype), grid=(t // tm,), in_specs=in_specs,
                  out_specs=pl.BlockSpec((tm, k), row), semantics=("parallel",))(*args)


def matmul_tn(a, dy, *, name, col_shards, tk=512):
    t, k = a.shape
    n = dy.shape[1]
    if col_shards:
        tn = n // N_SHARDS

        def body(a_ref, dy_ref, o_ref):
            o_ref[...] = _dot_tn(a_ref[...].astype(BF16), dy_ref[...].astype(BF16))

        return _pcall(body, name=name, out_shape=_sds((N_SHARDS, k, tn), F32), grid=(N_SHARDS,),
                      in_specs=[_full((t, k)), pl.BlockSpec((t, tn), lambda j: (0, j))],
                      out_specs=pl.BlockSpec((None, k, tn), lambda j: (j, 0, 0)), semantics=("parallel",))(a, dy)

    tk = min(tk, k)

    def body(a_ref, dy_ref, o_ref, dy_bf):
        @pl.when(pl.program_id(0) == 0)
        def _():
            dy_bf[...] = dy_ref[...].astype(BF16)

        o_ref[...] = _dot_tn(a_ref[...].astype(BF16), dy_bf[...])

    return _pcall(body, name=name, out_shape=_sds((k, n), F32), grid=(k // tk,),
                  in_specs=[pl.BlockSpec((t, tk), lambda i: (0, i)), _full((t, n))],
                  out_specs=pl.BlockSpec((tk, n), lambda i: (i, 0)),
                  scratch_shapes=[pltpu.VMEM((t, n), BF16)], semantics=("arbitrary",))(a, dy)


def norm_backward(dpre, w, x, g, rstd, dx_out, *, name, tm=512):
    t, d = x.shape
    n = dpre.shape[1]
    tm = min(tm, t)
    if w.ndim == 3:
        w_spec = pl.BlockSpec(w.shape, lambda i: (0, 0, 0))
    else:
        w_spec = pl.BlockSpec(w.shape, lambda i: (0, 0))

    def body(dp_ref, w_ref, x_ref, g_ref, r_ref, dxo_ref, dx_ref, dg_ref):
        @pl.when(pl.program_id(0) == 0)
        def _():
            dg_ref[...] = jnp.zeros_like(dg_ref)

        if w.ndim == 3:
            per = n // N_SHARDS
            dh = _dot_nt(dp_ref[:, 0:per], w_ref[0])
            for s in range(1, N_SHARDS):
                dh = dh + _dot_nt(dp_ref[:, s * per:(s + 1) * per], w_ref[s])
        else:
            dh = _dot_nt(dp_ref[...], w_ref[...])
        r = r_ref[...]
        xn = x_ref[...] * r
        dg_ref[...] += jnp.sum(dh * xn, axis=0, keepdims=True)
        dxn = dh * g_ref[...]
        dx = r * (dxn - xn * jnp.mean(dxn * xn, axis=-1, keepdims=True))
        dx_ref[...] = dxo_ref[...] + dx

    row = lambda i: (i, 0)
    fixed = lambda i: (0, 0)
    return _pcall(
        body, name=name, out_shape=[_sds((t, d), F32), _sds((1, d), F32)], grid=(t // tm,),
        in_specs=[pl.BlockSpec((tm, n), row), w_spec, pl.BlockSpec((tm, d), row),
                  pl.BlockSpec((1, d), fixed), pl.BlockSpec((tm, 1), row), pl.BlockSpec((tm, d), row)],
        out_specs=[pl.BlockSpec((tm, d), row), pl.BlockSpec((1, d), fixed)],
        semantics=("arbitrary",))(dpre, w, x, g, rstd, dx_out)


def ple_backward(dx, gate, pp, *, name, tm=512):
    t, d = dx.shape
    tm = min(tm, t)

    def body(dx_ref, gate_ref, pp_ref, dg_ref, dp_ref):
        dxv = dx_ref[...]
        gate = gate_ref[...]
        dg_ref[...] = (dxv * pp_ref[...].astype(F32) * (gate * (1.0 - gate))).astype(BF16)
        dp_ref[...] = (dxv * gate).astype(BF16)

    spec = pl.BlockSpec((tm, d), lambda i: (i, 0))
    return _pcall(body, name=name, out_shape=[_sds((t, d), BF16)] * 2, grid=(t // tm,), in_specs=[spec] * 3,
                  out_specs=[spec] * 2, semantics=("parallel",))(dx, gate, pp)


def sgu_backward(dy, pre, g_v, w_s, b_full, *, name):
    t = pre.shape[0]
    n_chunks = t // CHUNK

    def body(dy_ref, pre_ref, gv_ref, ws_ref, b_ref, dpre_ref, dws_ref, db_ref, dgv_ref, dvn_s, dbf_s):
        step = pl.program_id(0)

        @pl.when(step == 0)
        def _():
            dws_ref[...] = jnp.zeros_like(dws_ref)
            dgv_ref[...] = jnp.zeros_like(dgv_ref)
            dbf_s[...] = jnp.zeros_like(dbf_s)

        pre_u, pre_v, u, r, vhat, vn, wm, tril = _sgu_common(pre_ref, gv_ref, ws_ref)
        dyv = dy_ref[...]
        for g in range(N_GROUPS):
            cols = slice(g * LANES, (g + 1) * LANES)
            mix = _dot(wm[g], vn[:, cols]) + b_ref[:, cols]
            dmix = dyv[:, cols] * u[:, cols]
            dmix_b = dmix.astype(BF16)
            du = dyv[:, cols] * mix
            dpre_ref[:, cols] = (du * _gelu_grad(pre_u[:, cols])).astype(BF16)
            dws_ref[g] += jnp.where(tril, _dot_nt(dmix_b, vn[:, cols]), 0.0)
            dbf_s[:, cols] += dmix
            dvn_s[:, cols] = _dot_tn(wm[g], dmix_b)
        dvn = dvn_s[...]
        dgv_ref[...] += jnp.sum(dvn * vhat, axis=0, keepdims=True)
        dxn = dvn * gv_ref[...]
        dv = r * (dxn - vhat * jnp.mean(dxn * vhat, axis=-1, keepdims=True))
        dpre_ref[:, D_MODEL:] = (dv * _gelu_grad(pre_v)).astype(BF16)

        @pl.when(step == n_chunks - 1)
        def _():
            lane = lax.broadcasted_iota(jnp.int32, (CHUNK, LANES), 1)
            acc = jnp.zeros((CHUNK, LANES), F32)
            for g in range(N_GROUPS):
                s = jnp.sum(dbf_s[:, g * LANES:(g + 1) * LANES], axis=-1, keepdims=True)
                acc = jnp.where(lane == g, s, acc)
            db_ref[...] = acc

    fixed2 = lambda i: (0, 0)
    return _pcall(
        body, name=name,
        out_shape=[_sds((t, 2 * D_MODEL), BF16), _sds((N_GROUPS, CHUNK, CHUNK), F32), _sds((CHUNK, LANES), F32),
                   _sds((1, D_MODEL), F32)],
        grid=(n_chunks,),
        in_specs=[pl.BlockSpec((CHUNK, D_MODEL), lambda i: (i, 0)), pl.BlockSpec((CHUNK, 2 * D_MODEL), lambda i: (i, 0)),
                  pl.BlockSpec((1, D_MODEL), fixed2), pl.BlockSpec((N_GROUPS, CHUNK, CHUNK), lambda i: (0, 0, 0)),
                  pl.BlockSpec((CHUNK, D_MODEL), fixed2)],
        out_specs=[pl.BlockSpec((CHUNK, 2 * D_MODEL), lambda i: (i, 0)),
                   pl.BlockSpec((N_GROUPS, CHUNK, CHUNK), lambda i: (0, 0, 0)), pl.BlockSpec((CHUNK, LANES), fixed2),
                   pl.BlockSpec((1, D_MODEL), fixed2)],
        scratch_shapes=[pltpu.VMEM((CHUNK, D_MODEL), F32), pltpu.VMEM((CHUNK, D_MODEL), F32)],
        semantics=("arbitrary",))(dy, pre, g_v, w_s, b_full)


def head_norm_backward(dy, pre, g128, *, name, col_block=0, scale=1.0, passthrough=None, tm=512):
    t = dy.shape[0]
    tm = min(tm, t)
    width = 2 * D_MODEL if passthrough is not None else D_MODEL

    def body(*refs):
        if passthrough is not None:
            dy_ref, x_ref, g_ref, dv_ref, o_ref, dg_ref = refs
            o_ref[:, D_MODEL:] = dv_ref[...].astype(BF16)
        else:
            dy_ref, x_ref, g_ref, o_ref, dg_ref = refs

        @pl.when(pl.program_id(0) == 0)
        def _():
            dg_ref[...] = jnp.zeros_like(dg_ref)

        g = g_ref[...]
        dg = jnp.zeros((1, LANES), F32)
        for b in range(D_MODEL // LANES):
            cols = slice(b * LANES, (b + 1) * LANES)
            xv = x_ref[:, cols]
            r = _head_rstd(xv)
            xn = xv * r
            dyv = dy_ref[:, cols] * scale
            dg = dg + jnp.sum(dyv * xn, axis=0, keepdims=True)
            dxn = dyv * g
            o_ref[:, cols] = (r * (dxn - xn * _head_mean(dxn * xn))).astype(BF16)
        dg_ref[...] += dg

    row = lambda i: (i, 0)
    in_specs = [pl.BlockSpec((tm, D_MODEL), row), pl.BlockSpec((tm, D_MODEL), lambda i: (i, col_block)),
                pl.BlockSpec((1, LANES), lambda i: (0, 0))]
    args = [dy, pre, g128]
    if passthrough is not None:
        in_specs.append(pl.BlockSpec((tm, D_MODEL), row))
        args.append(passthrough)
    return _pcall(body, name=name, out_shape=[_sds((t, width), BF16), _sds((1, LANES), F32)], grid=(t // tm,),
                  in_specs=in_specs,
                  out_specs=[pl.BlockSpec((tm, width), row), pl.BlockSpec((1, LANES), lambda i: (0, 0))],
                  semantics=("arbitrary",))(*args)


def stick_breaking_backward(q, k, v, do, *, name):
    t = q.shape[0]
    bq, bk, ratio = _att_blocks(t)

    def body(q_ref, k_ref, v_ref, do_ref, dq_ref, dk_ref, dv_ref, s_buf, sg_buf):
        i = pl.program_id(1)

        @pl.when(i == 0)
        def _():
            dk_ref[...] = jnp.zeros_like(dk_ref)
            dv_ref[...] = jnp.zeros_like(dv_ref)

        low = lax.broadcasted_iota(jnp.int32, (bq, LANES), 1) < HEAD_DIM
        suffix = _suffix_matrix(bk)
        prefix = _prefix_matrix(bk)
        qs = _stack_heads(q_ref[...], low)
        dos = _stack_heads(do_ref[...], low)
        first = ratio * i

        def log_weights(j, carry, causal=None):
            rows = pl.ds(pl.multiple_of(j * bk, bk), bk)
            z = _dot_nt(qs, k_ref[rows, :])
            ls = _log_sigmoid(z)
            lg = ls - z
            if causal is not None:
                lg = jnp.where(causal, lg, 0.0)
            s_buf[j] = ls + _block_cumsum(lg, suffix) + carry
            sg_buf[j] = jnp.exp(ls)
            return carry + jnp.sum(lg, axis=-1, keepdims=True)

        carry = jnp.zeros((2 * bq, 1), F32)
        for m in reversed(range(ratio)):
            carry = log_weights(first + m, carry, _stacked_causal(bq, bk, m * bk))
        carry = lax.fori_loop(0, first // 2,
                              lambda n, c: log_weights(first - 2 - 2 * n, log_weights(first - 1 - 2 * n, c)), carry)
        lax.fori_loop(0, first % 2, lambda n, c: log_weights(0, c), carry)

        def grads(j, pcarry, dq_acc, causal=None):
            rows = pl.ds(pl.multiple_of(j * bk, bk), bk)
            a = jnp.exp(s_buf[j])
            if causal is not None:
                a = jnp.where(causal, a, 0.0)
            sg = sg_buf[j]
            ds = _dot_nt(dos, v_ref[rows, :]) * a
            before = _block_cumsum(ds, prefix) + pcarry
            if causal is not None:
                before = jnp.where(causal, before, 0.0)
            dz = (ds - sg * (ds + before)).astype(BF16)
            dq_acc = dq_acc + _dot(dz, k_ref[rows, :])
            dk_ref[rows, :] += _dot_tn(dz, qs)
            dv_ref[rows, :] += _dot_tn(a.astype(BF16), dos)
            return pcarry + jnp.sum(ds, axis=-1, keepdims=True), dq_acc

        def two_blocks(n, st):
            st = grads(2 * n, st[0], st[1])
            return grads(2 * n + 1, st[0], st[1])

        state = lax.fori_loop(0, first // 2, two_blocks,
                              (jnp.zeros((2 * bq, 1), F32), jnp.zeros((2 * bq, LANES), F32)))
        state = lax.fori_loop(0, first % 2, lambda n, st: grads(first - 1, st[0], st[1]), state)
        for m in range(ratio):
            state = grads(first + m, state[0], state[1], _stacked_causal(bq, bk, m * bk))
        dq_ref[...] = jnp.where(low, state[1][:bq], state[1][bq:])

    full = pl.BlockSpec((t, LANES), lambda p, i: (0, p))
    qblk = pl.BlockSpec((bq, LANES), lambda p, i: (i, p))
    return _pcall(
        body, name=name, out_shape=[_sds((t, D_MODEL), F32)] * 3, grid=(D_MODEL // LANES, t // bq),
        in_specs=[qblk, full, full, qblk], out_specs=[qblk, full, full],
        scratch_shapes=[pltpu.VMEM((t // bk, 2 * bq, bk), F32), pltpu.VMEM((t // bk, 2 * bq, bk), F32)],
        semantics=("parallel", "arbitrary"))(q, k, v, do)


def _mlp_backward(dx, saved, g, w_up, w_down, tag):
    x, h, r, a, a2 = saved
    d_w_down = matmul_tn(a2, dx, name=f"d_w_down_{tag}", col_shards=False)
    dpre = matmul_nt(dx, w_down, name=f"d_mlp_act_{tag}", mul=a, out_dtype=BF16)
    d_w_up = matmul_tn(h, dpre, name=f"d_w_up_{tag}", col_shards=True)
    dx, d_g = norm_backward(dpre, w_up, x, g, r, dx, name=f"d_mlp_norm_{tag}")
    return dx, d_w_up, d_w_down, d_g


def _ple_backward(dx, saved, p, g, w_gate, tag):
    x, h, r, gate, pp = saved
    dgate, dproj = ple_backward(dx, gate, pp, name=f"d_ple_{tag}")
    d_w_proj = matmul_tn(p, dproj, name=f"d_w_ple_proj_{tag}", col_shards=True)
    d_w_gate = matmul_tn(h, dgate, name=f"d_w_ple_gate_{tag}", col_shards=False)
    dx, d_g = norm_backward(dgate, w_gate, x, g, r, dx, name=f"d_ple_norm_{tag}")
    return dx, d_w_gate, d_w_proj, d_g


def local_step(x, p, target, w, late=None):
    row = lambda v: v.reshape(1, -1)
    g128 = lambda v: jnp.tile(v.reshape(1, HEAD_DIM), (1, 2))
    scale = HEAD_DIM ** -0.5
    b_full = jnp.repeat(jnp.transpose(w["b_spatial"][0]), LANES, axis=1)
    w_s = w["w_spatial"][0]

    mats = {}
    for name, value in w.items():
        if isinstance(value, tuple):
            mats.update({(name, layer): v for layer, v in enumerate(value)})
    if "w_kv" in w:
        mats[("w_kv", 0)] = w["w_kv"]

    def fetch(name, layer, after):
        if (name, layer) not in mats:
            mats.update(late.weights(name, layer, after))
        return mats[(name, layer)]

    def mlp_forward(x_in, layer):
        h, r, a, a2 = norm_matmul(x_in, row(w["ln_mlp"][layer]), fetch("w_up", layer, x_in), name=f"mlp_up_{layer}",
                                  epilogue="relu2")
        return matmul_residual(a2, fetch("w_down", layer, a2), x_in, name=f"mlp_down_{layer}"), (x_in, h, r, a, a2)

    def ple(x_in, layer):
        return ple_forward(x_in, row(w["ln_ple"][layer]), fetch("w_ple_gate", layer, x_in), p[layer],
                           fetch("w_ple_proj", layer, x_in), name=f"ple_{layer}")

    x0 = x
    h_a, r_a, pre_a = norm_matmul(x0, row(w["ln_mix_a"][0]), fetch("w_in_a", 0, x0), name="sgu_in")
    y_a = sgu_forward(pre_a, row(w["g_v_a"][0]), w_s, b_full, name="sgu_mix")
    x1 = matmul_residual(y_a, fetch("w_out_a", 0, y_a), x0, name="sgu_out")
    x2, mlp0 = mlp_forward(x1, 0)
    ple0 = ple(x2, 0)
    x3 = ple0[4]
    h_kv, r_kv, kv_pre = norm_matmul(x3, row(w["ln_kv"]), fetch("w_kv", 0, x3), name="kv_proj")
    k_n, v_b = head_norm(kv_pre, g128(w["g_k"]), name="k_norm", passthrough=True)
    h_q, r_q, q_pre = norm_matmul(x3, row(w["ln_mix_b"][0]), fetch("w_q", 0, k_n), name="q_proj")
    q_n = head_norm(q_pre, g128(w["g_q"][0]), name="q_norm", scale=scale)
    o = stick_breaking_forward(q_n, k_n, v_b, name="sb_fwd")
    if late is not None:
        late.pass_on("w_up", 1, o)
    x4 = matmul_residual(o, fetch("w_out_b", 0, o), x3, name="sb_out")
    x5, mlp1 = mlp_forward(x4, 1)
    ple1 = ple(x5, 1)
    x6 = ple1[4]
    loss_blk, dx = loss_forward(x6, target, name="loss")

    g = {}
    dx, dwg1, dwp1, dlnp1 = _ple_backward(dx, (x5,) + tuple(ple1[:4]), p[1], row(w["ln_ple"][1]),
                                          mats[("w_ple_gate", 1)], 1)
    dx, dwu1, dwd1, dlnm1 = _mlp_backward(dx, mlp1, row(w["ln_mlp"][1]), mats[("w_up", 1)], mats[("w_down", 1)], 1)
    g["w_out_b"] = matmul_tn(o, dx, name="d_w_out_b", col_shards=False)
    do = matmul_nt(dx, mats[("w_out_b", 0)], name="d_sb_out", out_dtype=BF16)
    dq_n, dk_n, dv = stick_breaking_backward(q_n, k_n, v_b, do, name="sb_bwd")
    dq_pre, dgq = head_norm_backward(dq_n, q_pre, g128(w["g_q"][0]), name="d_q_norm", scale=scale)
    dkv_pre, dgk = head_norm_backward(dk_n, kv_pre, g128(w["g_k"]), name="d_k_norm", passthrough=dv)
    g["w_q"] = matmul_tn(h_q, dq_pre, name="d_w_q", col_shards=False)
    g["w_kv"] = matmul_tn(h_kv, dkv_pre, name="d_w_kv", col_shards=True)
    dx, g["ln_mix_b"] = norm_backward(dq_pre, mats[("w_q", 0)], x3, row(w["ln_mix_b"][0]), r_q, dx, name="d_q_in")
    dx, g["ln_kv"] = norm_backward(dkv_pre, mats[("w_kv", 0)], x3, row(w["ln_kv"]), r_kv, dx, name="d_kv_in")
    g["g_q"] = dgq[:, :HEAD_DIM] + dgq[:, HEAD_DIM:]
    g["g_k"] = (dgk[:, :HEAD_DIM] + dgk[:, HEAD_DIM:]).reshape(HEAD_DIM)
    g["ln_kv"] = g["ln_kv"].reshape(D_MODEL)
    if late is not None:
        late.pair_start({("w_kv", 0): g["w_kv"], ("w_q", 0): g["w_q"], ("w_out_b", 0): g["w_out_b"],
                         ("w_up", 1): dwu1, ("w_down", 1): dwd1, ("w_ple_gate", 1): dwg1, ("w_ple_proj", 1): dwp1}, dx)
    dx, dwg0, dwp0, dlnp0 = _ple_backward(dx, (x2,) + tuple(ple0[:4]), p[0], row(w["ln_ple"][0]),
                                          mats[("w_ple_gate", 0)], 0)
    if late is not None:
        late.chip_start(dx)
    dx, dwu0, dwd0, dlnm0 = _mlp_backward(dx, mlp0, row(w["ln_mlp"][0]), mats[("w_up", 0)], mats[("w_down", 0)], 0)
    if late is not None:
        late.pair_start({("w_up", 0): dwu0, ("w_down", 0): dwd0, ("w_ple_gate", 0): dwg0, ("w_ple_proj", 0): dwp0}, dx)
    g["w_out_a"] = matmul_tn(y_a, dx, name="d_w_out_a", col_shards=False)
    dy_a = matmul_nt(dx, mats[("w_out_a", 0)], name="d_sgu_out")
    dpre_a, dws, db, g["g_v_a"] = sgu_backward(dy_a, pre_a, row(w["g_v_a"][0]), w_s, b_full, name="d_sgu_mix")
    if late is not None:
        late.chip_start(dpre_a)
    g["w_in_a"] = matmul_tn(h_a, dpre_a, name="d_w_in_a", col_shards=True)
    dx, g["ln_mix_a"] = norm_backward(dpre_a, mats[("w_in_a", 0)], x0, row(w["ln_mix_a"][0]), r_a, dx, name="d_sgu_in")
    g["w_spatial"] = dws[None]
    g["b_spatial"] = jnp.transpose(db[:, :N_GROUPS])[None]
    g["w_up"] = (dwu0, dwu1)
    g["w_down"] = (dwd0, dwd1)
    g["w_ple_gate"] = (dwg0, dwg1)
    g["w_ple_proj"] = (dwp0, dwp1)
    g["ln_mlp"] = jnp.concatenate([dlnm0, dlnm1], axis=0)
    g["ln_ple"] = jnp.concatenate([dlnp0, dlnp1], axis=0)
    return loss_blk, dx, g


ANY = pl.BlockSpec(memory_space=pl.ANY)


def _place():
    x, y, c = lax.axis_index("x"), lax.axis_index("y"), lax.axis_index("c")
    others = [(1 - x, y), (x, 1 - y), (1 - x, 1 - y)]
    return x, y, c, 2 * x + y, others


def cast_into_slot(w3, layer, slot, *, name, after=None, tm=512):
    _, r, c = w3.shape
    tm = min(tm, r)

    def body(slot_ref, w_ref, *rest):
        rest[-1][...] = w_ref[...].astype(BF16)

    in_specs = [pl.BlockSpec((None, tm, c), lambda i, s: (layer, i, 0))]
    args = [slot, w3]
    if after is not None:
        in_specs.append(ANY)
        args.append(after)
    return _pcall(body, name=name, out_shape=_sds((N_SHARDS, r, c), BF16), grid=(r // tm,), num_prefetch=1,
                  in_specs=in_specs, out_specs=pl.BlockSpec((None, tm, c), lambda i, s: (s[0], i, 0)),
                  semantics=("parallel",))(*args)


def gather_vectors(vecs, *, name):
    n = len(vecs)

    def body(*refs):
        src, out = refs[:n], refs[n:2 * n]
        send, recv, loc = refs[2 * n:]
        x, y, c, s_me, others = _place()

        def copy(l, k, slot):
            ox, oy = others[k]
            return pltpu.make_async_remote_copy(src[l], out[l].at[slot], send.at[l, k], recv.at[l, k],
                                                device_id=(ox, oy, c), device_id_type=MESH)

        for l in range(n):
            for k in range(3):
                copy(l, k, s_me).start()
        for l in range(n):
            own = pltpu.make_async_copy(src[l], out[l].at[s_me], loc)
            own.start()
            own.wait()
        for l in range(n):
            for k in range(3):
                ox, oy = others[k]
                copy(l, k, 2 * ox + oy).wait_recv()
                copy(l, k, s_me).wait_send()

    return _pcall(body, name=name, out_shape=[_sds((N_SHARDS,) + v.shape, F32) for v in vecs], in_specs=[ANY] * n,
                  out_specs=[ANY] * n,
                  scratch_shapes=[pltpu.SemaphoreType.DMA((n, 3)), pltpu.SemaphoreType.DMA((n, 3)),
                                  pltpu.SemaphoreType.DMA(())],
                  side_effects=True)(*vecs)


HBM = pl.BlockSpec(memory_space=pltpu.HBM)
SEM = pl.BlockSpec(memory_space=pltpu.SEMAPHORE)
DATAFLOW = pltpu.SideEffectType.DATAFLOW_SIDE_EFFECTING


def _split_call(body, *, name, out_shape, in_specs, out_specs, aliases, views_of=()):
    def make(wrap, specs):
        body_ = wrap(body)
        return pl.pallas_call(body_, name=name, out_shape=out_shape, in_specs=specs, out_specs=out_specs,
                              input_output_aliases=aliases,
                              compiler_params=pltpu.CompilerParams(has_side_effects=DATAFLOW))

    return lambda *args: _in_order(make, in_specs, args, views_of)


def _token_shape():
    return jax.ShapeDtypeStruct((8, LANES), F32)


def gather_start(mats, after, *, name):
    n = len(mats)
    halves = [pltpu.with_memory_space_constraint(m.reshape(N_SHARDS, 2, m.shape[1] // 2, m.shape[2]), pltpu.HBM)
              for m in mats]

    def body(*refs):
        send, recv = refs[n + 1], refs[n + 2]
        out, token = refs[n + 3:2 * n + 3], refs[2 * n + 3]
        x, y, c, s_me, others = _place()
        for l in range(n):
            for k in range(3):
                ox, oy = others[k]
                pltpu.make_async_remote_copy(out[l].at[s_me, c], out[l].at[s_me, c], send.at[3 * l + k],
                                             recv.at[3 * l + k], device_id=(ox, oy, c), device_id_type=MESH).start()
        token[...] = jnp.zeros_like(token)

    res = _split_call(
        body, name=name,
        out_shape=(pltpu.SemaphoreType.DMA((3 * n,)), pltpu.SemaphoreType.DMA((3 * n,)),
                   *[pltpu.HBM(h.shape, BF16) for h in halves], _token_shape()),
        in_specs=[HBM] * n + [ANY], out_specs=(SEM, SEM, *[HBM] * n, pl.BlockSpec(memory_space=pltpu.VMEM)),
        aliases={l: 2 + l for l in range(n)}, views_of=mats)(*halves, after)
    return res[0], res[1], list(res[2:2 + n]), res[2 + n]


def gather_pass_on(bufs, send_a, recv_a, after, *, name, base=0):
    n = len(bufs)

    def body(*refs):
        send_a, recv_a = refs[n], refs[n + 1]
        out = refs[n + 3:2 * n + 3]
        send_b, recv_b, token = refs[2 * n + 3:]
        x, y, c, s_me, others = _place()
        for l in range(n):
            for k in range(3):
                ox, oy = others[k]
                landed, i = out[l].at[2 * ox + oy, c], 3 * l + k
                pltpu.make_async_remote_copy(landed, landed, send_a.at[3 * base + i], recv_a.at[3 * base + i],
                                             device_id=(x, y, 1 - c), device_id_type=MESH).wait_recv()
                pltpu.make_async_remote_copy(landed, landed, send_b.at[i], recv_b.at[i],
                                             device_id=(x, y, 1 - c), device_id_type=MESH).start()
        for l in range(n):
            for k in range(3):
                mine, i = out[l].at[s_me, c], 3 * (base + l) + k
                pltpu.make_async_remote_copy(mine, mine, send_a.at[i], recv_a.at[i],
                                             device_id=(x, y, 1 - c), device_id_type=MESH).wait_send()
        token[...] = jnp.zeros_like(token)

    res = _split_call(
        body, name=name,
        out_shape=(*[pltpu.HBM(b.shape, BF16) for b in bufs], pltpu.SemaphoreType.DMA((3 * n,)),
                   pltpu.SemaphoreType.DMA((3 * n,)), _token_shape()),
        in_specs=[HBM] * n + [SEM, SEM, ANY],
        out_specs=(*[HBM] * n, SEM, SEM, pl.BlockSpec(memory_space=pltpu.VMEM)),
        aliases={l: l for l in range(n)})(*bufs, send_a, recv_a, after)
    return list(res[:n]), res[n], res[n + 1], res[n + 2]


def gather_finish(bufs, send_b, recv_b, after, shapes, *, name):
    n = len(bufs)

    def body(*refs):
        send_b, recv_b = refs[n], refs[n + 1]
        out = refs[n + 3:]
        x, y, c, _, others = _place()
        for l in range(n):
            for k in range(3):
                ox, oy = others[k]
                theirs, mine, i = out[l].at[2 * ox + oy, 1 - c], out[l].at[2 * ox + oy, c], 3 * l + k
                pltpu.make_async_remote_copy(theirs, theirs, send_b.at[i], recv_b.at[i],
                                             device_id=(x, y, 1 - c), device_id_type=MESH).wait_recv()
                pltpu.make_async_remote_copy(mine, mine, send_b.at[i], recv_b.at[i],
                                             device_id=(x, y, 1 - c), device_id_type=MESH).wait_send()

    res = _split_call(
        body, name=name, out_shape=tuple(pltpu.HBM(b.shape, BF16) for b in bufs),
        in_specs=[HBM] * n + [SEM, SEM, ANY], out_specs=tuple([HBM] * n),
        aliases={l: l for l in range(n)})(*bufs, send_b, recv_b, after)
    return [r.reshape(s) for r, s in zip(res, shapes)]


def exchange_start(srcs, dst_shapes, dst_dtype, plan, count, after, *, name):
    n, m = len(srcs), len(dst_shapes)
    given = list(srcs)
    srcs = [pltpu.with_memory_space_constraint(s, pltpu.HBM) for s in srcs]
    lands = [pltpu.with_memory_space_constraint(lax.empty(s, dst_dtype), pltpu.HBM) for s in dst_shapes]

    def body(*refs):
        send, recv = refs[n + m + 1], refs[n + m + 2]
        src, dst, token = refs[n + m + 3:2 * n + m + 3], refs[2 * n + m + 3:2 * (n + m) + 3], refs[2 * (n + m) + 3]
        for i, (s, d, dev) in enumerate(plan(_place(), src, dst)):
            pltpu.make_async_remote_copy(s, d, send.at[i], recv.at[i], device_id=dev, device_id_type=MESH).start()
        token[...] = jnp.zeros_like(token)

    res = _split_call(
        body, name=name,
        out_shape=(pltpu.SemaphoreType.DMA((count,)), pltpu.SemaphoreType.DMA((count,)),
                   *[pltpu.HBM(s.shape, s.dtype) for s in srcs], *[pltpu.HBM(s, dst_dtype) for s in dst_shapes],
                   _token_shape()),
        in_specs=[HBM] * (n + m) + [ANY],
        out_specs=(SEM, SEM, *[HBM] * (n + m), pl.BlockSpec(memory_space=pltpu.VMEM)),
        aliases={i: 2 + i for i in range(n + m)}, views_of=given)(*srcs, *lands, after)
    return (list(res[2:2 + n]), list(res[2 + n:2 + n + m]), res[0], res[1], plan), res[2 + n + m]


def exchange_finish(state, after, *, name):
    srcs, lands, send, recv, plan = state
    n, m = len(srcs), len(lands)

    def body(*refs):
        send, recv = refs[n + m], refs[n + m + 1]
        src, dst = refs[n + m + 3:2 * n + m + 3], refs[2 * n + m + 3:]
        for i, (s, d, dev) in enumerate(plan(_place(), src, dst)):
            pltpu.make_async_remote_copy(s, d, send.at[i], recv.at[i], device_id=dev, device_id_type=MESH).wait()

    res = _split_call(
        body, name=name,
        out_shape=tuple(pltpu.HBM(a.shape, a.dtype) for a in srcs + lands),
        in_specs=[HBM] * (n + m) + [SEM, SEM, ANY], out_specs=tuple([HBM] * (n + m)),
        aliases={i: i for i in range(n + m)})(*srcs, *lands, send, recv, after)
    return list(res[:n]), list(res[n:])


def pair_plan(place, src, dst):
    x, y, c, _, _ = place
    return [(s.at[:, 1 - c], d, (x, y, 1 - c)) for s, d in zip(src, dst)]


def chip_plan(place, src, dst):
    x, y, c, _, others = place
    return [(s.at[2 * ox + oy], d.at[k], (ox, oy, c)) for s, d in zip(src, dst) for k, (ox, oy) in enumerate(others)]


def pair_exchange(grads, *, name):
    n = len(grads)

    def body(*refs):
        src, got = refs[:n], refs[n:2 * n]
        send, recv = refs[2 * n:]
        x, y, c, _, _ = _place()

        def swap(l):
            return pltpu.make_async_remote_copy(src[l].at[:, 1 - c], got[l], send.at[l], recv.at[l],
                                                device_id=(x, y, 1 - c), device_id_type=MESH)

        for l in range(n):
            swap(l).start()
        for l in range(n):
            swap(l).wait()

    res = _pcall(body, name=name, out_shape=[_sds((N_SHARDS,) + g.shape[2:], F32) for g in grads],
                 in_specs=[ANY] * n, out_specs=[ANY] * n,
                 scratch_shapes=[pltpu.SemaphoreType.DMA((n,)), pltpu.SemaphoreType.DMA((n,))],
                 side_effects=True)(*grads)
    return list(res)


def add_to_wire(mine, theirs, core, *, name, tm=512):
    s, _, r, c = mine.shape
    tm = min(tm, r)

    def body(core_ref, a_ref, b_ref, o_ref):
        o_ref[...] = (a_ref[...] + b_ref[...]).astype(BF16)

    spec = pl.BlockSpec((None, tm, c), lambda i, j, cr: (i, j, 0))
    return _pcall(body, name=name, out_shape=_sds((s, r, c), BF16), grid=(s, r // tm), num_prefetch=1,
                  in_specs=[pl.BlockSpec((None, None, tm, c), lambda i, j, cr: (i, cr[0], j, 0)), spec],
                  out_specs=spec, semantics=("parallel", "parallel"))(core, mine, theirs)


def sum_chips(wire, landed, place, dest, layer, n_layers, *, name, tm=512):
    _, r, c = wire.shape
    tm = min(tm, r)

    def body(place_ref, w_ref, l_ref, *rest):
        o_ref = rest[-1]
        o_ref[...] = ((w_ref[...].astype(F32) + l_ref[0].astype(F32)) + l_ref[1].astype(F32)) + l_ref[2].astype(F32)

    in_specs = [pl.BlockSpec((None, tm, c), lambda i, pr: (pr[0], i, 0)),
                pl.BlockSpec((3, tm, c), lambda i, pr: (0, i, 0))]
    args = [place, wire, landed]
    aliases = None
    if dest is not None:
        in_specs.append(ANY)
        args.append(dest)
        aliases = {3: 0}
    return _pcall(body, name=name, out_shape=_sds((n_layers, 2, r, c), F32), grid=(r // tm,), num_prefetch=1,
                  in_specs=in_specs,
                  out_specs=pl.BlockSpec((None, None, tm, c), lambda i, pr: (layer, pr[1], i, 0)),
                  aliases=aliases, semantics=("parallel",))(*args)


def pair_share(bufs, slots, *, name):
    n = len(bufs)

    def body(*refs):
        out = refs[n:2 * n]
        send, recv = refs[2 * n:]
        x, y, c, _, _ = _place()

        def share(i, half):
            o, l = slots[i]
            return pltpu.make_async_remote_copy(out[o].at[l, half], out[o].at[l, half], send.at[i], recv.at[i],
                                                device_id=(x, y, 1 - c), device_id_type=MESH)

        for i in range(len(slots)):
            share(i, c).start()
        for i in range(len(slots)):
            share(i, 1 - c).wait_recv()
            share(i, c).wait_send()

    res = _pcall(body, name=name, out_shape=[_sds(b.shape, F32) for b in bufs], in_specs=[ANY] * n,
                 out_specs=[ANY] * n,
                 scratch_shapes=[pltpu.SemaphoreType.DMA((len(slots),)), pltpu.SemaphoreType.DMA((len(slots),))],
                 aliases={o: o for o in range(n)}, side_effects=True)(*bufs)
    return list(res)


def all_reduce_small(packed, *, name):
    n_dev, r, c = packed.shape

    def body(in_ref, out_ref, land, send, recv):
        x, y, cc, _, _ = _place()
        me = 4 * x + 2 * y + cc
        peers = [(px, py, pc) for px in range(2) for py in range(2) for pc in range(2)]

        def scatter(d):
            return pltpu.make_async_remote_copy(in_ref.at[d], land.at[me], send.at[0, d], recv.at[0, me],
                                                device_id=peers[d], device_id_type=MESH)

        def gather(d):
            return pltpu.make_async_remote_copy(out_ref.at[me], out_ref.at[me], send.at[1, d], recv.at[1, me],
                                                device_id=peers[d], device_id_type=MESH)

        for d in range(n_dev):
            @pl.when(d != me)
            def _():
                scatter(d).start()
        land[me] = in_ref[me]
        for d in range(n_dev):
            @pl.when(d != me)
            def _():
                pltpu.make_async_remote_copy(in_ref.at[d], land.at[d], send.at[0, d], recv.at[0, d],
                                             device_id=peers[d], device_id_type=MESH).wait_recv()
        total = land[0]
        for d in range(1, n_dev):
            total = total + land[d]
        out_ref[me] = total
        for d in range(n_dev):
            @pl.when(d != me)
            def _():
                gather(d).start()
        for d in range(n_dev):
            @pl.when(d != me)
            def _():
                pltpu.make_async_remote_copy(out_ref.at[d], out_ref.at[d], send.at[1, d], recv.at[1, d],
                                             device_id=peers[d], device_id_type=MESH).wait_recv()
        for d in range(n_dev):
            @pl.when(d != me)
            def _():
                scatter(d).wait_send()
                gather(d).wait_send()

    vm = pl.BlockSpec(memory_space=pltpu.VMEM)
    return _pcall(body, name=name, out_shape=_sds(packed.shape, F32), in_specs=[vm], out_specs=vm,
                  scratch_shapes=[pltpu.VMEM(packed.shape, F32), pltpu.SemaphoreType.DMA((2, n_dev)),
                                  pltpu.SemaphoreType.DMA((2, n_dev))],
                  side_effects=True)(packed)


def adamw(w, g, m, v, *, name, part=None, dest=None, tm=512):
    shape = w.shape
    cols = shape[-1]
    rows = 1
    for s in shape[:-1]:
        rows *= s
    first, count = 0, rows
    if part is not None:
        count = rows // part[1]
        first = part[0] * count
    tm = min(tm, count)
    assert count % tm == 0
    two_d = lambda a: a.reshape(rows, cols)

    def body(w_ref, g_ref, m_ref, v_ref, *rest):
        d_ref, mo_ref, vo_ref = rest[-3:]
        gv = g_ref[...]
        m_new = ADAM_B1 * m_ref[...] + (1.0 - ADAM_B1) * gv
        v_new = ADAM_B2 * v_ref[...] + (1.0 - ADAM_B2) * (gv * gv)
        m_hat = m_new / (1.0 - ADAM_B1 ** ADAM_STEP)
        v_hat = v_new / (1.0 - ADAM_B2 ** ADAM_STEP)
        d_ref[...] = -ADAM_LR * (m_hat / (jnp.sqrt(v_hat) + ADAM_EPS) + ADAM_WD * w_ref[...])
        mo_ref[...] = m_new
        vo_ref[...] = v_new

    spec = pl.BlockSpec((tm, cols), lambda i: (first // tm + i, 0))
    args = [two_d(w), two_d(g), two_d(m), two_d(v)]
    in_specs = [spec] * 4
    aliases = None
    if dest is not None:
        args += [two_d(d) for d in dest]
        in_specs = in_specs + [ANY] * 3
        aliases = {4: 0, 5: 1, 6: 2}
    outs = _pcall(body, name=name, out_shape=[_sds((rows, cols), F32)] * 3, grid=(count // tm,), in_specs=in_specs,
                  out_specs=[spec] * 3, aliases=aliases, semantics=("parallel",))(*args)
    return [o.reshape(shape) for o in outs]


WEIGHTS = ("ln_mix_a", "w_in_a", "g_v_a", "w_spatial", "b_spatial", "w_out_a", "ln_kv", "w_kv", "g_k", "ln_mix_b",
           "w_q", "g_q", "w_out_b", "ln_mlp", "w_up", "w_down", "ln_ple", "w_ple_gate", "w_ple_proj")
MATRICES = (("w_in_a", 1, True), ("w_out_a", 1, False), ("w_kv", 0, True), ("w_q", 1, False), ("w_out_b", 1, False),
            ("w_up", 2, True), ("w_down", 2, False), ("w_ple_gate", 2, False), ("w_ple_proj", 2, True))
GATHER_STAGES = ((("w_in_a", 0),), (("w_out_a", 0),), (("w_up", 0),), (("w_down", 0),),
                 (("w_ple_gate", 0), ("w_ple_proj", 0), ("w_kv", 0)), (("w_q", 0),), (("w_out_b", 0),),
                 (("w_up", 1), ("w_down", 1), ("w_ple_gate", 1), ("w_ple_proj", 1)))
REPLICATED = ("w_spatial", "b_spatial", "ln_kv", "g_k", "ln_mix_b", "g_q", "ln_mlp", "ln_ple")
SHARDED_VECTORS = ("ln_mix_a", "g_v_a")
SMALL_ROWS = 18


def kernel(x, p, ln_mix_a, w_in_a, g_v_a, w_spatial, b_spatial, w_out_a, ln_kv, w_kv, g_k, ln_mix_b, w_q, g_q, w_out_b, ln_mlp, w_up, w_down, ln_ple, w_ple_gate, w_ple_proj, loss_target, m_ln_mix_a, m_w_in_a, m_g_v_a, m_w_spatial, m_b_spatial, m_w_out_a, m_ln_kv, m_w_kv, m_g_k, m_ln_mix_b, m_w_q, m_g_q, m_w_out_b, m_ln_mlp, m_w_up, m_w_down, m_ln_ple, m_w_ple_gate, m_w_ple_proj, v_ln_mix_a, v_w_in_a, v_g_v_a, v_w_spatial, v_b_spatial, v_w_out_a, v_ln_kv, v_w_kv, v_g_k, v_ln_mix_b, v_w_q, v_g_q, v_w_out_b, v_ln_mlp, v_w_up, v_w_down, v_ln_ple, v_w_ple_gate, v_w_ple_proj):
    given = dict(locals())
    _PREVIOUS.clear()
    weights = {n: given[n] for n in WEIGHTS}
    shard = 2 * lax.axis_index("x") + lax.axis_index("y")
    core = lax.axis_index("c")
    shard_1 = shard.astype(jnp.int32).reshape(1)
    core_1 = core.astype(jnp.int32).reshape(1)
    place = jnp.stack([shard, core]).astype(jnp.int32)

    col_sharded = {name: cols for name, _, cols in MATRICES}
    layer_count = {name: max(layers, 1) for name, layers, _ in MATRICES}

    def cast(key, after):
        name, layer = key
        w3 = weights[name] if weights[name].ndim == 3 else weights[name][None]
        return (name, layer, col_sharded[name],
                cast_into_slot(w3, layer, shard_1, name=f"cast_{name}_{layer}", after=after))

    head = [cast(key, None) for key in GATHER_STAGES[0]]
    send_h, recv_h, flying_h, token_h = gather_start([lf[3] for lf in head], shard_1, name="gather_start_0")
    tail = [cast(key, token_h) for stage in GATHER_STAGES[1:] for key in stage]
    vec_a = gather_vectors([ln_mix_a, g_v_a], name="gather_vectors")
    send_a, recv_a, flying, token = gather_start([lf[3] for lf in tail], vec_a[0], name="gather_start_1")

    w = {"ln_mix_a": vec_a[0].reshape(1, D_MODEL),
         "g_v_a": vec_a[1].reshape(1, D_MODEL)}
    for name in REPLICATED:
        w[name] = weights[name]

    class Late:
        passed = {}

        def pass_on(self, name, layer, after):
            stage = [(name, layer) in s for s in GATHER_STAGES].index(True)
            if stage not in self.passed:
                if stage == 0:
                    base, members, sems, fly = 0, head, (send_h, recv_h), flying_h
                else:
                    base = sum(len(s) for s in GATHER_STAGES[1:stage])
                    members, sems, fly = tail[base:base + len(GATHER_STAGES[stage])], (send_a, recv_a), flying
                self.passed[stage] = (members, gather_pass_on(fly[base:base + len(members)], sems[0], sems[1], after,
                                                              name=f"gather_pass_on_{stage}", base=base))
            return stage

        def weights(self, name, layer, after):
            stage = self.pass_on(name, layer, after)
            members, (bufs, send_b, recv_b, tok) = self.passed[stage]
            got = gather_finish(bufs, send_b, recv_b, tok, [lf[3].shape for lf in members],
                                name=f"gather_finish_{stage}")
            out = {}
            for (leaf_name, leaf_layer, cols, _), arr in zip(members, got):
                out[(leaf_name, leaf_layer)] = arr if cols else arr.reshape(N_SHARDS * arr.shape[1], arr.shape[2])
            return out

        groups = []

        def pair_start(self, grads_done, after):
            self.keys = sorted(grads_done)
            views = [view(k, grads_done[k]) for k in self.keys]
            self.pair, token = exchange_start(views, [(N_SHARDS,) + v.shape[2:] for v in views], F32, pair_plan,
                                              len(views), after, name=f"grad_pair_start_{len(self.groups)}")
            return token

        def chip_start(self, after):
            tag = len(self.groups)
            mine, theirs = exchange_finish(self.pair, after, name=f"grad_pair_finish_{tag}")
            wire = [add_to_wire(a, b, core_1, name=f"grad_pair_sum_{tag}_{i}")
                    for i, (a, b) in enumerate(zip(mine, theirs))]
            chip, token = exchange_start(wire, [(3,) + v.shape[1:] for v in wire], BF16, chip_plan, 3 * len(wire),
                                         theirs[-1], name=f"grad_chip_start_{tag}")
            self.groups.append((self.keys, chip))
            return token

    def view(key, arr):
        rows = arr.shape[-2] if col_sharded[key[0]] else arr.shape[0] // N_SHARDS
        return arr.reshape(N_SHARDS, 2, rows // 2, arr.shape[-1])

    t = x.shape[1]
    late = Late()
    loss_blk, dx, g = local_step(x[0], p.reshape(2, t, PLE_DIM), loss_target[0], w, late)

    sent = {k for keys, _ in late.groups for k in keys}
    keys_last = [(name, layer) for name, layers, _ in MATRICES for layer in range(max(layers, 1))
                 if (name, layer) not in sent]
    views = [view(k, g[k[0]][k[1]] if layer_count[k[0]] == 2 else g[k[0]]) for k in keys_last]

    theirs = pair_exchange(views, name="grad_pair_exchange_last")
    wire_0 = [add_to_wire(a, b, core_1, name=f"grad_pair_sum_last_{i}") for i, (a, b) in enumerate(zip(views, theirs))]
    chip_0, token_0 = exchange_start(wire_0, [(3,) + v.shape[1:] for v in wire_0], BF16, chip_plan, 3 * len(wire_0),
                                     theirs[-1], name="grad_chip_start_last")

    grads, bufs = {}, {}

    def sum_and_share(keys, wire, landed, tag):
        for i, (key, wv, lv) in enumerate(zip(keys, wire, landed)):
            name, layer = key
            bufs[name] = sum_chips(wv, lv, place, bufs.get(name), layer, layer_count[name],
                                   name=f"grad_chip_sum_{tag}_{i}")
        names = sorted({k[0] for k in keys})
        shared = pair_share([bufs[n] for n in names], [(names.index(k[0]), k[1]) for k in keys],
                            name=f"grad_pair_share_{tag}")
        bufs.update(zip(names, shared))

    updates = {}

    def update(n, gn, part=None):
        wn, mn, vn = weights[n], given["m_" + n], given["v_" + n]
        if wn.ndim == 1:
            wn, gn, mn, vn = (a.reshape(1, -1) for a in (wn, gn, mn, vn))
        tag = "" if part is None else f"_{part[0]}"
        updates[n] = adamw(wn, gn.reshape(wn.shape), mn, vn, name=f"adamw_{n}{tag}", part=part, dest=updates.get(n))

    after = token_0
    for tag, (keys, chip) in enumerate(late.groups + [(keys_last, chip_0)]):
        wire, landed = exchange_finish(chip, after, name=f"grad_chip_finish_{tag}")
        sum_and_share(keys, wire, landed, tag)
        for name, layer in keys:
            update(name, bufs[name], (layer, layer_count[name]) if layer_count[name] == 2 else None)
        after = updates[keys[-1][0]][0]

    small = REPLICATED + SHARDED_VECTORS
    flat = jnp.concatenate([g[n].reshape(-1) for n in small] + [loss_blk[0, :1]])
    room = 8 * SMALL_ROWS * D_MODEL
    flat = jnp.concatenate([flat, jnp.zeros((room - flat.shape[0],), F32)])
    reduced = all_reduce_small(flat.reshape(8, SMALL_ROWS, D_MODEL), name="grad_small_all_reduce").reshape(-1)
    loss = reduced[sum(g[n].size for n in small)]
    at = 0
    for n in small:
        size = g[n].size
        piece = reduced[at:at + size]
        at += size
        if n in SHARDED_VECTORS:
            per = D_MODEL // N_SHARDS
            grads[n] = lax.dynamic_slice(piece, (shard * per,), (per,)).reshape(weights[n].shape)
        else:
            grads[n] = piece.reshape(weights[n].shape)
        update(n, grads[n])
    for name, _, _ in MATRICES:
        grads[name] = bufs[name].reshape(weights[name].shape)
    delta = {n: updates[n][0].reshape(weights[n].shape) for n in WEIGHTS}
    new_m = {n: updates[n][1].reshape(weights[n].shape) for n in WEIGHTS}
    new_v = {n: updates[n][2].reshape(weights[n].shape) for n in WEIGHTS}
    return (loss, dx.reshape(x.shape), *[grads[n] for n in WEIGHTS], *[delta[n] for n in WEIGHTS],
            *[new_m[n] for n in WEIGHTS], *[new_v[n] for n in WEIGHTS])
```

```python
import jax
import jax.numpy as jnp
from jax import lax
from jax.experimental import pallas as pl
from jax.experimental.pallas import tpu as pltpu

F32 = jnp.float32
BF16 = jnp.bfloat16

D_MODEL = 1024
D_FF = 4096
PLE_DIM = 256
N_GROUPS = 8
CHUNK = 128
HEAD_DIM = 64
LANES = 128
ATT_K_BLOCK = 256
ATT_Q_BLOCK = 512
EPS = 1e-6
N_SHARDS = 4
VMEM_LIMIT = 56 * 1024 * 1024

ADAM_LR = 0.001
ADAM_B1 = 0.9
ADAM_B2 = 0.999
ADAM_EPS = 1e-08
ADAM_WD = 0.01
ADAM_STEP = 10

MESH = pl.DeviceIdType.MESH


_PREVIOUS = []


def _in_order(make, in_specs, args, views_of=()):
    previous = _PREVIOUS[-1] if _PREVIOUS else None
    if previous is not None and any(a is previous for a in (*args, *views_of)):
        previous = None
    if previous is None:
        result = make(lambda body: body, list(in_specs))(*args)
    else:
        count = len(args)

        def skip(body):
            return lambda *refs: body(*refs[:count], *refs[count + 1:])

        result = make(skip, list(in_specs) + [pl.BlockSpec(memory_space=pl.ANY)])(*args, previous)
    _PREVIOUS[:] = [jax.tree_util.tree_leaves(result)[-1]]
    return result


def _pcall(body, *, name, out_shape, grid=None, in_specs=None, out_specs=None, scratch_shapes=(),
           semantics=None, aliases=None, side_effects=False, num_prefetch=0):
    params = dict(vmem_limit_bytes=VMEM_LIMIT)
    if semantics is not None:
        params["dimension_semantics"] = semantics
    if side_effects:
        params["has_side_effects"] = True
    kwargs = {}
    if aliases:
        kwargs["input_output_aliases"] = aliases

    def make(wrap, specs):
        body_ = wrap(body)
        if num_prefetch:
            spec = pltpu.PrefetchScalarGridSpec(num_scalar_prefetch=num_prefetch, grid=grid, in_specs=specs,
                                                out_specs=out_specs, scratch_shapes=list(scratch_shapes))
            return pl.pallas_call(body_, name=name, out_shape=out_shape, grid_spec=spec,
                                  compiler_params=pltpu.CompilerParams(**params), **kwargs)
        more = dict(kwargs, in_specs=specs)
        if grid is not None:
            more["grid"] = grid
        if out_specs is not None:
            more["out_specs"] = out_specs
        return pl.pallas_call(body_, name=name, out_shape=out_shape, scratch_shapes=list(scratch_shapes),
                              compiler_params=pltpu.CompilerParams(**params), **more)

    return lambda *args: _in_order(make, in_specs, args)


def _sds(shape, dtype):
    return jax.ShapeDtypeStruct(shape, dtype)


_GELU_C = 0.7978845608028654
_GELU_A = 0.044715


def _gelu(x):
    inner = _GELU_C * (x + _GELU_A * (x * x * x))
    return 0.5 * x * (1.0 + jnp.tanh(inner))


def _gelu_grad(x):
    x2 = x * x
    t = jnp.tanh(_GELU_C * (x + _GELU_A * (x2 * x)))
    return 0.5 * (1.0 + t) + 0.5 * x * (1.0 - t * t) * (_GELU_C * (1.0 + 3.0 * _GELU_A * x2))


def _sigmoid(x):
    return 1.0 / (1.0 + jnp.exp(-x))


def _log_sigmoid(z):
    return jnp.minimum(z, 0.0) - jnp.log(1.0 + jnp.exp(-jnp.abs(z)))


def _dot(a, b):
    return jnp.dot(a, b, preferred_element_type=F32)


def _dot_nt(a, b):
    return lax.dot_general(a, b, (((1,), (1,)), ((), ())), preferred_element_type=F32)


def _dot_tn(a, b):
    return lax.dot_general(a, b, (((0,), (0,)), ((), ())), preferred_element_type=F32)


def _head_rstd(x):
    lane = lax.broadcasted_iota(jnp.int32, x.shape, 1)
    low = lane < HEAD_DIM
    sq = x * x
    s_lo = jnp.sum(jnp.where(low, sq, 0.0), axis=-1, keepdims=True)
    s_hi = jnp.sum(jnp.where(low, 0.0, sq), axis=-1, keepdims=True)
    ms = jnp.where(low, s_lo, s_hi) * (1.0 / HEAD_DIM)
    return lax.rsqrt(ms + EPS)


def _head_mean(x):
    lane = lax.broadcasted_iota(jnp.int32, x.shape, 1)
    low = lane < HEAD_DIM
    s_lo = jnp.sum(jnp.where(low, x, 0.0), axis=-1, keepdims=True)
    s_hi = jnp.sum(jnp.where(low, 0.0, x), axis=-1, keepdims=True)
    return jnp.where(low, s_lo, s_hi) * (1.0 / HEAD_DIM)


def _full(shape):
    zeros = (0,) * len(shape)
    return pl.BlockSpec(shape, lambda i: zeros)


def norm_matmul(x, g, w, *, name, epilogue="none", tm=512):
    t, d = x.shape
    sharded = w.ndim == 3
    per = w.shape[2] if sharded else w.shape[1]
    n = N_SHARDS * per if sharded else per
    tm = min(tm, t)

    def body(x_ref, g_ref, w_ref, h_ref, r_ref, *outs):
        xv = x_ref[...]
        r = lax.rsqrt(jnp.mean(xv * xv, axis=-1, keepdims=True) + EPS)
        h = ((xv * r) * g_ref[...]).astype(BF16)
        h_ref[...] = h
        r_ref[...] = r
        for s in range(N_SHARDS if sharded else 1):
            cols = slice(s * per, (s + 1) * per)
            y = _dot(h, w_ref[s] if sharded else w_ref[...])
            if epilogue == "none":
                outs[0][:, cols] = y
            else:
                a = jnp.maximum(y, 0.0)
                outs[0][:, cols] = a.astype(BF16)
                outs[1][:, cols] = (a * a).astype(BF16)

    row = lambda i: (i, 0)
    out_shape = [_sds((t, d), BF16), _sds((t, 1), F32)]
    out_specs = [pl.BlockSpec((tm, d), row), pl.BlockSpec((tm, 1), row)]
    if epilogue == "none":
        out_shape.append(_sds((t, n), F32))
        out_specs.append(pl.BlockSpec((tm, n), row))
    else:
        out_shape += [_sds((t, n), BF16), _sds((t, n), BF16)]
        out_specs += [pl.BlockSpec((tm, n), row)] * 2
    return _pcall(
        body, name=name, out_shape=out_shape, grid=(t // tm,),
        in_specs=[pl.BlockSpec((tm, d), row), _full((1, d)), _full(w.shape)],
        out_specs=out_specs, semantics=("parallel",))(x, g, w)


def matmul_residual(a, w, res, *, name, tm=512):
    t, k = a.shape
    n = w.shape[1]
    tm = min(tm, t)

    def body(a_ref, w_ref, res_ref, o_ref):
        o_ref[...] = res_ref[...] + _dot(a_ref[...], w_ref[...])

    row = lambda i: (i, 0)
    return _pcall(
        body, name=name, out_shape=_sds((t, n), F32), grid=(t // tm,),
        in_specs=[pl.BlockSpec((tm, k), row), _full(w.shape), pl.BlockSpec((tm, n), row)],
        out_specs=pl.BlockSpec((tm, n), row), semantics=("parallel",))(a, w, res)


def ple_forward(x, g, w_gate, p, w_proj, *, name, tm=256):
    t, d = x.shape
    tm = min(tm, t)

    def body(x_ref, g_ref, wg_ref, p_ref, wp_ref, h_ref, r_ref, gate_ref, pp_ref, o_ref):
        xv = x_ref[...]
        r = lax.rsqrt(jnp.mean(xv * xv, axis=-1, keepdims=True) + EPS)
        h = ((xv * r) * g_ref[...]).astype(BF16)
        h_ref[...] = h
        r_ref[...] = r
        gate = _sigmoid(_dot(h, wg_ref[...]))
        gate_ref[...] = gate
        pb = p_ref[...].astype(BF16)
        per = d // N_SHARDS
        for s in range(N_SHARDS):
            cols = slice(s * per, (s + 1) * per)
            pp = _dot(pb, wp_ref[s])
            pp_ref[:, cols] = pp.astype(BF16)
            o_ref[:, cols] = xv[:, cols] + pp * gate[:, cols]

    row = lambda i: (i, 0)
    fixed = lambda i: (0, 0)
    return _pcall(
        body, name=name,
        out_shape=[_sds((t, d), BF16), _sds((t, 1), F32), _sds((t, d), F32), _sds((t, d), BF16), _sds((t, d), F32)],
        grid=(t // tm,),
        in_specs=[pl.BlockSpec((tm, d), row), pl.BlockSpec((1, d), fixed), pl.BlockSpec((d, d), fixed),
                  pl.BlockSpec((tm, PLE_DIM), row),
                  pl.BlockSpec((N_SHARDS, PLE_DIM, d // N_SHARDS), lambda i: (0, 0, 0))],
        out_specs=[pl.BlockSpec((tm, d), row), pl.BlockSpec((tm, 1), row), pl.BlockSpec((tm, d), row),
                   pl.BlockSpec((tm, d), row), pl.BlockSpec((tm, d), row)],
        semantics=("parallel",))(x, g, w_gate, p, w_proj)


def _tril_mask():
    r = lax.broadcasted_iota(jnp.int32, (CHUNK, CHUNK), 0)
    c = lax.broadcasted_iota(jnp.int32, (CHUNK, CHUNK), 1)
    return c <= r


def _sgu_common(pre_ref, gv_ref, ws_ref):
    pre = pre_ref[...]
    pre_u, pre_v = pre[:, :D_MODEL], pre[:, D_MODEL:]
    u = _gelu(pre_u)
    v = _gelu(pre_v)
    r = lax.rsqrt(jnp.mean(v * v, axis=-1, keepdims=True) + EPS)
    vhat = v * r
    vn = (vhat * gv_ref[...]).astype(BF16)
    tril = _tril_mask()
    wm = [jnp.where(tril, ws_ref[g], 0.0).astype(BF16) for g in range(N_GROUPS)]
    return pre_u, pre_v, u, r, vhat, vn, wm, tril


def sgu_forward(pre, g_v, w_s, b_full, *, name):
    t = pre.shape[0]

    def body(pre_ref, gv_ref, ws_ref, b_ref, y_ref):
        _, _, u, _, _, vn, wm, _ = _sgu_common(pre_ref, gv_ref, ws_ref)
        for g in range(N_GROUPS):
            cols = slice(g * LANES, (g + 1) * LANES)
            mix = _dot(wm[g], vn[:, cols]) + b_ref[:, cols]
            y_ref[:, cols] = (u[:, cols] * mix).astype(BF16)

    return _pcall(
        body, name=name, out_shape=_sds((t, D_MODEL), BF16), grid=(t // CHUNK,),
        in_specs=[pl.BlockSpec((CHUNK, 2 * D_MODEL), lambda i: (i, 0)), pl.BlockSpec((1, D_MODEL), lambda i: (0, 0)),
                  pl.BlockSpec((N_GROUPS, CHUNK, CHUNK), lambda i: (0, 0, 0)),
                  pl.BlockSpec((CHUNK, D_MODEL), lambda i: (0, 0))],
        out_specs=pl.BlockSpec((CHUNK, D_MODEL), lambda i: (i, 0)),
        semantics=("parallel",))(pre, g_v, w_s, b_full)


def head_norm(pre, g128, *, name, col_block=0, scale=1.0, passthrough=False, tm=512):
    t = pre.shape[0]
    tm = min(tm, t)

    def body(*refs):
        if passthrough:
            x_ref, v_ref, g_ref, o_ref, vo_ref = refs
            vo_ref[...] = v_ref[...].astype(BF16)
        else:
            x_ref, g_ref, o_ref = refs
        g = g_ref[...] * scale
        for b in range(D_MODEL // LANES):
            cols = slice(b * LANES, (b + 1) * LANES)
            xv = x_ref[:, cols]
            o_ref[:, cols] = ((xv * _head_rstd(xv)) * g).astype(BF16)

    x_spec = pl.BlockSpec((tm, D_MODEL), lambda i: (i, col_block))
    g_spec = pl.BlockSpec((1, LANES), lambda i: (0, 0))
    o_spec = pl.BlockSpec((tm, D_MODEL), lambda i: (i, 0))
    if passthrough:
        return _pcall(body, name=name, out_shape=[_sds((t, D_MODEL), BF16)] * 2, grid=(t // tm,),
                      in_specs=[x_spec, pl.BlockSpec((tm, D_MODEL), lambda i: (i, 1)), g_spec],
                      out_specs=[o_spec, o_spec], semantics=("parallel",))(pre, pre, g128)
    return _pcall(body, name=name, out_shape=_sds((t, D_MODEL), BF16), grid=(t // tm,),
                  in_specs=[x_spec, g_spec], out_specs=o_spec, semantics=("parallel",))(pre, g128)


def _suffix_matrix(n):
    r = lax.broadcasted_iota(jnp.int32, (n, n), 0)
    c = lax.broadcasted_iota(jnp.int32, (n, n), 1)
    return jnp.where(r > c, 1.0, 0.0).astype(BF16)


def _prefix_matrix(n):
    r = lax.broadcasted_iota(jnp.int32, (n, n), 0)
    c = lax.broadcasted_iota(jnp.int32, (n, n), 1)
    return jnp.where(r < c, 1.0, 0.0).astype(BF16)


def _block_cumsum(a, tri):
    return _dot(a.astype(BF16), tri)


def _stacked_causal(nq, nk, shift):
    r = lax.broadcasted_iota(jnp.int32, (2 * nq, nk), 0)
    c = lax.broadcasted_iota(jnp.int32, (2 * nq, nk), 1)
    return c + shift < jnp.where(r >= nq, r - nq, r)


def _att_blocks(t):
    bq, bk = min(ATT_Q_BLOCK, t), min(ATT_K_BLOCK, t)
    return bq, bk, bq // bk


def _stack_heads(a, low):
    zero = jnp.zeros_like(a)
    return jnp.concatenate([jnp.where(low, a, zero), jnp.where(low, zero, a)], axis=0)


def stick_breaking_forward(q, k, v, *, name):
    t = q.shape[0]
    bq, bk, ratio = _att_blocks(t)

    def body(q_ref, k_ref, v_ref, o_ref):
        i = pl.program_id(1)
        low = lax.broadcasted_iota(jnp.int32, (bq, LANES), 1) < HEAD_DIM
        tri = _suffix_matrix(bk)
        qs = _stack_heads(q_ref[...], low)

        def block(j, carry, acc, causal=None):
            rows = pl.ds(pl.multiple_of(j * bk, bk), bk)
            z = _dot_nt(qs, k_ref[rows, :])
            ls = _log_sigmoid(z)
            lg = ls - z
            if causal is not None:
                lg = jnp.where(causal, lg, 0.0)
            s = ls + _block_cumsum(lg, tri) + carry
            a = jnp.exp(s)
            if causal is not None:
                a = jnp.where(causal, a, 0.0)
            acc = acc + _dot(a.astype(BF16), v_ref[rows, :])
            return carry + jnp.sum(lg, axis=-1, keepdims=True), acc

        state = (jnp.zeros((2 * bq, 1), F32), jnp.zeros((2 * bq, LANES), F32))
        for m in reversed(range(ratio)):
            state = block(ratio * i + m, state[0], state[1], _stacked_causal(bq, bk, m * bk))
        first = ratio * i

        def two_blocks(n, st):
            st = block(first - 1 - 2 * n, st[0], st[1])
            return block(first - 2 - 2 * n, st[0], st[1])

        state = lax.fori_loop(0, first // 2, two_blocks, state)
        _, acc = lax.fori_loop(0, first % 2, lambda n, st: block(0, st[0], st[1]), state)
        o_ref[...] = jnp.where(low, acc[:bq], acc[bq:]).astype(BF16)

    return _pcall(
        body, name=name, out_shape=_sds((t, D_MODEL), BF16), grid=(D_MODEL // LANES, t // bq),
        in_specs=[pl.BlockSpec((bq, LANES), lambda p, i: (i, p)), pl.BlockSpec((t, LANES), lambda p, i: (0, p)),
                  pl.BlockSpec((t, LANES), lambda p, i: (0, p))],
        out_specs=pl.BlockSpec((bq, LANES), lambda p, i: (i, p)),
        semantics=("parallel", "arbitrary"))(q, k, v)


def loss_forward(x, target, *, name, tm=512):
    t, d = x.shape
    tm = min(tm, t)

    def body(x_ref, t_ref, l_ref, dx_ref):
        @pl.when(pl.program_id(0) == 0)
        def _():
            l_ref[...] = jnp.zeros_like(l_ref)

        diff = x_ref[...] - t_ref[...]
        dx_ref[...] = diff * (1.0 / d)
        l_ref[...] += 0.5 * jnp.sum(jnp.mean(diff * diff, axis=-1, keepdims=True))

    return _pcall(
        body, name=name, out_shape=[_sds((8, LANES), F32), _sds((t, d), F32)], grid=(t // tm,),
        in_specs=[pl.BlockSpec((tm, d), lambda i: (i, 0))] * 2,
        out_specs=[pl.BlockSpec((8, LANES), lambda i: (0, 0)), pl.BlockSpec((tm, d), lambda i: (i, 0))],
        semantics=("arbitrary",))(x, target)


def matmul_nt(dy, w, *, name, mul=None, out_dtype=F32, tm=512):
    t, n = dy.shape
    k = w.shape[0]
    tm = min(tm, t)

    def body(*refs):
        if mul is None:
            dy_ref, w_ref, o_ref = refs
        else:
            dy_ref, w_ref, m_ref, o_ref = refs
        y = _dot_nt(dy_ref[...].astype(BF16), w_ref[...])
        if mul is not None:
            y = y * (2.0 * m_ref[...].astype(F32))
        o_ref[...] = y.astype(out_dtype)

    row = lambda i: (i, 0)
    in_specs = [pl.BlockSpec((tm, n), row), _full(w.shape)]
    args = [dy, w]
    if mul is not None:
        in_specs.append(pl.BlockSpec((tm, k), row))
        args.append(mul)
    return _pcall(body, name=name, out_shape=_sds((t, k), out_dtype), grid=(t // tm,), in_specs=in_specs,
                  out_specs=pl.BlockSpec((tm, k), row), semantics=("parallel",))(*args)


def matmul_tn(a, dy, *, name, col_shards, tk=512):
    t, k = a.shape
    n = dy.shape[1]
    if col_shards:
        tn = n // N_SHARDS

        def body(a_ref, dy_ref, o_ref):
            o_ref[...] = _dot_tn(a_ref[...].astype(BF16), dy_ref[...].astype(BF16))

        return _pcall(body, name=name, out_shape=_sds((N_SHARDS, k, tn), F32), grid=(N_SHARDS,),
                      in_specs=[_full((t, k)), pl.BlockSpec((t, tn), lambda j: (0, j))],
                      out_specs=pl.BlockSpec((None, k, tn), lambda j: (j, 0, 0)), semantics=("parallel",))(a, dy)

    tk = min(tk, k)

    def body(a_ref, dy_ref, o_ref, dy_bf):
        @pl.when(pl.program_id(0) == 0)
        def _():
            dy_bf[...] = dy_ref[...].astype(BF16)

        o_ref[...] = _dot_tn(a_ref[...].astype(BF16), dy_bf[...])

    return _pcall(body, name=name, out_shape=_sds((k, n), F32), grid=(k // tk,),
                  in_specs=[pl.BlockSpec((t, tk), lambda i: (0, i)), _full((t, n))],
                  out_specs=pl.BlockSpec((tk, n), lambda i: (i, 0)),
                  scratch_shapes=[pltpu.VMEM((t, n), BF16)], semantics=("arbitrary",))(a, dy)


def norm_backward(dpre, w, x, g, rstd, dx_out, *, name, tm=512):
    t, d = x.shape
    n = dpre.shape[1]
    tm = min(tm, t)
    if w.ndim == 3:
        w_spec = pl.BlockSpec(w.shape, lambda i: (0, 0, 0))
    else:
        w_spec = pl.BlockSpec(w.shape, lambda i: (0, 0))

    def body(dp_ref, w_ref, x_ref, g_ref, r_ref, dxo_ref, dx_ref, dg_ref):
        @pl.when(pl.program_id(0) == 0)
        def _():
            dg_ref[...] = jnp.zeros_like(dg_ref)

        if w.ndim == 3:
            per = n // N_SHARDS
            dh = _dot_nt(dp_ref[:, 0:per], w_ref[0])
            for s in range(1, N_SHARDS):
                dh = dh + _dot_nt(dp_ref[:, s * per:(s + 1) * per], w_ref[s])
        else:
            dh = _dot_nt(dp_ref[...], w_ref[...])
        r = r_ref[...]
        xn = x_ref[...] * r
        dg_ref[...] += jnp.sum(dh * xn, axis=0, keepdims=True)
        dxn = dh * g_ref[...]
        dx = r * (dxn - xn * jnp.mean(dxn * xn, axis=-1, keepdims=True))
        dx_ref[...] = dxo_ref[...] + dx

    row = lambda i: (i, 0)
    fixed = lambda i: (0, 0)
    return _pcall(
        body, name=name, out_shape=[_sds((t, d), F32), _sds((1, d), F32)], grid=(t // tm,),
        in_specs=[pl.BlockSpec((tm, n), row), w_spec, pl.BlockSpec((tm, d), row),
                  pl.BlockSpec((1, d), fixed), pl.BlockSpec((tm, 1), row), pl.BlockSpec((tm, d), row)],
        out_specs=[pl.BlockSpec((tm, d), row), pl.BlockSpec((1, d), fixed)],
        semantics=("arbitrary",))(dpre, w, x, g, rstd, dx_out)


def ple_backward(dx, gate, pp, *, name, tm=512):
    t, d = dx.shape
    tm = min(tm, t)

    def body(dx_ref, gate_ref, pp_ref, dg_ref, dp_ref):
        dxv = dx_ref[...]
        gate = gate_ref[...]
        dg_ref[...] = (dxv * pp_ref[...].astype(F32) * (gate * (1.0 - gate))).astype(BF16)
        dp_ref[...] = (dxv * gate).astype(BF16)

    spec = pl.BlockSpec((tm, d), lambda i: (i, 0))
    return _pcall(body, name=name, out_shape=[_sds((t, d), BF16)] * 2, grid=(t // tm,), in_specs=[spec] * 3,
                  out_specs=[spec] * 2, semantics=("parallel",))(dx, gate, pp)


def sgu_backward(dy, pre, g_v, w_s, b_full, *, name):
    t = pre.shape[0]
    n_chunks = t // CHUNK

    def body(dy_ref, pre_ref, gv_ref, ws_ref, b_ref, dpre_ref, dws_ref, db_ref, dgv_ref, dvn_s, dbf_s):
        step = pl.program_id(0)

        @pl.when(step == 0)
        def _():
            dws_ref[...] = jnp.zeros_like(dws_ref)
            dgv_ref[...] = jnp.zeros_like(dgv_ref)
            dbf_s[...] = jnp.zeros_like(dbf_s)

        pre_u, pre_v, u, r, vhat, vn, wm, tril = _sgu_common(pre_ref, gv_ref, ws_ref)
        dyv = dy_ref[...]
        for g in range(N_GROUPS):
            cols = slice(g * LANES, (g + 1) * LANES)
            mix = _dot(wm[g], vn[:, cols]) + b_ref[:, cols]
            dmix = dyv[:, cols] * u[:, cols]
            dmix_b = dmix.astype(BF16)
            du = dyv[:, cols] * mix
            dpre_ref[:, cols] = (du * _gelu_grad(pre_u[:, cols])).astype(BF16)
            dws_ref[g] += jnp.where(tril, _dot_nt(dmix_b, vn[:, cols]), 0.0)
            dbf_s[:, cols] += dmix
            dvn_s[:, cols] = _dot_tn(wm[g], dmix_b)
        dvn = dvn_s[...]
        dgv_ref[...] += jnp.sum(dvn * vhat, axis=0, keepdims=True)
        dxn = dvn * gv_ref[...]
        dv = r * (dxn - vhat * jnp.mean(dxn * vhat, axis=-1, keepdims=True))
        dpre_ref[:, D_MODEL:] = (dv * _gelu_grad(pre_v)).astype(BF16)

        @pl.when(step == n_chunks - 1)
        def _():
            lane = lax.broadcasted_iota(jnp.int32, (CHUNK, LANES), 1)
            acc = jnp.zeros((CHUNK, LANES), F32)
            for g in range(N_GROUPS):
                s = jnp.sum(dbf_s[:, g * LANES:(g + 1) * LANES], axis=-1, keepdims=True)
                acc = jnp.where(lane == g, s, acc)
            db_ref[...] = acc

    fixed2 = lambda i: (0, 0)
    return _pcall(
        body, name=name,
        out_shape=[_sds((t, 2 * D_MODEL), BF16), _sds((N_GROUPS, CHUNK, CHUNK), F32), _sds((CHUNK, LANES), F32),
                   _sds((1, D_MODEL), F32)],
        grid=(n_chunks,),
        in_specs=[pl.BlockSpec((CHUNK, D_MODEL), lambda i: (i, 0)), pl.BlockSpec((CHUNK, 2 * D_MODEL), lambda i: (i, 0)),
                  pl.BlockSpec((1, D_MODEL), fixed2), pl.BlockSpec((N_GROUPS, CHUNK, CHUNK), lambda i: (0, 0, 0)),
                  pl.BlockSpec((CHUNK, D_MODEL), fixed2)],
        out_specs=[pl.BlockSpec((CHUNK, 2 * D_MODEL), lambda i: (i, 0)),
                   pl.BlockSpec((N_GROUPS, CHUNK, CHUNK), lambda i: (0, 0, 0)), pl.BlockSpec((CHUNK, LANES), fixed2),
                   pl.BlockSpec((1, D_MODEL), fixed2)],
        scratch_shapes=[pltpu.VMEM((CHUNK, D_MODEL), F32), pltpu.VMEM((CHUNK, D_MODEL), F32)],
        semantics=("arbitrary",))(dy, pre, g_v, w_s, b_full)


def head_norm_backward(dy, pre, g128, *, name, col_block=0, scale=1.0, passthrough=None, tm=512):
    t = dy.shape[0]
    tm = min(tm, t)
    width = 2 * D_MODEL if passthrough is not None else D_MODEL

    def body(*refs):
        if passthrough is not None:
            dy_ref, x_ref, g_ref, dv_ref, o_ref, dg_ref = refs
            o_ref[:, D_MODEL:] = dv_ref[...].astype(BF16)
        else:
            dy_ref, x_ref, g_ref, o_ref, dg_ref = refs

        @pl.when(pl.program_id(0) == 0)
        def _():
            dg_ref[...] = jnp.zeros_like(dg_ref)

        g = g_ref[...]
        dg = jnp.zeros((1, LANES), F32)
        for b in range(D_MODEL // LANES):
            cols = slice(b * LANES, (b + 1) * LANES)
            xv = x_ref[:, cols]
            r = _head_rstd(xv)
            xn = xv * r
            dyv = dy_ref[:, cols] * scale
            dg = dg + jnp.sum(dyv * xn, axis=0, keepdims=True)
            dxn = dyv * g
            o_ref[:, cols] = (r * (dxn - xn * _head_mean(dxn * xn))).astype(BF16)
        dg_ref[...] += dg

    row = lambda i: (i, 0)
    in_specs = [pl.BlockSpec((tm, D_MODEL), row), pl.BlockSpec((tm, D_MODEL), lambda i: (i, col_block)),
                pl.BlockSpec((1, LANES), lambda i: (0, 0))]
    args = [dy, pre, g128]
    if passthrough is not None:
        in_specs.append(pl.BlockSpec((tm, D_MODEL), row))
        args.append(passthrough)
    return _pcall(body, name=name, out_shape=[_sds((t, width), BF16), _sds((1, LANES), F32)], grid=(t // tm,),
                  in_specs=in_specs,
                  out_specs=[pl.BlockSpec((tm, width), row), pl.BlockSpec((1, LANES), lambda i: (0, 0))],
                  semantics=("arbitrary",))(*args)


def stick_breaking_backward(q, k, v, do, *, name):
    t = q.shape[0]
    bq, bk, ratio = _att_blocks(t)

    def body(q_ref, k_ref, v_ref, do_ref, dq_ref, dk_ref, dv_ref, s_buf, sg_buf):
        i = pl.program_id(1)

        @pl.when(i == 0)
        def _():
            dk_ref[...] = jnp.zeros_like(dk_ref)
            dv_ref[...] = jnp.zeros_like(dv_ref)

        low = lax.broadcasted_iota(jnp.int32, (bq, LANES), 1) < HEAD_DIM
        suffix = _suffix_matrix(bk)
        prefix = _prefix_matrix(bk)
        qs = _stack_heads(q_ref[...], low)
        dos = _stack_heads(do_ref[...], low)
        first = ratio * i

        def log_weights(j, carry, causal=None):
            rows = pl.ds(pl.multiple_of(j * bk, bk), bk)
            z = _dot_nt(qs, k_ref[rows, :])
            ls = _log_sigmoid(z)
            lg = ls - z
            if causal is not None:
                lg = jnp.where(causal, lg, 0.0)
            s_buf[j] = ls + _block_cumsum(lg, suffix) + carry
            sg_buf[j] = jnp.exp(ls)
            return carry + jnp.sum(lg, axis=-1, keepdims=True)

        carry = jnp.zeros((2 * bq, 1), F32)
        for m in reversed(range(ratio)):
            carry = log_weights(first + m, carry, _stacked_causal(bq, bk, m * bk))
        carry = lax.fori_loop(0, first // 2,
                              lambda n, c: log_weights(first - 2 - 2 * n, log_weights(first - 1 - 2 * n, c)), carry)
        lax.fori_loop(0, first % 2, lambda n, c: log_weights(0, c), carry)

        def grads(j, pcarry, dq_acc, causal=None):
            rows = pl.ds(pl.multiple_of(j * bk, bk), bk)
            a = jnp.exp(s_buf[j])
            if causal is not None:
                a = jnp.where(causal, a, 0.0)
            sg = sg_buf[j]
            ds = _dot_nt(dos, v_ref[rows, :]) * a
            before = _block_cumsum(ds, prefix) + pcarry
            if causal is not None:
                before = jnp.where(causal, before, 0.0)
            dz = (ds - sg * (ds + before)).astype(BF16)
            dq_acc = dq_acc + _dot(dz, k_ref[rows, :])
            dk_ref[rows, :] += _dot_tn(dz, qs)
            dv_ref[rows, :] += _dot_tn(a.astype(BF16), dos)
            return pcarry + jnp.sum(ds, axis=-1, keepdims=True), dq_acc

        def two_blocks(n, st):
            st = grads(2 * n, st[0], st[1])
            return grads(2 * n + 1, st[0], st[1])

        state = lax.fori_loop(0, first // 2, two_blocks,
                              (jnp.zeros((2 * bq, 1), F32), jnp.zeros((2 * bq, LANES), F32)))
        state = lax.fori_loop(0, first % 2, lambda n, st: grads(first - 1, st[0], st[1]), state)
        for m in range(ratio):
            state = grads(first + m, state[0], state[1], _stacked_causal(bq, bk, m * bk))
        dq_ref[...] = jnp.where(low, state[1][:bq], state[1][bq:])

    full = pl.BlockSpec((t, LANES), lambda p, i: (0, p))
    qblk = pl.BlockSpec((bq, LANES), lambda p, i: (i, p))
    return _pcall(
        body, name=name, out_shape=[_sds((t, D_MODEL), F32)] * 3, grid=(D_MODEL // LANES, t // bq),
        in_specs=[qblk, full, full, qblk], out_specs=[qblk, full, full],
        scratch_shapes=[pltpu.VMEM((t // bk, 2 * bq, bk), F32), pltpu.VMEM((t // bk, 2 * bq, bk), F32)],
        semantics=("parallel", "arbitrary"))(q, k, v, do)


def _mlp_backward(dx, saved, g, w_up, w_down, tag):
    x, h, r, a, a2 = saved
    d_w_down = matmul_tn(a2, dx, name=f"d_w_down_{tag}", col_shards=False)
    dpre = matmul_nt(dx, w_down, name=f"d_mlp_act_{tag}", mul=a, out_dtype=BF16)
    d_w_up = matmul_tn(h, dpre, name=f"d_w_up_{tag}", col_shards=True)
    dx, d_g = norm_backward(dpre, w_up, x, g, r, dx, name=f"d_mlp_norm_{tag}")
    return dx, d_w_up, d_w_down, d_g


def _ple_backward(dx, saved, p, g, w_gate, tag):
    x, h, r, gate, pp = saved
    dgate, dproj = ple_backward(dx, gate, pp, name=f"d_ple_{tag}")
    d_w_proj = matmul_tn(p, dproj, name=f"d_w_ple_proj_{tag}", col_shards=True)
    d_w_gate = matmul_tn(h, dgate, name=f"d_w_ple_gate_{tag}", col_shards=False)
    dx, d_g = norm_backward(dgate, w_gate, x, g, r, dx, name=f"d_ple_norm_{tag}")
    return dx, d_w_gate, d_w_proj, d_g


def local_step(x, p, target, w, late=None):
    row = lambda v: v.reshape(1, -1)
    g128 = lambda v: jnp.tile(v.reshape(1, HEAD_DIM), (1, 2))
    scale = HEAD_DIM ** -0.5
    b_full = jnp.repeat(jnp.transpose(w["b_spatial"][0]), LANES, axis=1)
    w_s = w["w_spatial"][0]

    mats = {}
    for name, value in w.items():
        if isinstance(value, tuple):
            mats.update({(name, layer): v for layer, v in enumerate(value)})
    if "w_kv" in w:
        mats[("w_kv", 0)] = w["w_kv"]

    def fetch(name, layer, after):
        if (name, layer) not in mats:
            mats.update(late.weights(name, layer, after))
        return mats[(name, layer)]

    def mlp_forward(x_in, layer):
        h, r, a, a2 = norm_matmul(x_in, row(w["ln_mlp"][layer]), fetch("w_up", layer, x_in), name=f"mlp_up_{layer}",
                                  epilogue="relu2")
        return matmul_residual(a2, fetch("w_down", layer, a2), x_in, name=f"mlp_down_{layer}"), (x_in, h, r, a, a2)

    def ple(x_in, layer):
        return ple_forward(x_in, row(w["ln_ple"][layer]), fetch("w_ple_gate", layer, x_in), p[layer],
                           fetch("w_ple_proj", layer, x_in), name=f"ple_{layer}")

    x0 = x
    h_a, r_a, pre_a = norm_matmul(x0, row(w["ln_mix_a"][0]), fetch("w_in_a", 0, x0), name="sgu_in")
    y_a = sgu_forward(pre_a, row(w["g_v_a"][0]), w_s, b_full, name="sgu_mix")
    x1 = matmul_residual(y_a, fetch("w_out_a", 0, y_a), x0, name="sgu_out")
    x2, mlp0 = mlp_forward(x1, 0)
    ple0 = ple(x2, 0)
    x3 = ple0[4]
    h_kv, r_kv, kv_pre = norm_matmul(x3, row(w["ln_kv"]), fetch("w_kv", 0, x3), name="kv_proj")
    k_n, v_b = head_norm(kv_pre, g128(w["g_k"]), name="k_norm", passthrough=True)
    h_q, r_q, q_pre = norm_matmul(x3, row(w["ln_mix_b"][0]), fetch("w_q", 0, k_n), name="q_proj")
    q_n = head_norm(q_pre, g128(w["g_q"][0]), name="q_norm", scale=scale)
    o = stick_breaking_forward(q_n, k_n, v_b, name="sb_fwd")
    if late is not None:
        late.pass_on("w_up", 1, o)
    x4 = matmul_residual(o, fetch("w_out_b", 0, o), x3, name="sb_out")
    x5, mlp1 = mlp_forward(x4, 1)
    ple1 = ple(x5, 1)
    x6 = ple1[4]
    loss_blk, dx = loss_forward(x6, target, name="loss")

    g = {}
    dx, dwg1, dwp1, dlnp1 = _ple_backward(dx, (x5,) + tuple(ple1[:4]), p[1], row(w["ln_ple"][1]),
                                          mats[("w_ple_gate", 1)], 1)
    dx, dwu1, dwd1, dlnm1 = _mlp_backward(dx, mlp1, row(w["ln_mlp"][1]), mats[("w_up", 1)], mats[("w_down", 1)], 1)
    g["w_out_b"] = matmul_tn(o, dx, name="d_w_out_b", col_shards=False)
    do = matmul_nt(dx, mats[("w_out_b", 0)], name="d_sb_out", out_dtype=BF16)
    dq_n, dk_n, dv = stick_breaking_backward(q_n, k_n, v_b, do, name="sb_bwd")
    dq_pre, dgq = head_norm_backward(dq_n, q_pre, g128(w["g_q"][0]), name="d_q_norm", scale=scale)
    dkv_pre, dgk = head_norm_backward(dk_n, kv_pre, g128(w["g_k"]), name="d_k_norm", passthrough=dv)
    g["w_q"] = matmul_tn(h_q, dq_pre, name="d_w_q", col_shards=False)
    g["w_kv"] = matmul_tn(h_kv, dkv_pre, name="d_w_kv", col_shards=True)
    dx, g["ln_mix_b"] = norm_backward(dq_pre, mats[("w_q", 0)], x3, row(w["ln_mix_b"][0]), r_q, dx, name="d_q_in")
    dx, g["ln_kv"] = norm_backward(dkv_pre, mats[("w_kv", 0)], x3, row(w["ln_kv"]), r_kv, dx, name="d_kv_in")
    g["g_q"] = dgq[:, :HEAD_DIM] + dgq[:, HEAD_DIM:]
    g["g_k"] = (dgk[:, :HEAD_DIM] + dgk[:, HEAD_DIM:]).reshape(HEAD_DIM)
    g["ln_kv"] = g["ln_kv"].reshape(D_MODEL)
    if late is not None:
        late.pair_start({("w_kv", 0): g["w_kv"], ("w_q", 0): g["w_q"], ("w_out_b", 0): g["w_out_b"],
                         ("w_up", 1): dwu1, ("w_down", 1): dwd1, ("w_ple_gate", 1): dwg1, ("w_ple_proj", 1): dwp1}, dx)
    dx, dwg0, dwp0, dlnp0 = _ple_backward(dx, (x2,) + tuple(ple0[:4]), p[0], row(w["ln_ple"][0]),
                                          mats[("w_ple_gate", 0)], 0)
    if late is not None:
        late.chip_start(dx)
    dx, dwu0, dwd0, dlnm0 = _mlp_backward(dx, mlp0, row(w["ln_mlp"][0]), mats[("w_up", 0)], mats[("w_down", 0)], 0)
    if late is not None:
        late.pair_start({("w_up", 0): dwu0, ("w_down", 0): dwd0, ("w_ple_gate", 0): dwg0, ("w_ple_proj", 0): dwp0}, dx)
    g["w_out_a"] = matmul_tn(y_a, dx, name="d_w_out_a", col_shards=False)
    dy_a = matmul_nt(dx, mats[("w_out_a", 0)], name="d_sgu_out")
    dpre_a, dws, db, g["g_v_a"] = sgu_backward(dy_a, pre_a, row(w["g_v_a"][0]), w_s, b_full, name="d_sgu_mix")
    if late is not None:
        late.chip_start(dpre_a)
    g["w_in_a"] = matmul_tn(h_a, dpre_a, name="d_w_in_a", col_shards=True)
    dx, g["ln_mix_a"] = norm_backward(dpre_a, mats[("w_in_a", 0)], x0, row(w["ln_mix_a"][0]), r_a, dx, name="d_sgu_in")
    g["w_spatial"] = dws[None]
    g["b_spatial"] = jnp.transpose(db[:, :N_GROUPS])[None]
    g["w_up"] = (dwu0, dwu1)
    g["w_down"] = (dwd0, dwd1)
    g["w_ple_gate"] = (dwg0, dwg1)
    g["w_ple_proj"] = (dwp0, dwp1)
    g["ln_mlp"] = jnp.concatenate([dlnm0, dlnm1], axis=0)
    g["ln_ple"] = jnp.concatenate([dlnp0, dlnp1], axis=0)
    return loss_blk, dx, g


ANY = pl.BlockSpec(memory_space=pl.ANY)


def _place():
    x, y, c = lax.axis_index("x"), lax.axis_index("y"), lax.axis_index("c")
    others = [(1 - x, y), (x, 1 - y), (1 - x, 1 - y)]
    return x, y, c, 2 * x + y, others


def cast_into_slot(w3, layer, slot, *, name, after=None, tm=512):
    _, r, c = w3.shape
    tm = min(tm, r)

    def body(slot_ref, w_ref, *rest):
        rest[-1][...] = w_ref[...].astype(BF16)

    in_specs = [pl.BlockSpec((None, tm, c), lambda i, s: (layer, i, 0))]
    args = [slot, w3]
    if after is not None:
        in_specs.append(ANY)
        args.append(after)
    return _pcall(body, name=name, out_shape=_sds((N_SHARDS, r, c), BF16), grid=(r // tm,), num_prefetch=1,
                  in_specs=in_specs, out_specs=pl.BlockSpec((None, tm, c), lambda i, s: (s[0], i, 0)),
                  semantics=("parallel",))(*args)


def gather_vectors(vecs, *, name):
    n = len(vecs)

    def body(*refs):
        src, out = refs[:n], refs[n:2 * n]
        send, recv, loc = refs[2 * n:]
        x, y, c, s_me, others = _place()

        def copy(l, k, slot):
            ox, oy = others[k]
            return pltpu.make_async_remote_copy(src[l], out[l].at[slot], send.at[l, k], recv.at[l, k],
                                                device_id=(ox, oy, c), device_id_type=MESH)

        for l in range(n):
            for k in range(3):
                copy(l, k, s_me).start()
        for l in range(n):
            own = pltpu.make_async_copy(src[l], out[l].at[s_me], loc)
            own.start()
            own.wait()
        for l in range(n):
            for k in range(3):
                ox, oy = others[k]
                copy(l, k, 2 * ox + oy).wait_recv()
                copy(l, k, s_me).wait_send()

    return _pcall(body, name=name, out_shape=[_sds((N_SHARDS,) + v.shape, F32) for v in vecs], in_specs=[ANY] * n,
                  out_specs=[ANY] * n,
                  scratch_shapes=[pltpu.SemaphoreType.DMA((n, 3)), pltpu.SemaphoreType.DMA((n, 3)),
                                  pltpu.SemaphoreType.DMA(())],
                  side_effects=True)(*vecs)


HBM = pl.BlockSpec(memory_space=pltpu.HBM)
SEM = pl.BlockSpec(memory_space=pltpu.SEMAPHORE)
DATAFLOW = pltpu.SideEffectType.DATAFLOW_SIDE_EFFECTING


def _split_call(body, *, name, out_shape, in_specs, out_specs, aliases, views_of=()):
    def make(wrap, specs):
        body_ = wrap(body)
        return pl.pallas_call(body_, name=name, out_shape=out_shape, in_specs=specs, out_specs=out_specs,
                              input_output_aliases=aliases,
                              compiler_params=pltpu.CompilerParams(has_side_effects=DATAFLOW))

    return lambda *args: _in_order(make, in_specs, args, views_of)


def _token_shape():
    return jax.ShapeDtypeStruct((8, LANES), F32)


def gather_start(mats, after, *, name):
    n = len(mats)
    halves = [pltpu.with_memory_space_constraint(m.reshape(N_SHARDS, 2, m.shape[1] // 2, m.shape[2]), pltpu.HBM)
              for m in mats]

    def body(*refs):
        send, recv = refs[n + 1], refs[n + 2]
        out, token = refs[n + 3:2 * n + 3], refs[2 * n + 3]
        x, y, c, s_me, others = _place()
        for l in range(n):
            for k in range(3):
                ox, oy = others[k]
                pltpu.make_async_remote_copy(out[l].at[s_me, c], out[l].at[s_me, c], send.at[3 * l + k],
                                             recv.at[3 * l + k], device_id=(ox, oy, c), device_id_type=MESH).start()
        token[...] = jnp.zeros_like(token)

    res = _split_call(
        body, name=name,
        out_shape=(pltpu.SemaphoreType.DMA((3 * n,)), pltpu.SemaphoreType.DMA((3 * n,)),
                   *[pltpu.HBM(h.shape, BF16) for h in halves], _token_shape()),
        in_specs=[HBM] * n + [ANY], out_specs=(SEM, SEM, *[HBM] * n, pl.BlockSpec(memory_space=pltpu.VMEM)),
        aliases={l: 2 + l for l in range(n)}, views_of=mats)(*halves, after)
    return res[0], res[1], list(res[2:2 + n]), res[2 + n]


def gather_pass_on(bufs, send_a, recv_a, after, *, name, base=0):
    n = len(bufs)

    def body(*refs):
        send_a, recv_a = refs[n], refs[n + 1]
        out = refs[n + 3:2 * n + 3]
        send_b, recv_b, token = refs[2 * n + 3:]
        x, y, c, s_me, others = _place()
        for l in range(n):
            for k in range(3):
                ox, oy = others[k]
                landed, i = out[l].at[2 * ox + oy, c], 3 * l + k
                pltpu.make_async_remote_copy(landed, landed, send_a.at[3 * base + i], recv_a.at[3 * base + i],
                                             device_id=(x, y, 1 - c), device_id_type=MESH).wait_recv()
                pltpu.make_async_remote_copy(landed, landed, send_b.at[i], recv_b.at[i],
                                             device_id=(x, y, 1 - c), device_id_type=MESH).start()
        for l in range(n):
            for k in range(3):
                mine, i = out[l].at[s_me, c], 3 * (base + l) + k
                pltpu.make_async_remote_copy(mine, mine, send_a.at[i], recv_a.at[i],
                                             device_id=(x, y, 1 - c), device_id_type=MESH).wait_send()
        token[...] = jnp.zeros_like(token)

    res = _split_call(
        body, name=name,
        out_shape=(*[pltpu.HBM(b.shape, BF16) for b in bufs], pltpu.SemaphoreType.DMA((3 * n,)),
                   pltpu.SemaphoreType.DMA((3 * n,)), _token_shape()),
        in_specs=[HBM] * n + [SEM, SEM, ANY],
        out_specs=(*[HBM] * n, SEM, SEM, pl.BlockSpec(memory_space=pltpu.VMEM)),
        aliases={l: l for l in range(n)})(*bufs, send_a, recv_a, after)
    return list(res[:n]), res[n], res[n + 1], res[n + 2]


def gather_finish(bufs, send_b, recv_b, after, shapes, *, name):
    n = len(bufs)

    def body(*refs):
        send_b, recv_b = refs[n], refs[n + 1]
        out = refs[n + 3:]
        x, y, c, _, others = _place()
        for l in range(n):
            for k in range(3):
                ox, oy = others[k]
                theirs, mine, i = out[l].at[2 * ox + oy, 1 - c], out[l].at[2 * ox + oy, c], 3 * l + k
                pltpu.make_async_remote_copy(theirs, theirs, send_b.at[i], recv_b.at[i],
                                             device_id=(x, y, 1 - c), device_id_type=MESH).wait_recv()
                pltpu.make_async_remote_copy(mine, mine, send_b.at[i], recv_b.at[i],
                                             device_id=(x, y, 1 - c), device_id_type=MESH).wait_send()

    res = _split_call(
        body, name=name, out_shape=tuple(pltpu.HBM(b.shape, BF16) for b in bufs),
        in_specs=[HBM] * n + [SEM, SEM, ANY], out_specs=tuple([HBM] * n),
        aliases={l: l for l in range(n)})(*bufs, send_b, recv_b, after)
    return [r.reshape(s) for r, s in zip(res, shapes)]


def exchange_start(srcs, dst_shapes, dst_dtype, plan, count, after, *, name):
    n, m = len(srcs), len(dst_shapes)
    given = list(srcs)
    srcs = [pltpu.with_memory_space_constraint(s, pltpu.HBM) for s in srcs]
    lands = [pltpu.with_memory_space_constraint(lax.empty(s, dst_dtype), pltpu.HBM) for s in dst_shapes]

    def body(*refs):
        send, recv = refs[n + m + 1], refs[n + m + 2]
        src, dst, token = refs[n + m + 3:2 * n + m + 3], refs[2 * n + m + 3:2 * (n + m) + 3], refs[2 * (n + m) + 3]
        for i, (s, d, dev) in enumerate(plan(_place(), src, dst)):
            pltpu.make_async_remote_copy(s, d, send.at[i], recv.at[i], device_id=dev, device_id_type=MESH).start()
        token[...] = jnp.zeros_like(token)

    res = _split_call(
        body, name=name,
        out_shape=(pltpu.SemaphoreType.DMA((count,)), pltpu.SemaphoreType.DMA((count,)),
                   *[pltpu.HBM(s.shape, s.dtype) for s in srcs], *[pltpu.HBM(s, dst_dtype) for s in dst_shapes],
                   _token_shape()),
        in_specs=[HBM] * (n + m) + [ANY],
        out_specs=(SEM, SEM, *[HBM] * (n + m), pl.BlockSpec(memory_space=pltpu.VMEM)),
        aliases={i: 2 + i for i in range(n + m)}, views_of=given)(*srcs, *lands, after)
    return (list(res[2:2 + n]), list(res[2 + n:2 + n + m]), res[0], res[1], plan), res[2 + n + m]


def exchange_finish(state, after, *, name):
    srcs, lands, send, recv, plan = state
    n, m = len(srcs), len(lands)

    def body(*refs):
        send, recv = refs[n + m], refs[n + m + 1]
        src, dst = refs[n + m + 3:2 * n + m + 3], refs[2 * n + m + 3:]
        for i, (s, d, dev) in enumerate(plan(_place(), src, dst)):
            pltpu.make_async_remote_copy(s, d, send.at[i], recv.at[i], device_id=dev, device_id_type=MESH).wait()

    res = _split_call(
        body, name=name,
        out_shape=tuple(pltpu.HBM(a.shape, a.dtype) for a in srcs + lands),
        in_specs=[HBM] * (n + m) + [SEM, SEM, ANY], out_specs=tuple([HBM] * (n + m)),
        aliases={i: i for i in range(n + m)})(*srcs, *lands, send, recv, after)
    return list(res[:n]), list(res[n:])


def pair_plan(place, src, dst):
    x, y, c, _, _ = place
    return [(s.at[:, 1 - c], d, (x, y, 1 - c)) for s, d in zip(src, dst)]


def chip_plan(place, src, dst):
    x, y, c, _, others = place
    return [(s.at[2 * ox + oy], d.at[k], (ox, oy, c)) for s, d in zip(src, dst) for k, (ox, oy) in enumerate(others)]


def pair_exchange(grads, *, name):
    n = len(grads)

    def body(*refs):
        src, got = refs[:n], refs[n:2 * n]
        send, recv = refs[2 * n:]
        x, y, c, _, _ = _place()

        def swap(l):
            return pltpu.make_async_remote_copy(src[l].at[:, 1 - c], got[l], send.at[l], recv.at[l],
                                                device_id=(x, y, 1 - c), device_id_type=MESH)

        for l in range(n):
            swap(l).start()
        for l in range(n):
            swap(l).wait()

    res = _pcall(body, name=name, out_shape=[_sds((N_SHARDS,) + g.shape[2:], F32) for g in grads],
                 in_specs=[ANY] * n, out_specs=[ANY] * n,
                 scratch_shapes=[pltpu.SemaphoreType.DMA((n,)), pltpu.SemaphoreType.DMA((n,))],
                 side_effects=True)(*grads)
    return list(res)


def add_to_wire(mine, theirs, core, *, name, tm=512):
    s, _, r, c = mine.shape
    tm = min(tm, r)

    def body(core_ref, a_ref, b_ref, o_ref):
        o_ref[...] = (a_ref[...] + b_ref[...]).astype(BF16)

    spec = pl.BlockSpec((None, tm, c), lambda i, j, cr: (i, j, 0))
    return _pcall(body, name=name, out_shape=_sds((s, r, c), BF16), grid=(s, r // tm), num_prefetch=1,
                  in_specs=[pl.BlockSpec((None, None, tm, c), lambda i, j, cr: (i, cr[0], j, 0)), spec],
                  out_specs=spec, semantics=("parallel", "parallel"))(core, mine, theirs)


def sum_chips(wire, landed, place, dest, layer, n_layers, *, name, tm=512):
    _, r, c = wire.shape
    tm = min(tm, r)

    def body(place_ref, w_ref, l_ref, *rest):
        o_ref = rest[-1]
        o_ref[...] = ((w_ref[...].astype(F32) + l_ref[0].astype(F32)) + l_ref[1].astype(F32)) + l_ref[2].astype(F32)

    in_specs = [pl.BlockSpec((None, tm, c), lambda i, pr: (pr[0], i, 0)),
                pl.BlockSpec((3, tm, c), lambda i, pr: (0, i, 0))]
    args = [place, wire, landed]
    aliases = None
    if dest is not None:
        in_specs.append(ANY)
        args.append(dest)
        aliases = {3: 0}
    return _pcall(body, name=name, out_shape=_sds((n_layers, 2, r, c), F32), grid=(r // tm,), num_prefetch=1,
                  in_specs=in_specs,
                  out_specs=pl.BlockSpec((None, None, tm, c), lambda i, pr: (layer, pr[1], i, 0)),
                  aliases=aliases, semantics=("parallel",))(*args)


def pair_share(bufs, slots, *, name):
    n = len(bufs)

    def body(*refs):
        out = refs[n:2 * n]
        send, recv = refs[2 * n:]
        x, y, c, _, _ = _place()

        def share(i, half):
            o, l = slots[i]
            return pltpu.make_async_remote_copy(out[o].at[l, half], out[o].at[l, half], send.at[i], recv.at[i],
                                                device_id=(x, y, 1 - c), device_id_type=MESH)

        for i in range(len(slots)):
            share(i, c).start()
        for i in range(len(slots)):
            share(i, 1 - c).wait_recv()
            share(i, c).wait_send()

    res = _pcall(body, name=name, out_shape=[_sds(b.shape, F32) for b in bufs], in_specs=[ANY] * n,
                 out_specs=[ANY] * n,
                 scratch_shapes=[pltpu.SemaphoreType.DMA((len(slots),)), pltpu.SemaphoreType.DMA((len(slots),))],
                 aliases={o: o for o in range(n)}, side_effects=True)(*bufs)
    return list(res)


def all_reduce_small(packed, *, name):
    n_dev, r, c = packed.shape

    def body(in_ref, out_ref, land, send, recv):
        x, y, cc, _, _ = _place()
        me = 4 * x + 2 * y + cc
        peers = [(px, py, pc) for px in range(2) for py in range(2) for pc in range(2)]

        def scatter(d):
            return pltpu.make_async_remote_copy(in_ref.at[d], land.at[me], send.at[0, d], recv.at[0, me],
                                                device_id=peers[d], device_id_type=MESH)

        def gather(d):
            return pltpu.make_async_remote_copy(out_ref.at[me], out_ref.at[me], send.at[1, d], recv.at[1, me],
                                                device_id=peers[d], device_id_type=MESH)

        for d in range(n_dev):
            @pl.when(d != me)
            def _():
                scatter(d).start()
        land[me] = in_ref[me]
        for d in range(n_dev):
            @pl.when(d != me)
            def _():
                pltpu.make_async_remote_copy(in_ref.at[d], land.at[d], send.at[0, d], recv.at[0, d],
                                             device_id=peers[d], device_id_type=MESH).wait_recv()
        total = land[0]
        for d in range(1, n_dev):
            total = total + land[d]
        out_ref[me] = total
        for d in range(n_dev):
            @pl.when(d != me)
            def _():
                gather(d).start()
        for d in range(n_dev):
            @pl.when(d != me)
            def _():
                pltpu.make_async_remote_copy(out_ref.at[d], out_ref.at[d], send.at[1, d], recv.at[1, d],
                                             device_id=peers[d], device_id_type=MESH).wait_recv()
        for d in range(n_dev):
            @pl.when(d != me)
            def _():
                scatter(d).wait_send()
                gather(d).wait_send()

    vm = pl.BlockSpec(memory_space=pltpu.VMEM)
    return _pcall(body, name=name, out_shape=_sds(packed.shape, F32), in_specs=[vm], out_specs=vm,
                  scratch_shapes=[pltpu.VMEM(packed.shape, F32), pltpu.SemaphoreType.DMA((2, n_dev)),
                                  pltpu.SemaphoreType.DMA((2, n_dev))],
                  side_effects=True)(packed)


def adamw(w, g, m, v, *, name, part=None, dest=None, tm=512):
    shape = w.shape
    cols = shape[-1]
    rows = 1
    for s in shape[:-1]:
        rows *= s
    first, count = 0, rows
    if part is not None:
        count = rows // part[1]
        first = part[0] * count
    tm = min(tm, count)
    assert count % tm == 0
    two_d = lambda a: a.reshape(rows, cols)

    def body(w_ref, g_ref, m_ref, v_ref, *rest):
        d_ref, mo_ref, vo_ref = rest[-3:]
        gv = g_ref[...]
        m_new = ADAM_B1 * m_ref[...] + (1.0 - ADAM_B1) * gv
        v_new = ADAM_B2 * v_ref[...] + (1.0 - ADAM_B2) * (gv * gv)
        m_hat = m_new / (1.0 - ADAM_B1 ** ADAM_STEP)
        v_hat = v_new / (1.0 - ADAM_B2 ** ADAM_STEP)
        d_ref[...] = -ADAM_LR * (m_hat / (jnp.sqrt(v_hat) + ADAM_EPS) + ADAM_WD * w_ref[...])
        mo_ref[...] = m_new
        vo_ref[...] = v_new

    spec = pl.BlockSpec((tm, cols), lambda i: (first // tm + i, 0))
    args = [two_d(w), two_d(g), two_d(m), two_d(v)]
    in_specs = [spec] * 4
    aliases = None
    if dest is not None:
        args += [two_d(d) for d in dest]
        in_specs = in_specs + [ANY] * 3
        aliases = {4: 0, 5: 1, 6: 2}
    outs = _pcall(body, name=name, out_shape=[_sds((rows, cols), F32)] * 3, grid=(count // tm,), in_specs=in_specs,
                  out_specs=[spec] * 3, aliases=aliases, semantics=("parallel",))(*args)
    return [o.reshape(shape) for o in outs]


WEIGHTS = ("ln_mix_a", "w_in_a", "g_v_a", "w_spatial", "b_spatial", "w_out_a", "ln_kv", "w_kv", "g_k", "ln_mix_b",
           "w_q", "g_q", "w_out_b", "ln_mlp", "w_up", "w_down", "ln_ple", "w_ple_gate", "w_ple_proj")
MATRICES = (("w_in_a", 1, True), ("w_out_a", 1, False), ("w_kv", 0, True), ("w_q", 1, False), ("w_out_b", 1, False),
            ("w_up", 2, True), ("w_down", 2, False), ("w_ple_gate", 2, False), ("w_ple_proj", 2, True))
GATHER_STAGES = ((("w_in_a", 0),), (("w_out_a", 0),), (("w_up", 0),), (("w_down", 0),),
                 (("w_ple_gate", 0), ("w_ple_proj", 0), ("w_kv", 0)), (("w_q", 0), ("w_out_b", 0)),
                 (("w_up", 1), ("w_down", 1), ("w_ple_gate", 1), ("w_ple_proj", 1)))
REPLICATED = ("w_spatial", "b_spatial", "ln_kv", "g_k", "ln_mix_b", "g_q", "ln_mlp", "ln_ple")
SHARDED_VECTORS = ("ln_mix_a", "g_v_a")
SMALL_ROWS = 18


def kernel(x, p, ln_mix_a, w_in_a, g_v_a, w_spatial, b_spatial, w_out_a, ln_kv, w_kv, g_k, ln_mix_b, w_q, g_q, w_out_b, ln_mlp, w_up, w_down, ln_ple, w_ple_gate, w_ple_proj, loss_target, m_ln_mix_a, m_w_in_a, m_g_v_a, m_w_spatial, m_b_spatial, m_w_out_a, m_ln_kv, m_w_kv, m_g_k, m_ln_mix_b, m_w_q, m_g_q, m_w_out_b, m_ln_mlp, m_w_up, m_w_down, m_ln_ple, m_w_ple_gate, m_w_ple_proj, v_ln_mix_a, v_w_in_a, v_g_v_a, v_w_spatial, v_b_spatial, v_w_out_a, v_ln_kv, v_w_kv, v_g_k, v_ln_mix_b, v_w_q, v_g_q, v_w_out_b, v_ln_mlp, v_w_up, v_w_down, v_ln_ple, v_w_ple_gate, v_w_ple_proj):
    given = dict(locals())
    _PREVIOUS.clear()
    weights = {n: given[n] for n in WEIGHTS}
    shard = 2 * lax.axis_index("x") + lax.axis_index("y")
    core = lax.axis_index("c")
    shard_1 = shard.astype(jnp.int32).reshape(1)
    core_1 = core.astype(jnp.int32).reshape(1)
    place = jnp.stack([shard, core]).astype(jnp.int32)

    col_sharded = {name: cols for name, _, cols in MATRICES}
    layer_count = {name: max(layers, 1) for name, layers, _ in MATRICES}

    def cast(key, after):
        name, layer = key
        w3 = weights[name] if weights[name].ndim == 3 else weights[name][None]
        return (name, layer, col_sharded[name],
                cast_into_slot(w3, layer, shard_1, name=f"cast_{name}_{layer}", after=after))

    head = [cast(key, None) for key in GATHER_STAGES[0]]
    send_h, recv_h, flying_h, token_h = gather_start([lf[3] for lf in head], shard_1, name="gather_start_0")
    tail = [cast(key, token_h) for stage in GATHER_STAGES[1:] for key in stage]
    vec_a = gather_vectors([ln_mix_a, g_v_a], name="gather_vectors")
    send_a, recv_a, flying, token = gather_start([lf[3] for lf in tail], vec_a[0], name="gather_start_1")

    w = {"ln_mix_a": vec_a[0].reshape(1, D_MODEL),
         "g_v_a": vec_a[1].reshape(1, D_MODEL)}
    for name in REPLICATED:
        w[name] = weights[name]

    class Late:
        passed = {}

        def pass_on(self, name, layer, after):
            stage = [(name, layer) in s for s in GATHER_STAGES].index(True)
            if stage not in self.passed:
                if stage == 0:
                    base, members, sems, fly = 0, head, (send_h, recv_h), flying_h
                else:
                    base = sum(len(s) for s in GATHER_STAGES[1:stage])
                    members, sems, fly = tail[base:base + len(GATHER_STAGES[stage])], (send_a, recv_a), flying
                self.passed[stage] = (members, gather_pass_on(fly[base:base + len(members)], sems[0], sems[1], after,
                                                              name=f"gather_pass_on_{stage}", base=base))
            return stage

        def weights(self, name, layer, after):
            stage = self.pass_on(name, layer, after)
            members, (bufs, send_b, recv_b, tok) = self.passed[stage]
            got = gather_finish(bufs, send_b, recv_b, tok, [lf[3].shape for lf in members],
                                name=f"gather_finish_{stage}")
            out = {}
            for (leaf_name, leaf_layer, cols, _), arr in zip(members, got):
                out[(leaf_name, leaf_layer)] = arr if cols else arr.reshape(N_SHARDS * arr.shape[1], arr.shape[2])
            return out

        groups = []

        def pair_start(self, grads_done, after):
            self.keys = sorted(grads_done)
            views = [view(k, grads_done[k]) for k in self.keys]
            self.pair, token = exchange_start(views, [(N_SHARDS,) + v.shape[2:] for v in views], F32, pair_plan,
                                              len(views), after, name=f"grad_pair_start_{len(self.groups)}")
            return token

        def chip_start(self, after):
            tag = len(self.groups)
            mine, theirs = exchange_finish(self.pair, after, name=f"grad_pair_finish_{tag}")
            wire = [add_to_wire(a, b, core_1, name=f"grad_pair_sum_{tag}_{i}")
                    for i, (a, b) in enumerate(zip(mine, theirs))]
            chip, token = exchange_start(wire, [(3,) + v.shape[1:] for v in wire], BF16, chip_plan, 3 * len(wire),
                                         theirs[-1], name=f"grad_chip_start_{tag}")
            self.groups.append((self.keys, chip))
            return token

    def view(key, arr):
        rows = arr.shape[-2] if col_sharded[key[0]] else arr.shape[0] // N_SHARDS
        return arr.reshape(N_SHARDS, 2, rows // 2, arr.shape[-1])

    t = x.shape[1]
    late = Late()
    loss_blk, dx, g = local_step(x[0], p.reshape(2, t, PLE_DIM), loss_target[0], w, late)

    sent = {k for keys, _ in late.groups for k in keys}
    keys_last = [(name, layer) for name, layers, _ in MATRICES for layer in range(max(layers, 1))
                 if (name, layer) not in sent]
    views = [view(k, g[k[0]][k[1]] if layer_count[k[0]] == 2 else g[k[0]]) for k in keys_last]

    theirs = pair_exchange(views, name="grad_pair_exchange_last")
    wire_0 = [add_to_wire(a, b, core_1, name=f"grad_pair_sum_last_{i}") for i, (a, b) in enumerate(zip(views, theirs))]
    chip_0, token_0 = exchange_start(wire_0, [(3,) + v.shape[1:] for v in wire_0], BF16, chip_plan, 3 * len(wire_0),
                                     theirs[-1], name="grad_chip_start_last")

    grads, bufs = {}, {}

    def sum_and_share(keys, wire, landed, tag):
        for i, (key, wv, lv) in enumerate(zip(keys, wire, landed)):
            name, layer = key
            bufs[name] = sum_chips(wv, lv, place, bufs.get(name), layer, layer_count[name],
                                   name=f"grad_chip_sum_{tag}_{i}")
        names = sorted({k[0] for k in keys})
        shared = pair_share([bufs[n] for n in names], [(names.index(k[0]), k[1]) for k in keys],
                            name=f"grad_pair_share_{tag}")
        bufs.update(zip(names, shared))

    updates = {}

    def update(n, gn, part=None):
        wn, mn, vn = weights[n], given["m_" + n], given["v_" + n]
        if wn.ndim == 1:
            wn, gn, mn, vn = (a.reshape(1, -1) for a in (wn, gn, mn, vn))
        tag = "" if part is None else f"_{part[0]}"
        updates[n] = adamw(wn, gn.reshape(wn.shape), mn, vn, name=f"adamw_{n}{tag}", part=part, dest=updates.get(n))

    after = token_0
    for tag, (keys, chip) in enumerate(late.groups + [(keys_last, chip_0)]):
        wire, landed = exchange_finish(chip, after, name=f"grad_chip_finish_{tag}")
        sum_and_share(keys, wire, landed, tag)
        for name, layer in keys:
            update(name, bufs[name], (layer, layer_count[name]) if layer_count[name] == 2 else None)
        after = updates[keys[-1][0]][0]

    small = REPLICATED + SHARDED_VECTORS
    flat = jnp.concatenate([g[n].reshape(-1) for n in small] + [loss_blk[0, :1]])
    room = 8 * SMALL_ROWS * D_MODEL
    flat = jnp.concatenate([flat, jnp.zeros((room - flat.shape[0],), F32)])
    reduced = all_reduce_small(flat.reshape(8, SMALL_ROWS, D_MODEL), name="grad_small_all_reduce").reshape(-1)
    loss = reduced[sum(g[n].size for n in small)]
    at = 0
    for n in small:
        size = g[n].size
        piece = reduced[at:at + size]
        at += size
        if n in SHARDED_VECTORS:
            per = D_MODEL // N_SHARDS
            grads[n] = lax.dynamic_slice(piece, (shard * per,), (per,)).reshape(weights[n].shape)
        else:
            grads[n] = piece.reshape(weights[n].shape)
        update(n, grads[n])
    for name, _, _ in MATRICES:
        grads[name] = bufs[name].reshape(weights[name].shape)
    delta = {n: updates[n][0].reshape(weights[n].shape) for n in WEIGHTS}
    new_m = {n: updates[n][1].reshape(weights[n].shape) for n in WEIGHTS}
    new_v = {n: updates[n][2].reshape(weights[n].shape) for n in WEIGHTS}
    return (loss, dx.reshape(x.shape), *[grads[n] for n in WEIGHTS], *[delta[n] for n in WEIGHTS],
            *[new_m[n] for n in WEIGHTS], *[new_v[n] for n in WEIGHTS])
```

```python
import jax
import jax.numpy as jnp
from jax import lax
from jax.experimental import pallas as pl
from jax.experimental.pallas import tpu as pltpu

F32 = jnp.float32
BF16 = jnp.bfloat16

D_MODEL = 1024
D_FF = 4096
PLE_DIM = 256
N_GROUPS = 8
CHUNK = 128
HEAD_DIM = 64
LANES = 128
ATT_K_BLOCK = 256
ATT_Q_BLOCK = 512
EPS = 1e-6
N_SHARDS = 4
VMEM_LIMIT = 56 * 1024 * 1024

ADAM_LR = 0.001
ADAM_B1 = 0.9
ADAM_B2 = 0.999
ADAM_EPS = 1e-08
ADAM_WD = 0.01
ADAM_STEP = 10

MESH = pl.DeviceIdType.MESH


_PREVIOUS = []


def _in_order(make, in_specs, args, views_of=()):
    previous = _PREVIOUS[-1] if _PREVIOUS else None
    if previous is not None and any(a is previous for a in (*args, *views_of)):
        previous = None
    if previous is None:
        result = make(lambda body: body, list(in_specs))(*args)
    else:
        count = len(args)

        def skip(body):
            return lambda *refs: body(*refs[:count], *refs[count + 1:])

        result = make(skip, list(in_specs) + [pl.BlockSpec(memory_space=pl.ANY)])(*args, previous)
    _PREVIOUS[:] = [jax.tree_util.tree_leaves(result)[-1]]
    return result


def _pcall(body, *, name, out_shape, grid=None, in_specs=None, out_specs=None, scratch_shapes=(),
           semantics=None, aliases=None, side_effects=False, num_prefetch=0):
    params = dict(vmem_limit_bytes=VMEM_LIMIT)
    if semantics is not None:
        params["dimension_semantics"] = semantics
    if side_effects:
        params["has_side_effects"] = True
    kwargs = {}
    if aliases:
        kwargs["input_output_aliases"] = aliases

    def make(wrap, specs):
        body_ = wrap(body)
        if num_prefetch:
            spec = pltpu.PrefetchScalarGridSpec(num_scalar_prefetch=num_prefetch, grid=grid, in_specs=specs,
                                                out_specs=out_specs, scratch_shapes=list(scratch_shapes))
            return pl.pallas_call(body_, name=name, out_shape=out_shape, grid_spec=spec,
                                  compiler_params=pltpu.CompilerParams(**params), **kwargs)
        more = dict(kwargs, in_specs=specs)
        if grid is not None:
            more["grid"] = grid
        if out_specs is not None:
            more["out_specs"] = out_specs
        return pl.pallas_call(body_, name=name, out_shape=out_shape, scratch_shapes=list(scratch_shapes),
                              compiler_params=pltpu.CompilerParams(**params), **more)

    return lambda *args: _in_order(make, in_specs, args)


def _sds(shape, dtype):
    return jax.ShapeDtypeStruct(shape, dtype)


_GELU_C = 0.7978845608028654
_GELU_A = 0.044715


def _gelu(x):
    inner = _GELU_C * (x + _GELU_A * (x * x * x))
    return 0.5 * x * (1.0 + jnp.tanh(inner))


def _gelu_grad(x):
    x2 = x * x
    t = jnp.tanh(_GELU_C * (x + _GELU_A * (x2 * x)))
    return 0.5 * (1.0 + t) + 0.5 * x * (1.0 - t * t) * (_GELU_C * (1.0 + 3.0 * _GELU_A * x2))


def _sigmoid(x):
    return 1.0 / (1.0 + jnp.exp(-x))


def _log_sigmoid(z):
    return jnp.minimum(z, 0.0) - jnp.log(1.0 + jnp.exp(-jnp.abs(z)))


def _dot(a, b):
    return jnp.dot(a, b, preferred_element_type=F32)


def _dot_nt(a, b):
    return lax.dot_general(a, b, (((1,), (1,)), ((), ())), preferred_element_type=F32)


def _dot_tn(a, b):
    return lax.dot_general(a, b, (((0,), (0,)), ((), ())), preferred_element_type=F32)


def _head_rstd(x):
    lane = lax.broadcasted_iota(jnp.int32, x.shape, 1)
    low = lane < HEAD_DIM
    sq = x * x
    s_lo = jnp.sum(jnp.where(low, sq, 0.0), axis=-1, keepdims=True)
    s_hi = jnp.sum(jnp.where(low, 0.0, sq), axis=-1, keepdims=True)
    ms = jnp.where(low, s_lo, s_hi) * (1.0 / HEAD_DIM)
    return lax.rsqrt(ms + EPS)


def _head_mean(x):
    lane = lax.broadcasted_iota(jnp.int32, x.shape, 1)
    low = lane < HEAD_DIM
    s_lo = jnp.sum(jnp.where(low, x, 0.0), axis=-1, keepdims=True)
    s_hi = jnp.sum(jnp.where(low, 0.0, x), axis=-1, keepdims=True)
    return jnp.where(low, s_lo, s_hi) * (1.0 / HEAD_DIM)


def _full(shape):
    zeros = (0,) * len(shape)
    return pl.BlockSpec(shape, lambda i: zeros)


def norm_matmul(x, g, w, *, name, epilogue="none", tm=512):
    t, d = x.shape
    sharded = w.ndim == 3
    per = w.shape[2] if sharded else w.shape[1]
    n = N_SHARDS * per if sharded else per
    tm = min(tm, t)

    def body(x_ref, g_ref, w_ref, h_ref, r_ref, *outs):
        xv = x_ref[...]
        r = lax.rsqrt(jnp.mean(xv * xv, axis=-1, keepdims=True) + EPS)
        h = ((xv * r) * g_ref[...]).astype(BF16)
        h_ref[...] = h
        r_ref[...] = r
        for s in range(N_SHARDS if sharded else 1):
            cols = slice(s * per, (s + 1) * per)
            y = _dot(h, w_ref[s] if sharded else w_ref[...])
            if epilogue == "none":
                outs[0][:, cols] = y
            else:
                a = jnp.maximum(y, 0.0)
                outs[0][:, cols] = a.astype(BF16)
                outs[1][:, cols] = (a * a).astype(BF16)

    row = lambda i: (i, 0)
    out_shape = [_sds((t, d), BF16), _sds((t, 1), F32)]
    out_specs = [pl.BlockSpec((tm, d), row), pl.BlockSpec((tm, 1), row)]
    if epilogue == "none":
        out_shape.append(_sds((t, n), F32))
        out_specs.append(pl.BlockSpec((tm, n), row))
    else:
        out_shape += [_sds((t, n), BF16), _sds((t, n), BF16)]
        out_specs += [pl.BlockSpec((tm, n), row)] * 2
    return _pcall(
        body, name=name, out_shape=out_shape, grid=(t // tm,),
        in_specs=[pl.BlockSpec((tm, d), row), _full((1, d)), _full(w.shape)],
        out_specs=out_specs, semantics=("parallel",))(x, g, w)


def matmul_residual(a, w, res, *, name, tm=512):
    t, k = a.shape
    n = w.shape[1]
    tm = min(tm, t)

    def body(a_ref, w_ref, res_ref, o_ref):
        o_ref[...] = res_ref[...] + _dot(a_ref[...], w_ref[...])

    row = lambda i: (i, 0)
    return _pcall(
        body, name=name, out_shape=_sds((t, n), F32), grid=(t // tm,),
        in_specs=[pl.BlockSpec((tm, k), row), _full(w.shape), pl.BlockSpec((tm, n), row)],
        out_specs=pl.BlockSpec((tm, n), row), semantics=("parallel",))(a, w, res)


def ple_forward(x, g, w_gate, p, w_proj, *, name, tm=256):
    t, d = x.shape
    tm = min(tm, t)

    def body(x_ref, g_ref, wg_ref, p_ref, wp_ref, h_ref, r_ref, gate_ref, pp_ref, o_ref):
        xv = x_ref[...]
        r = lax.rsqrt(jnp.mean(xv * xv, axis=-1, keepdims=True) + EPS)
        h = ((xv * r) * g_ref[...]).astype(BF16)
        h_ref[...] = h
        r_ref[...] = r
        gate = _sigmoid(_dot(h, wg_ref[...]))
        gate_ref[...] = gate
        pb = p_ref[...].astype(BF16)
        per = d // N_SHARDS
        for s in range(N_SHARDS):
            cols = slice(s * per, (s + 1) * per)
            pp = _dot(pb, wp_ref[s])
            pp_ref[:, cols] = pp.astype(BF16)
            o_ref[:, cols] = xv[:, cols] + pp * gate[:, cols]

    row = lambda i: (i, 0)
    fixed = lambda i: (0, 0)
    return _pcall(
        body, name=name,
        out_shape=[_sds((t, d), BF16), _sds((t, 1), F32), _sds((t, d), F32), _sds((t, d), BF16), _sds((t, d), F32)],
        grid=(t // tm,),
        in_specs=[pl.BlockSpec((tm, d), row), pl.BlockSpec((1, d), fixed), pl.BlockSpec((d, d), fixed),
                  pl.BlockSpec((tm, PLE_DIM), row),
                  pl.BlockSpec((N_SHARDS, PLE_DIM, d // N_SHARDS), lambda i: (0, 0, 0))],
        out_specs=[pl.BlockSpec((tm, d), row), pl.BlockSpec((tm, 1), row), pl.BlockSpec((tm, d), row),
                   pl.BlockSpec((tm, d), row), pl.BlockSpec((tm, d), row)],
        semantics=("parallel",))(x, g, w_gate, p, w_proj)


def _tril_mask():
    r = lax.broadcasted_iota(jnp.int32, (CHUNK, CHUNK), 0)
    c = lax.broadcasted_iota(jnp.int32, (CHUNK, CHUNK), 1)
    return c <= r


def _sgu_common(pre_ref, gv_ref, ws_ref):
    pre = pre_ref[...]
    pre_u, pre_v = pre[:, :D_MODEL], pre[:, D_MODEL:]
    u = _gelu(pre_u)
    v = _gelu(pre_v)
    r = lax.rsqrt(jnp.mean(v * v, axis=-1, keepdims=True) + EPS)
    vhat = v * r
    vn = (vhat * gv_ref[...]).astype(BF16)
    tril = _tril_mask()
    wm = [jnp.where(tril, ws_ref[g], 0.0).astype(BF16) for g in range(N_GROUPS)]
    return pre_u, pre_v, u, r, vhat, vn, wm, tril


def sgu_forward(pre, g_v, w_s, b_full, *, name):
    t = pre.shape[0]

    def body(pre_ref, gv_ref, ws_ref, b_ref, y_ref):
        _, _, u, _, _, vn, wm, _ = _sgu_common(pre_ref, gv_ref, ws_ref)
        for g in range(N_GROUPS):
            cols = slice(g * LANES, (g + 1) * LANES)
            mix = _dot(wm[g], vn[:, cols]) + b_ref[:, cols]
            y_ref[:, cols] = (u[:, cols] * mix).astype(BF16)

    return _pcall(
        body, name=name, out_shape=_sds((t, D_MODEL), BF16), grid=(t // CHUNK,),
        in_specs=[pl.BlockSpec((CHUNK, 2 * D_MODEL), lambda i: (i, 0)), pl.BlockSpec((1, D_MODEL), lambda i: (0, 0)),
                  pl.BlockSpec((N_GROUPS, CHUNK, CHUNK), lambda i: (0, 0, 0)),
                  pl.BlockSpec((CHUNK, D_MODEL), lambda i: (0, 0))],
        out_specs=pl.BlockSpec((CHUNK, D_MODEL), lambda i: (i, 0)),
        semantics=("parallel",))(pre, g_v, w_s, b_full)


def head_norm(pre, g128, *, name, col_block=0, scale=1.0, passthrough=False, tm=512):
    t = pre.shape[0]
    tm = min(tm, t)

    def body(*refs):
        if passthrough:
            x_ref, v_ref, g_ref, o_ref, vo_ref = refs
            vo_ref[...] = v_ref[...].astype(BF16)
        else:
            x_ref, g_ref, o_ref = refs
        g = g_ref[...] * scale
        for b in range(D_MODEL // LANES):
            cols = slice(b * LANES, (b + 1) * LANES)
            xv = x_ref[:, cols]
            o_ref[:, cols] = ((xv * _head_rstd(xv)) * g).astype(BF16)

    x_spec = pl.BlockSpec((tm, D_MODEL), lambda i: (i, col_block))
    g_spec = pl.BlockSpec((1, LANES), lambda i: (0, 0))
    o_spec = pl.BlockSpec((tm, D_MODEL), lambda i: (i, 0))
    if passthrough:
        return _pcall(body, name=name, out_shape=[_sds((t, D_MODEL), BF16)] * 2, grid=(t // tm,),
                      in_specs=[x_spec, pl.BlockSpec((tm, D_MODEL), lambda i: (i, 1)), g_spec],
                      out_specs=[o_spec, o_spec], semantics=("parallel",))(pre, pre, g128)
    return _pcall(body, name=name, out_shape=_sds((t, D_MODEL), BF16), grid=(t // tm,),
                  in_specs=[x_spec, g_spec], out_specs=o_spec, semantics=("parallel",))(pre, g128)


def _suffix_matrix(n):
    r = lax.broadcasted_iota(jnp.int32, (n, n), 0)
    c = lax.broadcasted_iota(jnp.int32, (n, n), 1)
    return jnp.where(r > c, 1.0, 0.0).astype(BF16)


def _prefix_matrix(n):
    r = lax.broadcasted_iota(jnp.int32, (n, n), 0)
    c = lax.broadcasted_iota(jnp.int32, (n, n), 1)
    return jnp.where(r < c, 1.0, 0.0).astype(BF16)


def _block_cumsum(a, tri):
    return _dot(a.astype(BF16), tri)


def _stacked_causal(nq, nk, shift):
    r = lax.broadcasted_iota(jnp.int32, (2 * nq, nk), 0)
    c = lax.broadcasted_iota(jnp.int32, (2 * nq, nk), 1)
    return c + shift < jnp.where(r >= nq, r - nq, r)


def _att_blocks(t):
    bq, bk = min(ATT_Q_BLOCK, t), min(ATT_K_BLOCK, t)
    return bq, bk, bq // bk


def _stack_heads(a, low):
    zero = jnp.zeros_like(a)
    return jnp.concatenate([jnp.where(low, a, zero), jnp.where(low, zero, a)], axis=0)


def stick_breaking_forward(q, k, v, *, name):
    t = q.shape[0]
    bq, bk, ratio = _att_blocks(t)

    def body(q_ref, k_ref, v_ref, o_ref):
        i = pl.program_id(1)
        low = lax.broadcasted_iota(jnp.int32, (bq, LANES), 1) < HEAD_DIM
        tri = _suffix_matrix(bk)
        qs = _stack_heads(q_ref[...], low)

        def block(j, carry, acc, causal=None):
            rows = pl.ds(pl.multiple_of(j * bk, bk), bk)
            z = _dot_nt(qs, k_ref[rows, :])
            ls = _log_sigmoid(z)
            lg = ls - z
            if causal is not None:
                lg = jnp.where(causal, lg, 0.0)
            s = ls + _block_cumsum(lg, tri) + carry
            a = jnp.exp(s)
            if causal is not None:
                a = jnp.where(causal, a, 0.0)
            acc = acc + _dot(a.astype(BF16), v_ref[rows, :])
            return carry + jnp.sum(lg, axis=-1, keepdims=True), acc

        state = (jnp.zeros((2 * bq, 1), F32), jnp.zeros((2 * bq, LANES), F32))
        for m in reversed(range(ratio)):
            state = block(ratio * i + m, state[0], state[1], _stacked_causal(bq, bk, m * bk))
        first = ratio * i

        def two_blocks(n, st):
            st = block(first - 1 - 2 * n, st[0], st[1])
            return block(first - 2 - 2 * n, st[0], st[1])

        state = lax.fori_loop(0, first // 2, two_blocks, state)
        _, acc = lax.fori_loop(0, first % 2, lambda n, st: block(0, st[0], st[1]), state)
        o_ref[...] = jnp.where(low, acc[:bq], acc[bq:]).astype(BF16)

    return _pcall(
        body, name=name, out_shape=_sds((t, D_MODEL), BF16), grid=(D_MODEL // LANES, t // bq),
        in_specs=[pl.BlockSpec((bq, LANES), lambda p, i: (i, p)), pl.BlockSpec((t, LANES), lambda p, i: (0, p)),
                  pl.BlockSpec((t, LANES), lambda p, i: (0, p))],
        out_specs=pl.BlockSpec((bq, LANES), lambda p, i: (i, p)),
        semantics=("parallel", "arbitrary"))(q, k, v)


def loss_forward(x, target, *, name, tm=512):
    t, d = x.shape
    tm = min(tm, t)

    def body(x_ref, t_ref, l_ref, dx_ref):
        @pl.when(pl.program_id(0) == 0)
        def _():
            l_ref[...] = jnp.zeros_like(l_ref)

        diff = x_ref[...] - t_ref[...]
        dx_ref[...] = diff * (1.0 / d)
        l_ref[...] += 0.5 * jnp.sum(jnp.mean(diff * diff, axis=-1, keepdims=True))

    return _pcall(
        body, name=name, out_shape=[_sds((8, LANES), F32), _sds((t, d), F32)], grid=(t // tm,),
        in_specs=[pl.BlockSpec((tm, d), lambda i: (i, 0))] * 2,
        out_specs=[pl.BlockSpec((8, LANES), lambda i: (0, 0)), pl.BlockSpec((tm, d), lambda i: (i, 0))],
        semantics=("arbitrary",))(x, target)


def matmul_nt(dy, w, *, name, mul=None, out_dtype=F32, tm=512):
    t, n = dy.shape
    k = w.shape[0]
    tm = min(tm, t)

    def body(*refs):
        if mul is None:
            dy_ref, w_ref, o_ref = refs
        else:
            dy_ref, w_ref, m_ref, o_ref = refs
        y = _dot_nt(dy_ref[...].astype(BF16), w_ref[...])
        if mul is not None:
            y = y * (2.0 * m_ref[...].astype(F32))
        o_ref[...] = y.astype(out_dtype)

    row = lambda i: (i, 0)
    in_specs = [pl.BlockSpec((tm, n), row), _full(w.shape)]
    args = [dy, w]
    if mul is not None:
        in_specs.append(pl.BlockSpec((tm, k), row))
        args.append(mul)
    return _pcall(body, name=name, out_shape=_sds((t, k), out_dtype), grid=(t // tm,), in_specs=in_specs,
                  out_specs=pl.BlockSpec((tm, k), row), semantics=("parallel",))(*args)


def matmul_tn(a, dy, *, name, col_shards, tk=512):
    t, k = a.shape
    n = dy.shape[1]
    if col_shards:
        tn = n // N_SHARDS

        def body(a_ref, dy_ref, o_ref):
            o_ref[...] = _dot_tn(a_ref[...].astype(BF16), dy_ref[...].astype(BF16))

        return _pcall(body, name=name, out_shape=_sds((N_SHARDS, k, tn), F32), grid=(N_SHARDS,),
                      in_specs=[_full((t, k)), pl.BlockSpec((t, tn), lambda j: (0, j))],
                      out_specs=pl.BlockSpec((None, k, tn), lambda j: (j, 0, 0)), semantics=("parallel",))(a, dy)

    tk = min(tk, k)

    def body(a_ref, dy_ref, o_ref, dy_bf):
        @pl.when(pl.program_id(0) == 0)
        def _():
            dy_bf[...] = dy_ref[...].astype(BF16)

        o_ref[...] = _dot_tn(a_ref[...].astype(BF16), dy_bf[...])

    return _pcall(body, name=name, out_shape=_sds((k, n), F32), grid=(k // tk,),
                  in_specs=[pl.BlockSpec((t, tk), lambda i: (0, i)), _full((t, n))],
                  out_specs=pl.BlockSpec((tk, n), lambda i: (i, 0)),
                  scratch_shapes=[pltpu.VMEM((t, n), BF16)], semantics=("arbitrary",))(a, dy)


def norm_backward(dpre, w, x, g, rstd, dx_out, *, name, tm=512):
    t, d = x.shape
    n = dpre.shape[1]
    tm = min(tm, t)
    if w.ndim == 3:
        w_spec = pl.BlockSpec(w.shape, lambda i: (0, 0, 0))
    else:
        w_spec = pl.BlockSpec(w.shape, lambda i: (0, 0))

    def body(dp_ref, w_ref, x_ref, g_ref, r_ref, dxo_ref, dx_ref, dg_ref):
        @pl.when(pl.program_id(0) == 0)
        def _():
            dg_ref[...] = jnp.zeros_like(dg_ref)

        if w.ndim == 3:
            per = n // N_SHARDS
            dh = _dot_nt(dp_ref[:, 0:per], w_ref[0])
            for s in range(1, N_SHARDS):
                dh = dh + _dot_nt(dp_ref[:, s * per:(s + 1) * per], w_ref[s])
        else:
            dh = _dot_nt(dp_ref[...], w_ref[...])
        r = r_ref[...]
        xn = x_ref[...] * r
        dg_ref[...] += jnp.sum(dh * xn, axis=0, keepdims=True)
        dxn = dh * g_ref[...]
        dx = r * (dxn - xn * jnp.mean(dxn * xn, axis=-1, keepdims=True))
        dx_ref[...] = dxo_ref[...] + dx

    row = lambda i: (i, 0)
    fixed = lambda i: (0, 0)
    return _pcall(
        body, name=name, out_shape=[_sds((t, d), F32), _sds((1, d), F32)], grid=(t // tm,),
        in_specs=[pl.BlockSpec((tm, n), row), w_spec, pl.BlockSpec((tm, d), row),
                  pl.BlockSpec((1, d), fixed), pl.BlockSpec((tm, 1), row), pl.BlockSpec((tm, d), row)],
        out_specs=[pl.BlockSpec((tm, d), row), pl.BlockSpec((1, d), fixed)],
        semantics=("arbitrary",))(dpre, w, x, g, rstd, dx_out)


def ple_backward(dx, gate, pp, *, name, tm=512):
    t, d = dx.shape
    tm = min(tm, t)

    def body(dx_ref, gate_ref, pp_ref, dg_ref, dp_ref):
        dxv = dx_ref[...]
        gate = gate_ref[...]
        dg_ref[...] = (dxv * pp_ref[...].astype(F32) * (gate * (1.0 - gate))).astype(BF16)
        dp_ref[...] = (dxv * gate).astype(BF16)

    spec = pl.BlockSpec((tm, d), lambda i: (i, 0))
    return _pcall(body, name=name, out_shape=[_sds((t, d), BF16)] * 2, grid=(t // tm,), in_specs=[spec] * 3,
                  out_specs=[spec] * 2, semantics=("parallel",))(dx, gate, pp)


def sgu_backward(dy, pre, g_v, w_s, b_full, *, name):
    t = pre.shape[0]
    n_chunks = t // CHUNK

    def body(dy_ref, pre_ref, gv_ref, ws_ref, b_ref, dpre_ref, dws_ref, db_ref, dgv_ref, dvn_s, dbf_s):
        step = pl.program_id(0)

        @pl.when(step == 0)
        def _():
            dws_ref[...] = jnp.zeros_like(dws_ref)
            dgv_ref[...] = jnp.zeros_like(dgv_ref)
            dbf_s[...] = jnp.zeros_like(dbf_s)

        pre_u, pre_v, u, r, vhat, vn, wm, tril = _sgu_common(pre_ref, gv_ref, ws_ref)
        dyv = dy_ref[...]
        for g in range(N_GROUPS):
            cols = slice(g * LANES, (g + 1) * LANES)
            mix = _dot(wm[g], vn[:, cols]) + b_ref[:, cols]
            dmix = dyv[:, cols] * u[:, cols]
            dmix_b = dmix.astype(BF16)
            du = dyv[:, cols] * mix
            dpre_ref[:, cols] = (du * _gelu_grad(pre_u[:, cols])).astype(BF16)
            dws_ref[g] += jnp.where(tril, _dot_nt(dmix_b, vn[:, cols]), 0.0)
            dbf_s[:, cols] += dmix
            dvn_s[:, cols] = _dot_tn(wm[g], dmix_b)
        dvn = dvn_s[...]
        dgv_ref[...] += jnp.sum(dvn * vhat, axis=0, keepdims=True)
        dxn = dvn * gv_ref[...]
        dv = r * (dxn - vhat * jnp.mean(dxn * vhat, axis=-1, keepdims=True))
        dpre_ref[:, D_MODEL:] = (dv * _gelu_grad(pre_v)).astype(BF16)

        @pl.when(step == n_chunks - 1)
        def _():
            lane = lax.broadcasted_iota(jnp.int32, (CHUNK, LANES), 1)
            acc = jnp.zeros((CHUNK, LANES), F32)
            for g in range(N_GROUPS):
                s = jnp.sum(dbf_s[:, g * LANES:(g + 1) * LANES], axis=-1, keepdims=True)
                acc = jnp.where(lane == g, s, acc)
            db_ref[...] = acc

    fixed2 = lambda i: (0, 0)
    return _pcall(
        body, name=name,
        out_shape=[_sds((t, 2 * D_MODEL), BF16), _sds((N_GROUPS, CHUNK, CHUNK), F32), _sds((CHUNK, LANES), F32),
                   _sds((1, D_MODEL), F32)],
        grid=(n_chunks,),
        in_specs=[pl.BlockSpec((CHUNK, D_MODEL), lambda i: (i, 0)), pl.BlockSpec((CHUNK, 2 * D_MODEL), lambda i: (i, 0)),
                  pl.BlockSpec((1, D_MODEL), fixed2), pl.BlockSpec((N_GROUPS, CHUNK, CHUNK), lambda i: (0, 0, 0)),
                  pl.BlockSpec((CHUNK, D_MODEL), fixed2)],
        out_specs=[pl.BlockSpec((CHUNK, 2 * D_MODEL), lambda i: (i, 0)),
                   pl.BlockSpec((N_GROUPS, CHUNK, CHUNK), lambda i: (0, 0, 0)), pl.BlockSpec((CHUNK, LANES), fixed2),
                   pl.BlockSpec((1, D_MODEL), fixed2)],
        scratch_shapes=[pltpu.VMEM((CHUNK, D_MODEL), F32), pltpu.VMEM((CHUNK, D_MODEL), F32)],
        semantics=("arbitrary",))(dy, pre, g_v, w_s, b_full)


def head_norm_backward(dy, pre, g128, *, name, col_block=0, scale=1.0, passthrough=None, tm=512):
    t = dy.shape[0]
    tm = min(tm, t)
    width = 2 * D_MODEL if passthrough is not None else D_MODEL

    def body(*refs):
        if passthrough is not None:
            dy_ref, x_ref, g_ref, dv_ref, o_ref, dg_ref = refs
            o_ref[:, D_MODEL:] = dv_ref[...].astype(BF16)
        else:
            dy_ref, x_ref, g_ref, o_ref, dg_ref = refs

        @pl.when(pl.program_id(0) == 0)
        def _():
            dg_ref[...] = jnp.zeros_like(dg_ref)

        g = g_ref[...]
        dg = jnp.zeros((1, LANES), F32)
        for b in range(D_MODEL // LANES):
            cols = slice(b * LANES, (b + 1) * LANES)
            xv = x_ref[:, cols]
            r = _head_rstd(xv)
            xn = xv * r
            dyv = dy_ref[:, cols] * scale
            dg = dg + jnp.sum(dyv * xn, axis=0, keepdims=True)
            dxn = dyv * g
            o_ref[:, cols] = (r * (dxn - xn * _head_mean(dxn * xn))).astype(BF16)
        dg_ref[...] += dg

    row = lambda i: (i, 0)
    in_specs = [pl.BlockSpec((tm, D_MODEL), row), pl.BlockSpec((tm, D_MODEL), lambda i: (i, col_block)),
                pl.BlockSpec((1, LANES), lambda i: (0, 0))]
    args = [dy, pre, g128]
    if passthrough is not None:
        in_specs.append(pl.BlockSpec((tm, D_MODEL), row))
        args.append(passthrough)
    return _pcall(body, name=name, out_shape=[_sds((t, width), BF16), _sds((1, LANES), F32)], grid=(t // tm,),
                  in_specs=in_specs,
                  out_specs=[pl.BlockSpec((tm, width), row), pl.BlockSpec((1, LANES), lambda i: (0, 0))],
                  semantics=("arbitrary",))(*args)


def stick_breaking_backward(q, k, v, do, *, name):
    t = q.shape[0]
    bq, bk, ratio = _att_blocks(t)

    def body(q_ref, k_ref, v_ref, do_ref, dq_ref, dk_ref, dv_ref, s_buf, sg_buf):
        i = pl.program_id(1)

        @pl.when(i == 0)
        def _():
            dk_ref[...] = jnp.zeros_like(dk_ref)
            dv_ref[...] = jnp.zeros_like(dv_ref)

        low = lax.broadcasted_iota(jnp.int32, (bq, LANES), 1) < HEAD_DIM
        suffix = _suffix_matrix(bk)
        prefix = _prefix_matrix(bk)
        qs = _stack_heads(q_ref[...], low)
        dos = _stack_heads(do_ref[...], low)
        first = ratio * i

        def log_weights(j, carry, causal=None):
            rows = pl.ds(pl.multiple_of(j * bk, bk), bk)
            z = _dot_nt(qs, k_ref[rows, :])
            ls = _log_sigmoid(z)
            lg = ls - z
            if causal is not None:
                lg = jnp.where(causal, lg, 0.0)
            s_buf[j] = ls + _block_cumsum(lg, suffix) + carry
            sg_buf[j] = jnp.exp(ls)
            return carry + jnp.sum(lg, axis=-1, keepdims=True)

        carry = jnp.zeros((2 * bq, 1), F32)
        for m in reversed(range(ratio)):
            carry = log_weights(first + m, carry, _stacked_causal(bq, bk, m * bk))
        carry = lax.fori_loop(0, first // 2,
                              lambda n, c: log_weights(first - 2 - 2 * n, log_weights(first - 1 - 2 * n, c)), carry)
        lax.fori_loop(0, first % 2, lambda n, c: log_weights(0, c), carry)

        def grads(j, pcarry, dq_acc, causal=None):
            rows = pl.ds(pl.multiple_of(j * bk, bk), bk)
            a = jnp.exp(s_buf[j])
            if causal is not None:
                a = jnp.where(causal, a, 0.0)
            sg = sg_buf[j]
            ds = _dot_nt(dos, v_ref[rows, :]) * a
            before = _block_cumsum(ds, prefix) + pcarry
            if causal is not None:
                before = jnp.where(causal, before, 0.0)
            dz = (ds - sg * (ds + before)).astype(BF16)
            dq_acc = dq_acc + _dot(dz, k_ref[rows, :])
            dk_ref[rows, :] += _dot_tn(dz, qs)
            dv_ref[rows, :] += _dot_tn(a.astype(BF16), dos)
            return pcarry + jnp.sum(ds, axis=-1, keepdims=True), dq_acc

        def two_blocks(n, st):
            st = grads(2 * n, st[0], st[1])
            return grads(2 * n + 1, st[0], st[1])

        state = lax.fori_loop(0, first // 2, two_blocks,
                              (jnp.zeros((2 * bq, 1), F32), jnp.zeros((2 * bq, LANES), F32)))
        state = lax.fori_loop(0, first % 2, lambda n, st: grads(first - 1, st[0], st[1]), state)
        for m in range(ratio):
            state = grads(first + m, state[0], state[1], _stacked_causal(bq, bk, m * bk))
        dq_ref[...] = jnp.where(low, state[1][:bq], state[1][bq:])

    full = pl.BlockSpec((t, LANES), lambda p, i: (0, p))
    qblk = pl.BlockSpec((bq, LANES), lambda p, i: (i, p))
    return _pcall(
        body, name=name, out_shape=[_sds((t, D_MODEL), F32)] * 3, grid=(D_MODEL // LANES, t // bq),
        in_specs=[qblk, full, full, qblk], out_specs=[qblk, full, full],
        scratch_shapes=[pltpu.VMEM((t // bk, 2 * bq, bk), F32), pltpu.VMEM((t // bk, 2 * bq, bk), F32)],
        semantics=("parallel", "arbitrary"))(q, k, v, do)


def _mlp_backward(dx, saved, g, w_up, w_down, tag):
    x, h, r, a, a2 = saved
    d_w_down = matmul_tn(a2, dx, name=f"d_w_down_{tag}", col_shards=False)
    dpre = matmul_nt(dx, w_down, name=f"d_mlp_act_{tag}", mul=a, out_dtype=BF16)
    d_w_up = matmul_tn(h, dpre, name=f"d_w_up_{tag}", col_shards=True)
    dx, d_g = norm_backward(dpre, w_up, x, g, r, dx, name=f"d_mlp_norm_{tag}")
    return dx, d_w_up, d_w_down, d_g


def _ple_backward(dx, saved, p, g, w_gate, tag):
    x, h, r, gate, pp = saved
    dgate, dproj = ple_backward(dx, gate, pp, name=f"d_ple_{tag}")
    d_w_proj = matmul_tn(p, dproj, name=f"d_w_ple_proj_{tag}", col_shards=True)
    d_w_gate = matmul_tn(h, dgate, name=f"d_w_ple_gate_{tag}", col_shards=False)
    dx, d_g = norm_backward(dgate, w_gate, x, g, r, dx, name=f"d_ple_norm_{tag}")
    return dx, d_w_gate, d_w_proj, d_g


def local_step(x, p, target, w, late=None):
    row = lambda v: v.reshape(1, -1)
    g128 = lambda v: jnp.tile(v.reshape(1, HEAD_DIM), (1, 2))
    scale = HEAD_DIM ** -0.5
    b_full = jnp.repeat(jnp.transpose(w["b_spatial"][0]), LANES, axis=1)
    w_s = w["w_spatial"][0]

    mats = {}
    for name, value in w.items():
        if isinstance(value, tuple):
            mats.update({(name, layer): v for layer, v in enumerate(value)})
    if "w_kv" in w:
        mats[("w_kv", 0)] = w["w_kv"]

    def fetch(name, layer, after):
        if (name, layer) not in mats:
            mats.update(late.weights(name, layer, after))
        return mats[(name, layer)]

    def mlp_forward(x_in, layer):
        h, r, a, a2 = norm_matmul(x_in, row(w["ln_mlp"][layer]), fetch("w_up", layer, x_in), name=f"mlp_up_{layer}",
                                  epilogue="relu2")
        return matmul_residual(a2, fetch("w_down", layer, a2), x_in, name=f"mlp_down_{layer}"), (x_in, h, r, a, a2)

    def ple(x_in, layer):
        return ple_forward(x_in, row(w["ln_ple"][layer]), fetch("w_ple_gate", layer, x_in), p[layer],
                           fetch("w_ple_proj", layer, x_in), name=f"ple_{layer}")

    x0 = x
    h_a, r_a, pre_a = norm_matmul(x0, row(w["ln_mix_a"][0]), fetch("w_in_a", 0, x0), name="sgu_in")
    y_a = sgu_forward(pre_a, row(w["g_v_a"][0]), w_s, b_full, name="sgu_mix")
    x1 = matmul_residual(y_a, fetch("w_out_a", 0, y_a), x0, name="sgu_out")
    x2, mlp0 = mlp_forward(x1, 0)
    ple0 = ple(x2, 0)
    x3 = ple0[4]
    h_kv, r_kv, kv_pre = norm_matmul(x3, row(w["ln_kv"]), fetch("w_kv", 0, x3), name="kv_proj")
    k_n, v_b = head_norm(kv_pre, g128(w["g_k"]), name="k_norm", passthrough=True)
    h_q, r_q, q_pre = norm_matmul(x3, row(w["ln_mix_b"][0]), fetch("w_q", 0, k_n), name="q_proj")
    q_n = head_norm(q_pre, g128(w["g_q"][0]), name="q_norm", scale=scale)
    o = stick_breaking_forward(q_n, k_n, v_b, name="sb_fwd")
    if late is not None:
        late.pass_on("w_up", 1, o)
    x4 = matmul_residual(o, fetch("w_out_b", 0, o), x3, name="sb_out")
    x5, mlp1 = mlp_forward(x4, 1)
    ple1 = ple(x5, 1)
    x6 = ple1[4]
    loss_blk, dx = loss_forward(x6, target, name="loss")

    g = {}
    dx, dwg1, dwp1, dlnp1 = _ple_backward(dx, (x5,) + tuple(ple1[:4]), p[1], row(w["ln_ple"][1]),
                                          mats[("w_ple_gate", 1)], 1)
    dx, dwu1, dwd1, dlnm1 = _mlp_backward(dx, mlp1, row(w["ln_mlp"][1]), mats[("w_up", 1)], mats[("w_down", 1)], 1)
    g["w_out_b"] = matmul_tn(o, dx, name="d_w_out_b", col_shards=False)
    do = matmul_nt(dx, mats[("w_out_b", 0)], name="d_sb_out", out_dtype=BF16)
    dq_n, dk_n, dv = stick_breaking_backward(q_n, k_n, v_b, do, name="sb_bwd")
    dq_pre, dgq = head_norm_backward(dq_n, q_pre, g128(w["g_q"][0]), name="d_q_norm", scale=scale)
    dkv_pre, dgk = head_norm_backward(dk_n, kv_pre, g128(w["g_k"]), name="d_k_norm", passthrough=dv)
    g["w_q"] = matmul_tn(h_q, dq_pre, name="d_w_q", col_shards=False)
    g["w_kv"] = matmul_tn(h_kv, dkv_pre, name="d_w_kv", col_shards=True)
    dx, g["ln_mix_b"] = norm_backward(dq_pre, mats[("w_q", 0)], x3, row(w["ln_mix_b"][0]), r_q, dx, name="d_q_in")
    dx, g["ln_kv"] = norm_backward(dkv_pre, mats[("w_kv", 0)], x3, row(w["ln_kv"]), r_kv, dx, name="d_kv_in")
    g["g_q"] = dgq[:, :HEAD_DIM] + dgq[:, HEAD_DIM:]
    g["g_k"] = (dgk[:, :HEAD_DIM] + dgk[:, HEAD_DIM:]).reshape(HEAD_DIM)
    g["ln_kv"] = g["ln_kv"].reshape(D_MODEL)
    if late is not None:
        late.pair_start({("w_kv", 0): g["w_kv"], ("w_q", 0): g["w_q"], ("w_out_b", 0): g["w_out_b"],
                         ("w_up", 1): dwu1, ("w_down", 1): dwd1, ("w_ple_gate", 1): dwg1, ("w_ple_proj", 1): dwp1}, dx)
    dx, dwg0, dwp0, dlnp0 = _ple_backward(dx, (x2,) + tuple(ple0[:4]), p[0], row(w["ln_ple"][0]),
                                          mats[("w_ple_gate", 0)], 0)
    if late is not None:
        late.chip_start(dx)
    dx, dwu0, dwd0, dlnm0 = _mlp_backward(dx, mlp0, row(w["ln_mlp"][0]), mats[("w_up", 0)], mats[("w_down", 0)], 0)
    if late is not None:
        late.pair_start({("w_up", 0): dwu0, ("w_down", 0): dwd0, ("w_ple_gate", 0): dwg0, ("w_ple_proj", 0): dwp0}, dx)
    g["w_out_a"] = matmul_tn(y_a, dx, name="d_w_out_a", col_shards=False)
    dy_a = matmul_nt(dx, mats[("w_out_a", 0)], name="d_sgu_out")
    dpre_a, dws, db, g["g_v_a"] = sgu_backward(dy_a, pre_a, row(w["g_v_a"][0]), w_s, b_full, name="d_sgu_mix")
    if late is not None:
        late.chip_start(dpre_a)
    g["w_in_a"] = matmul_tn(h_a, dpre_a, name="d_w_in_a", col_shards=True)
    dx, g["ln_mix_a"] = norm_backward(dpre_a, mats[("w_in_a", 0)], x0, row(w["ln_mix_a"][0]), r_a, dx, name="d_sgu_in")
    g["w_spatial"] = dws[None]
    g["b_spatial"] = jnp.transpose(db[:, :N_GROUPS])[None]
    g["w_up"] = (dwu0, dwu1)
    g["w_down"] = (dwd0, dwd1)
    g["w_ple_gate"] = (dwg0, dwg1)
    g["w_ple_proj"] = (dwp0, dwp1)
    g["ln_mlp"] = jnp.concatenate([dlnm0, dlnm1], axis=0)
    g["ln_ple"] = jnp.concatenate([dlnp0, dlnp1], axis=0)
    return loss_blk, dx, g


ANY = pl.BlockSpec(memory_space=pl.ANY)


def _place():
    x, y, c = lax.axis_index("x"), lax.axis_index("y"), lax.axis_index("c")
    others = [(1 - x, y), (x, 1 - y), (1 - x, 1 - y)]
    return x, y, c, 2 * x + y, others


def cast_into_slot(w3, layer, slot, *, name, after=None, tm=512):
    _, r, c = w3.shape
    tm = min(tm, r)

    def body(slot_ref, w_ref, *rest):
        rest[-1][...] = w_ref[...].astype(BF16)

    in_specs = [pl.BlockSpec((None, tm, c), lambda i, s: (layer, i, 0))]
    args = [slot, w3]
    if after is not None:
        in_specs.append(ANY)
        args.append(after)
    return _pcall(body, name=name, out_shape=_sds((N_SHARDS, r, c), BF16), grid=(r // tm,), num_prefetch=1,
                  in_specs=in_specs, out_specs=pl.BlockSpec((None, tm, c), lambda i, s: (s[0], i, 0)),
                  semantics=("parallel",))(*args)


def gather_vectors(vecs, *, name):
    n = len(vecs)

    def body(*refs):
        src, out = refs[:n], refs[n:2 * n]
        send, recv, loc = refs[2 * n:]
        x, y, c, s_me, others = _place()

        def copy(l, k, slot):
            ox, oy = others[k]
            return pltpu.make_async_remote_copy(src[l], out[l].at[slot], send.at[l, k], recv.at[l, k],
                                                device_id=(ox, oy, c), device_id_type=MESH)

        for l in range(n):
            for k in range(3):
                copy(l, k, s_me).start()
        for l in range(n):
            own = pltpu.make_async_copy(src[l], out[l].at[s_me], loc)
            own.start()
            own.wait()
        for l in range(n):
            for k in range(3):
                ox, oy = others[k]
                copy(l, k, 2 * ox + oy).wait_recv()
                copy(l, k, s_me).wait_send()

    return _pcall(body, name=name, out_shape=[_sds((N_SHARDS,) + v.shape, F32) for v in vecs], in_specs=[ANY] * n,
                  out_specs=[ANY] * n,
                  scratch_shapes=[pltpu.SemaphoreType.DMA((n, 3)), pltpu.SemaphoreType.DMA((n, 3)),
                                  pltpu.SemaphoreType.DMA(())],
                  side_effects=True)(*vecs)


HBM = pl.BlockSpec(memory_space=pltpu.HBM)
SEM = pl.BlockSpec(memory_space=pltpu.SEMAPHORE)
DATAFLOW = pltpu.SideEffectType.DATAFLOW_SIDE_EFFECTING


def _split_call(body, *, name, out_shape, in_specs, out_specs, aliases, views_of=()):
    def make(wrap, specs):
        body_ = wrap(body)
        return pl.pallas_call(body_, name=name, out_shape=out_shape, in_specs=specs, out_specs=out_specs,
                              input_output_aliases=aliases,
                              compiler_params=pltpu.CompilerParams(has_side_effects=DATAFLOW))

    return lambda *args: _in_order(make, in_specs, args, views_of)


def _token_shape():
    return jax.ShapeDtypeStruct((8, LANES), F32)


def gather_start(mats, after, *, name):
    n = len(mats)
    halves = [pltpu.with_memory_space_constraint(m.reshape(N_SHARDS, 2, m.shape[1] // 2, m.shape[2]), pltpu.HBM)
              for m in mats]

    def body(*refs):
        send, recv = refs[n + 1], refs[n + 2]
        out, token = refs[n + 3:2 * n + 3], refs[2 * n + 3]
        x, y, c, s_me, others = _place()
        for l in range(n):
            for k in range(3):
                ox, oy = others[k]
                pltpu.make_async_remote_copy(out[l].at[s_me, c], out[l].at[s_me, c], send.at[3 * l + k],
                                             recv.at[3 * l + k], device_id=(ox, oy, c), device_id_type=MESH).start()
        token[...] = jnp.zeros_like(token)

    res = _split_call(
        body, name=name,
        out_shape=(pltpu.SemaphoreType.DMA((3 * n,)), pltpu.SemaphoreType.DMA((3 * n,)),
                   *[pltpu.HBM(h.shape, BF16) for h in halves], _token_shape()),
        in_specs=[HBM] * n + [ANY], out_specs=(SEM, SEM, *[HBM] * n, pl.BlockSpec(memory_space=pltpu.VMEM)),
        aliases={l: 2 + l for l in range(n)}, views_of=mats)(*halves, after)
    return res[0], res[1], list(res[2:2 + n]), res[2 + n]


def gather_pass_on(bufs, send_a, recv_a, after, *, name, base=0):
    n = len(bufs)

    def body(*refs):
        send_a, recv_a = refs[n], refs[n + 1]
        out = refs[n + 3:2 * n + 3]
        send_b, recv_b, token = refs[2 * n + 3:]
        x, y, c, s_me, others = _place()
        for l in range(n):
            for k in range(3):
                ox, oy = others[k]
                landed, i = out[l].at[2 * ox + oy, c], 3 * l + k
                pltpu.make_async_remote_copy(landed, landed, send_a.at[3 * base + i], recv_a.at[3 * base + i],
                                             device_id=(x, y, 1 - c), device_id_type=MESH).wait_recv()
                pltpu.make_async_remote_copy(landed, landed, send_b.at[i], recv_b.at[i],
                                             device_id=(x, y, 1 - c), device_id_type=MESH).start()
        for l in range(n):
            for k in range(3):
                mine, i = out[l].at[s_me, c], 3 * (base + l) + k
                pltpu.make_async_remote_copy(mine, mine, send_a.at[i], recv_a.at[i],
                                             device_id=(x, y, 1 - c), device_id_type=MESH).wait_send()
        token[...] = jnp.zeros_like(token)

    res = _split_call(
        body, name=name,
        out_shape=(*[pltpu.HBM(b.shape, BF16) for b in bufs], pltpu.SemaphoreType.DMA((3 * n,)),
                   pltpu.SemaphoreType.DMA((3 * n,)), _token_shape()),
        in_specs=[HBM] * n + [SEM, SEM, ANY],
        out_specs=(*[HBM] * n, SEM, SEM, pl.BlockSpec(memory_space=pltpu.VMEM)),
        aliases={l: l for l in range(n)})(*bufs, send_a, recv_a, after)
    return list(res[:n]), res[n], res[n + 1], res[n + 2]


def gather_finish(bufs, send_b, recv_b, after, shapes, *, name):
    n = len(bufs)

    def body(*refs):
        send_b, recv_b = refs[n], refs[n + 1]
        out = refs[n + 3:]
        x, y, c, _, others = _place()
        for l in range(n):
            for k in range(3):
                ox, oy = others[k]
                theirs, mine, i = out[l].at[2 * ox + oy, 1 - c], out[l].at[2 * ox + oy, c], 3 * l + k
                pltpu.make_async_remote_copy(theirs, theirs, send_b.at[i], recv_b.at[i],
                                             device_id=(x, y, 1 - c), device_id_type=MESH).wait_recv()
                pltpu.make_async_remote_copy(mine, mine, send_b.at[i], recv_b.at[i],
                                             device_id=(x, y, 1 - c), device_id_type=MESH).wait_send()

    res = _split_call(
        body, name=name, out_shape=tuple(pltpu.HBM(b.shape, BF16) for b in bufs),
        in_specs=[HBM] * n + [SEM, SEM, ANY], out_specs=tuple([HBM] * n),
        aliases={l: l for l in range(n)})(*bufs, send_b, recv_b, after)
    return [r.reshape(s) for r, s in zip(res, shapes)]


def exchange_start(srcs, dst_shapes, dst_dtype, plan, count, after, *, name):
    n, m = len(srcs), len(dst_shapes)
    given = list(srcs)
    srcs = [pltpu.with_memory_space_constraint(s, pltpu.HBM) for s in srcs]
    lands = [pltpu.with_memory_space_constraint(lax.empty(s, dst_dtype), pltpu.HBM) for s in dst_shapes]

    def body(*refs):
        send, recv = refs[n + m + 1], refs[n + m + 2]
        src, dst, token = refs[n + m + 3:2 * n + m + 3], refs[2 * n + m + 3:2 * (n + m) + 3], refs[2 * (n + m) + 3]
        for i, (s, d, dev) in enumerate(plan(_place(), src, dst)):
            pltpu.make_async_remote_copy(s, d, send.at[i], recv.at[i], device_id=dev, device_id_type=MESH).start()
        token[...] = jnp.zeros_like(token)

    res = _split_call(
        body, name=name,
        out_shape=(pltpu.SemaphoreType.DMA((count,)), pltpu.SemaphoreType.DMA((count,)),
                   *[pltpu.HBM(s.shape, s.dtype) for s in srcs], *[pltpu.HBM(s, dst_dtype) for s in dst_shapes],
                   _token_shape()),
        in_specs=[HBM] * (n + m) + [ANY],
        out_specs=(SEM, SEM, *[HBM] * (n + m), pl.BlockSpec(memory_space=pltpu.VMEM)),
        aliases={i: 2 + i for i in range(n + m)}, views_of=given)(*srcs, *lands, after)
    return (list(res[2:2 + n]), list(res[2 + n:2 + n + m]), res[0], res[1], plan), res[2 + n + m]


def exchange_finish(state, after, *, name):
    srcs, lands, send, recv, plan = state
    n, m = len(srcs), len(lands)

    def body(*refs):
        send, recv = refs[n + m], refs[n + m + 1]
        src, dst = refs[n + m + 3:2 * n + m + 3], refs[2 * n + m + 3:]
        for i, (s, d, dev) in enumerate(plan(_place(), src, dst)):
            pltpu.make_async_remote_copy(s, d, send.at[i], recv.at[i], device_id=dev, device_id_type=MESH).wait()

    res = _split_call(
        body, name=name,
        out_shape=tuple(pltpu.HBM(a.shape, a.dtype) for a in srcs + lands),
        in_specs=[HBM] * (n + m) + [SEM, SEM, ANY], out_specs=tuple([HBM] * (n + m)),
        aliases={i: i for i in range(n + m)})(*srcs, *lands, send, recv, after)
    return list(res[:n]), list(res[n:])


def pair_plan(place, src, dst):
    x, y, c, _, _ = place
    return [(s.at[:, 1 - c], d, (x, y, 1 - c)) for s, d in zip(src, dst)]


def chip_plan(place, src, dst):
    x, y, c, _, others = place
    return [(s.at[2 * ox + oy], d.at[k], (ox, oy, c)) for s, d in zip(src, dst) for k, (ox, oy) in enumerate(others)]


def pair_exchange(grads, *, name):
    n = len(grads)

    def body(*refs):
        src, got = refs[:n], refs[n:2 * n]
        send, recv = refs[2 * n:]
        x, y, c, _, _ = _place()

        def swap(l):
            return pltpu.make_async_remote_copy(src[l].at[:, 1 - c], got[l], send.at[l], recv.at[l],
                                                device_id=(x, y, 1 - c), device_id_type=MESH)

        for l in range(n):
            swap(l).start()
        for l in range(n):
            swap(l).wait()

    res = _pcall(body, name=name, out_shape=[_sds((N_SHARDS,) + g.shape[2:], F32) for g in grads],
                 in_specs=[ANY] * n, out_specs=[ANY] * n,
                 scratch_shapes=[pltpu.SemaphoreType.DMA((n,)), pltpu.SemaphoreType.DMA((n,))],
                 side_effects=True)(*grads)
    return list(res)


def add_to_wire(mine, theirs, core, *, name, tm=512):
    s, _, r, c = mine.shape
    tm = min(tm, r)

    def body(core_ref, a_ref, b_ref, o_ref):
        o_ref[...] = (a_ref[...] + b_ref[...]).astype(BF16)

    spec = pl.BlockSpec((None, tm, c), lambda i, j, cr: (i, j, 0))
    return _pcall(body, name=name, out_shape=_sds((s, r, c), BF16), grid=(s, r // tm), num_prefetch=1,
                  in_specs=[pl.BlockSpec((None, None, tm, c), lambda i, j, cr: (i, cr[0], j, 0)), spec],
                  out_specs=spec, semantics=("parallel", "parallel"))(core, mine, theirs)


def sum_chips(wire, landed, place, dest, layer, n_layers, *, name, tm=512):
    _, r, c = wire.shape
    tm = min(tm, r)

    def body(place_ref, w_ref, l_ref, *rest):
        o_ref = rest[-1]
        o_ref[...] = ((w_ref[...].astype(F32) + l_ref[0].astype(F32)) + l_ref[1].astype(F32)) + l_ref[2].astype(F32)

    in_specs = [pl.BlockSpec((None, tm, c), lambda i, pr: (pr[0], i, 0)),
                pl.BlockSpec((3, tm, c), lambda i, pr: (0, i, 0))]
    args = [place, wire, landed]
    aliases = None
    if dest is not None:
        in_specs.append(ANY)
        args.append(dest)
        aliases = {3: 0}
    return _pcall(body, name=name, out_shape=_sds((n_layers, 2, r, c), F32), grid=(r // tm,), num_prefetch=1,
                  in_specs=in_specs,
                  out_specs=pl.BlockSpec((None, None, tm, c), lambda i, pr: (layer, pr[1], i, 0)),
                  aliases=aliases, semantics=("parallel",))(*args)


def pair_share(bufs, slots, *, name):
    n = len(bufs)

    def body(*refs):
        out = refs[n:2 * n]
        send, recv = refs[2 * n:]
        x, y, c, _, _ = _place()

        def share(i, half):
            o, l = slots[i]
            return pltpu.make_async_remote_copy(out[o].at[l, half], out[o].at[l, half], send.at[i], recv.at[i],
                                                device_id=(x, y, 1 - c), device_id_type=MESH)

        for i in range(len(slots)):
            share(i, c).start()
        for i in range(len(slots)):
            share(i, 1 - c).wait_recv()
            share(i, c).wait_send()

    res = _pcall(body, name=name, out_shape=[_sds(b.shape, F32) for b in bufs], in_specs=[ANY] * n,
                 out_specs=[ANY] * n,
                 scratch_shapes=[pltpu.SemaphoreType.DMA((len(slots),)), pltpu.SemaphoreType.DMA((len(slots),))],
                 aliases={o: o for o in range(n)}, side_effects=True)(*bufs)
    return list(res)


def all_reduce_small(packed, *, name):
    n_dev, r, c = packed.shape

    def body(in_ref, out_ref, land, send, recv):
        x, y, cc, _, _ = _place()
        me = 4 * x + 2 * y + cc
        peers = [(px, py, pc) for px in range(2) for py in range(2) for pc in range(2)]

        def scatter(d):
            return pltpu.make_async_remote_copy(in_ref.at[d], land.at[me], send.at[0, d], recv.at[0, me],
                                                device_id=peers[d], device_id_type=MESH)

        def gather(d):
            return pltpu.make_async_remote_copy(out_ref.at[me], out_ref.at[me], send.at[1, d], recv.at[1, me],
                                                device_id=peers[d], device_id_type=MESH)

        for d in range(n_dev):
            @pl.when(d != me)
            def _():
                scatter(d).start()
        land[me] = in_ref[me]
        for d in range(n_dev):
            @pl.when(d != me)
            def _():
                pltpu.make_async_remote_copy(in_ref.at[d], land.at[d], send.at[0, d], recv.at[0, d],
                                             device_id=peers[d], device_id_type=MESH).wait_recv()
        total = land[0]
        for d in range(1, n_dev):
            total = total + land[d]
        out_ref[me] = total
        for d in range(n_dev):
            @pl.when(d != me)
            def _():
                gather(d).start()
        for d in range(n_dev):
            @pl.when(d != me)
            def _():
                pltpu.make_async_remote_copy(out_ref.at[d], out_ref.at[d], send.at[1, d], recv.at[1, d],
                                             device_id=peers[d], device_id_type=MESH).wait_recv()
        for d in range(n_dev):
            @pl.when(d != me)
            def _():
                scatter(d).wait_send()
                gather(d).wait_send()

    vm = pl.BlockSpec(memory_space=pltpu.VMEM)
    return _pcall(body, name=name, out_shape=_sds(packed.shape, F32), in_specs=[vm], out_specs=vm,
                  scratch_shapes=[pltpu.VMEM(packed.shape, F32), pltpu.SemaphoreType.DMA((2, n_dev)),
                                  pltpu.SemaphoreType.DMA((2, n_dev))],
                  side_effects=True)(packed)


def adamw(w, g, m, v, *, name, part=None, dest=None, tm=512):
    shape = w.shape
    cols = shape[-1]
    rows = 1
    for s in shape[:-1]:
        rows *= s
    first, count = 0, rows
    if part is not None:
        count = rows // part[1]
        first = part[0] * count
    tm = min(tm, count)
    assert count % tm == 0
    two_d = lambda a: a.reshape(rows, cols)

    def body(w_ref, g_ref, m_ref, v_ref, *rest):
        d_ref, mo_ref, vo_ref = rest[-3:]
        gv = g_ref[...]
        m_new = ADAM_B1 * m_ref[...] + (1.0 - ADAM_B1) * gv
        v_new = ADAM_B2 * v_ref[...] + (1.0 - ADAM_B2) * (gv * gv)
        m_hat = m_new / (1.0 - ADAM_B1 ** ADAM_STEP)
        v_hat = v_new / (1.0 - ADAM_B2 ** ADAM_STEP)
        d_ref[...] = -ADAM_LR * (m_hat / (jnp.sqrt(v_hat) + ADAM_EPS) + ADAM_WD * w_ref[...])
        mo_ref[...] = m_new
        vo_ref[...] = v_new

    spec = pl.BlockSpec((tm, cols), lambda i: (first // tm + i, 0))
    args = [two_d(w), two_d(g), two_d(m), two_d(v)]
    in_specs = [spec] * 4
    aliases = None
    if dest is not None:
        args += [two_d(d) for d in dest]
        in_specs = in_specs + [ANY] * 3
        aliases = {4: 0, 5: 1, 6: 2}
    outs = _pcall(body, name=name, out_shape=[_sds((rows, cols), F32)] * 3, grid=(count // tm,), in_specs=in_specs,
                  out_specs=[spec] * 3, aliases=aliases, semantics=("parallel",))(*args)
    return [o.reshape(shape) for o in outs]


WEIGHTS = ("ln_mix_a", "w_in_a", "g_v_a", "w_spatial", "b_spatial", "w_out_a", "ln_kv", "w_kv", "g_k", "ln_mix_b",
           "w_q", "g_q", "w_out_b", "ln_mlp", "w_up", "w_down", "ln_ple", "w_ple_gate", "w_ple_proj")
MATRICES = (("w_in_a", 1, True), ("w_out_a", 1, False), ("w_kv", 0, True), ("w_q", 1, False), ("w_out_b", 1, False),
            ("w_up", 2, True), ("w_down", 2, False), ("w_ple_gate", 2, False), ("w_ple_proj", 2, True))
GATHER_STAGES = ((("w_in_a", 0), ("w_out_a", 0)), (("w_up", 0),), (("w_down", 0),),
                 (("w_ple_gate", 0), ("w_ple_proj", 0), ("w_kv", 0)), (("w_q", 0), ("w_out_b", 0)),
                 (("w_up", 1), ("w_down", 1), ("w_ple_gate", 1), ("w_ple_proj", 1)))
REPLICATED = ("w_spatial", "b_spatial", "ln_kv", "g_k", "ln_mix_b", "g_q", "ln_mlp", "ln_ple")
SHARDED_VECTORS = ("ln_mix_a", "g_v_a")
SMALL_ROWS = 18


def kernel(x, p, ln_mix_a, w_in_a, g_v_a, w_spatial, b_spatial, w_out_a, ln_kv, w_kv, g_k, ln_mix_b, w_q, g_q, w_out_b, ln_mlp, w_up, w_down, ln_ple, w_ple_gate, w_ple_proj, loss_target, m_ln_mix_a, m_w_in_a, m_g_v_a, m_w_spatial, m_b_spatial, m_w_out_a, m_ln_kv, m_w_kv, m_g_k, m_ln_mix_b, m_w_q, m_g_q, m_w_out_b, m_ln_mlp, m_w_up, m_w_down, m_ln_ple, m_w_ple_gate, m_w_ple_proj, v_ln_mix_a, v_w_in_a, v_g_v_a, v_w_spatial, v_b_spatial, v_w_out_a, v_ln_kv, v_w_kv, v_g_k, v_ln_mix_b, v_w_q, v_g_q, v_w_out_b, v_ln_mlp, v_w_up, v_w_down, v_ln_ple, v_w_ple_gate, v_w_ple_proj):
    given = dict(locals())
    _PREVIOUS.clear()
    weights = {n: given[n] for n in WEIGHTS}
    shard = 2 * lax.axis_index("x") + lax.axis_index("y")
    core = lax.axis_index("c")
    shard_1 = shard.astype(jnp.int32).reshape(1)
    core_1 = core.astype(jnp.int32).reshape(1)
    place = jnp.stack([shard, core]).astype(jnp.int32)

    col_sharded = {name: cols for name, _, cols in MATRICES}
    layer_count = {name: max(layers, 1) for name, layers, _ in MATRICES}

    def cast(key, after):
        name, layer = key
        w3 = weights[name] if weights[name].ndim == 3 else weights[name][None]
        return (name, layer, col_sharded[name],
                cast_into_slot(w3, layer, shard_1, name=f"cast_{name}_{layer}", after=after))

    head = [cast(key, None) for key in GATHER_STAGES[0]]
    send_h, recv_h, flying_h, token_h = gather_start([lf[3] for lf in head], shard_1, name="gather_start_0")
    tail = [cast(key, token_h) for stage in GATHER_STAGES[1:] for key in stage]
    vec_a = gather_vectors([ln_mix_a, g_v_a], name="gather_vectors")
    send_a, recv_a, flying, token = gather_start([lf[3] for lf in tail], vec_a[0], name="gather_start_1")

    w = {"ln_mix_a": vec_a[0].reshape(1, D_MODEL),
         "g_v_a": vec_a[1].reshape(1, D_MODEL)}
    for name in REPLICATED:
        w[name] = weights[name]

    class Late:
        passed = {}

        def pass_on(self, name, layer, after):
            stage = [(name, layer) in s for s in GATHER_STAGES].index(True)
            if stage not in self.passed:
                if stage == 0:
                    base, members, sems, fly = 0, head, (send_h, recv_h), flying_h
                else:
                    base = sum(len(s) for s in GATHER_STAGES[1:stage])
                    members, sems, fly = tail[base:base + len(GATHER_STAGES[stage])], (send_a, recv_a), flying
                self.passed[stage] = (members, gather_pass_on(fly[base:base + len(members)], sems[0], sems[1], after,
                                                              name=f"gather_pass_on_{stage}", base=base))
            return stage

        def weights(self, name, layer, after):
            stage = self.pass_on(name, layer, after)
            members, (bufs, send_b, recv_b, tok) = self.passed[stage]
            got = gather_finish(bufs, send_b, recv_b, tok, [lf[3].shape for lf in members],
                                name=f"gather_finish_{stage}")
            out = {}
            for (leaf_name, leaf_layer, cols, _), arr in zip(members, got):
                out[(leaf_name, leaf_layer)] = arr if cols else arr.reshape(N_SHARDS * arr.shape[1], arr.shape[2])
            return out

        groups = []

        def pair_start(self, grads_done, after):
            self.keys = sorted(grads_done)
            views = [view(k, grads_done[k]) for k in self.keys]
            self.pair, token = exchange_start(views, [(N_SHARDS,) + v.shape[2:] for v in views], F32, pair_plan,
                                              len(views), after, name=f"grad_pair_start_{len(self.groups)}")
            return token

        def chip_start(self, after):
            tag = len(self.groups)
            mine, theirs = exchange_finish(self.pair, after, name=f"grad_pair_finish_{tag}")
            wire = [add_to_wire(a, b, core_1, name=f"grad_pair_sum_{tag}_{i}")
                    for i, (a, b) in enumerate(zip(mine, theirs))]
            chip, token = exchange_start(wire, [(3,) + v.shape[1:] for v in wire], BF16, chip_plan, 3 * len(wire),
                                         theirs[-1], name=f"grad_chip_start_{tag}")
            self.groups.append((self.keys, chip))
            return token

    def view(key, arr):
        rows = arr.shape[-2] if col_sharded[key[0]] else arr.shape[0] // N_SHARDS
        return arr.reshape(N_SHARDS, 2, rows // 2, arr.shape[-1])

    t = x.shape[1]
    late = Late()
    loss_blk, dx, g = local_step(x[0], p.reshape(2, t, PLE_DIM), loss_target[0], w, late)

    sent = {k for keys, _ in late.groups for k in keys}
    keys_last = [(name, layer) for name, layers, _ in MATRICES for layer in range(max(layers, 1))
                 if (name, layer) not in sent]
    views = [view(k, g[k[0]][k[1]] if layer_count[k[0]] == 2 else g[k[0]]) for k in keys_last]

    theirs = pair_exchange(views, name="grad_pair_exchange_last")
    wire_0 = [add_to_wire(a, b, core_1, name=f"grad_pair_sum_last_{i}") for i, (a, b) in enumerate(zip(views, theirs))]
    chip_0, token_0 = exchange_start(wire_0, [(3,) + v.shape[1:] for v in wire_0], BF16, chip_plan, 3 * len(wire_0),
                                     theirs[-1], name="grad_chip_start_last")

    grads, bufs = {}, {}

    def sum_and_share(keys, wire, landed, tag):
        for i, (key, wv, lv) in enumerate(zip(keys, wire, landed)):
            name, layer = key
            bufs[name] = sum_chips(wv, lv, place, bufs.get(name), layer, layer_count[name],
                                   name=f"grad_chip_sum_{tag}_{i}")
        names = sorted({k[0] for k in keys})
        shared = pair_share([bufs[n] for n in names], [(names.index(k[0]), k[1]) for k in keys],
                            name=f"grad_pair_share_{tag}")
        bufs.update(zip(names, shared))

    updates = {}

    def update(n, gn, part=None):
        wn, mn, vn = weights[n], given["m_" + n], given["v_" + n]
        if wn.ndim == 1:
            wn, gn, mn, vn = (a.reshape(1, -1) for a in (wn, gn, mn, vn))
        tag = "" if part is None else f"_{part[0]}"
        updates[n] = adamw(wn, gn.reshape(wn.shape), mn, vn, name=f"adamw_{n}{tag}", part=part, dest=updates.get(n))

    after = token_0
    for tag, (keys, chip) in enumerate(late.groups + [(keys_last, chip_0)]):
        wire, landed = exchange_finish(chip, after, name=f"grad_chip_finish_{tag}")
        sum_and_share(keys, wire, landed, tag)
        for name, layer in keys:
            update(name, bufs[name], (layer, layer_count[name]) if layer_count[name] == 2 else None)
        after = updates[keys[-1][0]][0]

    small = REPLICATED + SHARDED_VECTORS
    flat = jnp.concatenate([g[n].reshape(-1) for n in small] + [loss_blk[0, :1]])
    room = 8 * SMALL_ROWS * D_MODEL
    flat = jnp.concatenate([flat, jnp.zeros((room - flat.shape[0],), F32)])
    reduced = all_reduce_small(flat.reshape(8, SMALL_ROWS, D_MODEL), name="grad_small_all_reduce").reshape(-1)
    loss = reduced[sum(g[n].size for n in small)]
    at = 0
    for n in small:
        size = g[n].size
        piece = reduced[at:at + size]
        at += size
        if n in SHARDED_VECTORS:
            per = D_MODEL // N_SHARDS
            grads[n] = lax.dynamic_slice(piece, (shard * per,), (per,)).reshape(weights[n].shape)
        else:
            grads[n] = piece.reshape(weights[n].shape)
        update(n, grads[n])
    for name, _, _ in MATRICES:
        grads[name] = bufs[name].reshape(weights[name].shape)
    delta = {n: updates[n][0].reshape(weights[n].shape) for n in WEIGHTS}
    new_m = {n: updates[n][1].reshape(weights[n].shape) for n in WEIGHTS}
    new_v = {n: updates[n][2].reshape(weights[n].shape) for n in WEIGHTS}
    return (loss, dx.reshape(x.shape), *[grads[n] for n in WEIGHTS], *[delta[n] for n in WEIGHTS],
            *[new_m[n] for n in WEIGHTS], *[new_v[n] for n in WEIGHTS])
```

```python
import jax
import jax.numpy as jnp
from jax import lax
from jax.experimental import pallas as pl
from jax.experimental.pallas import tpu as pltpu

F32 = jnp.float32
BF16 = jnp.bfloat16

D_MODEL = 1024
D_FF = 4096
PLE_DIM = 256
N_GROUPS = 8
CHUNK = 128
HEAD_DIM = 64
LANES = 128
ATT_K_BLOCK = 256
ATT_Q_BLOCK = 512
EPS = 1e-6
N_SHARDS = 4
VMEM_LIMIT = 56 * 1024 * 1024

ADAM_LR = 0.001
ADAM_B1 = 0.9
ADAM_B2 = 0.999
ADAM_EPS = 1e-08
ADAM_WD = 0.01
ADAM_STEP = 10

MESH = pl.DeviceIdType.MESH


_PREVIOUS = []


def _in_order(make, in_specs, args, views_of=()):
    previous = _PREVIOUS[-1] if _PREVIOUS else None
    if previous is not None and any(a is previous for a in (*args, *views_of)):
        previous = None
    if previous is None:
        result = make(lambda body: body, list(in_specs))(*args)
    else:
        count = len(args)

        def skip(body):
            return lambda *refs: body(*refs[:count], *refs[count + 1:])

        result = make(skip, list(in_specs) + [pl.BlockSpec(memory_space=pl.ANY)])(*args, previous)
    _PREVIOUS[:] = [jax.tree_util.tree_leaves(result)[-1]]
    return result


def _pcall(body, *, name, out_shape, grid=None, in_specs=None, out_specs=None, scratch_shapes=(),
           semantics=None, aliases=None, side_effects=False, num_prefetch=0):
    params = dict(vmem_limit_bytes=VMEM_LIMIT)
    if semantics is not None:
        params["dimension_semantics"] = semantics
    if side_effects:
        params["has_side_effects"] = True
    kwargs = {}
    if aliases:
        kwargs["input_output_aliases"] = aliases

    def make(wrap, specs):
        body_ = wrap(body)
        if num_prefetch:
            spec = pltpu.PrefetchScalarGridSpec(num_scalar_prefetch=num_prefetch, grid=grid, in_specs=specs,
                                                out_specs=out_specs, scratch_shapes=list(scratch_shapes))
            return pl.pallas_call(body_, name=name, out_shape=out_shape, grid_spec=spec,
                                  compiler_params=pltpu.CompilerParams(**params), **kwargs)
        more = dict(kwargs, in_specs=specs)
        if grid is not None:
            more["grid"] = grid
        if out_specs is not None:
            more["out_specs"] = out_specs
        return pl.pallas_call(body_, name=name, out_shape=out_shape, scratch_shapes=list(scratch_shapes),
                              compiler_params=pltpu.CompilerParams(**params), **more)

    return lambda *args: _in_order(make, in_specs, args)


def _sds(shape, dtype):
    return jax.ShapeDtypeStruct(shape, dtype)


_GELU_C = 0.7978845608028654
_GELU_A = 0.044715


def _gelu(x):
    inner = _GELU_C * (x + _GELU_A * (x * x * x))
    return 0.5 * x * (1.0 + jnp.tanh(inner))


def _gelu_grad(x):
    x2 = x * x
    t = jnp.tanh(_GELU_C * (x + _GELU_A * (x2 * x)))
    return 0.5 * (1.0 + t) + 0.5 * x * (1.0 - t * t) * (_GELU_C * (1.0 + 3.0 * _GELU_A * x2))


def _sigmoid(x):
    return 1.0 / (1.0 + jnp.exp(-x))


def _log_sigmoid(z):
    return jnp.minimum(z, 0.0) - jnp.log(1.0 + jnp.exp(-jnp.abs(z)))


def _dot(a, b):
    return jnp.dot(a, b, preferred_element_type=F32)


def _dot_nt(a, b):
    return lax.dot_general(a, b, (((1,), (1,)), ((), ())), preferred_element_type=F32)


def _dot_tn(a, b):
    return lax.dot_general(a, b, (((0,), (0,)), ((), ())), preferred_element_type=F32)


def _head_rstd(x):
    lane = lax.broadcasted_iota(jnp.int32, x.shape, 1)
    low = lane < HEAD_DIM
    sq = x * x
    s_lo = jnp.sum(jnp.where(low, sq, 0.0), axis=-1, keepdims=True)
    s_hi = jnp.sum(jnp.where(low, 0.0, sq), axis=-1, keepdims=True)
    ms = jnp.where(low, s_lo, s_hi) * (1.0 / HEAD_DIM)
    return lax.rsqrt(ms + EPS)


def _head_mean(x):
    lane = lax.broadcasted_iota(jnp.int32, x.shape, 1)
    low = lane < HEAD_DIM
    s_lo = jnp.sum(jnp.where(low, x, 0.0), axis=-1, keepdims=True)
    s_hi = jnp.sum(jnp.where(low, 0.0, x), axis=-1, keepdims=True)
    return jnp.where(low, s_lo, s_hi) * (1.0 / HEAD_DIM)


def _full(shape):
    zeros = (0,) * len(shape)
    return pl.BlockSpec(shape, lambda i: zeros)


def norm_matmul(x, g, w, *, name, epilogue="none", head_gain=None, head_scale=1.0, tm=512):
    t, d = x.shape
    sharded = w.ndim == 3
    per = w.shape[2] if sharded else w.shape[1]
    n = N_SHARDS * per if sharded else per
    tm = min(tm, t)
    heads = epilogue == "heads"

    def body(x_ref, g_ref, w_ref, *rest):
        if heads:
            hg_ref, rest = rest[0], rest[1:]
        h_ref, r_ref, outs = rest[0], rest[1], rest[2:]
        xv = x_ref[...]
        r = lax.rsqrt(jnp.mean(xv * xv, axis=-1, keepdims=True) + EPS)
        h = ((xv * r) * g_ref[...]).astype(BF16)
        h_ref[...] = h
        r_ref[...] = r
        for s in range(N_SHARDS if sharded else 1):
            cols = slice(s * per, (s + 1) * per)
            y = _dot(h, w_ref[s] if sharded else w_ref[...])
            if epilogue == "relu2":
                a = jnp.maximum(y, 0.0)
                outs[0][:, cols] = a.astype(BF16)
                outs[1][:, cols] = (a * a).astype(BF16)
                continue
            outs[0][:, cols] = y
            if heads:
                gain = hg_ref[...] * head_scale
                for b in range(per // LANES):
                    at = s * per + b * LANES
                    yb = y[:, b * LANES:(b + 1) * LANES]
                    if at < D_MODEL:
                        outs[1][:, at:at + LANES] = ((yb * _head_rstd(yb)) * gain).astype(BF16)
                    else:
                        outs[2][:, at - D_MODEL:at - D_MODEL + LANES] = yb.astype(BF16)

    row = lambda i: (i, 0)
    in_specs = [pl.BlockSpec((tm, d), row), _full((1, d)), _full(w.shape)]
    args = [x, g, w]
    out_shape = [_sds((t, d), BF16), _sds((t, 1), F32)]
    out_specs = [pl.BlockSpec((tm, d), row), pl.BlockSpec((tm, 1), row)]
    if epilogue == "relu2":
        out_shape += [_sds((t, n), BF16), _sds((t, n), BF16)]
        out_specs += [pl.BlockSpec((tm, n), row)] * 2
    else:
        out_shape.append(_sds((t, n), F32))
        out_specs.append(pl.BlockSpec((tm, n), row))
    if heads:
        in_specs.append(_full((1, LANES)))
        args.append(head_gain)
        for width in [D_MODEL] + ([n - D_MODEL] if n > D_MODEL else []):
            out_shape.append(_sds((t, width), BF16))
            out_specs.append(pl.BlockSpec((tm, width), row))
    return _pcall(body, name=name, out_shape=out_shape, grid=(t // tm,), in_specs=in_specs, out_specs=out_specs,
                  semantics=("parallel",))(*args)


def matmul_residual(a, w, res, *, name, tm=512):
    t, k = a.shape
    n = w.shape[1]
    tm = min(tm, t)

    def body(a_ref, w_ref, res_ref, o_ref):
        o_ref[...] = res_ref[...] + _dot(a_ref[...], w_ref[...])

    row = lambda i: (i, 0)
    return _pcall(
        body, name=name, out_shape=_sds((t, n), F32), grid=(t // tm,),
        in_specs=[pl.BlockSpec((tm, k), row), _full(w.shape), pl.BlockSpec((tm, n), row)],
        out_specs=pl.BlockSpec((tm, n), row), semantics=("parallel",))(a, w, res)


def ple_forward(x, g, w_gate, p, w_proj, *, name, tm=256):
    t, d = x.shape
    tm = min(tm, t)

    def body(x_ref, g_ref, wg_ref, p_ref, wp_ref, h_ref, r_ref, gate_ref, pp_ref, o_ref):
        xv = x_ref[...]
        r = lax.rsqrt(jnp.mean(xv * xv, axis=-1, keepdims=True) + EPS)
        h = ((xv * r) * g_ref[...]).astype(BF16)
        h_ref[...] = h
        r_ref[...] = r
        gate = _sigmoid(_dot(h, wg_ref[...]))
        gate_ref[...] = gate
        pb = p_ref[...].astype(BF16)
        per = d // N_SHARDS
        for s in range(N_SHARDS):
            cols = slice(s * per, (s + 1) * per)
            pp = _dot(pb, wp_ref[s])
            pp_ref[:, cols] = pp.astype(BF16)
            o_ref[:, cols] = xv[:, cols] + pp * gate[:, cols]

    row = lambda i: (i, 0)
    fixed = lambda i: (0, 0)
    return _pcall(
        body, name=name,
        out_shape=[_sds((t, d), BF16), _sds((t, 1), F32), _sds((t, d), F32), _sds((t, d), BF16), _sds((t, d), F32)],
        grid=(t // tm,),
        in_specs=[pl.BlockSpec((tm, d), row), pl.BlockSpec((1, d), fixed), pl.BlockSpec((d, d), fixed),
                  pl.BlockSpec((tm, PLE_DIM), row),
                  pl.BlockSpec((N_SHARDS, PLE_DIM, d // N_SHARDS), lambda i: (0, 0, 0))],
        out_specs=[pl.BlockSpec((tm, d), row), pl.BlockSpec((tm, 1), row), pl.BlockSpec((tm, d), row),
                   pl.BlockSpec((tm, d), row), pl.BlockSpec((tm, d), row)],
        semantics=("parallel",))(x, g, w_gate, p, w_proj)


def _tril_mask():
    r = lax.broadcasted_iota(jnp.int32, (CHUNK, CHUNK), 0)
    c = lax.broadcasted_iota(jnp.int32, (CHUNK, CHUNK), 1)
    return c <= r


def _sgu_common(pre_ref, gv_ref, ws_ref):
    pre = pre_ref[...]
    pre_u, pre_v = pre[:, :D_MODEL], pre[:, D_MODEL:]
    u = _gelu(pre_u)
    v = _gelu(pre_v)
    r = lax.rsqrt(jnp.mean(v * v, axis=-1, keepdims=True) + EPS)
    vhat = v * r
    vn = (vhat * gv_ref[...]).astype(BF16)
    tril = _tril_mask()
    wm = [jnp.where(tril, ws_ref[g], 0.0).astype(BF16) for g in range(N_GROUPS)]
    return pre_u, pre_v, u, r, vhat, vn, wm, tril


def sgu_forward(pre, g_v, w_s, b_full, *, name):
    t = pre.shape[0]

    def body(pre_ref, gv_ref, ws_ref, b_ref, y_ref):
        _, _, u, _, _, vn, wm, _ = _sgu_common(pre_ref, gv_ref, ws_ref)
        for g in range(N_GROUPS):
            cols = slice(g * LANES, (g + 1) * LANES)
            mix = _dot(wm[g], vn[:, cols]) + b_ref[:, cols]
            y_ref[:, cols] = (u[:, cols] * mix).astype(BF16)

    return _pcall(
        body, name=name, out_shape=_sds((t, D_MODEL), BF16), grid=(t // CHUNK,),
        in_specs=[pl.BlockSpec((CHUNK, 2 * D_MODEL), lambda i: (i, 0)), pl.BlockSpec((1, D_MODEL), lambda i: (0, 0)),
                  pl.BlockSpec((N_GROUPS, CHUNK, CHUNK), lambda i: (0, 0, 0)),
                  pl.BlockSpec((CHUNK, D_MODEL), lambda i: (0, 0))],
        out_specs=pl.BlockSpec((CHUNK, D_MODEL), lambda i: (i, 0)),
        semantics=("parallel",))(pre, g_v, w_s, b_full)


def _suffix_matrix(n):
    r = lax.broadcasted_iota(jnp.int32, (n, n), 0)
    c = lax.broadcasted_iota(jnp.int32, (n, n), 1)
    return jnp.where(r > c, 1.0, 0.0).astype(BF16)


def _prefix_matrix(n):
    r = lax.broadcasted_iota(jnp.int32, (n, n), 0)
    c = lax.broadcasted_iota(jnp.int32, (n, n), 1)
    return jnp.where(r < c, 1.0, 0.0).astype(BF16)


def _block_cumsum(a, tri):
    return _dot(a.astype(BF16), tri)


def _stacked_causal(nq, nk, shift):
    r = lax.broadcasted_iota(jnp.int32, (2 * nq, nk), 0)
    c = lax.broadcasted_iota(jnp.int32, (2 * nq, nk), 1)
    return c + shift < jnp.where(r >= nq, r - nq, r)


def _att_blocks(t):
    bq, bk = min(ATT_Q_BLOCK, t), min(ATT_K_BLOCK, t)
    return bq, bk, bq // bk


def _stack_heads(a, low):
    zero = jnp.zeros_like(a)
    return jnp.concatenate([jnp.where(low, a, zero), jnp.where(low, zero, a)], axis=0)


def stick_breaking_forward(q, k, v, *, name):
    t = q.shape[0]
    bq, bk, ratio = _att_blocks(t)

    def body(q_ref, k_ref, v_ref, o_ref):
        i = pl.program_id(1)
        low = lax.broadcasted_iota(jnp.int32, (bq, LANES), 1) < HEAD_DIM
        tri = _suffix_matrix(bk)
        qs = _stack_heads(q_ref[...], low)

        def block(j, carry, acc, causal=None):
            rows = pl.ds(pl.multiple_of(j * bk, bk), bk)
            z = _dot_nt(qs, k_ref[rows, :])
            ls = _log_sigmoid(z)
            lg = ls - z
            if causal is not None:
                lg = jnp.where(causal, lg, 0.0)
            s = ls + _block_cumsum(lg, tri) + carry
            a = jnp.exp(s)
            if causal is not None:
                a = jnp.where(causal, a, 0.0)
            acc = acc + _dot(a.astype(BF16), v_ref[rows, :])
            return carry + jnp.sum(lg, axis=-1, keepdims=True), acc

        state = (jnp.zeros((2 * bq, 1), F32), jnp.zeros((2 * bq, LANES), F32))
        for m in reversed(range(ratio)):
            state = block(ratio * i + m, state[0], state[1], _stacked_causal(bq, bk, m * bk))
        first = ratio * i

        def two_blocks(n, st):
            st = block(first - 1 - 2 * n, st[0], st[1])
            return block(first - 2 - 2 * n, st[0], st[1])

        state = lax.fori_loop(0, first // 2, two_blocks, state)
        _, acc = lax.fori_loop(0, first % 2, lambda n, st: block(0, st[0], st[1]), state)
        o_ref[...] = jnp.where(low, acc[:bq], acc[bq:]).astype(BF16)

    return _pcall(
        body, name=name, out_shape=_sds((t, D_MODEL), BF16), grid=(D_MODEL // LANES, t // bq),
        in_specs=[pl.BlockSpec((bq, LANES), lambda p, i: (i, p)), pl.BlockSpec((t, LANES), lambda p, i: (0, p)),
                  pl.BlockSpec((t, LANES), lambda p, i: (0, p))],
        out_specs=pl.BlockSpec((bq, LANES), lambda p, i: (i, p)),
        semantics=("parallel", "arbitrary"))(q, k, v)


def loss_forward(x, target, *, name, tm=512):
    t, d = x.shape
    tm = min(tm, t)

    def body(x_ref, t_ref, l_ref, dx_ref):
        @pl.when(pl.program_id(0) == 0)
        def _():
            l_ref[...] = jnp.zeros_like(l_ref)

        diff = x_ref[...] - t_ref[...]
        dx_ref[...] = diff * (1.0 / d)
        l_ref[...] += 0.5 * jnp.sum(jnp.mean(diff * diff, axis=-1, keepdims=True))

    return _pcall(
        body, name=name, out_shape=[_sds((8, LANES), F32), _sds((t, d), F32)], grid=(t // tm,),
        in_specs=[pl.BlockSpec((tm, d), lambda i: (i, 0))] * 2,
        out_specs=[pl.BlockSpec((8, LANES), lambda i: (0, 0)), pl.BlockSpec((tm, d), lambda i: (i, 0))],
        semantics=("arbitrary",))(x, target)


def matmul_nt(dy, w, *, name, mul=None, out_dtype=F32, tm=512):
    t, n = dy.shape
    k = w.shape[0]
    tm = min(tm, t)

    def body(*refs):
        if mul is None:
            dy_ref, w_ref, o_ref = refs
        else:
            dy_ref, w_ref, m_ref, o_ref = refs
        y = _dot_nt(dy_ref[...].astype(BF16), w_ref[...])
        if mul is not None:
            y = y * (2.0 * m_ref[...].astype(F32))
        o_ref[...] = y.astype(out_dtype)

    row = lambda i: (i, 0)
    in_specs = [pl.BlockSpec((tm, n), row), _full(w.shape)]
    args = [dy, w]
    if mul is not None:
        in_specs.append(pl.BlockSpec((tm, k), row))
        args.append(mul)
    return _pcall(body, name=name, out_shape=_sds((t, k), out_dtype), grid=(t // tm,), in_specs=in_specs,
                  out_specs=pl.BlockSpec((tm, k), row), semantics=("parallel",))(*args)


def matmul_tn(a, dy, *, name, col_shards, tk=512):
    t, k = a.shape
    n = dy.shape[1]
    if col_shards:
        tn = n // N_SHARDS

        def body(a_ref, dy_ref, o_ref):
            o_ref[...] = _dot_tn(a_ref[...].astype(BF16), dy_ref[...].astype(BF16))

        return _pcall(body, name=name, out_shape=_sds((N_SHARDS, k, tn), F32), grid=(N_SHARDS,),
                      in_specs=[_full((t, k)), pl.BlockSpec((t, tn), lambda j: (0, j))],
                      out_specs=pl.BlockSpec((None, k, tn), lambda j: (j, 0, 0)), semantics=("parallel",))(a, dy)

    tk = min(tk, k)

    def body(a_ref, dy_ref, o_ref, dy_bf):
        @pl.when(pl.program_id(0) == 0)
        def _():
            dy_bf[...] = dy_ref[...].astype(BF16)

        o_ref[...] = _dot_tn(a_ref[...].astype(BF16), dy_bf[...])

    return _pcall(body, name=name, out_shape=_sds((k, n), F32), grid=(k // tk,),
                  in_specs=[pl.BlockSpec((t, tk), lambda i: (0, i)), _full((t, n))],
                  out_specs=pl.BlockSpec((tk, n), lambda i: (i, 0)),
                  scratch_shapes=[pltpu.VMEM((t, n), BF16)], semantics=("arbitrary",))(a, dy)


def norm_backward(dpre, w, x, g, rstd, dx_out, *, name, tm=512):
    t, d = x.shape
    n = dpre.shape[1]
    tm = min(tm, t)
    if w.ndim == 3:
        w_spec = pl.BlockSpec(w.shape, lambda i: (0, 0, 0))
    else:
        w_spec = pl.BlockSpec(w.shape, lambda i: (0, 0))

    def body(dp_ref, w_ref, x_ref, g_ref, r_ref, dxo_ref, dx_ref, dg_ref):
        @pl.when(pl.program_id(0) == 0)
        def _():
            dg_ref[...] = jnp.zeros_like(dg_ref)

        if w.ndim == 3:
            per = n // N_SHARDS
            dh = _dot_nt(dp_ref[:, 0:per], w_ref[0])
            for s in range(1, N_SHARDS):
                dh = dh + _dot_nt(dp_ref[:, s * per:(s + 1) * per], w_ref[s])
        else:
            dh = _dot_nt(dp_ref[...], w_ref[...])
        r = r_ref[...]
        xn = x_ref[...] * r
        dg_ref[...] += jnp.sum(dh * xn, axis=0, keepdims=True)
        dxn = dh * g_ref[...]
        dx = r * (dxn - xn * jnp.mean(dxn * xn, axis=-1, keepdims=True))
        dx_ref[...] = dxo_ref[...] + dx

    row = lambda i: (i, 0)
    fixed = lambda i: (0, 0)
    return _pcall(
        body, name=name, out_shape=[_sds((t, d), F32), _sds((1, d), F32)], grid=(t // tm,),
        in_specs=[pl.BlockSpec((tm, n), row), w_spec, pl.BlockSpec((tm, d), row),
                  pl.BlockSpec((1, d), fixed), pl.BlockSpec((tm, 1), row), pl.BlockSpec((tm, d), row)],
        out_specs=[pl.BlockSpec((tm, d), row), pl.BlockSpec((1, d), fixed)],
        semantics=("arbitrary",))(dpre, w, x, g, rstd, dx_out)


def ple_backward(dx, gate, pp, *, name, tm=512):
    t, d = dx.shape
    tm = min(tm, t)

    def body(dx_ref, gate_ref, pp_ref, dg_ref, dp_ref):
        dxv = dx_ref[...]
        gate = gate_ref[...]
        dg_ref[...] = (dxv * pp_ref[...].astype(F32) * (gate * (1.0 - gate))).astype(BF16)
        dp_ref[...] = (dxv * gate).astype(BF16)

    spec = pl.BlockSpec((tm, d), lambda i: (i, 0))
    return _pcall(body, name=name, out_shape=[_sds((t, d), BF16)] * 2, grid=(t // tm,), in_specs=[spec] * 3,
                  out_specs=[spec] * 2, semantics=("parallel",))(dx, gate, pp)


def sgu_backward(dy, pre, g_v, w_s, b_full, *, name):
    t = pre.shape[0]
    n_chunks = t // CHUNK

    def body(dy_ref, pre_ref, gv_ref, ws_ref, b_ref, dpre_ref, dws_ref, db_ref, dgv_ref, dvn_s, dbf_s):
        step = pl.program_id(0)

        @pl.when(step == 0)
        def _():
            dws_ref[...] = jnp.zeros_like(dws_ref)
            dgv_ref[...] = jnp.zeros_like(dgv_ref)
            dbf_s[...] = jnp.zeros_like(dbf_s)

        pre_u, pre_v, u, r, vhat, vn, wm, tril = _sgu_common(pre_ref, gv_ref, ws_ref)
        dyv = dy_ref[...]
        for g in range(N_GROUPS):
            cols = slice(g * LANES, (g + 1) * LANES)
            mix = _dot(wm[g], vn[:, cols]) + b_ref[:, cols]
            dmix = dyv[:, cols] * u[:, cols]
            dmix_b = dmix.astype(BF16)
            du = dyv[:, cols] * mix
            dpre_ref[:, cols] = (du * _gelu_grad(pre_u[:, cols])).astype(BF16)
            dws_ref[g] += jnp.where(tril, _dot_nt(dmix_b, vn[:, cols]), 0.0)
            dbf_s[:, cols] += dmix
            dvn_s[:, cols] = _dot_tn(wm[g], dmix_b)
        dvn = dvn_s[...]
        dgv_ref[...] += jnp.sum(dvn * vhat, axis=0, keepdims=True)
        dxn = dvn * gv_ref[...]
        dv = r * (dxn - vhat * jnp.mean(dxn * vhat, axis=-1, keepdims=True))
        dpre_ref[:, D_MODEL:] = (dv * _gelu_grad(pre_v)).astype(BF16)

        @pl.when(step == n_chunks - 1)
        def _():
            lane = lax.broadcasted_iota(jnp.int32, (CHUNK, LANES), 1)
            acc = jnp.zeros((CHUNK, LANES), F32)
            for g in range(N_GROUPS):
                s = jnp.sum(dbf_s[:, g * LANES:(g + 1) * LANES], axis=-1, keepdims=True)
                acc = jnp.where(lane == g, s, acc)
            db_ref[...] = acc

    fixed2 = lambda i: (0, 0)
    return _pcall(
        body, name=name,
        out_shape=[_sds((t, 2 * D_MODEL), BF16), _sds((N_GROUPS, CHUNK, CHUNK), F32), _sds((CHUNK, LANES), F32),
                   _sds((1, D_MODEL), F32)],
        grid=(n_chunks,),
        in_specs=[pl.BlockSpec((CHUNK, D_MODEL), lambda i: (i, 0)), pl.BlockSpec((CHUNK, 2 * D_MODEL), lambda i: (i, 0)),
                  pl.BlockSpec((1, D_MODEL), fixed2), pl.BlockSpec((N_GROUPS, CHUNK, CHUNK), lambda i: (0, 0, 0)),
                  pl.BlockSpec((CHUNK, D_MODEL), fixed2)],
        out_specs=[pl.BlockSpec((CHUNK, 2 * D_MODEL), lambda i: (i, 0)),
                   pl.BlockSpec((N_GROUPS, CHUNK, CHUNK), lambda i: (0, 0, 0)), pl.BlockSpec((CHUNK, LANES), fixed2),
                   pl.BlockSpec((1, D_MODEL), fixed2)],
        scratch_shapes=[pltpu.VMEM((CHUNK, D_MODEL), F32), pltpu.VMEM((CHUNK, D_MODEL), F32)],
        semantics=("arbitrary",))(dy, pre, g_v, w_s, b_full)


def head_norm_backward(dy, pre, g128, *, name, col_block=0, scale=1.0, passthrough=None, tm=512):
    t = dy.shape[0]
    tm = min(tm, t)
    width = 2 * D_MODEL if passthrough is not None else D_MODEL

    def body(*refs):
        if passthrough is not None:
            dy_ref, x_ref, g_ref, dv_ref, o_ref, dg_ref = refs
            o_ref[:, D_MODEL:] = dv_ref[...].astype(BF16)
        else:
            dy_ref, x_ref, g_ref, o_ref, dg_ref = refs

        @pl.when(pl.program_id(0) == 0)
        def _():
            dg_ref[...] = jnp.zeros_like(dg_ref)

        g = g_ref[...]
        dg = jnp.zeros((1, LANES), F32)
        for b in range(D_MODEL // LANES):
            cols = slice(b * LANES, (b + 1) * LANES)
            xv = x_ref[:, cols]
            r = _head_rstd(xv)
            xn = xv * r
            dyv = dy_ref[:, cols] * scale
            dg = dg + jnp.sum(dyv * xn, axis=0, keepdims=True)
            dxn = dyv * g
            o_ref[:, cols] = (r * (dxn - xn * _head_mean(dxn * xn))).astype(BF16)
        dg_ref[...] += dg

    row = lambda i: (i, 0)
    in_specs = [pl.BlockSpec((tm, D_MODEL), row), pl.BlockSpec((tm, D_MODEL), lambda i: (i, col_block)),
                pl.BlockSpec((1, LANES), lambda i: (0, 0))]
    args = [dy, pre, g128]
    if passthrough is not None:
        in_specs.append(pl.BlockSpec((tm, D_MODEL), row))
        args.append(passthrough)
    return _pcall(body, name=name, out_shape=[_sds((t, width), BF16), _sds((1, LANES), F32)], grid=(t // tm,),
                  in_specs=in_specs,
                  out_specs=[pl.BlockSpec((tm, width), row), pl.BlockSpec((1, LANES), lambda i: (0, 0))],
                  semantics=("arbitrary",))(*args)


def stick_breaking_backward(q, k, v, do, *, name):
    t = q.shape[0]
    bq, bk, ratio = _att_blocks(t)

    def body(q_ref, k_ref, v_ref, do_ref, dq_ref, dk_ref, dv_ref, s_buf, sg_buf):
        i = pl.program_id(1)

        @pl.when(i == 0)
        def _():
            dk_ref[...] = jnp.zeros_like(dk_ref)
            dv_ref[...] = jnp.zeros_like(dv_ref)

        low = lax.broadcasted_iota(jnp.int32, (bq, LANES), 1) < HEAD_DIM
        suffix = _suffix_matrix(bk)
        prefix = _prefix_matrix(bk)
        qs = _stack_heads(q_ref[...], low)
        dos = _stack_heads(do_ref[...], low)
        first = ratio * i

        def log_weights(j, carry, causal=None):
            rows = pl.ds(pl.multiple_of(j * bk, bk), bk)
            z = _dot_nt(qs, k_ref[rows, :])
            ls = _log_sigmoid(z)
            lg = ls - z
            if causal is not None:
                lg = jnp.where(causal, lg, 0.0)
            s_buf[j] = ls + _block_cumsum(lg, suffix) + carry
            sg_buf[j] = jnp.exp(ls)
            return carry + jnp.sum(lg, axis=-1, keepdims=True)

        carry = jnp.zeros((2 * bq, 1), F32)
        for m in reversed(range(ratio)):
            carry = log_weights(first + m, carry, _stacked_causal(bq, bk, m * bk))
        carry = lax.fori_loop(0, first // 2,
                              lambda n, c: log_weights(first - 2 - 2 * n, log_weights(first - 1 - 2 * n, c)), carry)
        lax.fori_loop(0, first % 2, lambda n, c: log_weights(0, c), carry)

        def grads(j, pcarry, dq_acc, causal=None):
            rows = pl.ds(pl.multiple_of(j * bk, bk), bk)
            a = jnp.exp(s_buf[j])
            if causal is not None:
                a = jnp.where(causal, a, 0.0)
            sg = sg_buf[j]
            ds = _dot_nt(dos, v_ref[rows, :]) * a
            before = _block_cumsum(ds, prefix) + pcarry
            if causal is not None:
                before = jnp.where(causal, before, 0.0)
            dz = (ds - sg * (ds + before)).astype(BF16)
            dq_acc = dq_acc + _dot(dz, k_ref[rows, :])
            dk_ref[rows, :] += _dot_tn(dz, qs)
            dv_ref[rows, :] += _dot_tn(a.astype(BF16), dos)
            return pcarry + jnp.sum(ds, axis=-1, keepdims=True), dq_acc

        def two_blocks(n, st):
            st = grads(2 * n, st[0], st[1])
            return grads(2 * n + 1, st[0], st[1])

        state = lax.fori_loop(0, first // 2, two_blocks,
                              (jnp.zeros((2 * bq, 1), F32), jnp.zeros((2 * bq, LANES), F32)))
        state = lax.fori_loop(0, first % 2, lambda n, st: grads(first - 1, st[0], st[1]), state)
        for m in range(ratio):
            state = grads(first + m, state[0], state[1], _stacked_causal(bq, bk, m * bk))
        dq_ref[...] = jnp.where(low, state[1][:bq], state[1][bq:])

    full = pl.BlockSpec((t, LANES), lambda p, i: (0, p))
    qblk = pl.BlockSpec((bq, LANES), lambda p, i: (i, p))
    return _pcall(
        body, name=name, out_shape=[_sds((t, D_MODEL), F32)] * 3, grid=(D_MODEL // LANES, t // bq),
        in_specs=[qblk, full, full, qblk], out_specs=[qblk, full, full],
        scratch_shapes=[pltpu.VMEM((t // bk, 2 * bq, bk), F32), pltpu.VMEM((t // bk, 2 * bq, bk), F32)],
        semantics=("parallel", "arbitrary"))(q, k, v, do)


def _mlp_backward(dx, saved, g, w_up, w_down, tag):
    x, h, r, a, a2 = saved
    d_w_down = matmul_tn(a2, dx, name=f"d_w_down_{tag}", col_shards=False)
    dpre = matmul_nt(dx, w_down, name=f"d_mlp_act_{tag}", mul=a, out_dtype=BF16)
    d_w_up = matmul_tn(h, dpre, name=f"d_w_up_{tag}", col_shards=True)
    dx, d_g = norm_backward(dpre, w_up, x, g, r, dx, name=f"d_mlp_norm_{tag}")
    return dx, d_w_up, d_w_down, d_g


def _ple_backward(dx, saved, p, g, w_gate, tag):
    x, h, r, gate, pp = saved
    dgate, dproj = ple_backward(dx, gate, pp, name=f"d_ple_{tag}")
    d_w_proj = matmul_tn(p, dproj, name=f"d_w_ple_proj_{tag}", col_shards=True)
    d_w_gate = matmul_tn(h, dgate, name=f"d_w_ple_gate_{tag}", col_shards=False)
    dx, d_g = norm_backward(dgate, w_gate, x, g, r, dx, name=f"d_ple_norm_{tag}")
    return dx, d_w_gate, d_w_proj, d_g


def local_step(x, p, target, w, late=None):
    row = lambda v: v.reshape(1, -1)
    g128 = lambda v: jnp.tile(v.reshape(1, HEAD_DIM), (1, 2))
    scale = HEAD_DIM ** -0.5
    b_full = jnp.repeat(jnp.transpose(w["b_spatial"][0]), LANES, axis=1)
    w_s = w["w_spatial"][0]

    mats = {}
    for name, value in w.items():
        if isinstance(value, tuple):
            mats.update({(name, layer): v for layer, v in enumerate(value)})
    if "w_kv" in w:
        mats[("w_kv", 0)] = w["w_kv"]

    def fetch(name, layer, after):
        if (name, layer) not in mats:
            mats.update(late.weights(name, layer, after))
        return mats[(name, layer)]

    def mlp_forward(x_in, layer):
        h, r, a, a2 = norm_matmul(x_in, row(w["ln_mlp"][layer]), fetch("w_up", layer, x_in), name=f"mlp_up_{layer}",
                                  epilogue="relu2")
        return matmul_residual(a2, fetch("w_down", layer, a2), x_in, name=f"mlp_down_{layer}"), (x_in, h, r, a, a2)

    def ple(x_in, layer):
        return ple_forward(x_in, row(w["ln_ple"][layer]), fetch("w_ple_gate", layer, x_in), p[layer],
                           fetch("w_ple_proj", layer, x_in), name=f"ple_{layer}")

    x0 = x
    h_a, r_a, pre_a = norm_matmul(x0, row(w["ln_mix_a"][0]), fetch("w_in_a", 0, x0), name="sgu_in")
    y_a = sgu_forward(pre_a, row(w["g_v_a"][0]), w_s, b_full, name="sgu_mix")
    x1 = matmul_residual(y_a, fetch("w_out_a", 0, y_a), x0, name="sgu_out")
    x2, mlp0 = mlp_forward(x1, 0)
    ple0 = ple(x2, 0)
    x3 = ple0[4]
    h_kv, r_kv, kv_pre, k_n, v_b = norm_matmul(x3, row(w["ln_kv"]), fetch("w_kv", 0, x3), name="kv_proj",
                                               epilogue="heads", head_gain=g128(w["g_k"]))
    h_q, r_q, q_pre, q_n = norm_matmul(x3, row(w["ln_mix_b"][0]), fetch("w_q", 0, k_n), name="q_proj",
                                       epilogue="heads", head_gain=g128(w["g_q"][0]), head_scale=scale)
    o = stick_breaking_forward(q_n, k_n, v_b, name="sb_fwd")
    if late is not None:
        late.pass_on("w_up", 1, o)
    x4 = matmul_residual(o, fetch("w_out_b", 0, o), x3, name="sb_out")
    x5, mlp1 = mlp_forward(x4, 1)
    ple1 = ple(x5, 1)
    x6 = ple1[4]
    loss_blk, dx = loss_forward(x6, target, name="loss")

    g = {}
    dx, dwg1, dwp1, dlnp1 = _ple_backward(dx, (x5,) + tuple(ple1[:4]), p[1], row(w["ln_ple"][1]),
                                          mats[("w_ple_gate", 1)], 1)
    dx, dwu1, dwd1, dlnm1 = _mlp_backward(dx, mlp1, row(w["ln_mlp"][1]), mats[("w_up", 1)], mats[("w_down", 1)], 1)
    g["w_out_b"] = matmul_tn(o, dx, name="d_w_out_b", col_shards=False)
    do = matmul_nt(dx, mats[("w_out_b", 0)], name="d_sb_out", out_dtype=BF16)
    dq_n, dk_n, dv = stick_breaking_backward(q_n, k_n, v_b, do, name="sb_bwd")
    dq_pre, dgq = head_norm_backward(dq_n, q_pre, g128(w["g_q"][0]), name="d_q_norm", scale=scale)
    dkv_pre, dgk = head_norm_backward(dk_n, kv_pre, g128(w["g_k"]), name="d_k_norm", passthrough=dv)
    g["w_q"] = matmul_tn(h_q, dq_pre, name="d_w_q", col_shards=False)
    g["w_kv"] = matmul_tn(h_kv, dkv_pre, name="d_w_kv", col_shards=True)
    dx, g["ln_mix_b"] = norm_backward(dq_pre, mats[("w_q", 0)], x3, row(w["ln_mix_b"][0]), r_q, dx, name="d_q_in")
    dx, g["ln_kv"] = norm_backward(dkv_pre, mats[("w_kv", 0)], x3, row(w["ln_kv"]), r_kv, dx, name="d_kv_in")
    g["g_q"] = dgq[:, :HEAD_DIM] + dgq[:, HEAD_DIM:]
    g["g_k"] = (dgk[:, :HEAD_DIM] + dgk[:, HEAD_DIM:]).reshape(HEAD_DIM)
    g["ln_kv"] = g["ln_kv"].reshape(D_MODEL)
    if late is not None:
        late.pair_start({("w_kv", 0): g["w_kv"], ("w_q", 0): g["w_q"], ("w_out_b", 0): g["w_out_b"],
                         ("w_up", 1): dwu1, ("w_down", 1): dwd1, ("w_ple_gate", 1): dwg1, ("w_ple_proj", 1): dwp1}, dx)
    dx, dwg0, dwp0, dlnp0 = _ple_backward(dx, (x2,) + tuple(ple0[:4]), p[0], row(w["ln_ple"][0]),
                                          mats[("w_ple_gate", 0)], 0)
    if late is not None:
        late.chip_start(dx)
    dx, dwu0, dwd0, dlnm0 = _mlp_backward(dx, mlp0, row(w["ln_mlp"][0]), mats[("w_up", 0)], mats[("w_down", 0)], 0)
    if late is not None:
        late.pair_start({("w_up", 0): dwu0, ("w_down", 0): dwd0, ("w_ple_gate", 0): dwg0, ("w_ple_proj", 0): dwp0}, dx)
    g["w_out_a"] = matmul_tn(y_a, dx, name="d_w_out_a", col_shards=False)
    dy_a = matmul_nt(dx, mats[("w_out_a", 0)], name="d_sgu_out")
    dpre_a, dws, db, g["g_v_a"] = sgu_backward(dy_a, pre_a, row(w["g_v_a"][0]), w_s, b_full, name="d_sgu_mix")
    if late is not None:
        late.chip_start(dpre_a)
    g["w_in_a"] = matmul_tn(h_a, dpre_a, name="d_w_in_a", col_shards=True)
    dx, g["ln_mix_a"] = norm_backward(dpre_a, mats[("w_in_a", 0)], x0, row(w["ln_mix_a"][0]), r_a, dx, name="d_sgu_in")
    g["w_spatial"] = dws[None]
    g["b_spatial"] = jnp.transpose(db[:, :N_GROUPS])[None]
    g["w_up"] = (dwu0, dwu1)
    g["w_down"] = (dwd0, dwd1)
    g["w_ple_gate"] = (dwg0, dwg1)
    g["w_ple_proj"] = (dwp0, dwp1)
    g["ln_mlp"] = jnp.concatenate([dlnm0, dlnm1], axis=0)
    g["ln_ple"] = jnp.concatenate([dlnp0, dlnp1], axis=0)
    return loss_blk, dx, g


ANY = pl.BlockSpec(memory_space=pl.ANY)


def _place():
    x, y, c = lax.axis_index("x"), lax.axis_index("y"), lax.axis_index("c")
    others = [(1 - x, y), (x, 1 - y), (1 - x, 1 - y)]
    return x, y, c, 2 * x + y, others


def cast_into_slot(w3, layer, slot, *, name, after=None, tm=512):
    _, r, c = w3.shape
    tm = min(tm, r)

    def body(slot_ref, w_ref, *rest):
        rest[-1][...] = w_ref[...].astype(BF16)

    in_specs = [pl.BlockSpec((None, tm, c), lambda i, s: (layer, i, 0))]
    args = [slot, w3]
    if after is not None:
        in_specs.append(ANY)
        args.append(after)
    return _pcall(body, name=name, out_shape=_sds((N_SHARDS, r, c), BF16), grid=(r // tm,), num_prefetch=1,
                  in_specs=in_specs, out_specs=pl.BlockSpec((None, tm, c), lambda i, s: (s[0], i, 0)),
                  semantics=("parallel",))(*args)


def gather_vectors(vecs, *, name):
    n = len(vecs)

    def body(*refs):
        src, out = refs[:n], refs[n:2 * n]
        send, recv, loc = refs[2 * n:]
        x, y, c, s_me, others = _place()

        def copy(l, k, slot):
            ox, oy = others[k]
            return pltpu.make_async_remote_copy(src[l], out[l].at[slot], send.at[l, k], recv.at[l, k],
                                                device_id=(ox, oy, c), device_id_type=MESH)

        for l in range(n):
            for k in range(3):
                copy(l, k, s_me).start()
        for l in range(n):
            own = pltpu.make_async_copy(src[l], out[l].at[s_me], loc)
            own.start()
            own.wait()
        for l in range(n):
            for k in range(3):
                ox, oy = others[k]
                copy(l, k, 2 * ox + oy).wait_recv()
                copy(l, k, s_me).wait_send()

    return _pcall(body, name=name, out_shape=[_sds((N_SHARDS,) + v.shape, F32) for v in vecs], in_specs=[ANY] * n,
                  out_specs=[ANY] * n,
                  scratch_shapes=[pltpu.SemaphoreType.DMA((n, 3)), pltpu.SemaphoreType.DMA((n, 3)),
                                  pltpu.SemaphoreType.DMA(())],
                  side_effects=True)(*vecs)


HBM = pl.BlockSpec(memory_space=pltpu.HBM)
SEM = pl.BlockSpec(memory_space=pltpu.SEMAPHORE)
DATAFLOW = pltpu.SideEffectType.DATAFLOW_SIDE_EFFECTING


def _split_call(body, *, name, out_shape, in_specs, out_specs, aliases, views_of=()):
    def make(wrap, specs):
        body_ = wrap(body)
        return pl.pallas_call(body_, name=name, out_shape=out_shape, in_specs=specs, out_specs=out_specs,
                              input_output_aliases=aliases,
                              compiler_params=pltpu.CompilerParams(has_side_effects=DATAFLOW))

    return lambda *args: _in_order(make, in_specs, args, views_of)


def _token_shape():
    return jax.ShapeDtypeStruct((8, LANES), F32)


def gather_start(mats, after, *, name):
    n = len(mats)
    halves = [pltpu.with_memory_space_constraint(m.reshape(N_SHARDS, 2, m.shape[1] // 2, m.shape[2]), pltpu.HBM)
              for m in mats]

    def body(*refs):
        send, recv = refs[n + 1], refs[n + 2]
        out, token = refs[n + 3:2 * n + 3], refs[2 * n + 3]
        x, y, c, s_me, others = _place()
        for l in range(n):
            for k in range(3):
                ox, oy = others[k]
                pltpu.make_async_remote_copy(out[l].at[s_me, c], out[l].at[s_me, c], send.at[3 * l + k],
                                             recv.at[3 * l + k], device_id=(ox, oy, c), device_id_type=MESH).start()
        token[...] = jnp.zeros_like(token)

    res = _split_call(
        body, name=name,
        out_shape=(pltpu.SemaphoreType.DMA((3 * n,)), pltpu.SemaphoreType.DMA((3 * n,)),
                   *[pltpu.HBM(h.shape, BF16) for h in halves], _token_shape()),
        in_specs=[HBM] * n + [ANY], out_specs=(SEM, SEM, *[HBM] * n, pl.BlockSpec(memory_space=pltpu.VMEM)),
        aliases={l: 2 + l for l in range(n)}, views_of=mats)(*halves, after)
    return res[0], res[1], list(res[2:2 + n]), res[2 + n]


def gather_pass_on(bufs, send_a, recv_a, after, *, name, base=0):
    n = len(bufs)

    def body(*refs):
        send_a, recv_a = refs[n], refs[n + 1]
        out = refs[n + 3:2 * n + 3]
        send_b, recv_b, token = refs[2 * n + 3:]
        x, y, c, s_me, others = _place()
        for l in range(n):
            for k in range(3):
                ox, oy = others[k]
                landed, i = out[l].at[2 * ox + oy, c], 3 * l + k
                pltpu.make_async_remote_copy(landed, landed, send_a.at[3 * base + i], recv_a.at[3 * base + i],
                                             device_id=(x, y, 1 - c), device_id_type=MESH).wait_recv()
                pltpu.make_async_remote_copy(landed, landed, send_b.at[i], recv_b.at[i],
                                             device_id=(x, y, 1 - c), device_id_type=MESH).start()
        for l in range(n):
            for k in range(3):
                mine, i = out[l].at[s_me, c], 3 * (base + l) + k
                pltpu.make_async_remote_copy(mine, mine, send_a.at[i], recv_a.at[i],
                                             device_id=(x, y, 1 - c), device_id_type=MESH).wait_send()
        token[...] = jnp.zeros_like(token)

    res = _split_call(
        body, name=name,
        out_shape=(*[pltpu.HBM(b.shape, BF16) for b in bufs], pltpu.SemaphoreType.DMA((3 * n,)),
                   pltpu.SemaphoreType.DMA((3 * n,)), _token_shape()),
        in_specs=[HBM] * n + [SEM, SEM, ANY],
        out_specs=(*[HBM] * n, SEM, SEM, pl.BlockSpec(memory_space=pltpu.VMEM)),
        aliases={l: l for l in range(n)})(*bufs, send_a, recv_a, after)
    return list(res[:n]), res[n], res[n + 1], res[n + 2]


def gather_finish(bufs, send_b, recv_b, after, shapes, *, name):
    n = len(bufs)

    def body(*refs):
        send_b, recv_b = refs[n], refs[n + 1]
        out = refs[n + 3:]
        x, y, c, _, others = _place()
        for l in range(n):
            for k in range(3):
                ox, oy = others[k]
                theirs, mine, i = out[l].at[2 * ox + oy, 1 - c], out[l].at[2 * ox + oy, c], 3 * l + k
                pltpu.make_async_remote_copy(theirs, theirs, send_b.at[i], recv_b.at[i],
                                             device_id=(x, y, 1 - c), device_id_type=MESH).wait_recv()
                pltpu.make_async_remote_copy(mine, mine, send_b.at[i], recv_b.at[i],
                                             device_id=(x, y, 1 - c), device_id_type=MESH).wait_send()

    res = _split_call(
        body, name=name, out_shape=tuple(pltpu.HBM(b.shape, BF16) for b in bufs),
        in_specs=[HBM] * n + [SEM, SEM, ANY], out_specs=tuple([HBM] * n),
        aliases={l: l for l in range(n)})(*bufs, send_b, recv_b, after)
    return [r.reshape(s) for r, s in zip(res, shapes)]


def exchange_start(srcs, dst_shapes, dst_dtype, plan, count, after, *, name):
    n, m = len(srcs), len(dst_shapes)
    given = list(srcs)
    srcs = [pltpu.with_memory_space_constraint(s, pltpu.HBM) for s in srcs]
    lands = [pltpu.with_memory_space_constraint(lax.empty(s, dst_dtype), pltpu.HBM) for s in dst_shapes]

    def body(*refs):
        send, recv = refs[n + m + 1], refs[n + m + 2]
        src, dst, token = refs[n + m + 3:2 * n + m + 3], refs[2 * n + m + 3:2 * (n + m) + 3], refs[2 * (n + m) + 3]
        for i, (s, d, dev) in enumerate(plan(_place(), src, dst)):
            pltpu.make_async_remote_copy(s, d, send.at[i], recv.at[i], device_id=dev, device_id_type=MESH).start()
        token[...] = jnp.zeros_like(token)

    res = _split_call(
        body, name=name,
        out_shape=(pltpu.SemaphoreType.DMA((count,)), pltpu.SemaphoreType.DMA((count,)),
                   *[pltpu.HBM(s.shape, s.dtype) for s in srcs], *[pltpu.HBM(s, dst_dtype) for s in dst_shapes],
                   _token_shape()),
        in_specs=[HBM] * (n + m) + [ANY],
        out_specs=(SEM, SEM, *[HBM] * (n + m), pl.BlockSpec(memory_space=pltpu.VMEM)),
        aliases={i: 2 + i for i in range(n + m)}, views_of=given)(*srcs, *lands, after)
    return (list(res[2:2 + n]), list(res[2 + n:2 + n + m]), res[0], res[1], plan), res[2 + n + m]


def exchange_finish(state, after, *, name):
    srcs, lands, send, recv, plan = state
    n, m = len(srcs), len(lands)

    def body(*refs):
        send, recv = refs[n + m], refs[n + m + 1]
        src, dst = refs[n + m + 3:2 * n + m + 3], refs[2 * n + m + 3:]
        for i, (s, d, dev) in enumerate(plan(_place(), src, dst)):
            pltpu.make_async_remote_copy(s, d, send.at[i], recv.at[i], device_id=dev, device_id_type=MESH).wait()

    res = _split_call(
        body, name=name,
        out_shape=tuple(pltpu.HBM(a.shape, a.dtype) for a in srcs + lands),
        in_specs=[HBM] * (n + m) + [SEM, SEM, ANY], out_specs=tuple([HBM] * (n + m)),
        aliases={i: i for i in range(n + m)})(*srcs, *lands, send, recv, after)
    return list(res[:n]), list(res[n:])


def pair_plan(place, src, dst):
    x, y, c, _, _ = place
    return [(s.at[:, 1 - c], d, (x, y, 1 - c)) for s, d in zip(src, dst)]


def chip_plan(place, src, dst):
    x, y, c, _, others = place
    return [(s.at[2 * ox + oy], d.at[k], (ox, oy, c)) for s, d in zip(src, dst) for k, (ox, oy) in enumerate(others)]


def pair_exchange(grads, *, name):
    n = len(grads)

    def body(*refs):
        src, got = refs[:n], refs[n:2 * n]
        send, recv = refs[2 * n:]
        x, y, c, _, _ = _place()

        def swap(l):
            return pltpu.make_async_remote_copy(src[l].at[:, 1 - c], got[l], send.at[l], recv.at[l],
                                                device_id=(x, y, 1 - c), device_id_type=MESH)

        for l in range(n):
            swap(l).start()
        for l in range(n):
            swap(l).wait()

    res = _pcall(body, name=name, out_shape=[_sds((N_SHARDS,) + g.shape[2:], F32) for g in grads],
                 in_specs=[ANY] * n, out_specs=[ANY] * n,
                 scratch_shapes=[pltpu.SemaphoreType.DMA((n,)), pltpu.SemaphoreType.DMA((n,))],
                 side_effects=True)(*grads)
    return list(res)


def add_to_wire(mine, theirs, core, *, name, tm=512):
    s, _, r, c = mine.shape
    tm = min(tm, r)

    def body(core_ref, a_ref, b_ref, o_ref):
        o_ref[...] = (a_ref[...] + b_ref[...]).astype(BF16)

    spec = pl.BlockSpec((None, tm, c), lambda i, j, cr: (i, j, 0))
    return _pcall(body, name=name, out_shape=_sds((s, r, c), BF16), grid=(s, r // tm), num_prefetch=1,
                  in_specs=[pl.BlockSpec((None, None, tm, c), lambda i, j, cr: (i, cr[0], j, 0)), spec],
                  out_specs=spec, semantics=("parallel", "parallel"))(core, mine, theirs)


def sum_chips(wire, landed, place, dest, layer, n_layers, *, name, tm=512):
    _, r, c = wire.shape
    tm = min(tm, r)

    def body(place_ref, w_ref, l_ref, *rest):
        o_ref = rest[-1]
        o_ref[...] = ((w_ref[...].astype(F32) + l_ref[0].astype(F32)) + l_ref[1].astype(F32)) + l_ref[2].astype(F32)

    in_specs = [pl.BlockSpec((None, tm, c), lambda i, pr: (pr[0], i, 0)),
                pl.BlockSpec((3, tm, c), lambda i, pr: (0, i, 0))]
    args = [place, wire, landed]
    aliases = None
    if dest is not None:
        in_specs.append(ANY)
        args.append(dest)
        aliases = {3: 0}
    return _pcall(body, name=name, out_shape=_sds((n_layers, 2, r, c), F32), grid=(r // tm,), num_prefetch=1,
                  in_specs=in_specs,
                  out_specs=pl.BlockSpec((None, None, tm, c), lambda i, pr: (layer, pr[1], i, 0)),
                  aliases=aliases, semantics=("parallel",))(*args)


def pair_share(bufs, slots, *, name):
    n = len(bufs)

    def body(*refs):
        out = refs[n:2 * n]
        send, recv = refs[2 * n:]
        x, y, c, _, _ = _place()

        def share(i, half):
            o, l = slots[i]
            return pltpu.make_async_remote_copy(out[o].at[l, half], out[o].at[l, half], send.at[i], recv.at[i],
                                                device_id=(x, y, 1 - c), device_id_type=MESH)

        for i in range(len(slots)):
            share(i, c).start()
        for i in range(len(slots)):
            share(i, 1 - c).wait_recv()
            share(i, c).wait_send()

    res = _pcall(body, name=name, out_shape=[_sds(b.shape, F32) for b in bufs], in_specs=[ANY] * n,
                 out_specs=[ANY] * n,
                 scratch_shapes=[pltpu.SemaphoreType.DMA((len(slots),)), pltpu.SemaphoreType.DMA((len(slots),))],
                 aliases={o: o for o in range(n)}, side_effects=True)(*bufs)
    return list(res)


def all_reduce_small(packed, *, name):
    n_dev, r, c = packed.shape

    def body(in_ref, out_ref, land, send, recv):
        x, y, cc, _, _ = _place()
        me = 4 * x + 2 * y + cc
        peers = [(px, py, pc) for px in range(2) for py in range(2) for pc in range(2)]

        def scatter(d):
            return pltpu.make_async_remote_copy(in_ref.at[d], land.at[me], send.at[0, d], recv.at[0, me],
                                                device_id=peers[d], device_id_type=MESH)

        def gather(d):
            return pltpu.make_async_remote_copy(out_ref.at[me], out_ref.at[me], send.at[1, d], recv.at[1, me],
                                                device_id=peers[d], device_id_type=MESH)

        for d in range(n_dev):
            @pl.when(d != me)
            def _():
                scatter(d).start()
        land[me] = in_ref[me]
        for d in range(n_dev):
            @pl.when(d != me)
            def _():
                pltpu.make_async_remote_copy(in_ref.at[d], land.at[d], send.at[0, d], recv.at[0, d],
                                             device_id=peers[d], device_id_type=MESH).wait_recv()
        total = land[0]
        for d in range(1, n_dev):
            total = total + land[d]
        out_ref[me] = total
        for d in range(n_dev):
            @pl.when(d != me)
            def _():
                gather(d).start()
        for d in range(n_dev):
            @pl.when(d != me)
            def _():
                pltpu.make_async_remote_copy(out_ref.at[d], out_ref.at[d], send.at[1, d], recv.at[1, d],
                                             device_id=peers[d], device_id_type=MESH).wait_recv()
        for d in range(n_dev):
            @pl.when(d != me)
            def _():
                scatter(d).wait_send()
                gather(d).wait_send()

    vm = pl.BlockSpec(memory_space=pltpu.VMEM)
    return _pcall(body, name=name, out_shape=_sds(packed.shape, F32), in_specs=[vm], out_specs=vm,
                  scratch_shapes=[pltpu.VMEM(packed.shape, F32), pltpu.SemaphoreType.DMA((2, n_dev)),
                                  pltpu.SemaphoreType.DMA((2, n_dev))],
                  side_effects=True)(packed)


def adamw(w, g, m, v, *, name, part=None, dest=None, tm=512):
    shape = w.shape
    cols = shape[-1]
    rows = 1
    for s in shape[:-1]:
        rows *= s
    first, count = 0, rows
    if part is not None:
        count = rows // part[1]
        first = part[0] * count
    tm = min(tm, count)
    assert count % tm == 0
    two_d = lambda a: a.reshape(rows, cols)

    def body(w_ref, g_ref, m_ref, v_ref, *rest):
        d_ref, mo_ref, vo_ref = rest[-3:]
        gv = g_ref[...]
        m_new = ADAM_B1 * m_ref[...] + (1.0 - ADAM_B1) * gv
        v_new = ADAM_B2 * v_ref[...] + (1.0 - ADAM_B2) * (gv * gv)
        m_hat = m_new / (1.0 - ADAM_B1 ** ADAM_STEP)
        v_hat = v_new / (1.0 - ADAM_B2 ** ADAM_STEP)
        d_ref[...] = -ADAM_LR * (m_hat / (jnp.sqrt(v_hat) + ADAM_EPS) + ADAM_WD * w_ref[...])
        mo_ref[...] = m_new
        vo_ref[...] = v_new

    spec = pl.BlockSpec((tm, cols), lambda i: (first // tm + i, 0))
    args = [two_d(w), two_d(g), two_d(m), two_d(v)]
    in_specs = [spec] * 4
    aliases = None
    if dest is not None:
        args += [two_d(d) for d in dest]
        in_specs = in_specs + [ANY] * 3
        aliases = {4: 0, 5: 1, 6: 2}
    outs = _pcall(body, name=name, out_shape=[_sds((rows, cols), F32)] * 3, grid=(count // tm,), in_specs=in_specs,
                  out_specs=[spec] * 3, aliases=aliases, semantics=("parallel",))(*args)
    return [o.reshape(shape) for o in outs]


WEIGHTS = ("ln_mix_a", "w_in_a", "g_v_a", "w_spatial", "b_spatial", "w_out_a", "ln_kv", "w_kv", "g_k", "ln_mix_b",
           "w_q", "g_q", "w_out_b", "ln_mlp", "w_up", "w_down", "ln_ple", "w_ple_gate", "w_ple_proj")
MATRICES = (("w_in_a", 1, True), ("w_out_a", 1, False), ("w_kv", 0, True), ("w_q", 1, False), ("w_out_b", 1, False),
            ("w_up", 2, True), ("w_down", 2, False), ("w_ple_gate", 2, False), ("w_ple_proj", 2, True))
GATHER_STAGES = ((("w_in_a", 0), ("w_out_a", 0)), (("w_up", 0),), (("w_down", 0),),
                 (("w_ple_gate", 0), ("w_ple_proj", 0), ("w_kv", 0)), (("w_q", 0), ("w_out_b", 0)),
                 (("w_up", 1), ("w_down", 1), ("w_ple_gate", 1), ("w_ple_proj", 1)))
REPLICATED = ("w_spatial", "b_spatial", "ln_kv", "g_k", "ln_mix_b", "g_q", "ln_mlp", "ln_ple")
SHARDED_VECTORS = ("ln_mix_a", "g_v_a")
SMALL_ROWS = 18


def kernel(x, p, ln_mix_a, w_in_a, g_v_a, w_spatial, b_spatial, w_out_a, ln_kv, w_kv, g_k, ln_mix_b, w_q, g_q, w_out_b, ln_mlp, w_up, w_down, ln_ple, w_ple_gate, w_ple_proj, loss_target, m_ln_mix_a, m_w_in_a, m_g_v_a, m_w_spatial, m_b_spatial, m_w_out_a, m_ln_kv, m_w_kv, m_g_k, m_ln_mix_b, m_w_q, m_g_q, m_w_out_b, m_ln_mlp, m_w_up, m_w_down, m_ln_ple, m_w_ple_gate, m_w_ple_proj, v_ln_mix_a, v_w_in_a, v_g_v_a, v_w_spatial, v_b_spatial, v_w_out_a, v_ln_kv, v_w_kv, v_g_k, v_ln_mix_b, v_w_q, v_g_q, v_w_out_b, v_ln_mlp, v_w_up, v_w_down, v_ln_ple, v_w_ple_gate, v_w_ple_proj):
    given = dict(locals())
    _PREVIOUS.clear()
    weights = {n: given[n] for n in WEIGHTS}
    shard = 2 * lax.axis_index("x") + lax.axis_index("y")
    core = lax.axis_index("c")
    shard_1 = shard.astype(jnp.int32).reshape(1)
    core_1 = core.astype(jnp.int32).reshape(1)
    place = jnp.stack([shard, core]).astype(jnp.int32)

    col_sharded = {name: cols for name, _, cols in MATRICES}
    layer_count = {name: max(layers, 1) for name, layers, _ in MATRICES}

    def cast(key, after):
        name, layer = key
        w3 = weights[name] if weights[name].ndim == 3 else weights[name][None]
        return (name, layer, col_sharded[name],
                cast_into_slot(w3, layer, shard_1, name=f"cast_{name}_{layer}", after=after))

    head = [cast(key, None) for key in GATHER_STAGES[0]]
    send_h, recv_h, flying_h, token_h = gather_start([lf[3] for lf in head], shard_1, name="gather_start_0")
    tail = [cast(key, token_h) for stage in GATHER_STAGES[1:] for key in stage]
    vec_a = gather_vectors([ln_mix_a, g_v_a], name="gather_vectors")
    send_a, recv_a, flying, token = gather_start([lf[3] for lf in tail], vec_a[0], name="gather_start_1")

    w = {"ln_mix_a": vec_a[0].reshape(1, D_MODEL),
         "g_v_a": vec_a[1].reshape(1, D_MODEL)}
    for name in REPLICATED:
        w[name] = weights[name]

    class Late:
        passed = {}

        def pass_on(self, name, layer, after):
            stage = [(name, layer) in s for s in GATHER_STAGES].index(True)
            if stage not in self.passed:
                if stage == 0:
                    base, members, sems, fly = 0, head, (send_h, recv_h), flying_h
                else:
                    base = sum(len(s) for s in GATHER_STAGES[1:stage])
                    members, sems, fly = tail[base:base + len(GATHER_STAGES[stage])], (send_a, recv_a), flying
                self.passed[stage] = (members, gather_pass_on(fly[base:base + len(members)], sems[0], sems[1], after,
                                                              name=f"gather_pass_on_{stage}", base=base))
            return stage

        def weights(self, name, layer, after):
            stage = self.pass_on(name, layer, after)
            members, (bufs, send_b, recv_b, tok) = self.passed[stage]
            got = gather_finish(bufs, send_b, recv_b, tok, [lf[3].shape for lf in members],
                                name=f"gather_finish_{stage}")
            out = {}
            for (leaf_name, leaf_layer, cols, _), arr in zip(members, got):
                out[(leaf_name, leaf_layer)] = arr if cols else arr.reshape(N_SHARDS * arr.shape[1], arr.shape[2])
            return out

        groups = []

        def pair_start(self, grads_done, after):
            self.keys = sorted(grads_done)
            views = [view(k, grads_done[k]) for k in self.keys]
            self.pair, token = exchange_start(views, [(N_SHARDS,) + v.shape[2:] for v in views], F32, pair_plan,
                                              len(views), after, name=f"grad_pair_start_{len(self.groups)}")
            return token

        def chip_start(self, after):
            tag = len(self.groups)
            mine, theirs = exchange_finish(self.pair, after, name=f"grad_pair_finish_{tag}")
            wire = [add_to_wire(a, b, core_1, name=f"grad_pair_sum_{tag}_{i}")
                    for i, (a, b) in enumerate(zip(mine, theirs))]
            chip, token = exchange_start(wire, [(3,) + v.shape[1:] for v in wire], BF16, chip_plan, 3 * len(wire),
                                         theirs[-1], name=f"grad_chip_start_{tag}")
            self.groups.append((self.keys, chip))
            return token

    def view(key, arr):
        rows = arr.shape[-2] if col_sharded[key[0]] else arr.shape[0] // N_SHARDS
        return arr.reshape(N_SHARDS, 2, rows // 2, arr.shape[-1])

    t = x.shape[1]
    late = Late()
    loss_blk, dx, g = local_step(x[0], p.reshape(2, t, PLE_DIM), loss_target[0], w, late)

    sent = {k for keys, _ in late.groups for k in keys}
    keys_last = [(name, layer) for name, layers, _ in MATRICES for layer in range(max(layers, 1))
                 if (name, layer) not in sent]
    views = [view(k, g[k[0]][k[1]] if layer_count[k[0]] == 2 else g[k[0]]) for k in keys_last]

    theirs = pair_exchange(views, name="grad_pair_exchange_last")
    wire_0 = [add_to_wire(a, b, core_1, name=f"grad_pair_sum_last_{i}") for i, (a, b) in enumerate(zip(views, theirs))]
    chip_0, token_0 = exchange_start(wire_0, [(3,) + v.shape[1:] for v in wire_0], BF16, chip_plan, 3 * len(wire_0),
                                     theirs[-1], name="grad_chip_start_last")

    grads, bufs = {}, {}

    def sum_and_share(keys, wire, landed, tag):
        for i, (key, wv, lv) in enumerate(zip(keys, wire, landed)):
            name, layer = key
            bufs[name] = sum_chips(wv, lv, place, bufs.get(name), layer, layer_count[name],
                                   name=f"grad_chip_sum_{tag}_{i}")
        names = sorted({k[0] for k in keys})
        shared = pair_share([bufs[n] for n in names], [(names.index(k[0]), k[1]) for k in keys],
                            name=f"grad_pair_share_{tag}")
        bufs.update(zip(names, shared))

    updates = {}

    def update(n, gn, part=None):
        wn, mn, vn = weights[n], given["m_" + n], given["v_" + n]
        if wn.ndim == 1:
            wn, gn, mn, vn = (a.reshape(1, -1) for a in (wn, gn, mn, vn))
        tag = "" if part is None else f"_{part[0]}"
        updates[n] = adamw(wn, gn.reshape(wn.shape), mn, vn, name=f"adamw_{n}{tag}", part=part, dest=updates.get(n))

    after = token_0
    for tag, (keys, chip) in enumerate(late.groups + [(keys_last, chip_0)]):
        wire, landed = exchange_finish(chip, after, name=f"grad_chip_finish_{tag}")
        sum_and_share(keys, wire, landed, tag)
        for name, layer in keys:
            update(name, bufs[name], (layer, layer_count[name]) if layer_count[name] == 2 else None)
        after = updates[keys[-1][0]][0]

    small = REPLICATED + SHARDED_VECTORS
    flat = jnp.concatenate([g[n].reshape(-1) for n in small] + [loss_blk[0, :1]])
    room = 8 * SMALL_ROWS * D_MODEL
    flat = jnp.concatenate([flat, jnp.zeros((room - flat.shape[0],), F32)])
    reduced = all_reduce_small(flat.reshape(8, SMALL_ROWS, D_MODEL), name="grad_small_all_reduce").reshape(-1)
    loss = reduced[sum(g[n].size for n in small)]
    at = 0
    for n in small:
        size = g[n].size
        piece = reduced[at:at + size]
        at += size
        if n in SHARDED_VECTORS:
            per = D_MODEL // N_SHARDS
            grads[n] = lax.dynamic_slice(piece, (shard * per,), (per,)).reshape(weights[n].shape)
        else:
            grads[n] = piece.reshape(weights[n].shape)
        update(n, grads[n])
    for name, _, _ in MATRICES:
        grads[name] = bufs[name].reshape(weights[name].shape)
    delta = {n: updates[n][0].reshape(weights[n].shape) for n in WEIGHTS}
    new_m = {n: updates[n][1].reshape(weights[n].shape) for n in WEIGHTS}
    new_v = {n: updates[n][2].reshape(weights[n].shape) for n in WEIGHTS}
    return (loss, dx.reshape(x.shape), *[grads[n] for n in WEIGHTS], *[delta[n] for n in WEIGHTS],
            *[new_m[n] for n in WEIGHTS], *[new_v[n] for n in WEIGHTS])
```

```python
import jax
import jax.numpy as jnp
from jax import lax
from jax.experimental import pallas as pl
from jax.experimental.pallas import tpu as pltpu

F32 = jnp.float32
BF16 = jnp.bfloat16

D_MODEL = 1024
D_FF = 4096
PLE_DIM = 256
N_GROUPS = 8
CHUNK = 128
HEAD_DIM = 64
LANES = 128
ATT_K_BLOCK = 256
ATT_Q_BLOCK = 512
EPS = 1e-6
N_SHARDS = 4
VMEM_LIMIT = 56 * 1024 * 1024

ADAM_LR = 0.001
ADAM_B1 = 0.9
ADAM_B2 = 0.999
ADAM_EPS = 1e-08
ADAM_WD = 0.01
ADAM_STEP = 10

MESH = pl.DeviceIdType.MESH


_PREVIOUS = []


def _in_order(make, in_specs, args, views_of=()):
    previous = _PREVIOUS[-1] if _PREVIOUS else None
    if previous is not None and any(a is previous for a in (*args, *views_of)):
        previous = None
    if previous is None:
        result = make(lambda body: body, list(in_specs))(*args)
    else:
        count = len(args)

        def skip(body):
            return lambda *refs: body(*refs[:count], *refs[count + 1:])

        result = make(skip, list(in_specs) + [pl.BlockSpec(memory_space=pl.ANY)])(*args, previous)
    _PREVIOUS[:] = [jax.tree_util.tree_leaves(result)[-1]]
    return result


def _pcall(body, *, name, out_shape, grid=None, in_specs=None, out_specs=None, scratch_shapes=(),
           semantics=None, aliases=None, side_effects=False, num_prefetch=0):
    params = dict(vmem_limit_bytes=VMEM_LIMIT)
    if semantics is not None:
        params["dimension_semantics"] = semantics
    if side_effects:
        params["has_side_effects"] = True
    kwargs = {}
    if aliases:
        kwargs["input_output_aliases"] = aliases

    def make(wrap, specs):
        body_ = wrap(body)
        if num_prefetch:
            spec = pltpu.PrefetchScalarGridSpec(num_scalar_prefetch=num_prefetch, grid=grid, in_specs=specs,
                                                out_specs=out_specs, scratch_shapes=list(scratch_shapes))
            return pl.pallas_call(body_, name=name, out_shape=out_shape, grid_spec=spec,
                                  compiler_params=pltpu.CompilerParams(**params), **kwargs)
        more = dict(kwargs, in_specs=specs)
        if grid is not None:
            more["grid"] = grid
        if out_specs is not None:
            more["out_specs"] = out_specs
        return pl.pallas_call(body_, name=name, out_shape=out_shape, scratch_shapes=list(scratch_shapes),
                              compiler_params=pltpu.CompilerParams(**params), **more)

    return lambda *args: _in_order(make, in_specs, args)


def _sds(shape, dtype):
    return jax.ShapeDtypeStruct(shape, dtype)


_GELU_C = 0.7978845608028654
_GELU_A = 0.044715


def _gelu(x):
    inner = _GELU_C * (x + _GELU_A * (x * x * x))
    return 0.5 * x * (1.0 + jnp.tanh(inner))


def _gelu_grad(x):
    x2 = x * x
    t = jnp.tanh(_GELU_C * (x + _GELU_A * (x2 * x)))
    return 0.5 * (1.0 + t) + 0.5 * x * (1.0 - t * t) * (_GELU_C * (1.0 + 3.0 * _GELU_A * x2))


def _sigmoid(x):
    return 1.0 / (1.0 + jnp.exp(-x))


def _log_sigmoid(z):
    return jnp.minimum(z, 0.0) - jnp.log(1.0 + jnp.exp(-jnp.abs(z)))


def _dot(a, b):
    return jnp.dot(a, b, preferred_element_type=F32)


def _dot_nt(a, b):
    return lax.dot_general(a, b, (((1,), (1,)), ((), ())), preferred_element_type=F32)


def _dot_tn(a, b):
    return lax.dot_general(a, b, (((0,), (0,)), ((), ())), preferred_element_type=F32)


def _head_rstd(x):
    lane = lax.broadcasted_iota(jnp.int32, x.shape, 1)
    low = lane < HEAD_DIM
    sq = x * x
    s_lo = jnp.sum(jnp.where(low, sq, 0.0), axis=-1, keepdims=True)
    s_hi = jnp.sum(jnp.where(low, 0.0, sq), axis=-1, keepdims=True)
    ms = jnp.where(low, s_lo, s_hi) * (1.0 / HEAD_DIM)
    return lax.rsqrt(ms + EPS)


def _head_mean(x):
    lane = lax.broadcasted_iota(jnp.int32, x.shape, 1)
    low = lane < HEAD_DIM
    s_lo = jnp.sum(jnp.where(low, x, 0.0), axis=-1, keepdims=True)
    s_hi = jnp.sum(jnp.where(low, 0.0, x), axis=-1, keepdims=True)
    return jnp.where(low, s_lo, s_hi) * (1.0 / HEAD_DIM)


def _full(shape):
    zeros = (0,) * len(shape)
    return pl.BlockSpec(shape, lambda i: zeros)


def norm_matmul(x, g, w, *, name, epilogue="none", head_gain=None, head_scale=1.0, tm=512):
    t, d = x.shape
    sharded = w.ndim == 3
    per = w.shape[2] if sharded else w.shape[1]
    n = N_SHARDS * per if sharded else per
    tm = min(tm, t)
    heads = epilogue == "heads"

    def body(x_ref, g_ref, w_ref, *rest):
        if heads:
            hg_ref, rest = rest[0], rest[1:]
        h_ref, r_ref, outs = rest[0], rest[1], rest[2:]
        xv = x_ref[...]
        r = lax.rsqrt(jnp.mean(xv * xv, axis=-1, keepdims=True) + EPS)
        h = ((xv * r) * g_ref[...]).astype(BF16)
        h_ref[...] = h
        r_ref[...] = r
        for s in range(N_SHARDS if sharded else 1):
            cols = slice(s * per, (s + 1) * per)
            y = _dot(h, w_ref[s] if sharded else w_ref[...])
            if epilogue == "relu2":
                a = jnp.maximum(y, 0.0)
                outs[0][:, cols] = a.astype(BF16)
                outs[1][:, cols] = (a * a).astype(BF16)
                continue
            outs[0][:, cols] = y
            if heads:
                gain = hg_ref[...] * head_scale
                for b in range(per // LANES):
                    at = s * per + b * LANES
                    yb = y[:, b * LANES:(b + 1) * LANES]
                    if at < D_MODEL:
                        outs[1][:, at:at + LANES] = ((yb * _head_rstd(yb)) * gain).astype(BF16)
                    else:
                        outs[2][:, at - D_MODEL:at - D_MODEL + LANES] = yb.astype(BF16)

    row = lambda i: (i, 0)
    in_specs = [pl.BlockSpec((tm, d), row), _full((1, d)), _full(w.shape)]
    args = [x, g, w]
    out_shape = [_sds((t, d), BF16), _sds((t, 1), F32)]
    out_specs = [pl.BlockSpec((tm, d), row), pl.BlockSpec((tm, 1), row)]
    if epilogue == "relu2":
        out_shape += [_sds((t, n), BF16), _sds((t, n), BF16)]
        out_specs += [pl.BlockSpec((tm, n), row)] * 2
    else:
        out_shape.append(_sds((t, n), F32))
        out_specs.append(pl.BlockSpec((tm, n), row))
    if heads:
        in_specs.append(_full((1, LANES)))
        args.append(head_gain)
        for width in [D_MODEL] + ([n - D_MODEL] if n > D_MODEL else []):
            out_shape.append(_sds((t, width), BF16))
            out_specs.append(pl.BlockSpec((tm, width), row))
    return _pcall(body, name=name, out_shape=out_shape, grid=(t // tm,), in_specs=in_specs, out_specs=out_specs,
                  semantics=("parallel",))(*args)


def matmul_residual(a, w, res, *, name, tm=512):
    t, k = a.shape
    n = w.shape[1]
    tm = min(tm, t)

    def body(a_ref, w_ref, res_ref, o_ref):
        o_ref[...] = res_ref[...] + _dot(a_ref[...], w_ref[...])

    row = lambda i: (i, 0)
    return _pcall(
        body, name=name, out_shape=_sds((t, n), F32), grid=(t // tm,),
        in_specs=[pl.BlockSpec((tm, k), row), _full(w.shape), pl.BlockSpec((tm, n), row)],
        out_specs=pl.BlockSpec((tm, n), row), semantics=("parallel",))(a, w, res)


def ple_forward(x, g, w_gate, p, w_proj, *, name, tm=256):
    t, d = x.shape
    tm = min(tm, t)

    def body(x_ref, g_ref, wg_ref, p_ref, wp_ref, h_ref, r_ref, gate_ref, pp_ref, o_ref):
        xv = x_ref[...]
        r = lax.rsqrt(jnp.mean(xv * xv, axis=-1, keepdims=True) + EPS)
        h = ((xv * r) * g_ref[...]).astype(BF16)
        h_ref[...] = h
        r_ref[...] = r
        gate = _sigmoid(_dot(h, wg_ref[...]))
        gate_ref[...] = gate
        pb = p_ref[...].astype(BF16)
        per = d // N_SHARDS
        for s in range(N_SHARDS):
            cols = slice(s * per, (s + 1) * per)
            pp = _dot(pb, wp_ref[s])
            pp_ref[:, cols] = pp.astype(BF16)
            o_ref[:, cols] = xv[:, cols] + pp * gate[:, cols]

    row = lambda i: (i, 0)
    fixed = lambda i: (0, 0)
    return _pcall(
        body, name=name,
        out_shape=[_sds((t, d), BF16), _sds((t, 1), F32), _sds((t, d), F32), _sds((t, d), BF16), _sds((t, d), F32)],
        grid=(t // tm,),
        in_specs=[pl.BlockSpec((tm, d), row), pl.BlockSpec((1, d), fixed), pl.BlockSpec((d, d), fixed),
                  pl.BlockSpec((tm, PLE_DIM), row),
                  pl.BlockSpec((N_SHARDS, PLE_DIM, d // N_SHARDS), lambda i: (0, 0, 0))],
        out_specs=[pl.BlockSpec((tm, d), row), pl.BlockSpec((tm, 1), row), pl.BlockSpec((tm, d), row),
                   pl.BlockSpec((tm, d), row), pl.BlockSpec((tm, d), row)],
        semantics=("parallel",))(x, g, w_gate, p, w_proj)


def _tril_mask():
    r = lax.broadcasted_iota(jnp.int32, (CHUNK, CHUNK), 0)
    c = lax.broadcasted_iota(jnp.int32, (CHUNK, CHUNK), 1)
    return c <= r


def _sgu_common(pre_ref, gv_ref, ws_ref):
    pre = pre_ref[...]
    pre_u, pre_v = pre[:, :D_MODEL], pre[:, D_MODEL:]
    u = _gelu(pre_u)
    v = _gelu(pre_v)
    r = lax.rsqrt(jnp.mean(v * v, axis=-1, keepdims=True) + EPS)
    vhat = v * r
    vn = (vhat * gv_ref[...]).astype(BF16)
    tril = _tril_mask()
    wm = [jnp.where(tril, ws_ref[g], 0.0).astype(BF16) for g in range(N_GROUPS)]
    return pre_u, pre_v, u, r, vhat, vn, wm, tril


def sgu_forward(pre, g_v, w_s, b_full, *, name):
    t = pre.shape[0]

    def body(pre_ref, gv_ref, ws_ref, b_ref, y_ref):
        _, _, u, _, _, vn, wm, _ = _sgu_common(pre_ref, gv_ref, ws_ref)
        for g in range(N_GROUPS):
            cols = slice(g * LANES, (g + 1) * LANES)
            mix = _dot(wm[g], vn[:, cols]) + b_ref[:, cols]
            y_ref[:, cols] = (u[:, cols] * mix).astype(BF16)

    return _pcall(
        body, name=name, out_shape=_sds((t, D_MODEL), BF16), grid=(t // CHUNK,),
        in_specs=[pl.BlockSpec((CHUNK, 2 * D_MODEL), lambda i: (i, 0)), pl.BlockSpec((1, D_MODEL), lambda i: (0, 0)),
                  pl.BlockSpec((N_GROUPS, CHUNK, CHUNK), lambda i: (0, 0, 0)),
                  pl.BlockSpec((CHUNK, D_MODEL), lambda i: (0, 0))],
        out_specs=pl.BlockSpec((CHUNK, D_MODEL), lambda i: (i, 0)),
        semantics=("parallel",))(pre, g_v, w_s, b_full)


def _suffix_matrix(n):
    r = lax.broadcasted_iota(jnp.int32, (n, n), 0)
    c = lax.broadcasted_iota(jnp.int32, (n, n), 1)
    return jnp.where(r > c, 1.0, 0.0).astype(BF16)


def _prefix_matrix(n):
    r = lax.broadcasted_iota(jnp.int32, (n, n), 0)
    c = lax.broadcasted_iota(jnp.int32, (n, n), 1)
    return jnp.where(r < c, 1.0, 0.0).astype(BF16)


def _block_cumsum(a, tri):
    return _dot(a.astype(BF16), tri)


def _stacked_causal(nq, nk, shift):
    r = lax.broadcasted_iota(jnp.int32, (2 * nq, nk), 0)
    c = lax.broadcasted_iota(jnp.int32, (2 * nq, nk), 1)
    return c + shift < jnp.where(r >= nq, r - nq, r)


def _att_blocks(t):
    bq, bk = min(ATT_Q_BLOCK, t), min(ATT_K_BLOCK, t)
    return bq, bk, bq // bk


def _stack_heads(a, low):
    zero = jnp.zeros_like(a)
    return jnp.concatenate([jnp.where(low, a, zero), jnp.where(low, zero, a)], axis=0)


def stick_breaking_forward(q, k, v, *, name):
    t = q.shape[0]
    bq, bk, ratio = _att_blocks(t)

    def body(q_ref, k_ref, v_ref, o_ref):
        i = pl.program_id(1)
        low = lax.broadcasted_iota(jnp.int32, (bq, LANES), 1) < HEAD_DIM
        tri = _suffix_matrix(bk)
        qs = _stack_heads(q_ref[...], low)

        def block(j, carry, acc, causal=None):
            rows = pl.ds(pl.multiple_of(j * bk, bk), bk)
            z = _dot_nt(qs, k_ref[rows, :])
            ls = _log_sigmoid(z)
            lg = ls - z
            if causal is not None:
                lg = jnp.where(causal, lg, 0.0)
            s = ls + _block_cumsum(lg, tri) + carry
            a = jnp.exp(s)
            if causal is not None:
                a = jnp.where(causal, a, 0.0)
            acc = acc + _dot(a.astype(BF16), v_ref[rows, :])
            return carry + jnp.sum(lg, axis=-1, keepdims=True), acc

        state = (jnp.zeros((2 * bq, 1), F32), jnp.zeros((2 * bq, LANES), F32))
        for m in reversed(range(ratio)):
            state = block(ratio * i + m, state[0], state[1], _stacked_causal(bq, bk, m * bk))
        first = ratio * i

        def two_blocks(n, st):
            st = block(first - 1 - 2 * n, st[0], st[1])
            return block(first - 2 - 2 * n, st[0], st[1])

        state = lax.fori_loop(0, first // 2, two_blocks, state)
        _, acc = lax.fori_loop(0, first % 2, lambda n, st: block(0, st[0], st[1]), state)
        o_ref[...] = jnp.where(low, acc[:bq], acc[bq:]).astype(BF16)

    return _pcall(
        body, name=name, out_shape=_sds((t, D_MODEL), BF16), grid=(D_MODEL // LANES, t // bq),
        in_specs=[pl.BlockSpec((bq, LANES), lambda p, i: (i, p)), pl.BlockSpec((t, LANES), lambda p, i: (0, p)),
                  pl.BlockSpec((t, LANES), lambda p, i: (0, p))],
        out_specs=pl.BlockSpec((bq, LANES), lambda p, i: (i, p)),
        semantics=("parallel", "arbitrary"))(q, k, v)


def loss_forward(x, target, *, name, tm=512):
    t, d = x.shape
    tm = min(tm, t)

    def body(x_ref, t_ref, l_ref, dx_ref):
        @pl.when(pl.program_id(0) == 0)
        def _():
            l_ref[...] = jnp.zeros_like(l_ref)

        diff = x_ref[...] - t_ref[...]
        dx_ref[...] = diff * (1.0 / d)
        l_ref[...] += 0.5 * jnp.sum(jnp.mean(diff * diff, axis=-1, keepdims=True))

    return _pcall(
        body, name=name, out_shape=[_sds((8, LANES), F32), _sds((t, d), F32)], grid=(t // tm,),
        in_specs=[pl.BlockSpec((tm, d), lambda i: (i, 0))] * 2,
        out_specs=[pl.BlockSpec((8, LANES), lambda i: (0, 0)), pl.BlockSpec((tm, d), lambda i: (i, 0))],
        semantics=("arbitrary",))(x, target)


def matmul_nt(dy, w, *, name, mul=None, out_dtype=F32, tm=512):
    t, n = dy.shape
    k = w.shape[0]
    tm = min(tm, t)

    def body(*refs):
        if mul is None:
            dy_ref, w_ref, o_ref = refs
        else:
            dy_ref, w_ref, m_ref, o_ref = refs
        y = _dot_nt(dy_ref[...].astype(BF16), w_ref[...])
        if mul is not None:
            y = y * (2.0 * m_ref[...].astype(F32))
        o_ref[...] = y.astype(out_dtype)

    row = lambda i: (i, 0)
    in_specs = [pl.BlockSpec((tm, n), row), _full(w.shape)]
    args = [dy, w]
    if mul is not None:
        in_specs.append(pl.BlockSpec((tm, k), row))
        args.append(mul)
    return _pcall(body, name=name, out_shape=_sds((t, k), out_dtype), grid=(t // tm,), in_specs=in_specs,
                  out_specs=pl.BlockSpec((tm, k), row), semantics=("parallel",))(*args)


def matmul_tn(a, dy, *, name, col_shards, tk=512):
    t, k = a.shape
    n = dy.shape[1]
    if col_shards:
        tn = n // N_SHARDS

        def body(a_ref, dy_ref, o_ref):
            o_ref[...] = _dot_tn(a_ref[...].astype(BF16), dy_ref[...].astype(BF16))

        return _pcall(body, name=name, out_shape=_sds((N_SHARDS, k, tn), F32), grid=(N_SHARDS,),
                      in_specs=[_full((t, k)), pl.BlockSpec((t, tn), lambda j: (0, j))],
                      out_specs=pl.BlockSpec((None, k, tn), lambda j: (j, 0, 0)), semantics=("parallel",))(a, dy)

    tk = min(tk, k)

    def body(a_ref, dy_ref, o_ref, dy_bf):
        @pl.when(pl.program_id(0) == 0)
        def _():
            dy_bf[...] = dy_ref[...].astype(BF16)

        o_ref[...] = _dot_tn(a_ref[...].astype(BF16), dy_bf[...])

    return _pcall(body, name=name, out_shape=_sds((k, n), F32), grid=(k // tk,),
                  in_specs=[pl.BlockSpec((t, tk), lambda i: (0, i)), _full((t, n))],
                  out_specs=pl.BlockSpec((tk, n), lambda i: (i, 0)),
                  scratch_shapes=[pltpu.VMEM((t, n), BF16)], semantics=("arbitrary",))(a, dy)


def norm_backward(dpre, w, x, g, rstd, dx_out, *, name, tm=512):
    t, d = x.shape
    n = dpre.shape[1]
    tm = min(tm, t)
    if w.ndim == 3:
        w_spec = pl.BlockSpec(w.shape, lambda i: (0, 0, 0))
    else:
        w_spec = pl.BlockSpec(w.shape, lambda i: (0, 0))

    def body(dp_ref, w_ref, x_ref, g_ref, r_ref, dxo_ref, dx_ref, dg_ref):
        @pl.when(pl.program_id(0) == 0)
        def _():
            dg_ref[...] = jnp.zeros_like(dg_ref)

        if w.ndim == 3:
            per = n // N_SHARDS
            dh = _dot_nt(dp_ref[:, 0:per], w_ref[0])
            for s in range(1, N_SHARDS):
                dh = dh + _dot_nt(dp_ref[:, s * per:(s + 1) * per], w_ref[s])
        else:
            dh = _dot_nt(dp_ref[...], w_ref[...])
        r = r_ref[...]
        xn = x_ref[...] * r
        dg_ref[...] += jnp.sum(dh * xn, axis=0, keepdims=True)
        dxn = dh * g_ref[...]
        dx = r * (dxn - xn * jnp.mean(dxn * xn, axis=-1, keepdims=True))
        dx_ref[...] = dxo_ref[...] + dx

    row = lambda i: (i, 0)
    fixed = lambda i: (0, 0)
    return _pcall(
        body, name=name, out_shape=[_sds((t, d), F32), _sds((1, d), F32)], grid=(t // tm,),
        in_specs=[pl.BlockSpec((tm, n), row), w_spec, pl.BlockSpec((tm, d), row),
                  pl.BlockSpec((1, d), fixed), pl.BlockSpec((tm, 1), row), pl.BlockSpec((tm, d), row)],
        out_specs=[pl.BlockSpec((tm, d), row), pl.BlockSpec((1, d), fixed)],
        semantics=("arbitrary",))(dpre, w, x, g, rstd, dx_out)


def ple_backward(dx, gate, pp, w_gate, x, g, rstd, *, name, tm=512):
    t, d = dx.shape
    tm = min(tm, t)

    def body(dx_ref, gate_ref, pp_ref, w_ref, x_ref, g_ref, r_ref, dxn_ref, dg_ref, dgate_ref, dproj_ref):
        @pl.when(pl.program_id(0) == 0)
        def _():
            dg_ref[...] = jnp.zeros_like(dg_ref)

        dxv = dx_ref[...]
        gate = gate_ref[...]
        dgate = (dxv * pp_ref[...].astype(F32) * (gate * (1.0 - gate))).astype(BF16)
        dgate_ref[...] = dgate
        dproj_ref[...] = (dxv * gate).astype(BF16)
        dh = _dot_nt(dgate, w_ref[...])
        r = r_ref[...]
        xn = x_ref[...] * r
        dg_ref[...] += jnp.sum(dh * xn, axis=0, keepdims=True)
        dxn = dh * g_ref[...]
        dxn_ref[...] = dxv + r * (dxn - xn * jnp.mean(dxn * xn, axis=-1, keepdims=True))

    row = lambda i: (i, 0)
    blk = pl.BlockSpec((tm, d), row)
    return _pcall(
        body, name=name, out_shape=[_sds((t, d), F32), _sds((1, d), F32), _sds((t, d), BF16), _sds((t, d), BF16)],
        grid=(t // tm,),
        in_specs=[blk, blk, blk, _full(w_gate.shape), blk, _full((1, d)), pl.BlockSpec((tm, 1), row)],
        out_specs=[blk, _full((1, d)), blk, blk], semantics=("arbitrary",))(dx, gate, pp, w_gate, x, g, rstd)


def sgu_backward(dy, pre, g_v, w_s, b_full, *, name):
    t = pre.shape[0]
    n_chunks = t // CHUNK

    def body(dy_ref, pre_ref, gv_ref, ws_ref, b_ref, dpre_ref, dws_ref, db_ref, dgv_ref, dvn_s, dbf_s):
        step = pl.program_id(0)

        @pl.when(step == 0)
        def _():
            dws_ref[...] = jnp.zeros_like(dws_ref)
            dgv_ref[...] = jnp.zeros_like(dgv_ref)
            dbf_s[...] = jnp.zeros_like(dbf_s)

        pre_u, pre_v, u, r, vhat, vn, wm, tril = _sgu_common(pre_ref, gv_ref, ws_ref)
        dyv = dy_ref[...]
        for g in range(N_GROUPS):
            cols = slice(g * LANES, (g + 1) * LANES)
            mix = _dot(wm[g], vn[:, cols]) + b_ref[:, cols]
            dmix = dyv[:, cols] * u[:, cols]
            dmix_b = dmix.astype(BF16)
            du = dyv[:, cols] * mix
            dpre_ref[:, cols] = (du * _gelu_grad(pre_u[:, cols])).astype(BF16)
            dws_ref[g] += jnp.where(tril, _dot_nt(dmix_b, vn[:, cols]), 0.0)
            dbf_s[:, cols] += dmix
            dvn_s[:, cols] = _dot_tn(wm[g], dmix_b)
        dvn = dvn_s[...]
        dgv_ref[...] += jnp.sum(dvn * vhat, axis=0, keepdims=True)
        dxn = dvn * gv_ref[...]
        dv = r * (dxn - vhat * jnp.mean(dxn * vhat, axis=-1, keepdims=True))
        dpre_ref[:, D_MODEL:] = (dv * _gelu_grad(pre_v)).astype(BF16)

        @pl.when(step == n_chunks - 1)
        def _():
            lane = lax.broadcasted_iota(jnp.int32, (CHUNK, LANES), 1)
            acc = jnp.zeros((CHUNK, LANES), F32)
            for g in range(N_GROUPS):
                s = jnp.sum(dbf_s[:, g * LANES:(g + 1) * LANES], axis=-1, keepdims=True)
                acc = jnp.where(lane == g, s, acc)
            db_ref[...] = acc

    fixed2 = lambda i: (0, 0)
    return _pcall(
        body, name=name,
        out_shape=[_sds((t, 2 * D_MODEL), BF16), _sds((N_GROUPS, CHUNK, CHUNK), F32), _sds((CHUNK, LANES), F32),
                   _sds((1, D_MODEL), F32)],
        grid=(n_chunks,),
        in_specs=[pl.BlockSpec((CHUNK, D_MODEL), lambda i: (i, 0)), pl.BlockSpec((CHUNK, 2 * D_MODEL), lambda i: (i, 0)),
                  pl.BlockSpec((1, D_MODEL), fixed2), pl.BlockSpec((N_GROUPS, CHUNK, CHUNK), lambda i: (0, 0, 0)),
                  pl.BlockSpec((CHUNK, D_MODEL), fixed2)],
        out_specs=[pl.BlockSpec((CHUNK, 2 * D_MODEL), lambda i: (i, 0)),
                   pl.BlockSpec((N_GROUPS, CHUNK, CHUNK), lambda i: (0, 0, 0)), pl.BlockSpec((CHUNK, LANES), fixed2),
                   pl.BlockSpec((1, D_MODEL), fixed2)],
        scratch_shapes=[pltpu.VMEM((CHUNK, D_MODEL), F32), pltpu.VMEM((CHUNK, D_MODEL), F32)],
        semantics=("arbitrary",))(dy, pre, g_v, w_s, b_full)


def head_norm_backward(dy, pre, g128, *, name, col_block=0, scale=1.0, passthrough=None, tm=512):
    t = dy.shape[0]
    tm = min(tm, t)
    width = 2 * D_MODEL if passthrough is not None else D_MODEL

    def body(*refs):
        if passthrough is not None:
            dy_ref, x_ref, g_ref, dv_ref, o_ref, dg_ref = refs
            o_ref[:, D_MODEL:] = dv_ref[...].astype(BF16)
        else:
            dy_ref, x_ref, g_ref, o_ref, dg_ref = refs

        @pl.when(pl.program_id(0) == 0)
        def _():
            dg_ref[...] = jnp.zeros_like(dg_ref)

        g = g_ref[...]
        dg = jnp.zeros((1, LANES), F32)
        for b in range(D_MODEL // LANES):
            cols = slice(b * LANES, (b + 1) * LANES)
            xv = x_ref[:, cols]
            r = _head_rstd(xv)
            xn = xv * r
            dyv = dy_ref[:, cols] * scale
            dg = dg + jnp.sum(dyv * xn, axis=0, keepdims=True)
            dxn = dyv * g
            o_ref[:, cols] = (r * (dxn - xn * _head_mean(dxn * xn))).astype(BF16)
        dg_ref[...] += dg

    row = lambda i: (i, 0)
    in_specs = [pl.BlockSpec((tm, D_MODEL), row), pl.BlockSpec((tm, D_MODEL), lambda i: (i, col_block)),
                pl.BlockSpec((1, LANES), lambda i: (0, 0))]
    args = [dy, pre, g128]
    if passthrough is not None:
        in_specs.append(pl.BlockSpec((tm, D_MODEL), row))
        args.append(passthrough)
    return _pcall(body, name=name, out_shape=[_sds((t, width), BF16), _sds((1, LANES), F32)], grid=(t // tm,),
                  in_specs=in_specs,
                  out_specs=[pl.BlockSpec((tm, width), row), pl.BlockSpec((1, LANES), lambda i: (0, 0))],
                  semantics=("arbitrary",))(*args)


def stick_breaking_backward(q, k, v, do, *, name):
    t = q.shape[0]
    bq, bk, ratio = _att_blocks(t)

    def body(q_ref, k_ref, v_ref, do_ref, dq_ref, dk_ref, dv_ref, s_buf, sg_buf):
        i = pl.program_id(1)

        @pl.when(i == 0)
        def _():
            dk_ref[...] = jnp.zeros_like(dk_ref)
            dv_ref[...] = jnp.zeros_like(dv_ref)

        low = lax.broadcasted_iota(jnp.int32, (bq, LANES), 1) < HEAD_DIM
        suffix = _suffix_matrix(bk)
        prefix = _prefix_matrix(bk)
        qs = _stack_heads(q_ref[...], low)
        dos = _stack_heads(do_ref[...], low)
        first = ratio * i

        def log_weights(j, carry, causal=None):
            rows = pl.ds(pl.multiple_of(j * bk, bk), bk)
            z = _dot_nt(qs, k_ref[rows, :])
            ls = _log_sigmoid(z)
            lg = ls - z
            if causal is not None:
                lg = jnp.where(causal, lg, 0.0)
            s_buf[j] = ls + _block_cumsum(lg, suffix) + carry
            sg_buf[j] = jnp.exp(ls)
            return carry + jnp.sum(lg, axis=-1, keepdims=True)

        carry = jnp.zeros((2 * bq, 1), F32)
        for m in reversed(range(ratio)):
            carry = log_weights(first + m, carry, _stacked_causal(bq, bk, m * bk))
        carry = lax.fori_loop(0, first // 2,
                              lambda n, c: log_weights(first - 2 - 2 * n, log_weights(first - 1 - 2 * n, c)), carry)
        lax.fori_loop(0, first % 2, lambda n, c: log_weights(0, c), carry)

        def grads(j, pcarry, dq_acc, causal=None):
            rows = pl.ds(pl.multiple_of(j * bk, bk), bk)
            a = jnp.exp(s_buf[j])
            if causal is not None:
                a = jnp.where(causal, a, 0.0)
            sg = sg_buf[j]
            ds = _dot_nt(dos, v_ref[rows, :]) * a
            before = _block_cumsum(ds, prefix) + pcarry
            if causal is not None:
                before = jnp.where(causal, before, 0.0)
            dz = (ds - sg * (ds + before)).astype(BF16)
            dq_acc = dq_acc + _dot(dz, k_ref[rows, :])
            dk_ref[rows, :] += _dot_tn(dz, qs)
            dv_ref[rows, :] += _dot_tn(a.astype(BF16), dos)
            return pcarry + jnp.sum(ds, axis=-1, keepdims=True), dq_acc

        def two_blocks(n, st):
            st = grads(2 * n, st[0], st[1])
            return grads(2 * n + 1, st[0], st[1])

        state = lax.fori_loop(0, first // 2, two_blocks,
                              (jnp.zeros((2 * bq, 1), F32), jnp.zeros((2 * bq, LANES), F32)))
        state = lax.fori_loop(0, first % 2, lambda n, st: grads(first - 1, st[0], st[1]), state)
        for m in range(ratio):
            state = grads(first + m, state[0], state[1], _stacked_causal(bq, bk, m * bk))
        dq_ref[...] = jnp.where(low, state[1][:bq], state[1][bq:])

    full = pl.BlockSpec((t, LANES), lambda p, i: (0, p))
    qblk = pl.BlockSpec((bq, LANES), lambda p, i: (i, p))
    return _pcall(
        body, name=name, out_shape=[_sds((t, D_MODEL), F32)] * 3, grid=(D_MODEL // LANES, t // bq),
        in_specs=[qblk, full, full, qblk], out_specs=[qblk, full, full],
        scratch_shapes=[pltpu.VMEM((t // bk, 2 * bq, bk), F32), pltpu.VMEM((t // bk, 2 * bq, bk), F32)],
        semantics=("parallel", "arbitrary"))(q, k, v, do)


def _mlp_backward(dx, saved, g, w_up, w_down, tag):
    x, h, r, a, a2 = saved
    d_w_down = matmul_tn(a2, dx, name=f"d_w_down_{tag}", col_shards=False)
    dpre = matmul_nt(dx, w_down, name=f"d_mlp_act_{tag}", mul=a, out_dtype=BF16)
    d_w_up = matmul_tn(h, dpre, name=f"d_w_up_{tag}", col_shards=True)
    dx, d_g = norm_backward(dpre, w_up, x, g, r, dx, name=f"d_mlp_norm_{tag}")
    return dx, d_w_up, d_w_down, d_g


def _ple_backward(dx, saved, p, g, w_gate, tag):
    x, h, r, gate, pp = saved
    dx, d_g, dgate, dproj = ple_backward(dx, gate, pp, w_gate, x, g, r, name=f"d_ple_{tag}")
    d_w_proj = matmul_tn(p, dproj, name=f"d_w_ple_proj_{tag}", col_shards=True)
    d_w_gate = matmul_tn(h, dgate, name=f"d_w_ple_gate_{tag}", col_shards=False)
    return dx, d_w_gate, d_w_proj, d_g


def local_step(x, p, target, w, late=None):
    row = lambda v: v.reshape(1, -1)
    g128 = lambda v: jnp.tile(v.reshape(1, HEAD_DIM), (1, 2))
    scale = HEAD_DIM ** -0.5
    b_full = jnp.repeat(jnp.transpose(w["b_spatial"][0]), LANES, axis=1)
    w_s = w["w_spatial"][0]

    mats = {}
    for name, value in w.items():
        if isinstance(value, tuple):
            mats.update({(name, layer): v for layer, v in enumerate(value)})
    if "w_kv" in w:
        mats[("w_kv", 0)] = w["w_kv"]

    def fetch(name, layer, after):
        if (name, layer) not in mats:
            mats.update(late.weights(name, layer, after))
        return mats[(name, layer)]

    def mlp_forward(x_in, layer):
        h, r, a, a2 = norm_matmul(x_in, row(w["ln_mlp"][layer]), fetch("w_up", layer, x_in), name=f"mlp_up_{layer}",
                                  epilogue="relu2")
        return matmul_residual(a2, fetch("w_down", layer, a2), x_in, name=f"mlp_down_{layer}"), (x_in, h, r, a, a2)

    def ple(x_in, layer):
        return ple_forward(x_in, row(w["ln_ple"][layer]), fetch("w_ple_gate", layer, x_in), p[layer],
                           fetch("w_ple_proj", layer, x_in), name=f"ple_{layer}")

    x0 = x
    h_a, r_a, pre_a = norm_matmul(x0, row(w["ln_mix_a"][0]), fetch("w_in_a", 0, x0), name="sgu_in")
    y_a = sgu_forward(pre_a, row(w["g_v_a"][0]), w_s, b_full, name="sgu_mix")
    x1 = matmul_residual(y_a, fetch("w_out_a", 0, y_a), x0, name="sgu_out")
    x2, mlp0 = mlp_forward(x1, 0)
    ple0 = ple(x2, 0)
    x3 = ple0[4]
    h_kv, r_kv, kv_pre, k_n, v_b = norm_matmul(x3, row(w["ln_kv"]), fetch("w_kv", 0, x3), name="kv_proj",
                                               epilogue="heads", head_gain=g128(w["g_k"]))
    h_q, r_q, q_pre, q_n = norm_matmul(x3, row(w["ln_mix_b"][0]), fetch("w_q", 0, k_n), name="q_proj",
                                       epilogue="heads", head_gain=g128(w["g_q"][0]), head_scale=scale)
    o = stick_breaking_forward(q_n, k_n, v_b, name="sb_fwd")
    if late is not None:
        late.pass_on("w_up", 1, o)
    x4 = matmul_residual(o, fetch("w_out_b", 0, o), x3, name="sb_out")
    x5, mlp1 = mlp_forward(x4, 1)
    ple1 = ple(x5, 1)
    x6 = ple1[4]
    loss_blk, dx = loss_forward(x6, target, name="loss")

    g = {}
    dx, dwg1, dwp1, dlnp1 = _ple_backward(dx, (x5,) + tuple(ple1[:4]), p[1], row(w["ln_ple"][1]),
                                          mats[("w_ple_gate", 1)], 1)
    dx, dwu1, dwd1, dlnm1 = _mlp_backward(dx, mlp1, row(w["ln_mlp"][1]), mats[("w_up", 1)], mats[("w_down", 1)], 1)
    g["w_out_b"] = matmul_tn(o, dx, name="d_w_out_b", col_shards=False)
    do = matmul_nt(dx, mats[("w_out_b", 0)], name="d_sb_out", out_dtype=BF16)
    dq_n, dk_n, dv = stick_breaking_backward(q_n, k_n, v_b, do, name="sb_bwd")
    dq_pre, dgq = head_norm_backward(dq_n, q_pre, g128(w["g_q"][0]), name="d_q_norm", scale=scale)
    dkv_pre, dgk = head_norm_backward(dk_n, kv_pre, g128(w["g_k"]), name="d_k_norm", passthrough=dv)
    g["w_q"] = matmul_tn(h_q, dq_pre, name="d_w_q", col_shards=False)
    g["w_kv"] = matmul_tn(h_kv, dkv_pre, name="d_w_kv", col_shards=True)
    dx, g["ln_mix_b"] = norm_backward(dq_pre, mats[("w_q", 0)], x3, row(w["ln_mix_b"][0]), r_q, dx, name="d_q_in")
    dx, g["ln_kv"] = norm_backward(dkv_pre, mats[("w_kv", 0)], x3, row(w["ln_kv"]), r_kv, dx, name="d_kv_in")
    g["g_q"] = dgq[:, :HEAD_DIM] + dgq[:, HEAD_DIM:]
    g["g_k"] = (dgk[:, :HEAD_DIM] + dgk[:, HEAD_DIM:]).reshape(HEAD_DIM)
    g["ln_kv"] = g["ln_kv"].reshape(D_MODEL)
    if late is not None:
        late.pair_start({("w_kv", 0): g["w_kv"], ("w_q", 0): g["w_q"], ("w_out_b", 0): g["w_out_b"],
                         ("w_up", 1): dwu1, ("w_down", 1): dwd1, ("w_ple_gate", 1): dwg1, ("w_ple_proj", 1): dwp1}, dx)
    dx, dwg0, dwp0, dlnp0 = _ple_backward(dx, (x2,) + tuple(ple0[:4]), p[0], row(w["ln_ple"][0]),
                                          mats[("w_ple_gate", 0)], 0)
    if late is not None:
        late.chip_start(dx)
    dx, dwu0, dwd0, dlnm0 = _mlp_backward(dx, mlp0, row(w["ln_mlp"][0]), mats[("w_up", 0)], mats[("w_down", 0)], 0)
    if late is not None:
        late.pair_start({("w_up", 0): dwu0, ("w_down", 0): dwd0, ("w_ple_gate", 0): dwg0, ("w_ple_proj", 0): dwp0}, dx)
    g["w_out_a"] = matmul_tn(y_a, dx, name="d_w_out_a", col_shards=False)
    dy_a = matmul_nt(dx, mats[("w_out_a", 0)], name="d_sgu_out")
    dpre_a, dws, db, g["g_v_a"] = sgu_backward(dy_a, pre_a, row(w["g_v_a"][0]), w_s, b_full, name="d_sgu_mix")
    if late is not None:
        late.chip_start(dpre_a)
    g["w_in_a"] = matmul_tn(h_a, dpre_a, name="d_w_in_a", col_shards=True)
    dx, g["ln_mix_a"] = norm_backward(dpre_a, mats[("w_in_a", 0)], x0, row(w["ln_mix_a"][0]), r_a, dx, name="d_sgu_in")
    g["w_spatial"] = dws[None]
    g["b_spatial"] = jnp.transpose(db[:, :N_GROUPS])[None]
    g["w_up"] = (dwu0, dwu1)
    g["w_down"] = (dwd0, dwd1)
    g["w_ple_gate"] = (dwg0, dwg1)
    g["w_ple_proj"] = (dwp0, dwp1)
    g["ln_mlp"] = jnp.concatenate([dlnm0, dlnm1], axis=0)
    g["ln_ple"] = jnp.concatenate([dlnp0, dlnp1], axis=0)
    return loss_blk, dx, g


ANY = pl.BlockSpec(memory_space=pl.ANY)


def _place():
    x, y, c = lax.axis_index("x"), lax.axis_index("y"), lax.axis_index("c")
    others = [(1 - x, y), (x, 1 - y), (1 - x, 1 - y)]
    return x, y, c, 2 * x + y, others


def cast_into_slot(w3, layer, slot, *, name, after=None, tm=512):
    _, r, c = w3.shape
    tm = min(tm, r)

    def body(slot_ref, w_ref, *rest):
        rest[-1][...] = w_ref[...].astype(BF16)

    in_specs = [pl.BlockSpec((None, tm, c), lambda i, s: (layer, i, 0))]
    args = [slot, w3]
    if after is not None:
        in_specs.append(ANY)
        args.append(after)
    return _pcall(body, name=name, out_shape=_sds((N_SHARDS, r, c), BF16), grid=(r // tm,), num_prefetch=1,
                  in_specs=in_specs, out_specs=pl.BlockSpec((None, tm, c), lambda i, s: (s[0], i, 0)),
                  semantics=("parallel",))(*args)


def gather_vectors(vecs, *, name):
    n = len(vecs)

    def body(*refs):
        src, out = refs[:n], refs[n:2 * n]
        send, recv, loc = refs[2 * n:]
        x, y, c, s_me, others = _place()

        def copy(l, k, slot):
            ox, oy = others[k]
            return pltpu.make_async_remote_copy(src[l], out[l].at[slot], send.at[l, k], recv.at[l, k],
                                                device_id=(ox, oy, c), device_id_type=MESH)

        for l in range(n):
            for k in range(3):
                copy(l, k, s_me).start()
        for l in range(n):
            own = pltpu.make_async_copy(src[l], out[l].at[s_me], loc)
            own.start()
            own.wait()
        for l in range(n):
            for k in range(3):
                ox, oy = others[k]
                copy(l, k, 2 * ox + oy).wait_recv()
                copy(l, k, s_me).wait_send()

    return _pcall(body, name=name, out_shape=[_sds((N_SHARDS,) + v.shape, F32) for v in vecs], in_specs=[ANY] * n,
                  out_specs=[ANY] * n,
                  scratch_shapes=[pltpu.SemaphoreType.DMA((n, 3)), pltpu.SemaphoreType.DMA((n, 3)),
                                  pltpu.SemaphoreType.DMA(())],
                  side_effects=True)(*vecs)


HBM = pl.BlockSpec(memory_space=pltpu.HBM)
SEM = pl.BlockSpec(memory_space=pltpu.SEMAPHORE)
DATAFLOW = pltpu.SideEffectType.DATAFLOW_SIDE_EFFECTING


def _split_call(body, *, name, out_shape, in_specs, out_specs, aliases, views_of=()):
    def make(wrap, specs):
        body_ = wrap(body)
        return pl.pallas_call(body_, name=name, out_shape=out_shape, in_specs=specs, out_specs=out_specs,
                              input_output_aliases=aliases,
                              compiler_params=pltpu.CompilerParams(has_side_effects=DATAFLOW))

    return lambda *args: _in_order(make, in_specs, args, views_of)


def _token_shape():
    return jax.ShapeDtypeStruct((8, LANES), F32)


def gather_start(mats, after, *, name):
    n = len(mats)
    halves = [pltpu.with_memory_space_constraint(m.reshape(N_SHARDS, 2, m.shape[1] // 2, m.shape[2]), pltpu.HBM)
              for m in mats]

    def body(*refs):
        send, recv = refs[n + 1], refs[n + 2]
        out, token = refs[n + 3:2 * n + 3], refs[2 * n + 3]
        x, y, c, s_me, others = _place()
        for l in range(n):
            for k in range(3):
                ox, oy = others[k]
                pltpu.make_async_remote_copy(out[l].at[s_me, c], out[l].at[s_me, c], send.at[3 * l + k],
                                             recv.at[3 * l + k], device_id=(ox, oy, c), device_id_type=MESH).start()
        token[...] = jnp.zeros_like(token)

    res = _split_call(
        body, name=name,
        out_shape=(pltpu.SemaphoreType.DMA((3 * n,)), pltpu.SemaphoreType.DMA((3 * n,)),
                   *[pltpu.HBM(h.shape, BF16) for h in halves], _token_shape()),
        in_specs=[HBM] * n + [ANY], out_specs=(SEM, SEM, *[HBM] * n, pl.BlockSpec(memory_space=pltpu.VMEM)),
        aliases={l: 2 + l for l in range(n)}, views_of=mats)(*halves, after)
    return res[0], res[1], list(res[2:2 + n]), res[2 + n]


def gather_pass_on(bufs, send_a, recv_a, after, *, name, base=0):
    n = len(bufs)

    def body(*refs):
        send_a, recv_a = refs[n], refs[n + 1]
        out = refs[n + 3:2 * n + 3]
        send_b, recv_b, token = refs[2 * n + 3:]
        x, y, c, s_me, others = _place()
        for l in range(n):
            for k in range(3):
                ox, oy = others[k]
                landed, i = out[l].at[2 * ox + oy, c], 3 * l + k
                pltpu.make_async_remote_copy(landed, landed, send_a.at[3 * base + i], recv_a.at[3 * base + i],
                                             device_id=(x, y, 1 - c), device_id_type=MESH).wait_recv()
                pltpu.make_async_remote_copy(landed, landed, send_b.at[i], recv_b.at[i],
                                             device_id=(x, y, 1 - c), device_id_type=MESH).start()
        for l in range(n):
            for k in range(3):
                mine, i = out[l].at[s_me, c], 3 * (base + l) + k
                pltpu.make_async_remote_copy(mine, mine, send_a.at[i], recv_a.at[i],
                                             device_id=(x, y, 1 - c), device_id_type=MESH).wait_send()
        token[...] = jnp.zeros_like(token)

    res = _split_call(
        body, name=name,
        out_shape=(*[pltpu.HBM(b.shape, BF16) for b in bufs], pltpu.SemaphoreType.DMA((3 * n,)),
                   pltpu.SemaphoreType.DMA((3 * n,)), _token_shape()),
        in_specs=[HBM] * n + [SEM, SEM, ANY],
        out_specs=(*[HBM] * n, SEM, SEM, pl.BlockSpec(memory_space=pltpu.VMEM)),
        aliases={l: l for l in range(n)})(*bufs, send_a, recv_a, after)
    return list(res[:n]), res[n], res[n + 1], res[n + 2]


def gather_finish(bufs, send_b, recv_b, after, shapes, *, name):
    n = len(bufs)

    def body(*refs):
        send_b, recv_b = refs[n], refs[n + 1]
        out = refs[n + 3:]
        x, y, c, _, others = _place()
        for l in range(n):
            for k in range(3):
                ox, oy = others[k]
                theirs, mine, i = out[l].at[2 * ox + oy, 1 - c], out[l].at[2 * ox + oy, c], 3 * l + k
                pltpu.make_async_remote_copy(theirs, theirs, send_b.at[i], recv_b.at[i],
                                             device_id=(x, y, 1 - c), device_id_type=MESH).wait_recv()
                pltpu.make_async_remote_copy(mine, mine, send_b.at[i], recv_b.at[i],
                                             device_id=(x, y, 1 - c), device_id_type=MESH).wait_send()

    res = _split_call(
        body, name=name, out_shape=tuple(pltpu.HBM(b.shape, BF16) for b in bufs),
        in_specs=[HBM] * n + [SEM, SEM, ANY], out_specs=tuple([HBM] * n),
        aliases={l: l for l in range(n)})(*bufs, send_b, recv_b, after)
    return [r.reshape(s) for r, s in zip(res, shapes)]


def exchange_start(srcs, dst_shapes, dst_dtype, plan, count, after, *, name):
    n, m = len(srcs), len(dst_shapes)
    given = list(srcs)
    srcs = [pltpu.with_memory_space_constraint(s, pltpu.HBM) for s in srcs]
    lands = [pltpu.with_memory_space_constraint(lax.empty(s, dst_dtype), pltpu.HBM) for s in dst_shapes]

    def body(*refs):
        send, recv = refs[n + m + 1], refs[n + m + 2]
        src, dst, token = refs[n + m + 3:2 * n + m + 3], refs[2 * n + m + 3:2 * (n + m) + 3], refs[2 * (n + m) + 3]
        for i, (s, d, dev) in enumerate(plan(_place(), src, dst)):
            pltpu.make_async_remote_copy(s, d, send.at[i], recv.at[i], device_id=dev, device_id_type=MESH).start()
        token[...] = jnp.zeros_like(token)

    res = _split_call(
        body, name=name,
        out_shape=(pltpu.SemaphoreType.DMA((count,)), pltpu.SemaphoreType.DMA((count,)),
                   *[pltpu.HBM(s.shape, s.dtype) for s in srcs], *[pltpu.HBM(s, dst_dtype) for s in dst_shapes],
                   _token_shape()),
        in_specs=[HBM] * (n + m) + [ANY],
        out_specs=(SEM, SEM, *[HBM] * (n + m), pl.BlockSpec(memory_space=pltpu.VMEM)),
        aliases={i: 2 + i for i in range(n + m)}, views_of=given)(*srcs, *lands, after)
    return (list(res[2:2 + n]), list(res[2 + n:2 + n + m]), res[0], res[1], plan), res[2 + n + m]


def exchange_finish(state, after, *, name):
    srcs, lands, send, recv, plan = state
    n, m = len(srcs), len(lands)

    def body(*refs):
        send, recv = refs[n + m], refs[n + m + 1]
        src, dst = refs[n + m + 3:2 * n + m + 3], refs[2 * n + m + 3:]
        for i, (s, d, dev) in enumerate(plan(_place(), src, dst)):
            pltpu.make_async_remote_copy(s, d, send.at[i], recv.at[i], device_id=dev, device_id_type=MESH).wait()

    res = _split_call(
        body, name=name,
        out_shape=tuple(pltpu.HBM(a.shape, a.dtype) for a in srcs + lands),
        in_specs=[HBM] * (n + m) + [SEM, SEM, ANY], out_specs=tuple([HBM] * (n + m)),
        aliases={i: i for i in range(n + m)})(*srcs, *lands, send, recv, after)
    return list(res[:n]), list(res[n:])


def pair_plan(place, src, dst):
    x, y, c, _, _ = place
    return [(s.at[:, 1 - c], d, (x, y, 1 - c)) for s, d in zip(src, dst)]


def chip_plan(place, src, dst):
    x, y, c, _, others = place
    return [(s.at[2 * ox + oy], d.at[k], (ox, oy, c)) for s, d in zip(src, dst) for k, (ox, oy) in enumerate(others)]


def pair_exchange(grads, *, name):
    n = len(grads)

    def body(*refs):
        src, got = refs[:n], refs[n:2 * n]
        send, recv = refs[2 * n:]
        x, y, c, _, _ = _place()

        def swap(l):
            return pltpu.make_async_remote_copy(src[l].at[:, 1 - c], got[l], send.at[l], recv.at[l],
                                                device_id=(x, y, 1 - c), device_id_type=MESH)

        for l in range(n):
            swap(l).start()
        for l in range(n):
            swap(l).wait()

    res = _pcall(body, name=name, out_shape=[_sds((N_SHARDS,) + g.shape[2:], F32) for g in grads],
                 in_specs=[ANY] * n, out_specs=[ANY] * n,
                 scratch_shapes=[pltpu.SemaphoreType.DMA((n,)), pltpu.SemaphoreType.DMA((n,))],
                 side_effects=True)(*grads)
    return list(res)


def add_to_wire(mine, theirs, core, *, name, tm=512):
    s, _, r, c = mine.shape
    tm = min(tm, r)

    def body(core_ref, a_ref, b_ref, o_ref):
        o_ref[...] = (a_ref[...] + b_ref[...]).astype(BF16)

    spec = pl.BlockSpec((None, tm, c), lambda i, j, cr: (i, j, 0))
    return _pcall(body, name=name, out_shape=_sds((s, r, c), BF16), grid=(s, r // tm), num_prefetch=1,
                  in_specs=[pl.BlockSpec((None, None, tm, c), lambda i, j, cr: (i, cr[0], j, 0)), spec],
                  out_specs=spec, semantics=("parallel", "parallel"))(core, mine, theirs)


def sum_chips(wire, landed, place, dest, layer, n_layers, *, name, tm=512):
    _, r, c = wire.shape
    tm = min(tm, r)

    def body(place_ref, w_ref, l_ref, *rest):
        o_ref = rest[-1]
        o_ref[...] = ((w_ref[...].astype(F32) + l_ref[0].astype(F32)) + l_ref[1].astype(F32)) + l_ref[2].astype(F32)

    in_specs = [pl.BlockSpec((None, tm, c), lambda i, pr: (pr[0], i, 0)),
                pl.BlockSpec((3, tm, c), lambda i, pr: (0, i, 0))]
    args = [place, wire, landed]
    aliases = None
    if dest is not None:
        in_specs.append(ANY)
        args.append(dest)
        aliases = {3: 0}
    return _pcall(body, name=name, out_shape=_sds((n_layers, 2, r, c), F32), grid=(r // tm,), num_prefetch=1,
                  in_specs=in_specs,
                  out_specs=pl.BlockSpec((None, None, tm, c), lambda i, pr: (layer, pr[1], i, 0)),
                  aliases=aliases, semantics=("parallel",))(*args)


def pair_share(bufs, slots, *, name):
    n = len(bufs)

    def body(*refs):
        out = refs[n:2 * n]
        send, recv = refs[2 * n:]
        x, y, c, _, _ = _place()

        def share(i, half):
            o, l = slots[i]
            return pltpu.make_async_remote_copy(out[o].at[l, half], out[o].at[l, half], send.at[i], recv.at[i],
                                                device_id=(x, y, 1 - c), device_id_type=MESH)

        for i in range(len(slots)):
            share(i, c).start()
        for i in range(len(slots)):
            share(i, 1 - c).wait_recv()
            share(i, c).wait_send()

    res = _pcall(body, name=name, out_shape=[_sds(b.shape, F32) for b in bufs], in_specs=[ANY] * n,
                 out_specs=[ANY] * n,
                 scratch_shapes=[pltpu.SemaphoreType.DMA((len(slots),)), pltpu.SemaphoreType.DMA((len(slots),))],
                 aliases={o: o for o in range(n)}, side_effects=True)(*bufs)
    return list(res)


def all_reduce_small(packed, *, name):
    n_dev, r, c = packed.shape

    def body(in_ref, out_ref, land, send, recv):
        x, y, cc, _, _ = _place()
        me = 4 * x + 2 * y + cc
        peers = [(px, py, pc) for px in range(2) for py in range(2) for pc in range(2)]

        def scatter(d):
            return pltpu.make_async_remote_copy(in_ref.at[d], land.at[me], send.at[0, d], recv.at[0, me],
                                                device_id=peers[d], device_id_type=MESH)

        def gather(d):
            return pltpu.make_async_remote_copy(out_ref.at[me], out_ref.at[me], send.at[1, d], recv.at[1, me],
                                                device_id=peers[d], device_id_type=MESH)

        for d in range(n_dev):
            @pl.when(d != me)
            def _():
                scatter(d).start()
        land[me] = in_ref[me]
        for d in range(n_dev):
            @pl.when(d != me)
            def _():
                pltpu.make_async_remote_copy(in_ref.at[d], land.at[d], send.at[0, d], recv.at[0, d],
                                             device_id=peers[d], device_id_type=MESH).wait_recv()
        total = land[0]
        for d in range(1, n_dev):
            total = total + land[d]
        out_ref[me] = total
        for d in range(n_dev):
            @pl.when(d != me)
            def _():
                gather(d).start()
        for d in range(n_dev):
            @pl.when(d != me)
            def _():
                pltpu.make_async_remote_copy(out_ref.at[d], out_ref.at[d], send.at[1, d], recv.at[1, d],
                                             device_id=peers[d], device_id_type=MESH).wait_recv()
        for d in range(n_dev):
            @pl.when(d != me)
            def _():
                scatter(d).wait_send()
                gather(d).wait_send()

    vm = pl.BlockSpec(memory_space=pltpu.VMEM)
    return _pcall(body, name=name, out_shape=_sds(packed.shape, F32), in_specs=[vm], out_specs=vm,
                  scratch_shapes=[pltpu.VMEM(packed.shape, F32), pltpu.SemaphoreType.DMA((2, n_dev)),
                                  pltpu.SemaphoreType.DMA((2, n_dev))],
                  side_effects=True)(packed)


def adamw(w, g, m, v, *, name, part=None, dest=None, tm=512):
    shape = w.shape
    cols = shape[-1]
    rows = 1
    for s in shape[:-1]:
        rows *= s
    first, count = 0, rows
    if part is not None:
        count = rows // part[1]
        first = part[0] * count
    tm = min(tm, count)
    assert count % tm == 0
    two_d = lambda a: a.reshape(rows, cols)

    def body(w_ref, g_ref, m_ref, v_ref, *rest):
        d_ref, mo_ref, vo_ref = rest[-3:]
        gv = g_ref[...]
        m_new = ADAM_B1 * m_ref[...] + (1.0 - ADAM_B1) * gv
        v_new = ADAM_B2 * v_ref[...] + (1.0 - ADAM_B2) * (gv * gv)
        m_hat = m_new / (1.0 - ADAM_B1 ** ADAM_STEP)
        v_hat = v_new / (1.0 - ADAM_B2 ** ADAM_STEP)
        d_ref[...] = -ADAM_LR * (m_hat / (jnp.sqrt(v_hat) + ADAM_EPS) + ADAM_WD * w_ref[...])
        mo_ref[...] = m_new
        vo_ref[...] = v_new

    spec = pl.BlockSpec((tm, cols), lambda i: (first // tm + i, 0))
    args = [two_d(w), two_d(g), two_d(m), two_d(v)]
    in_specs = [spec] * 4
    aliases = None
    if dest is not None:
        args += [two_d(d) for d in dest]
        in_specs = in_specs + [ANY] * 3
        aliases = {4: 0, 5: 1, 6: 2}
    outs = _pcall(body, name=name, out_shape=[_sds((rows, cols), F32)] * 3, grid=(count // tm,), in_specs=in_specs,
                  out_specs=[spec] * 3, aliases=aliases, semantics=("parallel",))(*args)
    return [o.reshape(shape) for o in outs]


WEIGHTS = ("ln_mix_a", "w_in_a", "g_v_a", "w_spatial", "b_spatial", "w_out_a", "ln_kv", "w_kv", "g_k", "ln_mix_b",
           "w_q", "g_q", "w_out_b", "ln_mlp", "w_up", "w_down", "ln_ple", "w_ple_gate", "w_ple_proj")
MATRICES = (("w_in_a", 1, True), ("w_out_a", 1, False), ("w_kv", 0, True), ("w_q", 1, False), ("w_out_b", 1, False),
            ("w_up", 2, True), ("w_down", 2, False), ("w_ple_gate", 2, False), ("w_ple_proj", 2, True))
GATHER_STAGES = ((("w_in_a", 0), ("w_out_a", 0)), (("w_up", 0),), (("w_down", 0),),
                 (("w_ple_gate", 0), ("w_ple_proj", 0), ("w_kv", 0)), (("w_q", 0), ("w_out_b", 0)),
                 (("w_up", 1), ("w_down", 1), ("w_ple_gate", 1), ("w_ple_proj", 1)))
REPLICATED = ("w_spatial", "b_spatial", "ln_kv", "g_k", "ln_mix_b", "g_q", "ln_mlp", "ln_ple")
SHARDED_VECTORS = ("ln_mix_a", "g_v_a")
SMALL_ROWS = 18


def kernel(x, p, ln_mix_a, w_in_a, g_v_a, w_spatial, b_spatial, w_out_a, ln_kv, w_kv, g_k, ln_mix_b, w_q, g_q, w_out_b, ln_mlp, w_up, w_down, ln_ple, w_ple_gate, w_ple_proj, loss_target, m_ln_mix_a, m_w_in_a, m_g_v_a, m_w_spatial, m_b_spatial, m_w_out_a, m_ln_kv, m_w_kv, m_g_k, m_ln_mix_b, m_w_q, m_g_q, m_w_out_b, m_ln_mlp, m_w_up, m_w_down, m_ln_ple, m_w_ple_gate, m_w_ple_proj, v_ln_mix_a, v_w_in_a, v_g_v_a, v_w_spatial, v_b_spatial, v_w_out_a, v_ln_kv, v_w_kv, v_g_k, v_ln_mix_b, v_w_q, v_g_q, v_w_out_b, v_ln_mlp, v_w_up, v_w_down, v_ln_ple, v_w_ple_gate, v_w_ple_proj):
    given = dict(locals())
    _PREVIOUS.clear()
    weights = {n: given[n] for n in WEIGHTS}
    shard = 2 * lax.axis_index("x") + lax.axis_index("y")
    core = lax.axis_index("c")
    shard_1 = shard.astype(jnp.int32).reshape(1)
    core_1 = core.astype(jnp.int32).reshape(1)
    place = jnp.stack([shard, core]).astype(jnp.int32)

    col_sharded = {name: cols for name, _, cols in MATRICES}
    layer_count = {name: max(layers, 1) for name, layers, _ in MATRICES}

    def cast(key, after):
        name, layer = key
        w3 = weights[name] if weights[name].ndim == 3 else weights[name][None]
        return (name, layer, col_sharded[name],
                cast_into_slot(w3, layer, shard_1, name=f"cast_{name}_{layer}", after=after))

    head = [cast(key, None) for key in GATHER_STAGES[0]]
    send_h, recv_h, flying_h, token_h = gather_start([lf[3] for lf in head], shard_1, name="gather_start_0")
    tail = [cast(key, token_h) for stage in GATHER_STAGES[1:] for key in stage]
    vec_a = gather_vectors([ln_mix_a, g_v_a], name="gather_vectors")
    send_a, recv_a, flying, token = gather_start([lf[3] for lf in tail], vec_a[0], name="gather_start_1")

    w = {"ln_mix_a": vec_a[0].reshape(1, D_MODEL),
         "g_v_a": vec_a[1].reshape(1, D_MODEL)}
    for name in REPLICATED:
        w[name] = weights[name]

    class Late:
        passed = {}

        def pass_on(self, name, layer, after):
            stage = [(name, layer) in s for s in GATHER_STAGES].index(True)
            if stage not in self.passed:
                if stage == 0:
                    base, members, sems, fly = 0, head, (send_h, recv_h), flying_h
                else:
                    base = sum(len(s) for s in GATHER_STAGES[1:stage])
                    members, sems, fly = tail[base:base + len(GATHER_STAGES[stage])], (send_a, recv_a), flying
                self.passed[stage] = (members, gather_pass_on(fly[base:base + len(members)], sems[0], sems[1], after,
                                                              name=f"gather_pass_on_{stage}", base=base))
            return stage

        def weights(self, name, layer, after):
            stage = self.pass_on(name, layer, after)
            members, (bufs, send_b, recv_b, tok) = self.passed[stage]
            got = gather_finish(bufs, send_b, recv_b, tok, [lf[3].shape for lf in members],
                                name=f"gather_finish_{stage}")
            out = {}
            for (leaf_name, leaf_layer, cols, _), arr in zip(members, got):
                out[(leaf_name, leaf_layer)] = arr if cols else arr.reshape(N_SHARDS * arr.shape[1], arr.shape[2])
            return out

        groups = []

        def pair_start(self, grads_done, after):
            self.keys = sorted(grads_done)
            views = [view(k, grads_done[k]) for k in self.keys]
            self.pair, token = exchange_start(views, [(N_SHARDS,) + v.shape[2:] for v in views], F32, pair_plan,
                                              len(views), after, name=f"grad_pair_start_{len(self.groups)}")
            return token

        def chip_start(self, after):
            tag = len(self.groups)
            mine, theirs = exchange_finish(self.pair, after, name=f"grad_pair_finish_{tag}")
            wire = [add_to_wire(a, b, core_1, name=f"grad_pair_sum_{tag}_{i}")
                    for i, (a, b) in enumerate(zip(mine, theirs))]
            chip, token = exchange_start(wire, [(3,) + v.shape[1:] for v in wire], BF16, chip_plan, 3 * len(wire),
                                         theirs[-1], name=f"grad_chip_start_{tag}")
            self.groups.append((self.keys, chip))
            return token

    def view(key, arr):
        rows = arr.shape[-2] if col_sharded[key[0]] else arr.shape[0] // N_SHARDS
        return arr.reshape(N_SHARDS, 2, rows // 2, arr.shape[-1])

    t = x.shape[1]
    late = Late()
    loss_blk, dx, g = local_step(x[0], p.reshape(2, t, PLE_DIM), loss_target[0], w, late)

    sent = {k for keys, _ in late.groups for k in keys}
    keys_last = [(name, layer) for name, layers, _ in MATRICES for layer in range(max(layers, 1))
                 if (name, layer) not in sent]
    views = [view(k, g[k[0]][k[1]] if layer_count[k[0]] == 2 else g[k[0]]) for k in keys_last]

    theirs = pair_exchange(views, name="grad_pair_exchange_last")
    wire_0 = [add_to_wire(a, b, core_1, name=f"grad_pair_sum_last_{i}") for i, (a, b) in enumerate(zip(views, theirs))]
    chip_0, token_0 = exchange_start(wire_0, [(3,) + v.shape[1:] for v in wire_0], BF16, chip_plan, 3 * len(wire_0),
                                     theirs[-1], name="grad_chip_start_last")

    grads, bufs = {}, {}

    def sum_and_share(keys, wire, landed, tag):
        for i, (key, wv, lv) in enumerate(zip(keys, wire, landed)):
            name, layer = key
            bufs[name] = sum_chips(wv, lv, place, bufs.get(name), layer, layer_count[name],
                                   name=f"grad_chip_sum_{tag}_{i}")
        names = sorted({k[0] for k in keys})
        shared = pair_share([bufs[n] for n in names], [(names.index(k[0]), k[1]) for k in keys],
                            name=f"grad_pair_share_{tag}")
        bufs.update(zip(names, shared))

    updates = {}

    def update(n, gn, part=None):
        wn, mn, vn = weights[n], given["m_" + n], given["v_" + n]
        if wn.ndim == 1:
            wn, gn, mn, vn = (a.reshape(1, -1) for a in (wn, gn, mn, vn))
        tag = "" if part is None else f"_{part[0]}"
        updates[n] = adamw(wn, gn.reshape(wn.shape), mn, vn, name=f"adamw_{n}{tag}", part=part, dest=updates.get(n))

    after = token_0
    for tag, (keys, chip) in enumerate(late.groups + [(keys_last, chip_0)]):
        wire, landed = exchange_finish(chip, after, name=f"grad_chip_finish_{tag}")
        sum_and_share(keys, wire, landed, tag)
        for name, layer in keys:
            update(name, bufs[name], (layer, layer_count[name]) if layer_count[name] == 2 else None)
        after = updates[keys[-1][0]][0]

    small = REPLICATED + SHARDED_VECTORS
    flat = jnp.concatenate([g[n].reshape(-1) for n in small] + [loss_blk[0, :1]])
    room = 8 * SMALL_ROWS * D_MODEL
    flat = jnp.concatenate([flat, jnp.zeros((room - flat.shape[0],), F32)])
    reduced = all_reduce_small(flat.reshape(8, SMALL_ROWS, D_MODEL), name="grad_small_all_reduce").reshape(-1)
    loss = reduced[sum(g[n].size for n in small)]
    at = 0
    for n in small:
        size = g[n].size
        piece = reduced[at:at + size]
        at += size
        if n in SHARDED_VECTORS:
            per = D_MODEL // N_SHARDS
            grads[n] = lax.dynamic_slice(piece, (shard * per,), (per,)).reshape(weights[n].shape)
        else:
            grads[n] = piece.reshape(weights[n].shape)
        update(n, grads[n])
    for name, _, _ in MATRICES:
        grads[name] = bufs[name].reshape(weights[name].shape)
    delta = {n: updates[n][0].reshape(weights[n].shape) for n in WEIGHTS}
    new_m = {n: updates[n][1].reshape(weights[n].shape) for n in WEIGHTS}
    new_v = {n: updates[n][2].reshape(weights[n].shape) for n in WEIGHTS}
    return (loss, dx.reshape(x.shape), *[grads[n] for n in WEIGHTS], *[delta[n] for n in WEIGHTS],
            *[new_m[n] for n in WEIGHTS], *[new_v[n] for n in WEIGHTS])
```

```python
import jax
import jax.numpy as jnp
from jax import lax
from jax.experimental import pallas as pl
from jax.experimental.pallas import tpu as pltpu

F32 = jnp.float32
BF16 = jnp.bfloat16

D_MODEL = 1024
D_FF = 4096
PLE_DIM = 256
N_GROUPS = 8
CHUNK = 128
HEAD_DIM = 64
LANES = 128
ATT_K_BLOCK = 256
ATT_Q_BLOCK = 512
EPS = 1e-6
N_SHARDS = 4
VMEM_LIMIT = 56 * 1024 * 1024

ADAM_LR = 0.001
ADAM_B1 = 0.9
ADAM_B2 = 0.999
ADAM_EPS = 1e-08
ADAM_WD = 0.01
ADAM_STEP = 10

MESH = pl.DeviceIdType.MESH


_PREVIOUS = []


def _in_order(make, in_specs, args, views_of=()):
    previous = _PREVIOUS[-1] if _PREVIOUS else None
    if previous is not None and any(a is previous for a in (*args, *views_of)):
        previous = None
    if previous is None:
        result = make(lambda body: body, list(in_specs))(*args)
    else:
        count = len(args)

        def skip(body):
            return lambda *refs: body(*refs[:count], *refs[count + 1:])

        result = make(skip, list(in_specs) + [pl.BlockSpec(memory_space=pl.ANY)])(*args, previous)
    _PREVIOUS[:] = [jax.tree_util.tree_leaves(result)[-1]]
    return result


def _pcall(body, *, name, out_shape, grid=None, in_specs=None, out_specs=None, scratch_shapes=(),
           semantics=None, aliases=None, side_effects=False, num_prefetch=0):
    params = dict(vmem_limit_bytes=VMEM_LIMIT)
    if semantics is not None:
        params["dimension_semantics"] = semantics
    if side_effects:
        params["has_side_effects"] = True
    kwargs = {}
    if aliases:
        kwargs["input_output_aliases"] = aliases

    def make(wrap, specs):
        body_ = wrap(body)
        if num_prefetch:
            spec = pltpu.PrefetchScalarGridSpec(num_scalar_prefetch=num_prefetch, grid=grid, in_specs=specs,
                                                out_specs=out_specs, scratch_shapes=list(scratch_shapes))
            return pl.pallas_call(body_, name=name, out_shape=out_shape, grid_spec=spec,
                                  compiler_params=pltpu.CompilerParams(**params), **kwargs)
        more = dict(kwargs, in_specs=specs)
        if grid is not None:
            more["grid"] = grid
        if out_specs is not None:
            more["out_specs"] = out_specs
        return pl.pallas_call(body_, name=name, out_shape=out_shape, scratch_shapes=list(scratch_shapes),
                              compiler_params=pltpu.CompilerParams(**params), **more)

    return lambda *args: _in_order(make, in_specs, args)


def _sds(shape, dtype):
    return jax.ShapeDtypeStruct(shape, dtype)


_GELU_C = 0.7978845608028654
_GELU_A = 0.044715


def _gelu(x):
    inner = _GELU_C * (x + _GELU_A * (x * x * x))
    return 0.5 * x * (1.0 + jnp.tanh(inner))


def _gelu_grad(x):
    x2 = x * x
    t = jnp.tanh(_GELU_C * (x + _GELU_A * (x2 * x)))
    return 0.5 * (1.0 + t) + 0.5 * x * (1.0 - t * t) * (_GELU_C * (1.0 + 3.0 * _GELU_A * x2))


def _sigmoid(x):
    return 1.0 / (1.0 + jnp.exp(-x))


def _log_sigmoid(z):
    return jnp.minimum(z, 0.0) - jnp.log(1.0 + jnp.exp(-jnp.abs(z)))


def _dot(a, b):
    return jnp.dot(a, b, preferred_element_type=F32)


def _dot_nt(a, b):
    return lax.dot_general(a, b, (((1,), (1,)), ((), ())), preferred_element_type=F32)


def _dot_tn(a, b):
    return lax.dot_general(a, b, (((0,), (0,)), ((), ())), preferred_element_type=F32)


def _head_rstd(x):
    lane = lax.broadcasted_iota(jnp.int32, x.shape, 1)
    low = lane < HEAD_DIM
    sq = x * x
    s_lo = jnp.sum(jnp.where(low, sq, 0.0), axis=-1, keepdims=True)
    s_hi = jnp.sum(jnp.where(low, 0.0, sq), axis=-1, keepdims=True)
    ms = jnp.where(low, s_lo, s_hi) * (1.0 / HEAD_DIM)
    return lax.rsqrt(ms + EPS)


def _head_mean(x):
    lane = lax.broadcasted_iota(jnp.int32, x.shape, 1)
    low = lane < HEAD_DIM
    s_lo = jnp.sum(jnp.where(low, x, 0.0), axis=-1, keepdims=True)
    s_hi = jnp.sum(jnp.where(low, 0.0, x), axis=-1, keepdims=True)
    return jnp.where(low, s_lo, s_hi) * (1.0 / HEAD_DIM)


def _full(shape):
    zeros = (0,) * len(shape)
    return pl.BlockSpec(shape, lambda i: zeros)


def norm_matmul(x, g, w, *, name, epilogue="none", head_gain=None, head_scale=1.0, tm=512):
    t, d = x.shape
    sharded = w.ndim == 3
    per = w.shape[2] if sharded else w.shape[1]
    n = N_SHARDS * per if sharded else per
    tm = min(tm, t)
    heads = epilogue == "heads"

    def body(x_ref, g_ref, w_ref, *rest):
        if heads:
            hg_ref, rest = rest[0], rest[1:]
        h_ref, r_ref, outs = rest[0], rest[1], rest[2:]
        xv = x_ref[...]
        r = lax.rsqrt(jnp.mean(xv * xv, axis=-1, keepdims=True) + EPS)
        h = ((xv * r) * g_ref[...]).astype(BF16)
        h_ref[...] = h
        r_ref[...] = r
        for s in range(N_SHARDS if sharded else 1):
            cols = slice(s * per, (s + 1) * per)
            y = _dot(h, w_ref[s] if sharded else w_ref[...])
            if epilogue == "relu2":
                a = jnp.maximum(y, 0.0)
                outs[0][:, cols] = a.astype(BF16)
                outs[1][:, cols] = (a * a).astype(BF16)
                continue
            outs[0][:, cols] = y
            if heads:
                gain = hg_ref[...] * head_scale
                for b in range(per // LANES):
                    at = s * per + b * LANES
                    yb = y[:, b * LANES:(b + 1) * LANES]
                    if at < D_MODEL:
                        outs[1][:, at:at + LANES] = ((yb * _head_rstd(yb)) * gain).astype(BF16)
                    else:
                        outs[2][:, at - D_MODEL:at - D_MODEL + LANES] = yb.astype(BF16)

    row = lambda i: (i, 0)
    in_specs = [pl.BlockSpec((tm, d), row), _full((1, d)), _full(w.shape)]
    args = [x, g, w]
    out_shape = [_sds((t, d), BF16), _sds((t, 1), F32)]
    out_specs = [pl.BlockSpec((tm, d), row), pl.BlockSpec((tm, 1), row)]
    if epilogue == "relu2":
        out_shape += [_sds((t, n), BF16), _sds((t, n), BF16)]
        out_specs += [pl.BlockSpec((tm, n), row)] * 2
    else:
        out_shape.append(_sds((t, n), F32))
        out_specs.append(pl.BlockSpec((tm, n), row))
    if heads:
        in_specs.append(_full((1, LANES)))
        args.append(head_gain)
        for width in [D_MODEL] + ([n - D_MODEL] if n > D_MODEL else []):
            out_shape.append(_sds((t, width), BF16))
            out_specs.append(pl.BlockSpec((tm, width), row))
    return _pcall(body, name=name, out_shape=out_shape, grid=(t // tm,), in_specs=in_specs, out_specs=out_specs,
                  semantics=("parallel",))(*args)


def mlp_forward_fused(x, g, w_up, w_down, *, name, tm=256):
    t, d = x.shape
    per = w_up.shape[2]
    n = N_SHARDS * per
    tm = min(tm, t)

    def body(x_ref, g_ref, wu_ref, wd_ref, h_ref, r_ref, a_ref, a2_ref, o_ref):
        xv = x_ref[...]
        r = lax.rsqrt(jnp.mean(xv * xv, axis=-1, keepdims=True) + EPS)
        h = ((xv * r) * g_ref[...]).astype(BF16)
        h_ref[...] = h
        r_ref[...] = r
        acc = xv
        for s in range(N_SHARDS):
            cols = slice(s * per, (s + 1) * per)
            a = jnp.maximum(_dot(h, wu_ref[s]), 0.0)
            a2 = (a * a).astype(BF16)
            a_ref[:, cols] = a.astype(BF16)
            a2_ref[:, cols] = a2
            acc = acc + _dot(a2, wd_ref[cols, :])
        o_ref[...] = acc

    row = lambda i: (i, 0)
    once = lambda shape: pl.BlockSpec(shape, lambda i: (0,) * len(shape), pipeline_mode=pl.Buffered(1))
    h, r, a, a2, out = _pcall(
        body, name=name,
        out_shape=[_sds((t, d), BF16), _sds((t, 1), F32), _sds((t, n), BF16), _sds((t, n), BF16), _sds((t, d), F32)],
        grid=(t // tm,),
        in_specs=[pl.BlockSpec((tm, d), row), _full((1, d)), once(w_up.shape), once(w_down.shape)],
        out_specs=[pl.BlockSpec((tm, d), row), pl.BlockSpec((tm, 1), row), pl.BlockSpec((tm, n), row),
                   pl.BlockSpec((tm, n), row), pl.BlockSpec((tm, d), row)],
        semantics=("parallel",))(x, g, w_up, w_down)
    return out, (x, h, r, a, a2)


def matmul_residual(a, w, res, *, name, tm=512):
    t, k = a.shape
    n = w.shape[1]
    tm = min(tm, t)

    def body(a_ref, w_ref, res_ref, o_ref):
        o_ref[...] = res_ref[...] + _dot(a_ref[...], w_ref[...])

    row = lambda i: (i, 0)
    return _pcall(
        body, name=name, out_shape=_sds((t, n), F32), grid=(t // tm,),
        in_specs=[pl.BlockSpec((tm, k), row), _full(w.shape), pl.BlockSpec((tm, n), row)],
        out_specs=pl.BlockSpec((tm, n), row), semantics=("parallel",))(a, w, res)


def ple_forward(x, g, w_gate, p, w_proj, *, name, tm=256):
    t, d = x.shape
    tm = min(tm, t)

    def body(x_ref, g_ref, wg_ref, p_ref, wp_ref, h_ref, r_ref, gate_ref, pp_ref, o_ref):
        xv = x_ref[...]
        r = lax.rsqrt(jnp.mean(xv * xv, axis=-1, keepdims=True) + EPS)
        h = ((xv * r) * g_ref[...]).astype(BF16)
        h_ref[...] = h
        r_ref[...] = r
        gate = _sigmoid(_dot(h, wg_ref[...]))
        gate_ref[...] = gate
        pb = p_ref[...].astype(BF16)
        per = d // N_SHARDS
        for s in range(N_SHARDS):
            cols = slice(s * per, (s + 1) * per)
            pp = _dot(pb, wp_ref[s])
            pp_ref[:, cols] = pp.astype(BF16)
            o_ref[:, cols] = xv[:, cols] + pp * gate[:, cols]

    row = lambda i: (i, 0)
    fixed = lambda i: (0, 0)
    return _pcall(
        body, name=name,
        out_shape=[_sds((t, d), BF16), _sds((t, 1), F32), _sds((t, d), F32), _sds((t, d), BF16), _sds((t, d), F32)],
        grid=(t // tm,),
        in_specs=[pl.BlockSpec((tm, d), row), pl.BlockSpec((1, d), fixed), pl.BlockSpec((d, d), fixed),
                  pl.BlockSpec((tm, PLE_DIM), row),
                  pl.BlockSpec((N_SHARDS, PLE_DIM, d // N_SHARDS), lambda i: (0, 0, 0))],
        out_specs=[pl.BlockSpec((tm, d), row), pl.BlockSpec((tm, 1), row), pl.BlockSpec((tm, d), row),
                   pl.BlockSpec((tm, d), row), pl.BlockSpec((tm, d), row)],
        semantics=("parallel",))(x, g, w_gate, p, w_proj)


def _tril_mask():
    r = lax.broadcasted_iota(jnp.int32, (CHUNK, CHUNK), 0)
    c = lax.broadcasted_iota(jnp.int32, (CHUNK, CHUNK), 1)
    return c <= r


def _sgu_common(pre_ref, gv_ref, ws_ref):
    pre = pre_ref[...]
    pre_u, pre_v = pre[:, :D_MODEL], pre[:, D_MODEL:]
    u = _gelu(pre_u)
    v = _gelu(pre_v)
    r = lax.rsqrt(jnp.mean(v * v, axis=-1, keepdims=True) + EPS)
    vhat = v * r
    vn = (vhat * gv_ref[...]).astype(BF16)
    tril = _tril_mask()
    wm = [jnp.where(tril, ws_ref[g], 0.0).astype(BF16) for g in range(N_GROUPS)]
    return pre_u, pre_v, u, r, vhat, vn, wm, tril


def sgu_forward(pre, g_v, w_s, b_full, *, name):
    t = pre.shape[0]

    def body(pre_ref, gv_ref, ws_ref, b_ref, y_ref):
        _, _, u, _, _, vn, wm, _ = _sgu_common(pre_ref, gv_ref, ws_ref)
        for g in range(N_GROUPS):
            cols = slice(g * LANES, (g + 1) * LANES)
            mix = _dot(wm[g], vn[:, cols]) + b_ref[:, cols]
            y_ref[:, cols] = (u[:, cols] * mix).astype(BF16)

    return _pcall(
        body, name=name, out_shape=_sds((t, D_MODEL), BF16), grid=(t // CHUNK,),
        in_specs=[pl.BlockSpec((CHUNK, 2 * D_MODEL), lambda i: (i, 0)), pl.BlockSpec((1, D_MODEL), lambda i: (0, 0)),
                  pl.BlockSpec((N_GROUPS, CHUNK, CHUNK), lambda i: (0, 0, 0)),
                  pl.BlockSpec((CHUNK, D_MODEL), lambda i: (0, 0))],
        out_specs=pl.BlockSpec((CHUNK, D_MODEL), lambda i: (i, 0)),
        semantics=("parallel",))(pre, g_v, w_s, b_full)


def _suffix_matrix(n):
    r = lax.broadcasted_iota(jnp.int32, (n, n), 0)
    c = lax.broadcasted_iota(jnp.int32, (n, n), 1)
    return jnp.where(r > c, 1.0, 0.0).astype(BF16)


def _prefix_matrix(n):
    r = lax.broadcasted_iota(jnp.int32, (n, n), 0)
    c = lax.broadcasted_iota(jnp.int32, (n, n), 1)
    return jnp.where(r < c, 1.0, 0.0).astype(BF16)


def _block_cumsum(a, tri):
    return _dot(a.astype(BF16), tri)


def _stacked_causal(nq, nk, shift):
    r = lax.broadcasted_iota(jnp.int32, (2 * nq, nk), 0)
    c = lax.broadcasted_iota(jnp.int32, (2 * nq, nk), 1)
    return c + shift < jnp.where(r >= nq, r - nq, r)


def _att_blocks(t):
    bq, bk = min(ATT_Q_BLOCK, t), min(ATT_K_BLOCK, t)
    return bq, bk, bq // bk


def _stack_heads(a, low):
    zero = jnp.zeros_like(a)
    return jnp.concatenate([jnp.where(low, a, zero), jnp.where(low, zero, a)], axis=0)


def stick_breaking_forward(q, k, v, *, name):
    t = q.shape[0]
    bq, bk, ratio = _att_blocks(t)

    def body(q_ref, k_ref, v_ref, o_ref):
        i = pl.program_id(1)
        low = lax.broadcasted_iota(jnp.int32, (bq, LANES), 1) < HEAD_DIM
        tri = _suffix_matrix(bk)
        qs = _stack_heads(q_ref[...], low)

        def block(j, carry, acc, causal=None):
            rows = pl.ds(pl.multiple_of(j * bk, bk), bk)
            z = _dot_nt(qs, k_ref[rows, :])
            ls = _log_sigmoid(z)
            lg = ls - z
            if causal is not None:
                lg = jnp.where(causal, lg, 0.0)
            s = ls + _block_cumsum(lg, tri) + carry
            a = jnp.exp(s)
            if causal is not None:
                a = jnp.where(causal, a, 0.0)
            acc = acc + _dot(a.astype(BF16), v_ref[rows, :])
            return carry + jnp.sum(lg, axis=-1, keepdims=True), acc

        state = (jnp.zeros((2 * bq, 1), F32), jnp.zeros((2 * bq, LANES), F32))
        for m in reversed(range(ratio)):
            state = block(ratio * i + m, state[0], state[1], _stacked_causal(bq, bk, m * bk))
        first = ratio * i

        def two_blocks(n, st):
            st = block(first - 1 - 2 * n, st[0], st[1])
            return block(first - 2 - 2 * n, st[0], st[1])

        state = lax.fori_loop(0, first // 2, two_blocks, state)
        _, acc = lax.fori_loop(0, first % 2, lambda n, st: block(0, st[0], st[1]), state)
        o_ref[...] = jnp.where(low, acc[:bq], acc[bq:]).astype(BF16)

    return _pcall(
        body, name=name, out_shape=_sds((t, D_MODEL), BF16), grid=(D_MODEL // LANES, t // bq),
        in_specs=[pl.BlockSpec((bq, LANES), lambda p, i: (i, p)), pl.BlockSpec((t, LANES), lambda p, i: (0, p)),
                  pl.BlockSpec((t, LANES), lambda p, i: (0, p))],
        out_specs=pl.BlockSpec((bq, LANES), lambda p, i: (i, p)),
        semantics=("parallel", "arbitrary"))(q, k, v)


def loss_forward(x, target, *, name, tm=512):
    t, d = x.shape
    tm = min(tm, t)

    def body(x_ref, t_ref, l_ref, dx_ref):
        @pl.when(pl.program_id(0) == 0)
        def _():
            l_ref[...] = jnp.zeros_like(l_ref)

        diff = x_ref[...] - t_ref[...]
        dx_ref[...] = diff * (1.0 / d)
        l_ref[...] += 0.5 * jnp.sum(jnp.mean(diff * diff, axis=-1, keepdims=True))

    return _pcall(
        body, name=name, out_shape=[_sds((8, LANES), F32), _sds((t, d), F32)], grid=(t // tm,),
        in_specs=[pl.BlockSpec((tm, d), lambda i: (i, 0))] * 2,
        out_specs=[pl.BlockSpec((8, LANES), lambda i: (0, 0)), pl.BlockSpec((tm, d), lambda i: (i, 0))],
        semantics=("arbitrary",))(x, target)


def matmul_nt(dy, w, *, name, mul=None, out_dtype=F32, tm=512):
    t, n = dy.shape
    k = w.shape[0]
    tm = min(tm, t)

    def body(*refs):
        if mul is None:
            dy_ref, w_ref, o_ref = refs
        else:
            dy_ref, w_ref, m_ref, o_ref = refs
        y = _dot_nt(dy_ref[...].astype(BF16), w_ref[...])
        if mul is not None:
            y = y * (2.0 * m_ref[...].astype(F32))
        o_ref[...] = y.astype(out_dtype)

    row = lambda i: (i, 0)
    in_specs = [pl.BlockSpec((tm, n), row), _full(w.shape)]
    args = [dy, w]
    if mul is not None:
        in_specs.append(pl.BlockSpec((tm, k), row))
        args.append(mul)
    return _pcall(body, name=name, out_shape=_sds((t, k), out_dtype), grid=(t // tm,), in_specs=in_specs,
                  out_specs=pl.BlockSpec((tm, k), row), semantics=("parallel",))(*args)


def matmul_tn(a, dy, *, name, col_shards, tk=512):
    t, k = a.shape
    n = dy.shape[1]
    if col_shards:
        tn = n // N_SHARDS

        def body(a_ref, dy_ref, o_ref):
            o_ref[...] = _dot_tn(a_ref[...].astype(BF16), dy_ref[...].astype(BF16))

        return _pcall(body, name=name, out_shape=_sds((N_SHARDS, k, tn), F32), grid=(N_SHARDS,),
                      in_specs=[_full((t, k)), pl.BlockSpec((t, tn), lambda j: (0, j))],
                      out_specs=pl.BlockSpec((None, k, tn), lambda j: (j, 0, 0)), semantics=("parallel",))(a, dy)

    tk = min(tk, k)

    def body(a_ref, dy_ref, o_ref, dy_bf):
        @pl.when(pl.program_id(0) == 0)
        def _():
            dy_bf[...] = dy_ref[...].astype(BF16)

        o_ref[...] = _dot_tn(a_ref[...].astype(BF16), dy_bf[...])

    return _pcall(body, name=name, out_shape=_sds((k, n), F32), grid=(k // tk,),
                  in_specs=[pl.BlockSpec((t, tk), lambda i: (0, i)), _full((t, n))],
                  out_specs=pl.BlockSpec((tk, n), lambda i: (i, 0)),
                  scratch_shapes=[pltpu.VMEM((t, n), BF16)], semantics=("arbitrary",))(a, dy)


def norm_backward(dpre, w, x, g, rstd, dx_out, *, name, tm=512):
    t, d = x.shape
    n = dpre.shape[1]
    tm = min(tm, t)
    if w.ndim == 3:
        w_spec = pl.BlockSpec(w.shape, lambda i: (0, 0, 0))
    else:
        w_spec = pl.BlockSpec(w.shape, lambda i: (0, 0))

    def body(dp_ref, w_ref, x_ref, g_ref, r_ref, dxo_ref, dx_ref, dg_ref):
        @pl.when(pl.program_id(0) == 0)
        def _():
            dg_ref[...] = jnp.zeros_like(dg_ref)

        if w.ndim == 3:
            per = n // N_SHARDS
            dh = _dot_nt(dp_ref[:, 0:per], w_ref[0])
            for s in range(1, N_SHARDS):
                dh = dh + _dot_nt(dp_ref[:, s * per:(s + 1) * per], w_ref[s])
        else:
            dh = _dot_nt(dp_ref[...], w_ref[...])
        r = r_ref[...]
        xn = x_ref[...] * r
        dg_ref[...] += jnp.sum(dh * xn, axis=0, keepdims=True)
        dxn = dh * g_ref[...]
        dx = r * (dxn - xn * jnp.mean(dxn * xn, axis=-1, keepdims=True))
        dx_ref[...] = dxo_ref[...] + dx

    row = lambda i: (i, 0)
    fixed = lambda i: (0, 0)
    return _pcall(
        body, name=name, out_shape=[_sds((t, d), F32), _sds((1, d), F32)], grid=(t // tm,),
        in_specs=[pl.BlockSpec((tm, n), row), w_spec, pl.BlockSpec((tm, d), row),
                  pl.BlockSpec((1, d), fixed), pl.BlockSpec((tm, 1), row), pl.BlockSpec((tm, d), row)],
        out_specs=[pl.BlockSpec((tm, d), row), pl.BlockSpec((1, d), fixed)],
        semantics=("arbitrary",))(dpre, w, x, g, rstd, dx_out)


def ple_backward(dx, gate, pp, w_gate, x, g, rstd, *, name, tm=512):
    t, d = dx.shape
    tm = min(tm, t)

    def body(dx_ref, gate_ref, pp_ref, w_ref, x_ref, g_ref, r_ref, dxn_ref, dg_ref, dgate_ref, dproj_ref):
        @pl.when(pl.program_id(0) == 0)
        def _():
            dg_ref[...] = jnp.zeros_like(dg_ref)

        dxv = dx_ref[...]
        gate = gate_ref[...]
        dgate = (dxv * pp_ref[...].astype(F32) * (gate * (1.0 - gate))).astype(BF16)
        dgate_ref[...] = dgate
        dproj_ref[...] = (dxv * gate).astype(BF16)
        dh = _dot_nt(dgate, w_ref[...])
        r = r_ref[...]
        xn = x_ref[...] * r
        dg_ref[...] += jnp.sum(dh * xn, axis=0, keepdims=True)
        dxn = dh * g_ref[...]
        dxn_ref[...] = dxv + r * (dxn - xn * jnp.mean(dxn * xn, axis=-1, keepdims=True))

    row = lambda i: (i, 0)
    blk = pl.BlockSpec((tm, d), row)
    return _pcall(
        body, name=name, out_shape=[_sds((t, d), F32), _sds((1, d), F32), _sds((t, d), BF16), _sds((t, d), BF16)],
        grid=(t // tm,),
        in_specs=[blk, blk, blk, _full(w_gate.shape), blk, _full((1, d)), pl.BlockSpec((tm, 1), row)],
        out_specs=[blk, _full((1, d)), blk, blk], semantics=("arbitrary",))(dx, gate, pp, w_gate, x, g, rstd)


def sgu_backward(dy, pre, g_v, w_s, b_full, *, name):
    t = pre.shape[0]
    n_chunks = t // CHUNK

    def body(dy_ref, pre_ref, gv_ref, ws_ref, b_ref, dpre_ref, dws_ref, db_ref, dgv_ref, dvn_s, dbf_s):
        step = pl.program_id(0)

        @pl.when(step == 0)
        def _():
            dws_ref[...] = jnp.zeros_like(dws_ref)
            dgv_ref[...] = jnp.zeros_like(dgv_ref)
            dbf_s[...] = jnp.zeros_like(dbf_s)

        pre_u, pre_v, u, r, vhat, vn, wm, tril = _sgu_common(pre_ref, gv_ref, ws_ref)
        dyv = dy_ref[...]
        for g in range(N_GROUPS):
            cols = slice(g * LANES, (g + 1) * LANES)
            mix = _dot(wm[g], vn[:, cols]) + b_ref[:, cols]
            dmix = dyv[:, cols] * u[:, cols]
            dmix_b = dmix.astype(BF16)
            du = dyv[:, cols] * mix
            dpre_ref[:, cols] = (du * _gelu_grad(pre_u[:, cols])).astype(BF16)
            dws_ref[g] += jnp.where(tril, _dot_nt(dmix_b, vn[:, cols]), 0.0)
            dbf_s[:, cols] += dmix
            dvn_s[:, cols] = _dot_tn(wm[g], dmix_b)
        dvn = dvn_s[...]
        dgv_ref[...] += jnp.sum(dvn * vhat, axis=0, keepdims=True)
        dxn = dvn * gv_ref[...]
        dv = r * (dxn - vhat * jnp.mean(dxn * vhat, axis=-1, keepdims=True))
        dpre_ref[:, D_MODEL:] = (dv * _gelu_grad(pre_v)).astype(BF16)

        @pl.when(step == n_chunks - 1)
        def _():
            lane = lax.broadcasted_iota(jnp.int32, (CHUNK, LANES), 1)
            acc = jnp.zeros((CHUNK, LANES), F32)
            for g in range(N_GROUPS):
                s = jnp.sum(dbf_s[:, g * LANES:(g + 1) * LANES], axis=-1, keepdims=True)
                acc = jnp.where(lane == g, s, acc)
            db_ref[...] = acc

    fixed2 = lambda i: (0, 0)
    return _pcall(
        body, name=name,
        out_shape=[_sds((t, 2 * D_MODEL), BF16), _sds((N_GROUPS, CHUNK, CHUNK), F32), _sds((CHUNK, LANES), F32),
                   _sds((1, D_MODEL), F32)],
        grid=(n_chunks,),
        in_specs=[pl.BlockSpec((CHUNK, D_MODEL), lambda i: (i, 0)), pl.BlockSpec((CHUNK, 2 * D_MODEL), lambda i: (i, 0)),
                  pl.BlockSpec((1, D_MODEL), fixed2), pl.BlockSpec((N_GROUPS, CHUNK, CHUNK), lambda i: (0, 0, 0)),
                  pl.BlockSpec((CHUNK, D_MODEL), fixed2)],
        out_specs=[pl.BlockSpec((CHUNK, 2 * D_MODEL), lambda i: (i, 0)),
                   pl.BlockSpec((N_GROUPS, CHUNK, CHUNK), lambda i: (0, 0, 0)), pl.BlockSpec((CHUNK, LANES), fixed2),
                   pl.BlockSpec((1, D_MODEL), fixed2)],
        scratch_shapes=[pltpu.VMEM((CHUNK, D_MODEL), F32), pltpu.VMEM((CHUNK, D_MODEL), F32)],
        semantics=("arbitrary",))(dy, pre, g_v, w_s, b_full)


def head_norm_backward(dy, pre, g128, *, name, col_block=0, scale=1.0, passthrough=None, tm=512):
    t = dy.shape[0]
    tm = min(tm, t)
    width = 2 * D_MODEL if passthrough is not None else D_MODEL

    def body(*refs):
        if passthrough is not None:
            dy_ref, x_ref, g_ref, dv_ref, o_ref, dg_ref = refs
            o_ref[:, D_MODEL:] = dv_ref[...].astype(BF16)
        else:
            dy_ref, x_ref, g_ref, o_ref, dg_ref = refs

        @pl.when(pl.program_id(0) == 0)
        def _():
            dg_ref[...] = jnp.zeros_like(dg_ref)

        g = g_ref[...]
        dg = jnp.zeros((1, LANES), F32)
        for b in range(D_MODEL // LANES):
            cols = slice(b * LANES, (b + 1) * LANES)
            xv = x_ref[:, cols]
            r = _head_rstd(xv)
            xn = xv * r
            dyv = dy_ref[:, cols] * scale
            dg = dg + jnp.sum(dyv * xn, axis=0, keepdims=True)
            dxn = dyv * g
            o_ref[:, cols] = (r * (dxn - xn * _head_mean(dxn * xn))).astype(BF16)
        dg_ref[...] += dg

    row = lambda i: (i, 0)
    in_specs = [pl.BlockSpec((tm, D_MODEL), row), pl.BlockSpec((tm, D_MODEL), lambda i: (i, col_block)),
                pl.BlockSpec((1, LANES), lambda i: (0, 0))]
    args = [dy, pre, g128]
    if passthrough is not None:
        in_specs.append(pl.BlockSpec((tm, D_MODEL), row))
        args.append(passthrough)
    return _pcall(body, name=name, out_shape=[_sds((t, width), BF16), _sds((1, LANES), F32)], grid=(t // tm,),
                  in_specs=in_specs,
                  out_specs=[pl.BlockSpec((tm, width), row), pl.BlockSpec((1, LANES), lambda i: (0, 0))],
                  semantics=("arbitrary",))(*args)


def stick_breaking_backward(q, k, v, do, *, name):
    t = q.shape[0]
    bq, bk, ratio = _att_blocks(t)

    def body(q_ref, k_ref, v_ref, do_ref, dq_ref, dk_ref, dv_ref, s_buf, sg_buf):
        i = pl.program_id(1)

        @pl.when(i == 0)
        def _():
            dk_ref[...] = jnp.zeros_like(dk_ref)
            dv_ref[...] = jnp.zeros_like(dv_ref)

        low = lax.broadcasted_iota(jnp.int32, (bq, LANES), 1) < HEAD_DIM
        suffix = _suffix_matrix(bk)
        prefix = _prefix_matrix(bk)
        qs = _stack_heads(q_ref[...], low)
        dos = _stack_heads(do_ref[...], low)
        first = ratio * i

        def log_weights(j, carry, causal=None):
            rows = pl.ds(pl.multiple_of(j * bk, bk), bk)
            z = _dot_nt(qs, k_ref[rows, :])
            ls = _log_sigmoid(z)
            lg = ls - z
            if causal is not None:
                lg = jnp.where(causal, lg, 0.0)
            s_buf[j] = ls + _block_cumsum(lg, suffix) + carry
            sg_buf[j] = jnp.exp(ls)
            return carry + jnp.sum(lg, axis=-1, keepdims=True)

        carry = jnp.zeros((2 * bq, 1), F32)
        for m in reversed(range(ratio)):
            carry = log_weights(first + m, carry, _stacked_causal(bq, bk, m * bk))
        carry = lax.fori_loop(0, first // 2,
                              lambda n, c: log_weights(first - 2 - 2 * n, log_weights(first - 1 - 2 * n, c)), carry)
        lax.fori_loop(0, first % 2, lambda n, c: log_weights(0, c), carry)

        def grads(j, pcarry, dq_acc, causal=None):
            rows = pl.ds(pl.multiple_of(j * bk, bk), bk)
            a = jnp.exp(s_buf[j])
            if causal is not None:
                a = jnp.where(causal, a, 0.0)
            sg = sg_buf[j]
            ds = _dot_nt(dos, v_ref[rows, :]) * a
            before = _block_cumsum(ds, prefix) + pcarry
            if causal is not None:
                before = jnp.where(causal, before, 0.0)
            dz = (ds - sg * (ds + before)).astype(BF16)
            dq_acc = dq_acc + _dot(dz, k_ref[rows, :])
            dk_ref[rows, :] += _dot_tn(dz, qs)
            dv_ref[rows, :] += _dot_tn(a.astype(BF16), dos)
            return pcarry + jnp.sum(ds, axis=-1, keepdims=True), dq_acc

        def two_blocks(n, st):
            st = grads(2 * n, st[0], st[1])
            return grads(2 * n + 1, st[0], st[1])

        state = lax.fori_loop(0, first // 2, two_blocks,
                              (jnp.zeros((2 * bq, 1), F32), jnp.zeros((2 * bq, LANES), F32)))
        state = lax.fori_loop(0, first % 2, lambda n, st: grads(first - 1, st[0], st[1]), state)
        for m in range(ratio):
            state = grads(first + m, state[0], state[1], _stacked_causal(bq, bk, m * bk))
        dq_ref[...] = jnp.where(low, state[1][:bq], state[1][bq:])

    full = pl.BlockSpec((t, LANES), lambda p, i: (0, p))
    qblk = pl.BlockSpec((bq, LANES), lambda p, i: (i, p))
    return _pcall(
        body, name=name, out_shape=[_sds((t, D_MODEL), F32)] * 3, grid=(D_MODEL // LANES, t // bq),
        in_specs=[qblk, full, full, qblk], out_specs=[qblk, full, full],
        scratch_shapes=[pltpu.VMEM((t // bk, 2 * bq, bk), F32), pltpu.VMEM((t // bk, 2 * bq, bk), F32)],
        semantics=("parallel", "arbitrary"))(q, k, v, do)


def _mlp_backward(dx, saved, g, w_up, w_down, tag):
    x, h, r, a, a2 = saved
    d_w_down = matmul_tn(a2, dx, name=f"d_w_down_{tag}", col_shards=False)
    dpre = matmul_nt(dx, w_down, name=f"d_mlp_act_{tag}", mul=a, out_dtype=BF16)
    d_w_up = matmul_tn(h, dpre, name=f"d_w_up_{tag}", col_shards=True)
    dx, d_g = norm_backward(dpre, w_up, x, g, r, dx, name=f"d_mlp_norm_{tag}")
    return dx, d_w_up, d_w_down, d_g


def _ple_backward(dx, saved, p, g, w_gate, tag):
    x, h, r, gate, pp = saved
    dx, d_g, dgate, dproj = ple_backward(dx, gate, pp, w_gate, x, g, r, name=f"d_ple_{tag}")
    d_w_proj = matmul_tn(p, dproj, name=f"d_w_ple_proj_{tag}", col_shards=True)
    d_w_gate = matmul_tn(h, dgate, name=f"d_w_ple_gate_{tag}", col_shards=False)
    return dx, d_w_gate, d_w_proj, d_g


def local_step(x, p, target, w, late=None):
    row = lambda v: v.reshape(1, -1)
    g128 = lambda v: jnp.tile(v.reshape(1, HEAD_DIM), (1, 2))
    scale = HEAD_DIM ** -0.5
    b_full = jnp.repeat(jnp.transpose(w["b_spatial"][0]), LANES, axis=1)
    w_s = w["w_spatial"][0]

    mats = {}
    for name, value in w.items():
        if isinstance(value, tuple):
            mats.update({(name, layer): v for layer, v in enumerate(value)})
    if "w_kv" in w:
        mats[("w_kv", 0)] = w["w_kv"]

    def fetch(name, layer, after):
        if (name, layer) not in mats:
            mats.update(late.weights(name, layer, after))
        return mats[(name, layer)]

    def mlp_forward(x_in, layer):
        if layer == 1:
            return mlp_forward_fused(x_in, row(w["ln_mlp"][1]), fetch("w_up", 1, x_in), fetch("w_down", 1, x_in),
                                     name="mlp_1")
        h, r, a, a2 = norm_matmul(x_in, row(w["ln_mlp"][layer]), fetch("w_up", layer, x_in), name=f"mlp_up_{layer}",
                                  epilogue="relu2")
        return matmul_residual(a2, fetch("w_down", layer, a2), x_in, name=f"mlp_down_{layer}"), (x_in, h, r, a, a2)

    def ple(x_in, layer):
        return ple_forward(x_in, row(w["ln_ple"][layer]), fetch("w_ple_gate", layer, x_in), p[layer],
                           fetch("w_ple_proj", layer, x_in), name=f"ple_{layer}")

    x0 = x
    h_a, r_a, pre_a = norm_matmul(x0, row(w["ln_mix_a"][0]), fetch("w_in_a", 0, x0), name="sgu_in")
    y_a = sgu_forward(pre_a, row(w["g_v_a"][0]), w_s, b_full, name="sgu_mix")
    x1 = matmul_residual(y_a, fetch("w_out_a", 0, y_a), x0, name="sgu_out")
    x2, mlp0 = mlp_forward(x1, 0)
    ple0 = ple(x2, 0)
    x3 = ple0[4]
    h_kv, r_kv, kv_pre, k_n, v_b = norm_matmul(x3, row(w["ln_kv"]), fetch("w_kv", 0, x3), name="kv_proj",
                                               epilogue="heads", head_gain=g128(w["g_k"]))
    h_q, r_q, q_pre, q_n = norm_matmul(x3, row(w["ln_mix_b"][0]), fetch("w_q", 0, k_n), name="q_proj",
                                       epilogue="heads", head_gain=g128(w["g_q"][0]), head_scale=scale)
    o = stick_breaking_forward(q_n, k_n, v_b, name="sb_fwd")
    if late is not None:
        late.pass_on("w_up", 1, o)
    x4 = matmul_residual(o, fetch("w_out_b", 0, o), x3, name="sb_out")
    x5, mlp1 = mlp_forward(x4, 1)
    ple1 = ple(x5, 1)
    x6 = ple1[4]
    loss_blk, dx = loss_forward(x6, target, name="loss")

    g = {}
    dx, dwg1, dwp1, dlnp1 = _ple_backward(dx, (x5,) + tuple(ple1[:4]), p[1], row(w["ln_ple"][1]),
                                          mats[("w_ple_gate", 1)], 1)
    dx, dwu1, dwd1, dlnm1 = _mlp_backward(dx, mlp1, row(w["ln_mlp"][1]), mats[("w_up", 1)], mats[("w_down", 1)], 1)
    g["w_out_b"] = matmul_tn(o, dx, name="d_w_out_b", col_shards=False)
    do = matmul_nt(dx, mats[("w_out_b", 0)], name="d_sb_out", out_dtype=BF16)
    dq_n, dk_n, dv = stick_breaking_backward(q_n, k_n, v_b, do, name="sb_bwd")
    dq_pre, dgq = head_norm_backward(dq_n, q_pre, g128(w["g_q"][0]), name="d_q_norm", scale=scale)
    dkv_pre, dgk = head_norm_backward(dk_n, kv_pre, g128(w["g_k"]), name="d_k_norm", passthrough=dv)
    g["w_q"] = matmul_tn(h_q, dq_pre, name="d_w_q", col_shards=False)
    g["w_kv"] = matmul_tn(h_kv, dkv_pre, name="d_w_kv", col_shards=True)
    dx, g["ln_mix_b"] = norm_backward(dq_pre, mats[("w_q", 0)], x3, row(w["ln_mix_b"][0]), r_q, dx, name="d_q_in")
    dx, g["ln_kv"] = norm_backward(dkv_pre, mats[("w_kv", 0)], x3, row(w["ln_kv"]), r_kv, dx, name="d_kv_in")
    g["g_q"] = dgq[:, :HEAD_DIM] + dgq[:, HEAD_DIM:]
    g["g_k"] = (dgk[:, :HEAD_DIM] + dgk[:, HEAD_DIM:]).reshape(HEAD_DIM)
    g["ln_kv"] = g["ln_kv"].reshape(D_MODEL)
    if late is not None:
        late.pair_start({("w_kv", 0): g["w_kv"], ("w_q", 0): g["w_q"], ("w_out_b", 0): g["w_out_b"],
                         ("w_up", 1): dwu1, ("w_down", 1): dwd1, ("w_ple_gate", 1): dwg1, ("w_ple_proj", 1): dwp1}, dx)
    dx, dwg0, dwp0, dlnp0 = _ple_backward(dx, (x2,) + tuple(ple0[:4]), p[0], row(w["ln_ple"][0]),
                                          mats[("w_ple_gate", 0)], 0)
    if late is not None:
        late.chip_start(dx)
    dx, dwu0, dwd0, dlnm0 = _mlp_backward(dx, mlp0, row(w["ln_mlp"][0]), mats[("w_up", 0)], mats[("w_down", 0)], 0)
    if late is not None:
        late.pair_start({("w_up", 0): dwu0, ("w_down", 0): dwd0, ("w_ple_gate", 0): dwg0, ("w_ple_proj", 0): dwp0}, dx)
    g["w_out_a"] = matmul_tn(y_a, dx, name="d_w_out_a", col_shards=False)
    dy_a = matmul_nt(dx, mats[("w_out_a", 0)], name="d_sgu_out")
    dpre_a, dws, db, g["g_v_a"] = sgu_backward(dy_a, pre_a, row(w["g_v_a"][0]), w_s, b_full, name="d_sgu_mix")
    if late is not None:
        late.chip_start(dpre_a)
    g["w_in_a"] = matmul_tn(h_a, dpre_a, name="d_w_in_a", col_shards=True)
    dx, g["ln_mix_a"] = norm_backward(dpre_a, mats[("w_in_a", 0)], x0, row(w["ln_mix_a"][0]), r_a, dx, name="d_sgu_in")
    g["w_spatial"] = dws[None]
    g["b_spatial"] = jnp.transpose(db[:, :N_GROUPS])[None]
    g["w_up"] = (dwu0, dwu1)
    g["w_down"] = (dwd0, dwd1)
    g["w_ple_gate"] = (dwg0, dwg1)
    g["w_ple_proj"] = (dwp0, dwp1)
    g["ln_mlp"] = jnp.concatenate([dlnm0, dlnm1], axis=0)
    g["ln_ple"] = jnp.concatenate([dlnp0, dlnp1], axis=0)
    return loss_blk, dx, g


ANY = pl.BlockSpec(memory_space=pl.ANY)


def _place():
    x, y, c = lax.axis_index("x"), lax.axis_index("y"), lax.axis_index("c")
    others = [(1 - x, y), (x, 1 - y), (1 - x, 1 - y)]
    return x, y, c, 2 * x + y, others


def cast_into_slot(w3, layer, slot, *, name, after=None, tm=512):
    _, r, c = w3.shape
    tm = min(tm, r)

    def body(slot_ref, w_ref, *rest):
        rest[-1][...] = w_ref[...].astype(BF16)

    in_specs = [pl.BlockSpec((None, tm, c), lambda i, s: (layer, i, 0))]
    args = [slot, w3]
    if after is not None:
        in_specs.append(ANY)
        args.append(after)
    return _pcall(body, name=name, out_shape=_sds((N_SHARDS, r, c), BF16), grid=(r // tm,), num_prefetch=1,
                  in_specs=in_specs, out_specs=pl.BlockSpec((None, tm, c), lambda i, s: (s[0], i, 0)),
                  semantics=("parallel",))(*args)


def gather_vectors(vecs, *, name):
    n = len(vecs)

    def body(*refs):
        src, out = refs[:n], refs[n:2 * n]
        send, recv, loc = refs[2 * n:]
        x, y, c, s_me, others = _place()

        def copy(l, k, slot):
            ox, oy = others[k]
            return pltpu.make_async_remote_copy(src[l], out[l].at[slot], send.at[l, k], recv.at[l, k],
                                                device_id=(ox, oy, c), device_id_type=MESH)

        for l in range(n):
            for k in range(3):
                copy(l, k, s_me).start()
        for l in range(n):
            own = pltpu.make_async_copy(src[l], out[l].at[s_me], loc)
            own.start()
            own.wait()
        for l in range(n):
            for k in range(3):
                ox, oy = others[k]
                copy(l, k, 2 * ox + oy).wait_recv()
                copy(l, k, s_me).wait_send()

    return _pcall(body, name=name, out_shape=[_sds((N_SHARDS,) + v.shape, F32) for v in vecs], in_specs=[ANY] * n,
                  out_specs=[ANY] * n,
                  scratch_shapes=[pltpu.SemaphoreType.DMA((n, 3)), pltpu.SemaphoreType.DMA((n, 3)),
                                  pltpu.SemaphoreType.DMA(())],
                  side_effects=True)(*vecs)


HBM = pl.BlockSpec(memory_space=pltpu.HBM)
SEM = pl.BlockSpec(memory_space=pltpu.SEMAPHORE)
DATAFLOW = pltpu.SideEffectType.DATAFLOW_SIDE_EFFECTING


def _split_call(body, *, name, out_shape, in_specs, out_specs, aliases, views_of=()):
    def make(wrap, specs):
        body_ = wrap(body)
        return pl.pallas_call(body_, name=name, out_shape=out_shape, in_specs=specs, out_specs=out_specs,
                              input_output_aliases=aliases,
                              compiler_params=pltpu.CompilerParams(has_side_effects=DATAFLOW))

    return lambda *args: _in_order(make, in_specs, args, views_of)


def _token_shape():
    return jax.ShapeDtypeStruct((8, LANES), F32)


def gather_start(mats, after, *, name):
    n = len(mats)
    halves = [pltpu.with_memory_space_constraint(m.reshape(N_SHARDS, 2, m.shape[1] // 2, m.shape[2]), pltpu.HBM)
              for m in mats]

    def body(*refs):
        send, recv = refs[n + 1], refs[n + 2]
        out, token = refs[n + 3:2 * n + 3], refs[2 * n + 3]
        x, y, c, s_me, others = _place()
        for l in range(n):
            for k in range(3):
                ox, oy = others[k]
                pltpu.make_async_remote_copy(out[l].at[s_me, c], out[l].at[s_me, c], send.at[3 * l + k],
                                             recv.at[3 * l + k], device_id=(ox, oy, c), device_id_type=MESH).start()
        token[...] = jnp.zeros_like(token)

    res = _split_call(
        body, name=name,
        out_shape=(pltpu.SemaphoreType.DMA((3 * n,)), pltpu.SemaphoreType.DMA((3 * n,)),
                   *[pltpu.HBM(h.shape, BF16) for h in halves], _token_shape()),
        in_specs=[HBM] * n + [ANY], out_specs=(SEM, SEM, *[HBM] * n, pl.BlockSpec(memory_space=pltpu.VMEM)),
        aliases={l: 2 + l for l in range(n)}, views_of=mats)(*halves, after)
    return res[0], res[1], list(res[2:2 + n]), res[2 + n]


def gather_pass_on(bufs, send_a, recv_a, after, *, name, base=0):
    n = len(bufs)

    def body(*refs):
        send_a, recv_a = refs[n], refs[n + 1]
        out = refs[n + 3:2 * n + 3]
        send_b, recv_b, token = refs[2 * n + 3:]
        x, y, c, s_me, others = _place()
        for l in range(n):
            for k in range(3):
                ox, oy = others[k]
                landed, i = out[l].at[2 * ox + oy, c], 3 * l + k
                pltpu.make_async_remote_copy(landed, landed, send_a.at[3 * base + i], recv_a.at[3 * base + i],
                                             device_id=(x, y, 1 - c), device_id_type=MESH).wait_recv()
                pltpu.make_async_remote_copy(landed, landed, send_b.at[i], recv_b.at[i],
                                             device_id=(x, y, 1 - c), device_id_type=MESH).start()
        for l in range(n):
            for k in range(3):
                mine, i = out[l].at[s_me, c], 3 * (base + l) + k
                pltpu.make_async_remote_copy(mine, mine, send_a.at[i], recv_a.at[i],
                                             device_id=(x, y, 1 - c), device_id_type=MESH).wait_send()
        token[...] = jnp.zeros_like(token)

    res = _split_call(
        body, name=name,
        out_shape=(*[pltpu.HBM(b.shape, BF16) for b in bufs], pltpu.SemaphoreType.DMA((3 * n,)),
                   pltpu.SemaphoreType.DMA((3 * n,)), _token_shape()),
        in_specs=[HBM] * n + [SEM, SEM, ANY],
        out_specs=(*[HBM] * n, SEM, SEM, pl.BlockSpec(memory_space=pltpu.VMEM)),
        aliases={l: l for l in range(n)})(*bufs, send_a, recv_a, after)
    return list(res[:n]), res[n], res[n + 1], res[n + 2]


def gather_finish(bufs, send_b, recv_b, after, shapes, *, name):
    n = len(bufs)

    def body(*refs):
        send_b, recv_b = refs[n], refs[n + 1]
        out = refs[n + 3:]
        x, y, c, _, others = _place()
        for l in range(n):
            for k in range(3):
                ox, oy = others[k]
                theirs, mine, i = out[l].at[2 * ox + oy, 1 - c], out[l].at[2 * ox + oy, c], 3 * l + k
                pltpu.make_async_remote_copy(theirs, theirs, send_b.at[i], recv_b.at[i],
                                             device_id=(x, y, 1 - c), device_id_type=MESH).wait_recv()
                pltpu.make_async_remote_copy(mine, mine, send_b.at[i], recv_b.at[i],
                                             device_id=(x, y, 1 - c), device_id_type=MESH).wait_send()

    res = _split_call(
        body, name=name, out_shape=tuple(pltpu.HBM(b.shape, BF16) for b in bufs),
        in_specs=[HBM] * n + [SEM, SEM, ANY], out_specs=tuple([HBM] * n),
        aliases={l: l for l in range(n)})(*bufs, send_b, recv_b, after)
    return [r.reshape(s) for r, s in zip(res, shapes)]


def exchange_start(srcs, dst_shapes, dst_dtype, plan, count, after, *, name):
    n, m = len(srcs), len(dst_shapes)
    given = list(srcs)
    srcs = [pltpu.with_memory_space_constraint(s, pltpu.HBM) for s in srcs]
    lands = [pltpu.with_memory_space_constraint(lax.empty(s, dst_dtype), pltpu.HBM) for s in dst_shapes]

    def body(*refs):
        send, recv = refs[n + m + 1], refs[n + m + 2]
        src, dst, token = refs[n + m + 3:2 * n + m + 3], refs[2 * n + m + 3:2 * (n + m) + 3], refs[2 * (n + m) + 3]
        for i, (s, d, dev) in enumerate(plan(_place(), src, dst)):
            pltpu.make_async_remote_copy(s, d, send.at[i], recv.at[i], device_id=dev, device_id_type=MESH).start()
        token[...] = jnp.zeros_like(token)

    res = _split_call(
        body, name=name,
        out_shape=(pltpu.SemaphoreType.DMA((count,)), pltpu.SemaphoreType.DMA((count,)),
                   *[pltpu.HBM(s.shape, s.dtype) for s in srcs], *[pltpu.HBM(s, dst_dtype) for s in dst_shapes],
                   _token_shape()),
        in_specs=[HBM] * (n + m) + [ANY],
        out_specs=(SEM, SEM, *[HBM] * (n + m), pl.BlockSpec(memory_space=pltpu.VMEM)),
        aliases={i: 2 + i for i in range(n + m)}, views_of=given)(*srcs, *lands, after)
    return (list(res[2:2 + n]), list(res[2 + n:2 + n + m]), res[0], res[1], plan), res[2 + n + m]


def exchange_finish(state, after, *, name):
    srcs, lands, send, recv, plan = state
    n, m = len(srcs), len(lands)

    def body(*refs):
        send, recv = refs[n + m], refs[n + m + 1]
        src, dst = refs[n + m + 3:2 * n + m + 3], refs[2 * n + m + 3:]
        for i, (s, d, dev) in enumerate(plan(_place(), src, dst)):
            pltpu.make_async_remote_copy(s, d, send.at[i], recv.at[i], device_id=dev, device_id_type=MESH).wait()

    res = _split_call(
        body, name=name,
        out_shape=tuple(pltpu.HBM(a.shape, a.dtype) for a in srcs + lands),
        in_specs=[HBM] * (n + m) + [SEM, SEM, ANY], out_specs=tuple([HBM] * (n + m)),
        aliases={i: i for i in range(n + m)})(*srcs, *lands, send, recv, after)
    return list(res[:n]), list(res[n:])


def pair_plan(place, src, dst):
    x, y, c, _, _ = place
    return [(s.at[:, 1 - c], d, (x, y, 1 - c)) for s, d in zip(src, dst)]


def chip_plan(place, src, dst):
    x, y, c, _, others = place
    return [(s.at[2 * ox + oy], d.at[k], (ox, oy, c)) for s, d in zip(src, dst) for k, (ox, oy) in enumerate(others)]


def pair_exchange(grads, *, name):
    n = len(grads)

    def body(*refs):
        src, got = refs[:n], refs[n:2 * n]
        send, recv = refs[2 * n:]
        x, y, c, _, _ = _place()

        def swap(l):
            return pltpu.make_async_remote_copy(src[l].at[:, 1 - c], got[l], send.at[l], recv.at[l],
                                                device_id=(x, y, 1 - c), device_id_type=MESH)

        for l in range(n):
            swap(l).start()
        for l in range(n):
            swap(l).wait()

    res = _pcall(body, name=name, out_shape=[_sds((N_SHARDS,) + g.shape[2:], F32) for g in grads],
                 in_specs=[ANY] * n, out_specs=[ANY] * n,
                 scratch_shapes=[pltpu.SemaphoreType.DMA((n,)), pltpu.SemaphoreType.DMA((n,))],
                 side_effects=True)(*grads)
    return list(res)


def add_to_wire(mine, theirs, core, *, name, tm=512):
    s, _, r, c = mine.shape
    tm = min(tm, r)

    def body(core_ref, a_ref, b_ref, o_ref):
        o_ref[...] = (a_ref[...] + b_ref[...]).astype(BF16)

    spec = pl.BlockSpec((None, tm, c), lambda i, j, cr: (i, j, 0))
    return _pcall(body, name=name, out_shape=_sds((s, r, c), BF16), grid=(s, r // tm), num_prefetch=1,
                  in_specs=[pl.BlockSpec((None, None, tm, c), lambda i, j, cr: (i, cr[0], j, 0)), spec],
                  out_specs=spec, semantics=("parallel", "parallel"))(core, mine, theirs)


def sum_chips(wire, landed, place, dest, layer, n_layers, *, name, tm=512):
    _, r, c = wire.shape
    tm = min(tm, r)

    def body(place_ref, w_ref, l_ref, *rest):
        o_ref = rest[-1]
        o_ref[...] = ((w_ref[...].astype(F32) + l_ref[0].astype(F32)) + l_ref[1].astype(F32)) + l_ref[2].astype(F32)

    in_specs = [pl.BlockSpec((None, tm, c), lambda i, pr: (pr[0], i, 0)),
                pl.BlockSpec((3, tm, c), lambda i, pr: (0, i, 0))]
    args = [place, wire, landed]
    aliases = None
    if dest is not None:
        in_specs.append(ANY)
        args.append(dest)
        aliases = {3: 0}
    return _pcall(body, name=name, out_shape=_sds((n_layers, 2, r, c), F32), grid=(r // tm,), num_prefetch=1,
                  in_specs=in_specs,
                  out_specs=pl.BlockSpec((None, None, tm, c), lambda i, pr: (layer, pr[1], i, 0)),
                  aliases=aliases, semantics=("parallel",))(*args)


def pair_share(bufs, slots, *, name):
    n = len(bufs)

    def body(*refs):
        out = refs[n:2 * n]
        send, recv = refs[2 * n:]
        x, y, c, _, _ = _place()

        def share(i, half):
            o, l = slots[i]
            return pltpu.make_async_remote_copy(out[o].at[l, half], out[o].at[l, half], send.at[i], recv.at[i],
                                                device_id=(x, y, 1 - c), device_id_type=MESH)

        for i in range(len(slots)):
            share(i, c).start()
        for i in range(len(slots)):
            share(i, 1 - c).wait_recv()
            share(i, c).wait_send()

    res = _pcall(body, name=name, out_shape=[_sds(b.shape, F32) for b in bufs], in_specs=[ANY] * n,
                 out_specs=[ANY] * n,
                 scratch_shapes=[pltpu.SemaphoreType.DMA((len(slots),)), pltpu.SemaphoreType.DMA((len(slots),))],
                 aliases={o: o for o in range(n)}, side_effects=True)(*bufs)
    return list(res)


def all_reduce_small(packed, *, name):
    n_dev, r, c = packed.shape

    def body(in_ref, out_ref, land, send, recv):
        x, y, cc, _, _ = _place()
        me = 4 * x + 2 * y + cc
        peers = [(px, py, pc) for px in range(2) for py in range(2) for pc in range(2)]

        def scatter(d):
            return pltpu.make_async_remote_copy(in_ref.at[d], land.at[me], send.at[0, d], recv.at[0, me],
                                                device_id=peers[d], device_id_type=MESH)

        def gather(d):
            return pltpu.make_async_remote_copy(out_ref.at[me], out_ref.at[me], send.at[1, d], recv.at[1, me],
                                                device_id=peers[d], device_id_type=MESH)

        for d in range(n_dev):
            @pl.when(d != me)
            def _():
                scatter(d).start()
        land[me] = in_ref[me]
        for d in range(n_dev):
            @pl.when(d != me)
            def _():
                pltpu.make_async_remote_copy(in_ref.at[d], land.at[d], send.at[0, d], recv.at[0, d],
                                             device_id=peers[d], device_id_type=MESH).wait_recv()
        total = land[0]
        for d in range(1, n_dev):
            total = total + land[d]
        out_ref[me] = total
        for d in range(n_dev):
            @pl.when(d != me)
            def _():
                gather(d).start()
        for d in range(n_dev):
            @pl.when(d != me)
            def _():
                pltpu.make_async_remote_copy(out_ref.at[d], out_ref.at[d], send.at[1, d], recv.at[1, d],
                                             device_id=peers[d], device_id_type=MESH).wait_recv()
        for d in range(n_dev):
            @pl.when(d != me)
            def _():
                scatter(d).wait_send()
                gather(d).wait_send()

    vm = pl.BlockSpec(memory_space=pltpu.VMEM)
    return _pcall(body, name=name, out_shape=_sds(packed.shape, F32), in_specs=[vm], out_specs=vm,
                  scratch_shapes=[pltpu.VMEM(packed.shape, F32), pltpu.SemaphoreType.DMA((2, n_dev)),
                                  pltpu.SemaphoreType.DMA((2, n_dev))],
                  side_effects=True)(packed)


def adamw(w, g, m, v, *, name, part=None, dest=None, tm=512):
    shape = w.shape
    cols = shape[-1]
    rows = 1
    for s in shape[:-1]:
        rows *= s
    first, count = 0, rows
    if part is not None:
        count = rows // part[1]
        first = part[0] * count
    tm = min(tm, count)
    assert count % tm == 0
    two_d = lambda a: a.reshape(rows, cols)

    def body(w_ref, g_ref, m_ref, v_ref, *rest):
        d_ref, mo_ref, vo_ref = rest[-3:]
        gv = g_ref[...]
        m_new = ADAM_B1 * m_ref[...] + (1.0 - ADAM_B1) * gv
        v_new = ADAM_B2 * v_ref[...] + (1.0 - ADAM_B2) * (gv * gv)
        m_hat = m_new / (1.0 - ADAM_B1 ** ADAM_STEP)
        v_hat = v_new / (1.0 - ADAM_B2 ** ADAM_STEP)
        d_ref[...] = -ADAM_LR * (m_hat / (jnp.sqrt(v_hat) + ADAM_EPS) + ADAM_WD * w_ref[...])
        mo_ref[...] = m_new
        vo_ref[...] = v_new

    spec = pl.BlockSpec((tm, cols), lambda i: (first // tm + i, 0))
    args = [two_d(w), two_d(g), two_d(m), two_d(v)]
    in_specs = [spec] * 4
    aliases = None
    if dest is not None:
        args += [two_d(d) for d in dest]
        in_specs = in_specs + [ANY] * 3
        aliases = {4: 0, 5: 1, 6: 2}
    outs = _pcall(body, name=name, out_shape=[_sds((rows, cols), F32)] * 3, grid=(count // tm,), in_specs=in_specs,
                  out_specs=[spec] * 3, aliases=aliases, semantics=("parallel",))(*args)
    return [o.reshape(shape) for o in outs]


WEIGHTS = ("ln_mix_a", "w_in_a", "g_v_a", "w_spatial", "b_spatial", "w_out_a", "ln_kv", "w_kv", "g_k", "ln_mix_b",
           "w_q", "g_q", "w_out_b", "ln_mlp", "w_up", "w_down", "ln_ple", "w_ple_gate", "w_ple_proj")
MATRICES = (("w_in_a", 1, True), ("w_out_a", 1, False), ("w_kv", 0, True), ("w_q", 1, False), ("w_out_b", 1, False),
            ("w_up", 2, True), ("w_down", 2, False), ("w_ple_gate", 2, False), ("w_ple_proj", 2, True))
GATHER_STAGES = ((("w_in_a", 0), ("w_out_a", 0)), (("w_up", 0),), (("w_down", 0),),
                 (("w_ple_gate", 0), ("w_ple_proj", 0), ("w_kv", 0)), (("w_q", 0), ("w_out_b", 0)),
                 (("w_up", 1), ("w_down", 1), ("w_ple_gate", 1), ("w_ple_proj", 1)))
REPLICATED = ("w_spatial", "b_spatial", "ln_kv", "g_k", "ln_mix_b", "g_q", "ln_mlp", "ln_ple")
SHARDED_VECTORS = ("ln_mix_a", "g_v_a")
SMALL_ROWS = 18


def kernel(x, p, ln_mix_a, w_in_a, g_v_a, w_spatial, b_spatial, w_out_a, ln_kv, w_kv, g_k, ln_mix_b, w_q, g_q, w_out_b, ln_mlp, w_up, w_down, ln_ple, w_ple_gate, w_ple_proj, loss_target, m_ln_mix_a, m_w_in_a, m_g_v_a, m_w_spatial, m_b_spatial, m_w_out_a, m_ln_kv, m_w_kv, m_g_k, m_ln_mix_b, m_w_q, m_g_q, m_w_out_b, m_ln_mlp, m_w_up, m_w_down, m_ln_ple, m_w_ple_gate, m_w_ple_proj, v_ln_mix_a, v_w_in_a, v_g_v_a, v_w_spatial, v_b_spatial, v_w_out_a, v_ln_kv, v_w_kv, v_g_k, v_ln_mix_b, v_w_q, v_g_q, v_w_out_b, v_ln_mlp, v_w_up, v_w_down, v_ln_ple, v_w_ple_gate, v_w_ple_proj):
    given = dict(locals())
    _PREVIOUS.clear()
    weights = {n: given[n] for n in WEIGHTS}
    shard = 2 * lax.axis_index("x") + lax.axis_index("y")
    core = lax.axis_index("c")
    shard_1 = shard.astype(jnp.int32).reshape(1)
    core_1 = core.astype(jnp.int32).reshape(1)
    place = jnp.stack([shard, core]).astype(jnp.int32)

    col_sharded = {name: cols for name, _, cols in MATRICES}
    layer_count = {name: max(layers, 1) for name, layers, _ in MATRICES}

    def cast(key, after):
        name, layer = key
        w3 = weights[name] if weights[name].ndim == 3 else weights[name][None]
        return (name, layer, col_sharded[name],
                cast_into_slot(w3, layer, shard_1, name=f"cast_{name}_{layer}", after=after))

    head = [cast(key, None) for key in GATHER_STAGES[0]]
    send_h, recv_h, flying_h, token_h = gather_start([lf[3] for lf in head], shard_1, name="gather_start_0")
    tail = [cast(key, token_h) for stage in GATHER_STAGES[1:] for key in stage]
    vec_a = gather_vectors([ln_mix_a, g_v_a], name="gather_vectors")
    send_a, recv_a, flying, token = gather_start([lf[3] for lf in tail], vec_a[0], name="gather_start_1")

    w = {"ln_mix_a": vec_a[0].reshape(1, D_MODEL),
         "g_v_a": vec_a[1].reshape(1, D_MODEL)}
    for name in REPLICATED:
        w[name] = weights[name]

    class Late:
        passed = {}

        def pass_on(self, name, layer, after):
            stage = [(name, layer) in s for s in GATHER_STAGES].index(True)
            if stage not in self.passed:
                if stage == 0:
                    base, members, sems, fly = 0, head, (send_h, recv_h), flying_h
                else:
                    base = sum(len(s) for s in GATHER_STAGES[1:stage])
                    members, sems, fly = tail[base:base + len(GATHER_STAGES[stage])], (send_a, recv_a), flying
                self.passed[stage] = (members, gather_pass_on(fly[base:base + len(members)], sems[0], sems[1], after,
                                                              name=f"gather_pass_on_{stage}", base=base))
            return stage

        def weights(self, name, layer, after):
            stage = self.pass_on(name, layer, after)
            members, (bufs, send_b, recv_b, tok) = self.passed[stage]
            got = gather_finish(bufs, send_b, recv_b, tok, [lf[3].shape for lf in members],
                                name=f"gather_finish_{stage}")
            out = {}
            for (leaf_name, leaf_layer, cols, _), arr in zip(members, got):
                out[(leaf_name, leaf_layer)] = arr if cols else arr.reshape(N_SHARDS * arr.shape[1], arr.shape[2])
            return out

        groups = []

        def pair_start(self, grads_done, after):
            self.keys = sorted(grads_done)
            views = [view(k, grads_done[k]) for k in self.keys]
            self.pair, token = exchange_start(views, [(N_SHARDS,) + v.shape[2:] for v in views], F32, pair_plan,
                                              len(views), after, name=f"grad_pair_start_{len(self.groups)}")
            return token

        def chip_start(self, after):
            tag = len(self.groups)
            mine, theirs = exchange_finish(self.pair, after, name=f"grad_pair_finish_{tag}")
            wire = [add_to_wire(a, b, core_1, name=f"grad_pair_sum_{tag}_{i}")
                    for i, (a, b) in enumerate(zip(mine, theirs))]
            chip, token = exchange_start(wire, [(3,) + v.shape[1:] for v in wire], BF16, chip_plan, 3 * len(wire),
                                         theirs[-1], name=f"grad_chip_start_{tag}")
            self.groups.append((self.keys, chip))
            return token

    def view(key, arr):
        rows = arr.shape[-2] if col_sharded[key[0]] else arr.shape[0] // N_SHARDS
        return arr.reshape(N_SHARDS, 2, rows // 2, arr.shape[-1])

    t = x.shape[1]
    late = Late()
    loss_blk, dx, g = local_step(x[0], p.reshape(2, t, PLE_DIM), loss_target[0], w, late)

    sent = {k for keys, _ in late.groups for k in keys}
    keys_last = [(name, layer) for name, layers, _ in MATRICES for layer in range(max(layers, 1))
                 if (name, layer) not in sent]
    views = [view(k, g[k[0]][k[1]] if layer_count[k[0]] == 2 else g[k[0]]) for k in keys_last]

    theirs = pair_exchange(views, name="grad_pair_exchange_last")
    wire_0 = [add_to_wire(a, b, core_1, name=f"grad_pair_sum_last_{i}") for i, (a, b) in enumerate(zip(views, theirs))]
    chip_0, token_0 = exchange_start(wire_0, [(3,) + v.shape[1:] for v in wire_0], BF16, chip_plan, 3 * len(wire_0),
                                     theirs[-1], name="grad_chip_start_last")

    grads, bufs = {}, {}

    def sum_and_share(keys, wire, landed, tag):
        for i, (key, wv, lv) in enumerate(zip(keys, wire, landed)):
            name, layer = key
            bufs[name] = sum_chips(wv, lv, place, bufs.get(name), layer, layer_count[name],
                                   name=f"grad_chip_sum_{tag}_{i}")
        names = sorted({k[0] for k in keys})
        shared = pair_share([bufs[n] for n in names], [(names.index(k[0]), k[1]) for k in keys],
                            name=f"grad_pair_share_{tag}")
        bufs.update(zip(names, shared))

    updates = {}

    def update(n, gn, part=None):
        wn, mn, vn = weights[n], given["m_" + n], given["v_" + n]
        if wn.ndim == 1:
            wn, gn, mn, vn = (a.reshape(1, -1) for a in (wn, gn, mn, vn))
        tag = "" if part is None else f"_{part[0]}"
        updates[n] = adamw(wn, gn.reshape(wn.shape), mn, vn, name=f"adamw_{n}{tag}", part=part, dest=updates.get(n))

    after = token_0
    for tag, (keys, chip) in enumerate(late.groups + [(keys_last, chip_0)]):
        wire, landed = exchange_finish(chip, after, name=f"grad_chip_finish_{tag}")
        sum_and_share(keys, wire, landed, tag)
        for name, layer in keys:
            update(name, bufs[name], (layer, layer_count[name]) if layer_count[name] == 2 else None)
        after = updates[keys[-1][0]][0]

    small = REPLICATED + SHARDED_VECTORS
    flat = jnp.concatenate([g[n].reshape(-1) for n in small] + [loss_blk[0, :1]])
    room = 8 * SMALL_ROWS * D_MODEL
    flat = jnp.concatenate([flat, jnp.zeros((room - flat.shape[0],), F32)])
    reduced = all_reduce_small(flat.reshape(8, SMALL_ROWS, D_MODEL), name="grad_small_all_reduce").reshape(-1)
    loss = reduced[sum(g[n].size for n in small)]
    at = 0
    for n in small:
        size = g[n].size
        piece = reduced[at:at + size]
        at += size
        if n in SHARDED_VECTORS:
            per = D_MODEL // N_SHARDS
            grads[n] = lax.dynamic_slice(piece, (shard * per,), (per,)).reshape(weights[n].shape)
        else:
            grads[n] = piece.reshape(weights[n].shape)
        update(n, grads[n])
    for name, _, _ in MATRICES:
        grads[name] = bufs[name].reshape(weights[name].shape)
    delta = {n: updates[n][0].reshape(weights[n].shape) for n in WEIGHTS}
    new_m = {n: updates[n][1].reshape(weights[n].shape) for n in WEIGHTS}
    new_v = {n: updates[n][2].reshape(weights[n].shape) for n in WEIGHTS}
    return (loss, dx.reshape(x.shape), *[grads[n] for n in WEIGHTS], *[delta[n] for n in WEIGHTS],
            *[new_m[n] for n in WEIGHTS], *[new_v[n] for n in WEIGHTS])
```
